```python
import math
import jax, jax.numpy as jnp
from jax import lax
import numpy as np

D_MODEL = 1024
BATCH = 8
SEQ = 8192
DEPTH = 2

N_MEM = 256
CONV_WIDTH = D_MODEL // 2
CONV_KSIZE = 31
ATT_PATTERNS = ((128, 1), (512, 4), (2048, 16))
N_GROUPS = len(ATT_PATTERNS)
HEADS_PER_GROUP = 4
ATT_HEAD_DIM = D_MODEL // 16
N_ATT_HEADS = N_GROUPS * HEADS_PER_GROUP
ATT_WIDTH = N_ATT_HEADS * ATT_HEAD_DIM
ATT_OUT_WIDTH = HEADS_PER_GROUP * ATT_HEAD_DIM
MEM_HEADS = 4
MEM_HEAD_DIM = D_MODEL // 8
MEM_WIDTH = MEM_HEADS * MEM_HEAD_DIM
N_BRANCHES = 3
IN_WIDTH = 2 * CONV_WIDTH + 3 * ATT_WIDTH + MEM_WIDTH + N_BRANCHES * D_MODEL
FFN_HIDDEN = -(-(8 * D_MODEL) // (3 * 256)) * 256
NUM_BUCKETS = 32
MAX_DISTANCE = 1024
RMS_EPS = 1e-6
LN_EPS = 1e-5
NEG_INF = -1e30

kernel_name = "hybrid_gated_conv_dilated_memory_encoder"


def _rms_norm(x, g):
    xf = x.astype(jnp.float32)
    y = xf * lax.rsqrt(jnp.mean(xf * xf, axis=-1, keepdims=True) + RMS_EPS)
    return (y * g.astype(jnp.float32)).astype(x.dtype)


def _layer_norm(x, g, b):
    xf = x.astype(jnp.float32)
    mu = jnp.mean(xf, axis=-1, keepdims=True)
    xc = xf - mu
    y = xc * lax.rsqrt(jnp.mean(xc * xc, axis=-1, keepdims=True) + LN_EPS)
    return (y * g.astype(jnp.float32) + b.astype(jnp.float32)).astype(x.dtype)


def _t5_bucket(rel):
    nb = NUM_BUCKETS // 2
    max_exact = nb // 2
    ret = jnp.where(rel > 0, nb, 0)
    n = jnp.abs(rel)
    nf = jnp.maximum(n, 1).astype(jnp.float32)
    large = max_exact + (jnp.log(nf / max_exact) / math.log(MAX_DISTANCE / max_exact)
                         * (nb - max_exact)).astype(jnp.int32)
    large = jnp.minimum(large, nb - 1)
    return ret + jnp.where(n < max_exact, n, large)


def _dilated_group(q, k, v, bias_tab, dilation, radius):
    B, S, H, E = q.shape
    blk = radius
    unit = dilation * blk
    sp = -(-S // unit) * unit
    L = sp // dilation
    nb = L // blk

    def to_blocks(t):
        t = jnp.pad(t, ((0, 0), (0, sp - S), (0, 0), (0, 0)))
        t = t.reshape(B, L, dilation, H, E).transpose(0, 2, 1, 3, 4)
        return t.reshape(B, dilation, nb, blk, H, E)

    def band_keys(t):
        tp = jnp.pad(t, ((0, 0), (0, 0), (1, 1), (0, 0), (0, 0), (0, 0)))
        return jnp.concatenate([tp[:, :, :-2], tp[:, :, 1:-1], tp[:, :, 2:]], axis=3)

    qb = to_blocks(q)
    kb = band_keys(to_blocks(k))
    vb = band_keys(to_blocks(v))
    valid = (jnp.arange(sp) < S).reshape(L, dilation).T.reshape(dilation, nb, blk)
    vp = jnp.pad(valid, ((0, 0), (1, 1), (0, 0)))
    kvalid = jnp.concatenate([vp[:, :-2], vp[:, 1:-1], vp[:, 2:]], axis=2)

    s = jnp.einsum('brnqhe,brnkhe->brnhqk', qb, kb).astype(jnp.float32) * (E ** -0.5)
    off = jnp.arange(3 * blk)[None, :] - blk - jnp.arange(blk)[:, None]
    band = jnp.abs(off) <= radius
    bias = bias_tab[_t5_bucket(off * dilation)].transpose(2, 0, 1).astype(jnp.float32)
    mask = band & kvalid[:, :, None, None, :]
    s = jnp.where(mask, s + bias, NEG_INF)
    m = jnp.max(s, axis=-1, keepdims=True)
    e = jnp.exp(s - m)
    den = jnp.sum(e, axis=-1, keepdims=True)
    lse = (m + jnp.log(den))[..., 0]
    o = jnp.einsum('brnhqk,brnkhe->brnqhe', (e / den).astype(v.dtype), vb)

    def from_blocks(t):
        tail = t.shape[4:]
        t = t.reshape(B, dilation, L, *tail).swapaxes(1, 2).reshape(B, sp, *tail)
        return t[:, :S]

    return from_blocks(o), from_blocks(lse.transpose(0, 1, 2, 4, 3))


def _dilated_attention(att_in, rel_bias):
    B, S, _ = att_in.shape
    qkv = att_in.reshape(B, S, 3, N_GROUPS, HEADS_PER_GROUP, ATT_HEAD_DIM)
    outs, lses = [], []
    for g, (window, dilation) in enumerate(ATT_PATTERNS):
        radius = window // (2 * dilation)
        tab = rel_bias[:, g * HEADS_PER_GROUP:(g + 1) * HEADS_PER_GROUP]
        o, l = _dilated_group(qkv[:, :, 0, g], qkv[:, :, 1, g], qkv[:, :, 2, g], tab, dilation, radius)
        outs.append(o)
        lses.append(l)
    w = jax.nn.softmax(jnp.stack(lses, axis=0), axis=0)
    o = jnp.sum(w[..., None].astype(outs[0].dtype) * jnp.stack(outs, axis=0), axis=0)
    return o.reshape(B, S, ATT_OUT_WIDTH)


def _conv_module(u, w_dw, b_dw, ln_g, ln_b, w_o):
    a, gt = jnp.split(u, 2, axis=-1)
    u = a * jax.nn.sigmoid(gt)
    pad = CONV_KSIZE // 2
    y = lax.conv_general_dilated(u, w_dw[:, None, :], window_strides=(1,), padding=((pad, pad),),
                                 dimension_numbers=('NWC', 'WIO', 'NWC'),
                                 feature_group_count=CONV_WIDTH) + b_dw
    y = jax.nn.silu(_layer_norm(y, ln_g, ln_b))
    return y @ w_o


def _memory_attention(q_in, mem, g_mem, w_kv, w_o):
    B, S, _ = q_in.shape
    M = mem.shape[1]
    q = q_in.reshape(B, S, MEM_HEADS, MEM_HEAD_DIM)
    kv = (_rms_norm(mem, g_mem) @ w_kv).reshape(B, M, 2, MEM_HEADS, MEM_HEAD_DIM)
    k, v = kv[:, :, 0], kv[:, :, 1]
    s = jnp.einsum('bshe,bmhe->bhsm', q, k).astype(jnp.float32) * (MEM_HEAD_DIM ** -0.5)
    p = jax.nn.softmax(s, axis=-1).astype(v.dtype)
    o = jnp.einsum('bhsm,bmhe->bshe', p, v).reshape(B, S, MEM_WIDTH)
    return o @ w_o


def _fwd_setup_inputs(seed: int = 0) -> dict:
    key = jax.random.key(seed)
    ks = jax.random.split(key, 24)
    f32 = jnp.float32

    def nrm(k, shape, scale):
        return jax.random.normal(k, shape, f32) * scale

    def gain(k, shape):
        return 1.0 + 0.05 * jax.random.normal(k, shape, f32)

    return {
        "x": jax.random.normal(ks[0], (BATCH, SEQ, D_MODEL), f32),
        "mem": jax.random.normal(ks[1], (BATCH, N_MEM, D_MODEL), f32),
        "rel_bias": nrm(ks[2], (NUM_BUCKETS, N_ATT_HEADS), 0.5),
        "norm_mix_pre": gain(ks[3], (DEPTH, D_MODEL)),
        "w_in": nrm(ks[4], (DEPTH, D_MODEL, IN_WIDTH), D_MODEL ** -0.5),
        "b_gate": nrm(ks[5], (DEPTH, N_BRANCHES * D_MODEL), 0.01),
        "conv_dw": nrm(ks[6], (DEPTH, CONV_KSIZE, CONV_WIDTH), CONV_KSIZE ** -0.5),
        "conv_dw_bias": nrm(ks[7], (DEPTH, CONV_WIDTH), 0.02),
        "conv_ln_g": gain(ks[8], (DEPTH, CONV_WIDTH)),
        "conv_ln_b": nrm(ks[9], (DEPTH, CONV_WIDTH), 0.02),
        "w_conv_out": nrm(ks[10], (DEPTH, CONV_WIDTH, D_MODEL), CONV_WIDTH ** -0.5),
        "w_att_out": nrm(ks[11], (DEPTH, ATT_OUT_WIDTH, D_MODEL), ATT_OUT_WIDTH ** -0.5),
        "norm_mem": gain(ks[12], (DEPTH, D_MODEL)),
        "w_mem_kv": nrm(ks[13], (DEPTH, D_MODEL, 2 * MEM_WIDTH), D_MODEL ** -0.5),
        "w_mem_out": nrm(ks[14], (DEPTH, MEM_WIDTH, D_MODEL), MEM_WIDTH ** -0.5),
        "w_out": nrm(ks[15], (DEPTH, D_MODEL, D_MODEL), D_MODEL ** -0.5),
        "norm_mix_post": gain(ks[16], (DEPTH, D_MODEL)),
        "norm_ffn_pre": gain(ks[17], (DEPTH, D_MODEL)),
        "w_ffn_in": nrm(ks[18], (DEPTH, D_MODEL, 2 * FFN_HIDDEN), D_MODEL ** -0.5),
        "w_ffn_out": nrm(ks[19], (DEPTH, FFN_HIDDEN, D_MODEL), FFN_HIDDEN ** -0.5),
        "norm_ffn_post": gain(ks[20], (DEPTH, D_MODEL)),
    }


def _fwd_reference(x, mem, rel_bias, norm_mix_pre, w_in, b_gate, conv_dw, conv_dw_bias, conv_ln_g,
              conv_ln_b, w_conv_out, w_att_out, norm_mem, w_mem_kv, w_mem_out, w_out,
              norm_mix_post, norm_ffn_pre, w_ffn_in, w_ffn_out, norm_ffn_post):
    B, S, _ = x.shape
    c1 = 2 * CONV_WIDTH
    c2 = c1 + 3 * ATT_WIDTH
    c3 = c2 + MEM_WIDTH
    for l in range(DEPTH):
        h = _rms_norm(x, norm_mix_pre[l])
        z = h @ w_in[l]
        y_conv = _conv_module(z[..., :c1], conv_dw[l], conv_dw_bias[l], conv_ln_g[l],
                              conv_ln_b[l], w_conv_out[l])
        y_att = _dilated_attention(z[..., c1:c2], rel_bias) @ w_att_out[l]
        y_mem = _memory_attention(z[..., c2:c3], mem, norm_mem[l], w_mem_kv[l], w_mem_out[l])
        gates = jax.nn.sigmoid(z[..., c3:] + b_gate[l]).reshape(B, S, N_BRANCHES, D_MODEL)
        merged = gates[:, :, 0] * y_conv + gates[:, :, 1] * y_att + gates[:, :, 2] * y_mem
        x = x + _rms_norm(merged @ w_out[l], norm_mix_post[l])
        h = _rms_norm(x, norm_ffn_pre[l])
        gu = h @ w_ffn_in[l]
        g_ff, u_ff = gu[..., :FFN_HIDDEN], gu[..., FFN_HIDDEN:]
        x = x + _rms_norm((jax.nn.silu(g_ff) * u_ff) @ w_ffn_out[l], norm_ffn_post[l])
    return x


import jax as _jax
import jax.numpy as _jnp

TWIN_FORMAT = 'train_step'
FWD_PARAMS = ['x', 'mem', 'rel_bias', 'norm_mix_pre', 'w_in', 'b_gate', 'conv_dw', 'conv_dw_bias', 'conv_ln_g', 'conv_ln_b', 'w_conv_out', 'w_att_out', 'norm_mem', 'w_mem_kv', 'w_mem_out', 'w_out', 'norm_mix_post', 'norm_ffn_pre', 'w_ffn_in', 'w_ffn_out', 'norm_ffn_post']
TWIN_WEIGHTS = ['rel_bias', 'norm_mix_pre', 'w_in', 'b_gate', 'conv_dw', 'conv_dw_bias', 'conv_ln_g', 'conv_ln_b', 'w_conv_out', 'w_att_out', 'norm_mem', 'w_mem_kv', 'w_mem_out', 'w_out', 'norm_mix_post', 'norm_ffn_pre', 'w_ffn_in', 'w_ffn_out', 'norm_ffn_post']
TWIN_DIFF_INPUT = 'x'
TWIN_INPUTS = ['x', 'mem', 'rel_bias', 'norm_mix_pre', 'w_in', 'b_gate', 'conv_dw', 'conv_dw_bias', 'conv_ln_g', 'conv_ln_b', 'w_conv_out', 'w_att_out', 'norm_mem', 'w_mem_kv', 'w_mem_out', 'w_out', 'norm_mix_post', 'norm_ffn_pre', 'w_ffn_in', 'w_ffn_out', 'norm_ffn_post', 'loss_target', 'm_rel_bias', 'm_norm_mix_pre', 'm_w_in', 'm_b_gate', 'm_conv_dw', 'm_conv_dw_bias', 'm_conv_ln_g', 'm_conv_ln_b', 'm_w_conv_out', 'm_w_att_out', 'm_norm_mem', 'm_w_mem_kv', 'm_w_mem_out', 'm_w_out', 'm_norm_mix_post', 'm_norm_ffn_pre', 'm_w_ffn_in', 'm_w_ffn_out', 'm_norm_ffn_post', 'v_rel_bias', 'v_norm_mix_pre', 'v_w_in', 'v_b_gate', 'v_conv_dw', 'v_conv_dw_bias', 'v_conv_ln_g', 'v_conv_ln_b', 'v_w_conv_out', 'v_w_att_out', 'v_norm_mem', 'v_w_mem_kv', 'v_w_mem_out', 'v_w_out', 'v_norm_mix_post', 'v_norm_ffn_pre', 'v_w_ffn_in', 'v_w_ffn_out', 'v_norm_ffn_post']
TWIN_OUTPUTS = ['loss', 'grad_x', 'grad_rel_bias', 'grad_norm_mix_pre', 'grad_w_in', 'grad_b_gate', 'grad_conv_dw', 'grad_conv_dw_bias', 'grad_conv_ln_g', 'grad_conv_ln_b', 'grad_w_conv_out', 'grad_w_att_out', 'grad_norm_mem', 'grad_w_mem_kv', 'grad_w_mem_out', 'grad_w_out', 'grad_norm_mix_post', 'grad_norm_ffn_pre', 'grad_w_ffn_in', 'grad_w_ffn_out', 'grad_norm_ffn_post', 'delta_rel_bias', 'delta_norm_mix_pre', 'delta_w_in', 'delta_b_gate', 'delta_conv_dw', 'delta_conv_dw_bias', 'delta_conv_ln_g', 'delta_conv_ln_b', 'delta_w_conv_out', 'delta_w_att_out', 'delta_norm_mem', 'delta_w_mem_kv', 'delta_w_mem_out', 'delta_w_out', 'delta_norm_mix_post', 'delta_norm_ffn_pre', 'delta_w_ffn_in', 'delta_w_ffn_out', 'delta_norm_ffn_post', 'new_m_rel_bias', 'new_m_norm_mix_pre', 'new_m_w_in', 'new_m_b_gate', 'new_m_conv_dw', 'new_m_conv_dw_bias', 'new_m_conv_ln_g', 'new_m_conv_ln_b', 'new_m_w_conv_out', 'new_m_w_att_out', 'new_m_norm_mem', 'new_m_w_mem_kv', 'new_m_w_mem_out', 'new_m_w_out', 'new_m_norm_mix_post', 'new_m_norm_ffn_pre', 'new_m_w_ffn_in', 'new_m_w_ffn_out', 'new_m_norm_ffn_post', 'new_v_rel_bias', 'new_v_norm_mix_pre', 'new_v_w_in', 'new_v_b_gate', 'new_v_conv_dw', 'new_v_conv_dw_bias', 'new_v_conv_ln_g', 'new_v_conv_ln_b', 'new_v_w_conv_out', 'new_v_w_att_out', 'new_v_norm_mem', 'new_v_w_mem_kv', 'new_v_w_mem_out', 'new_v_w_out', 'new_v_norm_mix_post', 'new_v_norm_ffn_pre', 'new_v_w_ffn_in', 'new_v_w_ffn_out', 'new_v_norm_ffn_post']
TWIN_LEAF_KINDS = {'loss': 'loss', 'grad_x': 'grad_x', 'grad_rel_bias': 'grad_w', 'grad_norm_mix_pre': 'grad_w', 'grad_w_in': 'grad_w', 'grad_b_gate': 'grad_w', 'grad_conv_dw': 'grad_w', 'grad_conv_dw_bias': 'grad_w', 'grad_conv_ln_g': 'grad_w', 'grad_conv_ln_b': 'grad_w', 'grad_w_conv_out': 'grad_w', 'grad_w_att_out': 'grad_w', 'grad_norm_mem': 'grad_w', 'grad_w_mem_kv': 'grad_w', 'grad_w_mem_out': 'grad_w', 'grad_w_out': 'grad_w', 'grad_norm_mix_post': 'grad_w', 'grad_norm_ffn_pre': 'grad_w', 'grad_w_ffn_in': 'grad_w', 'grad_w_ffn_out': 'grad_w', 'grad_norm_ffn_post': 'grad_w', 'delta_rel_bias': 'delta_w', 'delta_norm_mix_pre': 'delta_w', 'delta_w_in': 'delta_w', 'delta_b_gate': 'delta_w', 'delta_conv_dw': 'delta_w', 'delta_conv_dw_bias': 'delta_w', 'delta_conv_ln_g': 'delta_w', 'delta_conv_ln_b': 'delta_w', 'delta_w_conv_out': 'delta_w', 'delta_w_att_out': 'delta_w', 'delta_norm_mem': 'delta_w', 'delta_w_mem_kv': 'delta_w', 'delta_w_mem_out': 'delta_w', 'delta_w_out': 'delta_w', 'delta_norm_mix_post': 'delta_w', 'delta_norm_ffn_pre': 'delta_w', 'delta_w_ffn_in': 'delta_w', 'delta_w_ffn_out': 'delta_w', 'delta_norm_ffn_post': 'delta_w', 'new_m_rel_bias': 'new_m', 'new_m_norm_mix_pre': 'new_m', 'new_m_w_in': 'new_m', 'new_m_b_gate': 'new_m', 'new_m_conv_dw': 'new_m', 'new_m_conv_dw_bias': 'new_m', 'new_m_conv_ln_g': 'new_m', 'new_m_conv_ln_b': 'new_m', 'new_m_w_conv_out': 'new_m', 'new_m_w_att_out': 'new_m', 'new_m_norm_mem': 'new_m', 'new_m_w_mem_kv': 'new_m', 'new_m_w_mem_out': 'new_m', 'new_m_w_out': 'new_m', 'new_m_norm_mix_post': 'new_m', 'new_m_norm_ffn_pre': 'new_m', 'new_m_w_ffn_in': 'new_m', 'new_m_w_ffn_out': 'new_m', 'new_m_norm_ffn_post': 'new_m', 'new_v_rel_bias': 'new_v', 'new_v_norm_mix_pre': 'new_v', 'new_v_w_in': 'new_v', 'new_v_b_gate': 'new_v', 'new_v_conv_dw': 'new_v', 'new_v_conv_dw_bias': 'new_v', 'new_v_conv_ln_g': 'new_v', 'new_v_conv_ln_b': 'new_v', 'new_v_w_conv_out': 'new_v', 'new_v_w_att_out': 'new_v', 'new_v_norm_mem': 'new_v', 'new_v_w_mem_kv': 'new_v', 'new_v_w_mem_out': 'new_v', 'new_v_w_out': 'new_v', 'new_v_norm_mix_post': 'new_v', 'new_v_norm_ffn_pre': 'new_v', 'new_v_w_ffn_in': 'new_v', 'new_v_w_ffn_out': 'new_v', 'new_v_norm_ffn_post': 'new_v'}


def _forward(args):
    return _fwd_reference(*[args[k] for k in FWD_PARAMS])


def _output_shape():
    def fwd():
        inp = _fwd_setup_inputs(0)
        return _fwd_reference(*[inp[k] for k in FWD_PARAMS])
    out = _jax.eval_shape(fwd)
    return out.shape, out.dtype

N_MICROBATCH = 1
ADAM_LR = 0.001
ADAM_B1 = 0.9
ADAM_B2 = 0.999
ADAM_EPS = 1e-08
ADAM_WD = 0.01
ADAM_STEP = 10
PER_EXAMPLE_BATCH_AXIS = {'x': 0, 'mem': 0, 'loss_target': 0}
SHARED_INPUTS = []
_WEIGHT_DTYPES = {'rel_bias': _jnp.float32, 'norm_mix_pre': _jnp.float32, 'w_in': _jnp.float32, 'b_gate': _jnp.float32, 'conv_dw': _jnp.float32, 'conv_dw_bias': _jnp.float32, 'conv_ln_g': _jnp.float32, 'conv_ln_b': _jnp.float32, 'w_conv_out': _jnp.float32, 'w_att_out': _jnp.float32, 'norm_mem': _jnp.float32, 'w_mem_kv': _jnp.float32, 'w_mem_out': _jnp.float32, 'w_out': _jnp.float32, 'norm_mix_post': _jnp.float32, 'norm_ffn_pre': _jnp.float32, 'w_ffn_in': _jnp.float32, 'w_ffn_out': _jnp.float32, 'norm_ffn_post': _jnp.float32}
MOMENT_SCALE = {'rel_bias': 1.567921e+00, 'norm_mix_pre': 2.591237e+00, 'w_in': 9.732923e-01, 'b_gate': 2.020532e+00, 'conv_dw': 3.309761e+00, 'conv_dw_bias': 5.383821e+01, 'conv_ln_g': 2.029064e+01, 'conv_ln_b': 2.942487e+01, 'w_conv_out': 8.125305e+00, 'w_att_out': 2.127598e+00, 'norm_mem': 8.762870e-01, 'w_mem_kv': 7.611102e-01, 'w_mem_out': 7.970745e-01, 'w_out': 8.109726e+00, 'norm_mix_post': 6.581107e+01, 'norm_ffn_pre': 3.373761e+00, 'w_ffn_in': 1.399581e+00, 'w_ffn_out': 2.960067e+00, 'norm_ffn_post': 6.399174e+01}


def _to_microbatches(a, axis):
    t = _jnp.moveaxis(a, axis, 0)
    t = t.reshape((N_MICROBATCH, t.shape[0] // N_MICROBATCH) + t.shape[1:])
    return _jnp.moveaxis(t, 1, axis + 1)


def setup_inputs(seed: int = 0) -> dict:
    inp = _fwd_setup_inputs(seed)
    key = _jax.random.fold_in(_jax.random.key(seed), 7919)
    shape, _ = _output_shape()
    out = dict(inp)
    out["loss_target"] = _jax.random.normal(_jax.random.fold_in(key, 0), shape, _jnp.float32)
    for i, name in enumerate(TWIN_WEIGHTS):
        w = inp[name].astype(_jnp.float32)
        if MOMENT_SCALE is None:
            s = _jnp.sqrt(_jnp.mean(_jnp.square(w)) + 1e-30)
        else:
            s = MOMENT_SCALE[name]
        km, kv = _jax.random.split(_jax.random.fold_in(key, i + 1))
        out[name] = w
        out["m_" + name] = s * _jax.random.normal(km, w.shape, _jnp.float32)
        out["v_" + name] = (s * s) * _jax.random.uniform(kv, w.shape, _jnp.float32, 0.5, 1.5)
    if N_MICROBATCH > 1:
        for name, axis in PER_EXAMPLE_BATCH_AXIS.items():
            out[name] = _to_microbatches(out[name], axis)
    return {'x': out['x'], 'mem': out['mem'], 'rel_bias': out['rel_bias'], 'norm_mix_pre': out['norm_mix_pre'], 'w_in': out['w_in'], 'b_gate': out['b_gate'], 'conv_dw': out['conv_dw'], 'conv_dw_bias': out['conv_dw_bias'], 'conv_ln_g': out['conv_ln_g'], 'conv_ln_b': out['conv_ln_b'], 'w_conv_out': out['w_conv_out'], 'w_att_out': out['w_att_out'], 'norm_mem': out['norm_mem'], 'w_mem_kv': out['w_mem_kv'], 'w_mem_out': out['w_mem_out'], 'w_out': out['w_out'], 'norm_mix_post': out['norm_mix_post'], 'norm_ffn_pre': out['norm_ffn_pre'], 'w_ffn_in': out['w_ffn_in'], 'w_ffn_out': out['w_ffn_out'], 'norm_ffn_post': out['norm_ffn_post'], 'loss_target': out['loss_target'], 'm_rel_bias': out['m_rel_bias'], 'm_norm_mix_pre': out['m_norm_mix_pre'], 'm_w_in': out['m_w_in'], 'm_b_gate': out['m_b_gate'], 'm_conv_dw': out['m_conv_dw'], 'm_conv_dw_bias': out['m_conv_dw_bias'], 'm_conv_ln_g': out['m_conv_ln_g'], 'm_conv_ln_b': out['m_conv_ln_b'], 'm_w_conv_out': out['m_w_conv_out'], 'm_w_att_out': out['m_w_att_out'], 'm_norm_mem': out['m_norm_mem'], 'm_w_mem_kv': out['m_w_mem_kv'], 'm_w_mem_out': out['m_w_mem_out'], 'm_w_out': out['m_w_out'], 'm_norm_mix_post': out['m_norm_mix_post'], 'm_norm_ffn_pre': out['m_norm_ffn_pre'], 'm_w_ffn_in': out['m_w_ffn_in'], 'm_w_ffn_out': out['m_w_ffn_out'], 'm_norm_ffn_post': out['m_norm_ffn_post'], 'v_rel_bias': out['v_rel_bias'], 'v_norm_mix_pre': out['v_norm_mix_pre'], 'v_w_in': out['v_w_in'], 'v_b_gate': out['v_b_gate'], 'v_conv_dw': out['v_conv_dw'], 'v_conv_dw_bias': out['v_conv_dw_bias'], 'v_conv_ln_g': out['v_conv_ln_g'], 'v_conv_ln_b': out['v_conv_ln_b'], 'v_w_conv_out': out['v_w_conv_out'], 'v_w_att_out': out['v_w_att_out'], 'v_norm_mem': out['v_norm_mem'], 'v_w_mem_kv': out['v_w_mem_kv'], 'v_w_mem_out': out['v_w_mem_out'], 'v_w_out': out['v_w_out'], 'v_norm_mix_post': out['v_norm_mix_post'], 'v_norm_ffn_pre': out['v_norm_ffn_pre'], 'v_w_ffn_in': out['v_w_ffn_in'], 'v_w_ffn_out': out['v_w_ffn_out'], 'v_norm_ffn_post': out['v_norm_ffn_post']}


def _loss(weights, diff, rest, loss_target):
    with _jax.named_scope("forward"):
        args = {**rest, TWIN_DIFF_INPUT: diff, **{k: w.astype(_WEIGHT_DTYPES[k]) for k, w in weights.items()}}
        y = _forward(args)
    with _jax.named_scope("loss_head"):
        err = _jnp.square(y.astype(_jnp.float32) - loss_target)
        return 0.5 * _jnp.sum(_jnp.mean(err, axis=-1)) if err.ndim else 0.5 * err


def _adamw(w, g, m, v):
    m = ADAM_B1 * m + (1.0 - ADAM_B1) * g
    v = ADAM_B2 * v + (1.0 - ADAM_B2) * _jnp.square(g)
    m_hat = m / (1.0 - ADAM_B1 ** ADAM_STEP)
    v_hat = v / (1.0 - ADAM_B2 ** ADAM_STEP)
    delta = -ADAM_LR * (m_hat / (_jnp.sqrt(v_hat) + ADAM_EPS) + ADAM_WD * w)
    return delta, m, v


def reference(x, mem, rel_bias, norm_mix_pre, w_in, b_gate, conv_dw, conv_dw_bias, conv_ln_g, conv_ln_b, w_conv_out, w_att_out, norm_mem, w_mem_kv, w_mem_out, w_out, norm_mix_post, norm_ffn_pre, w_ffn_in, w_ffn_out, norm_ffn_post, loss_target, m_rel_bias, m_norm_mix_pre, m_w_in, m_b_gate, m_conv_dw, m_conv_dw_bias, m_conv_ln_g, m_conv_ln_b, m_w_conv_out, m_w_att_out, m_norm_mem, m_w_mem_kv, m_w_mem_out, m_w_out, m_norm_mix_post, m_norm_ffn_pre, m_w_ffn_in, m_w_ffn_out, m_norm_ffn_post, v_rel_bias, v_norm_mix_pre, v_w_in, v_b_gate, v_conv_dw, v_conv_dw_bias, v_conv_ln_g, v_conv_ln_b, v_w_conv_out, v_w_att_out, v_norm_mem, v_w_mem_kv, v_w_mem_out, v_w_out, v_norm_mix_post, v_norm_ffn_pre, v_w_ffn_in, v_w_ffn_out, v_norm_ffn_post):
    given = dict(x=x, mem=mem, rel_bias=rel_bias, norm_mix_pre=norm_mix_pre, w_in=w_in, b_gate=b_gate, conv_dw=conv_dw, conv_dw_bias=conv_dw_bias, conv_ln_g=conv_ln_g, conv_ln_b=conv_ln_b, w_conv_out=w_conv_out, w_att_out=w_att_out, norm_mem=norm_mem, w_mem_kv=w_mem_kv, w_mem_out=w_mem_out, w_out=w_out, norm_mix_post=norm_mix_post, norm_ffn_pre=norm_ffn_pre, w_ffn_in=w_ffn_in, w_ffn_out=w_ffn_out, norm_ffn_post=norm_ffn_post, loss_target=loss_target, m_rel_bias=m_rel_bias, m_norm_mix_pre=m_norm_mix_pre, m_w_in=m_w_in, m_b_gate=m_b_gate, m_conv_dw=m_conv_dw, m_conv_dw_bias=m_conv_dw_bias, m_conv_ln_g=m_conv_ln_g, m_conv_ln_b=m_conv_ln_b, m_w_conv_out=m_w_conv_out, m_w_att_out=m_w_att_out, m_norm_mem=m_norm_mem, m_w_mem_kv=m_w_mem_kv, m_w_mem_out=m_w_mem_out, m_w_out=m_w_out, m_norm_mix_post=m_norm_mix_post, m_norm_ffn_pre=m_norm_ffn_pre, m_w_ffn_in=m_w_ffn_in, m_w_ffn_out=m_w_ffn_out, m_norm_ffn_post=m_norm_ffn_post, v_rel_bias=v_rel_bias, v_norm_mix_pre=v_norm_mix_pre, v_w_in=v_w_in, v_b_gate=v_b_gate, v_conv_dw=v_conv_dw, v_conv_dw_bias=v_conv_dw_bias, v_conv_ln_g=v_conv_ln_g, v_conv_ln_b=v_conv_ln_b, v_w_conv_out=v_w_conv_out, v_w_att_out=v_w_att_out, v_norm_mem=v_norm_mem, v_w_mem_kv=v_w_mem_kv, v_w_mem_out=v_w_mem_out, v_w_out=v_w_out, v_norm_mix_post=v_norm_mix_post, v_norm_ffn_pre=v_norm_ffn_pre, v_w_ffn_in=v_w_ffn_in, v_w_ffn_out=v_w_ffn_out, v_norm_ffn_post=v_norm_ffn_post)
    weights = {n: given[n] for n in TWIN_WEIGHTS}
    shared = {n: given[n] for n in SHARED_INPUTS}
    per_example = {n: given[n] for n in ['x', 'mem']}
    grad_fn = _jax.value_and_grad(_loss, argnums=(0, 1))

    def one_microbatch(ex, loss_target):
        ex = dict(ex)
        diff = ex.pop(TWIN_DIFF_INPUT)
        return grad_fn(weights, diff, {**shared, **ex}, loss_target)

    if N_MICROBATCH == 1:
        loss, (grad_w, grad_x) = one_microbatch(per_example, given["loss_target"])
    else:
        def body(carry, xs):
            loss_sum, grad_sum = carry
            l_k, (gw_k, gx_k) = one_microbatch(xs[0], xs[1])
            with _jax.named_scope("update"):
                return (loss_sum + l_k, _jax.tree.map(_jnp.add, grad_sum, gw_k)), gx_k

        init = (_jnp.zeros((), _jnp.float32), _jax.tree.map(_jnp.zeros_like, weights))
        (loss, grad_w), grad_x = _jax.lax.scan(body, init, (per_example, given["loss_target"]))
    with _jax.named_scope("update"):
        delta_w, new_m, new_v = {}, {}, {}
        for n in TWIN_WEIGHTS:
            delta_w[n], new_m[n], new_v[n] = _adamw(weights[n], grad_w[n], given["m_" + n], given["v_" + n])
    return (loss, grad_x, *[grad_w[n] for n in TWIN_WEIGHTS], *[delta_w[n] for n in TWIN_WEIGHTS],
            *[new_m[n] for n in TWIN_WEIGHTS], *[new_v[n] for n in TWIN_WEIGHTS])
```

```python
import functools
import math

import jax
import jax.numpy as jnp
from jax import lax
from jax.experimental import pallas as pl
from jax.experimental.pallas import tpu as pltpu

F32 = jnp.float32
BF16 = jnp.bfloat16
I32 = jnp.int32

D = 1024
DEPTH = 2
N_MEM = 256
CW = 512
KSIZE = 31
PAD = KSIZE // 2
DILS = (1, 4, 16)
RADIUS = 64
HPG = 4
HD = 64
GW = HPG * HD
MH = 4
MHD = 128
MW = MH * MHD
FH = 2816
NIN = 6912
C1 = 2 * CW
C2 = C1 + 9 * GW
C3 = C2 + MW
NUM_BUCKETS = 32
MAX_DISTANCE = 1024
RMS_EPS = 1e-6
LN_EPS = 1e-5
NEG_INF = -1e30
ATT_SCALE = HD ** -0.5
MEM_SCALE = MHD ** -0.5

ADAM_LR = 0.001
ADAM_B1 = 0.9
ADAM_B2 = 0.999
ADAM_EPS = 1e-08
ADAM_WD = 0.01
ADAM_STEP = 10

VMEM_LIMIT_BYTES = 56 * 1024 * 1024
ATT_QB = 128
ATT_TB = 16 * ATT_QB

MESH = pl.DeviceIdType.MESH


def _params(sem=None):
    return pltpu.CompilerParams(dimension_semantics=sem, vmem_limit_bytes=VMEM_LIMIT_BYTES)


def _sigmoid(v):
    return 1.0 / (1.0 + jnp.exp(-v))


def _dot(a, b):
    return jnp.dot(a, b, preferred_element_type=F32)


def _dot_nt(a, b):
    return lax.dot_general(a, b, (((1,), (1,)), ((), ())), preferred_element_type=F32)


def _dot_tn(a, b):
    return lax.dot_general(a, b, (((0,), (0,)), ((), ())), preferred_element_type=F32)


def _rms_fwd_val(v, g):
    r = lax.rsqrt(jnp.mean(v * v, axis=-1, keepdims=True) + RMS_EPS)
    return v * r * g


def _rms_bwd_val(v, g, dy):
    r = lax.rsqrt(jnp.mean(v * v, axis=-1, keepdims=True) + RMS_EPS)
    vh = v * r
    dvh = dy * g
    dv = r * (dvh - vh * jnp.mean(dvh * vh, axis=-1, keepdims=True))
    return dv, dy * vh


def _row(i):
    return (i, 0)


def _fixed(*_):
    return (0, 0)


def _mm_nn(a, b, tm, tn, out_dtype, name):
    M, K = a.shape
    N = b.shape[1]

    def body(a_ref, b_ref, o_ref):
        o_ref[...] = _dot(a_ref[...], b_ref[...]).astype(out_dtype)

    return pl.pallas_call(
        body, name=name, grid=(N // tn, M // tm),
        in_specs=[pl.BlockSpec((tm, K), lambda j, i: (i, 0)), pl.BlockSpec((K, tn), lambda j, i: (0, j))],
        out_specs=pl.BlockSpec((tm, tn), lambda j, i: (i, j)),
        out_shape=jax.ShapeDtypeStruct((M, N), out_dtype),
        compiler_params=_params(("parallel", "parallel")),
    )(a, b)


def _mm_nt(a, b, tm, tc, out_dtype, name):
    M, N = a.shape
    K = b.shape[0]
    nk = N // tc

    def body(a_ref, b_ref, o_ref, acc_ref):
        k = pl.program_id(1)

        @pl.when(k == 0)
        def _():
            acc_ref[...] = jnp.zeros_like(acc_ref)

        acc_ref[...] += _dot_nt(a_ref[...], b_ref[...])

        @pl.when(k == nk - 1)
        def _():
            o_ref[...] = acc_ref[...].astype(out_dtype)

    return pl.pallas_call(
        body, name=name, grid=(M // tm, nk),
        in_specs=[pl.BlockSpec((tm, tc), lambda i, k: (i, k)), pl.BlockSpec((K, tc), lambda i, k: (0, k))],
        out_specs=pl.BlockSpec((tm, K), lambda i, k: (i, 0)),
        out_shape=jax.ShapeDtypeStruct((M, K), out_dtype),
        scratch_shapes=[pltpu.VMEM((tm, K), F32)],
        compiler_params=_params(("parallel", "arbitrary")),
    )(a, b)


def _mm_tn(a, b, ts, tn, name):
    S, K = a.shape
    N = b.shape[1]

    def body(a_ref, b_ref, o_ref):
        @pl.when(pl.program_id(1) == 0)
        def _():
            o_ref[...] = jnp.zeros_like(o_ref)

        o_ref[...] += _dot_tn(a_ref[...], b_ref[...])

    return pl.pallas_call(
        body, name=name, grid=(N // tn, S // ts),
        in_specs=[pl.BlockSpec((ts, K), lambda j, s: (s, 0)), pl.BlockSpec((ts, tn), lambda j, s: (s, j))],
        out_specs=pl.BlockSpec((K, tn), lambda j, s: (0, j)),
        out_shape=jax.ShapeDtypeStruct((K, N), F32),
        compiler_params=_params(("parallel", "arbitrary")),
    )(a, b)


def _rms_h(x, g, name):
    S = x.shape[0]
    T = 512

    def body(x_ref, g_ref, h_ref):
        h_ref[...] = _rms_fwd_val(x_ref[...], g_ref[...]).astype(BF16)

    return pl.pallas_call(
        body, name=name, grid=(S // T,),
        in_specs=[pl.BlockSpec((T, D), _row), pl.BlockSpec((1, D), _fixed)],
        out_specs=pl.BlockSpec((T, D), _row),
        out_shape=jax.ShapeDtypeStruct((S, D), BF16),
        compiler_params=_params(("parallel",)),
    )(x, g)


def _rms_bwd(x, g, dh, dres, name):
    S = x.shape[0]
    T = 512

    def body(x_ref, g_ref, dh_ref, dres_ref, dx_ref, dg_ref):
        dv, dgr = _rms_bwd_val(x_ref[...], g_ref[...], dh_ref[...].astype(F32))
        dx_ref[...] = dres_ref[...] + dv

        @pl.when(pl.program_id(0) == 0)
        def _():
            dg_ref[...] = jnp.zeros_like(dg_ref)

        dg_ref[...] += jnp.sum(dgr, axis=0, keepdims=True)

    return pl.pallas_call(
        body, name=name, grid=(S // T,),
        in_specs=[pl.BlockSpec((T, D), _row), pl.BlockSpec((1, D), _fixed), pl.BlockSpec((T, D), _row),
                  pl.BlockSpec((T, D), _row)],
        out_specs=[pl.BlockSpec((T, D), _row), pl.BlockSpec((1, D), _fixed)],
        out_shape=[jax.ShapeDtypeStruct((S, D), F32), jax.ShapeDtypeStruct((1, D), F32)],
        compiler_params=_params(("arbitrary",)),
    )(x, g, dh, dres)


CONV_T = 256
CONV_HALO = 16
CONV_RC = 64


def _halo_specs(T, halo, S, width, col):
    per = T // halo
    last = S // halo - 1
    return [
        pl.BlockSpec((T, width), lambda i: (i, col)),
        pl.BlockSpec((halo, width), lambda i: (jnp.maximum(i * per - 1, 0), col)),
        pl.BlockSpec((halo, width), lambda i: (jnp.minimum((i + 1) * per, last), col)),
    ]


def _glu(zb):
    zb = zb.astype(F32)
    return zb[:, :CW] * _sigmoid(zb[:, CW:])


def _fill_ext(ext_ref, cur, prev, nxt, T, halo):
    i = pl.program_id(0)
    n = pl.num_programs(0)
    ext_ref[0:halo, :] = jnp.where(i > 0, prev, 0.0)
    ext_ref[halo:halo + T, :] = cur
    ext_ref[halo + T:2 * halo + T, :] = jnp.where(i < n - 1, nxt, 0.0)


def _conv_fwd(z, wdw, bdw, lng, lnb, name):
    S = z.shape[0]
    T, HL, RC = CONV_T, CONV_HALO, CONV_RC

    def body(cur_ref, prev_ref, next_ref, w_ref, b_ref, g_ref, bb_ref, yc_ref, act_ref, ext_ref):
        _fill_ext(ext_ref, _glu(cur_ref[...]), _glu(prev_ref[...]), _glu(next_ref[...]), T, HL)
        for c in range(T // RC):
            acc = jnp.zeros((RC, CW), F32)
            for k in range(KSIZE):
                o = c * RC + k + HL - PAD
                acc = acc + w_ref[k:k + 1, :] * ext_ref[o:o + RC, :]
            yc = acc + b_ref[...]
            yc_ref[c * RC:(c + 1) * RC, :] = yc
            mu = jnp.mean(yc, axis=-1, keepdims=True)
            xc = yc - mu
            ln = xc * lax.rsqrt(jnp.mean(xc * xc, axis=-1, keepdims=True) + LN_EPS) * g_ref[...] + bb_ref[...]
            act_ref[c * RC:(c + 1) * RC, :] = (ln * _sigmoid(ln)).astype(BF16)

    return pl.pallas_call(
        body, name=name, grid=(S // T,),
        in_specs=_halo_specs(T, HL, S, C1, 0) + [pl.BlockSpec((32, CW), _fixed)] + [pl.BlockSpec((1, CW), _fixed)] * 3,
        out_specs=[pl.BlockSpec((T, CW), _row), pl.BlockSpec((T, CW), _row)],
        out_shape=[jax.ShapeDtypeStruct((S, CW), F32), jax.ShapeDtypeStruct((S, CW), BF16)],
        scratch_shapes=[pltpu.VMEM((T + 2 * HL, CW), F32)],
        compiler_params=_params(("parallel",)),
    )(z, z, z, wdw, bdw, lng, lnb)


def _conv_bwd_ln(yc, dact, lng, lnb, name):
    S = yc.shape[0]
    T = 512

    def body(yc_ref, da_ref, g_ref, b_ref, dyc_ref, dg_ref, db_ref, dbias_ref):
        yc_v = yc_ref[...]
        mu = jnp.mean(yc_v, axis=-1, keepdims=True)
        xc = yc_v - mu
        r = lax.rsqrt(jnp.mean(xc * xc, axis=-1, keepdims=True) + LN_EPS)
        yn = xc * r
        ln = yn * g_ref[...] + b_ref[...]
        sg = _sigmoid(ln)
        dln = da_ref[...].astype(F32) * (sg * (1.0 + ln * (1.0 - sg)))
        dyn = dln * g_ref[...]
        dyc = r * (dyn - jnp.mean(dyn, axis=-1, keepdims=True) - yn * jnp.mean(dyn * yn, axis=-1, keepdims=True))
        dyc_ref[...] = dyc

        @pl.when(pl.program_id(0) == 0)
        def _():
            dg_ref[...] = jnp.zeros_like(dg_ref)
            db_ref[...] = jnp.zeros_like(db_ref)
            dbias_ref[...] = jnp.zeros_like(dbias_ref)

        dg_ref[...] += jnp.sum(dln * yn, axis=0, keepdims=True)
        db_ref[...] += jnp.sum(dln, axis=0, keepdims=True)
        dbias_ref[...] += jnp.sum(dyc, axis=0, keepdims=True)

    vec = pl.BlockSpec((1, CW), _fixed)
    return pl.pallas_call(
        body, name=name, grid=(S // T,),
        in_specs=[pl.BlockSpec((T, CW), _row), pl.BlockSpec((T, CW), _row), vec, vec],
        out_specs=[pl.BlockSpec((T, CW), _row), vec, vec, vec],
        out_shape=[jax.ShapeDtypeStruct((S, CW), F32)] + [jax.ShapeDtypeStruct((1, CW), F32)] * 3,
        compiler_params=_params(("arbitrary",)),
    )(yc, dact, lng, lnb)


def _conv_bwd_dw(z, dyc, wdw, name):
    S = z.shape[0]
    T, HL, RC = CONV_T, CONV_HALO, CONV_RC

    def body(zc_ref, zp_ref, zn_ref, dc_ref, dp_ref, dn_ref, w_ref, dz_ref, dw_ref, uext_ref, dext_ref):
        _fill_ext(uext_ref, _glu(zc_ref[...]), _glu(zp_ref[...]), _glu(zn_ref[...]), T, HL)
        _fill_ext(dext_ref, dc_ref[...], dp_ref[...], dn_ref[...], T, HL)

        @pl.when(pl.program_id(0) == 0)
        def _():
            dw_ref[...] = jnp.zeros_like(dw_ref)

        for c in range(T // RC):
            dcur = dext_ref[HL + c * RC:HL + (c + 1) * RC, :]
            du = jnp.zeros((RC, CW), F32)
            for k in range(KSIZE):
                o = c * RC + HL + PAD - k
                du = du + w_ref[k:k + 1, :] * dext_ref[o:o + RC, :]
                o2 = c * RC + k + HL - PAD
                dw_ref[k:k + 1, :] += jnp.sum(dcur * uext_ref[o2:o2 + RC, :], axis=0, keepdims=True)
            zc = zc_ref[c * RC:(c + 1) * RC, :].astype(F32)
            a, gt = zc[:, :CW], zc[:, CW:]
            sg = _sigmoid(gt)
            dz_ref[c * RC:(c + 1) * RC, 0:CW] = (du * sg).astype(BF16)
            dz_ref[c * RC:(c + 1) * RC, CW:C1] = (du * a * sg * (1.0 - sg)).astype(BF16)

    return pl.pallas_call(
        body, name=name, grid=(S // T,),
        in_specs=_halo_specs(T, HL, S, C1, 0) + _halo_specs(T, HL, S, CW, 0) + [pl.BlockSpec((32, CW), _fixed)],
        out_specs=[pl.BlockSpec((T, C1), _row), pl.BlockSpec((32, CW), _fixed)],
        out_shape=[jax.ShapeDtypeStruct((S, C1), BF16), jax.ShapeDtypeStruct((32, CW), F32)],
        scratch_shapes=[pltpu.VMEM((T + 2 * HL, CW), F32), pltpu.VMEM((T + 2 * HL, CW), F32)],
        compiler_params=_params(("arbitrary",)),
    )(z, z, z, dyc, dyc, dyc, wdw)


def _t5_bucket(rel):
    nb = NUM_BUCKETS // 2
    max_exact = nb // 2
    ret = jnp.where(rel > 0, nb, 0)
    n = jnp.abs(rel)
    nf = jnp.maximum(n, 1).astype(F32)
    large = max_exact + (jnp.log(nf / max_exact) / math.log(MAX_DISTANCE / max_exact)
                         * (nb - max_exact)).astype(I32)
    large = jnp.minimum(large, nb - 1)
    return ret + jnp.where(n < max_exact, n, large)


def _offsets_qk(nq, nk, shift):
    return lax.broadcasted_iota(I32, (nq, nk), 1) + shift - lax.broadcasted_iota(I32, (nq, nk), 0)


def _bias_table(bk, rb_ref, col, off):
    acc = jnp.zeros(bk.shape, F32)
    for b in range(NUM_BUCKETS):
        acc = jnp.where(bk == b, rb_ref[b, col], acc)
    return jnp.where(jnp.abs(off) <= RADIUS, acc, NEG_INF)


def _to_halves(scr, row0, val):
    rows = val.shape[0]
    v = val.astype(F32)
    scr[0, row0:row0 + rows, :] = v[:, :128]
    scr[1, row0:row0 + rows, :] = v[:, 128:]


def _heads(halves):
    return [halves[h // 2][:, (h % 2) * HD:(h % 2 + 1) * HD] for h in range(HPG)]


def _join_heads(parts):
    return [jnp.concatenate(parts[0:2], axis=-1), jnp.concatenate(parts[2:4], axis=-1)]


def _att_units(d, fn):
    nj = ATT_TB // (ATT_QB * d)
    for r in range(d):
        if nj == 1:
            fn(r, 0)
        else:
            def step(j, c, r=r):
                fn(r, j)
                return c
            lax.fori_loop(0, nj, step, 0)


def _unit_row(r, j, d):
    if isinstance(j, int):
        return j * ATT_QB * d + r
    return pl.multiple_of(j * (ATT_QB * d), ATT_QB) + r


def _att_fwd(z, rel_bias, g, name):
    S = z.shape[0]
    d = DILS[g]
    TB, QB = ATT_TB, ATT_QB
    H = RADIUS * d
    L = S // d
    cq, ck, cv = (C1 + g * GW) // GW, (C1 + 3 * GW + g * GW) // GW, (C1 + 6 * GW + g * GW) // GW
    bk = _t5_bucket(_offsets_qk(QB, 2 * QB, -RADIUS) * d)

    def body(rb_ref, bk_ref, q_ref, kc_ref, kp_ref, kn_ref, vc_ref, vp_ref, vn_ref, o_ref, l_ref,
             qs, ks, vs, os_, ls, bias):
        i = pl.program_id(0)

        @pl.when(i == 0)
        def _():
            off = _offsets_qk(QB, 2 * QB, -RADIUS)
            for h in range(HPG):
                bias[h] = _bias_table(bk_ref[...], rb_ref, g * HPG + h, off)

        _to_halves(qs, 0, q_ref[...])
        for scr, p_ref, c_ref, n_ref in ((ks, kp_ref, kc_ref, kn_ref), (vs, vp_ref, vc_ref, vn_ref)):
            _to_halves(scr, 0, p_ref[...])
            _to_halves(scr, H, c_ref[...])
            _to_halves(scr, H + TB, n_ref[...])

        def unit(r, j):
            row = _unit_row(r, j, d)
            q4 = _heads([qs[hf, pl.ds(row, QB, stride=d), :] for hf in (0, 1)])
            k4 = _heads([ks[hf, pl.ds(row, 2 * QB, stride=d), :] for hf in (0, 1)])
            v4 = _heads([vs[hf, pl.ds(row, 2 * QB, stride=d), :] for hf in (0, 1)])
            km = lax.broadcasted_iota(I32, (QB, 2 * QB), 1) + (i * (TB // d) + j * QB - RADIUS)
            valid = jnp.where(km >= 0, km, L) < L
            o4, l4 = [], []
            for h in range(HPG):
                s = _dot_nt(q4[h].astype(BF16), k4[h].astype(BF16)) * ATT_SCALE + bias[h]
                s = jnp.where(valid, s, NEG_INF)
                m = jnp.max(s, axis=-1, keepdims=True)
                e = jnp.exp(s - m)
                den = jnp.sum(e, axis=-1, keepdims=True)
                o4.append(_dot((e / den).astype(BF16), v4[h].astype(BF16)))
                l4.append(jnp.broadcast_to(m + jnp.log(den), (QB, HD)))
            for hf, (ov, lv) in enumerate(zip(_join_heads(o4), _join_heads(l4))):
                os_[hf, pl.ds(row, QB, stride=d), :] = ov
                ls[hf, pl.ds(row, QB, stride=d), :] = lv

        _att_units(d, unit)
        for hf in (0, 1):
            o_ref[:, hf * 128:(hf + 1) * 128] = os_[hf].astype(BF16)
            l_ref[:, hf * 128:(hf + 1) * 128] = ls[hf]

    def halo3(col):
        c, p, n = _halo_specs(TB, H, S, GW, col)
        return [c, p, n]

    return pl.pallas_call(
        body, name=name, grid=(S // TB,),
        in_specs=[pl.BlockSpec(memory_space=pltpu.SMEM), pl.BlockSpec((QB, 2 * QB), _fixed),
                  pl.BlockSpec((TB, GW), lambda i: (i, cq))] + halo3(ck) + halo3(cv),
        out_specs=[pl.BlockSpec((TB, GW), _row), pl.BlockSpec((TB, GW), _row)],
        out_shape=[jax.ShapeDtypeStruct((S, GW), BF16), jax.ShapeDtypeStruct((S, GW), F32)],
        scratch_shapes=[pltpu.VMEM((2, TB, 128), F32), pltpu.VMEM((2, TB + 2 * H, 128), F32),
                        pltpu.VMEM((2, TB + 2 * H, 128), F32), pltpu.VMEM((2, TB, 128), F32),
                        pltpu.VMEM((2, TB, 128), F32), pltpu.VMEM((HPG, QB, 2 * QB), F32)],
        compiler_params=_params(("arbitrary",)),
    )(rel_bias, bk, z, z, z, z, z, z, z)


def _att_combine(os3, ls3, name):
    S = os3[0].shape[0]
    T = 1024

    def body(o1, o2, o3, l1, l2, l3, o_ref, l_ref):
        lv = [l1[...], l2[...], l3[...]]
        m = jnp.maximum(jnp.maximum(lv[0], lv[1]), lv[2])
        e = [jnp.exp(v - m) for v in lv]
        den = e[0] + e[1] + e[2]
        acc = jnp.zeros_like(m)
        for ev, o in zip(e, (o1, o2, o3)):
            acc = acc + (ev / den) * o[...].astype(F32)
        o_ref[...] = acc.astype(BF16)
        l_ref[...] = m + jnp.log(den)

    blk = pl.BlockSpec((T, GW), _row)
    return pl.pallas_call(
        body, name=name, grid=(S // T,), in_specs=[blk] * 6, out_specs=[blk, blk],
        out_shape=[jax.ShapeDtypeStruct((S, GW), BF16), jax.ShapeDtypeStruct((S, GW), F32)],
        compiler_params=_params(("parallel",)),
    )(*os3, *ls3)


def _att_prep(do, o, lse, name):
    S = do.shape[0]
    T = 1024

    def body(do_ref, o_ref, l_ref, out_ref):
        prod = do_ref[...].astype(F32) * o_ref[...].astype(F32)
        dd = [jnp.broadcast_to(jnp.sum(prod[:, h * HD:(h + 1) * HD], axis=-1, keepdims=True), (T, HD))
              for h in range(HPG)]
        lane = lax.broadcasted_iota(I32, (T, GW), 1)
        out_ref[...] = jnp.where(lane % HD < HD // 2, l_ref[...], jnp.concatenate(dd, axis=-1))

    blk = pl.BlockSpec((T, GW), _row)
    return pl.pallas_call(
        body, name=name, grid=(S // T,), in_specs=[blk] * 3, out_specs=blk,
        out_shape=jax.ShapeDtypeStruct((S, GW), F32), compiler_params=_params(("parallel",)),
    )(do, o, lse)


def _att_bwd(z, rel_bias, do, ld, g, name):
    S = z.shape[0]
    d = DILS[g]
    TB, QB = ATT_TB, ATT_QB
    H = RADIUS * d
    L = S // d
    E = TB + 2 * H
    cq, ck, cv = (C1 + g * GW) // GW, (C1 + 3 * GW + g * GW) // GW, (C1 + 6 * GW + g * GW) // GW
    bk_a = _t5_bucket(_offsets_qk(QB, 2 * QB, -RADIUS) * d)
    bk_b = _t5_bucket(_offsets_qk(2 * QB, QB, RADIUS) * d)

    def body(rb_ref, bka_ref, bkb_ref, *refs):
        ins, (dq_ref, dk_ref, dv_ref, db_ref) = refs[:15], refs[15:19]
        qs, ks, vs, dos, ls, dqs, dks, dvs, bias_a, bias_b, dbias = refs[19:]
        i = pl.program_id(0)
        n = pl.num_programs(0)

        @pl.when(i == 0)
        def _():
            off_a = _offsets_qk(QB, 2 * QB, -RADIUS)
            off_b = _offsets_qk(2 * QB, QB, RADIUS)
            for h in range(HPG):
                bias_a[h] = _bias_table(bka_ref[...], rb_ref, g * HPG + h, off_a)
                bias_b[h] = _bias_table(bkb_ref[...], rb_ref, g * HPG + h, off_b)
            dbias[...] = jnp.zeros_like(dbias)

        for a, scr in enumerate((qs, ks, vs, dos, ls)):
            c_ref, p_ref, n_ref = ins[3 * a:3 * a + 3]
            _to_halves(scr, 0, p_ref[...])
            _to_halves(scr, H, c_ref[...])
            _to_halves(scr, H + TB, n_ref[...])

        def unit(r, j):
            row = _unit_row(r, j, d)
            cur = row + H
            ld = lambda scr, at, nrow: _heads([scr[hf, pl.ds(at, nrow, stride=d), :] for hf in (0, 1)])
            q_c, q_e = ld(qs, cur, QB), ld(qs, row, 2 * QB)
            k_c, k_e = ld(ks, cur, QB), ld(ks, row, 2 * QB)
            v_c, v_e = ld(vs, cur, QB), ld(vs, row, 2 * QB)
            do_c, do_e = ld(dos, cur, QB), ld(dos, row, 2 * QB)
            l_c, l_e = ld(ls, cur, QB), ld(ls, row, 2 * QB)
            m0 = i * (TB // d) + j * QB - RADIUS
            km = lax.broadcasted_iota(I32, (QB, 2 * QB), 1) + m0
            valid_a = jnp.where(km >= 0, km, L) < L
            qm = lax.broadcasted_iota(I32, (2 * QB, QB), 0) + m0
            valid_b = jnp.where(qm >= 0, qm, L) < L
            dq4, dk4, dv4 = [], [], []
            for h in range(HPG):
                qc, qe = q_c[h].astype(BF16), q_e[h].astype(BF16)
                kc, ke = k_c[h].astype(BF16), k_e[h].astype(BF16)
                vc, ve = v_c[h].astype(BF16), v_e[h].astype(BF16)
                dc, de = do_c[h].astype(BF16), do_e[h].astype(BF16)
                dd_c = l_c[h][:, HD // 2:HD // 2 + 1]
                dd_e = l_e[h][:, HD // 2:HD // 2 + 1]
                s = _dot_nt(qc, ke) * ATT_SCALE + bias_a[h]
                p = jnp.where(valid_a, jnp.exp(s - l_c[h][:, 0:1]), 0.0)
                ds = p * (_dot_nt(dc, ve) - dd_c)
                dbias[h] += ds
                dq4.append(_dot(ds.astype(BF16), ke) * ATT_SCALE)
                s2 = _dot_nt(qe, kc) * ATT_SCALE + bias_b[h]
                p2 = jnp.where(valid_b, jnp.exp(s2 - l_e[h][:, 0:1]), 0.0)
                dv4.append(_dot_tn(p2.astype(BF16), de))
                ds2 = p2 * (_dot_nt(de, vc) - dd_e)
                dk4.append(_dot_tn(ds2.astype(BF16), qe) * ATT_SCALE)
            for scr, parts in ((dqs, dq4), (dks, dk4), (dvs, dv4)):
                for hf, val in enumerate(_join_heads(parts)):
                    scr[hf, pl.ds(row, QB, stride=d), :] = val

        _att_units(d, unit)
        for scr, ref in ((dqs, dq_ref), (dks, dk_ref), (dvs, dv_ref)):
            for hf in (0, 1):
                ref[:, hf * 128:(hf + 1) * 128] = scr[hf].astype(BF16)

        @pl.when(i == n - 1)
        def _():
            rows = lax.broadcasted_iota(I32, (NUM_BUCKETS, 128), 0)
            lanes = lax.broadcasted_iota(I32, (NUM_BUCKETS, 128), 1)
            out = jnp.zeros((NUM_BUCKETS, 128), F32)
            bk = bka_ref[...]
            for h in range(HPG):
                acc = dbias[h]
                for b in range(NUM_BUCKETS):
                    tot = jnp.sum(jnp.sum(jnp.where(bk == b, acc, 0.0), axis=1, keepdims=True), axis=0, keepdims=True)
                    out = out + jnp.where((rows == b) & (lanes == h), tot, 0.0)
            db_ref[...] = out

    def halo3(col, width=GW):
        return _halo_specs(TB, H, S, width, col)

    one = pl.Buffered(1)

    def single(specs):
        return [pl.BlockSpec(s.block_shape, s.index_map, pipeline_mode=one) for s in specs]

    in_specs = ([pl.BlockSpec(memory_space=pltpu.SMEM), pl.BlockSpec((QB, 2 * QB), _fixed),
                 pl.BlockSpec((2 * QB, QB), _fixed)]
                + single(halo3(cq) + halo3(ck) + halo3(cv) + halo3(0) + halo3(0)))
    blk = pl.BlockSpec((TB, GW), _row)
    return pl.pallas_call(
        body, name=name, grid=(S // TB,), in_specs=in_specs,
        out_specs=[blk, blk, blk, pl.BlockSpec((NUM_BUCKETS, 128), _fixed)],
        out_shape=[jax.ShapeDtypeStruct((S, GW), BF16)] * 3 + [jax.ShapeDtypeStruct((NUM_BUCKETS, 128), F32)],
        scratch_shapes=[pltpu.VMEM((2, E, 128), F32)] * 5 + [pltpu.VMEM((2, TB, 128), F32)] * 3
        + [pltpu.VMEM((HPG, QB, 2 * QB), F32), pltpu.VMEM((HPG, 2 * QB, QB), F32), pltpu.VMEM((HPG, QB, 2 * QB), F32)],
        compiler_params=_params(("arbitrary",)),
    )(rel_bias, bk_a, bk_b, z, z, z, z, z, z, z, z, z, do, do, do, ld, ld, ld)


def _memkv_fwd(mem, gm, wkv, name):
    def body(m_ref, g_ref, w_ref, hm_ref, kv_ref):
        hm = _rms_fwd_val(m_ref[...], g_ref[...]).astype(BF16)
        hm_ref[...] = hm
        kv_ref[...] = _dot(hm, w_ref[...]).astype(BF16)

    return pl.pallas_call(
        body, name=name,
        out_shape=[jax.ShapeDtypeStruct((N_MEM, D), BF16), jax.ShapeDtypeStruct((N_MEM, 2 * MW), BF16)],
        compiler_params=_params(),
    )(mem, gm, wkv)


def _memkv_bwd(mem, gm, hm, wkv, dkv, name):
    def body(m_ref, g_ref, hm_ref, w_ref, dkv_ref, dw_ref, dg_ref):
        dkv_b = dkv_ref[...].astype(BF16)
        dw_ref[...] = _dot_tn(hm_ref[...], dkv_b)
        dhm = _dot_nt(dkv_b, w_ref[...])
        _, dgr = _rms_bwd_val(m_ref[...], g_ref[...], dhm)
        dg_ref[...] = jnp.sum(dgr, axis=0, keepdims=True)

    return pl.pallas_call(
        body, name=name,
        out_shape=[jax.ShapeDtypeStruct((D, 2 * MW), F32), jax.ShapeDtypeStruct((1, D), F32)],
        compiler_params=_params(),
    )(mem, gm, hm, wkv, dkv)


MEM_T = 512


def _mem_q_specs():
    return [pl.BlockSpec((MEM_T, MHD), lambda i, h=h: (i, C2 // MHD + h)) for h in range(MH)]


def _memattn_fwd(z, kv, name):
    S = z.shape[0]
    T = MEM_T

    def body(q0, q1, q2, q3, kv_ref, o_ref):
        for h, q_ref in enumerate((q0, q1, q2, q3)):
            kh = kv_ref[:, h * MHD:(h + 1) * MHD]
            vh = kv_ref[:, MW + h * MHD:MW + (h + 1) * MHD]
            s = _dot_nt(q_ref[...], kh) * MEM_SCALE
            e = jnp.exp(s - jnp.max(s, axis=-1, keepdims=True))
            p = e / jnp.sum(e, axis=-1, keepdims=True)
            o_ref[:, h * MHD:(h + 1) * MHD] = _dot(p.astype(BF16), vh).astype(BF16)

    return pl.pallas_call(
        body, name=name, grid=(S // T,),
        in_specs=_mem_q_specs() + [pl.BlockSpec((N_MEM, 2 * MW), _fixed)],
        out_specs=pl.BlockSpec((T, MW), _row),
        out_shape=jax.ShapeDtypeStruct((S, MW), BF16),
        compiler_params=_params(("parallel",)),
    )(z, z, z, z, kv)


def _memattn_bwd(z, kv, dom, name):
    S = z.shape[0]
    T = MEM_T

    def body(q0, q1, q2, q3, kv_ref, do_ref, dq_ref, dkv_ref):
        @pl.when(pl.program_id(0) == 0)
        def _():
            dkv_ref[...] = jnp.zeros_like(dkv_ref)

        for h, q_ref in enumerate((q0, q1, q2, q3)):
            kh = kv_ref[:, h * MHD:(h + 1) * MHD]
            vh = kv_ref[:, MW + h * MHD:MW + (h + 1) * MHD]
            qh = q_ref[...]
            doh = do_ref[:, h * MHD:(h + 1) * MHD]
            s = _dot_nt(qh, kh) * MEM_SCALE
            e = jnp.exp(s - jnp.max(s, axis=-1, keepdims=True))
            p = e / jnp.sum(e, axis=-1, keepdims=True)
            dkv_ref[:, MW + h * MHD:MW + (h + 1) * MHD] += _dot_tn(p.astype(BF16), doh)
            dp = _dot_nt(doh, vh)
            ds = (p * (dp - jnp.sum(dp * p, axis=-1, keepdims=True))).astype(BF16)
            dq_ref[:, h * MHD:(h + 1) * MHD] = (_dot(ds, kh) * MEM_SCALE).astype(BF16)
            dkv_ref[:, h * MHD:(h + 1) * MHD] += _dot_tn(ds, qh) * MEM_SCALE

    return pl.pallas_call(
        body, name=name, grid=(S // T,),
        in_specs=_mem_q_specs() + [pl.BlockSpec((N_MEM, 2 * MW), _fixed), pl.BlockSpec((T, MW), _row)],
        out_specs=[pl.BlockSpec((T, MW), _row), pl.BlockSpec((N_MEM, 2 * MW), _fixed)],
        out_shape=[jax.ShapeDtypeStruct((S, MW), BF16), jax.ShapeDtypeStruct((N_MEM, 2 * MW), F32)],
        compiler_params=_params(("arbitrary",)),
    )(z, z, z, z, kv, dom)


MERGE_T = 256
GATE_BLK = 768


def _gate_specs(T):
    return [pl.BlockSpec((T, GATE_BLK), lambda i, b=b: (i, C3 // GATE_BLK + b)) for b in range(3 * D // GATE_BLK)]


def _branches(ca_ref, oa_ref, om_ref, wco_ref, wao_ref, wmo_ref, gate_refs, bg_ref):
    ys = [_dot(ca_ref[...], wco_ref[...]), _dot(oa_ref[...], wao_ref[...]), _dot(om_ref[...], wmo_ref[...])]
    zg = jnp.concatenate([r[...] for r in gate_refs], axis=-1).astype(F32) + bg_ref[...]
    gs = [_sigmoid(zg[:, b * D:(b + 1) * D]) for b in range(3)]
    return ys, gs


def _merge_fwd(x, cact, oatt, om, z, wco, wao, wmo, wout, bgate, gpost, name):
    S = x.shape[0]
    T = MERGE_T

    def body(x_ref, ca_ref, oa_ref, om_ref, g0, g1, g2, g3, wco_ref, wao_ref, wmo_ref, wout_ref, bg_ref, gp_ref,
             x1_ref, mg_ref, t_ref):
        ys, gs = _branches(ca_ref, oa_ref, om_ref, wco_ref, wao_ref, wmo_ref, (g0, g1, g2, g3), bg_ref)
        mb = (gs[0] * ys[0] + gs[1] * ys[1] + gs[2] * ys[2]).astype(BF16)
        t = _dot(mb, wout_ref[...])
        mg_ref[...] = mb
        t_ref[...] = t
        x1_ref[...] = x_ref[...] + _rms_fwd_val(t, gp_ref[...])

    full = lambda a: pl.BlockSpec(a.shape, _fixed)
    return pl.pallas_call(
        body, name=name, grid=(S // T,),
        in_specs=[pl.BlockSpec((T, D), _row), pl.BlockSpec((T, CW), _row), pl.BlockSpec((T, GW), _row),
                  pl.BlockSpec((T, MW), _row)] + _gate_specs(T)
        + [full(wco), full(wao), full(wmo), full(wout), full(bgate), full(gpost)],
        out_specs=[pl.BlockSpec((T, D), _row)] * 3,
        out_shape=[jax.ShapeDtypeStruct((S, D), F32), jax.ShapeDtypeStruct((S, D), BF16), jax.ShapeDtypeStruct((S, D), F32)],
        compiler_params=_params(("parallel",)),
    )(x, cact, oatt, om, z, z, z, z, wco, wao, wmo, wout, bgate, gpost)


def _merge_bwd(dx1, t, mg, cact, oatt, om, z, wco, wao, wmo, wout, bgate, gpost, name):
    S = dx1.shape[0]
    T = MERGE_T

    def body(dx_ref, t_ref, mg_ref, ca_ref, oa_ref, om_ref, g0, g1, g2, g3, wco_ref, wao_ref, wmo_ref, wout_ref,
             bg_ref, gp_ref, dzg_ref, dca_ref, doa_ref, dom_ref, dwco_ref, dwao_ref, dwmo_ref, dwout_ref,
             dbg_ref, dgp_ref):
        accs = (dwco_ref, dwao_ref, dwmo_ref, dwout_ref, dbg_ref, dgp_ref)

        @pl.when(pl.program_id(0) == 0)
        def _():
            for a in accs:
                a[...] = jnp.zeros_like(a)

        dt, dgr = _rms_bwd_val(t_ref[...], gp_ref[...], dx_ref[...])
        dgp_ref[...] += jnp.sum(dgr, axis=0, keepdims=True)
        dtb = dt.astype(BF16)
        dwout_ref[...] += _dot_tn(mg_ref[...], dtb)
        dm = _dot_nt(dtb, wout_ref[...])
        ys, gs = _branches(ca_ref, oa_ref, om_ref, wco_ref, wao_ref, wmo_ref, (g0, g1, g2, g3), bg_ref)
        for b, (act_ref, w_ref, dw_ref, da_ref) in enumerate(
                ((ca_ref, wco_ref, dwco_ref, dca_ref), (oa_ref, wao_ref, dwao_ref, doa_ref),
                 (om_ref, wmo_ref, dwmo_ref, dom_ref))):
            dzg = dm * ys[b] * gs[b] * (1.0 - gs[b])
            dzg_ref[:, b * D:(b + 1) * D] = dzg.astype(BF16)
            dbg_ref[:, b * D:(b + 1) * D] += jnp.sum(dzg, axis=0, keepdims=True)
            dy = (dm * gs[b]).astype(BF16)
            dw_ref[...] += _dot_tn(act_ref[...], dy)
            da_ref[...] = _dot_nt(dy, w_ref[...]).astype(BF16)

    full = lambda a: pl.BlockSpec(a.shape, _fixed)
    fullf = lambda a: jax.ShapeDtypeStruct(a.shape, F32)
    return pl.pallas_call(
        body, name=name, grid=(S // T,),
        in_specs=[pl.BlockSpec((T, D), _row), pl.BlockSpec((T, D), _row), pl.BlockSpec((T, D), _row),
                  pl.BlockSpec((T, CW), _row), pl.BlockSpec((T, GW), _row), pl.BlockSpec((T, MW), _row)]
        + _gate_specs(T) + [full(wco), full(wao), full(wmo), full(wout), full(bgate), full(gpost)],
        out_specs=[pl.BlockSpec((T, 3 * D), _row), pl.BlockSpec((T, CW), _row), pl.BlockSpec((T, GW), _row),
                   pl.BlockSpec((T, MW), _row), full(wco), full(wao), full(wmo), full(wout), full(bgate), full(gpost)],
        out_shape=[jax.ShapeDtypeStruct((S, 3 * D), BF16), jax.ShapeDtypeStruct((S, CW), BF16),
                   jax.ShapeDtypeStruct((S, GW), BF16), jax.ShapeDtypeStruct((S, MW), BF16),
                   fullf(wco), fullf(wao), fullf(wmo), fullf(wout), fullf(bgate), fullf(gpost)],
        compiler_params=_params(("arbitrary",)),
    )(dx1, t, mg, cact, oatt, om, z, z, z, z, wco, wao, wmo, wout, bgate, gpost)


FFN_T = 256


def _ffn_fwd(x1, gu, wfo, gpost, name):
    S = x1.shape[0]
    T = FFN_T

    def body(x_ref, gu_ref, w_ref, gp_ref, x2_ref, f_ref):
        gv = gu_ref[:, :FH].astype(F32)
        uv = gu_ref[:, FH:].astype(F32)
        act = (gv * _sigmoid(gv) * uv).astype(BF16)
        f = _dot(act, w_ref[...])
        f_ref[...] = f
        x2_ref[...] = x_ref[...] + _rms_fwd_val(f, gp_ref[...])

    return pl.pallas_call(
        body, name=name, grid=(S // T,),
        in_specs=[pl.BlockSpec((T, D), _row), pl.BlockSpec((T, 2 * FH), _row), pl.BlockSpec((FH, D), _fixed),
                  pl.BlockSpec((1, D), _fixed)],
        out_specs=[pl.BlockSpec((T, D), _row)] * 2,
        out_shape=[jax.ShapeDtypeStruct((S, D), F32)] * 2,
        compiler_params=_params(("parallel",)),
    )(x1, gu, wfo, gpost)


def _ffn_bwd(dx2, f, gu, wfo, gpost, name):
    S = dx2.shape[0]
    T = FFN_T

    def body(dx_ref, f_ref, gu_ref, w_ref, gp_ref, dgu_ref, df_ref, act_ref, dgp_ref):
        @pl.when(pl.program_id(0) == 0)
        def _():
            dgp_ref[...] = jnp.zeros_like(dgp_ref)

        df, dgr = _rms_bwd_val(f_ref[...], gp_ref[...], dx_ref[...])
        dgp_ref[...] += jnp.sum(dgr, axis=0, keepdims=True)
        dfb = df.astype(BF16)
        df_ref[...] = dfb
        dact = _dot_nt(dfb, w_ref[...])
        gv = gu_ref[:, :FH].astype(F32)
        uv = gu_ref[:, FH:].astype(F32)
        sg = _sigmoid(gv)
        silu = gv * sg
        act_ref[...] = (silu * uv).astype(BF16)
        dgu_ref[:, :FH] = (dact * uv * (sg * (1.0 + gv * (1.0 - sg)))).astype(BF16)
        dgu_ref[:, FH:] = (dact * silu).astype(BF16)

    return pl.pallas_call(
        body, name=name, grid=(S // T,),
        in_specs=[pl.BlockSpec((T, D), _row), pl.BlockSpec((T, D), _row), pl.BlockSpec((T, 2 * FH), _row),
                  pl.BlockSpec((FH, D), _fixed), pl.BlockSpec((1, D), _fixed)],
        out_specs=[pl.BlockSpec((T, 2 * FH), _row), pl.BlockSpec((T, D), _row), pl.BlockSpec((T, FH), _row),
                   pl.BlockSpec((1, D), _fixed)],
        out_shape=[jax.ShapeDtypeStruct((S, 2 * FH), BF16), jax.ShapeDtypeStruct((S, D), BF16),
                   jax.ShapeDtypeStruct((S, FH), BF16), jax.ShapeDtypeStruct((1, D), F32)],
        compiler_params=_params(("arbitrary",)),
    )(dx2, f, gu, wfo, gpost)


def _loss_head(y, target, name):
    S = y.shape[0]
    T = 512

    def body(y_ref, t_ref, dy_ref, l_ref):
        @pl.when(pl.program_id(0) == 0)
        def _():
            l_ref[...] = jnp.zeros_like(l_ref)

        e = y_ref[...] - t_ref[...]
        dy_ref[...] = e * (1.0 / D)
        l_ref[...] += (0.5 / D) * jnp.sum(jnp.sum(e * e, axis=1, keepdims=True), axis=0, keepdims=True)

    return pl.pallas_call(
        body, name=name, grid=(S // T,),
        in_specs=[pl.BlockSpec((T, D), _row)] * 2,
        out_specs=[pl.BlockSpec((T, D), _row), pl.BlockSpec((8, 128), _fixed)],
        out_shape=[jax.ShapeDtypeStruct((S, D), F32), jax.ShapeDtypeStruct((8, 128), F32)],
        compiler_params=_params(("arbitrary",)),
    )(y, target)


BIG = ("w_in", "w_conv_out", "w_att_out", "w_mem_kv", "w_mem_out", "w_out", "w_ffn_in", "w_ffn_out")
SMALL = ("rel_bias", "norm_mix_pre", "b_gate", "conv_dw_bias", "conv_ln_g", "conv_ln_b", "norm_mem",
         "norm_mix_post", "norm_ffn_pre", "norm_ffn_post")


def _layer_fwd(l, x, mem, w, rel_bias):
    tag = f"_l{l}"
    h = _rms_h(x, w["norm_mix_pre"], "rms_mix" + tag)
    z = _mm_nn(h, w["w_in"], 1024, 768, BF16, "mm_in" + tag)
    yc, cact = _conv_fwd(z, w["conv_dw"], w["conv_dw_bias"], w["conv_ln_g"], w["conv_ln_b"], "conv_fwd" + tag)
    og, lg = zip(*[_att_fwd(z, rel_bias, g, f"att_fwd_g{g}" + tag) for g in range(3)])
    oatt, lse = _att_combine(og, lg, "att_combine" + tag)
    hm, kv = _memkv_fwd(mem, w["norm_mem"], w["w_mem_kv"], "memkv_fwd" + tag)
    om = _memattn_fwd(z, kv, "memattn_fwd" + tag)
    x1, mg, t = _merge_fwd(x, cact, oatt, om, z, w["w_conv_out"], w["w_att_out"], w["w_mem_out"], w["w_out"],
                           w["b_gate"], w["norm_mix_post"], "merge_fwd" + tag)
    h2 = _rms_h(x1, w["norm_ffn_pre"], "rms_ffn" + tag)
    gu = _mm_nn(h2, w["w_ffn_in"], 1024, 512, BF16, "mm_ffn_in" + tag)
    x2, f = _ffn_fwd(x1, gu, w["w_ffn_out"], w["norm_ffn_post"], "ffn_fwd" + tag)
    saved = dict(x=x, h=h, z=z, yc=yc, cact=cact, oatt=oatt, lse=lse, hm=hm, kv=kv, om=om, x1=x1, mg=mg, t=t,
                 h2=h2, gu=gu, f=f)
    return x2, saved


def _layer_bwd(l, dx2, mem, w, rel_bias, s):
    tag = f"_l{l}"
    gr = {}
    dgu, df, act, gr["norm_ffn_post"] = _ffn_bwd(dx2, s["f"], s["gu"], w["w_ffn_out"], w["norm_ffn_post"], "ffn_bwd" + tag)
    gr["w_ffn_out"] = _mm_tn(act, df, 1024, 512, "dw_ffn_out" + tag)
    gr["w_ffn_in"] = _mm_tn(s["h2"], dgu, 1024, 1408, "dw_ffn_in" + tag)
    dh2 = _mm_nt(dgu, w["w_ffn_in"], 1024, 1408, F32, "dh_ffn" + tag)
    dx1, gr["norm_ffn_pre"] = _rms_bwd(s["x1"], w["norm_ffn_pre"], dh2, dx2, "rms_ffn_bwd" + tag)
    (dzg, dcact, doatt, dom, gr["w_conv_out"], gr["w_att_out"], gr["w_mem_out"], gr["w_out"], gr["b_gate"],
     gr["norm_mix_post"]) = _merge_bwd(dx1, s["t"], s["mg"], s["cact"], s["oatt"], s["om"], s["z"], w["w_conv_out"],
                                       w["w_att_out"], w["w_mem_out"], w["w_out"], w["b_gate"], w["norm_mix_post"],
                                       "merge_bwd" + tag)
    dyc, gr["conv_ln_g"], gr["conv_ln_b"], gr["conv_dw_bias"] = _conv_bwd_ln(
        s["yc"], dcact, w["conv_ln_g"], w["conv_ln_b"], "conv_bwd_ln" + tag)
    dzc, dwdw = _conv_bwd_dw(s["z"], dyc, w["conv_dw"], "conv_bwd_dw" + tag)
    gr["conv_dw"] = dwdw[:KSIZE]
    ld = _att_prep(doatt, s["oatt"], s["lse"], "att_prep" + tag)
    dq, dk, dv, drb = zip(*[_att_bwd(s["z"], rel_bias, doatt, ld, g, f"att_bwd_g{g}" + tag) for g in range(3)])
    dqm, dkv = _memattn_bwd(s["z"], s["kv"], dom, "memattn_bwd" + tag)
    gr["w_mem_kv"], gr["norm_mem"] = _memkv_bwd(mem, w["norm_mem"], s["hm"], w["w_mem_kv"], dkv, "memkv_bwd" + tag)
    dz = jnp.concatenate([dzc, *dq, *dk, *dv, dqm, dzg], axis=1)
    gr["w_in"] = _mm_tn(s["h"], dz, 1024, 1152, "dw_in" + tag)
    dh = _mm_nt(dz, w["w_in"], 1024, 2304, F32, "dh_in" + tag)
    dx, gr["norm_mix_pre"] = _rms_bwd(s["x"], w["norm_mix_pre"], dh, dx1, "rms_mix_bwd" + tag)
    return dx, gr, list(drb)


def _rel_bias_total(parts, name):
    def body(*refs):
        out_ref = refs[-1]
        acc = jnp.zeros((NUM_BUCKETS, 128), F32)
        for l in range(DEPTH):
            for g in range(3):
                v = refs[l * 3 + g][...]
                acc = acc + (v if g == 0 else pltpu.roll(v, HPG * g, axis=1))
        out_ref[...] = acc

    return pl.pallas_call(body, name=name, out_shape=jax.ShapeDtypeStruct((NUM_BUCKETS, 128), F32),
                          compiler_params=_params())(*[p for layer in parts for p in layer])


def _local_step(x, mem, target, rel_bias, layers):
    saved = []
    for l in range(DEPTH):
        x, s = _layer_fwd(l, x, mem, layers[l], rel_bias)
        saved.append(s)
    dy, lpart = _loss_head(x, target, "loss_head")
    grads = [None] * DEPTH
    drb = [None] * DEPTH
    for l in reversed(range(DEPTH)):
        dy, grads[l], drb[l] = _layer_bwd(l, dy, mem, layers[l], rel_bias, saved[l])
    return lpart[0, 0], dy, grads, _rel_bias_total(drb, "rel_bias_total")


N_CHIPS = 4
SHARD = {"w_in": ((D, NIN // 4), 1), "w_conv_out": ((CW, D // 4), 1), "w_att_out": ((GW, D // 4), 1),
         "w_mem_kv": ((D // 4, 2 * MW), 0), "w_mem_out": ((MW, D // 4), 1), "w_out": ((D // 4, D), 0),
         "w_ffn_in": ((D, 2 * FH // 4), 1), "w_ffn_out": ((FH // 4, D), 0)}
PACK_ROWS = {n: SHARD[n][0][0] * SHARD[n][0][1] // D for n in BIG}
LROWS = sum(PACK_ROWS.values())
CDW_ROWS = 64
VEC_ROWS = (("norm_mix_pre", 1), ("b_gate", 3), ("conv_dw_bias", 1), ("conv_ln_g", 1), ("conv_ln_b", 1),
            ("norm_mem", 1), ("norm_mix_post", 1), ("norm_ffn_pre", 1), ("norm_ffn_post", 1))
VEC_LROWS = sum(r for _, r in VEC_ROWS)
REL_ROW = DEPTH * VEC_LROWS
CDW_ROW = REL_ROW + 1
CDW_GROWS = DEPTH * KSIZE * CW // D
SMALL_ROWS = -(-(CDW_ROW + CDW_GROWS) // 8) * 8


def _mesh_pos():
    return lax.axis_index("x"), lax.axis_index("y"), lax.axis_index("c")


def _other_chips(x, y):
    chips = [(1 - x, y), (x, 1 - y), (1 - x, 1 - y)]
    return chips, [2 * cx + cy for cx, cy in chips]


def _all_gather(flat, cdw):
    def body(flat_ref, cdw_ref, g_ref, gc_ref, send_sems, recv_sems, local_sems):
        x, y, c = _mesh_pos()
        j = 2 * x + y
        sibling = (x, y, 1 - c)
        chips, blocks = _other_chips(x, y)

        def copy(k, src, dst, to):
            return pltpu.make_async_remote_copy(src_ref=src, dst_ref=dst, send_sem=send_sems.at[k],
                                                recv_sem=recv_sems.at[k], device_id=to, device_id_type=MESH)

        mine = [pltpu.make_async_copy(flat_ref, g_ref.at[j], local_sems.at[0]),
                pltpu.make_async_copy(cdw_ref, gc_ref.at[j], local_sems.at[1])]
        for cp in mine:
            cp.start()
        first = [copy(k, flat_ref.at[c], g_ref.at[j, c], (*chip, c)) for k, chip in enumerate(chips)]
        first += [copy(6 + k, cdw_ref, gc_ref.at[j], (*chip, c)) for k, chip in enumerate(chips)]
        for cp in first:
            cp.start()
        passed = [copy(3 + k, g_ref.at[b, c], g_ref.at[b, c], sibling) for k, b in enumerate(blocks)]
        for k, b in enumerate(blocks):
            copy(k, flat_ref.at[c], g_ref.at[b, c], sibling).wait_recv()
            passed[k].start()
        for k, b in enumerate(blocks):
            copy(3 + k, flat_ref.at[c], g_ref.at[b, 1 - c], sibling).wait_recv()
            copy(6 + k, cdw_ref, gc_ref.at[b], sibling).wait_recv()
        for cp in first + passed:
            cp.wait_send()
        for cp in mine:
            cp.wait()

    any_spec = pl.BlockSpec(memory_space=pl.ANY)
    return pl.pallas_call(
        body, name="all_gather_weights",
        out_shape=[jax.ShapeDtypeStruct((N_CHIPS, DEPTH, LROWS, D), BF16),
                   jax.ShapeDtypeStruct((N_CHIPS, CDW_ROWS, 128), F32)],
        in_specs=[any_spec, any_spec], out_specs=[any_spec, any_spec],
        scratch_shapes=[pltpu.SemaphoreType.DMA((9,)), pltpu.SemaphoreType.DMA((9,)), pltpu.SemaphoreType.DMA((2,))],
    )(flat, cdw)


def _sibling_exchange(p):
    def body(p_ref, r_ref, send_sem, recv_sem):
        x, y, c = _mesh_pos()
        cp = pltpu.make_async_remote_copy(src_ref=p_ref.at[1 - c], dst_ref=r_ref, send_sem=send_sem,
                                          recv_sem=recv_sem, device_id=(x, y, 1 - c), device_id_type=MESH)
        cp.start()
        cp.wait()

    any_spec = pl.BlockSpec(memory_space=pl.ANY)
    return pl.pallas_call(
        body, name="grad_sibling_exchange", out_shape=jax.ShapeDtypeStruct(p.shape[1:], p.dtype),
        in_specs=[any_spec], out_specs=any_spec,
        scratch_shapes=[pltpu.SemaphoreType.DMA, pltpu.SemaphoreType.DMA],
    )(p)


def _add_own_layer(p, r):
    T = 1168

    def body(p_ref, r_ref, o_ref):
        o_ref[...] = (p_ref[0].astype(F32) + r_ref[...].astype(F32)).astype(BF16)

    return pl.pallas_call(
        body, name="grad_add_sibling", grid=(N_CHIPS, LROWS // T),
        in_specs=[pl.BlockSpec((1, 1, T, D), lambda j, i: (lax.axis_index("c"), j, i, 0)),
                  pl.BlockSpec((1, T, D), lambda j, i: (j, i, 0))],
        out_specs=pl.BlockSpec((1, T, D), lambda j, i: (j, i, 0)),
        out_shape=jax.ShapeDtypeStruct(r.shape, BF16), compiler_params=_params(("parallel", "parallel")),
    )(p, r)


def _chip_exchange(a):
    def body(a_ref, r_ref, send_sems, recv_sems, local_sem):
        x, y, c = _mesh_pos()
        j = 2 * x + y
        chips, blocks = _other_chips(x, y)
        mine = pltpu.make_async_copy(a_ref.at[j], r_ref.at[3], local_sem)
        mine.start()
        cps = [pltpu.make_async_remote_copy(src_ref=a_ref.at[b], dst_ref=r_ref.at[k], send_sem=send_sems.at[k],
                                            recv_sem=recv_sems.at[k], device_id=(*chip, c), device_id_type=MESH)
               for k, (chip, b) in enumerate(zip(chips, blocks))]
        for cp in cps:
            cp.start()
        for cp in cps:
            cp.wait_recv()
        for cp in cps:
            cp.wait_send()
        mine.wait()

    any_spec = pl.BlockSpec(memory_space=pl.ANY)
    return pl.pallas_call(
        body, name="grad_chip_exchange", out_shape=jax.ShapeDtypeStruct(a.shape, a.dtype),
        in_specs=[any_spec], out_specs=any_spec,
        scratch_shapes=[pltpu.SemaphoreType.DMA((3,)), pltpu.SemaphoreType.DMA((3,)), pltpu.SemaphoreType.DMA],
    )(a)


def _sum_chips(r):
    T = 1168

    def body(r_ref, o_ref):
        acc = r_ref[3].astype(F32)
        for k in range(3):
            acc = acc + r_ref[k].astype(F32)
        o_ref[...] = acc

    return pl.pallas_call(
        body, name="grad_sum_chips", grid=(LROWS // T,),
        in_specs=[pl.BlockSpec((4, T, D), lambda i: (0, i, 0))], out_specs=pl.BlockSpec((T, D), _row),
        out_shape=jax.ShapeDtypeStruct((LROWS, D), F32), compiler_params=_params(("parallel",)),
    )(r)


def _sibling_share(f):
    def body(f_ref, o_ref, send_sem, recv_sem, local_sem):
        x, y, c = _mesh_pos()
        mine = pltpu.make_async_copy(f_ref, o_ref.at[c], local_sem)
        mine.start()
        cp = pltpu.make_async_remote_copy(src_ref=f_ref, dst_ref=o_ref.at[c], send_sem=send_sem, recv_sem=recv_sem,
                                          device_id=(x, y, 1 - c), device_id_type=MESH)
        cp.start()
        pltpu.make_async_remote_copy(src_ref=f_ref, dst_ref=o_ref.at[1 - c], send_sem=send_sem, recv_sem=recv_sem,
                                     device_id=(x, y, 1 - c), device_id_type=MESH).wait_recv()
        cp.wait_send()
        mine.wait()

    any_spec = pl.BlockSpec(memory_space=pl.ANY)
    return pl.pallas_call(
        body, name="grad_sibling_share", out_shape=jax.ShapeDtypeStruct((DEPTH,) + f.shape, f.dtype),
        in_specs=[any_spec], out_specs=any_spec,
        scratch_shapes=[pltpu.SemaphoreType.DMA, pltpu.SemaphoreType.DMA, pltpu.SemaphoreType.DMA],
    )(f)


def _all_reduce_small(sp):
    def body(sp_ref, out_ref, buf, send_sems, recv_sems):
        x, y, c = _mesh_pos()
        me = 4 * x + 2 * y + c
        buf[0] = sp_ref[...]
        cps = []
        for k in range(1, 8):
            peer = (x ^ (k >> 2 & 1), y ^ (k >> 1 & 1), c ^ (k & 1))
            cps.append(pltpu.make_async_remote_copy(src_ref=sp_ref, dst_ref=buf.at[k], send_sem=send_sems.at[k - 1],
                                                    recv_sem=recv_sems.at[k - 1], device_id=peer, device_id_type=MESH))
        for cp in cps:
            cp.start()
        for cp in cps:
            cp.wait_recv()
        for cp in cps:
            cp.wait_send()
        acc = buf[me]
        for p in range(1, 8):
            acc = acc + buf[p ^ me]
        out_ref[...] = acc

    vm = pl.BlockSpec(memory_space=pltpu.VMEM)
    return pl.pallas_call(
        body, name="all_reduce_small", out_shape=jax.ShapeDtypeStruct(sp.shape, F32),
        in_specs=[vm], out_specs=vm,
        scratch_shapes=[pltpu.VMEM((8,) + sp.shape, F32), pltpu.SemaphoreType.DMA((7,)), pltpu.SemaphoreType.DMA((7,))],
        compiler_params=_params(),
    )(sp)


def _adamw(w, g, m, v, name):
    R, C = w.shape
    T = next((t for t in (256, 128) if R % t == 0), R)

    def body(w_ref, g_ref, m_ref, v_ref, d_ref, m2_ref, v2_ref):
        gv = g_ref[...]
        m2 = ADAM_B1 * m_ref[...] + (1.0 - ADAM_B1) * gv
        v2 = ADAM_B2 * v_ref[...] + (1.0 - ADAM_B2) * (gv * gv)
        m_hat = m2 / (1.0 - ADAM_B1 ** ADAM_STEP)
        v_hat = v2 / (1.0 - ADAM_B2 ** ADAM_STEP)
        d_ref[...] = -ADAM_LR * (m_hat / (jnp.sqrt(v_hat) + ADAM_EPS) + ADAM_WD * w_ref[...])
        m2_ref[...] = m2
        v2_ref[...] = v2

    blk = pl.BlockSpec((T, C), _row)
    return pl.pallas_call(
        body, name=name, grid=(R // T,), in_specs=[blk] * 4, out_specs=[blk] * 3,
        out_shape=[jax.ShapeDtypeStruct((R, C), F32)] * 3, compiler_params=_params(("parallel",)),
    )(w, g, m, v)


def _pack_vectors(get, rel, cdw):
    rows = []
    for l in range(DEPTH):
        for n, r in VEC_ROWS:
            v = get(n)[l]
            rows.append(jnp.pad(v, (0, r * D - v.shape[0])).reshape(r, D))
    rows.append(jnp.pad(rel.reshape(-1), (0, D - NUM_BUCKETS * 3 * HPG)).reshape(1, D))
    rows.append(cdw.reshape(CDW_GROWS, D))
    rows.append(jnp.zeros((SMALL_ROWS - CDW_ROW - CDW_GROWS, D), F32))
    return jnp.concatenate(rows, axis=0)


def _unpack_vectors(packed, lens):
    out = {n: [] for n, _ in VEC_ROWS}
    for l in range(DEPTH):
        at = l * VEC_LROWS
        for n, r in VEC_ROWS:
            out[n].append(packed[at:at + r].reshape(-1)[:lens[n]])
            at += r
    rel = packed[REL_ROW, :NUM_BUCKETS * 3 * HPG].reshape(NUM_BUCKETS, 3 * HPG)
    return {n: jnp.stack(v) for n, v in out.items()}, rel


INPUT_NAMES = ("x", "mem") + ("rel_bias", "norm_mix_pre", "w_in", "b_gate", "conv_dw", "conv_dw_bias", "conv_ln_g",
                              "conv_ln_b", "w_conv_out", "w_att_out", "norm_mem", "w_mem_kv", "w_mem_out", "w_out",
                              "norm_mix_post", "norm_ffn_pre", "w_ffn_in", "w_ffn_out", "norm_ffn_post")
WEIGHT_NAMES = INPUT_NAMES[2:]


def kernel(*args):
    nw = len(WEIGHT_NAMES)
    a = dict(zip(INPUT_NAMES, args[:2 + nw]))
    target = args[2 + nw]
    mom = dict(zip(WEIGHT_NAMES, args[3 + nw:3 + 2 * nw]))
    var = dict(zip(WEIGHT_NAMES, args[3 + 2 * nw:3 + 3 * nw]))
    xi, yi, _ = _mesh_pos()
    chip = 2 * xi + yi

    flat = jnp.concatenate([a[n].astype(BF16).reshape(DEPTH, PACK_ROWS[n], D) for n in BIG], axis=1)
    cdw = jnp.pad(a["conv_dw"].reshape(DEPTH * KSIZE, CW // 4), ((0, CDW_ROWS - DEPTH * KSIZE), (0, 0)))
    gathered, gcdw = _all_gather(flat, cdw)
    conv_dw = gcdw[:, :DEPTH * KSIZE].reshape(N_CHIPS, DEPTH, KSIZE, CW // 4).transpose(1, 2, 0, 3)
    conv_dw = jnp.pad(conv_dw.reshape(DEPTH, KSIZE, CW), ((0, 0), (0, 1), (0, 0)))
    layers = []
    for l in range(DEPTH):
        w = {"conv_dw": conv_dw[l]}
        at = 0
        for n in BIG:
            (s0, s1), axis = SHARD[n]
            blk = gathered[:, l, at:at + PACK_ROWS[n]].reshape(N_CHIPS, s0, s1)
            w[n] = blk.reshape(N_CHIPS * s0, s1) if axis == 0 else blk.transpose(1, 0, 2).reshape(s0, N_CHIPS * s1)
            at += PACK_ROWS[n]
        for n, _ in VEC_ROWS:
            w[n] = a[n][l][None, :]
        layers.append(w)

    loss_part, gx, grads, drel = _local_step(a["x"][0], a["mem"][0], target[0], a["rel_bias"], layers)
    loss = lax.psum(loss_part, ("x", "y", "c"))

    packed = []
    for l in range(DEPTH):
        parts = []
        for n in BIG:
            (s0, s1), axis = SHARD[n]
            g = grads[l][n]
            g = g.reshape(N_CHIPS, s0, s1) if axis == 0 else g.reshape(s0, N_CHIPS, s1).transpose(1, 0, 2)
            parts.append(g.reshape(N_CHIPS, PACK_ROWS[n], D))
        packed.append(jnp.concatenate(parts, axis=1).astype(BF16))
    packed = jnp.stack(packed)
    chip_sums = _add_own_layer(packed, _sibling_exchange(packed))
    reduced = _sibling_share(_sum_chips(_chip_exchange(chip_sums)))

    gvec = _all_reduce_small(_pack_vectors(
        lambda n: jnp.stack([grads[l][n][0] for l in range(DEPTH)]), drel[:, :3 * HPG],
        jnp.stack([grads[l]["conv_dw"] for l in range(DEPTH)])))
    lens = {n: a[n].shape[1] for n, _ in VEC_ROWS}
    g_vec, g_rel = _unpack_vectors(gvec, lens)
    g_cdw = lax.dynamic_slice_in_dim(gvec[CDW_ROW:CDW_ROW + CDW_GROWS].reshape(DEPTH, KSIZE, CW), chip * (CW // 4),
                                     CW // 4, axis=2)

    grad, delta, new_m, new_v = {}, {}, {}, {}
    at = 0
    for n in BIG:
        shape = a[n].shape
        g = reduced[:, at:at + PACK_ROWS[n]].reshape(shape)
        at += PACK_ROWS[n]
        flat2 = lambda t: t.reshape(shape[0] * shape[1], shape[2])
        d, m2, v2 = _adamw(flat2(a[n]), flat2(g), flat2(mom[n]), flat2(var[n]), "adamw_" + n)
        grad[n], delta[n], new_m[n], new_v[n] = g, d.reshape(shape), m2.reshape(shape), v2.reshape(shape)
    shape = a["conv_dw"].shape
    flat2 = lambda t: t.reshape(shape[0] * shape[1], shape[2])
    d, m2, v2 = _adamw(flat2(a["conv_dw"]), flat2(g_cdw), flat2(mom["conv_dw"]), flat2(var["conv_dw"]), "adamw_conv_dw")
    grad["conv_dw"], delta["conv_dw"], new_m["conv_dw"], new_v["conv_dw"] = (
        g_cdw, d.reshape(shape), m2.reshape(shape), v2.reshape(shape))
    zero_cdw = jnp.zeros((DEPTH, KSIZE, CW), F32)
    pk = lambda src: _pack_vectors(lambda n: src[n], src["rel_bias"], zero_cdw)
    d, m2, v2 = _adamw(pk(a), gvec, pk(mom), pk(var), "adamw_vectors")
    for src, dst in ((d, delta), (m2, new_m), (v2, new_v)):
        vec, rel = _unpack_vectors(src, lens)
        dst.update(vec)
        dst["rel_bias"] = rel
    grad.update(g_vec)
    grad["rel_bias"] = g_rel

    outs = [loss, gx[None]]
    for group in (grad, delta, new_m, new_v):
        outs += [group[n] for n in WEIGHT_NAMES]
    return tuple(outs)
```

```python
import functools
import math

import jax
import jax.numpy as jnp
from jax import lax
from jax.experimental import pallas as pl
from jax.experimental.pallas import tpu as pltpu

F32 = jnp.float32
BF16 = jnp.bfloat16
I32 = jnp.int32

D = 1024
DEPTH = 2
N_MEM = 256
CW = 512
KSIZE = 31
PAD = KSIZE // 2
DILS = (1, 4, 16)
RADIUS = 64
HPG = 4
HD = 64
GW = HPG * HD
MH = 4
MHD = 128
MW = MH * MHD
FH = 2816
NIN = 6912
C1 = 2 * CW
C2 = C1 + 9 * GW
C3 = C2 + MW
NUM_BUCKETS = 32
MAX_DISTANCE = 1024
RMS_EPS = 1e-6
LN_EPS = 1e-5
NEG_INF = -1e30
ATT_SCALE = HD ** -0.5
MEM_SCALE = MHD ** -0.5

ADAM_LR = 0.001
ADAM_B1 = 0.9
ADAM_B2 = 0.999
ADAM_EPS = 1e-08
ADAM_WD = 0.01
ADAM_STEP = 10

VMEM_LIMIT_BYTES = 56 * 1024 * 1024
ATT_QB = 128
ATT_TB = 16 * ATT_QB

MESH = pl.DeviceIdType.MESH


def _params(sem=None):
    return pltpu.CompilerParams(dimension_semantics=sem, vmem_limit_bytes=VMEM_LIMIT_BYTES)


def _sigmoid(v):
    return 1.0 / (1.0 + jnp.exp(-v))


def _dot(a, b):
    return jnp.dot(a, b, preferred_element_type=F32)


def _dot_nt(a, b):
    return lax.dot_general(a, b, (((1,), (1,)), ((), ())), preferred_element_type=F32)


def _dot_tn(a, b):
    return lax.dot_general(a, b, (((0,), (0,)), ((), ())), preferred_element_type=F32)


def _rms_fwd_val(v, g):
    r = lax.rsqrt(jnp.mean(v * v, axis=-1, keepdims=True) + RMS_EPS)
    return v * r * g


def _rms_bwd_val(v, g, dy):
    r = lax.rsqrt(jnp.mean(v * v, axis=-1, keepdims=True) + RMS_EPS)
    vh = v * r
    dvh = dy * g
    dv = r * (dvh - vh * jnp.mean(dvh * vh, axis=-1, keepdims=True))
    return dv, dy * vh


def _row(i):
    return (i, 0)


def _fixed(*_):
    return (0, 0)


def _mm_nn(a, b, tm, tn, out_dtype, name):
    M, K = a.shape
    N = b.shape[1]

    def body(a_ref, b_ref, o_ref):
        o_ref[...] = _dot(a_ref[...], b_ref[...]).astype(out_dtype)

    return pl.pallas_call(
        body, name=name, grid=(N // tn, M // tm),
        in_specs=[pl.BlockSpec((tm, K), lambda j, i: (i, 0)), pl.BlockSpec((K, tn), lambda j, i: (0, j))],
        out_specs=pl.BlockSpec((tm, tn), lambda j, i: (i, j)),
        out_shape=jax.ShapeDtypeStruct((M, N), out_dtype),
        compiler_params=_params(("parallel", "parallel")),
    )(a, b)


def _mm_nt(a, b, tm, tc, out_dtype, name):
    M, N = a.shape
    K = b.shape[0]
    nk = N // tc

    def body(a_ref, b_ref, o_ref, acc_ref):
        k = pl.program_id(1)

        @pl.when(k == 0)
        def _():
            acc_ref[...] = jnp.zeros_like(acc_ref)

        acc_ref[...] += _dot_nt(a_ref[...], b_ref[...])

        @pl.when(k == nk - 1)
        def _():
            o_ref[...] = acc_ref[...].astype(out_dtype)

    return pl.pallas_call(
        body, name=name, grid=(M // tm, nk),
        in_specs=[pl.BlockSpec((tm, tc), lambda i, k: (i, k)), pl.BlockSpec((K, tc), lambda i, k: (0, k))],
        out_specs=pl.BlockSpec((tm, K), lambda i, k: (i, 0)),
        out_shape=jax.ShapeDtypeStruct((M, K), out_dtype),
        scratch_shapes=[pltpu.VMEM((tm, K), F32)],
        compiler_params=_params(("parallel", "arbitrary")),
    )(a, b)


def _mm_tn(a, b, ts, tn, name):
    S, K = a.shape
    N = b.shape[1]

    def body(a_ref, b_ref, o_ref):
        @pl.when(pl.program_id(1) == 0)
        def _():
            o_ref[...] = jnp.zeros_like(o_ref)

        o_ref[...] += _dot_tn(a_ref[...], b_ref[...])

    return pl.pallas_call(
        body, name=name, grid=(N // tn, S // ts),
        in_specs=[pl.BlockSpec((ts, K), lambda j, s: (s, 0)), pl.BlockSpec((ts, tn), lambda j, s: (s, j))],
        out_specs=pl.BlockSpec((K, tn), lambda j, s: (0, j)),
        out_shape=jax.ShapeDtypeStruct((K, N), F32),
        compiler_params=_params(("parallel", "arbitrary")),
    )(a, b)


def _rms_h(x, g, name):
    S = x.shape[0]
    T = 512

    def body(x_ref, g_ref, h_ref):
        h_ref[...] = _rms_fwd_val(x_ref[...], g_ref[...]).astype(BF16)

    return pl.pallas_call(
        body, name=name, grid=(S // T,),
        in_specs=[pl.BlockSpec((T, D), _row), pl.BlockSpec((1, D), _fixed)],
        out_specs=pl.BlockSpec((T, D), _row),
        out_shape=jax.ShapeDtypeStruct((S, D), BF16),
        compiler_params=_params(("parallel",)),
    )(x, g)


def _rms_bwd(x, g, dh, dres, name):
    S = x.shape[0]
    T = 512

    def body(x_ref, g_ref, dh_ref, dres_ref, dx_ref, dg_ref):
        dv, dgr = _rms_bwd_val(x_ref[...], g_ref[...], dh_ref[...].astype(F32))
        dx_ref[...] = dres_ref[...] + dv

        @pl.when(pl.program_id(0) == 0)
        def _():
            dg_ref[...] = jnp.zeros_like(dg_ref)

        dg_ref[...] += jnp.sum(dgr, axis=0, keepdims=True)

    return pl.pallas_call(
        body, name=name, grid=(S // T,),
        in_specs=[pl.BlockSpec((T, D), _row), pl.BlockSpec((1, D), _fixed), pl.BlockSpec((T, D), _row),
                  pl.BlockSpec((T, D), _row)],
        out_specs=[pl.BlockSpec((T, D), _row), pl.BlockSpec((1, D), _fixed)],
        out_shape=[jax.ShapeDtypeStruct((S, D), F32), jax.ShapeDtypeStruct((1, D), F32)],
        compiler_params=_params(("arbitrary",)),
    )(x, g, dh, dres)


CONV_T = 256
CONV_HALO = 16
CONV_RC = 64


def _halo_specs(T, halo, S, width, col):
    per = T // halo
    last = S // halo - 1
    return [
        pl.BlockSpec((T, width), lambda i: (i, col)),
        pl.BlockSpec((halo, width), lambda i: (jnp.maximum(i * per - 1, 0), col)),
        pl.BlockSpec((halo, width), lambda i: (jnp.minimum((i + 1) * per, last), col)),
    ]


def _glu(zb):
    zb = zb.astype(F32)
    return zb[:, :CW] * _sigmoid(zb[:, CW:])


def _fill_ext(ext_ref, cur, prev, nxt, T, halo):
    i = pl.program_id(0)
    n = pl.num_programs(0)
    ext_ref[0:halo, :] = jnp.where(i > 0, prev, 0.0)
    ext_ref[halo:halo + T, :] = cur
    ext_ref[halo + T:2 * halo + T, :] = jnp.where(i < n - 1, nxt, 0.0)


def _conv_fwd(z, wdw, bdw, lng, lnb, name):
    S = z.shape[0]
    T, HL, RC = CONV_T, CONV_HALO, CONV_RC

    def body(cur_ref, prev_ref, next_ref, w_ref, b_ref, g_ref, bb_ref, yc_ref, act_ref, ext_ref):
        _fill_ext(ext_ref, _glu(cur_ref[...]), _glu(prev_ref[...]), _glu(next_ref[...]), T, HL)
        for c in range(T // RC):
            acc = jnp.zeros((RC, CW), F32)
            for k in range(KSIZE):
                o = c * RC + k + HL - PAD
                acc = acc + w_ref[k:k + 1, :] * ext_ref[o:o + RC, :]
            yc = acc + b_ref[...]
            yc_ref[c * RC:(c + 1) * RC, :] = yc
            mu = jnp.mean(yc, axis=-1, keepdims=True)
            xc = yc - mu
            ln = xc * lax.rsqrt(jnp.mean(xc * xc, axis=-1, keepdims=True) + LN_EPS) * g_ref[...] + bb_ref[...]
            act_ref[c * RC:(c + 1) * RC, :] = (ln * _sigmoid(ln)).astype(BF16)

    return pl.pallas_call(
        body, name=name, grid=(S // T,),
        in_specs=_halo_specs(T, HL, S, C1, 0) + [pl.BlockSpec((32, CW), _fixed)] + [pl.BlockSpec((1, CW), _fixed)] * 3,
        out_specs=[pl.BlockSpec((T, CW), _row), pl.BlockSpec((T, CW), _row)],
        out_shape=[jax.ShapeDtypeStruct((S, CW), F32), jax.ShapeDtypeStruct((S, CW), BF16)],
        scratch_shapes=[pltpu.VMEM((T + 2 * HL, CW), F32)],
        compiler_params=_params(("parallel",)),
    )(z, z, z, wdw, bdw, lng, lnb)


def _conv_bwd_ln(yc, dact, lng, lnb, name):
    S = yc.shape[0]
    T = 512

    def body(yc_ref, da_ref, g_ref, b_ref, dyc_ref, dg_ref, db_ref, dbias_ref):
        yc_v = yc_ref[...]
        mu = jnp.mean(yc_v, axis=-1, keepdims=True)
        xc = yc_v - mu
        r = lax.rsqrt(jnp.mean(xc * xc, axis=-1, keepdims=True) + LN_EPS)
        yn = xc * r
        ln = yn * g_ref[...] + b_ref[...]
        sg = _sigmoid(ln)
        dln = da_ref[...].astype(F32) * (sg * (1.0 + ln * (1.0 - sg)))
        dyn = dln * g_ref[...]
        dyc = r * (dyn - jnp.mean(dyn, axis=-1, keepdims=True) - yn * jnp.mean(dyn * yn, axis=-1, keepdims=True))
        dyc_ref[...] = dyc

        @pl.when(pl.program_id(0) == 0)
        def _():
            dg_ref[...] = jnp.zeros_like(dg_ref)
            db_ref[...] = jnp.zeros_like(db_ref)
            dbias_ref[...] = jnp.zeros_like(dbias_ref)

        dg_ref[...] += jnp.sum(dln * yn, axis=0, keepdims=True)
        db_ref[...] += jnp.sum(dln, axis=0, keepdims=True)
        dbias_ref[...] += jnp.sum(dyc, axis=0, keepdims=True)

    vec = pl.BlockSpec((1, CW), _fixed)
    return pl.pallas_call(
        body, name=name, grid=(S // T,),
        in_specs=[pl.BlockSpec((T, CW), _row), pl.BlockSpec((T, CW), _row), vec, vec],
        out_specs=[pl.BlockSpec((T, CW), _row), vec, vec, vec],
        out_shape=[jax.ShapeDtypeStruct((S, CW), F32)] + [jax.ShapeDtypeStruct((1, CW), F32)] * 3,
        compiler_params=_params(("arbitrary",)),
    )(yc, dact, lng, lnb)


def _conv_bwd_dw(z, dyc, wdw, name):
    S = z.shape[0]
    T, HL, RC = CONV_T, CONV_HALO, CONV_RC

    def body(zc_ref, zp_ref, zn_ref, dc_ref, dp_ref, dn_ref, w_ref, dz_ref, dw_ref, uext_ref, dext_ref):
        _fill_ext(uext_ref, _glu(zc_ref[...]), _glu(zp_ref[...]), _glu(zn_ref[...]), T, HL)
        _fill_ext(dext_ref, dc_ref[...], dp_ref[...], dn_ref[...], T, HL)

        @pl.when(pl.program_id(0) == 0)
        def _():
            dw_ref[...] = jnp.zeros_like(dw_ref)

        for c in range(T // RC):
            dcur = dext_ref[HL + c * RC:HL + (c + 1) * RC, :]
            du = jnp.zeros((RC, CW), F32)
            for k in range(KSIZE):
                o = c * RC + HL + PAD - k
                du = du + w_ref[k:k + 1, :] * dext_ref[o:o + RC, :]
                o2 = c * RC + k + HL - PAD
                dw_ref[k:k + 1, :] += jnp.sum(dcur * uext_ref[o2:o2 + RC, :], axis=0, keepdims=True)
            zc = zc_ref[c * RC:(c + 1) * RC, :].astype(F32)
            a, gt = zc[:, :CW], zc[:, CW:]
            sg = _sigmoid(gt)
            dz_ref[c * RC:(c + 1) * RC, 0:CW] = (du * sg).astype(BF16)
            dz_ref[c * RC:(c + 1) * RC, CW:C1] = (du * a * sg * (1.0 - sg)).astype(BF16)

    return pl.pallas_call(
        body, name=name, grid=(S // T,),
        in_specs=_halo_specs(T, HL, S, C1, 0) + _halo_specs(T, HL, S, CW, 0) + [pl.BlockSpec((32, CW), _fixed)],
        out_specs=[pl.BlockSpec((T, C1), _row), pl.BlockSpec((32, CW), _fixed)],
        out_shape=[jax.ShapeDtypeStruct((S, C1), BF16), jax.ShapeDtypeStruct((32, CW), F32)],
        scratch_shapes=[pltpu.VMEM((T + 2 * HL, CW), F32), pltpu.VMEM((T + 2 * HL, CW), F32)],
        compiler_params=_params(("arbitrary",)),
    )(z, z, z, dyc, dyc, dyc, wdw)


def _t5_bucket(rel):
    nb = NUM_BUCKETS // 2
    max_exact = nb // 2
    ret = jnp.where(rel > 0, nb, 0)
    n = jnp.abs(rel)
    nf = jnp.maximum(n, 1).astype(F32)
    large = max_exact + (jnp.log(nf / max_exact) / math.log(MAX_DISTANCE / max_exact)
                         * (nb - max_exact)).astype(I32)
    large = jnp.minimum(large, nb - 1)
    return ret + jnp.where(n < max_exact, n, large)


def _offsets_qk(nq, nk, shift):
    return lax.broadcasted_iota(I32, (nq, nk), 1) + shift - lax.broadcasted_iota(I32, (nq, nk), 0)


def _bias_table(bk, rb_ref, col, off):
    acc = jnp.zeros(bk.shape, F32)
    for b in range(NUM_BUCKETS):
        acc = jnp.where(bk == b, rb_ref[b, col], acc)
    return jnp.where(jnp.abs(off) <= RADIUS, acc, NEG_INF)


def _to_halves(scr, row0, val):
    rows = val.shape[0]
    v = val.astype(F32)
    scr[0, row0:row0 + rows, :] = v[:, :128]
    scr[1, row0:row0 + rows, :] = v[:, 128:]


def _heads(halves):
    return [halves[h // 2][:, (h % 2) * HD:(h % 2 + 1) * HD] for h in range(HPG)]


def _join_heads(parts):
    return [jnp.concatenate(parts[0:2], axis=-1), jnp.concatenate(parts[2:4], axis=-1)]


def _att_units(d, fn):
    nj = ATT_TB // (ATT_QB * d)
    for r in range(d):
        if nj == 1:
            fn(r, 0)
        else:
            def step(j, c, r=r):
                fn(r, j)
                return c
            lax.fori_loop(0, nj, step, 0)


def _unit_row(r, j, d):
    if isinstance(j, int):
        return j * ATT_QB * d + r
    return pl.multiple_of(j * (ATT_QB * d), ATT_QB) + r


def _att_fwd(z, rel_bias, g, name):
    S = z.shape[0]
    d = DILS[g]
    TB, QB = ATT_TB, ATT_QB
    H = RADIUS * d
    L = S // d
    cq, ck, cv = (C1 + g * GW) // GW, (C1 + 3 * GW + g * GW) // GW, (C1 + 6 * GW + g * GW) // GW
    bk = _t5_bucket(_offsets_qk(QB, 2 * QB, -RADIUS) * d)

    def body(rb_ref, bk_ref, q_ref, kc_ref, kp_ref, kn_ref, vc_ref, vp_ref, vn_ref, o_ref, l_ref,
             qs, ks, vs, os_, ls, bias):
        i = pl.program_id(0)

        @pl.when(i == 0)
        def _():
            off = _offsets_qk(QB, 2 * QB, -RADIUS)
            for h in range(HPG):
                bias[h] = _bias_table(bk_ref[...], rb_ref, g * HPG + h, off)

        _to_halves(qs, 0, q_ref[...])
        for scr, p_ref, c_ref, n_ref in ((ks, kp_ref, kc_ref, kn_ref), (vs, vp_ref, vc_ref, vn_ref)):
            _to_halves(scr, 0, p_ref[...])
            _to_halves(scr, H, c_ref[...])
            _to_halves(scr, H + TB, n_ref[...])

        def unit(r, j):
            row = _unit_row(r, j, d)
            q4 = _heads([qs[hf, pl.ds(row, QB, stride=d), :] for hf in (0, 1)])
            k4 = _heads([ks[hf, pl.ds(row, 2 * QB, stride=d), :] for hf in (0, 1)])
            v4 = _heads([vs[hf, pl.ds(row, 2 * QB, stride=d), :] for hf in (0, 1)])
            km = lax.broadcasted_iota(I32, (QB, 2 * QB), 1) + (i * (TB // d) + j * QB - RADIUS)
            valid = jnp.where(km >= 0, km, L) < L
            o4, l4 = [], []
            for h in range(HPG):
                s = _dot_nt(q4[h].astype(BF16), k4[h].astype(BF16)) * ATT_SCALE + bias[h]
                s = jnp.where(valid, s, NEG_INF)
                m = jnp.max(s, axis=-1, keepdims=True)
                e = jnp.exp(s - m)
                den = jnp.sum(e, axis=-1, keepdims=True)
                o4.append(_dot((e / den).astype(BF16), v4[h].astype(BF16)))
                l4.append(jnp.broadcast_to(m + jnp.log(den), (QB, HD)))
            for hf, (ov, lv) in enumerate(zip(_join_heads(o4), _join_heads(l4))):
                os_[hf, pl.ds(row, QB, stride=d), :] = ov
                ls[hf, pl.ds(row, QB, stride=d), :] = lv

        _att_units(d, unit)
        for hf in (0, 1):
            o_ref[:, hf * 128:(hf + 1) * 128] = os_[hf].astype(BF16)
            l_ref[:, hf * 128:(hf + 1) * 128] = ls[hf]

    def halo3(col):
        c, p, n = _halo_specs(TB, H, S, GW, col)
        return [c, p, n]

    return pl.pallas_call(
        body, name=name, grid=(S // TB,),
        in_specs=[pl.BlockSpec(memory_space=pltpu.SMEM), pl.BlockSpec((QB, 2 * QB), _fixed),
                  pl.BlockSpec((TB, GW), lambda i: (i, cq))] + halo3(ck) + halo3(cv),
        out_specs=[pl.BlockSpec((TB, GW), _row), pl.BlockSpec((TB, GW), _row)],
        out_shape=[jax.ShapeDtypeStruct((S, GW), BF16), jax.ShapeDtypeStruct((S, GW), F32)],
        scratch_shapes=[pltpu.VMEM((2, TB, 128), F32), pltpu.VMEM((2, TB + 2 * H, 128), F32),
                        pltpu.VMEM((2, TB + 2 * H, 128), F32), pltpu.VMEM((2, TB, 128), F32),
                        pltpu.VMEM((2, TB, 128), F32), pltpu.VMEM((HPG, QB, 2 * QB), F32)],
        compiler_params=_params(("arbitrary",)),
    )(rel_bias, bk, z, z, z, z, z, z, z)


def _att_combine(os3, ls3, name):
    S = os3[0].shape[0]
    T = 1024

    def body(o1, o2, o3, l1, l2, l3, o_ref, l_ref):
        lv = [l1[...], l2[...], l3[...]]
        m = jnp.maximum(jnp.maximum(lv[0], lv[1]), lv[2])
        e = [jnp.exp(v - m) for v in lv]
        den = e[0] + e[1] + e[2]
        acc = jnp.zeros_like(m)
        for ev, o in zip(e, (o1, o2, o3)):
            acc = acc + (ev / den) * o[...].astype(F32)
        o_ref[...] = acc.astype(BF16)
        l_ref[...] = m + jnp.log(den)

    blk = pl.BlockSpec((T, GW), _row)
    return pl.pallas_call(
        body, name=name, grid=(S // T,), in_specs=[blk] * 6, out_specs=[blk, blk],
        out_shape=[jax.ShapeDtypeStruct((S, GW), BF16), jax.ShapeDtypeStruct((S, GW), F32)],
        compiler_params=_params(("parallel",)),
    )(*os3, *ls3)


def _att_prep(do, o, lse, name):
    S = do.shape[0]
    T = 1024

    def body(do_ref, o_ref, l_ref, out_ref):
        prod = do_ref[...].astype(F32) * o_ref[...].astype(F32)
        dd = [jnp.broadcast_to(jnp.sum(prod[:, h * HD:(h + 1) * HD], axis=-1, keepdims=True), (T, HD))
              for h in range(HPG)]
        lane = lax.broadcasted_iota(I32, (T, GW), 1)
        out_ref[...] = jnp.where(lane % HD < HD // 2, l_ref[...], jnp.concatenate(dd, axis=-1))

    blk = pl.BlockSpec((T, GW), _row)
    return pl.pallas_call(
        body, name=name, grid=(S // T,), in_specs=[blk] * 3, out_specs=blk,
        out_shape=jax.ShapeDtypeStruct((S, GW), F32), compiler_params=_params(("parallel",)),
    )(do, o, lse)


def _att_bwd(z, rel_bias, do, ld, g, name):
    S = z.shape[0]
    d = DILS[g]
    TB, QB = ATT_TB, ATT_QB
    H = RADIUS * d
    L = S // d
    E = TB + 2 * H
    cq, ck, cv = (C1 + g * GW) // GW, (C1 + 3 * GW + g * GW) // GW, (C1 + 6 * GW + g * GW) // GW
    bk_a = _t5_bucket(_offsets_qk(QB, 2 * QB, -RADIUS) * d)
    bk_b = _t5_bucket(_offsets_qk(2 * QB, QB, RADIUS) * d)

    def body(rb_ref, bka_ref, bkb_ref, *refs):
        ins, (dq_ref, dk_ref, dv_ref, db_ref) = refs[:15], refs[15:19]
        qs, ks, vs, dos, ls, dqs, dks, dvs, bias_a, bias_b, dbias = refs[19:]
        i = pl.program_id(0)
        n = pl.num_programs(0)

        @pl.when(i == 0)
        def _():
            off_a = _offsets_qk(QB, 2 * QB, -RADIUS)
            off_b = _offsets_qk(2 * QB, QB, RADIUS)
            for h in range(HPG):
                bias_a[h] = _bias_table(bka_ref[...], rb_ref, g * HPG + h, off_a)
                bias_b[h] = _bias_table(bkb_ref[...], rb_ref, g * HPG + h, off_b)
            dbias[...] = jnp.zeros_like(dbias)

        for a, scr in enumerate((qs, ks, vs, dos, ls)):
            c_ref, p_ref, n_ref = ins[3 * a:3 * a + 3]
            _to_halves(scr, 0, p_ref[...])
            _to_halves(scr, H, c_ref[...])
            _to_halves(scr, H + TB, n_ref[...])

        def unit(r, j):
            row = _unit_row(r, j, d)
            cur = row + H
            ld = lambda scr, at, nrow: _heads([scr[hf, pl.ds(at, nrow, stride=d), :] for hf in (0, 1)])
            q_c, q_e = ld(qs, cur, QB), ld(qs, row, 2 * QB)
            k_c, k_e = ld(ks, cur, QB), ld(ks, row, 2 * QB)
            v_c, v_e = ld(vs, cur, QB), ld(vs, row, 2 * QB)
            do_c, do_e = ld(dos, cur, QB), ld(dos, row, 2 * QB)
            l_c, l_e = ld(ls, cur, QB), ld(ls, row, 2 * QB)
            m0 = i * (TB // d) + j * QB - RADIUS
            km = lax.broadcasted_iota(I32, (QB, 2 * QB), 1) + m0
            valid_a = jnp.where(km >= 0, km, L) < L
            qm = lax.broadcasted_iota(I32, (2 * QB, QB), 0) + m0
            valid_b = jnp.where(qm >= 0, qm, L) < L
            dq4, dk4, dv4 = [], [], []
            for h in range(HPG):
                qc, qe = q_c[h].astype(BF16), q_e[h].astype(BF16)
                kc, ke = k_c[h].astype(BF16), k_e[h].astype(BF16)
                vc, ve = v_c[h].astype(BF16), v_e[h].astype(BF16)
                dc, de = do_c[h].astype(BF16), do_e[h].astype(BF16)
                dd_c = l_c[h][:, HD // 2:HD // 2 + 1]
                dd_e = l_e[h][:, HD // 2:HD // 2 + 1]
                s = _dot_nt(qc, ke) * ATT_SCALE + bias_a[h]
                p = jnp.where(valid_a, jnp.exp(s - l_c[h][:, 0:1]), 0.0)
                ds = p * (_dot_nt(dc, ve) - dd_c)
                dbias[h] += ds
                dq4.append(_dot(ds.astype(BF16), ke) * ATT_SCALE)
                s2 = _dot_nt(qe, kc) * ATT_SCALE + bias_b[h]
                p2 = jnp.where(valid_b, jnp.exp(s2 - l_e[h][:, 0:1]), 0.0)
                dv4.append(_dot_tn(p2.astype(BF16), de))
                ds2 = p2 * (_dot_nt(de, vc) - dd_e)
                dk4.append(_dot_tn(ds2.astype(BF16), qe) * ATT_SCALE)
            for scr, parts in ((dqs, dq4), (dks, dk4), (dvs, dv4)):
                for hf, val in enumerate(_join_heads(parts)):
                    scr[hf, pl.ds(row, QB, stride=d), :] = val

        _att_units(d, unit)
        for scr, ref in ((dqs, dq_ref), (dks, dk_ref), (dvs, dv_ref)):
            for hf in (0, 1):
                ref[:, hf * 128:(hf + 1) * 128] = scr[hf].astype(BF16)

        @pl.when(i == n - 1)
        def _():
            rows = lax.broadcasted_iota(I32, (NUM_BUCKETS, 128), 0)
            lanes = lax.broadcasted_iota(I32, (NUM_BUCKETS, 128), 1)
            out = jnp.zeros((NUM_BUCKETS, 128), F32)
            bk = bka_ref[...]
            for h in range(HPG):
                acc = dbias[h]
                for b in range(NUM_BUCKETS):
                    tot = jnp.sum(jnp.sum(jnp.where(bk == b, acc, 0.0), axis=1, keepdims=True), axis=0, keepdims=True)
                    out = out + jnp.where((rows == b) & (lanes == h), tot, 0.0)
            db_ref[...] = out

    def halo3(col, width=GW):
        return _halo_specs(TB, H, S, width, col)

    one = pl.Buffered(1)

    def single(specs):
        return [pl.BlockSpec(s.block_shape, s.index_map, pipeline_mode=one) for s in specs]

    in_specs = ([pl.BlockSpec(memory_space=pltpu.SMEM), pl.BlockSpec((QB, 2 * QB), _fixed),
                 pl.BlockSpec((2 * QB, QB), _fixed)]
                + single(halo3(cq) + halo3(ck) + halo3(cv) + halo3(0) + halo3(0)))
    blk = pl.BlockSpec((TB, GW), _row)
    return pl.pallas_call(
        body, name=name, grid=(S // TB,), in_specs=in_specs,
        out_specs=[blk, blk, blk, pl.BlockSpec((NUM_BUCKETS, 128), _fixed)],
        out_shape=[jax.ShapeDtypeStruct((S, GW), BF16)] * 3 + [jax.ShapeDtypeStruct((NUM_BUCKETS, 128), F32)],
        scratch_shapes=[pltpu.VMEM((2, E, 128), F32)] * 5 + [pltpu.VMEM((2, TB, 128), F32)] * 3
        + [pltpu.VMEM((HPG, QB, 2 * QB), F32), pltpu.VMEM((HPG, 2 * QB, QB), F32), pltpu.VMEM((HPG, QB, 2 * QB), F32)],
        compiler_params=_params(("arbitrary",)),
    )(rel_bias, bk_a, bk_b, z, z, z, z, z, z, z, z, z, do, do, do, ld, ld, ld)


def _memkv_fwd(mem, gm, wkv, name):
    def body(m_ref, g_ref, w_ref, hm_ref, kv_ref):
        hm = _rms_fwd_val(m_ref[...], g_ref[...]).astype(BF16)
        hm_ref[...] = hm
        kv_ref[...] = _dot(hm, w_ref[...]).astype(BF16)

    return pl.pallas_call(
        body, name=name,
        out_shape=[jax.ShapeDtypeStruct((N_MEM, D), BF16), jax.ShapeDtypeStruct((N_MEM, 2 * MW), BF16)],
        compiler_params=_params(),
    )(mem, gm, wkv)


def _memkv_bwd(mem, gm, hm, wkv, dkv, name):
    def body(m_ref, g_ref, hm_ref, w_ref, dkv_ref, dw_ref, dg_ref):
        dkv_b = dkv_ref[...].astype(BF16)
        dw_ref[...] = _dot_tn(hm_ref[...], dkv_b)
        dhm = _dot_nt(dkv_b, w_ref[...])
        _, dgr = _rms_bwd_val(m_ref[...], g_ref[...], dhm)
        dg_ref[...] = jnp.sum(dgr, axis=0, keepdims=True)

    return pl.pallas_call(
        body, name=name,
        out_shape=[jax.ShapeDtypeStruct((D, 2 * MW), F32), jax.ShapeDtypeStruct((1, D), F32)],
        compiler_params=_params(),
    )(mem, gm, hm, wkv, dkv)


MEM_T = 512


def _mem_q_specs():
    return [pl.BlockSpec((MEM_T, MHD), lambda i, h=h: (i, C2 // MHD + h)) for h in range(MH)]


def _memattn_fwd(z, kv, name):
    S = z.shape[0]
    T = MEM_T

    def body(q0, q1, q2, q3, kv_ref, o_ref):
        for h, q_ref in enumerate((q0, q1, q2, q3)):
            kh = kv_ref[:, h * MHD:(h + 1) * MHD]
            vh = kv_ref[:, MW + h * MHD:MW + (h + 1) * MHD]
            s = _dot_nt(q_ref[...], kh) * MEM_SCALE
            e = jnp.exp(s - jnp.max(s, axis=-1, keepdims=True))
            p = e / jnp.sum(e, axis=-1, keepdims=True)
            o_ref[:, h * MHD:(h + 1) * MHD] = _dot(p.astype(BF16), vh).astype(BF16)

    return pl.pallas_call(
        body, name=name, grid=(S // T,),
        in_specs=_mem_q_specs() + [pl.BlockSpec((N_MEM, 2 * MW), _fixed)],
        out_specs=pl.BlockSpec((T, MW), _row),
        out_shape=jax.ShapeDtypeStruct((S, MW), BF16),
        compiler_params=_params(("parallel",)),
    )(z, z, z, z, kv)


def _memattn_bwd(z, kv, dom, name):
    S = z.shape[0]
    T = MEM_T

    def body(q0, q1, q2, q3, kv_ref, do_ref, dq_ref, dkv_ref):
        @pl.when(pl.program_id(0) == 0)
        def _():
            dkv_ref[...] = jnp.zeros_like(dkv_ref)

        for h, q_ref in enumerate((q0, q1, q2, q3)):
            kh = kv_ref[:, h * MHD:(h + 1) * MHD]
            vh = kv_ref[:, MW + h * MHD:MW + (h + 1) * MHD]
            qh = q_ref[...]
            doh = do_ref[:, h * MHD:(h + 1) * MHD]
            s = _dot_nt(qh, kh) * MEM_SCALE
            e = jnp.exp(s - jnp.max(s, axis=-1, keepdims=True))
            p = e / jnp.sum(e, axis=-1, keepdims=True)
            dkv_ref[:, MW + h * MHD:MW + (h + 1) * MHD] += _dot_tn(p.astype(BF16), doh)
            dp = _dot_nt(doh, vh)
            ds = (p * (dp - jnp.sum(dp * p, axis=-1, keepdims=True))).astype(BF16)
            dq_ref[:, h * MHD:(h + 1) * MHD] = (_dot(ds, kh) * MEM_SCALE).astype(BF16)
            dkv_ref[:, h * MHD:(h + 1) * MHD] += _dot_tn(ds, qh) * MEM_SCALE

    return pl.pallas_call(
        body, name=name, grid=(S // T,),
        in_specs=_mem_q_specs() + [pl.BlockSpec((N_MEM, 2 * MW), _fixed), pl.BlockSpec((T, MW), _row)],
        out_specs=[pl.BlockSpec((T, MW), _row), pl.BlockSpec((N_MEM, 2 * MW), _fixed)],
        out_shape=[jax.ShapeDtypeStruct((S, MW), BF16), jax.ShapeDtypeStruct((N_MEM, 2 * MW), F32)],
        compiler_params=_params(("arbitrary",)),
    )(z, z, z, z, kv, dom)


MERGE_T = 256
GATE_BLK = 768


def _gate_specs(T):
    return [pl.BlockSpec((T, GATE_BLK), lambda i, b=b: (i, C3 // GATE_BLK + b)) for b in range(3 * D // GATE_BLK)]


def _branches(ca_ref, oa_ref, om_ref, wco_ref, wao_ref, wmo_ref, gate_refs, bg_ref):
    ys = [_dot(ca_ref[...], wco_ref[...]), _dot(oa_ref[...], wao_ref[...]), _dot(om_ref[...], wmo_ref[...])]
    zg = jnp.concatenate([r[...] for r in gate_refs], axis=-1).astype(F32) + bg_ref[...]
    gs = [_sigmoid(zg[:, b * D:(b + 1) * D]) for b in range(3)]
    return ys, gs


def _merge_fwd(x, cact, oatt, om, z, wco, wao, wmo, wout, bgate, gpost, name):
    S = x.shape[0]
    T = MERGE_T

    def body(x_ref, ca_ref, oa_ref, om_ref, g0, g1, g2, g3, wco_ref, wao_ref, wmo_ref, wout_ref, bg_ref, gp_ref,
             x1_ref, mg_ref, t_ref):
        ys, gs = _branches(ca_ref, oa_ref, om_ref, wco_ref, wao_ref, wmo_ref, (g0, g1, g2, g3), bg_ref)
        mb = (gs[0] * ys[0] + gs[1] * ys[1] + gs[2] * ys[2]).astype(BF16)
        t = _dot(mb, wout_ref[...])
        mg_ref[...] = mb
        t_ref[...] = t
        x1_ref[...] = x_ref[...] + _rms_fwd_val(t, gp_ref[...])

    full = lambda a: pl.BlockSpec(a.shape, _fixed)
    return pl.pallas_call(
        body, name=name, grid=(S // T,),
        in_specs=[pl.BlockSpec((T, D), _row), pl.BlockSpec((T, CW), _row), pl.BlockSpec((T, GW), _row),
                  pl.BlockSpec((T, MW), _row)] + _gate_specs(T)
        + [full(wco), full(wao), full(wmo), full(wout), full(bgate), full(gpost)],
        out_specs=[pl.BlockSpec((T, D), _row)] * 3,
        out_shape=[jax.ShapeDtypeStruct((S, D), F32), jax.ShapeDtypeStruct((S, D), BF16), jax.ShapeDtypeStruct((S, D), F32)],
        compiler_params=_params(("parallel",)),
    )(x, cact, oatt, om, z, z, z, z, wco, wao, wmo, wout, bgate, gpost)


def _merge_bwd(dx1, t, mg, cact, oatt, om, z, wco, wao, wmo, wout, bgate, gpost, name):
    S = dx1.shape[0]
    T = MERGE_T

    def body(dx_ref, t_ref, mg_ref, ca_ref, oa_ref, om_ref, g0, g1, g2, g3, wco_ref, wao_ref, wmo_ref, wout_ref,
             bg_ref, gp_ref, dzg_ref, dca_ref, doa_ref, dom_ref, dwco_ref, dwao_ref, dwmo_ref, dwout_ref,
             dbg_ref, dgp_ref):
        accs = (dwco_ref, dwao_ref, dwmo_ref, dwout_ref, dbg_ref, dgp_ref)

        @pl.when(pl.program_id(0) == 0)
        def _():
            for a in accs:
                a[...] = jnp.zeros_like(a)

        dt, dgr = _rms_bwd_val(t_ref[...], gp_ref[...], dx_ref[...])
        dgp_ref[...] += jnp.sum(dgr, axis=0, keepdims=True)
        dtb = dt.astype(BF16)
        dwout_ref[...] += _dot_tn(mg_ref[...], dtb)
        dm = _dot_nt(dtb, wout_ref[...])
        ys, gs = _branches(ca_ref, oa_ref, om_ref, wco_ref, wao_ref, wmo_ref, (g0, g1, g2, g3), bg_ref)
        for b, (act_ref, w_ref, dw_ref, da_ref) in enumerate(
                ((ca_ref, wco_ref, dwco_ref, dca_ref), (oa_ref, wao_ref, dwao_ref, doa_ref),
                 (om_ref, wmo_ref, dwmo_ref, dom_ref))):
            dzg = dm * ys[b] * gs[b] * (1.0 - gs[b])
            dzg_ref[:, b * D:(b + 1) * D] = dzg.astype(BF16)
            dbg_ref[:, b * D:(b + 1) * D] += jnp.sum(dzg, axis=0, keepdims=True)
            dy = (dm * gs[b]).astype(BF16)
            dw_ref[...] += _dot_tn(act_ref[...], dy)
            da_ref[...] = _dot_nt(dy, w_ref[...]).astype(BF16)

    full = lambda a: pl.BlockSpec(a.shape, _fixed)
    fullf = lambda a: jax.ShapeDtypeStruct(a.shape, F32)
    return pl.pallas_call(
        body, name=name, grid=(S // T,),
        in_specs=[pl.BlockSpec((T, D), _row), pl.BlockSpec((T, D), _row), pl.BlockSpec((T, D), _row),
                  pl.BlockSpec((T, CW), _row), pl.BlockSpec((T, GW), _row), pl.BlockSpec((T, MW), _row)]
        + _gate_specs(T) + [full(wco), full(wao), full(wmo), full(wout), full(bgate), full(gpost)],
        out_specs=[pl.BlockSpec((T, 3 * D), _row), pl.BlockSpec((T, CW), _row), pl.BlockSpec((T, GW), _row),
                   pl.BlockSpec((T, MW), _row), full(wco), full(wao), full(wmo), full(wout), full(bgate), full(gpost)],
        out_shape=[jax.ShapeDtypeStruct((S, 3 * D), BF16), jax.ShapeDtypeStruct((S, CW), BF16),
                   jax.ShapeDtypeStruct((S, GW), BF16), jax.ShapeDtypeStruct((S, MW), BF16),
                   fullf(wco), fullf(wao), fullf(wmo), fullf(wout), fullf(bgate), fullf(gpost)],
        compiler_params=_params(("arbitrary",)),
    )(dx1, t, mg, cact, oatt, om, z, z, z, z, wco, wao, wmo, wout, bgate, gpost)


FFN_T = 256


def _ffn_fwd(x1, gu, wfo, gpost, name):
    S = x1.shape[0]
    T = FFN_T

    def body(x_ref, gu_ref, w_ref, gp_ref, x2_ref, f_ref):
        gv = gu_ref[:, :FH].astype(F32)
        uv = gu_ref[:, FH:].astype(F32)
        act = (gv * _sigmoid(gv) * uv).astype(BF16)
        f = _dot(act, w_ref[...])
        f_ref[...] = f
        x2_ref[...] = x_ref[...] + _rms_fwd_val(f, gp_ref[...])

    return pl.pallas_call(
        body, name=name, grid=(S // T,),
        in_specs=[pl.BlockSpec((T, D), _row), pl.BlockSpec((T, 2 * FH), _row), pl.BlockSpec((FH, D), _fixed),
                  pl.BlockSpec((1, D), _fixed)],
        out_specs=[pl.BlockSpec((T, D), _row)] * 2,
        out_shape=[jax.ShapeDtypeStruct((S, D), F32)] * 2,
        compiler_params=_params(("parallel",)),
    )(x1, gu, wfo, gpost)


def _ffn_bwd(dx2, f, gu, wfo, gpost, name):
    S = dx2.shape[0]
    T = FFN_T

    def body(dx_ref, f_ref, gu_ref, w_ref, gp_ref, dgu_ref, df_ref, act_ref, dgp_ref):
        @pl.when(pl.program_id(0) == 0)
        def _():
            dgp_ref[...] = jnp.zeros_like(dgp_ref)

        df, dgr = _rms_bwd_val(f_ref[...], gp_ref[...], dx_ref[...])
        dgp_ref[...] += jnp.sum(dgr, axis=0, keepdims=True)
        dfb = df.astype(BF16)
        df_ref[...] = dfb
        dact = _dot_nt(dfb, w_ref[...])
        gv = gu_ref[:, :FH].astype(F32)
        uv = gu_ref[:, FH:].astype(F32)
        sg = _sigmoid(gv)
        silu = gv * sg
        act_ref[...] = (silu * uv).astype(BF16)
        dgu_ref[:, :FH] = (dact * uv * (sg * (1.0 + gv * (1.0 - sg)))).astype(BF16)
        dgu_ref[:, FH:] = (dact * silu).astype(BF16)

    return pl.pallas_call(
        body, name=name, grid=(S // T,),
        in_specs=[pl.BlockSpec((T, D), _row), pl.BlockSpec((T, D), _row), pl.BlockSpec((T, 2 * FH), _row),
                  pl.BlockSpec((FH, D), _fixed), pl.BlockSpec((1, D), _fixed)],
        out_specs=[pl.BlockSpec((T, 2 * FH), _row), pl.BlockSpec((T, D), _row), pl.BlockSpec((T, FH), _row),
                   pl.BlockSpec((1, D), _fixed)],
        out_shape=[jax.ShapeDtypeStruct((S, 2 * FH), BF16), jax.ShapeDtypeStruct((S, D), BF16),
                   jax.ShapeDtypeStruct((S, FH), BF16), jax.ShapeDtypeStruct((1, D), F32)],
        compiler_params=_params(("arbitrary",)),
    )(dx2, f, gu, wfo, gpost)


def _loss_head(y, target, name):
    S = y.shape[0]
    T = 512

    def body(y_ref, t_ref, dy_ref, l_ref):
        @pl.when(pl.program_id(0) == 0)
        def _():
            l_ref[...] = jnp.zeros_like(l_ref)

        e = y_ref[...] - t_ref[...]
        dy_ref[...] = e * (1.0 / D)
        l_ref[...] += (0.5 / D) * jnp.sum(jnp.sum(e * e, axis=1, keepdims=True), axis=0, keepdims=True)

    return pl.pallas_call(
        body, name=name, grid=(S // T,),
        in_specs=[pl.BlockSpec((T, D), _row)] * 2,
        out_specs=[pl.BlockSpec((T, D), _row), pl.BlockSpec((8, 128), _fixed)],
        out_shape=[jax.ShapeDtypeStruct((S, D), F32), jax.ShapeDtypeStruct((8, 128), F32)],
        compiler_params=_params(("arbitrary",)),
    )(y, target)


BIG = ("w_in", "w_conv_out", "w_att_out", "w_mem_kv", "w_mem_out", "w_out", "w_ffn_in", "w_ffn_out")
SMALL = ("rel_bias", "norm_mix_pre", "b_gate", "conv_dw_bias", "conv_ln_g", "conv_ln_b", "norm_mem",
         "norm_mix_post", "norm_ffn_pre", "norm_ffn_post")


def _layer_fwd(l, x, mem, w, rel_bias):
    tag = f"_l{l}"
    h = _rms_h(x, w["norm_mix_pre"], "rms_mix" + tag)
    z = _mm_nn(h, w["w_in"], 1024, 768, BF16, "mm_in" + tag)
    yc, cact = _conv_fwd(z, w["conv_dw"], w["conv_dw_bias"], w["conv_ln_g"], w["conv_ln_b"], "conv_fwd" + tag)
    og, lg = zip(*[_att_fwd(z, rel_bias, g, f"att_fwd_g{g}" + tag) for g in range(3)])
    oatt, lse = _att_combine(og, lg, "att_combine" + tag)
    hm, kv = _memkv_fwd(mem, w["norm_mem"], w["w_mem_kv"], "memkv_fwd" + tag)
    om = _memattn_fwd(z, kv, "memattn_fwd" + tag)
    x1, mg, t = _merge_fwd(x, cact, oatt, om, z, w["w_conv_out"], w["w_att_out"], w["w_mem_out"], w["w_out"],
                           w["b_gate"], w["norm_mix_post"], "merge_fwd" + tag)
    h2 = _rms_h(x1, w["norm_ffn_pre"], "rms_ffn" + tag)
    gu = _mm_nn(h2, w["w_ffn_in"], 1024, 512, BF16, "mm_ffn_in" + tag)
    x2, f = _ffn_fwd(x1, gu, w["w_ffn_out"], w["norm_ffn_post"], "ffn_fwd" + tag)
    saved = dict(x=x, h=h, z=z, yc=yc, cact=cact, oatt=oatt, lse=lse, hm=hm, kv=kv, om=om, x1=x1, mg=mg, t=t,
                 h2=h2, gu=gu, f=f)
    return x2, saved


def _layer_bwd(l, dx2, mem, w, rel_bias, s):
    tag = f"_l{l}"
    gr = {}
    dgu, df, act, gr["norm_ffn_post"] = _ffn_bwd(dx2, s["f"], s["gu"], w["w_ffn_out"], w["norm_ffn_post"], "ffn_bwd" + tag)
    gr["w_ffn_out"] = _mm_tn(act, df, 1024, 512, "dw_ffn_out" + tag)
    gr["w_ffn_in"] = _mm_tn(s["h2"], dgu, 1024, 1408, "dw_ffn_in" + tag)
    dh2 = _mm_nt(dgu, w["w_ffn_in"], 1024, 1408, F32, "dh_ffn" + tag)
    dx1, gr["norm_ffn_pre"] = _rms_bwd(s["x1"], w["norm_ffn_pre"], dh2, dx2, "rms_ffn_bwd" + tag)
    (dzg, dcact, doatt, dom, gr["w_conv_out"], gr["w_att_out"], gr["w_mem_out"], gr["w_out"], gr["b_gate"],
     gr["norm_mix_post"]) = _merge_bwd(dx1, s["t"], s["mg"], s["cact"], s["oatt"], s["om"], s["z"], w["w_conv_out"],
                                       w["w_att_out"], w["w_mem_out"], w["w_out"], w["b_gate"], w["norm_mix_post"],
                                       "merge_bwd" + tag)
    dyc, gr["conv_ln_g"], gr["conv_ln_b"], gr["conv_dw_bias"] = _conv_bwd_ln(
        s["yc"], dcact, w["conv_ln_g"], w["conv_ln_b"], "conv_bwd_ln" + tag)
    dzc, dwdw = _conv_bwd_dw(s["z"], dyc, w["conv_dw"], "conv_bwd_dw" + tag)
    gr["conv_dw"] = dwdw[:KSIZE]
    ld = _att_prep(doatt, s["oatt"], s["lse"], "att_prep" + tag)
    dq, dk, dv, drb = zip(*[_att_bwd(s["z"], rel_bias, doatt, ld, g, f"att_bwd_g{g}" + tag) for g in range(3)])
    dqm, dkv = _memattn_bwd(s["z"], s["kv"], dom, "memattn_bwd" + tag)
    gr["w_mem_kv"], gr["norm_mem"] = _memkv_bwd(mem, w["norm_mem"], s["hm"], w["w_mem_kv"], dkv, "memkv_bwd" + tag)
    dz = jnp.concatenate([dzc, *dq, *dk, *dv, dqm, dzg], axis=1)
    gr["w_in"] = _mm_tn(s["h"], dz, 1024, 1152, "dw_in" + tag)
    dh = _mm_nt(dz, w["w_in"], 1024, 2304, F32, "dh_in" + tag)
    dx, gr["norm_mix_pre"] = _rms_bwd(s["x"], w["norm_mix_pre"], dh, dx1, "rms_mix_bwd" + tag)
    return dx, gr, list(drb)


def _rel_bias_total(parts, name):
    def body(*refs):
        out_ref = refs[-1]
        acc = jnp.zeros((NUM_BUCKETS, 128), F32)
        for l in range(DEPTH):
            for g in range(3):
                v = refs[l * 3 + g][...]
                acc = acc + (v if g == 0 else pltpu.roll(v, HPG * g, axis=1))
        out_ref[...] = acc

    return pl.pallas_call(body, name=name, out_shape=jax.ShapeDtypeStruct((NUM_BUCKETS, 128), F32),
                          compiler_params=_params())(*[p for layer in parts for p in layer])


def _local_step(x, mem, target, rel_bias, layers):
    saved = []
    for l in range(DEPTH):
        x, s = _layer_fwd(l, x, mem, layers[l], rel_bias)
        saved.append(s)
    dy, lpart = _loss_head(x, target, "loss_head")
    grads = [None] * DEPTH
    drb = [None] * DEPTH
    for l in reversed(range(DEPTH)):
        dy, grads[l], drb[l] = _layer_bwd(l, dy, mem, layers[l], rel_bias, saved[l])
    return lpart[0, 0], dy, grads, _rel_bias_total(drb, "rel_bias_total")


N_CHIPS = 4
SHARD = {"w_in": ((D, NIN // 4), 1), "w_conv_out": ((CW, D // 4), 1), "w_att_out": ((GW, D // 4), 1),
         "w_mem_kv": ((D // 4, 2 * MW), 0), "w_mem_out": ((MW, D // 4), 1), "w_out": ((D // 4, D), 0),
         "w_ffn_in": ((D, 2 * FH // 4), 1), "w_ffn_out": ((FH // 4, D), 0)}
PACK_ROWS = {n: SHARD[n][0][0] * SHARD[n][0][1] // D for n in BIG}
LROWS = sum(PACK_ROWS.values())
CDW_ROWS = 64
VEC_ROWS = (("norm_mix_pre", 1), ("b_gate", 3), ("conv_dw_bias", 1), ("conv_ln_g", 1), ("conv_ln_b", 1),
            ("norm_mem", 1), ("norm_mix_post", 1), ("norm_ffn_pre", 1), ("norm_ffn_post", 1))
VEC_LROWS = sum(r for _, r in VEC_ROWS)
REL_ROW = DEPTH * VEC_LROWS
CDW_ROW = REL_ROW + 1
CDW_GROWS = DEPTH * KSIZE * CW // D
SMALL_ROWS = -(-(CDW_ROW + CDW_GROWS) // 8) * 8


def _mesh_pos():
    return lax.axis_index("x"), lax.axis_index("y"), lax.axis_index("c")


def _other_chips(x, y):
    chips = [(1 - x, y), (x, 1 - y), (1 - x, 1 - y)]
    return chips, [2 * cx + cy for cx, cy in chips]


def _all_gather(flat, cdw):
    def body(flat_ref, cdw_ref, g_ref, gc_ref, send_sems, recv_sems):
        x, y, c = _mesh_pos()
        j = 2 * x + y
        sibling = (x, y, 1 - c)
        chips, blocks = _other_chips(x, y)

        def copy(k, src, dst, to):
            return pltpu.make_async_remote_copy(src_ref=src, dst_ref=dst, send_sem=send_sems.at[k],
                                                recv_sem=recv_sems.at[k], device_id=to, device_id_type=MESH)

        first = [copy(k, flat_ref.at[c], g_ref.at[j, c], (*chip, c)) for k, chip in enumerate(chips)]
        first += [copy(6 + k, cdw_ref, gc_ref.at[j], (*chip, c)) for k, chip in enumerate(chips)]
        for cp in first:
            cp.start()
        passed = [copy(3 + k, g_ref.at[b, c], g_ref.at[b, c], sibling) for k, b in enumerate(blocks)]
        for k, b in enumerate(blocks):
            copy(k, flat_ref.at[c], g_ref.at[b, c], sibling).wait_recv()
            passed[k].start()
        for k, b in enumerate(blocks):
            copy(3 + k, flat_ref.at[c], g_ref.at[b, 1 - c], sibling).wait_recv()
            copy(6 + k, cdw_ref, gc_ref.at[b], sibling).wait_recv()
        for cp in first + passed:
            cp.wait_send()

    any_spec = pl.BlockSpec(memory_space=pl.ANY)
    return pl.pallas_call(
        body, name="all_gather_weights",
        out_shape=[jax.ShapeDtypeStruct((N_CHIPS, DEPTH, LROWS, D), BF16),
                   jax.ShapeDtypeStruct((N_CHIPS, CDW_ROWS, 128), F32)],
        in_specs=[any_spec, any_spec], out_specs=[any_spec, any_spec],
        scratch_shapes=[pltpu.SemaphoreType.DMA((9,)), pltpu.SemaphoreType.DMA((9,))],
    )(flat, cdw)


def _sibling_exchange(p):
    def body(p_ref, r_ref, send_sem, recv_sem):
        x, y, c = _mesh_pos()
        cp = pltpu.make_async_remote_copy(src_ref=p_ref.at[1 - c], dst_ref=r_ref, send_sem=send_sem,
                                          recv_sem=recv_sem, device_id=(x, y, 1 - c), device_id_type=MESH)
        cp.start()
        cp.wait()

    any_spec = pl.BlockSpec(memory_space=pl.ANY)
    return pl.pallas_call(
        body, name="grad_sibling_exchange", out_shape=jax.ShapeDtypeStruct(p.shape[1:], p.dtype),
        in_specs=[any_spec], out_specs=any_spec,
        scratch_shapes=[pltpu.SemaphoreType.DMA, pltpu.SemaphoreType.DMA],
    )(p)


SUM_T = 1168


def _add_own_layer(where, p, r):
    T = SUM_T

    def body(where_ref, p_ref, r_ref, o_ref):
        o_ref[...] = (p_ref[0].astype(F32) + r_ref[...].astype(F32)).astype(BF16)

    return pl.pallas_call(
        body, name="grad_add_sibling",
        grid_spec=pltpu.PrefetchScalarGridSpec(
            num_scalar_prefetch=1, grid=(N_CHIPS, LROWS // T),
            in_specs=[pl.BlockSpec((1, 1, T, D), lambda j, i, wh: (wh[0], j, i, 0)),
                      pl.BlockSpec((1, T, D), lambda j, i, wh: (j, i, 0))],
            out_specs=pl.BlockSpec((1, T, D), lambda j, i, wh: (j, i, 0))),
        out_shape=jax.ShapeDtypeStruct(r.shape, BF16), compiler_params=_params(("parallel", "parallel")),
    )(where, p, r)


def _chip_exchange(a):
    def body(a_ref, r_ref, send_sems, recv_sems):
        x, y, c = _mesh_pos()
        chips, blocks = _other_chips(x, y)
        cps = [pltpu.make_async_remote_copy(src_ref=a_ref.at[b], dst_ref=r_ref.at[k], send_sem=send_sems.at[k],
                                            recv_sem=recv_sems.at[k], device_id=(*chip, c), device_id_type=MESH)
               for k, (chip, b) in enumerate(zip(chips, blocks))]
        for cp in cps:
            cp.start()
        for cp in cps:
            cp.wait_recv()
        for cp in cps:
            cp.wait_send()

    any_spec = pl.BlockSpec(memory_space=pl.ANY)
    return pl.pallas_call(
        body, name="grad_chip_exchange", out_shape=jax.ShapeDtypeStruct((3,) + a.shape[1:], a.dtype),
        in_specs=[any_spec], out_specs=any_spec,
        scratch_shapes=[pltpu.SemaphoreType.DMA((3,)), pltpu.SemaphoreType.DMA((3,))],
    )(a)


def _sum_chips(where, a, r):
    T = SUM_T

    def body(where_ref, a_ref, r_ref, o_ref):
        acc = a_ref[0].astype(F32)
        for k in range(3):
            acc = acc + r_ref[k].astype(F32)
        o_ref[...] = acc

    return pl.pallas_call(
        body, name="grad_sum_chips",
        grid_spec=pltpu.PrefetchScalarGridSpec(
            num_scalar_prefetch=1, grid=(LROWS // T,),
            in_specs=[pl.BlockSpec((1, T, D), lambda i, wh: (wh[1], i, 0)),
                      pl.BlockSpec((3, T, D), lambda i, wh: (0, i, 0))],
            out_specs=pl.BlockSpec((T, D), lambda i, wh: (i, 0))),
        out_shape=jax.ShapeDtypeStruct((LROWS, D), F32), compiler_params=_params(("parallel",)),
    )(where, a, r)


def _sibling_share(f):
    def body(f_ref, o_ref, send_sem, recv_sem):
        x, y, c = _mesh_pos()
        cp = pltpu.make_async_remote_copy(src_ref=f_ref, dst_ref=o_ref, send_sem=send_sem, recv_sem=recv_sem,
                                          device_id=(x, y, 1 - c), device_id_type=MESH)
        cp.start()
        cp.wait()

    any_spec = pl.BlockSpec(memory_space=pl.ANY)
    return pl.pallas_call(
        body, name="grad_sibling_share", out_shape=jax.ShapeDtypeStruct(f.shape, f.dtype),
        in_specs=[any_spec], out_specs=any_spec,
        scratch_shapes=[pltpu.SemaphoreType.DMA, pltpu.SemaphoreType.DMA],
    )(f)


def _all_reduce_small(sp):
    def body(sp_ref, out_ref, buf, send_sems, recv_sems):
        x, y, c = _mesh_pos()
        me = 4 * x + 2 * y + c
        buf[0] = sp_ref[...]
        cps = []
        for k in range(1, 8):
            peer = (x ^ (k >> 2 & 1), y ^ (k >> 1 & 1), c ^ (k & 1))
            cps.append(pltpu.make_async_remote_copy(src_ref=sp_ref, dst_ref=buf.at[k], send_sem=send_sems.at[k - 1],
                                                    recv_sem=recv_sems.at[k - 1], device_id=peer, device_id_type=MESH))
        for cp in cps:
            cp.start()
        for cp in cps:
            cp.wait_recv()
        for cp in cps:
            cp.wait_send()
        acc = buf[me]
        for p in range(1, 8):
            acc = acc + buf[p ^ me]
        out_ref[...] = acc

    vm = pl.BlockSpec(memory_space=pltpu.VMEM)
    return pl.pallas_call(
        body, name="all_reduce_small", out_shape=jax.ShapeDtypeStruct(sp.shape, F32),
        in_specs=[vm], out_specs=vm,
        scratch_shapes=[pltpu.VMEM((8,) + sp.shape, F32), pltpu.SemaphoreType.DMA((7,)), pltpu.SemaphoreType.DMA((7,))],
        compiler_params=_params(),
    )(sp)


def _adamw(w, g, m, v, name):
    R, C = w.shape
    T = next((t for t in (256, 128) if R % t == 0), R)

    def body(w_ref, g_ref, m_ref, v_ref, d_ref, m2_ref, v2_ref):
        gv = g_ref[...]
        m2 = ADAM_B1 * m_ref[...] + (1.0 - ADAM_B1) * gv
        v2 = ADAM_B2 * v_ref[...] + (1.0 - ADAM_B2) * (gv * gv)
        m_hat = m2 / (1.0 - ADAM_B1 ** ADAM_STEP)
        v_hat = v2 / (1.0 - ADAM_B2 ** ADAM_STEP)
        d_ref[...] = -ADAM_LR * (m_hat / (jnp.sqrt(v_hat) + ADAM_EPS) + ADAM_WD * w_ref[...])
        m2_ref[...] = m2
        v2_ref[...] = v2

    blk = pl.BlockSpec((T, C), _row)
    return pl.pallas_call(
        body, name=name, grid=(R // T,), in_specs=[blk] * 4, out_specs=[blk] * 3,
        out_shape=[jax.ShapeDtypeStruct((R, C), F32)] * 3, compiler_params=_params(("parallel",)),
    )(w, g, m, v)


def _pack_vectors(get, rel, cdw):
    rows = []
    for l in range(DEPTH):
        for n, r in VEC_ROWS:
            v = get(n)[l]
            rows.append(jnp.pad(v, (0, r * D - v.shape[0])).reshape(r, D))
    rows.append(jnp.pad(rel.reshape(-1), (0, D - NUM_BUCKETS * 3 * HPG)).reshape(1, D))
    rows.append(cdw.reshape(CDW_GROWS, D))
    rows.append(jnp.zeros((SMALL_ROWS - CDW_ROW - CDW_GROWS, D), F32))
    return jnp.concatenate(rows, axis=0)


def _unpack_vectors(packed, lens):
    out = {n: [] for n, _ in VEC_ROWS}
    for l in range(DEPTH):
        at = l * VEC_LROWS
        for n, r in VEC_ROWS:
            out[n].append(packed[at:at + r].reshape(-1)[:lens[n]])
            at += r
    rel = packed[REL_ROW, :NUM_BUCKETS * 3 * HPG].reshape(NUM_BUCKETS, 3 * HPG)
    return {n: jnp.stack(v) for n, v in out.items()}, rel


INPUT_NAMES = ("x", "mem") + ("rel_bias", "norm_mix_pre", "w_in", "b_gate", "conv_dw", "conv_dw_bias", "conv_ln_g",
                              "conv_ln_b", "w_conv_out", "w_att_out", "norm_mem", "w_mem_kv", "w_mem_out", "w_out",
                              "norm_mix_post", "norm_ffn_pre", "w_ffn_in", "w_ffn_out", "norm_ffn_post")
WEIGHT_NAMES = INPUT_NAMES[2:]


def kernel(*args):
    nw = len(WEIGHT_NAMES)
    a = dict(zip(INPUT_NAMES, args[:2 + nw]))
    target = args[2 + nw]
    mom = dict(zip(WEIGHT_NAMES, args[3 + nw:3 + 2 * nw]))
    var = dict(zip(WEIGHT_NAMES, args[3 + 2 * nw:3 + 3 * nw]))
    xi, yi, ci = _mesh_pos()
    chip = 2 * xi + yi
    where = jnp.stack([ci, chip]).astype(I32)

    flat = jnp.concatenate([a[n].astype(BF16).reshape(DEPTH, PACK_ROWS[n], D) for n in BIG], axis=1)
    cdw = jnp.pad(a["conv_dw"].reshape(DEPTH * KSIZE, CW // 4), ((0, CDW_ROWS - DEPTH * KSIZE), (0, 0)))
    gathered, gcdw = _all_gather(flat, cdw)
    gathered = lax.dynamic_update_slice(gathered, flat[None], (chip, 0, 0, 0))
    gcdw = lax.dynamic_update_slice(gcdw, cdw[None], (chip, 0, 0))
    conv_dw = gcdw[:, :DEPTH * KSIZE].reshape(N_CHIPS, DEPTH, KSIZE, CW // 4).transpose(1, 2, 0, 3)
    conv_dw = jnp.pad(conv_dw.reshape(DEPTH, KSIZE, CW), ((0, 0), (0, 1), (0, 0)))
    layers = []
    for l in range(DEPTH):
        w = {"conv_dw": conv_dw[l]}
        at = 0
        for n in BIG:
            (s0, s1), axis = SHARD[n]
            blk = gathered[:, l, at:at + PACK_ROWS[n]].reshape(N_CHIPS, s0, s1)
            w[n] = blk.reshape(N_CHIPS * s0, s1) if axis == 0 else blk.transpose(1, 0, 2).reshape(s0, N_CHIPS * s1)
            at += PACK_ROWS[n]
        for n, _ in VEC_ROWS:
            w[n] = a[n][l][None, :]
        layers.append(w)

    loss_part, gx, grads, drel = _local_step(a["x"][0], a["mem"][0], target[0], a["rel_bias"], layers)
    loss = lax.psum(loss_part, ("x", "y", "c"))

    packed = []
    for l in range(DEPTH):
        parts = []
        for n in BIG:
            (s0, s1), axis = SHARD[n]
            g = grads[l][n]
            g = g.reshape(N_CHIPS, s0, s1) if axis == 0 else g.reshape(s0, N_CHIPS, s1).transpose(1, 0, 2)
            parts.append(g.reshape(N_CHIPS, PACK_ROWS[n], D))
        packed.append(jnp.concatenate(parts, axis=1).astype(BF16))
    packed = jnp.stack(packed)
    chip_sums = _add_own_layer(where, packed, _sibling_exchange(packed))
    own_layer = _sum_chips(where, chip_sums, _chip_exchange(chip_sums))
    other_layer = _sibling_share(own_layer)

    gvec = _all_reduce_small(_pack_vectors(
        lambda n: jnp.stack([grads[l][n][0] for l in range(DEPTH)]), drel[:, :3 * HPG],
        jnp.stack([grads[l]["conv_dw"] for l in range(DEPTH)])))
    lens = {n: a[n].shape[1] for n, _ in VEC_ROWS}
    g_vec, g_rel = _unpack_vectors(gvec, lens)
    g_cdw = lax.dynamic_slice_in_dim(gvec[CDW_ROW:CDW_ROW + CDW_GROWS].reshape(DEPTH, KSIZE, CW), chip * (CW // 4),
                                     CW // 4, axis=2)

    grad, delta, new_m, new_v = {}, {}, {}, {}
    at = 0
    for n in BIG:
        shape = a[n].shape
        mine, other = (t[at:at + PACK_ROWS[n]].reshape(shape[1:]) for t in (own_layer, other_layer))
        g = jnp.stack([jnp.where(ci == l, mine, other) for l in range(DEPTH)])
        at += PACK_ROWS[n]
        flat2 = lambda t: t.reshape(shape[0] * shape[1], shape[2])
        d, m2, v2 = _adamw(flat2(a[n]), flat2(g), flat2(mom[n]), flat2(var[n]), "adamw_" + n)
        grad[n], delta[n], new_m[n], new_v[n] = g, d.reshape(shape), m2.reshape(shape), v2.reshape(shape)
    shape = a["conv_dw"].shape
    flat2 = lambda t: t.reshape(shape[0] * shape[1], shape[2])
    d, m2, v2 = _adamw(flat2(a["conv_dw"]), flat2(g_cdw), flat2(mom["conv_dw"]), flat2(var["conv_dw"]), "adamw_conv_dw")
    grad["conv_dw"], delta["conv_dw"], new_m["conv_dw"], new_v["conv_dw"] = (
        g_cdw, d.reshape(shape), m2.reshape(shape), v2.reshape(shape))
    zero_cdw = jnp.zeros((DEPTH, KSIZE, CW), F32)
    pk = lambda src: _pack_vectors(lambda n: src[n], src["rel_bias"], zero_cdw)
    d, m2, v2 = _adamw(pk(a), gvec, pk(mom), pk(var), "adamw_vectors")
    for src, dst in ((d, delta), (m2, new_m), (v2, new_v)):
        vec, rel = _unpack_vectors(src, lens)
        dst.update(vec)
        dst["rel_bias"] = rel
    grad.update(g_vec)
    grad["rel_bias"] = g_rel

    outs = [loss, gx[None]]
    for group in (grad, delta, new_m, new_v):
        outs += [group[n] for n in WEIGHT_NAMES]
    return tuple(outs)
```

```python
import functools
import math

import jax
import jax.numpy as jnp
from jax import lax
from jax.experimental import pallas as pl
from jax.experimental.pallas import tpu as pltpu

F32 = jnp.float32
BF16 = jnp.bfloat16
I32 = jnp.int32

D = 1024
DEPTH = 2
N_MEM = 256
CW = 512
KSIZE = 31
PAD = KSIZE // 2
DILS = (1, 4, 16)
RADIUS = 64
HPG = 4
HD = 64
GW = HPG * HD
MH = 4
MHD = 128
MW = MH * MHD
FH = 2816
NIN = 6912
C1 = 2 * CW
C2 = C1 + 9 * GW
C3 = C2 + MW
NUM_BUCKETS = 32
MAX_DISTANCE = 1024
RMS_EPS = 1e-6
LN_EPS = 1e-5
NEG_INF = -1e30
ATT_SCALE = HD ** -0.5
MEM_SCALE = MHD ** -0.5

ADAM_LR = 0.001
ADAM_B1 = 0.9
ADAM_B2 = 0.999
ADAM_EPS = 1e-08
ADAM_WD = 0.01
ADAM_STEP = 10

VMEM_LIMIT_BYTES = 56 * 1024 * 1024
ATT_QB = 128
ATT_TB = 16 * ATT_QB

MESH = pl.DeviceIdType.MESH


def _params(sem=None):
    return pltpu.CompilerParams(dimension_semantics=sem, vmem_limit_bytes=VMEM_LIMIT_BYTES)


def _sigmoid(v):
    return 1.0 / (1.0 + jnp.exp(-v))


def _dot(a, b):
    return jnp.dot(a, b, preferred_element_type=F32)


def _dot_nt(a, b):
    return lax.dot_general(a, b, (((1,), (1,)), ((), ())), preferred_element_type=F32)


def _dot_tn(a, b):
    return lax.dot_general(a, b, (((0,), (0,)), ((), ())), preferred_element_type=F32)


def _rms_fwd_val(v, g):
    r = lax.rsqrt(jnp.mean(v * v, axis=-1, keepdims=True) + RMS_EPS)
    return v * r * g


def _rms_bwd_val(v, g, dy):
    r = lax.rsqrt(jnp.mean(v * v, axis=-1, keepdims=True) + RMS_EPS)
    vh = v * r
    dvh = dy * g
    dv = r * (dvh - vh * jnp.mean(dvh * vh, axis=-1, keepdims=True))
    return dv, dy * vh


def _row(i):
    return (i, 0)


def _fixed(*_):
    return (0, 0)


def _mm_nn(a, b, tm, tn, out_dtype, name):
    M, K = a.shape
    N = b.shape[1]

    def body(a_ref, b_ref, o_ref):
        o_ref[...] = _dot(a_ref[...], b_ref[...]).astype(out_dtype)

    return pl.pallas_call(
        body, name=name, grid=(N // tn, M // tm),
        in_specs=[pl.BlockSpec((tm, K), lambda j, i: (i, 0)), pl.BlockSpec((K, tn), lambda j, i: (0, j))],
        out_specs=pl.BlockSpec((tm, tn), lambda j, i: (i, j)),
        out_shape=jax.ShapeDtypeStruct((M, N), out_dtype),
        compiler_params=_params(("parallel", "parallel")),
    )(a, b)


def _mm_nt(a, b, tm, tc, out_dtype, name):
    M, N = a.shape
    K = b.shape[0]
    nk = N // tc

    def body(a_ref, b_ref, o_ref, acc_ref):
        k = pl.program_id(1)

        @pl.when(k == 0)
        def _():
            acc_ref[...] = jnp.zeros_like(acc_ref)

        acc_ref[...] += _dot_nt(a_ref[...], b_ref[...])

        @pl.when(k == nk - 1)
        def _():
            o_ref[...] = acc_ref[...].astype(out_dtype)

    return pl.pallas_call(
        body, name=name, grid=(M // tm, nk),
        in_specs=[pl.BlockSpec((tm, tc), lambda i, k: (i, k)), pl.BlockSpec((K, tc), lambda i, k: (0, k))],
        out_specs=pl.BlockSpec((tm, K), lambda i, k: (i, 0)),
        out_shape=jax.ShapeDtypeStruct((M, K), out_dtype),
        scratch_shapes=[pltpu.VMEM((tm, K), F32)],
        compiler_params=_params(("parallel", "arbitrary")),
    )(a, b)


def _mm_tn(a, b, ts, tn, name):
    S, K = a.shape
    N = b.shape[1]

    def body(a_ref, b_ref, o_ref):
        @pl.when(pl.program_id(1) == 0)
        def _():
            o_ref[...] = jnp.zeros_like(o_ref)

        o_ref[...] += _dot_tn(a_ref[...], b_ref[...])

    return pl.pallas_call(
        body, name=name, grid=(N // tn, S // ts),
        in_specs=[pl.BlockSpec((ts, K), lambda j, s: (s, 0)), pl.BlockSpec((ts, tn), lambda j, s: (s, j))],
        out_specs=pl.BlockSpec((K, tn), lambda j, s: (0, j)),
        out_shape=jax.ShapeDtypeStruct((K, N), F32),
        compiler_params=_params(("parallel", "arbitrary")),
    )(a, b)


def _rms_h(x, g, name):
    S = x.shape[0]
    T = 512

    def body(x_ref, g_ref, h_ref):
        h_ref[...] = _rms_fwd_val(x_ref[...], g_ref[...]).astype(BF16)

    return pl.pallas_call(
        body, name=name, grid=(S // T,),
        in_specs=[pl.BlockSpec((T, D), _row), pl.BlockSpec((1, D), _fixed)],
        out_specs=pl.BlockSpec((T, D), _row),
        out_shape=jax.ShapeDtypeStruct((S, D), BF16),
        compiler_params=_params(("parallel",)),
    )(x, g)


def _rms_bwd(x, g, dh, dres, name):
    S = x.shape[0]
    T = 512

    def body(x_ref, g_ref, dh_ref, dres_ref, dx_ref, dg_ref):
        dv, dgr = _rms_bwd_val(x_ref[...], g_ref[...], dh_ref[...].astype(F32))
        dx_ref[...] = dres_ref[...] + dv

        @pl.when(pl.program_id(0) == 0)
        def _():
            dg_ref[...] = jnp.zeros_like(dg_ref)

        dg_ref[...] += jnp.sum(dgr, axis=0, keepdims=True)

    return pl.pallas_call(
        body, name=name, grid=(S // T,),
        in_specs=[pl.BlockSpec((T, D), _row), pl.BlockSpec((1, D), _fixed), pl.BlockSpec((T, D), _row),
                  pl.BlockSpec((T, D), _row)],
        out_specs=[pl.BlockSpec((T, D), _row), pl.BlockSpec((1, D), _fixed)],
        out_shape=[jax.ShapeDtypeStruct((S, D), F32), jax.ShapeDtypeStruct((1, D), F32)],
        compiler_params=_params(("arbitrary",)),
    )(x, g, dh, dres)


CONV_T = 256
CONV_HALO = 16
CONV_RC = 32


def _halo_specs(T, halo, S, width, col):
    per = T // halo
    last = S // halo - 1
    return [
        pl.BlockSpec((T, width), lambda i: (i, col)),
        pl.BlockSpec((halo, width), lambda i: (jnp.maximum(i * per - 1, 0), col)),
        pl.BlockSpec((halo, width), lambda i: (jnp.minimum((i + 1) * per, last), col)),
    ]


def _glu(zb):
    zb = zb.astype(F32)
    return zb[:, :CW] * _sigmoid(zb[:, CW:])


CONV_EXT = CONV_T + 2 * CONV_HALO
SUBLANES = 8


def _fill_shifted(sh_ref, ext_ref, cur, prev, nxt):
    T, halo = CONV_T, CONV_HALO
    i = pl.program_id(0)
    n = pl.num_programs(0)
    ext_ref[0:halo, :] = jnp.where(i > 0, prev, 0.0)
    ext_ref[halo:halo + T, :] = cur
    ext_ref[halo + T:CONV_EXT, :] = jnp.where(i < n - 1, nxt, 0.0)
    ext_ref[CONV_EXT:CONV_EXT + SUBLANES, :] = jnp.zeros((SUBLANES, CW), F32)
    for b in range(SUBLANES):
        sh_ref[b] = ext_ref[b:b + CONV_EXT, :]


def _window(sh_ref, start, rows):
    b = start % SUBLANES
    return sh_ref[b, start - b:start - b + rows, :]


def _shifted_scratch():
    return [pltpu.VMEM((CONV_EXT + SUBLANES, CW), F32), pltpu.VMEM((SUBLANES, CONV_EXT, CW), F32)]


def _conv_fwd(z, wdw, bdw, lng, lnb, name):
    S = z.shape[0]
    T, HL, RC = CONV_T, CONV_HALO, CONV_RC

    def body(cur_ref, prev_ref, next_ref, w_ref, b_ref, g_ref, bb_ref, yc_ref, act_ref, ext_ref, sh_ref):
        _fill_shifted(sh_ref, ext_ref, _glu(cur_ref[...]), _glu(prev_ref[...]), _glu(next_ref[...]))
        for c in range(T // RC):
            acc = jnp.zeros((RC, CW), F32)
            for k in range(KSIZE):
                acc = acc + w_ref[k:k + 1, :] * _window(sh_ref, c * RC + k + HL - PAD, RC)
            yc = acc + b_ref[...]
            yc_ref[c * RC:(c + 1) * RC, :] = yc
            mu = jnp.mean(yc, axis=-1, keepdims=True)
            xc = yc - mu
            ln = xc * lax.rsqrt(jnp.mean(xc * xc, axis=-1, keepdims=True) + LN_EPS) * g_ref[...] + bb_ref[...]
            act_ref[c * RC:(c + 1) * RC, :] = (ln * _sigmoid(ln)).astype(BF16)

    return pl.pallas_call(
        body, name=name, grid=(S // T,),
        in_specs=_halo_specs(T, HL, S, C1, 0) + [pl.BlockSpec((32, CW), _fixed)] + [pl.BlockSpec((1, CW), _fixed)] * 3,
        out_specs=[pl.BlockSpec((T, CW), _row), pl.BlockSpec((T, CW), _row)],
        out_shape=[jax.ShapeDtypeStruct((S, CW), F32), jax.ShapeDtypeStruct((S, CW), BF16)],
        scratch_shapes=_shifted_scratch(),
        compiler_params=_params(("parallel",)),
    )(z, z, z, wdw, bdw, lng, lnb)


def _conv_bwd_ln(yc, dact, lng, lnb, name):
    S = yc.shape[0]
    T = 512

    def body(yc_ref, da_ref, g_ref, b_ref, dyc_ref, dg_ref, db_ref, dbias_ref):
        yc_v = yc_ref[...]
        mu = jnp.mean(yc_v, axis=-1, keepdims=True)
        xc = yc_v - mu
        r = lax.rsqrt(jnp.mean(xc * xc, axis=-1, keepdims=True) + LN_EPS)
        yn = xc * r
        ln = yn * g_ref[...] + b_ref[...]
        sg = _sigmoid(ln)
        dln = da_ref[...].astype(F32) * (sg * (1.0 + ln * (1.0 - sg)))
        dyn = dln * g_ref[...]
        dyc = r * (dyn - jnp.mean(dyn, axis=-1, keepdims=True) - yn * jnp.mean(dyn * yn, axis=-1, keepdims=True))
        dyc_ref[...] = dyc

        @pl.when(pl.program_id(0) == 0)
        def _():
            dg_ref[...] = jnp.zeros_like(dg_ref)
            db_ref[...] = jnp.zeros_like(db_ref)
            dbias_ref[...] = jnp.zeros_like(dbias_ref)

        dg_ref[...] += jnp.sum(dln * yn, axis=0, keepdims=True)
        db_ref[...] += jnp.sum(dln, axis=0, keepdims=True)
        dbias_ref[...] += jnp.sum(dyc, axis=0, keepdims=True)

    vec = pl.BlockSpec((1, CW), _fixed)
    return pl.pallas_call(
        body, name=name, grid=(S // T,),
        in_specs=[pl.BlockSpec((T, CW), _row), pl.BlockSpec((T, CW), _row), vec, vec],
        out_specs=[pl.BlockSpec((T, CW), _row), vec, vec, vec],
        out_shape=[jax.ShapeDtypeStruct((S, CW), F32)] + [jax.ShapeDtypeStruct((1, CW), F32)] * 3,
        compiler_params=_params(("arbitrary",)),
    )(yc, dact, lng, lnb)


def _conv_bwd_dw(z, dyc, wdw, name):
    S = z.shape[0]
    T, HL, RC = CONV_T, CONV_HALO, CONV_RC

    def body(zc_ref, zp_ref, zn_ref, dc_ref, dp_ref, dn_ref, w_ref, dz_ref, dw_ref, uext_ref, ush_ref, dext_ref,
             dsh_ref, dwacc_ref):
        _fill_shifted(ush_ref, uext_ref, _glu(zc_ref[...]), _glu(zp_ref[...]), _glu(zn_ref[...]))
        _fill_shifted(dsh_ref, dext_ref, dc_ref[...], dp_ref[...], dn_ref[...])

        @pl.when(pl.program_id(0) == 0)
        def _():
            dwacc_ref[...] = jnp.zeros_like(dwacc_ref)

        for c in range(T // RC):
            dcur = dc_ref[c * RC:(c + 1) * RC, :]
            du = jnp.zeros((RC, CW), F32)
            for k in range(KSIZE):
                du = du + w_ref[k:k + 1, :] * _window(dsh_ref, c * RC + HL + PAD - k, RC)
                prod = dcur * _window(ush_ref, c * RC + k + HL - PAD, RC)
                dwacc_ref[k] += jnp.sum(prod.reshape(RC // SUBLANES, SUBLANES, CW), axis=0)
            zc = zc_ref[c * RC:(c + 1) * RC, :].astype(F32)
            a, gt = zc[:, :CW], zc[:, CW:]
            sg = _sigmoid(gt)
            dz_ref[c * RC:(c + 1) * RC, 0:CW] = (du * sg).astype(BF16)
            dz_ref[c * RC:(c + 1) * RC, CW:C1] = (du * a * sg * (1.0 - sg)).astype(BF16)

        @pl.when(pl.program_id(0) == pl.num_programs(0) - 1)
        def _():
            dw_ref[...] = jnp.sum(dwacc_ref[...], axis=1)

    return pl.pallas_call(
        body, name=name, grid=(S // T,),
        in_specs=_halo_specs(T, HL, S, C1, 0) + _halo_specs(T, HL, S, CW, 0) + [pl.BlockSpec((32, CW), _fixed)],
        out_specs=[pl.BlockSpec((T, C1), _row), pl.BlockSpec((32, CW), _fixed)],
        out_shape=[jax.ShapeDtypeStruct((S, C1), BF16), jax.ShapeDtypeStruct((32, CW), F32)],
        scratch_shapes=_shifted_scratch() + _shifted_scratch() + [pltpu.VMEM((32, SUBLANES, CW), F32)],
        compiler_params=_params(("arbitrary",)),
    )(z, z, z, dyc, dyc, dyc, wdw)


def _t5_bucket(rel):
    nb = NUM_BUCKETS // 2
    max_exact = nb // 2
    ret = jnp.where(rel > 0, nb, 0)
    n = jnp.abs(rel)
    nf = jnp.maximum(n, 1).astype(F32)
    large = max_exact + (jnp.log(nf / max_exact) / math.log(MAX_DISTANCE / max_exact)
                         * (nb - max_exact)).astype(I32)
    large = jnp.minimum(large, nb - 1)
    return ret + jnp.where(n < max_exact, n, large)


def _offsets_qk(nq, nk, shift):
    return lax.broadcasted_iota(I32, (nq, nk), 1) + shift - lax.broadcasted_iota(I32, (nq, nk), 0)


def _bias_table(bk, rb_ref, col, off):
    acc = jnp.zeros(bk.shape, F32)
    for b in range(NUM_BUCKETS):
        acc = jnp.where(bk == b, rb_ref[b, col], acc)
    return jnp.where(jnp.abs(off) <= RADIUS, acc, NEG_INF)


def _to_halves(scr, row0, val):
    rows = val.shape[0]
    v = val.astype(F32)
    scr[0, row0:row0 + rows, :] = v[:, :128]
    scr[1, row0:row0 + rows, :] = v[:, 128:]


def _heads(halves):
    return [halves[h // 2][:, (h % 2) * HD:(h % 2 + 1) * HD] for h in range(HPG)]


def _join_heads(parts):
    return [jnp.concatenate(parts[0:2], axis=-1), jnp.concatenate(parts[2:4], axis=-1)]


ATT_FWD_GROUP = 2
ATT_BWD_GROUP = 1


def _att_units(d, fn, group):
    nj = ATT_TB // (ATT_QB * d)
    if nj == 1:
        def trip(t, c):
            r0 = pl.multiple_of(t * 8, 8)
            for u in range(0, 8, group):
                fn([(r0 + u + v, 0) for v in range(group)])
            return c

        lax.fori_loop(0, d // 8, trip, 0)
        return
    for r in range(d):
        def step(t, c, r=r):
            fn([(r, t * group + u) for u in range(group)])
            return c

        lax.fori_loop(0, nj // group, step, 0)


def _unit_row(r, j, d):
    if isinstance(j, int):
        return j * ATT_QB * d + r
    return pl.multiple_of(j * (ATT_QB * d), ATT_QB) + r


def _att_fwd(z, rel_bias, g, name):
    S = z.shape[0]
    d = DILS[g]
    TB, QB = ATT_TB, ATT_QB
    H = RADIUS * d
    L = S // d
    cq, ck, cv = (C1 + g * GW) // GW, (C1 + 3 * GW + g * GW) // GW, (C1 + 6 * GW + g * GW) // GW
    bk = _t5_bucket(_offsets_qk(QB, 2 * QB, -RADIUS) * d)

    def body(rb_ref, bk_ref, q_ref, kc_ref, kp_ref, kn_ref, vc_ref, vp_ref, vn_ref, o_ref, l_ref,
             qs, ks, vs, os_, ls, bias):
        i = pl.program_id(0)

        @pl.when(i == 0)
        def _():
            off = _offsets_qk(QB, 2 * QB, -RADIUS)
            for h in range(HPG):
                bias[h] = _bias_table(bk_ref[...], rb_ref, g * HPG + h, off)

        _to_halves(qs, 0, q_ref[...].astype(F32) * ATT_SCALE)
        for scr, p_ref, c_ref, n_ref in ((ks, kp_ref, kc_ref, kn_ref), (vs, vp_ref, vc_ref, vn_ref)):
            _to_halves(scr, 0, p_ref[...])
            _to_halves(scr, H, c_ref[...])
            _to_halves(scr, H + TB, n_ref[...])

        lo = lax.broadcasted_iota(I32, (QB, 128), 1) < HD

        def units(rjs):
            work = []
            for r, j in rjs:
                row = _unit_row(r, j, d)
                km = lax.broadcasted_iota(I32, (1, 2 * QB), 1) + (i * (TB // d) + j * QB - RADIUS)
                edge = jnp.where(jnp.where(km >= 0, km, L) < L, 0.0, NEG_INF)
                for hf in (0, 1):
                    q2 = qs[hf, pl.ds(row, QB, stride=d), :]
                    k2 = ks[hf, pl.ds(row, 2 * QB, stride=d), :].astype(BF16)
                    v2 = vs[hf, pl.ds(row, 2 * QB, stride=d), :].astype(BF16)
                    qq = jnp.concatenate([jnp.where(lo, q2, 0.0), jnp.where(lo, 0.0, q2)], axis=0).astype(BF16)
                    work.append((row, hf, edge, k2, v2, qq))
            scores = [_dot_nt(qq, k2) for (_, _, _, k2, _, qq) in work]
            probs = []
            for (row, hf, edge, *_), ss in zip(work, scores):
                es, stats = [], []
                for hh in (0, 1):
                    s = ss[hh * QB:(hh + 1) * QB] + bias[2 * hf + hh] + edge
                    m = jnp.max(s, axis=-1, keepdims=True)
                    e = jnp.exp(s - m)
                    den = jnp.sum(e, axis=-1, keepdims=True)
                    es.append(e.astype(BF16))
                    stats.append((1.0 / den, m + jnp.log(den)))
                probs.append((jnp.concatenate(es, axis=0), stats))
            for (row, hf, _, _, v2, _), (ee, stats) in zip(work, probs):
                oo = _dot(ee, v2)
                os_[hf, pl.ds(row, QB, stride=d), :] = jnp.where(lo, oo[:QB] * stats[0][0], oo[QB:] * stats[1][0])
                ls[hf, pl.ds(row, QB, stride=d), :] = jnp.where(lo, stats[0][1], stats[1][1])

        _att_units(d, units, ATT_FWD_GROUP)
        for hf in (0, 1):
            o_ref[:, hf * 128:(hf + 1) * 128] = os_[hf].astype(BF16)
            l_ref[:, hf * 128:(hf + 1) * 128] = ls[hf]

    def halo3(col):
        c, p, n = _halo_specs(TB, H, S, GW, col)
        return [c, p, n]

    return pl.pallas_call(
        body, name=name, grid=(S // TB,),
        in_specs=[pl.BlockSpec(memory_space=pltpu.SMEM), pl.BlockSpec((QB, 2 * QB), _fixed),
                  pl.BlockSpec((TB, GW), lambda i: (i, cq))] + halo3(ck) + halo3(cv),
        out_specs=[pl.BlockSpec((TB, GW), _row), pl.BlockSpec((TB, GW), _row)],
        out_shape=[jax.ShapeDtypeStruct((S, GW), BF16), jax.ShapeDtypeStruct((S, GW), F32)],
        scratch_shapes=[pltpu.VMEM((2, TB, 128), F32), pltpu.VMEM((2, TB + 2 * H, 128), F32),
                        pltpu.VMEM((2, TB + 2 * H, 128), F32), pltpu.VMEM((2, TB, 128), F32),
                        pltpu.VMEM((2, TB, 128), F32), pltpu.VMEM((HPG, QB, 2 * QB), F32)],
        compiler_params=_params(("arbitrary",)),
    )(rel_bias, bk, z, z, z, z, z, z, z)


def _att_combine(os3, ls3, name):
    S = os3[0].shape[0]
    T = 1024

    def body(o1, o2, o3, l1, l2, l3, o_ref, l_ref):
        lv = [l1[...], l2[...], l3[...]]
        m = jnp.maximum(jnp.maximum(lv[0], lv[1]), lv[2])
        e = [jnp.exp(v - m) for v in lv]
        den = e[0] + e[1] + e[2]
        acc = jnp.zeros_like(m)
        for ev, o in zip(e, (o1, o2, o3)):
            acc = acc + (ev / den) * o[...].astype(F32)
        o_ref[...] = acc.astype(BF16)
        l_ref[...] = m + jnp.log(den)

    blk = pl.BlockSpec((T, GW), _row)
    return pl.pallas_call(
        body, name=name, grid=(S // T,), in_specs=[blk] * 6, out_specs=[blk, blk],
        out_shape=[jax.ShapeDtypeStruct((S, GW), BF16), jax.ShapeDtypeStruct((S, GW), F32)],
        compiler_params=_params(("parallel",)),
    )(*os3, *ls3)


def _att_prep(do, o, lse, name):
    S = do.shape[0]
    T = 1024

    def body(do_ref, o_ref, l_ref, out_ref):
        prod = do_ref[...].astype(F32) * o_ref[...].astype(F32)
        dd = [jnp.broadcast_to(jnp.sum(prod[:, h * HD:(h + 1) * HD], axis=-1, keepdims=True), (T, HD))
              for h in range(HPG)]
        lane = lax.broadcasted_iota(I32, (T, GW), 1)
        out_ref[...] = jnp.where(lane % HD < HD // 2, l_ref[...], jnp.concatenate(dd, axis=-1))

    blk = pl.BlockSpec((T, GW), _row)
    return pl.pallas_call(
        body, name=name, grid=(S // T,), in_specs=[blk] * 3, out_specs=blk,
        out_shape=jax.ShapeDtypeStruct((S, GW), F32), compiler_params=_params(("parallel",)),
    )(do, o, lse)


def _att_bwd(z, rel_bias, do, ld, g, name):
    S = z.shape[0]
    d = DILS[g]
    TB, QB = ATT_TB, ATT_QB
    H = RADIUS * d
    L = S // d
    E = TB + 2 * H
    cq, ck, cv = (C1 + g * GW) // GW, (C1 + 3 * GW + g * GW) // GW, (C1 + 6 * GW + g * GW) // GW
    bk_a = _t5_bucket(_offsets_qk(QB, 2 * QB, -RADIUS) * d)
    bk_b = _t5_bucket(-_offsets_qk(QB, 2 * QB, -RADIUS) * d)

    def body(rb_ref, bka_ref, bkb_ref, *refs):
        ins, (dq_ref, dk_ref, dv_ref, db_ref) = refs[:15], refs[15:19]
        qs, ks, vs, dos, ls, dqs, dks, dvs, bias_a, bias_b, dbias = refs[19:]
        i = pl.program_id(0)
        n = pl.num_programs(0)

        @pl.when(i == 0)
        def _():
            off = _offsets_qk(QB, 2 * QB, -RADIUS)
            for h in range(HPG):
                bias_a[h] = _bias_table(bka_ref[...], rb_ref, g * HPG + h, off)
                bias_b[h] = _bias_table(bkb_ref[...], rb_ref, g * HPG + h, off)
            dbias[...] = jnp.zeros_like(dbias)

        for a, scr in enumerate((qs, ks, vs, dos, ls)):
            c_ref, p_ref, n_ref = ins[3 * a:3 * a + 3]
            pre = (lambda v: v.astype(F32) * ATT_SCALE) if a == 0 else (lambda v: v)
            _to_halves(scr, 0, pre(p_ref[...]))
            _to_halves(scr, H, pre(c_ref[...]))
            _to_halves(scr, H + TB, pre(n_ref[...]))

        lo = lax.broadcasted_iota(I32, (QB, 128), 1) < HD

        def split(v):
            return jnp.concatenate([jnp.where(lo, v, 0.0), jnp.where(lo, 0.0, v)], axis=0).astype(BF16)

        def halves(v):
            return v[:QB], v[QB:]

        def units(rjs):
            work = []
            for r, j in rjs:
                row = _unit_row(r, j, d)
                cur = row + H
                m0 = i * (TB // d) + j * QB - RADIUS
                km = lax.broadcasted_iota(I32, (1, 2 * QB), 1) + m0
                edge_a = jnp.where(jnp.where(km >= 0, km, L) < L, 0.0, NEG_INF)
                for hf in (0, 1):
                    ld = lambda scr, at, nrow: scr[hf, pl.ds(at, nrow, stride=d), :]
                    w = dict(row=row, hf=hf, edge=edge_a, l_c=ld(ls, cur, QB), l_t=ld(ls, row, 2 * QB).T)
                    for nm, scr in (("q", qs), ("k", ks), ("v", vs), ("do", dos)):
                        w[nm + "_c"] = split(ld(scr, cur, QB))
                        w[nm + "_e"] = ld(scr, row, 2 * QB).astype(BF16)
                    work.append(w)
            for w in work:
                w["s"] = halves(_dot_nt(w["q_c"], w["k_e"]))
                w["dp"] = halves(_dot_nt(w["do_c"], w["v_e"]))
                w["s2"] = halves(_dot_nt(w["k_c"], w["q_e"]))
                w["dp2"] = halves(_dot_nt(w["v_c"], w["do_e"]))
            for w in work:
                w["ds"], w["p2"], w["ds2"] = [], [], []
                for hh in (0, 1):
                    h, c0 = 2 * w["hf"] + hh, HD * hh
                    l_c, l_t = w["l_c"], w["l_t"]
                    p = jnp.exp(w["s"][hh] + bias_a[h] + w["edge"] - l_c[:, c0:c0 + 1])
                    ds = p * (w["dp"][hh] - l_c[:, c0 + HD // 2:c0 + HD // 2 + 1])
                    dbias[h] += ds
                    p2 = jnp.exp(w["s2"][hh] + bias_b[h] + w["edge"] - l_t[c0:c0 + 1, :])
                    ds2 = p2 * (w["dp2"][hh] - l_t[c0 + HD // 2:c0 + HD // 2 + 1, :])
                    w["ds"].append(ds.astype(BF16))
                    w["p2"].append(p2.astype(BF16))
                    w["ds2"].append(ds2.astype(BF16))
            for w in work:
                at = pl.ds(w["row"], QB, stride=d)
                both = lambda pair, rhs: halves(_dot(jnp.concatenate(pair, axis=0), rhs))
                dq = both(w["ds"], w["k_e"])
                dqs[w["hf"], at, :] = jnp.where(lo, dq[0], dq[1]) * ATT_SCALE
                dv = both(w["p2"], w["do_e"])
                dvs[w["hf"], at, :] = jnp.where(lo, dv[0], dv[1])
                dk = both(w["ds2"], w["q_e"])
                dks[w["hf"], at, :] = jnp.where(lo, dk[0], dk[1])

        _att_units(d, units, ATT_BWD_GROUP)
        for scr, ref in ((dqs, dq_ref), (dks, dk_ref), (dvs, dv_ref)):
            for hf in (0, 1):
                ref[:, hf * 128:(hf + 1) * 128] = scr[hf].astype(BF16)

        @pl.when(i == n - 1)
        def _():
            rows = lax.broadcasted_iota(I32, (NUM_BUCKETS, 128), 0)
            lanes = lax.broadcasted_iota(I32, (NUM_BUCKETS, 128), 1)
            out = jnp.zeros((NUM_BUCKETS, 128), F32)
            bk = bka_ref[...]
            for h in range(HPG):
                acc = dbias[h]
                for b in range(NUM_BUCKETS):
                    tot = jnp.sum(jnp.sum(jnp.where(bk == b, acc, 0.0), axis=1, keepdims=True), axis=0, keepdims=True)
                    out = out + jnp.where((rows == b) & (lanes == h), tot, 0.0)
            db_ref[...] = out

    def halo3(col, width=GW):
        return _halo_specs(TB, H, S, width, col)

    one = pl.Buffered(1)

    def single(specs):
        return [pl.BlockSpec(s.block_shape, s.index_map, pipeline_mode=one) for s in specs]

    in_specs = ([pl.BlockSpec(memory_space=pltpu.SMEM), pl.BlockSpec((QB, 2 * QB), _fixed),
                 pl.BlockSpec((QB, 2 * QB), _fixed)]
                + single(halo3(cq) + halo3(ck) + halo3(cv) + halo3(0) + halo3(0)))
    blk = pl.BlockSpec((TB, GW), _row)
    return pl.pallas_call(
        body, name=name, grid=(S // TB,), in_specs=in_specs,
        out_specs=[blk, blk, blk, pl.BlockSpec((NUM_BUCKETS, 128), _fixed)],
        out_shape=[jax.ShapeDtypeStruct((S, GW), BF16)] * 3 + [jax.ShapeDtypeStruct((NUM_BUCKETS, 128), F32)],
        scratch_shapes=[pltpu.VMEM((2, E, 128), F32)] * 5 + [pltpu.VMEM((2, TB, 128), F32)] * 3
        + [pltpu.VMEM((HPG, QB, 2 * QB), F32)] * 3,
        compiler_params=_params(("arbitrary",)),
    )(rel_bias, bk_a, bk_b, z, z, z, z, z, z, z, z, z, do, do, do, ld, ld, ld)


def _memkv_fwd(mem, gm, wkv, name):
    def body(m_ref, g_ref, w_ref, hm_ref, kv_ref):
        hm = _rms_fwd_val(m_ref[...], g_ref[...]).astype(BF16)
        hm_ref[...] = hm
        kv_ref[...] = _dot(hm, w_ref[...]).astype(BF16)

    return pl.pallas_call(
        body, name=name,
        out_shape=[jax.ShapeDtypeStruct((N_MEM, D), BF16), jax.ShapeDtypeStruct((N_MEM, 2 * MW), BF16)],
        compiler_params=_params(),
    )(mem, gm, wkv)


def _memkv_bwd(mem, gm, hm, wkv, dkv, name):
    def body(m_ref, g_ref, hm_ref, w_ref, dkv_ref, dw_ref, dg_ref):
        dkv_b = dkv_ref[...].astype(BF16)
        dw_ref[...] = _dot_tn(hm_ref[...], dkv_b)
        dhm = _dot_nt(dkv_b, w_ref[...])
        _, dgr = _rms_bwd_val(m_ref[...], g_ref[...], dhm)
        dg_ref[...] = jnp.sum(dgr, axis=0, keepdims=True)

    return pl.pallas_call(
        body, name=name,
        out_shape=[jax.ShapeDtypeStruct((D, 2 * MW), F32), jax.ShapeDtypeStruct((1, D), F32)],
        compiler_params=_params(),
    )(mem, gm, hm, wkv, dkv)


MEM_T = 512


def _mem_q_specs():
    return [pl.BlockSpec((MEM_T, MHD), lambda i, h=h: (i, C2 // MHD + h)) for h in range(MH)]


def _memattn_fwd(z, kv, name):
    S = z.shape[0]
    T = MEM_T

    def body(q0, q1, q2, q3, kv_ref, o_ref):
        for h, q_ref in enumerate((q0, q1, q2, q3)):
            kh = kv_ref[:, h * MHD:(h + 1) * MHD]
            vh = kv_ref[:, MW + h * MHD:MW + (h + 1) * MHD]
            s = _dot_nt(q_ref[...], kh) * MEM_SCALE
            e = jnp.exp(s - jnp.max(s, axis=-1, keepdims=True))
            p = e / jnp.sum(e, axis=-1, keepdims=True)
            o_ref[:, h * MHD:(h + 1) * MHD] = _dot(p.astype(BF16), vh).astype(BF16)

    return pl.pallas_call(
        body, name=name, grid=(S // T,),
        in_specs=_mem_q_specs() + [pl.BlockSpec((N_MEM, 2 * MW), _fixed)],
        out_specs=pl.BlockSpec((T, MW), _row),
        out_shape=jax.ShapeDtypeStruct((S, MW), BF16),
        compiler_params=_params(("parallel",)),
    )(z, z, z, z, kv)


def _memattn_bwd(z, kv, dom, name):
    S = z.shape[0]
    T = MEM_T

    def body(q0, q1, q2, q3, kv_ref, do_ref, dq_ref, dkv_ref):
        @pl.when(pl.program_id(0) == 0)
        def _():
            dkv_ref[...] = jnp.zeros_like(dkv_ref)

        for h, q_ref in enumerate((q0, q1, q2, q3)):
            kh = kv_ref[:, h * MHD:(h + 1) * MHD]
            vh = kv_ref[:, MW + h * MHD:MW + (h + 1) * MHD]
            qh = q_ref[...]
            doh = do_ref[:, h * MHD:(h + 1) * MHD]
            s = _dot_nt(qh, kh) * MEM_SCALE
            e = jnp.exp(s - jnp.max(s, axis=-1, keepdims=True))
            p = e / jnp.sum(e, axis=-1, keepdims=True)
            dkv_ref[:, MW + h * MHD:MW + (h + 1) * MHD] += _dot_tn(p.astype(BF16), doh)
            dp = _dot_nt(doh, vh)
            ds = (p * (dp - jnp.sum(dp * p, axis=-1, keepdims=True))).astype(BF16)
            dq_ref[:, h * MHD:(h + 1) * MHD] = (_dot(ds, kh) * MEM_SCALE).astype(BF16)
            dkv_ref[:, h * MHD:(h + 1) * MHD] += _dot_tn(ds, qh) * MEM_SCALE

    return pl.pallas_call(
        body, name=name, grid=(S // T,),
        in_specs=_mem_q_specs() + [pl.BlockSpec((N_MEM, 2 * MW), _fixed), pl.BlockSpec((T, MW), _row)],
        out_specs=[pl.BlockSpec((T, MW), _row), pl.BlockSpec((N_MEM, 2 * MW), _fixed)],
        out_shape=[jax.ShapeDtypeStruct((S, MW), BF16), jax.ShapeDtypeStruct((N_MEM, 2 * MW), F32)],
        compiler_params=_params(("arbitrary",)),
    )(z, z, z, z, kv, dom)


MERGE_T = 256
GATE_BLK = 768


def _gate_specs(T):
    return [pl.BlockSpec((T, GATE_BLK), lambda i, b=b: (i, C3 // GATE_BLK + b)) for b in range(3 * D // GATE_BLK)]


def _branches(ca_ref, oa_ref, om_ref, wco_ref, wao_ref, wmo_ref, gate_refs, bg_ref):
    ys = [_dot(ca_ref[...], wco_ref[...]), _dot(oa_ref[...], wao_ref[...]), _dot(om_ref[...], wmo_ref[...])]
    zg = jnp.concatenate([r[...] for r in gate_refs], axis=-1).astype(F32) + bg_ref[...]
    gs = [_sigmoid(zg[:, b * D:(b + 1) * D]) for b in range(3)]
    return ys, gs


def _merge_fwd(x, cact, oatt, om, z, wco, wao, wmo, wout, bgate, gpost, name):
    S = x.shape[0]
    T = MERGE_T

    def body(x_ref, ca_ref, oa_ref, om_ref, g0, g1, g2, g3, wco_ref, wao_ref, wmo_ref, wout_ref, bg_ref, gp_ref,
             x1_ref, mg_ref, t_ref):
        ys, gs = _branches(ca_ref, oa_ref, om_ref, wco_ref, wao_ref, wmo_ref, (g0, g1, g2, g3), bg_ref)
        mb = (gs[0] * ys[0] + gs[1] * ys[1] + gs[2] * ys[2]).astype(BF16)
        t = _dot(mb, wout_ref[...])
        mg_ref[...] = mb
        t_ref[...] = t
        x1_ref[...] = x_ref[...] + _rms_fwd_val(t, gp_ref[...])

    full = lambda a: pl.BlockSpec(a.shape, _fixed)
    return pl.pallas_call(
        body, name=name, grid=(S // T,),
        in_specs=[pl.BlockSpec((T, D), _row), pl.BlockSpec((T, CW), _row), pl.BlockSpec((T, GW), _row),
                  pl.BlockSpec((T, MW), _row)] + _gate_specs(T)
        + [full(wco), full(wao), full(wmo), full(wout), full(bgate), full(gpost)],
        out_specs=[pl.BlockSpec((T, D), _row)] * 3,
        out_shape=[jax.ShapeDtypeStruct((S, D), F32), jax.ShapeDtypeStruct((S, D), BF16), jax.ShapeDtypeStruct((S, D), F32)],
        compiler_params=_params(("parallel",)),
    )(x, cact, oatt, om, z, z, z, z, wco, wao, wmo, wout, bgate, gpost)


def _merge_bwd(dx1, t, mg, cact, oatt, om, z, wco, wao, wmo, wout, bgate, gpost, name):
    S = dx1.shape[0]
    T = MERGE_T

    def body(dx_ref, t_ref, mg_ref, ca_ref, oa_ref, om_ref, g0, g1, g2, g3, wco_ref, wao_ref, wmo_ref, wout_ref,
             bg_ref, gp_ref, dzg_ref, dca_ref, doa_ref, dom_ref, dwco_ref, dwao_ref, dwmo_ref, dwout_ref,
             dbg_ref, dgp_ref):
        accs = (dwco_ref, dwao_ref, dwmo_ref, dwout_ref, dbg_ref, dgp_ref)

        @pl.when(pl.program_id(0) == 0)
        def _():
            for a in accs:
                a[...] = jnp.zeros_like(a)

        dt, dgr = _rms_bwd_val(t_ref[...], gp_ref[...], dx_ref[...])
        dgp_ref[...] += jnp.sum(dgr, axis=0, keepdims=True)
        dtb = dt.astype(BF16)
        dwout_ref[...] += _dot_tn(mg_ref[...], dtb)
        dm = _dot_nt(dtb, wout_ref[...])
        ys, gs = _branches(ca_ref, oa_ref, om_ref, wco_ref, wao_ref, wmo_ref, (g0, g1, g2, g3), bg_ref)
        for b, (act_ref, w_ref, dw_ref, da_ref) in enumerate(
                ((ca_ref, wco_ref, dwco_ref, dca_ref), (oa_ref, wao_ref, dwao_ref, doa_ref),
                 (om_ref, wmo_ref, dwmo_ref, dom_ref))):
            dzg = dm * ys[b] * gs[b] * (1.0 - gs[b])
            dzg_ref[:, b * D:(b + 1) * D] = dzg.astype(BF16)
            dbg_ref[:, b * D:(b + 1) * D] += jnp.sum(dzg, axis=0, keepdims=True)
            dy = (dm * gs[b]).astype(BF16)
            dw_ref[...] += _dot_tn(act_ref[...], dy)
            da_ref[...] = _dot_nt(dy, w_ref[...]).astype(BF16)

    full = lambda a: pl.BlockSpec(a.shape, _fixed)
    fullf = lambda a: jax.ShapeDtypeStruct(a.shape, F32)
    return pl.pallas_call(
        body, name=name, grid=(S // T,),
        in_specs=[pl.BlockSpec((T, D), _row), pl.BlockSpec((T, D), _row), pl.BlockSpec((T, D), _row),
                  pl.BlockSpec((T, CW), _row), pl.BlockSpec((T, GW), _row), pl.BlockSpec((T, MW), _row)]
        + _gate_specs(T) + [full(wco), full(wao), full(wmo), full(wout), full(bgate), full(gpost)],
        out_specs=[pl.BlockSpec((T, 3 * D), _row), pl.BlockSpec((T, CW), _row), pl.BlockSpec((T, GW), _row),
                   pl.BlockSpec((T, MW), _row), full(wco), full(wao), full(wmo), full(wout), full(bgate), full(gpost)],
        out_shape=[jax.ShapeDtypeStruct((S, 3 * D), BF16), jax.ShapeDtypeStruct((S, CW), BF16),
                   jax.ShapeDtypeStruct((S, GW), BF16), jax.ShapeDtypeStruct((S, MW), BF16),
                   fullf(wco), fullf(wao), fullf(wmo), fullf(wout), fullf(bgate), fullf(gpost)],
        compiler_params=_params(("arbitrary",)),
    )(dx1, t, mg, cact, oatt, om, z, z, z, z, wco, wao, wmo, wout, bgate, gpost)


FFN_T = 256


def _ffn_fwd(x1, gu, wfo, gpost, name):
    S = x1.shape[0]
    T = FFN_T

    def body(x_ref, gu_ref, w_ref, gp_ref, x2_ref, f_ref):
        gv = gu_ref[:, :FH].astype(F32)
        uv = gu_ref[:, FH:].astype(F32)
        act = (gv * _sigmoid(gv) * uv).astype(BF16)
        f = _dot(act, w_ref[...])
        f_ref[...] = f
        x2_ref[...] = x_ref[...] + _rms_fwd_val(f, gp_ref[...])

    return pl.pallas_call(
        body, name=name, grid=(S // T,),
        in_specs=[pl.BlockSpec((T, D), _row), pl.BlockSpec((T, 2 * FH), _row), pl.BlockSpec((FH, D), _fixed),
                  pl.BlockSpec((1, D), _fixed)],
        out_specs=[pl.BlockSpec((T, D), _row)] * 2,
        out_shape=[jax.ShapeDtypeStruct((S, D), F32)] * 2,
        compiler_params=_params(("parallel",)),
    )(x1, gu, wfo, gpost)


def _ffn_bwd(dx2, f, gu, wfo, gpost, name):
    S = dx2.shape[0]
    T = FFN_T

    def body(dx_ref, f_ref, gu_ref, w_ref, gp_ref, dgu_ref, df_ref, act_ref, dgp_ref):
        @pl.when(pl.program_id(0) == 0)
        def _():
            dgp_ref[...] = jnp.zeros_like(dgp_ref)

        df, dgr = _rms_bwd_val(f_ref[...], gp_ref[...], dx_ref[...])
        dgp_ref[...] += jnp.sum(dgr, axis=0, keepdims=True)
        dfb = df.astype(BF16)
        df_ref[...] = dfb
        dact = _dot_nt(dfb, w_ref[...])
        gv = gu_ref[:, :FH].astype(F32)
        uv = gu_ref[:, FH:].astype(F32)
        sg = _sigmoid(gv)
        silu = gv * sg
        act_ref[...] = (silu * uv).astype(BF16)
        dgu_ref[:, :FH] = (dact * uv * (sg * (1.0 + gv * (1.0 - sg)))).astype(BF16)
        dgu_ref[:, FH:] = (dact * silu).astype(BF16)

    return pl.pallas_call(
        body, name=name, grid=(S // T,),
        in_specs=[pl.BlockSpec((T, D), _row), pl.BlockSpec((T, D), _row), pl.BlockSpec((T, 2 * FH), _row),
                  pl.BlockSpec((FH, D), _fixed), pl.BlockSpec((1, D), _fixed)],
        out_specs=[pl.BlockSpec((T, 2 * FH), _row), pl.BlockSpec((T, D), _row), pl.BlockSpec((T, FH), _row),
                   pl.BlockSpec((1, D), _fixed)],
        out_shape=[jax.ShapeDtypeStruct((S, 2 * FH), BF16), jax.ShapeDtypeStruct((S, D), BF16),
                   jax.ShapeDtypeStruct((S, FH), BF16), jax.ShapeDtypeStruct((1, D), F32)],
        compiler_params=_params(("arbitrary",)),
    )(dx2, f, gu, wfo, gpost)


def _loss_head(y, target, name):
    S = y.shape[0]
    T = 512

    def body(y_ref, t_ref, dy_ref, l_ref):
        @pl.when(pl.program_id(0) == 0)
        def _():
            l_ref[...] = jnp.zeros_like(l_ref)

        e = y_ref[...] - t_ref[...]
        dy_ref[...] = e * (1.0 / D)
        l_ref[...] += (0.5 / D) * jnp.sum(jnp.sum(e * e, axis=1, keepdims=True), axis=0, keepdims=True)

    return pl.pallas_call(
        body, name=name, grid=(S // T,),
        in_specs=[pl.BlockSpec((T, D), _row)] * 2,
        out_specs=[pl.BlockSpec((T, D), _row), pl.BlockSpec((8, 128), _fixed)],
        out_shape=[jax.ShapeDtypeStruct((S, D), F32), jax.ShapeDtypeStruct((8, 128), F32)],
        compiler_params=_params(("arbitrary",)),
    )(y, target)


BIG = ("w_in", "w_conv_out", "w_att_out", "w_mem_kv", "w_mem_out", "w_out", "w_ffn_in", "w_ffn_out")
SMALL = ("rel_bias", "norm_mix_pre", "b_gate", "conv_dw_bias", "conv_ln_g", "conv_ln_b", "norm_mem",
         "norm_mix_post", "norm_ffn_pre", "norm_ffn_post")


def _layer_fwd(l, x, mem, w, rel_bias):
    tag = f"_l{l}"
    h = _rms_h(x, w["norm_mix_pre"], "rms_mix" + tag)
    z = _mm_nn(h, w["w_in"], 1024, 768, BF16, "mm_in" + tag)
    yc, cact = _conv_fwd(z, w["conv_dw"], w["conv_dw_bias"], w["conv_ln_g"], w["conv_ln_b"], "conv_fwd" + tag)
    og, lg = zip(*[_att_fwd(z, rel_bias, g, f"att_fwd_g{g}" + tag) for g in range(3)])
    oatt, lse = _att_combine(og, lg, "att_combine" + tag)
    hm, kv = _memkv_fwd(mem, w["norm_mem"], w["w_mem_kv"], "memkv_fwd" + tag)
    om = _memattn_fwd(z, kv, "memattn_fwd" + tag)
    x1, mg, t = _merge_fwd(x, cact, oatt, om, z, w["w_conv_out"], w["w_att_out"], w["w_mem_out"], w["w_out"],
                           w["b_gate"], w["norm_mix_post"], "merge_fwd" + tag)
    h2 = _rms_h(x1, w["norm_ffn_pre"], "rms_ffn" + tag)
    gu = _mm_nn(h2, w["w_ffn_in"], 1024, 512, BF16, "mm_ffn_in" + tag)
    x2, f = _ffn_fwd(x1, gu, w["w_ffn_out"], w["norm_ffn_post"], "ffn_fwd" + tag)
    saved = dict(x=x, h=h, z=z, yc=yc, cact=cact, oatt=oatt, lse=lse, hm=hm, kv=kv, om=om, x1=x1, mg=mg, t=t,
                 h2=h2, gu=gu, f=f)
    return x2, saved


def _layer_bwd(l, dx2, mem, w, rel_bias, s):
    tag = f"_l{l}"
    gr = {}
    dgu, df, act, gr["norm_ffn_post"] = _ffn_bwd(dx2, s["f"], s["gu"], w["w_ffn_out"], w["norm_ffn_post"], "ffn_bwd" + tag)
    gr["w_ffn_out"] = _mm_tn(act, df, 1024, 512, "dw_ffn_out" + tag)
    gr["w_ffn_in"] = _mm_tn(s["h2"], dgu, 1024, 1408, "dw_ffn_in" + tag)
    dh2 = _mm_nt(dgu, w["w_ffn_in"], 1024, 1408, F32, "dh_ffn" + tag)
    dx1, gr["norm_ffn_pre"] = _rms_bwd(s["x1"], w["norm_ffn_pre"], dh2, dx2, "rms_ffn_bwd" + tag)
    (dzg, dcact, doatt, dom, gr["w_conv_out"], gr["w_att_out"], gr["w_mem_out"], gr["w_out"], gr["b_gate"],
     gr["norm_mix_post"]) = _merge_bwd(dx1, s["t"], s["mg"], s["cact"], s["oatt"], s["om"], s["z"], w["w_conv_out"],
                                       w["w_att_out"], w["w_mem_out"], w["w_out"], w["b_gate"], w["norm_mix_post"],
                                       "merge_bwd" + tag)
    dyc, gr["conv_ln_g"], gr["conv_ln_b"], gr["conv_dw_bias"] = _conv_bwd_ln(
        s["yc"], dcact, w["conv_ln_g"], w["conv_ln_b"], "conv_bwd_ln" + tag)
    dzc, dwdw = _conv_bwd_dw(s["z"], dyc, w["conv_dw"], "conv_bwd_dw" + tag)
    gr["conv_dw"] = dwdw[:KSIZE]
    ld = _att_prep(doatt, s["oatt"], s["lse"], "att_prep" + tag)
    dq, dk, dv, drb = zip(*[_att_bwd(s["z"], rel_bias, doatt, ld, g, f"att_bwd_g{g}" + tag) for g in range(3)])
    dqm, dkv = _memattn_bwd(s["z"], s["kv"], dom, "memattn_bwd" + tag)
    gr["w_mem_kv"], gr["norm_mem"] = _memkv_bwd(mem, w["norm_mem"], s["hm"], w["w_mem_kv"], dkv, "memkv_bwd" + tag)
    dz = jnp.concatenate([dzc, *dq, *dk, *dv, dqm, dzg], axis=1)
    gr["w_in"] = _mm_tn(s["h"], dz, 1024, 1152, "dw_in" + tag)
    dh = _mm_nt(dz, w["w_in"], 1024, 2304, F32, "dh_in" + tag)
    dx, gr["norm_mix_pre"] = _rms_bwd(s["x"], w["norm_mix_pre"], dh, dx1, "rms_mix_bwd" + tag)
    return dx, gr, list(drb)


def _rel_bias_total(parts, name):
    def body(*refs):
        out_ref = refs[-1]
        acc = jnp.zeros((NUM_BUCKETS, 128), F32)
        for l in range(DEPTH):
            for g in range(3):
                v = refs[l * 3 + g][...]
                acc = acc + (v if g == 0 else pltpu.roll(v, HPG * g, axis=1))
        out_ref[...] = acc

    return pl.pallas_call(body, name=name, out_shape=jax.ShapeDtypeStruct((NUM_BUCKETS, 128), F32),
                          compiler_params=_params())(*[p for layer in parts for p in layer])


def _local_step(x, mem, target, rel_bias, layers):
    saved = []
    for l in range(DEPTH):
        x, s = _layer_fwd(l, x, mem, layers[l], rel_bias)
        saved.append(s)
    dy, lpart = _loss_head(x, target, "loss_head")
    grads = [None] * DEPTH
    drb = [None] * DEPTH
    for l in reversed(range(DEPTH)):
        dy, grads[l], drb[l] = _layer_bwd(l, dy, mem, layers[l], rel_bias, saved[l])
    return lpart[0, 0], dy, grads, _rel_bias_total(drb, "rel_bias_total")


N_CHIPS = 4
SHARD = {"w_in": ((D, NIN // 4), 1), "w_conv_out": ((CW, D // 4), 1), "w_att_out": ((GW, D // 4), 1),
         "w_mem_kv": ((D // 4, 2 * MW), 0), "w_mem_out": ((MW, D // 4), 1), "w_out": ((D // 4, D), 0),
         "w_ffn_in": ((D, 2 * FH // 4), 1), "w_ffn_out": ((FH // 4, D), 0)}
PACK_ROWS = {n: SHARD[n][0][0] * SHARD[n][0][1] // D for n in BIG}
LROWS = sum(PACK_ROWS.values())
CDW_ROWS = 64
VEC_ROWS = (("norm_mix_pre", 1), ("b_gate", 3), ("conv_dw_bias", 1), ("conv_ln_g", 1), ("conv_ln_b", 1),
            ("norm_mem", 1), ("norm_mix_post", 1), ("norm_ffn_pre", 1), ("norm_ffn_post", 1))
VEC_LROWS = sum(r for _, r in VEC_ROWS)
REL_ROW = DEPTH * VEC_LROWS
CDW_ROW = REL_ROW + 1
CDW_GROWS = DEPTH * KSIZE * CW // D
SMALL_ROWS = -(-(CDW_ROW + CDW_GROWS) // 8) * 8


def _mesh_pos():
    return lax.axis_index("x"), lax.axis_index("y"), lax.axis_index("c")


def _other_chips(x, y):
    chips = [(1 - x, y), (x, 1 - y), (1 - x, 1 - y)]
    return chips, [2 * cx + cy for cx, cy in chips]


def _all_gather(flat, cdw):
    def body(flat_ref, cdw_ref, g_ref, gc_ref, send_sems, recv_sems):
        x, y, c = _mesh_pos()
        j = 2 * x + y
        sibling = (x, y, 1 - c)
        chips, blocks = _other_chips(x, y)

        def copy(k, src, dst, to):
            return pltpu.make_async_remote_copy(src_ref=src, dst_ref=dst, send_sem=send_sems.at[k],
                                                recv_sem=recv_sems.at[k], device_id=to, device_id_type=MESH)

        first = [copy(k, flat_ref.at[c], g_ref.at[j, c], (*chip, c)) for k, chip in enumerate(chips)]
        first += [copy(6 + k, cdw_ref, gc_ref.at[j], (*chip, c)) for k, chip in enumerate(chips)]
        for cp in first:
            cp.start()
        passed = [copy(3 + k, g_ref.at[b, c], g_ref.at[b, c], sibling) for k, b in enumerate(blocks)]
        for k, b in enumerate(blocks):
            copy(k, flat_ref.at[c], g_ref.at[b, c], sibling).wait_recv()
            passed[k].start()
        for k, b in enumerate(blocks):
            copy(3 + k, flat_ref.at[c], g_ref.at[b, 1 - c], sibling).wait_recv()
            copy(6 + k, cdw_ref, gc_ref.at[b], sibling).wait_recv()
        for cp in first + passed:
            cp.wait_send()

    any_spec = pl.BlockSpec(memory_space=pl.ANY)
    return pl.pallas_call(
        body, name="all_gather_weights",
        out_shape=[jax.ShapeDtypeStruct((N_CHIPS, DEPTH, LROWS, D), BF16),
                   jax.ShapeDtypeStruct((N_CHIPS, CDW_ROWS, 128), F32)],
        in_specs=[any_spec, any_spec], out_specs=[any_spec, any_spec],
        scratch_shapes=[pltpu.SemaphoreType.DMA((9,)), pltpu.SemaphoreType.DMA((9,))],
    )(flat, cdw)


def _sibling_exchange(p):
    def body(p_ref, r_ref, send_sem, recv_sem):
        x, y, c = _mesh_pos()
        cp = pltpu.make_async_remote_copy(src_ref=p_ref.at[1 - c], dst_ref=r_ref, send_sem=send_sem,
                                          recv_sem=recv_sem, device_id=(x, y, 1 - c), device_id_type=MESH)
        cp.start()
        cp.wait()

    any_spec = pl.BlockSpec(memory_space=pl.ANY)
    return pl.pallas_call(
        body, name="grad_sibling_exchange", out_shape=jax.ShapeDtypeStruct(p.shape[1:], p.dtype),
        in_specs=[any_spec], out_specs=any_spec,
        scratch_shapes=[pltpu.SemaphoreType.DMA, pltpu.SemaphoreType.DMA],
    )(p)


SUM_T = 1168


def _add_own_layer(where, p, r):
    T = SUM_T

    def body(where_ref, p_ref, r_ref, o_ref):
        o_ref[...] = (p_ref[0].astype(F32) + r_ref[...].astype(F32)).astype(BF16)

    return pl.pallas_call(
        body, name="grad_add_sibling",
        grid_spec=pltpu.PrefetchScalarGridSpec(
            num_scalar_prefetch=1, grid=(N_CHIPS, LROWS // T),
            in_specs=[pl.BlockSpec((1, 1, T, D), lambda j, i, wh: (wh[0], j, i, 0)),
                      pl.BlockSpec((1, T, D), lambda j, i, wh: (j, i, 0))],
            out_specs=pl.BlockSpec((1, T, D), lambda j, i, wh: (j, i, 0))),
        out_shape=jax.ShapeDtypeStruct(r.shape, BF16), compiler_params=_params(("parallel", "parallel")),
    )(where, p, r)


def _chip_exchange(a):
    def body(a_ref, r_ref, send_sems, recv_sems):
        x, y, c = _mesh_pos()
        chips, blocks = _other_chips(x, y)
        cps = [pltpu.make_async_remote_copy(src_ref=a_ref.at[b], dst_ref=r_ref.at[k], send_sem=send_sems.at[k],
                                            recv_sem=recv_sems.at[k], device_id=(*chip, c), device_id_type=MESH)
               for k, (chip, b) in enumerate(zip(chips, blocks))]
        for cp in cps:
            cp.start()
        for cp in cps:
            cp.wait_recv()
        for cp in cps:
            cp.wait_send()

    any_spec = pl.BlockSpec(memory_space=pl.ANY)
    return pl.pallas_call(
        body, name="grad_chip_exchange", out_shape=jax.ShapeDtypeStruct((3,) + a.shape[1:], a.dtype),
        in_specs=[any_spec], out_specs=any_spec,
        scratch_shapes=[pltpu.SemaphoreType.DMA((3,)), pltpu.SemaphoreType.DMA((3,))],
    )(a)


def _sum_chips(where, a, r):
    T = SUM_T

    def body(where_ref, a_ref, r_ref, o_ref):
        acc = a_ref[0].astype(F32)
        for k in range(3):
            acc = acc + r_ref[k].astype(F32)
        o_ref[...] = acc

    return pl.pallas_call(
        body, name="grad_sum_chips",
        grid_spec=pltpu.PrefetchScalarGridSpec(
            num_scalar_prefetch=1, grid=(LROWS // T,),
            in_specs=[pl.BlockSpec((1, T, D), lambda i, wh: (wh[1], i, 0)),
                      pl.BlockSpec((3, T, D), lambda i, wh: (0, i, 0))],
            out_specs=pl.BlockSpec((T, D), lambda i, wh: (i, 0))),
        out_shape=jax.ShapeDtypeStruct((LROWS, D), F32), compiler_params=_params(("parallel",)),
    )(where, a, r)


def _sibling_share(f):
    def body(f_ref, o_ref, send_sem, recv_sem):
        x, y, c = _mesh_pos()
        cp = pltpu.make_async_remote_copy(src_ref=f_ref, dst_ref=o_ref, send_sem=send_sem, recv_sem=recv_sem,
                                          device_id=(x, y, 1 - c), device_id_type=MESH)
        cp.start()
        cp.wait()

    any_spec = pl.BlockSpec(memory_space=pl.ANY)
    return pl.pallas_call(
        body, name="grad_sibling_share", out_shape=jax.ShapeDtypeStruct(f.shape, f.dtype),
        in_specs=[any_spec], out_specs=any_spec,
        scratch_shapes=[pltpu.SemaphoreType.DMA, pltpu.SemaphoreType.DMA],
    )(f)


def _all_reduce_small(sp):
    def body(sp_ref, out_ref, buf, send_sems, recv_sems):
        x, y, c = _mesh_pos()
        me = 4 * x + 2 * y + c
        buf[0] = sp_ref[...]
        cps = []
        for k in range(1, 8):
            peer = (x ^ (k >> 2 & 1), y ^ (k >> 1 & 1), c ^ (k & 1))
            cps.append(pltpu.make_async_remote_copy(src_ref=sp_ref, dst_ref=buf.at[k], send_sem=send_sems.at[k - 1],
                                                    recv_sem=recv_sems.at[k - 1], device_id=peer, device_id_type=MESH))
        for cp in cps:
            cp.start()
        for cp in cps:
            cp.wait_recv()
        for cp in cps:
            cp.wait_send()
        acc = buf[me]
        for p in range(1, 8):
            acc = acc + buf[p ^ me]
        out_ref[...] = acc

    vm = pl.BlockSpec(memory_space=pltpu.VMEM)
    return pl.pallas_call(
        body, name="all_reduce_small", out_shape=jax.ShapeDtypeStruct(sp.shape, F32),
        in_specs=[vm], out_specs=vm,
        scratch_shapes=[pltpu.VMEM((8,) + sp.shape, F32), pltpu.SemaphoreType.DMA((7,)), pltpu.SemaphoreType.DMA((7,))],
        compiler_params=_params(),
    )(sp)


def _adamw(w, g, m, v, name):
    R, C = w.shape
    T = next((t for t in (256, 128) if R % t == 0), R)

    def body(w_ref, g_ref, m_ref, v_ref, d_ref, m2_ref, v2_ref):
        gv = g_ref[...]
        m2 = ADAM_B1 * m_ref[...] + (1.0 - ADAM_B1) * gv
        v2 = ADAM_B2 * v_ref[...] + (1.0 - ADAM_B2) * (gv * gv)
        m_hat = m2 / (1.0 - ADAM_B1 ** ADAM_STEP)
        v_hat = v2 / (1.0 - ADAM_B2 ** ADAM_STEP)
        d_ref[...] = -ADAM_LR * (m_hat / (jnp.sqrt(v_hat) + ADAM_EPS) + ADAM_WD * w_ref[...])
        m2_ref[...] = m2
        v2_ref[...] = v2

    blk = pl.BlockSpec((T, C), _row)
    return pl.pallas_call(
        body, name=name, grid=(R // T,), in_specs=[blk] * 4, out_specs=[blk] * 3,
        out_shape=[jax.ShapeDtypeStruct((R, C), F32)] * 3, compiler_params=_params(("parallel",)),
    )(w, g, m, v)


def _pack_vectors(get, rel, cdw):
    rows = []
    for l in range(DEPTH):
        for n, r in VEC_ROWS:
            v = get(n)[l]
            rows.append(jnp.pad(v, (0, r * D - v.shape[0])).reshape(r, D))
    rows.append(jnp.pad(rel.reshape(-1), (0, D - NUM_BUCKETS * 3 * HPG)).reshape(1, D))
    rows.append(cdw.reshape(CDW_GROWS, D))
    rows.append(jnp.zeros((SMALL_ROWS - CDW_ROW - CDW_GROWS, D), F32))
    return jnp.concatenate(rows, axis=0)


def _unpack_vectors(packed, lens):
    out = {n: [] for n, _ in VEC_ROWS}
    for l in range(DEPTH):
        at = l * VEC_LROWS
        for n, r in VEC_ROWS:
            out[n].append(packed[at:at + r].reshape(-1)[:lens[n]])
            at += r
    rel = packed[REL_ROW, :NUM_BUCKETS * 3 * HPG].reshape(NUM_BUCKETS, 3 * HPG)
    return {n: jnp.stack(v) for n, v in out.items()}, rel


INPUT_NAMES = ("x", "mem") + ("rel_bias", "norm_mix_pre", "w_in", "b_gate", "conv_dw", "conv_dw_bias", "conv_ln_g",
                              "conv_ln_b", "w_conv_out", "w_att_out", "norm_mem", "w_mem_kv", "w_mem_out", "w_out",
                              "norm_mix_post", "norm_ffn_pre", "w_ffn_in", "w_ffn_out", "norm_ffn_post")
WEIGHT_NAMES = INPUT_NAMES[2:]


def kernel(*args):
    nw = len(WEIGHT_NAMES)
    a = dict(zip(INPUT_NAMES, args[:2 + nw]))
    target = args[2 + nw]
    mom = dict(zip(WEIGHT_NAMES, args[3 + nw:3 + 2 * nw]))
    var = dict(zip(WEIGHT_NAMES, args[3 + 2 * nw:3 + 3 * nw]))
    xi, yi, ci = _mesh_pos()
    chip = 2 * xi + yi
    where = jnp.stack([ci, chip]).astype(I32)

    flat = jnp.concatenate([a[n].astype(BF16).reshape(DEPTH, PACK_ROWS[n], D) for n in BIG], axis=1)
    cdw = jnp.pad(a["conv_dw"].reshape(DEPTH * KSIZE, CW // 4), ((0, CDW_ROWS - DEPTH * KSIZE), (0, 0)))
    gathered, gcdw = _all_gather(flat, cdw)
    gathered = lax.dynamic_update_slice(gathered, flat[None], (chip, 0, 0, 0))
    gcdw = lax.dynamic_update_slice(gcdw, cdw[None], (chip, 0, 0))
    conv_dw = gcdw[:, :DEPTH * KSIZE].reshape(N_CHIPS, DEPTH, KSIZE, CW // 4).transpose(1, 2, 0, 3)
    conv_dw = jnp.pad(conv_dw.reshape(DEPTH, KSIZE, CW), ((0, 0), (0, 1), (0, 0)))
    layers = []
    for l in range(DEPTH):
        w = {"conv_dw": conv_dw[l]}
        at = 0
        for n in BIG:
            (s0, s1), axis = SHARD[n]
            blk = gathered[:, l, at:at + PACK_ROWS[n]].reshape(N_CHIPS, s0, s1)
            w[n] = blk.reshape(N_CHIPS * s0, s1) if axis == 0 else blk.transpose(1, 0, 2).reshape(s0, N_CHIPS * s1)
            at += PACK_ROWS[n]
        for n, _ in VEC_ROWS:
            w[n] = a[n][l][None, :]
        layers.append(w)

    loss_part, gx, grads, drel = _local_step(a["x"][0], a["mem"][0], target[0], a["rel_bias"], layers)
    loss = lax.psum(loss_part, ("x", "y", "c"))

    packed = []
    for l in range(DEPTH):
        parts = []
        for n in BIG:
            (s0, s1), axis = SHARD[n]
            g = grads[l][n]
            g = g.reshape(N_CHIPS, s0, s1) if axis == 0 else g.reshape(s0, N_CHIPS, s1).transpose(1, 0, 2)
            parts.append(g.reshape(N_CHIPS, PACK_ROWS[n], D))
        packed.append(jnp.concatenate(parts, axis=1).astype(BF16))
    packed = jnp.stack(packed)
    chip_sums = _add_own_layer(where, packed, _sibling_exchange(packed))
    own_layer = _sum_chips(where, chip_sums, _chip_exchange(chip_sums))
    other_layer = _sibling_share(own_layer)

    gvec = _all_reduce_small(_pack_vectors(
        lambda n: jnp.stack([grads[l][n][0] for l in range(DEPTH)]), drel[:, :3 * HPG],
        jnp.stack([grads[l]["conv_dw"] for l in range(DEPTH)])))
    lens = {n: a[n].shape[1] for n, _ in VEC_ROWS}
    g_vec, g_rel = _unpack_vectors(gvec, lens)
    g_cdw = lax.dynamic_slice_in_dim(gvec[CDW_ROW:CDW_ROW + CDW_GROWS].reshape(DEPTH, KSIZE, CW), chip * (CW // 4),
                                     CW // 4, axis=2)

    grad, delta, new_m, new_v = {}, {}, {}, {}
    at = 0
    for n in BIG:
        shape = a[n].shape
        mine, other = (t[at:at + PACK_ROWS[n]].reshape(shape[1:]) for t in (own_layer, other_layer))
        g = jnp.stack([jnp.where(ci == l, mine, other) for l in range(DEPTH)])
        at += PACK_ROWS[n]
        flat2 = lambda t: t.reshape(shape[0] * shape[1], shape[2])
        d, m2, v2 = _adamw(flat2(a[n]), flat2(g), flat2(mom[n]), flat2(var[n]), "adamw_" + n)
        grad[n], delta[n], new_m[n], new_v[n] = g, d.reshape(shape), m2.reshape(shape), v2.reshape(shape)
    shape = a["conv_dw"].shape
    flat2 = lambda t: t.reshape(shape[0] * shape[1], shape[2])
    d, m2, v2 = _adamw(flat2(a["conv_dw"]), flat2(g_cdw), flat2(mom["conv_dw"]), flat2(var["conv_dw"]), "adamw_conv_dw")
    grad["conv_dw"], delta["conv_dw"], new_m["conv_dw"], new_v["conv_dw"] = (
        g_cdw, d.reshape(shape), m2.reshape(shape), v2.reshape(shape))
    zero_cdw = jnp.zeros((DEPTH, KSIZE, CW), F32)
    pk = lambda src: _pack_vectors(lambda n: src[n], src["rel_bias"], zero_cdw)
    d, m2, v2 = _adamw(pk(a), gvec, pk(mom), pk(var), "adamw_vectors")
    for src, dst in ((d, delta), (m2, new_m), (v2, new_v)):
        vec, rel = _unpack_vectors(src, lens)
        dst.update(vec)
        dst["rel_bias"] = rel
    grad.update(g_vec)
    grad["rel_bias"] = g_rel

    outs = [loss, gx[None]]
    for group in (grad, delta, new_m, new_v):
        outs += [group[n] for n in WEIGHT_NAMES]
    return tuple(outs)
```

```python
import functools
import math

import jax
import jax.numpy as jnp
from jax import lax
from jax.experimental import pallas as pl
from jax.experimental.pallas import tpu as pltpu

F32 = jnp.float32
BF16 = jnp.bfloat16
I32 = jnp.int32

D = 1024
DEPTH = 2
N_MEM = 256
CW = 512
KSIZE = 31
PAD = KSIZE // 2
DILS = (1, 4, 16)
RADIUS = 64
HPG = 4
HD = 64
GW = HPG * HD
MH = 4
MHD = 128
MW = MH * MHD
FH = 2816
NIN = 6912
C1 = 2 * CW
R_ATT = C1
R_MEM = R_ATT + 9 * GW
R_GATE = R_MEM + MW
Z_GATE = 0
Z_CONV = 3 * D
Z_MEM = Z_CONV + C1
Z_ATT = Z_MEM + MW
NUM_BUCKETS = 32
MAX_DISTANCE = 1024
RMS_EPS = 1e-6
LN_EPS = 1e-5
NEG_INF = -1e30
ATT_SCALE = HD ** -0.5
MEM_SCALE = MHD ** -0.5

ADAM_LR = 0.001
ADAM_B1 = 0.9
ADAM_B2 = 0.999
ADAM_EPS = 1e-08
ADAM_WD = 0.01
ADAM_STEP = 10

VMEM_LIMIT_BYTES = 56 * 1024 * 1024
ATT_QB = 128
ATT_TB = 16 * ATT_QB

MESH = pl.DeviceIdType.MESH


def _params(sem=None):
    return pltpu.CompilerParams(dimension_semantics=sem, vmem_limit_bytes=VMEM_LIMIT_BYTES)


def _sigmoid(v):
    return 1.0 / (1.0 + jnp.exp(-v))


def _dot(a, b):
    return jnp.dot(a, b, preferred_element_type=F32)


def _dot_nt(a, b):
    return lax.dot_general(a, b, (((1,), (1,)), ((), ())), preferred_element_type=F32)


def _dot_tn(a, b):
    return lax.dot_general(a, b, (((0,), (0,)), ((), ())), preferred_element_type=F32)


def _rms_fwd_val(v, g):
    r = lax.rsqrt(jnp.mean(v * v, axis=-1, keepdims=True) + RMS_EPS)
    return v * r * g


def _rms_bwd_val(v, g, dy):
    r = lax.rsqrt(jnp.mean(v * v, axis=-1, keepdims=True) + RMS_EPS)
    vh = v * r
    dvh = dy * g
    dv = r * (dvh - vh * jnp.mean(dvh * vh, axis=-1, keepdims=True))
    return dv, dy * vh


def _row(i):
    return (i, 0)


def _fixed(*_):
    return (0, 0)


def _mm_nn(a, b, tm, tn, out_dtype, name):
    M, K = a.shape
    N = b.shape[1]

    def body(a_ref, b_ref, o_ref):
        o_ref[...] = _dot(a_ref[...], b_ref[...]).astype(out_dtype)

    return pl.pallas_call(
        body, name=name, grid=(N // tn, M // tm),
        in_specs=[pl.BlockSpec((tm, K), lambda j, i: (i, 0)), pl.BlockSpec((K, tn), lambda j, i: (0, j))],
        out_specs=pl.BlockSpec((tm, tn), lambda j, i: (i, j)),
        out_shape=jax.ShapeDtypeStruct((M, N), out_dtype),
        compiler_params=_params(("parallel", "parallel")),
    )(a, b)


def _mm_nt(a, b, tm, tc, out_dtype, name):
    M, N = a.shape
    K = b.shape[0]
    nk = N // tc

    def body(a_ref, b_ref, o_ref, acc_ref):
        k = pl.program_id(1)

        @pl.when(k == 0)
        def _():
            acc_ref[...] = jnp.zeros_like(acc_ref)

        acc_ref[...] += _dot_nt(a_ref[...], b_ref[...])

        @pl.when(k == nk - 1)
        def _():
            o_ref[...] = acc_ref[...].astype(out_dtype)

    return pl.pallas_call(
        body, name=name, grid=(M // tm, nk),
        in_specs=[pl.BlockSpec((tm, tc), lambda i, k: (i, k)), pl.BlockSpec((K, tc), lambda i, k: (0, k))],
        out_specs=pl.BlockSpec((tm, K), lambda i, k: (i, 0)),
        out_shape=jax.ShapeDtypeStruct((M, K), out_dtype),
        scratch_shapes=[pltpu.VMEM((tm, K), F32)],
        compiler_params=_params(("parallel", "arbitrary")),
    )(a, b)


def _mm_tn(a, b, ts, tn, name):
    S, K = a.shape
    N = b.shape[1]

    def body(a_ref, b_ref, o_ref):
        @pl.when(pl.program_id(1) == 0)
        def _():
            o_ref[...] = jnp.zeros_like(o_ref)

        o_ref[...] += _dot_tn(a_ref[...], b_ref[...])

    return pl.pallas_call(
        body, name=name, grid=(N // tn, S // ts),
        in_specs=[pl.BlockSpec((ts, K), lambda j, s: (s, 0)), pl.BlockSpec((ts, tn), lambda j, s: (s, j))],
        out_specs=pl.BlockSpec((K, tn), lambda j, s: (0, j)),
        out_shape=jax.ShapeDtypeStruct((K, N), F32),
        compiler_params=_params(("parallel", "arbitrary")),
    )(a, b)


def _rms_h(x, g, name):
    S = x.shape[0]
    T = 512

    def body(x_ref, g_ref, h_ref):
        h_ref[...] = _rms_fwd_val(x_ref[...], g_ref[...]).astype(BF16)

    return pl.pallas_call(
        body, name=name, grid=(S // T,),
        in_specs=[pl.BlockSpec((T, D), _row), pl.BlockSpec((1, D), _fixed)],
        out_specs=pl.BlockSpec((T, D), _row),
        out_shape=jax.ShapeDtypeStruct((S, D), BF16),
        compiler_params=_params(("parallel",)),
    )(x, g)


def _rms_bwd(x, g, dh, dres, name):
    S = x.shape[0]
    T = 512

    def body(x_ref, g_ref, dh_ref, dres_ref, dx_ref, dg_ref):
        dv, dgr = _rms_bwd_val(x_ref[...], g_ref[...], dh_ref[...].astype(F32))
        dx_ref[...] = dres_ref[...] + dv

        @pl.when(pl.program_id(0) == 0)
        def _():
            dg_ref[...] = jnp.zeros_like(dg_ref)

        dg_ref[...] += jnp.sum(dgr, axis=0, keepdims=True)

    return pl.pallas_call(
        body, name=name, grid=(S // T,),
        in_specs=[pl.BlockSpec((T, D), _row), pl.BlockSpec((1, D), _fixed), pl.BlockSpec((T, D), _row),
                  pl.BlockSpec((T, D), _row)],
        out_specs=[pl.BlockSpec((T, D), _row), pl.BlockSpec((1, D), _fixed)],
        out_shape=[jax.ShapeDtypeStruct((S, D), F32), jax.ShapeDtypeStruct((1, D), F32)],
        compiler_params=_params(("arbitrary",)),
    )(x, g, dh, dres)


CONV_T = 256
CONV_HALO = 16
CONV_RC = 32


def _halo_specs(T, halo, S, width, col):
    per = T // halo
    last = S // halo - 1
    return [
        pl.BlockSpec((T, width), lambda i: (i, col)),
        pl.BlockSpec((halo, width), lambda i: (jnp.maximum(i * per - 1, 0), col)),
        pl.BlockSpec((halo, width), lambda i: (jnp.minimum((i + 1) * per, last), col)),
    ]


def _glu(zb):
    zb = zb.astype(F32)
    return zb[:, :CW] * _sigmoid(zb[:, CW:])


CONV_EXT = CONV_T + 2 * CONV_HALO
SUBLANES = 8


def _fill_shifted(sh_ref, ext_ref, cur, prev, nxt):
    T, halo = CONV_T, CONV_HALO
    i = pl.program_id(0)
    n = pl.num_programs(0)
    ext_ref[0:halo, :] = jnp.where(i > 0, prev, 0.0)
    ext_ref[halo:halo + T, :] = cur
    ext_ref[halo + T:CONV_EXT, :] = jnp.where(i < n - 1, nxt, 0.0)
    ext_ref[CONV_EXT:CONV_EXT + SUBLANES, :] = jnp.zeros((SUBLANES, CW), F32)
    for b in range(SUBLANES):
        sh_ref[b] = ext_ref[b:b + CONV_EXT, :]


def _window(sh_ref, start, rows):
    b = start % SUBLANES
    return sh_ref[b, start - b:start - b + rows, :]


def _shifted_scratch():
    return [pltpu.VMEM((CONV_EXT + SUBLANES, CW), F32), pltpu.VMEM((SUBLANES, CONV_EXT, CW), F32)]


def _conv_fwd(z, wdw, bdw, lng, lnb, name):
    S = z.shape[0]
    T, HL, RC = CONV_T, CONV_HALO, CONV_RC

    def body(cur_ref, prev_ref, next_ref, w_ref, b_ref, g_ref, bb_ref, yc_ref, act_ref, ext_ref, sh_ref):
        _fill_shifted(sh_ref, ext_ref, _glu(cur_ref[...]), _glu(prev_ref[...]), _glu(next_ref[...]))
        for c in range(T // RC):
            acc = jnp.zeros((RC, CW), F32)
            for k in range(KSIZE):
                acc = acc + w_ref[k:k + 1, :] * _window(sh_ref, c * RC + k + HL - PAD, RC)
            yc = acc + b_ref[...]
            yc_ref[c * RC:(c + 1) * RC, :] = yc
            mu = jnp.mean(yc, axis=-1, keepdims=True)
            xc = yc - mu
            ln = xc * lax.rsqrt(jnp.mean(xc * xc, axis=-1, keepdims=True) + LN_EPS) * g_ref[...] + bb_ref[...]
            act_ref[c * RC:(c + 1) * RC, :] = (ln * _sigmoid(ln)).astype(BF16)

    return pl.pallas_call(
        body, name=name, grid=(S // T,),
        in_specs=_halo_specs(T, HL, S, C1, Z_CONV // C1) + [pl.BlockSpec((32, CW), _fixed)]
        + [pl.BlockSpec((1, CW), _fixed)] * 3,
        out_specs=[pl.BlockSpec((T, CW), _row), pl.BlockSpec((T, CW), _row)],
        out_shape=[jax.ShapeDtypeStruct((S, CW), F32), jax.ShapeDtypeStruct((S, CW), BF16)],
        scratch_shapes=_shifted_scratch(),
        compiler_params=_params(("parallel",)),
    )(z, z, z, wdw, bdw, lng, lnb)


def _conv_bwd_ln(yc, dact, lng, lnb, name):
    S = yc.shape[0]
    T = 512

    def body(yc_ref, da_ref, g_ref, b_ref, dyc_ref, dg_ref, db_ref, dbias_ref):
        yc_v = yc_ref[...]
        mu = jnp.mean(yc_v, axis=-1, keepdims=True)
        xc = yc_v - mu
        r = lax.rsqrt(jnp.mean(xc * xc, axis=-1, keepdims=True) + LN_EPS)
        yn = xc * r
        ln = yn * g_ref[...] + b_ref[...]
        sg = _sigmoid(ln)
        dln = da_ref[...].astype(F32) * (sg * (1.0 + ln * (1.0 - sg)))
        dyn = dln * g_ref[...]
        dyc = r * (dyn - jnp.mean(dyn, axis=-1, keepdims=True) - yn * jnp.mean(dyn * yn, axis=-1, keepdims=True))
        dyc_ref[...] = dyc

        @pl.when(pl.program_id(0) == 0)
        def _():
            dg_ref[...] = jnp.zeros_like(dg_ref)
            db_ref[...] = jnp.zeros_like(db_ref)
            dbias_ref[...] = jnp.zeros_like(dbias_ref)

        dg_ref[...] += jnp.sum(dln * yn, axis=0, keepdims=True)
        db_ref[...] += jnp.sum(dln, axis=0, keepdims=True)
        dbias_ref[...] += jnp.sum(dyc, axis=0, keepdims=True)

    vec = pl.BlockSpec((1, CW), _fixed)
    return pl.pallas_call(
        body, name=name, grid=(S // T,),
        in_specs=[pl.BlockSpec((T, CW), _row), pl.BlockSpec((T, CW), _row), vec, vec],
        out_specs=[pl.BlockSpec((T, CW), _row), vec, vec, vec],
        out_shape=[jax.ShapeDtypeStruct((S, CW), F32)] + [jax.ShapeDtypeStruct((1, CW), F32)] * 3,
        compiler_params=_params(("arbitrary",)),
    )(yc, dact, lng, lnb)


def _conv_bwd_dw(z, dyc, wdw, dz, name):
    S = z.shape[0]
    T, HL, RC = CONV_T, CONV_HALO, CONV_RC

    def body(zc_ref, zp_ref, zn_ref, dc_ref, dp_ref, dn_ref, w_ref, dz_in, dz_ref, dw_ref, uext_ref, ush_ref,
             dext_ref, dsh_ref, dwacc_ref):
        _fill_shifted(ush_ref, uext_ref, _glu(zc_ref[...]), _glu(zp_ref[...]), _glu(zn_ref[...]))
        _fill_shifted(dsh_ref, dext_ref, dc_ref[...], dp_ref[...], dn_ref[...])

        @pl.when(pl.program_id(0) == 0)
        def _():
            dwacc_ref[...] = jnp.zeros_like(dwacc_ref)

        for c in range(T // RC):
            dcur = dc_ref[c * RC:(c + 1) * RC, :]
            du = jnp.zeros((RC, CW), F32)
            for k in range(KSIZE):
                du = du + w_ref[k:k + 1, :] * _window(dsh_ref, c * RC + HL + PAD - k, RC)
                prod = dcur * _window(ush_ref, c * RC + k + HL - PAD, RC)
                dwacc_ref[k] += jnp.sum(prod.reshape(RC // SUBLANES, SUBLANES, CW), axis=0)
            zc = zc_ref[c * RC:(c + 1) * RC, :].astype(F32)
            a, gt = zc[:, :CW], zc[:, CW:]
            sg = _sigmoid(gt)
            dz_ref[c * RC:(c + 1) * RC, 0:CW] = (du * sg).astype(BF16)
            dz_ref[c * RC:(c + 1) * RC, CW:C1] = (du * a * sg * (1.0 - sg)).astype(BF16)

        @pl.when(pl.program_id(0) == pl.num_programs(0) - 1)
        def _():
            dw_ref[...] = jnp.sum(dwacc_ref[...], axis=1)

    return pl.pallas_call(
        body, name=name, grid=(S // T,),
        in_specs=_halo_specs(T, HL, S, C1, Z_CONV // C1) + _halo_specs(T, HL, S, CW, 0)
        + [pl.BlockSpec((32, CW), _fixed), pl.BlockSpec(memory_space=pl.ANY)],
        out_specs=[pl.BlockSpec((T, C1), lambda i: (i, Z_CONV // C1)), pl.BlockSpec((32, CW), _fixed)],
        out_shape=[jax.ShapeDtypeStruct(dz.shape, BF16), jax.ShapeDtypeStruct((32, CW), F32)],
        input_output_aliases={7: 0},
        scratch_shapes=_shifted_scratch() + _shifted_scratch() + [pltpu.VMEM((32, SUBLANES, CW), F32)],
        compiler_params=_params(("arbitrary",)),
    )(z, z, z, dyc, dyc, dyc, wdw, dz)


def _t5_bucket(rel):
    nb = NUM_BUCKETS // 2
    max_exact = nb // 2
    ret = jnp.where(rel > 0, nb, 0)
    n = jnp.abs(rel)
    nf = jnp.maximum(n, 1).astype(F32)
    large = max_exact + (jnp.log(nf / max_exact) / math.log(MAX_DISTANCE / max_exact)
                         * (nb - max_exact)).astype(I32)
    large = jnp.minimum(large, nb - 1)
    return ret + jnp.where(n < max_exact, n, large)


def _offsets_qk(nq, nk, shift):
    return lax.broadcasted_iota(I32, (nq, nk), 1) + shift - lax.broadcasted_iota(I32, (nq, nk), 0)


def _bias_table(bk, rb_ref, col, off):
    acc = jnp.zeros(bk.shape, F32)
    for b in range(NUM_BUCKETS):
        acc = jnp.where(bk == b, rb_ref[b, col], acc)
    return jnp.where(jnp.abs(off) <= RADIUS, acc, NEG_INF)


def _to_halves(scr, row0, val):
    rows = val.shape[0]
    v = val.astype(F32)
    scr[0, row0:row0 + rows, :] = v[:, :128]
    scr[1, row0:row0 + rows, :] = v[:, 128:]


def _heads(halves):
    return [halves[h // 2][:, (h % 2) * HD:(h % 2 + 1) * HD] for h in range(HPG)]


def _join_heads(parts):
    return [jnp.concatenate(parts[0:2], axis=-1), jnp.concatenate(parts[2:4], axis=-1)]


ATT_FWD_GROUP = 2
ATT_BWD_GROUP = 1


def _att_units(d, fn, group):
    nj = ATT_TB // (ATT_QB * d)
    if nj == 1:
        def trip(t, c):
            r0 = pl.multiple_of(t * 8, 8)
            for u in range(0, 8, group):
                fn([(r0 + u + v, 0) for v in range(group)])
            return c

        lax.fori_loop(0, d // 8, trip, 0)
        return
    for r in range(d):
        def step(t, c, r=r):
            fn([(r, t * group + u) for u in range(group)])
            return c

        lax.fori_loop(0, nj // group, step, 0)


def _unit_row(r, j, d):
    if isinstance(j, int):
        return j * ATT_QB * d + r
    return pl.multiple_of(j * (ATT_QB * d), ATT_QB) + r


def _att_fwd(z, rel_bias, g, name):
    S = z.shape[0]
    d = DILS[g]
    TB, QB = ATT_TB, ATT_QB
    H = RADIUS * d
    L = S // d
    cq = (Z_ATT + 3 * GW * g) // GW
    ck, cv = cq + 1, cq + 2
    bk = _t5_bucket(_offsets_qk(QB, 2 * QB, -RADIUS) * d)

    def body(rb_ref, bk_ref, q_ref, kc_ref, kp_ref, kn_ref, vc_ref, vp_ref, vn_ref, o_ref, l_ref,
             qs, ks, vs, os_, ls, bias):
        i = pl.program_id(0)

        @pl.when(i == 0)
        def _():
            off = _offsets_qk(QB, 2 * QB, -RADIUS)
            for h in range(HPG):
                bias[h] = _bias_table(bk_ref[...], rb_ref, g * HPG + h, off)

        _to_halves(qs, 0, q_ref[...].astype(F32) * ATT_SCALE)
        for scr, p_ref, c_ref, n_ref in ((ks, kp_ref, kc_ref, kn_ref), (vs, vp_ref, vc_ref, vn_ref)):
            _to_halves(scr, 0, p_ref[...])
            _to_halves(scr, H, c_ref[...])
            _to_halves(scr, H + TB, n_ref[...])

        lo = lax.broadcasted_iota(I32, (QB, 128), 1) < HD

        def units(rjs):
            work = []
            for r, j in rjs:
                row = _unit_row(r, j, d)
                km = lax.broadcasted_iota(I32, (1, 2 * QB), 1) + (i * (TB // d) + j * QB - RADIUS)
                edge = jnp.where(jnp.where(km >= 0, km, L) < L, 0.0, NEG_INF)
                for hf in (0, 1):
                    q2 = qs[hf, pl.ds(row, QB, stride=d), :]
                    k2 = ks[hf, pl.ds(row, 2 * QB, stride=d), :].astype(BF16)
                    v2 = vs[hf, pl.ds(row, 2 * QB, stride=d), :].astype(BF16)
                    qq = jnp.concatenate([jnp.where(lo, q2, 0.0), jnp.where(lo, 0.0, q2)], axis=0).astype(BF16)
                    work.append((row, hf, edge, k2, v2, qq))
            scores = [_dot_nt(qq, k2) for (_, _, _, k2, _, qq) in work]
            probs = []
            for (row, hf, edge, *_), ss in zip(work, scores):
                es, stats = [], []
                for hh in (0, 1):
                    s = ss[hh * QB:(hh + 1) * QB] + bias[2 * hf + hh] + edge
                    m = jnp.max(s, axis=-1, keepdims=True)
                    e = jnp.exp(s - m)
                    den = jnp.sum(e, axis=-1, keepdims=True)
                    es.append(e.astype(BF16))
                    stats.append((1.0 / den, m + jnp.log(den)))
                probs.append((jnp.concatenate(es, axis=0), stats))
            for (row, hf, _, _, v2, _), (ee, stats) in zip(work, probs):
                oo = _dot(ee, v2)
                os_[hf, pl.ds(row, QB, stride=d), :] = jnp.where(lo, oo[:QB] * stats[0][0], oo[QB:] * stats[1][0])
                ls[hf, pl.ds(row, QB, stride=d), :] = jnp.where(lo, stats[0][1], stats[1][1])

        _att_units(d, units, ATT_FWD_GROUP)
        for hf in (0, 1):
            o_ref[:, hf * 128:(hf + 1) * 128] = os_[hf].astype(BF16)
            l_ref[:, hf * 128:(hf + 1) * 128] = ls[hf]

    def halo3(col):
        c, p, n = _halo_specs(TB, H, S, GW, col)
        return [c, p, n]

    return pl.pallas_call(
        body, name=name, grid=(S // TB,),
        in_specs=[pl.BlockSpec(memory_space=pltpu.SMEM), pl.BlockSpec((QB, 2 * QB), _fixed),
                  pl.BlockSpec((TB, GW), lambda i: (i, cq))] + halo3(ck) + halo3(cv),
        out_specs=[pl.BlockSpec((TB, GW), _row), pl.BlockSpec((TB, GW), _row)],
        out_shape=[jax.ShapeDtypeStruct((S, GW), BF16), jax.ShapeDtypeStruct((S, GW), F32)],
        scratch_shapes=[pltpu.VMEM((2, TB, 128), F32), pltpu.VMEM((2, TB + 2 * H, 128), F32),
                        pltpu.VMEM((2, TB + 2 * H, 128), F32), pltpu.VMEM((2, TB, 128), F32),
                        pltpu.VMEM((2, TB, 128), F32), pltpu.VMEM((HPG, QB, 2 * QB), F32)],
        compiler_params=_params(("arbitrary",)),
    )(rel_bias, bk, z, z, z, z, z, z, z)


def _att_combine(os3, ls3, name):
    S = os3[0].shape[0]
    T = 1024

    def body(o1, o2, o3, l1, l2, l3, o_ref, l_ref):
        lv = [l1[...], l2[...], l3[...]]
        m = jnp.maximum(jnp.maximum(lv[0], lv[1]), lv[2])
        e = [jnp.exp(v - m) for v in lv]
        den = e[0] + e[1] + e[2]
        acc = jnp.zeros_like(m)
        for ev, o in zip(e, (o1, o2, o3)):
            acc = acc + (ev / den) * o[...].astype(F32)
        o_ref[...] = acc.astype(BF16)
        l_ref[...] = m + jnp.log(den)

    blk = pl.BlockSpec((T, GW), _row)
    return pl.pallas_call(
        body, name=name, grid=(S // T,), in_specs=[blk] * 6, out_specs=[blk, blk],
        out_shape=[jax.ShapeDtypeStruct((S, GW), BF16), jax.ShapeDtypeStruct((S, GW), F32)],
        compiler_params=_params(("parallel",)),
    )(*os3, *ls3)


def _att_prep(do, o, lse, name):
    S = do.shape[0]
    T = 1024

    def body(do_ref, o_ref, l_ref, out_ref):
        prod = do_ref[...].astype(F32) * o_ref[...].astype(F32)
        dd = [jnp.broadcast_to(jnp.sum(prod[:, h * HD:(h + 1) * HD], axis=-1, keepdims=True), (T, HD))
              for h in range(HPG)]
        lane = lax.broadcasted_iota(I32, (T, GW), 1)
        out_ref[...] = jnp.where(lane % HD < HD // 2, l_ref[...], jnp.concatenate(dd, axis=-1))

    blk = pl.BlockSpec((T, GW), _row)
    return pl.pallas_call(
        body, name=name, grid=(S // T,), in_specs=[blk] * 3, out_specs=blk,
        out_shape=jax.ShapeDtypeStruct((S, GW), F32), compiler_params=_params(("parallel",)),
    )(do, o, lse)


def _att_bwd(z, rel_bias, do, ld, dz, g, name):
    S = z.shape[0]
    d = DILS[g]
    TB, QB = ATT_TB, ATT_QB
    H = RADIUS * d
    L = S // d
    E = TB + 2 * H
    cq = (Z_ATT + 3 * GW * g) // GW
    ck, cv = cq + 1, cq + 2
    bk_a = _t5_bucket(_offsets_qk(QB, 2 * QB, -RADIUS) * d)
    bk_b = _t5_bucket(-_offsets_qk(QB, 2 * QB, -RADIUS) * d)

    def body(rb_ref, bka_ref, bkb_ref, *refs):
        ins, (dz_ref, db_ref) = refs[:15], refs[16:18]
        qs, ks, vs, dos, ls, dqs, dks, dvs, bias_a, bias_b, dbias = refs[18:]
        i = pl.program_id(0)
        n = pl.num_programs(0)

        @pl.when(i == 0)
        def _():
            off = _offsets_qk(QB, 2 * QB, -RADIUS)
            for h in range(HPG):
                bias_a[h] = _bias_table(bka_ref[...], rb_ref, g * HPG + h, off)
                bias_b[h] = _bias_table(bkb_ref[...], rb_ref, g * HPG + h, off)
            dbias[...] = jnp.zeros_like(dbias)

        for a, scr in enumerate((qs, ks, vs, dos, ls)):
            c_ref, p_ref, n_ref = ins[3 * a:3 * a + 3]
            pre = (lambda v: v.astype(F32) * ATT_SCALE) if a == 0 else (lambda v: v)
            _to_halves(scr, 0, pre(p_ref[...]))
            _to_halves(scr, H, pre(c_ref[...]))
            _to_halves(scr, H + TB, pre(n_ref[...]))

        lo = lax.broadcasted_iota(I32, (QB, 128), 1) < HD

        def split(v):
            return jnp.concatenate([jnp.where(lo, v, 0.0), jnp.where(lo, 0.0, v)], axis=0).astype(BF16)

        def halves(v):
            return v[:QB], v[QB:]

        def units(rjs):
            work = []
            for r, j in rjs:
                row = _unit_row(r, j, d)
                cur = row + H
                m0 = i * (TB // d) + j * QB - RADIUS
                km = lax.broadcasted_iota(I32, (1, 2 * QB), 1) + m0
                edge_a = jnp.where(jnp.where(km >= 0, km, L) < L, 0.0, NEG_INF)
                for hf in (0, 1):
                    ld = lambda scr, at, nrow: scr[hf, pl.ds(at, nrow, stride=d), :]
                    w = dict(row=row, hf=hf, edge=edge_a, l_c=ld(ls, cur, QB), l_t=ld(ls, row, 2 * QB).T)
                    for nm, scr in (("q", qs), ("k", ks), ("v", vs), ("do", dos)):
                        w[nm + "_c"] = split(ld(scr, cur, QB))
                        w[nm + "_e"] = ld(scr, row, 2 * QB).astype(BF16)
                    work.append(w)
            for w in work:
                w["s"] = halves(_dot_nt(w["q_c"], w["k_e"]))
                w["dp"] = halves(_dot_nt(w["do_c"], w["v_e"]))
                w["s2"] = halves(_dot_nt(w["k_c"], w["q_e"]))
                w["dp2"] = halves(_dot_nt(w["v_c"], w["do_e"]))
            for w in work:
                w["ds"], w["p2"], w["ds2"] = [], [], []
                for hh in (0, 1):
                    h, c0 = 2 * w["hf"] + hh, HD * hh
                    l_c, l_t = w["l_c"], w["l_t"]
                    p = jnp.exp(w["s"][hh] + bias_a[h] + w["edge"] - l_c[:, c0:c0 + 1])
                    ds = p * (w["dp"][hh] - l_c[:, c0 + HD // 2:c0 + HD // 2 + 1])
                    dbias[h] += ds
                    p2 = jnp.exp(w["s2"][hh] + bias_b[h] + w["edge"] - l_t[c0:c0 + 1, :])
                    ds2 = p2 * (w["dp2"][hh] - l_t[c0 + HD // 2:c0 + HD // 2 + 1, :])
                    w["ds"].append(ds.astype(BF16))
                    w["p2"].append(p2.astype(BF16))
                    w["ds2"].append(ds2.astype(BF16))
            for w in work:
                at = pl.ds(w["row"], QB, stride=d)
                both = lambda pair, rhs: halves(_dot(jnp.concatenate(pair, axis=0), rhs))
                dq = both(w["ds"], w["k_e"])
                dqs[w["hf"], at, :] = jnp.where(lo, dq[0], dq[1]) * ATT_SCALE
                dv = both(w["p2"], w["do_e"])
                dvs[w["hf"], at, :] = jnp.where(lo, dv[0], dv[1])
                dk = both(w["ds2"], w["q_e"])
                dks[w["hf"], at, :] = jnp.where(lo, dk[0], dk[1])

        _att_units(d, units, ATT_BWD_GROUP)
        for a, scr in enumerate((dqs, dks, dvs)):
            for hf in (0, 1):
                dz_ref[:, a * GW + hf * 128:a * GW + (hf + 1) * 128] = scr[hf].astype(BF16)

        @pl.when(i == n - 1)
        def _():
            rows = lax.broadcasted_iota(I32, (NUM_BUCKETS, 128), 0)
            lanes = lax.broadcasted_iota(I32, (NUM_BUCKETS, 128), 1)
            out = jnp.zeros((NUM_BUCKETS, 128), F32)
            bk = bka_ref[...]
            for h in range(HPG):
                acc = dbias[h]
                for b in range(NUM_BUCKETS):
                    tot = jnp.sum(jnp.sum(jnp.where(bk == b, acc, 0.0), axis=1, keepdims=True), axis=0, keepdims=True)
                    out = out + jnp.where((rows == b) & (lanes == h), tot, 0.0)
            db_ref[...] = out

    def halo3(col, width=GW):
        return _halo_specs(TB, H, S, width, col)

    one = pl.Buffered(1)

    def single(specs):
        return [pl.BlockSpec(s.block_shape, s.index_map, pipeline_mode=one) for s in specs]

    in_specs = ([pl.BlockSpec(memory_space=pltpu.SMEM), pl.BlockSpec((QB, 2 * QB), _fixed),
                 pl.BlockSpec((QB, 2 * QB), _fixed)]
                + single(halo3(cq) + halo3(ck) + halo3(cv) + halo3(0) + halo3(0))
                + [pl.BlockSpec(memory_space=pl.ANY)])
    return pl.pallas_call(
        body, name=name, grid=(S // TB,), in_specs=in_specs,
        out_specs=[pl.BlockSpec((TB, 3 * GW), lambda i: (i, cq // 3)), pl.BlockSpec((NUM_BUCKETS, 128), _fixed)],
        out_shape=[jax.ShapeDtypeStruct(dz.shape, BF16), jax.ShapeDtypeStruct((NUM_BUCKETS, 128), F32)],
        input_output_aliases={18: 0},
        scratch_shapes=[pltpu.VMEM((2, E, 128), F32)] * 5 + [pltpu.VMEM((2, TB, 128), F32)] * 3
        + [pltpu.VMEM((HPG, QB, 2 * QB), F32)] * 3,
        compiler_params=_params(("arbitrary",)),
    )(rel_bias, bk_a, bk_b, z, z, z, z, z, z, z, z, z, do, do, do, ld, ld, ld, dz)


def _memkv_fwd(mem, gm, wkv, name):
    def body(m_ref, g_ref, w_ref, hm_ref, kv_ref):
        hm = _rms_fwd_val(m_ref[...], g_ref[...]).astype(BF16)
        hm_ref[...] = hm
        kv_ref[...] = _dot(hm, w_ref[...]).astype(BF16)

    return pl.pallas_call(
        body, name=name,
        out_shape=[jax.ShapeDtypeStruct((N_MEM, D), BF16), jax.ShapeDtypeStruct((N_MEM, 2 * MW), BF16)],
        compiler_params=_params(),
    )(mem, gm, wkv)


def _memkv_bwd(mem, gm, hm, wkv, dkv, name):
    def body(m_ref, g_ref, hm_ref, w_ref, dkv_ref, dw_ref, dg_ref):
        dkv_b = dkv_ref[...].astype(BF16)
        dw_ref[...] = _dot_tn(hm_ref[...], dkv_b)
        dhm = _dot_nt(dkv_b, w_ref[...])
        _, dgr = _rms_bwd_val(m_ref[...], g_ref[...], dhm)
        dg_ref[...] = jnp.sum(dgr, axis=0, keepdims=True)

    return pl.pallas_call(
        body, name=name,
        out_shape=[jax.ShapeDtypeStruct((D, 2 * MW), F32), jax.ShapeDtypeStruct((1, D), F32)],
        compiler_params=_params(),
    )(mem, gm, hm, wkv, dkv)


MEM_T = 512


def _mem_q_spec():
    return pl.BlockSpec((MEM_T, MW), lambda i: (i, Z_MEM // MW))


def _memattn_fwd(z, kv, name):
    S = z.shape[0]
    T = MEM_T

    def body(q_ref, kv_ref, o_ref):
        for h in range(MH):
            kh = kv_ref[:, h * MHD:(h + 1) * MHD]
            vh = kv_ref[:, MW + h * MHD:MW + (h + 1) * MHD]
            s = _dot_nt(q_ref[:, h * MHD:(h + 1) * MHD], kh) * MEM_SCALE
            e = jnp.exp(s - jnp.max(s, axis=-1, keepdims=True))
            p = e / jnp.sum(e, axis=-1, keepdims=True)
            o_ref[:, h * MHD:(h + 1) * MHD] = _dot(p.astype(BF16), vh).astype(BF16)

    return pl.pallas_call(
        body, name=name, grid=(S // T,),
        in_specs=[_mem_q_spec(), pl.BlockSpec((N_MEM, 2 * MW), _fixed)],
        out_specs=pl.BlockSpec((T, MW), _row),
        out_shape=jax.ShapeDtypeStruct((S, MW), BF16),
        compiler_params=_params(("parallel",)),
    )(z, kv)


def _memattn_bwd(z, kv, dom, dz, name):
    S = z.shape[0]
    T = MEM_T

    def body(q_ref, kv_ref, do_ref, dz_in, dq_ref, dkv_ref):
        @pl.when(pl.program_id(0) == 0)
        def _():
            dkv_ref[...] = jnp.zeros_like(dkv_ref)

        for h in range(MH):
            kh = kv_ref[:, h * MHD:(h + 1) * MHD]
            vh = kv_ref[:, MW + h * MHD:MW + (h + 1) * MHD]
            qh = q_ref[:, h * MHD:(h + 1) * MHD]
            doh = do_ref[:, h * MHD:(h + 1) * MHD]
            s = _dot_nt(qh, kh) * MEM_SCALE
            e = jnp.exp(s - jnp.max(s, axis=-1, keepdims=True))
            p = e / jnp.sum(e, axis=-1, keepdims=True)
            dkv_ref[:, MW + h * MHD:MW + (h + 1) * MHD] += _dot_tn(p.astype(BF16), doh)
            dp = _dot_nt(doh, vh)
            ds = (p * (dp - jnp.sum(dp * p, axis=-1, keepdims=True))).astype(BF16)
            dq_ref[:, h * MHD:(h + 1) * MHD] = (_dot(ds, kh) * MEM_SCALE).astype(BF16)
            dkv_ref[:, h * MHD:(h + 1) * MHD] += _dot_tn(ds, qh) * MEM_SCALE

    return pl.pallas_call(
        body, name=name, grid=(S // T,),
        in_specs=[_mem_q_spec(), pl.BlockSpec((N_MEM, 2 * MW), _fixed), pl.BlockSpec((T, MW), _row),
                  pl.BlockSpec(memory_space=pl.ANY)],
        out_specs=[_mem_q_spec(), pl.BlockSpec((N_MEM, 2 * MW), _fixed)],
        out_shape=[jax.ShapeDtypeStruct(dz.shape, BF16), jax.ShapeDtypeStruct((N_MEM, 2 * MW), F32)],
        input_output_aliases={3: 0},
        compiler_params=_params(("arbitrary",)),
    )(z, kv, dom, dz)


MERGE_T = 256


def _gate_spec(T):
    return pl.BlockSpec((T, 3 * D), lambda i: (i, Z_GATE // (3 * D)))


def _branches(ca_ref, oa_ref, om_ref, wco_ref, wao_ref, wmo_ref, zg_ref, bg_ref):
    ys = [_dot(ca_ref[...], wco_ref[...]), _dot(oa_ref[...], wao_ref[...]), _dot(om_ref[...], wmo_ref[...])]
    gs = [_sigmoid(zg_ref[:, b * D:(b + 1) * D].astype(F32) + bg_ref[:, b * D:(b + 1) * D]) for b in range(3)]
    return ys, gs


def _merge_fwd(x, cact, oatt, om, z, wco, wao, wmo, wout, bgate, gpost, name):
    S = x.shape[0]
    T = MERGE_T

    def body(x_ref, ca_ref, oa_ref, om_ref, zg_ref, wco_ref, wao_ref, wmo_ref, wout_ref, bg_ref, gp_ref,
             x1_ref, mg_ref, t_ref):
        ys, gs = _branches(ca_ref, oa_ref, om_ref, wco_ref, wao_ref, wmo_ref, zg_ref, bg_ref)
        mb = (gs[0] * ys[0] + gs[1] * ys[1] + gs[2] * ys[2]).astype(BF16)
        t = _dot(mb, wout_ref[...])
        mg_ref[...] = mb
        t_ref[...] = t
        x1_ref[...] = x_ref[...] + _rms_fwd_val(t, gp_ref[...])

    full = lambda a: pl.BlockSpec(a.shape, _fixed)
    return pl.pallas_call(
        body, name=name, grid=(S // T,),
        in_specs=[pl.BlockSpec((T, D), _row), pl.BlockSpec((T, CW), _row), pl.BlockSpec((T, GW), _row),
                  pl.BlockSpec((T, MW), _row), _gate_spec(T)]
        + [full(wco), full(wao), full(wmo), full(wout), full(bgate), full(gpost)],
        out_specs=[pl.BlockSpec((T, D), _row)] * 3,
        out_shape=[jax.ShapeDtypeStruct((S, D), F32), jax.ShapeDtypeStruct((S, D), BF16), jax.ShapeDtypeStruct((S, D), F32)],
        compiler_params=_params(("parallel",)),
    )(x, cact, oatt, om, z, wco, wao, wmo, wout, bgate, gpost)


def _merge_bwd(dx1, t, mg, cact, oatt, om, z, wco, wao, wmo, wout, bgate, gpost, name):
    S = dx1.shape[0]
    T = MERGE_T

    def body(dx_ref, t_ref, mg_ref, ca_ref, oa_ref, om_ref, zg_ref, wco_ref, wao_ref, wmo_ref, wout_ref,
             bg_ref, gp_ref, dzg_ref, dca_ref, doa_ref, dom_ref, dwco_ref, dwao_ref, dwmo_ref, dwout_ref,
             dbg_ref, dgp_ref):
        accs = (dwco_ref, dwao_ref, dwmo_ref, dwout_ref, dbg_ref, dgp_ref)

        @pl.when(pl.program_id(0) == 0)
        def _():
            for a in accs:
                a[...] = jnp.zeros_like(a)

        dt, dgr = _rms_bwd_val(t_ref[...], gp_ref[...], dx_ref[...])
        dgp_ref[...] += jnp.sum(dgr, axis=0, keepdims=True)
        dtb = dt.astype(BF16)
        dwout_ref[...] += _dot_tn(mg_ref[...], dtb)
        dm = _dot_nt(dtb, wout_ref[...])
        ys, gs = _branches(ca_ref, oa_ref, om_ref, wco_ref, wao_ref, wmo_ref, zg_ref, bg_ref)
        for b, (act_ref, w_ref, dw_ref, da_ref) in enumerate(
                ((ca_ref, wco_ref, dwco_ref, dca_ref), (oa_ref, wao_ref, dwao_ref, doa_ref),
                 (om_ref, wmo_ref, dwmo_ref, dom_ref))):
            dzg = dm * ys[b] * gs[b] * (1.0 - gs[b])
            dzg_ref[:, b * D:(b + 1) * D] = dzg.astype(BF16)
            dbg_ref[:, b * D:(b + 1) * D] += jnp.sum(dzg, axis=0, keepdims=True)
            dy = (dm * gs[b]).astype(BF16)
            dw_ref[...] += _dot_tn(act_ref[...], dy)
            da_ref[...] = _dot_nt(dy, w_ref[...]).astype(BF16)

    full = lambda a: pl.BlockSpec(a.shape, _fixed)
    fullf = lambda a: jax.ShapeDtypeStruct(a.shape, F32)
    return pl.pallas_call(
        body, name=name, grid=(S // T,),
        in_specs=[pl.BlockSpec((T, D), _row), pl.BlockSpec((T, D), _row), pl.BlockSpec((T, D), _row),
                  pl.BlockSpec((T, CW), _row), pl.BlockSpec((T, GW), _row), pl.BlockSpec((T, MW), _row)]
        + [_gate_spec(T), full(wco), full(wao), full(wmo), full(wout), full(bgate), full(gpost)],
        out_specs=[_gate_spec(T), pl.BlockSpec((T, CW), _row), pl.BlockSpec((T, GW), _row),
                   pl.BlockSpec((T, MW), _row), full(wco), full(wao), full(wmo), full(wout), full(bgate), full(gpost)],
        out_shape=[jax.ShapeDtypeStruct((S, NIN), BF16), jax.ShapeDtypeStruct((S, CW), BF16),
                   jax.ShapeDtypeStruct((S, GW), BF16), jax.ShapeDtypeStruct((S, MW), BF16),
                   fullf(wco), fullf(wao), fullf(wmo), fullf(wout), fullf(bgate), fullf(gpost)],
        compiler_params=_params(("arbitrary",)),
    )(dx1, t, mg, cact, oatt, om, z, wco, wao, wmo, wout, bgate, gpost)


FFN_T = 256


def _ffn_fwd(x1, gu, wfo, gpost, name):
    S = x1.shape[0]
    T = FFN_T

    def body(x_ref, gu_ref, w_ref, gp_ref, x2_ref, f_ref):
        gv = gu_ref[:, :FH].astype(F32)
        uv = gu_ref[:, FH:].astype(F32)
        act = (gv * _sigmoid(gv) * uv).astype(BF16)
        f = _dot(act, w_ref[...])
        f_ref[...] = f
        x2_ref[...] = x_ref[...] + _rms_fwd_val(f, gp_ref[...])

    return pl.pallas_call(
        body, name=name, grid=(S // T,),
        in_specs=[pl.BlockSpec((T, D), _row), pl.BlockSpec((T, 2 * FH), _row), pl.BlockSpec((FH, D), _fixed),
                  pl.BlockSpec((1, D), _fixed)],
        out_specs=[pl.BlockSpec((T, D), _row)] * 2,
        out_shape=[jax.ShapeDtypeStruct((S, D), F32)] * 2,
        compiler_params=_params(("parallel",)),
    )(x1, gu, wfo, gpost)


def _ffn_bwd(dx2, f, gu, wfo, gpost, name):
    S = dx2.shape[0]
    T = FFN_T

    def body(dx_ref, f_ref, gu_ref, w_ref, gp_ref, dgu_ref, df_ref, act_ref, dgp_ref):
        @pl.when(pl.program_id(0) == 0)
        def _():
            dgp_ref[...] = jnp.zeros_like(dgp_ref)

        df, dgr = _rms_bwd_val(f_ref[...], gp_ref[...], dx_ref[...])
        dgp_ref[...] += jnp.sum(dgr, axis=0, keepdims=True)
        dfb = df.astype(BF16)
        df_ref[...] = dfb
        dact = _dot_nt(dfb, w_ref[...])
        gv = gu_ref[:, :FH].astype(F32)
        uv = gu_ref[:, FH:].astype(F32)
        sg = _sigmoid(gv)
        silu = gv * sg
        act_ref[...] = (silu * uv).astype(BF16)
        dgu_ref[:, :FH] = (dact * uv * (sg * (1.0 + gv * (1.0 - sg)))).astype(BF16)
        dgu_ref[:, FH:] = (dact * silu).astype(BF16)

    return pl.pallas_call(
        body, name=name, grid=(S // T,),
        in_specs=[pl.BlockSpec((T, D), _row), pl.BlockSpec((T, D), _row), pl.BlockSpec((T, 2 * FH), _row),
                  pl.BlockSpec((FH, D), _fixed), pl.BlockSpec((1, D), _fixed)],
        out_specs=[pl.BlockSpec((T, 2 * FH), _row), pl.BlockSpec((T, D), _row), pl.BlockSpec((T, FH), _row),
                   pl.BlockSpec((1, D), _fixed)],
        out_shape=[jax.ShapeDtypeStruct((S, 2 * FH), BF16), jax.ShapeDtypeStruct((S, D), BF16),
                   jax.ShapeDtypeStruct((S, FH), BF16), jax.ShapeDtypeStruct((1, D), F32)],
        compiler_params=_params(("arbitrary",)),
    )(dx2, f, gu, wfo, gpost)


def _loss_head(y, target, name):
    S = y.shape[0]
    T = 512

    def body(y_ref, t_ref, dy_ref, l_ref):
        @pl.when(pl.program_id(0) == 0)
        def _():
            l_ref[...] = jnp.zeros_like(l_ref)

        e = y_ref[...] - t_ref[...]
        dy_ref[...] = e * (1.0 / D)
        l_ref[...] += (0.5 / D) * jnp.sum(jnp.sum(e * e, axis=1, keepdims=True), axis=0, keepdims=True)

    return pl.pallas_call(
        body, name=name, grid=(S // T,),
        in_specs=[pl.BlockSpec((T, D), _row)] * 2,
        out_specs=[pl.BlockSpec((T, D), _row), pl.BlockSpec((8, 128), _fixed)],
        out_shape=[jax.ShapeDtypeStruct((S, D), F32), jax.ShapeDtypeStruct((8, 128), F32)],
        compiler_params=_params(("arbitrary",)),
    )(y, target)


BIG = ("w_in", "w_conv_out", "w_att_out", "w_mem_kv", "w_mem_out", "w_out", "w_ffn_in", "w_ffn_out")
SMALL = ("rel_bias", "norm_mix_pre", "b_gate", "conv_dw_bias", "conv_ln_g", "conv_ln_b", "norm_mem",
         "norm_mix_post", "norm_ffn_pre", "norm_ffn_post")


def _layer_fwd(l, x, mem, w, rel_bias):
    tag = f"_l{l}"
    h = _rms_h(x, w["norm_mix_pre"], "rms_mix" + tag)
    z = _mm_nn(h, w["w_in"], 1024, 768, BF16, "mm_in" + tag)
    yc, cact = _conv_fwd(z, w["conv_dw"], w["conv_dw_bias"], w["conv_ln_g"], w["conv_ln_b"], "conv_fwd" + tag)
    og, lg = zip(*[_att_fwd(z, rel_bias, g, f"att_fwd_g{g}" + tag) for g in range(3)])
    oatt, lse = _att_combine(og, lg, "att_combine" + tag)
    hm, kv = _memkv_fwd(mem, w["norm_mem"], w["w_mem_kv"], "memkv_fwd" + tag)
    om = _memattn_fwd(z, kv, "memattn_fwd" + tag)
    x1, mg, t = _merge_fwd(x, cact, oatt, om, z, w["w_conv_out"], w["w_att_out"], w["w_mem_out"], w["w_out"],
                           w["b_gate"], w["norm_mix_post"], "merge_fwd" + tag)
    h2 = _rms_h(x1, w["norm_ffn_pre"], "rms_ffn" + tag)
    gu = _mm_nn(h2, w["w_ffn_in"], 1024, 512, BF16, "mm_ffn_in" + tag)
    x2, f = _ffn_fwd(x1, gu, w["w_ffn_out"], w["norm_ffn_post"], "ffn_fwd" + tag)
    saved = dict(x=x, h=h, z=z, yc=yc, cact=cact, oatt=oatt, lse=lse, hm=hm, kv=kv, om=om, x1=x1, mg=mg, t=t,
                 h2=h2, gu=gu, f=f)
    return x2, saved


def _layer_bwd(l, dx2, mem, w, rel_bias, s):
    tag = f"_l{l}"
    gr = {}
    dgu, df, act, gr["norm_ffn_post"] = _ffn_bwd(dx2, s["f"], s["gu"], w["w_ffn_out"], w["norm_ffn_post"], "ffn_bwd" + tag)
    gr["w_ffn_out"] = _mm_tn(act, df, 1024, 512, "dw_ffn_out" + tag)
    gr["w_ffn_in"] = _mm_tn(s["h2"], dgu, 1024, 1408, "dw_ffn_in" + tag)
    dh2 = _mm_nt(dgu, w["w_ffn_in"], 1024, 1408, F32, "dh_ffn" + tag)
    dx1, gr["norm_ffn_pre"] = _rms_bwd(s["x1"], w["norm_ffn_pre"], dh2, dx2, "rms_ffn_bwd" + tag)
    (dz, dcact, doatt, dom, gr["w_conv_out"], gr["w_att_out"], gr["w_mem_out"], gr["w_out"], gr["b_gate"],
     gr["norm_mix_post"]) = _merge_bwd(dx1, s["t"], s["mg"], s["cact"], s["oatt"], s["om"], s["z"], w["w_conv_out"],
                                       w["w_att_out"], w["w_mem_out"], w["w_out"], w["b_gate"], w["norm_mix_post"],
                                       "merge_bwd" + tag)
    dyc, gr["conv_ln_g"], gr["conv_ln_b"], gr["conv_dw_bias"] = _conv_bwd_ln(
        s["yc"], dcact, w["conv_ln_g"], w["conv_ln_b"], "conv_bwd_ln" + tag)
    dz, dwdw = _conv_bwd_dw(s["z"], dyc, w["conv_dw"], dz, "conv_bwd_dw" + tag)
    gr["conv_dw"] = dwdw[:KSIZE]
    ld = _att_prep(doatt, s["oatt"], s["lse"], "att_prep" + tag)
    drb = []
    for g in range(3):
        dz, db = _att_bwd(s["z"], rel_bias, doatt, ld, dz, g, f"att_bwd_g{g}" + tag)
        drb.append(db)
    dz, dkv = _memattn_bwd(s["z"], s["kv"], dom, dz, "memattn_bwd" + tag)
    gr["w_mem_kv"], gr["norm_mem"] = _memkv_bwd(mem, w["norm_mem"], s["hm"], w["w_mem_kv"], dkv, "memkv_bwd" + tag)
    gr["w_in"] = _mm_tn(s["h"], dz, 1024, 1152, "dw_in" + tag)
    dh = _mm_nt(dz, w["w_in"], 1024, 2304, F32, "dh_in" + tag)
    dx, gr["norm_mix_pre"] = _rms_bwd(s["x"], w["norm_mix_pre"], dh, dx1, "rms_mix_bwd" + tag)
    return dx, gr, drb


def _rel_bias_total(parts, name):
    def body(*refs):
        out_ref = refs[-1]
        acc = jnp.zeros((NUM_BUCKETS, 128), F32)
        for l in range(DEPTH):
            for g in range(3):
                v = refs[l * 3 + g][...]
                acc = acc + (v if g == 0 else pltpu.roll(v, HPG * g, axis=1))
        out_ref[...] = acc

    return pl.pallas_call(body, name=name, out_shape=jax.ShapeDtypeStruct((NUM_BUCKETS, 128), F32),
                          compiler_params=_params())(*[p for layer in parts for p in layer])


def _local_step(x, mem, target, rel_bias, layers):
    saved = []
    for l in range(DEPTH):
        x, s = _layer_fwd(l, x, mem, layers[l], rel_bias)
        saved.append(s)
    dy, lpart = _loss_head(x, target, "loss_head")
    grads = [None] * DEPTH
    drb = [None] * DEPTH
    for l in reversed(range(DEPTH)):
        dy, grads[l], drb[l] = _layer_bwd(l, dy, mem, layers[l], rel_bias, saved[l])
    return lpart[0, 0], dy, grads, _rel_bias_total(drb, "rel_bias_total")


def _z_cols_from_ref(w):
    att = [w[..., R_ATT + (3 * j + g) * GW:R_ATT + (3 * j + g + 1) * GW] for g in range(3) for j in range(3)]
    return jnp.concatenate([w[..., R_GATE:], w[..., :C1], w[..., R_MEM:R_GATE]] + att, axis=-1)


def _ref_cols_from_z(w):
    att = [w[..., Z_ATT + (3 * g + j) * GW:Z_ATT + (3 * g + j + 1) * GW] for j in range(3) for g in range(3)]
    return jnp.concatenate([w[..., Z_CONV:Z_MEM]] + att + [w[..., Z_MEM:Z_ATT], w[..., Z_GATE:Z_CONV]], axis=-1)


N_CHIPS = 4
SHARD = {"w_in": ((D, NIN // 4), 1), "w_conv_out": ((CW, D // 4), 1), "w_att_out": ((GW, D // 4), 1),
         "w_mem_kv": ((D // 4, 2 * MW), 0), "w_mem_out": ((MW, D // 4), 1), "w_out": ((D // 4, D), 0),
         "w_ffn_in": ((D, 2 * FH // 4), 1), "w_ffn_out": ((FH // 4, D), 0)}
PACK_ROWS = {n: SHARD[n][0][0] * SHARD[n][0][1] // D for n in BIG}
LROWS = sum(PACK_ROWS.values())
CDW_ROWS = 64
VEC_ROWS = (("norm_mix_pre", 1), ("b_gate", 3), ("conv_dw_bias", 1), ("conv_ln_g", 1), ("conv_ln_b", 1),
            ("norm_mem", 1), ("norm_mix_post", 1), ("norm_ffn_pre", 1), ("norm_ffn_post", 1))
VEC_LROWS = sum(r for _, r in VEC_ROWS)
REL_ROW = DEPTH * VEC_LROWS
CDW_ROW = REL_ROW + 1
CDW_GROWS = DEPTH * KSIZE * CW // D
SMALL_ROWS = -(-(CDW_ROW + CDW_GROWS) // 8) * 8


def _mesh_pos():
    return lax.axis_index("x"), lax.axis_index("y"), lax.axis_index("c")


def _other_chips(x, y):
    chips = [(1 - x, y), (x, 1 - y), (1 - x, 1 - y)]
    return chips, [2 * cx + cy for cx, cy in chips]


NBIG = len(BIG)
ANY_SPEC = pl.BlockSpec(memory_space=pl.ANY)


def _remote(src, dst, send_sems, recv_sems, k, to):
    return pltpu.make_async_remote_copy(src_ref=src, dst_ref=dst, send_sem=send_sems.at[k], recv_sem=recv_sems.at[k],
                                        device_id=to, device_id_type=MESH)


def _all_gather(ws, cdw):
    def body(*refs):
        w_refs, cdw_ref = refs[:NBIG], refs[NBIG]
        g_refs, gc_ref = refs[NBIG + 1:2 * NBIG + 1], refs[2 * NBIG + 1]
        send_sems, recv_sems = refs[2 * NBIG + 2:]
        x, y, c = _mesh_pos()
        j = 2 * x + y
        sibling = (x, y, 1 - c)
        chips, blocks = _other_chips(x, y)
        copy = functools.partial(_remote, send_sems=send_sems, recv_sems=recv_sems)
        pairs = list(zip(w_refs, g_refs))
        first = [copy(w.at[c], g.at[j, c], k=k * NBIG + n, to=(*chip, c))
                 for k, chip in enumerate(chips) for n, (w, g) in enumerate(pairs)]
        first += [copy(cdw_ref, gc_ref.at[j], k=6 * NBIG + k, to=(*chip, c)) for k, chip in enumerate(chips)]
        for cp in first:
            cp.start()
        passed = []
        for k, b in enumerate(blocks):
            for n, (w, g) in enumerate(pairs):
                copy(w.at[c], g.at[b, c], k=k * NBIG + n, to=sibling).wait_recv()
            onward = [copy(g.at[b, c], g.at[b, c], k=(3 + k) * NBIG + n, to=sibling) for n, (w, g) in enumerate(pairs)]
            for cp in onward:
                cp.start()
            passed += onward
        for k, b in enumerate(blocks):
            for n, (w, g) in enumerate(pairs):
                copy(w.at[c], g.at[b, 1 - c], k=(3 + k) * NBIG + n, to=sibling).wait_recv()
            copy(cdw_ref, gc_ref.at[b], k=6 * NBIG + k, to=sibling).wait_recv()
        for cp in first + passed:
            cp.wait_send()

    nsem = 6 * NBIG + 3
    return pl.pallas_call(
        body, name="all_gather_weights",
        out_shape=[jax.ShapeDtypeStruct((N_CHIPS,) + w.shape, BF16) for w in ws]
        + [jax.ShapeDtypeStruct((N_CHIPS, CDW_ROWS, 128), F32)],
        in_specs=[ANY_SPEC] * (NBIG + 1), out_specs=[ANY_SPEC] * (NBIG + 1),
        scratch_shapes=[pltpu.SemaphoreType.DMA((nsem,)), pltpu.SemaphoreType.DMA((nsem,))],
    )(*ws, cdw)


def _sibling_exchange(ps):
    def body(*refs):
        p_refs, r_refs, (send_sems, recv_sems) = refs[:NBIG], refs[NBIG:2 * NBIG], refs[2 * NBIG:]
        x, y, c = _mesh_pos()
        cps = [_remote(p.at[1 - c], r, send_sems, recv_sems, n, (x, y, 1 - c))
               for n, (p, r) in enumerate(zip(p_refs, r_refs))]
        for cp in cps:
            cp.start()
        for cp in cps:
            cp.wait()

    return pl.pallas_call(
        body, name="grad_sibling_exchange", out_shape=[jax.ShapeDtypeStruct(p.shape[1:], p.dtype) for p in ps],
        in_specs=[ANY_SPEC] * NBIG, out_specs=[ANY_SPEC] * NBIG,
        scratch_shapes=[pltpu.SemaphoreType.DMA((NBIG,)), pltpu.SemaphoreType.DMA((NBIG,))],
    )(*ps)


SUM_BLOCK_BYTES = 2 * 1024 * 1024


def _sum_rows(s0, s1):
    return s0 if s0 * s1 * 2 <= SUM_BLOCK_BYTES else s0 // 2


def _add_own_layer(where, p, r, name):
    _, _, s0, s1 = p.shape
    T = _sum_rows(s0, s1)

    def body(where_ref, p_ref, r_ref, o_ref):
        o_ref[...] = (p_ref[0].astype(F32) + r_ref[...].astype(F32)).astype(BF16)

    return pl.pallas_call(
        body, name=name,
        grid_spec=pltpu.PrefetchScalarGridSpec(
            num_scalar_prefetch=1, grid=(N_CHIPS, s0 // T),
            in_specs=[pl.BlockSpec((1, 1, T, s1), lambda j, i, wh: (wh[0], j, i, 0)),
                      pl.BlockSpec((1, T, s1), lambda j, i, wh: (j, i, 0))],
            out_specs=pl.BlockSpec((1, T, s1), lambda j, i, wh: (j, i, 0))),
        out_shape=jax.ShapeDtypeStruct(r.shape, BF16), compiler_params=_params(("parallel", "parallel")),
    )(where, p, r)


def _chip_exchange(as_):
    def body(*refs):
        a_refs, r_refs, (send_sems, recv_sems) = refs[:NBIG], refs[NBIG:2 * NBIG], refs[2 * NBIG:]
        x, y, c = _mesh_pos()
        chips, blocks = _other_chips(x, y)
        cps = [_remote(a.at[b], r.at[k], send_sems, recv_sems, k * NBIG + n, (*chip, c))
               for k, (chip, b) in enumerate(zip(chips, blocks)) for n, (a, r) in enumerate(zip(a_refs, r_refs))]
        for cp in cps:
            cp.start()
        for cp in cps:
            cp.wait_recv()
        for cp in cps:
            cp.wait_send()

    return pl.pallas_call(
        body, name="grad_chip_exchange", out_shape=[jax.ShapeDtypeStruct((3,) + a.shape[1:], a.dtype) for a in as_],
        in_specs=[ANY_SPEC] * NBIG, out_specs=[ANY_SPEC] * NBIG,
        scratch_shapes=[pltpu.SemaphoreType.DMA((3 * NBIG,)), pltpu.SemaphoreType.DMA((3 * NBIG,))],
    )(*as_)


def _sum_chips(where, a, r, name):
    _, s0, s1 = a.shape
    T = _sum_rows(s0, s1)

    def body(where_ref, a_ref, r_ref, o_ref):
        acc = a_ref[0].astype(F32)
        for k in range(3):
            acc = acc + r_ref[k].astype(F32)
        o_ref[0] = acc

    return pl.pallas_call(
        body, name=name,
        grid_spec=pltpu.PrefetchScalarGridSpec(
            num_scalar_prefetch=1, grid=(s0 // T,),
            in_specs=[pl.BlockSpec((1, T, s1), lambda i, wh: (wh[1], i, 0)),
                      pl.BlockSpec((3, T, s1), lambda i, wh: (0, i, 0))],
            out_specs=pl.BlockSpec((1, T, s1), lambda i, wh: (wh[0], i, 0))),
        out_shape=jax.ShapeDtypeStruct((DEPTH, s0, s1), F32), compiler_params=_params(("parallel",)),
    )(where, a, r)


def _sibling_share(os_):
    def body(*refs):
        o_refs, (send_sems, recv_sems) = refs[NBIG:2 * NBIG], refs[2 * NBIG:]
        x, y, c = _mesh_pos()
        cps = [_remote(o.at[c], o.at[c], send_sems, recv_sems, n, (x, y, 1 - c)) for n, o in enumerate(o_refs)]
        for cp in cps:
            cp.start()
        for n, o in enumerate(o_refs):
            _remote(o.at[c], o.at[1 - c], send_sems, recv_sems, n, (x, y, 1 - c)).wait_recv()
        for cp in cps:
            cp.wait_send()

    return pl.pallas_call(
        body, name="grad_sibling_share", out_shape=[jax.ShapeDtypeStruct(o.shape, o.dtype) for o in os_],
        in_specs=[ANY_SPEC] * NBIG, out_specs=[ANY_SPEC] * NBIG,
        input_output_aliases={n: n for n in range(NBIG)},
        scratch_shapes=[pltpu.SemaphoreType.DMA((NBIG,)), pltpu.SemaphoreType.DMA((NBIG,))],
    )(*os_)


def _all_reduce_small(sp):
    def body(sp_ref, out_ref, buf, send_sems, recv_sems):
        x, y, c = _mesh_pos()
        me = 4 * x + 2 * y + c
        buf[0] = sp_ref[...]
        cps = []
        for k in range(1, 8):
            peer = (x ^ (k >> 2 & 1), y ^ (k >> 1 & 1), c ^ (k & 1))
            cps.append(pltpu.make_async_remote_copy(src_ref=sp_ref, dst_ref=buf.at[k], send_sem=send_sems.at[k - 1],
                                                    recv_sem=recv_sems.at[k - 1], device_id=peer, device_id_type=MESH))
        for cp in cps:
            cp.start()
        for cp in cps:
            cp.wait_recv()
        for cp in cps:
            cp.wait_send()
        acc = buf[me]
        for p in range(1, 8):
            acc = acc + buf[p ^ me]
        out_ref[...] = acc

    vm = pl.BlockSpec(memory_space=pltpu.VMEM)
    return pl.pallas_call(
        body, name="all_reduce_small", out_shape=jax.ShapeDtypeStruct(sp.shape, F32),
        in_specs=[vm], out_specs=vm,
        scratch_shapes=[pltpu.VMEM((8,) + sp.shape, F32), pltpu.SemaphoreType.DMA((7,)), pltpu.SemaphoreType.DMA((7,))],
        compiler_params=_params(),
    )(sp)


def _adamw(w, g, m, v, name):
    R, C = w.shape
    T = next((t for t in (256, 128) if R % t == 0), R)

    def body(w_ref, g_ref, m_ref, v_ref, d_ref, m2_ref, v2_ref):
        gv = g_ref[...]
        m2 = ADAM_B1 * m_ref[...] + (1.0 - ADAM_B1) * gv
        v2 = ADAM_B2 * v_ref[...] + (1.0 - ADAM_B2) * (gv * gv)
        m_hat = m2 / (1.0 - ADAM_B1 ** ADAM_STEP)
        v_hat = v2 / (1.0 - ADAM_B2 ** ADAM_STEP)
        d_ref[...] = -ADAM_LR * (m_hat / (jnp.sqrt(v_hat) + ADAM_EPS) + ADAM_WD * w_ref[...])
        m2_ref[...] = m2
        v2_ref[...] = v2

    blk = pl.BlockSpec((T, C), _row)
    return pl.pallas_call(
        body, name=name, grid=(R // T,), in_specs=[blk] * 4, out_specs=[blk] * 3,
        out_shape=[jax.ShapeDtypeStruct((R, C), F32)] * 3, compiler_params=_params(("parallel",)),
    )(w, g, m, v)


def _pack_vectors(get, rel, cdw):
    rows = []
    for l in range(DEPTH):
        for n, r in VEC_ROWS:
            v = get(n)[l]
            rows.append(jnp.pad(v, (0, r * D - v.shape[0])).reshape(r, D))
    rows.append(jnp.pad(rel.reshape(-1), (0, D - NUM_BUCKETS * 3 * HPG)).reshape(1, D))
    rows.append(cdw.reshape(CDW_GROWS, D))
    rows.append(jnp.zeros((SMALL_ROWS - CDW_ROW - CDW_GROWS, D), F32))
    return jnp.concatenate(rows, axis=0)


def _unpack_vectors(packed, lens):
    out = {n: [] for n, _ in VEC_ROWS}
    for l in range(DEPTH):
        at = l * VEC_LROWS
        for n, r in VEC_ROWS:
            out[n].append(packed[at:at + r].reshape(-1)[:lens[n]])
            at += r
    rel = packed[REL_ROW, :NUM_BUCKETS * 3 * HPG].reshape(NUM_BUCKETS, 3 * HPG)
    return {n: jnp.stack(v) for n, v in out.items()}, rel


INPUT_NAMES = ("x", "mem") + ("rel_bias", "norm_mix_pre", "w_in", "b_gate", "conv_dw", "conv_dw_bias", "conv_ln_g",
                              "conv_ln_b", "w_conv_out", "w_att_out", "norm_mem", "w_mem_kv", "w_mem_out", "w_out",
                              "norm_mix_post", "norm_ffn_pre", "w_ffn_in", "w_ffn_out", "norm_ffn_post")
WEIGHT_NAMES = INPUT_NAMES[2:]


def kernel(*args):
    nw = len(WEIGHT_NAMES)
    a = dict(zip(INPUT_NAMES, args[:2 + nw]))
    target = args[2 + nw]
    mom = dict(zip(WEIGHT_NAMES, args[3 + nw:3 + 2 * nw]))
    var = dict(zip(WEIGHT_NAMES, args[3 + 2 * nw:3 + 3 * nw]))
    xi, yi, ci = _mesh_pos()
    chip = 2 * xi + yi
    where = jnp.stack([ci, chip]).astype(I32)

    shards = [a[n].astype(BF16) for n in BIG]
    cdw = jnp.pad(a["conv_dw"].reshape(DEPTH * KSIZE, CW // 4), ((0, CDW_ROWS - DEPTH * KSIZE), (0, 0)))
    *gathered, gcdw = _all_gather(shards, cdw)
    gathered = [lax.dynamic_update_slice(g, s[None], (chip, 0, 0, 0)) for g, s in zip(gathered, shards)]
    gcdw = lax.dynamic_update_slice(gcdw, cdw[None], (chip, 0, 0))
    conv_dw = gcdw[:, :DEPTH * KSIZE].reshape(N_CHIPS, DEPTH, KSIZE, CW // 4).transpose(1, 2, 0, 3)
    conv_dw = jnp.pad(conv_dw.reshape(DEPTH, KSIZE, CW), ((0, 0), (0, 1), (0, 0)))
    layers = []
    for l in range(DEPTH):
        w = {"conv_dw": conv_dw[l]}
        for n, g in zip(BIG, gathered):
            (s0, s1), axis = SHARD[n]
            blk = g[:, l]
            w[n] = blk.reshape(N_CHIPS * s0, s1) if axis == 0 else blk.transpose(1, 0, 2).reshape(s0, N_CHIPS * s1)
        w["w_in"] = _z_cols_from_ref(w["w_in"])
        for n, _ in VEC_ROWS:
            w[n] = a[n][l][None, :]
        layers.append(w)

    loss_part, gx, grads, drel = _local_step(a["x"][0], a["mem"][0], target[0], a["rel_bias"], layers)
    loss = lax.psum(loss_part, ("x", "y", "c"))

    packed = []
    for n in BIG:
        (s0, s1), axis = SHARD[n]
        per_layer = []
        for l in range(DEPTH):
            g = _ref_cols_from_z(grads[l][n]) if n == "w_in" else grads[l][n]
            per_layer.append(g.reshape(N_CHIPS, s0, s1) if axis == 0 else g.reshape(s0, N_CHIPS, s1).transpose(1, 0, 2))
        packed.append(jnp.stack(per_layer).astype(BF16))
    from_sibling = _sibling_exchange(packed)
    chip_sums = [_add_own_layer(where, p, r, "grad_add_sibling_" + n) for n, p, r in zip(BIG, packed, from_sibling)]
    from_chips = _chip_exchange(chip_sums)
    reduced = _sibling_share([_sum_chips(where, s, r, "grad_sum_chips_" + n)
                              for n, s, r in zip(BIG, chip_sums, from_chips)])

    gvec = _all_reduce_small(_pack_vectors(
        lambda n: jnp.stack([grads[l][n][0] for l in range(DEPTH)]), drel[:, :3 * HPG],
        jnp.stack([grads[l]["conv_dw"] for l in range(DEPTH)])))
    lens = {n: a[n].shape[1] for n, _ in VEC_ROWS}
    g_vec, g_rel = _unpack_vectors(gvec, lens)
    g_cdw = lax.dynamic_slice_in_dim(gvec[CDW_ROW:CDW_ROW + CDW_GROWS].reshape(DEPTH, KSIZE, CW), chip * (CW // 4),
                                     CW // 4, axis=2)

    grad, delta, new_m, new_v = {}, {}, {}, {}
    for n, g in zip(BIG, reduced):
        shape = a[n].shape
        flat2 = lambda t: t.reshape(shape[0] * shape[1], shape[2])
        d, m2, v2 = _adamw(flat2(a[n]), flat2(g), flat2(mom[n]), flat2(var[n]), "adamw_" + n)
        grad[n], delta[n], new_m[n], new_v[n] = g, d.reshape(shape), m2.reshape(shape), v2.reshape(shape)
    shape = a["conv_dw"].shape
    flat2 = lambda t: t.reshape(shape[0] * shape[1], shape[2])
    d, m2, v2 = _adamw(flat2(a["conv_dw"]), flat2(g_cdw), flat2(mom["conv_dw"]), flat2(var["conv_dw"]), "adamw_conv_dw")
    grad["conv_dw"], delta["conv_dw"], new_m["conv_dw"], new_v["conv_dw"] = (
        g_cdw, d.reshape(shape), m2.reshape(shape), v2.reshape(shape))
    zero_cdw = jnp.zeros((DEPTH, KSIZE, CW), F32)
    pk = lambda src: _pack_vectors(lambda n: src[n], src["rel_bias"], zero_cdw)
    d, m2, v2 = _adamw(pk(a), gvec, pk(mom), pk(var), "adamw_vectors")
    for src, dst in ((d, delta), (m2, new_m), (v2, new_v)):
        vec, rel = _unpack_vectors(src, lens)
        dst.update(vec)
        dst["rel_bias"] = rel
    grad.update(g_vec)
    grad["rel_bias"] = g_rel

    outs = [loss, gx[None]]
    for group in (grad, delta, new_m, new_v):
        outs += [group[n] for n in WEIGHT_NAMES]
    return tuple(outs)
```

```python
import functools
import math

import jax
import jax.numpy as jnp
from jax import lax
from jax.experimental import pallas as pl
from jax.experimental.pallas import tpu as pltpu

F32 = jnp.float32
BF16 = jnp.bfloat16
I32 = jnp.int32

D = 1024
DEPTH = 2
N_MEM = 256
CW = 512
KSIZE = 31
PAD = KSIZE // 2
DILS = (1, 4, 16)
RADIUS = 64
HPG = 4
HD = 64
GW = HPG * HD
MH = 4
MHD = 128
MW = MH * MHD
FH = 2816
NIN = 6912
C1 = 2 * CW
R_ATT = C1
R_MEM = R_ATT + 9 * GW
R_GATE = R_MEM + MW
Z_GATE = 0
Z_CONV = 3 * D
Z_MEM = Z_CONV + C1
Z_ATT = Z_MEM + MW
NUM_BUCKETS = 32
MAX_DISTANCE = 1024
RMS_EPS = 1e-6
LN_EPS = 1e-5
NEG_INF = -1e30
ATT_SCALE = HD ** -0.5
MEM_SCALE = MHD ** -0.5

ADAM_LR = 0.001
ADAM_B1 = 0.9
ADAM_B2 = 0.999
ADAM_EPS = 1e-08
ADAM_WD = 0.01
ADAM_STEP = 10

VMEM_LIMIT_BYTES = 56 * 1024 * 1024
ATT_QB = 128
ATT_TB = 16 * ATT_QB

MESH = pl.DeviceIdType.MESH


def _params(sem=None):
    return pltpu.CompilerParams(dimension_semantics=sem, vmem_limit_bytes=VMEM_LIMIT_BYTES)


def _sigmoid(v):
    return 1.0 / (1.0 + jnp.exp(-v))


def _dot(a, b):
    return jnp.dot(a, b, preferred_element_type=F32)


def _dot_nt(a, b):
    return lax.dot_general(a, b, (((1,), (1,)), ((), ())), preferred_element_type=F32)


def _dot_tn(a, b):
    return lax.dot_general(a, b, (((0,), (0,)), ((), ())), preferred_element_type=F32)


def _rms_fwd_val(v, g):
    r = lax.rsqrt(jnp.mean(v * v, axis=-1, keepdims=True) + RMS_EPS)
    return v * r * g


def _rms_bwd_val(v, g, dy):
    r = lax.rsqrt(jnp.mean(v * v, axis=-1, keepdims=True) + RMS_EPS)
    vh = v * r
    dvh = dy * g
    dv = r * (dvh - vh * jnp.mean(dvh * vh, axis=-1, keepdims=True))
    return dv, dy * vh


def _row(i):
    return (i, 0)


def _fixed(*_):
    return (0, 0)


def _mm_nn(a, b, tm, tn, out_dtype, name):
    M, K = a.shape
    N = b.shape[1]

    def body(a_ref, b_ref, o_ref):
        o_ref[...] = _dot(a_ref[...], b_ref[...]).astype(out_dtype)

    return pl.pallas_call(
        body, name=name, grid=(N // tn, M // tm),
        in_specs=[pl.BlockSpec((tm, K), lambda j, i: (i, 0)), pl.BlockSpec((K, tn), lambda j, i: (0, j))],
        out_specs=pl.BlockSpec((tm, tn), lambda j, i: (i, j)),
        out_shape=jax.ShapeDtypeStruct((M, N), out_dtype),
        compiler_params=_params(("parallel", "parallel")),
    )(a, b)


def _mm_nt_rms_bwd(a, b, x, g, dres, tm, tc, name):
    M, N = a.shape
    nk = N // tc

    def body(a_ref, b_ref, x_ref, g_ref, dres_ref, dx_ref, dg_ref, acc_ref):
        i, k = pl.program_id(0), pl.program_id(1)

        @pl.when(k == 0)
        def _():
            acc_ref[...] = jnp.zeros_like(acc_ref)

            @pl.when(i == 0)
            def _():
                dg_ref[...] = jnp.zeros_like(dg_ref)

        acc_ref[...] += _dot_nt(a_ref[...], b_ref[...])

        @pl.when(k == nk - 1)
        def _():
            dv, dgr = _rms_bwd_val(x_ref[...], g_ref[...], acc_ref[...])
            dx_ref[...] = dres_ref[...] + dv
            dg_ref[...] += jnp.sum(dgr, axis=0, keepdims=True)

    rows = pl.BlockSpec((tm, D), lambda i, k: (i, 0))
    return pl.pallas_call(
        body, name=name, grid=(M // tm, nk),
        in_specs=[pl.BlockSpec((tm, tc), lambda i, k: (i, k)), pl.BlockSpec((D, tc), lambda i, k: (0, k)), rows,
                  pl.BlockSpec((1, D), _fixed), rows],
        out_specs=[rows, pl.BlockSpec((1, D), _fixed)],
        out_shape=[jax.ShapeDtypeStruct((M, D), F32), jax.ShapeDtypeStruct((1, D), F32)],
        scratch_shapes=[pltpu.VMEM((tm, D), F32)],
        compiler_params=_params(("arbitrary", "arbitrary")),
    )(a, b, x, g, dres)


def _mm_tn(a, b, ts, tn, name):
    S, K = a.shape
    N = b.shape[1]

    def body(a_ref, b_ref, o_ref):
        @pl.when(pl.program_id(1) == 0)
        def _():
            o_ref[...] = jnp.zeros_like(o_ref)

        o_ref[...] += _dot_tn(a_ref[...], b_ref[...])

    return pl.pallas_call(
        body, name=name, grid=(N // tn, S // ts),
        in_specs=[pl.BlockSpec((ts, K), lambda j, s: (s, 0)), pl.BlockSpec((ts, tn), lambda j, s: (s, j))],
        out_specs=pl.BlockSpec((K, tn), lambda j, s: (0, j)),
        out_shape=jax.ShapeDtypeStruct((K, N), F32),
        compiler_params=_params(("parallel", "arbitrary")),
    )(a, b)


def _rms_h(x, g, name):
    S = x.shape[0]
    T = 512

    def body(x_ref, g_ref, h_ref):
        h_ref[...] = _rms_fwd_val(x_ref[...], g_ref[...]).astype(BF16)

    return pl.pallas_call(
        body, name=name, grid=(S // T,),
        in_specs=[pl.BlockSpec((T, D), _row), pl.BlockSpec((1, D), _fixed)],
        out_specs=pl.BlockSpec((T, D), _row),
        out_shape=jax.ShapeDtypeStruct((S, D), BF16),
        compiler_params=_params(("parallel",)),
    )(x, g)


CONV_T = 256
CONV_HALO = 16
CONV_RC = 32


def _halo_specs(T, halo, S, width, col):
    per = T // halo
    last = S // halo - 1
    return [
        pl.BlockSpec((T, width), lambda i: (i, col)),
        pl.BlockSpec((halo, width), lambda i: (jnp.maximum(i * per - 1, 0), col)),
        pl.BlockSpec((halo, width), lambda i: (jnp.minimum((i + 1) * per, last), col)),
    ]


def _glu(zb):
    zb = zb.astype(F32)
    return zb[:, :CW] * _sigmoid(zb[:, CW:])


CONV_EXT = CONV_T + 2 * CONV_HALO
SUBLANES = 8


def _fill_shifted(sh_ref, ext_ref, cur, prev, nxt):
    T, halo = CONV_T, CONV_HALO
    i = pl.program_id(0)
    n = pl.num_programs(0)
    ext_ref[0:halo, :] = jnp.where(i > 0, prev, 0.0)
    ext_ref[halo:halo + T, :] = cur
    ext_ref[halo + T:CONV_EXT, :] = jnp.where(i < n - 1, nxt, 0.0)
    ext_ref[CONV_EXT:CONV_EXT + SUBLANES, :] = jnp.zeros((SUBLANES, CW), F32)
    for b in range(SUBLANES):
        sh_ref[b] = ext_ref[b:b + CONV_EXT, :]


def _window(sh_ref, start, rows):
    b = start % SUBLANES
    return sh_ref[b, start - b:start - b + rows, :]


def _shifted_scratch():
    return [pltpu.VMEM((CONV_EXT + SUBLANES, CW), F32), pltpu.VMEM((SUBLANES, CONV_EXT, CW), F32)]


def _conv_fwd(z, wdw, bdw, lng, lnb, name):
    S = z.shape[0]
    T, HL, RC = CONV_T, CONV_HALO, CONV_RC

    def body(cur_ref, prev_ref, next_ref, w_ref, b_ref, g_ref, bb_ref, yc_ref, act_ref, ext_ref, sh_ref):
        _fill_shifted(sh_ref, ext_ref, _glu(cur_ref[...]), _glu(prev_ref[...]), _glu(next_ref[...]))
        for c in range(T // RC):
            acc = jnp.zeros((RC, CW), F32)
            for k in range(KSIZE):
                acc = acc + w_ref[k:k + 1, :] * _window(sh_ref, c * RC + k + HL - PAD, RC)
            yc = acc + b_ref[...]
            yc_ref[c * RC:(c + 1) * RC, :] = yc
            mu = jnp.mean(yc, axis=-1, keepdims=True)
            xc = yc - mu
            ln = xc * lax.rsqrt(jnp.mean(xc * xc, axis=-1, keepdims=True) + LN_EPS) * g_ref[...] + bb_ref[...]
            act_ref[c * RC:(c + 1) * RC, :] = (ln * _sigmoid(ln)).astype(BF16)

    return pl.pallas_call(
        body, name=name, grid=(S // T,),
        in_specs=_halo_specs(T, HL, S, C1, Z_CONV // C1) + [pl.BlockSpec((32, CW), _fixed)]
        + [pl.BlockSpec((1, CW), _fixed)] * 3,
        out_specs=[pl.BlockSpec((T, CW), _row), pl.BlockSpec((T, CW), _row)],
        out_shape=[jax.ShapeDtypeStruct((S, CW), F32), jax.ShapeDtypeStruct((S, CW), BF16)],
        scratch_shapes=_shifted_scratch(),
        compiler_params=_params(("parallel",)),
    )(z, z, z, wdw, bdw, lng, lnb)


def _conv_bwd_ln(yc, dact, lng, lnb, name):
    S = yc.shape[0]
    T = 512

    def body(yc_ref, da_ref, g_ref, b_ref, dyc_ref, dg_ref, db_ref, dbias_ref):
        yc_v = yc_ref[...]
        mu = jnp.mean(yc_v, axis=-1, keepdims=True)
        xc = yc_v - mu
        r = lax.rsqrt(jnp.mean(xc * xc, axis=-1, keepdims=True) + LN_EPS)
        yn = xc * r
        ln = yn * g_ref[...] + b_ref[...]
        sg = _sigmoid(ln)
        dln = da_ref[...].astype(F32) * (sg * (1.0 + ln * (1.0 - sg)))
        dyn = dln * g_ref[...]
        dyc = r * (dyn - jnp.mean(dyn, axis=-1, keepdims=True) - yn * jnp.mean(dyn * yn, axis=-1, keepdims=True))
        dyc_ref[...] = dyc

        @pl.when(pl.program_id(0) == 0)
        def _():
            dg_ref[...] = jnp.zeros_like(dg_ref)
            db_ref[...] = jnp.zeros_like(db_ref)
            dbias_ref[...] = jnp.zeros_like(dbias_ref)

        dg_ref[...] += jnp.sum(dln * yn, axis=0, keepdims=True)
        db_ref[...] += jnp.sum(dln, axis=0, keepdims=True)
        dbias_ref[...] += jnp.sum(dyc, axis=0, keepdims=True)

    vec = pl.BlockSpec((1, CW), _fixed)
    return pl.pallas_call(
        body, name=name, grid=(S // T,),
        in_specs=[pl.BlockSpec((T, CW), _row), pl.BlockSpec((T, CW), _row), vec, vec],
        out_specs=[pl.BlockSpec((T, CW), _row), vec, vec, vec],
        out_shape=[jax.ShapeDtypeStruct((S, CW), F32)] + [jax.ShapeDtypeStruct((1, CW), F32)] * 3,
        compiler_params=_params(("arbitrary",)),
    )(yc, dact, lng, lnb)


def _conv_bwd_dw(z, dyc, wdw, dz, name):
    S = z.shape[0]
    T, HL, RC = CONV_T, CONV_HALO, CONV_RC

    def body(zc_ref, zp_ref, zn_ref, dc_ref, dp_ref, dn_ref, w_ref, dz_in, dz_ref, dw_ref, uext_ref, ush_ref,
             dext_ref, dsh_ref, dwacc_ref):
        _fill_shifted(ush_ref, uext_ref, _glu(zc_ref[...]), _glu(zp_ref[...]), _glu(zn_ref[...]))
        _fill_shifted(dsh_ref, dext_ref, dc_ref[...], dp_ref[...], dn_ref[...])

        @pl.when(pl.program_id(0) == 0)
        def _():
            dwacc_ref[...] = jnp.zeros_like(dwacc_ref)

        for c in range(T // RC):
            dcur = dc_ref[c * RC:(c + 1) * RC, :]
            du = jnp.zeros((RC, CW), F32)
            for k in range(KSIZE):
                du = du + w_ref[k:k + 1, :] * _window(dsh_ref, c * RC + HL + PAD - k, RC)
                prod = dcur * _window(ush_ref, c * RC + k + HL - PAD, RC)
                dwacc_ref[k] += jnp.sum(prod.reshape(RC // SUBLANES, SUBLANES, CW), axis=0)
            zc = zc_ref[c * RC:(c + 1) * RC, :].astype(F32)
            a, gt = zc[:, :CW], zc[:, CW:]
            sg = _sigmoid(gt)
            dz_ref[c * RC:(c + 1) * RC, 0:CW] = (du * sg).astype(BF16)
            dz_ref[c * RC:(c + 1) * RC, CW:C1] = (du * a * sg * (1.0 - sg)).astype(BF16)

        @pl.when(pl.program_id(0) == pl.num_programs(0) - 1)
        def _():
            dw_ref[...] = jnp.sum(dwacc_ref[...], axis=1)

    return pl.pallas_call(
        body, name=name, grid=(S // T,),
        in_specs=_halo_specs(T, HL, S, C1, Z_CONV // C1) + _halo_specs(T, HL, S, CW, 0)
        + [pl.BlockSpec((32, CW), _fixed), pl.BlockSpec(memory_space=pl.ANY)],
        out_specs=[pl.BlockSpec((T, C1), lambda i: (i, Z_CONV // C1)), pl.BlockSpec((32, CW), _fixed)],
        out_shape=[jax.ShapeDtypeStruct(dz.shape, BF16), jax.ShapeDtypeStruct((32, CW), F32)],
        input_output_aliases={7: 0},
        scratch_shapes=_shifted_scratch() + _shifted_scratch() + [pltpu.VMEM((32, SUBLANES, CW), F32)],
        compiler_params=_params(("arbitrary",)),
    )(z, z, z, dyc, dyc, dyc, wdw, dz)


def _t5_bucket(rel):
    nb = NUM_BUCKETS // 2
    max_exact = nb // 2
    ret = jnp.where(rel > 0, nb, 0)
    n = jnp.abs(rel)
    nf = jnp.maximum(n, 1).astype(F32)
    large = max_exact + (jnp.log(nf / max_exact) / math.log(MAX_DISTANCE / max_exact)
                         * (nb - max_exact)).astype(I32)
    large = jnp.minimum(large, nb - 1)
    return ret + jnp.where(n < max_exact, n, large)


def _offsets_qk(nq, nk, shift):
    return lax.broadcasted_iota(I32, (nq, nk), 1) + shift - lax.broadcasted_iota(I32, (nq, nk), 0)


def _bias_table(bk, rb_ref, col, off):
    acc = jnp.zeros(bk.shape, F32)
    for b in range(NUM_BUCKETS):
        acc = jnp.where(bk == b, rb_ref[b, col], acc)
    return jnp.where(jnp.abs(off) <= RADIUS, acc, NEG_INF)


def _to_halves(scr, row0, val):
    rows = val.shape[0]
    v = val.astype(F32)
    scr[0, row0:row0 + rows, :] = v[:, :128]
    scr[1, row0:row0 + rows, :] = v[:, 128:]


ATT_FWD_GROUP = 2
ATT_BWD_GROUP = 1


def _att_units(d, fn, group):
    nj = ATT_TB // (ATT_QB * d)
    if nj == 1:
        def trip(t, c):
            r0 = pl.multiple_of(t * 8, 8)
            for u in range(0, 8, group):
                fn([(r0 + u + v, 0) for v in range(group)])
            return c

        lax.fori_loop(0, d // 8, trip, 0)
        return
    for r in range(d):
        def step(t, c, r=r):
            fn([(r, t * group + u) for u in range(group)])
            return c

        lax.fori_loop(0, nj // group, step, 0)


def _unit_row(r, j, d):
    if isinstance(j, int):
        return j * ATT_QB * d + r
    return pl.multiple_of(j * (ATT_QB * d), ATT_QB) + r


def _att_fwd(z, rel_bias, g, name):
    S = z.shape[0]
    d = DILS[g]
    TB, QB = ATT_TB, ATT_QB
    H = RADIUS * d
    L = S // d
    cq = (Z_ATT + 3 * GW * g) // GW
    ck, cv = cq + 1, cq + 2
    bk = _t5_bucket(_offsets_qk(QB, 2 * QB, -RADIUS) * d)

    def body(rb_ref, bk_ref, q_ref, kc_ref, kp_ref, kn_ref, vc_ref, vp_ref, vn_ref, o_ref, l_ref,
             qs, ks, vs, os_, ls, bias):
        i = pl.program_id(0)

        @pl.when(i == 0)
        def _():
            off = _offsets_qk(QB, 2 * QB, -RADIUS)
            for h in range(HPG):
                bias[h] = _bias_table(bk_ref[...], rb_ref, g * HPG + h, off)

        _to_halves(qs, 0, q_ref[...].astype(F32) * ATT_SCALE)
        for scr, p_ref, c_ref, n_ref in ((ks, kp_ref, kc_ref, kn_ref), (vs, vp_ref, vc_ref, vn_ref)):
            _to_halves(scr, 0, p_ref[...])
            _to_halves(scr, H, c_ref[...])
            _to_halves(scr, H + TB, n_ref[...])

        lo = lax.broadcasted_iota(I32, (QB, 128), 1) < HD

        def units(rjs):
            work = []
            for r, j in rjs:
                row = _unit_row(r, j, d)
                km = lax.broadcasted_iota(I32, (1, 2 * QB), 1) + (i * (TB // d) + j * QB - RADIUS)
                edge = jnp.where(jnp.where(km >= 0, km, L) < L, 0.0, NEG_INF)
                for hf in (0, 1):
                    q2 = qs[hf, pl.ds(row, QB, stride=d), :]
                    k2 = ks[hf, pl.ds(row, 2 * QB, stride=d), :].astype(BF16)
                    v2 = vs[hf, pl.ds(row, 2 * QB, stride=d), :].astype(BF16)
                    qq = jnp.concatenate([jnp.where(lo, q2, 0.0), jnp.where(lo, 0.0, q2)], axis=0).astype(BF16)
                    work.append((row, hf, edge, k2, v2, qq))
            scores = [_dot_nt(qq, k2) for (_, _, _, k2, _, qq) in work]
            probs = []
            for (row, hf, edge, *_), ss in zip(work, scores):
                es, stats = [], []
                for hh in (0, 1):
                    s = ss[hh * QB:(hh + 1) * QB] + bias[2 * hf + hh] + edge
                    m = jnp.max(s, axis=-1, keepdims=True)
                    e = jnp.exp(s - m)
                    den = jnp.sum(e, axis=-1, keepdims=True)
                    es.append(e.astype(BF16))
                    stats.append((1.0 / den, m + jnp.log(den)))
                probs.append((jnp.concatenate(es, axis=0), stats))
            for (row, hf, _, _, v2, _), (ee, stats) in zip(work, probs):
                oo = _dot(ee, v2)
                os_[hf, pl.ds(row, QB, stride=d), :] = jnp.where(lo, oo[:QB] * stats[0][0], oo[QB:] * stats[1][0])
                ls[hf, pl.ds(row, QB, stride=d), :] = jnp.where(lo, stats[0][1], stats[1][1])

        _att_units(d, units, ATT_FWD_GROUP)
        for hf in (0, 1):
            o_ref[:, hf * 128:(hf + 1) * 128] = os_[hf].astype(BF16)
            l_ref[:, hf * 128:(hf + 1) * 128] = ls[hf]

    def halo3(col):
        c, p, n = _halo_specs(TB, H, S, GW, col)
        return [c, p, n]

    return pl.pallas_call(
        body, name=name, grid=(S // TB,),
        in_specs=[pl.BlockSpec(memory_space=pltpu.SMEM), pl.BlockSpec((QB, 2 * QB), _fixed),
                  pl.BlockSpec((TB, GW), lambda i: (i, cq))] + halo3(ck) + halo3(cv),
        out_specs=[pl.BlockSpec((TB, GW), _row), pl.BlockSpec((TB, GW), _row)],
        out_shape=[jax.ShapeDtypeStruct((S, GW), BF16), jax.ShapeDtypeStruct((S, GW), F32)],
        scratch_shapes=[pltpu.VMEM((2, TB, 128), F32), pltpu.VMEM((2, TB + 2 * H, 128), F32),
                        pltpu.VMEM((2, TB + 2 * H, 128), F32), pltpu.VMEM((2, TB, 128), F32),
                        pltpu.VMEM((2, TB, 128), F32), pltpu.VMEM((HPG, QB, 2 * QB), F32)],
        compiler_params=_params(("arbitrary",)),
    )(rel_bias, bk, z, z, z, z, z, z, z)


def _att_combine(os3, ls3, name):
    S = os3[0].shape[0]
    T = 1024

    def body(o1, o2, o3, l1, l2, l3, o_ref, l_ref):
        lv = [l1[...], l2[...], l3[...]]
        m = jnp.maximum(jnp.maximum(lv[0], lv[1]), lv[2])
        e = [jnp.exp(v - m) for v in lv]
        den = e[0] + e[1] + e[2]
        acc = jnp.zeros_like(m)
        for ev, o in zip(e, (o1, o2, o3)):
            acc = acc + (ev / den) * o[...].astype(F32)
        o_ref[...] = acc.astype(BF16)
        l_ref[...] = m + jnp.log(den)

    blk = pl.BlockSpec((T, GW), _row)
    return pl.pallas_call(
        body, name=name, grid=(S // T,), in_specs=[blk] * 6, out_specs=[blk, blk],
        out_shape=[jax.ShapeDtypeStruct((S, GW), BF16), jax.ShapeDtypeStruct((S, GW), F32)],
        compiler_params=_params(("parallel",)),
    )(*os3, *ls3)


def _att_prep(do, o, lse, name):
    S = do.shape[0]
    T = 1024

    def body(do_ref, o_ref, l_ref, out_ref):
        prod = do_ref[...].astype(F32) * o_ref[...].astype(F32)
        dd = [jnp.broadcast_to(jnp.sum(prod[:, h * HD:(h + 1) * HD], axis=-1, keepdims=True), (T, HD))
              for h in range(HPG)]
        lane = lax.broadcasted_iota(I32, (T, GW), 1)
        out_ref[...] = jnp.where(lane % HD < HD // 2, l_ref[...], jnp.concatenate(dd, axis=-1))

    blk = pl.BlockSpec((T, GW), _row)
    return pl.pallas_call(
        body, name=name, grid=(S // T,), in_specs=[blk] * 3, out_specs=blk,
        out_shape=jax.ShapeDtypeStruct((S, GW), F32), compiler_params=_params(("parallel",)),
    )(do, o, lse)


def _att_bwd(z, rel_bias, do, ld, dz, g, name):
    S = z.shape[0]
    d = DILS[g]
    TB, QB = ATT_TB, ATT_QB
    H = RADIUS * d
    L = S // d
    E = TB + 2 * H
    cq = (Z_ATT + 3 * GW * g) // GW
    ck, cv = cq + 1, cq + 2
    bk_a = _t5_bucket(_offsets_qk(QB, 2 * QB, -RADIUS) * d)
    bk_b = _t5_bucket(-_offsets_qk(QB, 2 * QB, -RADIUS) * d)

    def body(rb_ref, bka_ref, bkb_ref, *refs):
        ins, (dz_ref, db_ref) = refs[:15], refs[16:18]
        qs, ks, vs, dos, ls, dqs, dks, dvs, bias_a, bias_b, dbias = refs[18:]
        i = pl.program_id(0)
        n = pl.num_programs(0)

        @pl.when(i == 0)
        def _():
            off = _offsets_qk(QB, 2 * QB, -RADIUS)
            for h in range(HPG):
                bias_a[h] = _bias_table(bka_ref[...], rb_ref, g * HPG + h, off)
                bias_b[h] = _bias_table(bkb_ref[...], rb_ref, g * HPG + h, off)
            dbias[...] = jnp.zeros_like(dbias)

        for a, scr in enumerate((qs, ks, vs, dos, ls)):
            c_ref, p_ref, n_ref = ins[3 * a:3 * a + 3]
            pre = (lambda v: v.astype(F32) * ATT_SCALE) if a == 0 else (lambda v: v)
            _to_halves(scr, 0, pre(p_ref[...]))
            _to_halves(scr, H, pre(c_ref[...]))
            _to_halves(scr, H + TB, pre(n_ref[...]))

        lo = lax.broadcasted_iota(I32, (QB, 128), 1) < HD

        def split(v):
            return jnp.concatenate([jnp.where(lo, v, 0.0), jnp.where(lo, 0.0, v)], axis=0).astype(BF16)

        def halves(v):
            return v[:QB], v[QB:]

        def units(rjs):
            work = []
            for r, j in rjs:
                row = _unit_row(r, j, d)
                cur = row + H
                m0 = i * (TB // d) + j * QB - RADIUS
                km = lax.broadcasted_iota(I32, (1, 2 * QB), 1) + m0
                edge_a = jnp.where(jnp.where(km >= 0, km, L) < L, 0.0, NEG_INF)
                for hf in (0, 1):
                    ld = lambda scr, at, nrow: scr[hf, pl.ds(at, nrow, stride=d), :]
                    w = dict(row=row, hf=hf, edge=edge_a, l_c=ld(ls, cur, QB), l_t=ld(ls, row, 2 * QB).T)
                    for nm, scr in (("q", qs), ("k", ks), ("v", vs), ("do", dos)):
                        w[nm + "_c"] = split(ld(scr, cur, QB))
                        w[nm + "_e"] = ld(scr, row, 2 * QB).astype(BF16)
                    work.append(w)
            for w in work:
                w["s"] = halves(_dot_nt(w["q_c"], w["k_e"]))
                w["dp"] = halves(_dot_nt(w["do_c"], w["v_e"]))
                w["s2"] = halves(_dot_nt(w["k_c"], w["q_e"]))
                w["dp2"] = halves(_dot_nt(w["v_c"], w["do_e"]))
            for w in work:
                w["ds"], w["p2"], w["ds2"] = [], [], []
                for hh in (0, 1):
                    h, c0 = 2 * w["hf"] + hh, HD * hh
                    l_c, l_t = w["l_c"], w["l_t"]
                    p = jnp.exp(w["s"][hh] + bias_a[h] + w["edge"] - l_c[:, c0:c0 + 1])
                    ds = p * (w["dp"][hh] - l_c[:, c0 + HD // 2:c0 + HD // 2 + 1])
                    dbias[h] += ds
                    p2 = jnp.exp(w["s2"][hh] + bias_b[h] + w["edge"] - l_t[c0:c0 + 1, :])
                    ds2 = p2 * (w["dp2"][hh] - l_t[c0 + HD // 2:c0 + HD // 2 + 1, :])
                    w["ds"].append(ds.astype(BF16))
                    w["p2"].append(p2.astype(BF16))
                    w["ds2"].append(ds2.astype(BF16))
            for w in work:
                at = pl.ds(w["row"], QB, stride=d)
                both = lambda pair, rhs: halves(_dot(jnp.concatenate(pair, axis=0), rhs))
                dq = both(w["ds"], w["k_e"])
                dqs[w["hf"], at, :] = jnp.where(lo, dq[0], dq[1]) * ATT_SCALE
                dv = both(w["p2"], w["do_e"])
                dvs[w["hf"], at, :] = jnp.where(lo, dv[0], dv[1])
                dk = both(w["ds2"], w["q_e"])
                dks[w["hf"], at, :] = jnp.where(lo, dk[0], dk[1])

        _att_units(d, units, ATT_BWD_GROUP)
        for a, scr in enumerate((dqs, dks, dvs)):
            for hf in (0, 1):
                dz_ref[:, a * GW + hf * 128:a * GW + (hf + 1) * 128] = scr[hf].astype(BF16)

        @pl.when(i == n - 1)
        def _():
            rows = lax.broadcasted_iota(I32, (NUM_BUCKETS, 128), 0)
            lanes = lax.broadcasted_iota(I32, (NUM_BUCKETS, 128), 1)
            out = jnp.zeros((NUM_BUCKETS, 128), F32)
            bk = bka_ref[...]
            for h in range(HPG):
                acc = dbias[h]
                for b in range(NUM_BUCKETS):
                    tot = jnp.sum(jnp.sum(jnp.where(bk == b, acc, 0.0), axis=1, keepdims=True), axis=0, keepdims=True)
                    out = out + jnp.where((rows == b) & (lanes == h), tot, 0.0)
            db_ref[...] = out

    def halo3(col, width=GW):
        return _halo_specs(TB, H, S, width, col)

    one = pl.Buffered(1)

    def single(specs):
        return [pl.BlockSpec(s.block_shape, s.index_map, pipeline_mode=one) for s in specs]

    in_specs = ([pl.BlockSpec(memory_space=pltpu.SMEM), pl.BlockSpec((QB, 2 * QB), _fixed),
                 pl.BlockSpec((QB, 2 * QB), _fixed)]
                + single(halo3(cq) + halo3(ck) + halo3(cv) + halo3(0) + halo3(0))
                + [pl.BlockSpec(memory_space=pl.ANY)])
    return pl.pallas_call(
        body, name=name, grid=(S // TB,), in_specs=in_specs,
        out_specs=[pl.BlockSpec((TB, 3 * GW), lambda i: (i, cq // 3)), pl.BlockSpec((NUM_BUCKETS, 128), _fixed)],
        out_shape=[jax.ShapeDtypeStruct(dz.shape, BF16), jax.ShapeDtypeStruct((NUM_BUCKETS, 128), F32)],
        input_output_aliases={18: 0},
        scratch_shapes=[pltpu.VMEM((2, E, 128), F32)] * 5 + [pltpu.VMEM((2, TB, 128), F32)] * 3
        + [pltpu.VMEM((HPG, QB, 2 * QB), F32)] * 3,
        compiler_params=_params(("arbitrary",)),
    )(rel_bias, bk_a, bk_b, z, z, z, z, z, z, z, z, z, do, do, do, ld, ld, ld, dz)


def _memkv_fwd(mem, gm, wkv, name):
    def body(m_ref, g_ref, w_ref, hm_ref, kv_ref):
        hm = _rms_fwd_val(m_ref[...], g_ref[...]).astype(BF16)
        hm_ref[...] = hm
        kv_ref[...] = _dot(hm, w_ref[...]).astype(BF16)

    return pl.pallas_call(
        body, name=name,
        out_shape=[jax.ShapeDtypeStruct((N_MEM, D), BF16), jax.ShapeDtypeStruct((N_MEM, 2 * MW), BF16)],
        compiler_params=_params(),
    )(mem, gm, wkv)


def _memkv_bwd(mem, gm, hm, wkv, dkv, name):
    def body(m_ref, g_ref, hm_ref, w_ref, dkv_ref, dw_ref, dg_ref):
        dkv_b = dkv_ref[...].astype(BF16)
        dw_ref[...] = _dot_tn(hm_ref[...], dkv_b)
        dhm = _dot_nt(dkv_b, w_ref[...])
        _, dgr = _rms_bwd_val(m_ref[...], g_ref[...], dhm)
        dg_ref[...] = jnp.sum(dgr, axis=0, keepdims=True)

    return pl.pallas_call(
        body, name=name,
        out_shape=[jax.ShapeDtypeStruct((D, 2 * MW), F32), jax.ShapeDtypeStruct((1, D), F32)],
        compiler_params=_params(),
    )(mem, gm, hm, wkv, dkv)


MEM_T = 512


def _mem_q_spec():
    return pl.BlockSpec((MEM_T, MW), lambda i: (i, Z_MEM // MW))


def _memattn_fwd(z, kv, name):
    S = z.shape[0]
    T = MEM_T

    def body(q_ref, kv_ref, o_ref):
        for h in range(MH):
            kh = kv_ref[:, h * MHD:(h + 1) * MHD]
            vh = kv_ref[:, MW + h * MHD:MW + (h + 1) * MHD]
            s = _dot_nt(q_ref[:, h * MHD:(h + 1) * MHD], kh) * MEM_SCALE
            e = jnp.exp(s - jnp.max(s, axis=-1, keepdims=True))
            p = e / jnp.sum(e, axis=-1, keepdims=True)
            o_ref[:, h * MHD:(h + 1) * MHD] = _dot(p.astype(BF16), vh).astype(BF16)

    return pl.pallas_call(
        body, name=name, grid=(S // T,),
        in_specs=[_mem_q_spec(), pl.BlockSpec((N_MEM, 2 * MW), _fixed)],
        out_specs=pl.BlockSpec((T, MW), _row),
        out_shape=jax.ShapeDtypeStruct((S, MW), BF16),
        compiler_params=_params(("parallel",)),
    )(z, kv)


def _memattn_bwd(z, kv, dom, dz, name):
    S = z.shape[0]
    T = MEM_T

    def body(q_ref, kv_ref, do_ref, dz_in, dq_ref, dkv_ref):
        @pl.when(pl.program_id(0) == 0)
        def _():
            dkv_ref[...] = jnp.zeros_like(dkv_ref)

        for h in range(MH):
            kh = kv_ref[:, h * MHD:(h + 1) * MHD]
            vh = kv_ref[:, MW + h * MHD:MW + (h + 1) * MHD]
            qh = q_ref[:, h * MHD:(h + 1) * MHD]
            doh = do_ref[:, h * MHD:(h + 1) * MHD]
            s = _dot_nt(qh, kh) * MEM_SCALE
            e = jnp.exp(s - jnp.max(s, axis=-1, keepdims=True))
            p = e / jnp.sum(e, axis=-1, keepdims=True)
            dkv_ref[:, MW + h * MHD:MW + (h + 1) * MHD] += _dot_tn(p.astype(BF16), doh)
            dp = _dot_nt(doh, vh)
            ds = (p * (dp - jnp.sum(dp * p, axis=-1, keepdims=True))).astype(BF16)
            dq_ref[:, h * MHD:(h + 1) * MHD] = (_dot(ds, kh) * MEM_SCALE).astype(BF16)
            dkv_ref[:, h * MHD:(h + 1) * MHD] += _dot_tn(ds, qh) * MEM_SCALE

    return pl.pallas_call(
        body, name=name, grid=(S // T,),
        in_specs=[_mem_q_spec(), pl.BlockSpec((N_MEM, 2 * MW), _fixed), pl.BlockSpec((T, MW), _row),
                  pl.BlockSpec(memory_space=pl.ANY)],
        out_specs=[_mem_q_spec(), pl.BlockSpec((N_MEM, 2 * MW), _fixed)],
        out_shape=[jax.ShapeDtypeStruct(dz.shape, BF16), jax.ShapeDtypeStruct((N_MEM, 2 * MW), F32)],
        input_output_aliases={3: 0},
        compiler_params=_params(("arbitrary",)),
    )(z, kv, dom, dz)


MERGE_T = 256


def _gate_spec(T):
    return pl.BlockSpec((T, 3 * D), lambda i: (i, Z_GATE // (3 * D)))


def _branches(ca_ref, oa_ref, om_ref, wco_ref, wao_ref, wmo_ref, zg_ref, bg_ref):
    ys = [_dot(ca_ref[...], wco_ref[...]), _dot(oa_ref[...], wao_ref[...]), _dot(om_ref[...], wmo_ref[...])]
    gs = [_sigmoid(zg_ref[:, b * D:(b + 1) * D].astype(F32) + bg_ref[:, b * D:(b + 1) * D]) for b in range(3)]
    return ys, gs


def _merge_fwd(x, cact, oatt, om, z, wco, wao, wmo, wout, bgate, gpost, gnext, name):
    S = x.shape[0]
    T = MERGE_T

    def body(x_ref, ca_ref, oa_ref, om_ref, zg_ref, wco_ref, wao_ref, wmo_ref, wout_ref, bg_ref, gp_ref, gn_ref,
             x1_ref, mg_ref, t_ref, h_ref):
        ys, gs = _branches(ca_ref, oa_ref, om_ref, wco_ref, wao_ref, wmo_ref, zg_ref, bg_ref)
        mb = (gs[0] * ys[0] + gs[1] * ys[1] + gs[2] * ys[2]).astype(BF16)
        t = _dot(mb, wout_ref[...])
        mg_ref[...] = mb
        t_ref[...] = t
        x1 = x_ref[...] + _rms_fwd_val(t, gp_ref[...])
        x1_ref[...] = x1
        h_ref[...] = _rms_fwd_val(x1, gn_ref[...]).astype(BF16)

    full = lambda a: pl.BlockSpec(a.shape, _fixed)
    return pl.pallas_call(
        body, name=name, grid=(S // T,),
        in_specs=[pl.BlockSpec((T, D), _row), pl.BlockSpec((T, CW), _row), pl.BlockSpec((T, GW), _row),
                  pl.BlockSpec((T, MW), _row), _gate_spec(T)]
        + [full(wco), full(wao), full(wmo), full(wout), full(bgate), full(gpost), full(gnext)],
        out_specs=[pl.BlockSpec((T, D), _row)] * 4,
        out_shape=[jax.ShapeDtypeStruct((S, D), F32), jax.ShapeDtypeStruct((S, D), BF16),
                   jax.ShapeDtypeStruct((S, D), F32), jax.ShapeDtypeStruct((S, D), BF16)],
        compiler_params=_params(("parallel",)),
    )(x, cact, oatt, om, z, wco, wao, wmo, wout, bgate, gpost, gnext)


def _merge_bwd(dx1, t, mg, cact, oatt, om, z, wco, wao, wmo, wout, bgate, gpost, name):
    S = dx1.shape[0]
    T = MERGE_T

    def body(dx_ref, t_ref, mg_ref, ca_ref, oa_ref, om_ref, zg_ref, wco_ref, wao_ref, wmo_ref, wout_ref,
             bg_ref, gp_ref, dzg_ref, dca_ref, doa_ref, dom_ref, dwco_ref, dwao_ref, dwmo_ref, dwout_ref,
             dbg_ref, dgp_ref):
        accs = (dwco_ref, dwao_ref, dwmo_ref, dwout_ref, dbg_ref, dgp_ref)

        @pl.when(pl.program_id(0) == 0)
        def _():
            for a in accs:
                a[...] = jnp.zeros_like(a)

        dt, dgr = _rms_bwd_val(t_ref[...], gp_ref[...], dx_ref[...])
        dgp_ref[...] += jnp.sum(dgr, axis=0, keepdims=True)
        dtb = dt.astype(BF16)
        dwout_ref[...] += _dot_tn(mg_ref[...], dtb)
        dm = _dot_nt(dtb, wout_ref[...])
        ys, gs = _branches(ca_ref, oa_ref, om_ref, wco_ref, wao_ref, wmo_ref, zg_ref, bg_ref)
        for b, (act_ref, w_ref, dw_ref, da_ref) in enumerate(
                ((ca_ref, wco_ref, dwco_ref, dca_ref), (oa_ref, wao_ref, dwao_ref, doa_ref),
                 (om_ref, wmo_ref, dwmo_ref, dom_ref))):
            dzg = dm * ys[b] * gs[b] * (1.0 - gs[b])
            dzg_ref[:, b * D:(b + 1) * D] = dzg.astype(BF16)
            dbg_ref[:, b * D:(b + 1) * D] += jnp.sum(dzg, axis=0, keepdims=True)
            dy = (dm * gs[b]).astype(BF16)
            dw_ref[...] += _dot_tn(act_ref[...], dy)
            da_ref[...] = _dot_nt(dy, w_ref[...]).astype(BF16)

    full = lambda a: pl.BlockSpec(a.shape, _fixed)
    fullf = lambda a: jax.ShapeDtypeStruct(a.shape, F32)
    return pl.pallas_call(
        body, name=name, grid=(S // T,),
        in_specs=[pl.BlockSpec((T, D), _row), pl.BlockSpec((T, D), _row), pl.BlockSpec((T, D), _row),
                  pl.BlockSpec((T, CW), _row), pl.BlockSpec((T, GW), _row), pl.BlockSpec((T, MW), _row)]
        + [_gate_spec(T), full(wco), full(wao), full(wmo), full(wout), full(bgate), full(gpost)],
        out_specs=[_gate_spec(T), pl.BlockSpec((T, CW), _row), pl.BlockSpec((T, GW), _row),
                   pl.BlockSpec((T, MW), _row), full(wco), full(wao), full(wmo), full(wout), full(bgate), full(gpost)],
        out_shape=[jax.ShapeDtypeStruct((S, NIN), BF16), jax.ShapeDtypeStruct((S, CW), BF16),
                   jax.ShapeDtypeStruct((S, GW), BF16), jax.ShapeDtypeStruct((S, MW), BF16),
                   fullf(wco), fullf(wao), fullf(wmo), fullf(wout), fullf(bgate), fullf(gpost)],
        compiler_params=_params(("arbitrary",)),
    )(dx1, t, mg, cact, oatt, om, z, wco, wao, wmo, wout, bgate, gpost)


FFN_T = 256


def _ffn_fwd(x1, gu, wfo, gpost, gnext, name):
    S = x1.shape[0]
    T = FFN_T

    nxt = gnext is not None

    def body(x_ref, gu_ref, w_ref, gp_ref, *rest):
        x2_ref, f_ref = rest[nxt:nxt + 2]
        gv = gu_ref[:, :FH].astype(F32)
        uv = gu_ref[:, FH:].astype(F32)
        act = (gv * _sigmoid(gv) * uv).astype(BF16)
        f = _dot(act, w_ref[...])
        f_ref[...] = f
        x2 = x_ref[...] + _rms_fwd_val(f, gp_ref[...])
        x2_ref[...] = x2
        if nxt:
            rest[3][...] = _rms_fwd_val(x2, rest[0][...]).astype(BF16)

    return pl.pallas_call(
        body, name=name, grid=(S // T,),
        in_specs=[pl.BlockSpec((T, D), _row), pl.BlockSpec((T, 2 * FH), _row), pl.BlockSpec((FH, D), _fixed),
                  pl.BlockSpec((1, D), _fixed)] + [pl.BlockSpec((1, D), _fixed)] * nxt,
        out_specs=[pl.BlockSpec((T, D), _row)] * (2 + nxt),
        out_shape=[jax.ShapeDtypeStruct((S, D), F32)] * 2 + [jax.ShapeDtypeStruct((S, D), BF16)] * nxt,
        compiler_params=_params(("parallel",)),
    )(x1, gu, wfo, gpost, *([gnext] if nxt else []))


def _ffn_bwd(dx2, f, gu, wfo, gpost, name):
    S = dx2.shape[0]
    T = FFN_T

    def body(dx_ref, f_ref, gu_ref, w_ref, gp_ref, dgu_ref, df_ref, act_ref, dgp_ref):
        @pl.when(pl.program_id(0) == 0)
        def _():
            dgp_ref[...] = jnp.zeros_like(dgp_ref)

        df, dgr = _rms_bwd_val(f_ref[...], gp_ref[...], dx_ref[...])
        dgp_ref[...] += jnp.sum(dgr, axis=0, keepdims=True)
        dfb = df.astype(BF16)
        df_ref[...] = dfb
        dact = _dot_nt(dfb, w_ref[...])
        gv = gu_ref[:, :FH].astype(F32)
        uv = gu_ref[:, FH:].astype(F32)
        sg = _sigmoid(gv)
        silu = gv * sg
        act_ref[...] = (silu * uv).astype(BF16)
        dgu_ref[:, :FH] = (dact * uv * (sg * (1.0 + gv * (1.0 - sg)))).astype(BF16)
        dgu_ref[:, FH:] = (dact * silu).astype(BF16)

    return pl.pallas_call(
        body, name=name, grid=(S // T,),
        in_specs=[pl.BlockSpec((T, D), _row), pl.BlockSpec((T, D), _row), pl.BlockSpec((T, 2 * FH), _row),
                  pl.BlockSpec((FH, D), _fixed), pl.BlockSpec((1, D), _fixed)],
        out_specs=[pl.BlockSpec((T, 2 * FH), _row), pl.BlockSpec((T, D), _row), pl.BlockSpec((T, FH), _row),
                   pl.BlockSpec((1, D), _fixed)],
        out_shape=[jax.ShapeDtypeStruct((S, 2 * FH), BF16), jax.ShapeDtypeStruct((S, D), BF16),
                   jax.ShapeDtypeStruct((S, FH), BF16), jax.ShapeDtypeStruct((1, D), F32)],
        compiler_params=_params(("arbitrary",)),
    )(dx2, f, gu, wfo, gpost)


def _loss_head(y, target, name):
    S = y.shape[0]
    T = 512

    def body(y_ref, t_ref, dy_ref, l_ref):
        @pl.when(pl.program_id(0) == 0)
        def _():
            l_ref[...] = jnp.zeros_like(l_ref)

        e = y_ref[...] - t_ref[...]
        dy_ref[...] = e * (1.0 / D)
        l_ref[...] += (0.5 / D) * jnp.sum(jnp.sum(e * e, axis=1, keepdims=True), axis=0, keepdims=True)

    return pl.pallas_call(
        body, name=name, grid=(S // T,),
        in_specs=[pl.BlockSpec((T, D), _row)] * 2,
        out_specs=[pl.BlockSpec((T, D), _row), pl.BlockSpec((8, 128), _fixed)],
        out_shape=[jax.ShapeDtypeStruct((S, D), F32), jax.ShapeDtypeStruct((8, 128), F32)],
        compiler_params=_params(("arbitrary",)),
    )(y, target)


BIG = ("w_in", "w_conv_out", "w_att_out", "w_mem_kv", "w_mem_out", "w_out", "w_ffn_in", "w_ffn_out")
SMALL = ("rel_bias", "norm_mix_pre", "b_gate", "conv_dw_bias", "conv_ln_g", "conv_ln_b", "norm_mem",
         "norm_mix_post", "norm_ffn_pre", "norm_ffn_post")


def _layer_fwd(l, x, h, mem, w, rel_bias, gnext):
    tag = f"_l{l}"
    z = _mm_nn(h, w["w_in"], 1024, 768, BF16, "mm_in" + tag)
    yc, cact = _conv_fwd(z, w["conv_dw"], w["conv_dw_bias"], w["conv_ln_g"], w["conv_ln_b"], "conv_fwd" + tag)
    og, lg = zip(*[_att_fwd(z, rel_bias, g, f"att_fwd_g{g}" + tag) for g in range(3)])
    oatt, lse = _att_combine(og, lg, "att_combine" + tag)
    hm, kv = _memkv_fwd(mem, w["norm_mem"], w["w_mem_kv"], "memkv_fwd" + tag)
    om = _memattn_fwd(z, kv, "memattn_fwd" + tag)
    x1, mg, t, h2 = _merge_fwd(x, cact, oatt, om, z, w["w_conv_out"], w["w_att_out"], w["w_mem_out"], w["w_out"],
                               w["b_gate"], w["norm_mix_post"], w["norm_ffn_pre"], "merge_fwd" + tag)
    gu = _mm_nn(h2, w["w_ffn_in"], 1024, 1408, BF16, "mm_ffn_in" + tag)
    x2, f, *hn = _ffn_fwd(x1, gu, w["w_ffn_out"], w["norm_ffn_post"], gnext, "ffn_fwd" + tag)
    saved = dict(x=x, h=h, z=z, yc=yc, cact=cact, oatt=oatt, lse=lse, hm=hm, kv=kv, om=om, x1=x1, mg=mg, t=t,
                 h2=h2, gu=gu, f=f)
    return x2, (hn[0] if hn else None), saved


def _layer_bwd(l, dx2, mem, w, rel_bias, s):
    tag = f"_l{l}"
    gr = {}
    dgu, df, act, gr["norm_ffn_post"] = _ffn_bwd(dx2, s["f"], s["gu"], w["w_ffn_out"], w["norm_ffn_post"], "ffn_bwd" + tag)
    gr["w_ffn_out"] = _mm_tn(act, df, 1024, 512, "dw_ffn_out" + tag)
    gr["w_ffn_in"] = _mm_tn(s["h2"], dgu, 1024, 1408, "dw_ffn_in" + tag)
    dx1, gr["norm_ffn_pre"] = _mm_nt_rms_bwd(dgu, w["w_ffn_in"], s["x1"], w["norm_ffn_pre"], dx2, 512, 1408,
                                             "dh_ffn" + tag)
    (dz, dcact, doatt, dom, gr["w_conv_out"], gr["w_att_out"], gr["w_mem_out"], gr["w_out"], gr["b_gate"],
     gr["norm_mix_post"]) = _merge_bwd(dx1, s["t"], s["mg"], s["cact"], s["oatt"], s["om"], s["z"], w["w_conv_out"],
                                       w["w_att_out"], w["w_mem_out"], w["w_out"], w["b_gate"], w["norm_mix_post"],
                                       "merge_bwd" + tag)
    dyc, gr["conv_ln_g"], gr["conv_ln_b"], gr["conv_dw_bias"] = _conv_bwd_ln(
        s["yc"], dcact, w["conv_ln_g"], w["conv_ln_b"], "conv_bwd_ln" + tag)
    dz, dwdw = _conv_bwd_dw(s["z"], dyc, w["conv_dw"], dz, "conv_bwd_dw" + tag)
    gr["conv_dw"] = dwdw[:KSIZE]
    ld = _att_prep(doatt, s["oatt"], s["lse"], "att_prep" + tag)
    drb = []
    for g in range(3):
        dz, db = _att_bwd(s["z"], rel_bias, doatt, ld, dz, g, f"att_bwd_g{g}" + tag)
        drb.append(db)
    dz, dkv = _memattn_bwd(s["z"], s["kv"], dom, dz, "memattn_bwd" + tag)
    gr["w_mem_kv"], gr["norm_mem"] = _memkv_bwd(mem, w["norm_mem"], s["hm"], w["w_mem_kv"], dkv, "memkv_bwd" + tag)
    gr["w_in"] = _mm_tn(s["h"], dz, 1024, 1152, "dw_in" + tag)
    dx, gr["norm_mix_pre"] = _mm_nt_rms_bwd(dz, w["w_in"], s["x"], w["norm_mix_pre"], dx1, 512, 2304, "dh_in" + tag)
    return dx, gr, drb


def _rel_bias_total(parts, name):
    def body(*refs):
        out_ref = refs[-1]
        acc = jnp.zeros((NUM_BUCKETS, 128), F32)
        for l in range(DEPTH):
            for g in range(3):
                v = refs[l * 3 + g][...]
                acc = acc + (v if g == 0 else pltpu.roll(v, HPG * g, axis=1))
        out_ref[...] = acc

    return pl.pallas_call(body, name=name, out_shape=jax.ShapeDtypeStruct((NUM_BUCKETS, 128), F32),
                          compiler_params=_params())(*[p for layer in parts for p in layer])


def _local_step(x, mem, target, rel_bias, layers):
    saved = []
    h = _rms_h(x, layers[0]["norm_mix_pre"], "rms_mix_l0")
    for l in range(DEPTH):
        gnext = layers[l + 1]["norm_mix_pre"] if l + 1 < DEPTH else None
        x, h, s = _layer_fwd(l, x, h, mem, layers[l], rel_bias, gnext)
        saved.append(s)
    dy, lpart = _loss_head(x, target, "loss_head")
    grads = [None] * DEPTH
    drb = [None] * DEPTH
    for l in reversed(range(DEPTH)):
        dy, grads[l], drb[l] = _layer_bwd(l, dy, mem, layers[l], rel_bias, saved[l])
    return lpart[0, 0], dy, grads, _rel_bias_total(drb, "rel_bias_total")


def _z_cols_from_ref(w):
    att = [w[..., R_ATT + (3 * j + g) * GW:R_ATT + (3 * j + g + 1) * GW] for g in range(3) for j in range(3)]
    return jnp.concatenate([w[..., R_GATE:], w[..., :C1], w[..., R_MEM:R_GATE]] + att, axis=-1)


def _ref_cols_from_z(w):
    att = [w[..., Z_ATT + (3 * g + j) * GW:Z_ATT + (3 * g + j + 1) * GW] for j in range(3) for g in range(3)]
    return jnp.concatenate([w[..., Z_CONV:Z_MEM]] + att + [w[..., Z_MEM:Z_ATT], w[..., Z_GATE:Z_CONV]], axis=-1)


N_CHIPS = 4
SHARD = {"w_in": ((D, NIN // 4), 1), "w_conv_out": ((CW, D // 4), 1), "w_att_out": ((GW, D // 4), 1),
         "w_mem_kv": ((D // 4, 2 * MW), 0), "w_mem_out": ((MW, D // 4), 1), "w_out": ((D // 4, D), 0),
         "w_ffn_in": ((D, 2 * FH // 4), 1), "w_ffn_out": ((FH // 4, D), 0)}
CDW_ROWS = 64
VEC_ROWS = (("norm_mix_pre", 1), ("b_gate", 3), ("conv_dw_bias", 1), ("conv_ln_g", 1), ("conv_ln_b", 1),
            ("norm_mem", 1), ("norm_mix_post", 1), ("norm_ffn_pre", 1), ("norm_ffn_post", 1))
VEC_LROWS = sum(r for _, r in VEC_ROWS)
REL_ROW = DEPTH * VEC_LROWS
CDW_ROW = REL_ROW + 1
CDW_GROWS = DEPTH * KSIZE * CW // D
SMALL_ROWS = -(-(CDW_ROW + CDW_GROWS) // 8) * 8


def _mesh_pos():
    return lax.axis_index("x"), lax.axis_index("y"), lax.axis_index("c")


def _other_chips(x, y):
    chips = [(1 - x, y), (x, 1 - y), (1 - x, 1 - y)]
    return chips, [2 * cx + cy for cx, cy in chips]


NBIG = len(BIG)
ANY_SPEC = pl.BlockSpec(memory_space=pl.ANY)


def _remote(src, dst, send_sems, recv_sems, k, to):
    return pltpu.make_async_remote_copy(src_ref=src, dst_ref=dst, send_sem=send_sems.at[k], recv_sem=recv_sems.at[k],
                                        device_id=to, device_id_type=MESH)


def _all_gather(ws, cdw):
    def body(*refs):
        w_refs, cdw_ref = refs[:NBIG], refs[NBIG]
        g_refs, gc_ref = refs[NBIG + 1:2 * NBIG + 1], refs[2 * NBIG + 1]
        send_sems, recv_sems = refs[2 * NBIG + 2:]
        x, y, c = _mesh_pos()
        j = 2 * x + y
        sibling = (x, y, 1 - c)
        chips, blocks = _other_chips(x, y)
        copy = functools.partial(_remote, send_sems=send_sems, recv_sems=recv_sems)
        pairs = list(zip(w_refs, g_refs))
        first = [copy(w.at[c], g.at[j, c], k=k * NBIG + n, to=(*chip, c))
                 for k, chip in enumerate(chips) for n, (w, g) in enumerate(pairs)]
        first += [copy(cdw_ref, gc_ref.at[j], k=6 * NBIG + k, to=(*chip, c)) for k, chip in enumerate(chips)]
        for cp in first:
            cp.start()
        passed = []
        for k, b in enumerate(blocks):
            for n, (w, g) in enumerate(pairs):
                copy(w.at[c], g.at[b, c], k=k * NBIG + n, to=sibling).wait_recv()
            onward = [copy(g.at[b, c], g.at[b, c], k=(3 + k) * NBIG + n, to=sibling) for n, (w, g) in enumerate(pairs)]
            for cp in onward:
                cp.start()
            passed += onward
        for k, b in enumerate(blocks):
            for n, (w, g) in enumerate(pairs):
                copy(w.at[c], g.at[b, 1 - c], k=(3 + k) * NBIG + n, to=sibling).wait_recv()
            copy(cdw_ref, gc_ref.at[b], k=6 * NBIG + k, to=sibling).wait_recv()
        for cp in first + passed:
            cp.wait_send()

    nsem = 6 * NBIG + 3
    return pl.pallas_call(
        body, name="all_gather_weights",
        out_shape=[jax.ShapeDtypeStruct((N_CHIPS,) + w.shape, BF16) for w in ws]
        + [jax.ShapeDtypeStruct((N_CHIPS, CDW_ROWS, 128), F32)],
        in_specs=[ANY_SPEC] * (NBIG + 1), out_specs=[ANY_SPEC] * (NBIG + 1),
        scratch_shapes=[pltpu.SemaphoreType.DMA((nsem,)), pltpu.SemaphoreType.DMA((nsem,))],
    )(*ws, cdw)


def _sibling_exchange(ps):
    def body(*refs):
        p_refs, r_refs, (send_sems, recv_sems) = refs[:NBIG], refs[NBIG:2 * NBIG], refs[2 * NBIG:]
        x, y, c = _mesh_pos()
        cps = [_remote(p.at[1 - c], r, send_sems, recv_sems, n, (x, y, 1 - c))
               for n, (p, r) in enumerate(zip(p_refs, r_refs))]
        for cp in cps:
            cp.start()
        for cp in cps:
            cp.wait()

    return pl.pallas_call(
        body, name="grad_sibling_exchange", out_shape=[jax.ShapeDtypeStruct(p.shape[1:], p.dtype) for p in ps],
        in_specs=[ANY_SPEC] * NBIG, out_specs=[ANY_SPEC] * NBIG,
        scratch_shapes=[pltpu.SemaphoreType.DMA((NBIG,)), pltpu.SemaphoreType.DMA((NBIG,))],
    )(*ps)


SUM_BLOCK_BYTES = 2 * 1024 * 1024


def _sum_rows(s0, s1):
    return s0 if s0 * s1 * 2 <= SUM_BLOCK_BYTES else s0 // 2


def _add_own_layer(where, p, r, name):
    _, _, s0, s1 = p.shape
    T = _sum_rows(s0, s1)

    def body(where_ref, p_ref, r_ref, o_ref):
        o_ref[...] = (p_ref[0].astype(F32) + r_ref[...].astype(F32)).astype(BF16)

    return pl.pallas_call(
        body, name=name,
        grid_spec=pltpu.PrefetchScalarGridSpec(
            num_scalar_prefetch=1, grid=(N_CHIPS, s0 // T),
            in_specs=[pl.BlockSpec((1, 1, T, s1), lambda j, i, wh: (wh[0], j, i, 0)),
                      pl.BlockSpec((1, T, s1), lambda j, i, wh: (j, i, 0))],
            out_specs=pl.BlockSpec((1, T, s1), lambda j, i, wh: (j, i, 0))),
        out_shape=jax.ShapeDtypeStruct(r.shape, BF16), compiler_params=_params(("parallel", "parallel")),
    )(where, p, r)


def _chip_exchange(as_):
    def body(*refs):
        a_refs, r_refs, (send_sems, recv_sems) = refs[:NBIG], refs[NBIG:2 * NBIG], refs[2 * NBIG:]
        x, y, c = _mesh_pos()
        chips, blocks = _other_chips(x, y)
        cps = [_remote(a.at[b], r.at[k], send_sems, recv_sems, k * NBIG + n, (*chip, c))
               for k, (chip, b) in enumerate(zip(chips, blocks)) for n, (a, r) in enumerate(zip(a_refs, r_refs))]
        for cp in cps:
            cp.start()
        for cp in cps:
            cp.wait_recv()
        for cp in cps:
            cp.wait_send()

    return pl.pallas_call(
        body, name="grad_chip_exchange", out_shape=[jax.ShapeDtypeStruct((3,) + a.shape[1:], a.dtype) for a in as_],
        in_specs=[ANY_SPEC] * NBIG, out_specs=[ANY_SPEC] * NBIG,
        scratch_shapes=[pltpu.SemaphoreType.DMA((3 * NBIG,)), pltpu.SemaphoreType.DMA((3 * NBIG,))],
    )(*as_)


def _sum_chips(where, a, r, name):
    _, s0, s1 = a.shape
    T = _sum_rows(s0, s1)

    def body(where_ref, a_ref, r_ref, o_ref):
        acc = a_ref[0].astype(F32)
        for k in range(3):
            acc = acc + r_ref[k].astype(F32)
        o_ref[0] = acc

    return pl.pallas_call(
        body, name=name,
        grid_spec=pltpu.PrefetchScalarGridSpec(
            num_scalar_prefetch=1, grid=(s0 // T,),
            in_specs=[pl.BlockSpec((1, T, s1), lambda i, wh: (wh[1], i, 0)),
                      pl.BlockSpec((3, T, s1), lambda i, wh: (0, i, 0))],
            out_specs=pl.BlockSpec((1, T, s1), lambda i, wh: (wh[0], i, 0))),
        out_shape=jax.ShapeDtypeStruct((DEPTH, s0, s1), F32), compiler_params=_params(("parallel",)),
    )(where, a, r)


def _sibling_share(os_):
    def body(*refs):
        o_refs, (send_sems, recv_sems) = refs[NBIG:2 * NBIG], refs[2 * NBIG:]
        x, y, c = _mesh_pos()
        cps = [_remote(o.at[c], o.at[c], send_sems, recv_sems, n, (x, y, 1 - c)) for n, o in enumerate(o_refs)]
        for cp in cps:
            cp.start()
        for n, o in enumerate(o_refs):
            _remote(o.at[c], o.at[1 - c], send_sems, recv_sems, n, (x, y, 1 - c)).wait_recv()
        for cp in cps:
            cp.wait_send()

    return pl.pallas_call(
        body, name="grad_sibling_share", out_shape=[jax.ShapeDtypeStruct(o.shape, o.dtype) for o in os_],
        in_specs=[ANY_SPEC] * NBIG, out_specs=[ANY_SPEC] * NBIG,
        input_output_aliases={n: n for n in range(NBIG)},
        scratch_shapes=[pltpu.SemaphoreType.DMA((NBIG,)), pltpu.SemaphoreType.DMA((NBIG,))],
    )(*os_)


def _all_reduce_small(sp):
    def body(sp_ref, out_ref, buf, send_sems, recv_sems):
        x, y, c = _mesh_pos()
        me = 4 * x + 2 * y + c
        buf[0] = sp_ref[...]
        cps = []
        for k in range(1, 8):
            peer = (x ^ (k >> 2 & 1), y ^ (k >> 1 & 1), c ^ (k & 1))
            cps.append(pltpu.make_async_remote_copy(src_ref=sp_ref, dst_ref=buf.at[k], send_sem=send_sems.at[k - 1],
                                                    recv_sem=recv_sems.at[k - 1], device_id=peer, device_id_type=MESH))
        for cp in cps:
            cp.start()
        for cp in cps:
            cp.wait_recv()
        for cp in cps:
            cp.wait_send()
        acc = buf[me]
        for p in range(1, 8):
            acc = acc + buf[p ^ me]
        out_ref[...] = acc

    vm = pl.BlockSpec(memory_space=pltpu.VMEM)
    return pl.pallas_call(
        body, name="all_reduce_small", out_shape=jax.ShapeDtypeStruct(sp.shape, F32),
        in_specs=[vm], out_specs=vm,
        scratch_shapes=[pltpu.VMEM((8,) + sp.shape, F32), pltpu.SemaphoreType.DMA((7,)), pltpu.SemaphoreType.DMA((7,))],
        compiler_params=_params(),
    )(sp)


def _adamw(w, g, m, v, name):
    R, C = w.shape
    T = next((t for t in (256, 128) if R % t == 0), R)

    def body(w_ref, g_ref, m_ref, v_ref, d_ref, m2_ref, v2_ref):
        gv = g_ref[...]
        m2 = ADAM_B1 * m_ref[...] + (1.0 - ADAM_B1) * gv
        v2 = ADAM_B2 * v_ref[...] + (1.0 - ADAM_B2) * (gv * gv)
        m_hat = m2 / (1.0 - ADAM_B1 ** ADAM_STEP)
        v_hat = v2 / (1.0 - ADAM_B2 ** ADAM_STEP)
        d_ref[...] = -ADAM_LR * (m_hat / (jnp.sqrt(v_hat) + ADAM_EPS) + ADAM_WD * w_ref[...])
        m2_ref[...] = m2
        v2_ref[...] = v2

    blk = pl.BlockSpec((T, C), _row)
    return pl.pallas_call(
        body, name=name, grid=(R // T,), in_specs=[blk] * 4, out_specs=[blk] * 3,
        out_shape=[jax.ShapeDtypeStruct((R, C), F32)] * 3, compiler_params=_params(("parallel",)),
    )(w, g, m, v)


def _pack_vectors(get, rel, cdw, name):
    rows = []
    for l in range(DEPTH):
        for n, r in VEC_ROWS:
            v = get(n)[l]
            rows.append(jnp.pad(v, (0, r * D - v.shape[0])).reshape(r, D))
    rows.append(jnp.pad(rel.reshape(-1), (0, D - NUM_BUCKETS * 3 * HPG)).reshape(1, D))
    rows.append(cdw.reshape(CDW_GROWS, D))

    def body(*refs):
        out_ref = refs[-1]
        out_ref[...] = jnp.zeros_like(out_ref)
        at = 0
        for ref in refs[:-1]:
            out_ref[at:at + ref.shape[0], :] = ref[...]
            at += ref.shape[0]

    return pl.pallas_call(body, name=name, out_shape=jax.ShapeDtypeStruct((SMALL_ROWS, D), F32),
                          compiler_params=_params())(*rows)


def _unpack_vectors(packed, lens):
    out = {n: [] for n, _ in VEC_ROWS}
    for l in range(DEPTH):
        at = l * VEC_LROWS
        for n, r in VEC_ROWS:
            out[n].append(packed[at:at + r].reshape(-1)[:lens[n]])
            at += r
    rel = packed[REL_ROW, :NUM_BUCKETS * 3 * HPG].reshape(NUM_BUCKETS, 3 * HPG)
    return {n: jnp.stack(v) for n, v in out.items()}, rel


INPUT_NAMES = ("x", "mem") + ("rel_bias", "norm_mix_pre", "w_in", "b_gate", "conv_dw", "conv_dw_bias", "conv_ln_g",
                              "conv_ln_b", "w_conv_out", "w_att_out", "norm_mem", "w_mem_kv", "w_mem_out", "w_out",
                              "norm_mix_post", "norm_ffn_pre", "w_ffn_in", "w_ffn_out", "norm_ffn_post")
WEIGHT_NAMES = INPUT_NAMES[2:]


def kernel(*args):
    nw = len(WEIGHT_NAMES)
    a = dict(zip(INPUT_NAMES, args[:2 + nw]))
    target = args[2 + nw]
    mom = dict(zip(WEIGHT_NAMES, args[3 + nw:3 + 2 * nw]))
    var = dict(zip(WEIGHT_NAMES, args[3 + 2 * nw:3 + 3 * nw]))
    xi, yi, ci = _mesh_pos()
    chip = 2 * xi + yi
    where = jnp.stack([ci, chip]).astype(I32)

    shards = [a[n].astype(BF16) for n in BIG]
    cdw = jnp.pad(a["conv_dw"].reshape(DEPTH * KSIZE, CW // 4), ((0, CDW_ROWS - DEPTH * KSIZE), (0, 0)))
    *gathered, gcdw = _all_gather(shards, cdw)
    gathered = [lax.dynamic_update_slice(g, s[None], (chip, 0, 0, 0)) for g, s in zip(gathered, shards)]
    gcdw = lax.dynamic_update_slice(gcdw, cdw[None], (chip, 0, 0))
    conv_dw = gcdw[:, :DEPTH * KSIZE].reshape(N_CHIPS, DEPTH, KSIZE, CW // 4).transpose(1, 2, 0, 3)
    conv_dw = jnp.pad(conv_dw.reshape(DEPTH, KSIZE, CW), ((0, 0), (0, 1), (0, 0)))
    layers = []
    for l in range(DEPTH):
        w = {"conv_dw": conv_dw[l]}
        for n, g in zip(BIG, gathered):
            (s0, s1), axis = SHARD[n]
            blk = g[:, l]
            w[n] = blk.reshape(N_CHIPS * s0, s1) if axis == 0 else blk.transpose(1, 0, 2).reshape(s0, N_CHIPS * s1)
        w["w_in"] = _z_cols_from_ref(w["w_in"])
        for n, _ in VEC_ROWS:
            w[n] = a[n][l][None, :]
        layers.append(w)

    loss_part, gx, grads, drel = _local_step(a["x"][0], a["mem"][0], target[0], a["rel_bias"], layers)
    loss = lax.psum(loss_part, ("x", "y", "c"))

    packed = []
    for n in BIG:
        (s0, s1), axis = SHARD[n]
        per_layer = []
        for l in range(DEPTH):
            g = _ref_cols_from_z(grads[l][n]) if n == "w_in" else grads[l][n]
            per_layer.append(g.reshape(N_CHIPS, s0, s1) if axis == 0 else g.reshape(s0, N_CHIPS, s1).transpose(1, 0, 2))
        packed.append(jnp.stack(per_layer).astype(BF16))
    from_sibling = _sibling_exchange(packed)
    chip_sums = [_add_own_layer(where, p, r, "grad_add_sibling_" + n) for n, p, r in zip(BIG, packed, from_sibling)]
    from_chips = _chip_exchange(chip_sums)
    reduced = _sibling_share([_sum_chips(where, s, r, "grad_sum_chips_" + n)
                              for n, s, r in zip(BIG, chip_sums, from_chips)])

    gvec = _all_reduce_small(_pack_vectors(
        lambda n: jnp.stack([grads[l][n][0] for l in range(DEPTH)]), drel[:, :3 * HPG],
        jnp.stack([grads[l]["conv_dw"] for l in range(DEPTH)]), "pack_vector_grads"))
    lens = {n: a[n].shape[1] for n, _ in VEC_ROWS}
    g_vec, g_rel = _unpack_vectors(gvec, lens)
    g_cdw = lax.dynamic_slice_in_dim(gvec[CDW_ROW:CDW_ROW + CDW_GROWS].reshape(DEPTH, KSIZE, CW), chip * (CW // 4),
                                     CW // 4, axis=2)

    grad, delta, new_m, new_v = {}, {}, {}, {}
    for n, g in zip(BIG, reduced):
        shape = a[n].shape
        flat2 = lambda t: t.reshape(shape[0] * shape[1], shape[2])
        d, m2, v2 = _adamw(flat2(a[n]), flat2(g), flat2(mom[n]), flat2(var[n]), "adamw_" + n)
        grad[n], delta[n], new_m[n], new_v[n] = g, d.reshape(shape), m2.reshape(shape), v2.reshape(shape)
    shape = a["conv_dw"].shape
    flat2 = lambda t: t.reshape(shape[0] * shape[1], shape[2])
    d, m2, v2 = _adamw(flat2(a["conv_dw"]), flat2(g_cdw), flat2(mom["conv_dw"]), flat2(var["conv_dw"]), "adamw_conv_dw")
    grad["conv_dw"], delta["conv_dw"], new_m["conv_dw"], new_v["conv_dw"] = (
        g_cdw, d.reshape(shape), m2.reshape(shape), v2.reshape(shape))
    zero_cdw = jnp.zeros((DEPTH, KSIZE, CW), F32)
    pk = lambda src, name: _pack_vectors(lambda n: src[n], src["rel_bias"], zero_cdw, name)
    d, m2, v2 = _adamw(pk(a, "pack_vector_w"), gvec, pk(mom, "pack_vector_m"), pk(var, "pack_vector_v"),
                       "adamw_vectors")
    for src, dst in ((d, delta), (m2, new_m), (v2, new_v)):
        vec, rel = _unpack_vectors(src, lens)
        dst.update(vec)
        dst["rel_bias"] = rel
    grad.update(g_vec)
    grad["rel_bias"] = g_rel

    outs = [loss, gx[None]]
    for group in (grad, delta, new_m, new_v):
        outs += [group[n] for n in WEIGHT_NAMES]
    return tuple(outs)
```

```python
import functools
import math

import jax
import jax.numpy as jnp
from jax import lax
from jax.experimental import pallas as pl
from jax.experimental.pallas import tpu as pltpu

F32 = jnp.float32
BF16 = jnp.bfloat16
I32 = jnp.int32

D = 1024
DEPTH = 2
N_MEM = 256
CW = 512
KSIZE = 31
PAD = KSIZE // 2
DILS = (1, 4, 16)
RADIUS = 64
HPG = 4
HD = 64
GW = HPG * HD
MH = 4
MHD = 128
MW = MH * MHD
FH = 2816
NIN = 6912
C1 = 2 * CW
R_ATT = C1
R_MEM = R_ATT + 9 * GW
R_GATE = R_MEM + MW
Z_GATE = 0
Z_CONV = 3 * D
Z_MEM = Z_CONV + C1
Z_ATT = Z_MEM + MW
NUM_BUCKETS = 32
MAX_DISTANCE = 1024
RMS_EPS = 1e-6
LN_EPS = 1e-5
NEG_INF = -1e30
ATT_SCALE = HD ** -0.5
MEM_SCALE = MHD ** -0.5

ADAM_LR = 0.001
ADAM_B1 = 0.9
ADAM_B2 = 0.999
ADAM_EPS = 1e-08
ADAM_WD = 0.01
ADAM_STEP = 10

VMEM_LIMIT_BYTES = 56 * 1024 * 1024
ATT_QB = 128
ATT_TB = 16 * ATT_QB

MESH = pl.DeviceIdType.MESH


def _params(sem=None):
    return pltpu.CompilerParams(dimension_semantics=sem, vmem_limit_bytes=VMEM_LIMIT_BYTES)


def _sigmoid(v):
    return 1.0 / (1.0 + jnp.exp(-v))


def _dot(a, b):
    return jnp.dot(a, b, preferred_element_type=F32)


def _dot_nt(a, b):
    return lax.dot_general(a, b, (((1,), (1,)), ((), ())), preferred_element_type=F32)


def _dot_tn(a, b):
    return lax.dot_general(a, b, (((0,), (0,)), ((), ())), preferred_element_type=F32)


def _rms_fwd_val(v, g):
    r = lax.rsqrt(jnp.mean(v * v, axis=-1, keepdims=True) + RMS_EPS)
    return v * r * g


def _rms_bwd_val(v, g, dy):
    r = lax.rsqrt(jnp.mean(v * v, axis=-1, keepdims=True) + RMS_EPS)
    vh = v * r
    dvh = dy * g
    dv = r * (dvh - vh * jnp.mean(dvh * vh, axis=-1, keepdims=True))
    return dv, dy * vh


def _row(i):
    return (i, 0)


def _fixed(*_):
    return (0, 0)


def _mm_nn(a, b, tm, tn, out_dtype, name):
    M, K = a.shape
    N = b.shape[1]

    def body(a_ref, b_ref, o_ref):
        o_ref[...] = _dot(a_ref[...], b_ref[...]).astype(out_dtype)

    return pl.pallas_call(
        body, name=name, grid=(N // tn, M // tm),
        in_specs=[pl.BlockSpec((tm, K), lambda j, i: (i, 0)), pl.BlockSpec((K, tn), lambda j, i: (0, j))],
        out_specs=pl.BlockSpec((tm, tn), lambda j, i: (i, j)),
        out_shape=jax.ShapeDtypeStruct((M, N), out_dtype),
        compiler_params=_params(("parallel", "parallel")),
    )(a, b)


def _mm_nt_rms_bwd(a, b, x, g, dres, tm, tc, name):
    M, N = a.shape
    nk = N // tc

    def body(a_ref, b_ref, x_ref, g_ref, dres_ref, dx_ref, dg_ref, acc_ref):
        i, k = pl.program_id(0), pl.program_id(1)

        @pl.when(k == 0)
        def _():
            acc_ref[...] = jnp.zeros_like(acc_ref)

            @pl.when(i == 0)
            def _():
                dg_ref[...] = jnp.zeros_like(dg_ref)

        acc_ref[...] += _dot_nt(a_ref[...], b_ref[...])

        @pl.when(k == nk - 1)
        def _():
            dv, dgr = _rms_bwd_val(x_ref[...], g_ref[...], acc_ref[...])
            dx_ref[...] = dres_ref[...] + dv
            dg_ref[...] += jnp.sum(dgr, axis=0, keepdims=True)

    rows = pl.BlockSpec((tm, D), lambda i, k: (i, 0))
    return pl.pallas_call(
        body, name=name, grid=(M // tm, nk),
        in_specs=[pl.BlockSpec((tm, tc), lambda i, k: (i, k)), pl.BlockSpec((D, tc), lambda i, k: (0, k)), rows,
                  pl.BlockSpec((1, D), _fixed), rows],
        out_specs=[rows, pl.BlockSpec((1, D), _fixed)],
        out_shape=[jax.ShapeDtypeStruct((M, D), F32), jax.ShapeDtypeStruct((1, D), F32)],
        scratch_shapes=[pltpu.VMEM((tm, D), F32)],
        compiler_params=_params(("arbitrary", "arbitrary")),
    )(a, b, x, g, dres)


def _mm_tn(a, b, ts, tn, name):
    S, K = a.shape
    N = b.shape[1]

    def body(a_ref, b_ref, o_ref):
        @pl.when(pl.program_id(1) == 0)
        def _():
            o_ref[...] = jnp.zeros_like(o_ref)

        o_ref[...] += _dot_tn(a_ref[...], b_ref[...])

    return pl.pallas_call(
        body, name=name, grid=(N // tn, S // ts),
        in_specs=[pl.BlockSpec((ts, K), lambda j, s: (s, 0)), pl.BlockSpec((ts, tn), lambda j, s: (s, j))],
        out_specs=pl.BlockSpec((K, tn), lambda j, s: (0, j)),
        out_shape=jax.ShapeDtypeStruct((K, N), F32),
        compiler_params=_params(("parallel", "arbitrary")),
    )(a, b)


def _rms_h(x, g, name):
    S = x.shape[0]
    T = 512

    def body(x_ref, g_ref, h_ref):
        h_ref[...] = _rms_fwd_val(x_ref[...], g_ref[...]).astype(BF16)

    return pl.pallas_call(
        body, name=name, grid=(S // T,),
        in_specs=[pl.BlockSpec((T, D), _row), pl.BlockSpec((1, D), _fixed)],
        out_specs=pl.BlockSpec((T, D), _row),
        out_shape=jax.ShapeDtypeStruct((S, D), BF16),
        compiler_params=_params(("parallel",)),
    )(x, g)


CONV_T = 256
CONV_HALO = 16
CONV_RC = 32


def _halo_specs(T, halo, S, width, col):
    per = T // halo
    last = S // halo - 1
    return [
        pl.BlockSpec((T, width), lambda i: (i, col)),
        pl.BlockSpec((halo, width), lambda i: (jnp.maximum(i * per - 1, 0), col)),
        pl.BlockSpec((halo, width), lambda i: (jnp.minimum((i + 1) * per, last), col)),
    ]


def _glu(zb):
    zb = zb.astype(F32)
    return zb[:, :CW] * _sigmoid(zb[:, CW:])


CONV_EXT = CONV_T + 2 * CONV_HALO
SUBLANES = 8


def _fill_shifted(sh_ref, ext_ref, cur, prev, nxt):
    T, halo = CONV_T, CONV_HALO
    i = pl.program_id(0)
    n = pl.num_programs(0)
    ext_ref[0:halo, :] = jnp.where(i > 0, prev, 0.0)
    ext_ref[halo:halo + T, :] = cur
    ext_ref[halo + T:CONV_EXT, :] = jnp.where(i < n - 1, nxt, 0.0)
    ext_ref[CONV_EXT:CONV_EXT + SUBLANES, :] = jnp.zeros((SUBLANES, CW), F32)
    for b in range(SUBLANES):
        sh_ref[b] = ext_ref[b:b + CONV_EXT, :]


def _window(sh_ref, start, rows):
    b = start % SUBLANES
    return sh_ref[b, start - b:start - b + rows, :]


def _shifted_scratch():
    return [pltpu.VMEM((CONV_EXT + SUBLANES, CW), F32), pltpu.VMEM((SUBLANES, CONV_EXT, CW), F32)]


def _conv_fwd(z, wdw, bdw, lng, lnb, name):
    S = z.shape[0]
    T, HL, RC = CONV_T, CONV_HALO, CONV_RC

    def body(cur_ref, prev_ref, next_ref, w_ref, b_ref, g_ref, bb_ref, yc_ref, act_ref, ext_ref, sh_ref):
        _fill_shifted(sh_ref, ext_ref, _glu(cur_ref[...]), _glu(prev_ref[...]), _glu(next_ref[...]))
        for c in range(T // RC):
            acc = jnp.zeros((RC, CW), F32)
            for k in range(KSIZE):
                acc = acc + w_ref[k:k + 1, :] * _window(sh_ref, c * RC + k + HL - PAD, RC)
            yc = acc + b_ref[...]
            yc_ref[c * RC:(c + 1) * RC, :] = yc
            mu = jnp.mean(yc, axis=-1, keepdims=True)
            xc = yc - mu
            ln = xc * lax.rsqrt(jnp.mean(xc * xc, axis=-1, keepdims=True) + LN_EPS) * g_ref[...] + bb_ref[...]
            act_ref[c * RC:(c + 1) * RC, :] = (ln * _sigmoid(ln)).astype(BF16)

    return pl.pallas_call(
        body, name=name, grid=(S // T,),
        in_specs=_halo_specs(T, HL, S, C1, Z_CONV // C1) + [pl.BlockSpec((32, CW), _fixed)]
        + [pl.BlockSpec((1, CW), _fixed)] * 3,
        out_specs=[pl.BlockSpec((T, CW), _row), pl.BlockSpec((T, CW), _row)],
        out_shape=[jax.ShapeDtypeStruct((S, CW), F32), jax.ShapeDtypeStruct((S, CW), BF16)],
        scratch_shapes=_shifted_scratch(),
        compiler_params=_params(("parallel",)),
    )(z, z, z, wdw, bdw, lng, lnb)


def _conv_bwd_ln(yc, dact, lng, lnb, name):
    S = yc.shape[0]
    T = 512

    def body(yc_ref, da_ref, g_ref, b_ref, dyc_ref, dg_ref, db_ref, dbias_ref):
        yc_v = yc_ref[...]
        mu = jnp.mean(yc_v, axis=-1, keepdims=True)
        xc = yc_v - mu
        r = lax.rsqrt(jnp.mean(xc * xc, axis=-1, keepdims=True) + LN_EPS)
        yn = xc * r
        ln = yn * g_ref[...] + b_ref[...]
        sg = _sigmoid(ln)
        dln = da_ref[...].astype(F32) * (sg * (1.0 + ln * (1.0 - sg)))
        dyn = dln * g_ref[...]
        dyc = r * (dyn - jnp.mean(dyn, axis=-1, keepdims=True) - yn * jnp.mean(dyn * yn, axis=-1, keepdims=True))
        dyc_ref[...] = dyc

        @pl.when(pl.program_id(0) == 0)
        def _():
            dg_ref[...] = jnp.zeros_like(dg_ref)
            db_ref[...] = jnp.zeros_like(db_ref)
            dbias_ref[...] = jnp.zeros_like(dbias_ref)

        dg_ref[...] += jnp.sum(dln * yn, axis=0, keepdims=True)
        db_ref[...] += jnp.sum(dln, axis=0, keepdims=True)
        dbias_ref[...] += jnp.sum(dyc, axis=0, keepdims=True)

    vec = pl.BlockSpec((1, CW), _fixed)
    return pl.pallas_call(
        body, name=name, grid=(S // T,),
        in_specs=[pl.BlockSpec((T, CW), _row), pl.BlockSpec((T, CW), _row), vec, vec],
        out_specs=[pl.BlockSpec((T, CW), _row), vec, vec, vec],
        out_shape=[jax.ShapeDtypeStruct((S, CW), F32)] + [jax.ShapeDtypeStruct((1, CW), F32)] * 3,
        compiler_params=_params(("arbitrary",)),
    )(yc, dact, lng, lnb)


def _conv_bwd_dw(z, dyc, wdw, dz, name):
    S = z.shape[0]
    T, HL, RC = CONV_T, CONV_HALO, CONV_RC

    def body(zc_ref, zp_ref, zn_ref, dc_ref, dp_ref, dn_ref, w_ref, dz_in, dz_ref, dw_ref, uext_ref, ush_ref,
             dext_ref, dsh_ref, dwacc_ref):
        _fill_shifted(ush_ref, uext_ref, _glu(zc_ref[...]), _glu(zp_ref[...]), _glu(zn_ref[...]))
        _fill_shifted(dsh_ref, dext_ref, dc_ref[...], dp_ref[...], dn_ref[...])

        @pl.when(pl.program_id(0) == 0)
        def _():
            dwacc_ref[...] = jnp.zeros_like(dwacc_ref)

        for c in range(T // RC):
            dcur = dc_ref[c * RC:(c + 1) * RC, :]
            du = jnp.zeros((RC, CW), F32)
            for k in range(KSIZE):
                du = du + w_ref[k:k + 1, :] * _window(dsh_ref, c * RC + HL + PAD - k, RC)
                prod = dcur * _window(ush_ref, c * RC + k + HL - PAD, RC)
                dwacc_ref[k] += jnp.sum(prod.reshape(RC // SUBLANES, SUBLANES, CW), axis=0)
            zc = zc_ref[c * RC:(c + 1) * RC, :].astype(F32)
            a, gt = zc[:, :CW], zc[:, CW:]
            sg = _sigmoid(gt)
            dz_ref[c * RC:(c + 1) * RC, 0:CW] = (du * sg).astype(BF16)
            dz_ref[c * RC:(c + 1) * RC, CW:C1] = (du * a * sg * (1.0 - sg)).astype(BF16)

        @pl.when(pl.program_id(0) == pl.num_programs(0) - 1)
        def _():
            dw_ref[...] = jnp.sum(dwacc_ref[...], axis=1)

    return pl.pallas_call(
        body, name=name, grid=(S // T,),
        in_specs=_halo_specs(T, HL, S, C1, Z_CONV // C1) + _halo_specs(T, HL, S, CW, 0)
        + [pl.BlockSpec((32, CW), _fixed), pl.BlockSpec(memory_space=pl.ANY)],
        out_specs=[pl.BlockSpec((T, C1), lambda i: (i, Z_CONV // C1)), pl.BlockSpec((32, CW), _fixed)],
        out_shape=[jax.ShapeDtypeStruct(dz.shape, BF16), jax.ShapeDtypeStruct((32, CW), F32)],
        input_output_aliases={7: 0},
        scratch_shapes=_shifted_scratch() + _shifted_scratch() + [pltpu.VMEM((32, SUBLANES, CW), F32)],
        compiler_params=_params(("arbitrary",)),
    )(z, z, z, dyc, dyc, dyc, wdw, dz)


def _t5_bucket(rel):
    nb = NUM_BUCKETS // 2
    max_exact = nb // 2
    ret = jnp.where(rel > 0, nb, 0)
    n = jnp.abs(rel)
    nf = jnp.maximum(n, 1).astype(F32)
    large = max_exact + (jnp.log(nf / max_exact) / math.log(MAX_DISTANCE / max_exact)
                         * (nb - max_exact)).astype(I32)
    large = jnp.minimum(large, nb - 1)
    return ret + jnp.where(n < max_exact, n, large)


def _offsets_qk(nq, nk, shift):
    return lax.broadcasted_iota(I32, (nq, nk), 1) + shift - lax.broadcasted_iota(I32, (nq, nk), 0)


def _bias_table(bk, rb_ref, col, off):
    acc = jnp.zeros(bk.shape, F32)
    for b in range(NUM_BUCKETS):
        acc = jnp.where(bk == b, rb_ref[b, col], acc)
    return jnp.where(jnp.abs(off) <= RADIUS, acc, NEG_INF)


def _to_halves(scr, row0, val):
    rows = val.shape[0]
    v = val.astype(F32)
    scr[0, row0:row0 + rows, :] = v[:, :128]
    scr[1, row0:row0 + rows, :] = v[:, 128:]


ATT_FWD_GROUP = 2
ATT_BWD_GROUP = 1


def _att_units(d, fn, group):
    nj = ATT_TB // (ATT_QB * d)
    if nj == 1:
        def trip(t, c):
            r0 = pl.multiple_of(t * 8, 8)
            for u in range(0, 8, group):
                fn([(r0 + u + v, 0) for v in range(group)])
            return c

        lax.fori_loop(0, d // 8, trip, 0)
        return
    for r in range(d):
        def step(t, c, r=r):
            fn([(r, t * group + u) for u in range(group)])
            return c

        lax.fori_loop(0, nj // group, step, 0)


def _unit_row(r, j, d):
    if isinstance(j, int):
        return j * ATT_QB * d + r
    return pl.multiple_of(j * (ATT_QB * d), ATT_QB) + r


def _att_fwd(z, rel_bias, g, name):
    S = z.shape[0]
    d = DILS[g]
    TB, QB = ATT_TB, ATT_QB
    H = RADIUS * d
    L = S // d
    cq = (Z_ATT + 3 * GW * g) // GW
    ck, cv = cq + 1, cq + 2
    bk = _t5_bucket(_offsets_qk(QB, 2 * QB, -RADIUS) * d)

    def body(rb_ref, bk_ref, q_ref, kc_ref, kp_ref, kn_ref, vc_ref, vp_ref, vn_ref, o_ref, l_ref,
             qs, ks, vs, os_, ls, bias):
        i = pl.program_id(0)

        @pl.when(i == 0)
        def _():
            off = _offsets_qk(QB, 2 * QB, -RADIUS)
            for h in range(HPG):
                bias[h] = _bias_table(bk_ref[...], rb_ref, g * HPG + h, off)

        _to_halves(qs, 0, q_ref[...].astype(F32) * ATT_SCALE)
        for scr, p_ref, c_ref, n_ref in ((ks, kp_ref, kc_ref, kn_ref), (vs, vp_ref, vc_ref, vn_ref)):
            _to_halves(scr, 0, p_ref[...])
            _to_halves(scr, H, c_ref[...])
            _to_halves(scr, H + TB, n_ref[...])

        lo = lax.broadcasted_iota(I32, (QB, 128), 1) < HD

        def units(rjs):
            work = []
            for r, j in rjs:
                row = _unit_row(r, j, d)
                km = lax.broadcasted_iota(I32, (1, 2 * QB), 1) + (i * (TB // d) + j * QB - RADIUS)
                edge = jnp.where(jnp.where(km >= 0, km, L) < L, 0.0, NEG_INF)
                for hf in (0, 1):
                    q2 = qs[hf, pl.ds(row, QB, stride=d), :]
                    k2 = ks[hf, pl.ds(row, 2 * QB, stride=d), :].astype(BF16)
                    v2 = vs[hf, pl.ds(row, 2 * QB, stride=d), :].astype(BF16)
                    qq = jnp.concatenate([jnp.where(lo, q2, 0.0), jnp.where(lo, 0.0, q2)], axis=0).astype(BF16)
                    work.append((row, hf, edge, k2, v2, qq))
            scores = [_dot_nt(qq, k2) for (_, _, _, k2, _, qq) in work]
            probs = []
            for (row, hf, edge, *_), ss in zip(work, scores):
                es, stats = [], []
                for hh in (0, 1):
                    s = ss[hh * QB:(hh + 1) * QB] + bias[2 * hf + hh] + edge
                    m = jnp.max(s, axis=-1, keepdims=True)
                    e = jnp.exp(s - m)
                    den = jnp.sum(e, axis=-1, keepdims=True)
                    es.append(e.astype(BF16))
                    stats.append((1.0 / den, m + jnp.log(den)))
                probs.append((jnp.concatenate(es, axis=0), stats))
            for (row, hf, _, _, v2, _), (ee, stats) in zip(work, probs):
                oo = _dot(ee, v2)
                os_[hf, pl.ds(row, QB, stride=d), :] = jnp.where(lo, oo[:QB] * stats[0][0], oo[QB:] * stats[1][0])
                ls[hf, pl.ds(row, QB, stride=d), :] = jnp.where(lo, stats[0][1], stats[1][1])

        _att_units(d, units, ATT_FWD_GROUP)
        for hf in (0, 1):
            o_ref[:, hf * 128:(hf + 1) * 128] = os_[hf].astype(BF16)
            l_ref[:, hf * 128:(hf + 1) * 128] = ls[hf]

    def halo3(col):
        c, p, n = _halo_specs(TB, H, S, GW, col)
        return [c, p, n]

    return pl.pallas_call(
        body, name=name, grid=(S // TB,),
        in_specs=[pl.BlockSpec(memory_space=pltpu.SMEM), pl.BlockSpec((QB, 2 * QB), _fixed),
                  pl.BlockSpec((TB, GW), lambda i: (i, cq))] + halo3(ck) + halo3(cv),
        out_specs=[pl.BlockSpec((TB, GW), _row), pl.BlockSpec((TB, GW), _row)],
        out_shape=[jax.ShapeDtypeStruct((S, GW), BF16), jax.ShapeDtypeStruct((S, GW), F32)],
        scratch_shapes=[pltpu.VMEM((2, TB, 128), F32), pltpu.VMEM((2, TB + 2 * H, 128), F32),
                        pltpu.VMEM((2, TB + 2 * H, 128), F32), pltpu.VMEM((2, TB, 128), F32),
                        pltpu.VMEM((2, TB, 128), F32), pltpu.VMEM((HPG, QB, 2 * QB), F32)],
        compiler_params=_params(("arbitrary",)),
    )(rel_bias, bk, z, z, z, z, z, z, z)


def _att_combine(os3, ls3, name):
    S = os3[0].shape[0]
    T = 1024

    def body(o1, o2, o3, l1, l2, l3, o_ref, l_ref):
        lv = [l1[...], l2[...], l3[...]]
        m = jnp.maximum(jnp.maximum(lv[0], lv[1]), lv[2])
        e = [jnp.exp(v - m) for v in lv]
        den = e[0] + e[1] + e[2]
        acc = jnp.zeros_like(m)
        for ev, o in zip(e, (o1, o2, o3)):
            acc = acc + (ev / den) * o[...].astype(F32)
        o_ref[...] = acc.astype(BF16)
        l_ref[...] = m + jnp.log(den)

    blk = pl.BlockSpec((T, GW), _row)
    return pl.pallas_call(
        body, name=name, grid=(S // T,), in_specs=[blk] * 6, out_specs=[blk, blk],
        out_shape=[jax.ShapeDtypeStruct((S, GW), BF16), jax.ShapeDtypeStruct((S, GW), F32)],
        compiler_params=_params(("parallel",)),
    )(*os3, *ls3)


def _att_prep(do, o, lse, name):
    S = do.shape[0]
    T = 1024

    def body(do_ref, o_ref, l_ref, out_ref):
        prod = do_ref[...].astype(F32) * o_ref[...].astype(F32)
        dd = [jnp.broadcast_to(jnp.sum(prod[:, h * HD:(h + 1) * HD], axis=-1, keepdims=True), (T, HD))
              for h in range(HPG)]
        lane = lax.broadcasted_iota(I32, (T, GW), 1)
        out_ref[...] = jnp.where(lane % HD < HD // 2, l_ref[...], jnp.concatenate(dd, axis=-1))

    blk = pl.BlockSpec((T, GW), _row)
    return pl.pallas_call(
        body, name=name, grid=(S // T,), in_specs=[blk] * 3, out_specs=blk,
        out_shape=jax.ShapeDtypeStruct((S, GW), F32), compiler_params=_params(("parallel",)),
    )(do, o, lse)


def _att_bwd(z, rel_bias, do, ld, dz, g, name):
    S = z.shape[0]
    d = DILS[g]
    TB, QB = ATT_TB, ATT_QB
    H = RADIUS * d
    L = S // d
    E = TB + 2 * H
    cq = (Z_ATT + 3 * GW * g) // GW
    ck, cv = cq + 1, cq + 2
    bk_a = _t5_bucket(_offsets_qk(QB, 2 * QB, -RADIUS) * d)
    bk_b = _t5_bucket(-_offsets_qk(QB, 2 * QB, -RADIUS) * d)

    def body(rb_ref, bka_ref, bkb_ref, *refs):
        ins, (dz_ref, db_ref) = refs[:15], refs[16:18]
        qs, ks, vs, dos, ls, dqs, dks, dvs, bias_a, bias_b, dbias = refs[18:]
        i = pl.program_id(0)
        n = pl.num_programs(0)

        @pl.when(i == 0)
        def _():
            off = _offsets_qk(QB, 2 * QB, -RADIUS)
            for h in range(HPG):
                bias_a[h] = _bias_table(bka_ref[...], rb_ref, g * HPG + h, off)
                bias_b[h] = _bias_table(bkb_ref[...], rb_ref, g * HPG + h, off)
            dbias[...] = jnp.zeros_like(dbias)

        for a, scr in enumerate((qs, ks, vs, dos, ls)):
            c_ref, p_ref, n_ref = ins[3 * a:3 * a + 3]
            pre = (lambda v: v.astype(F32) * ATT_SCALE) if a == 0 else (lambda v: v)
            _to_halves(scr, 0, pre(p_ref[...]))
            _to_halves(scr, H, pre(c_ref[...]))
            _to_halves(scr, H + TB, pre(n_ref[...]))

        lo = lax.broadcasted_iota(I32, (QB, 128), 1) < HD

        def split(v):
            return jnp.concatenate([jnp.where(lo, v, 0.0), jnp.where(lo, 0.0, v)], axis=0).astype(BF16)

        def halves(v):
            return v[:QB], v[QB:]

        def units(rjs):
            work = []
            for r, j in rjs:
                row = _unit_row(r, j, d)
                cur = row + H
                m0 = i * (TB // d) + j * QB - RADIUS
                km = lax.broadcasted_iota(I32, (1, 2 * QB), 1) + m0
                edge_a = jnp.where(jnp.where(km >= 0, km, L) < L, 0.0, NEG_INF)
                for hf in (0, 1):
                    ld = lambda scr, at, nrow: scr[hf, pl.ds(at, nrow, stride=d), :]
                    w = dict(row=row, hf=hf, edge=edge_a, l_c=ld(ls, cur, QB), l_t=ld(ls, row, 2 * QB).T)
                    for nm, scr in (("q", qs), ("k", ks), ("v", vs), ("do", dos)):
                        w[nm + "_c"] = split(ld(scr, cur, QB))
                        w[nm + "_e"] = ld(scr, row, 2 * QB).astype(BF16)
                    work.append(w)
            for w in work:
                w["s"] = halves(_dot_nt(w["q_c"], w["k_e"]))
                w["dp"] = halves(_dot_nt(w["do_c"], w["v_e"]))
                w["s2"] = halves(_dot_nt(w["k_c"], w["q_e"]))
                w["dp2"] = halves(_dot_nt(w["v_c"], w["do_e"]))
            for w in work:
                w["ds"], w["p2"], w["ds2"] = [], [], []
                for hh in (0, 1):
                    h, c0 = 2 * w["hf"] + hh, HD * hh
                    l_c, l_t = w["l_c"], w["l_t"]
                    p = jnp.exp(w["s"][hh] + bias_a[h] + w["edge"] - l_c[:, c0:c0 + 1])
                    ds = p * (w["dp"][hh] - l_c[:, c0 + HD // 2:c0 + HD // 2 + 1])
                    dbias[h] += ds
                    p2 = jnp.exp(w["s2"][hh] + bias_b[h] + w["edge"] - l_t[c0:c0 + 1, :])
                    ds2 = p2 * (w["dp2"][hh] - l_t[c0 + HD // 2:c0 + HD // 2 + 1, :])
                    w["ds"].append(ds.astype(BF16))
                    w["p2"].append(p2.astype(BF16))
                    w["ds2"].append(ds2.astype(BF16))
            for w in work:
                at = pl.ds(w["row"], QB, stride=d)
                both = lambda pair, rhs: halves(_dot(jnp.concatenate(pair, axis=0), rhs))
                dq = both(w["ds"], w["k_e"])
                dqs[w["hf"], at, :] = jnp.where(lo, dq[0], dq[1]) * ATT_SCALE
                dv = both(w["p2"], w["do_e"])
                dvs[w["hf"], at, :] = jnp.where(lo, dv[0], dv[1])
                dk = both(w["ds2"], w["q_e"])
                dks[w["hf"], at, :] = jnp.where(lo, dk[0], dk[1])

        _att_units(d, units, ATT_BWD_GROUP)
        for a, scr in enumerate((dqs, dks, dvs)):
            for hf in (0, 1):
                dz_ref[:, a * GW + hf * 128:a * GW + (hf + 1) * 128] = scr[hf].astype(BF16)

        @pl.when(i == n - 1)
        def _():
            rows = lax.broadcasted_iota(I32, (NUM_BUCKETS, 128), 0)
            lanes = lax.broadcasted_iota(I32, (NUM_BUCKETS, 128), 1)
            out = jnp.zeros((NUM_BUCKETS, 128), F32)
            bk = bka_ref[...]
            for h in range(HPG):
                acc = dbias[h]
                for b in range(NUM_BUCKETS):
                    tot = jnp.sum(jnp.sum(jnp.where(bk == b, acc, 0.0), axis=1, keepdims=True), axis=0, keepdims=True)
                    out = out + jnp.where((rows == b) & (lanes == h), tot, 0.0)
            db_ref[...] = out

    def halo3(col, width=GW):
        return _halo_specs(TB, H, S, width, col)

    one = pl.Buffered(1)

    def single(specs):
        return [pl.BlockSpec(s.block_shape, s.index_map, pipeline_mode=one) for s in specs]

    in_specs = ([pl.BlockSpec(memory_space=pltpu.SMEM), pl.BlockSpec((QB, 2 * QB), _fixed),
                 pl.BlockSpec((QB, 2 * QB), _fixed)]
                + single(halo3(cq) + halo3(ck) + halo3(cv) + halo3(0) + halo3(0))
                + [pl.BlockSpec(memory_space=pl.ANY)])
    return pl.pallas_call(
        body, name=name, grid=(S // TB,), in_specs=in_specs,
        out_specs=[pl.BlockSpec((TB, 3 * GW), lambda i: (i, cq // 3)), pl.BlockSpec((NUM_BUCKETS, 128), _fixed)],
        out_shape=[jax.ShapeDtypeStruct(dz.shape, BF16), jax.ShapeDtypeStruct((NUM_BUCKETS, 128), F32)],
        input_output_aliases={18: 0},
        scratch_shapes=[pltpu.VMEM((2, E, 128), F32)] * 5 + [pltpu.VMEM((2, TB, 128), F32)] * 3
        + [pltpu.VMEM((HPG, QB, 2 * QB), F32)] * 3,
        compiler_params=_params(("arbitrary",)),
    )(rel_bias, bk_a, bk_b, z, z, z, z, z, z, z, z, z, do, do, do, ld, ld, ld, dz)


def _memkv_fwd(mem, gm, wkv, name):
    def body(m_ref, g_ref, w_ref, hm_ref, kv_ref):
        hm = _rms_fwd_val(m_ref[...], g_ref[...]).astype(BF16)
        hm_ref[...] = hm
        kv_ref[...] = _dot(hm, w_ref[...]).astype(BF16)

    return pl.pallas_call(
        body, name=name,
        out_shape=[jax.ShapeDtypeStruct((N_MEM, D), BF16), jax.ShapeDtypeStruct((N_MEM, 2 * MW), BF16)],
        compiler_params=_params(),
    )(mem, gm, wkv)


def _memkv_bwd(mem, gm, hm, wkv, dkv, name):
    def body(m_ref, g_ref, hm_ref, w_ref, dkv_ref, dw_ref, dg_ref):
        dkv_b = dkv_ref[...].astype(BF16)
        dw_ref[...] = _dot_tn(hm_ref[...], dkv_b)
        dhm = _dot_nt(dkv_b, w_ref[...])
        _, dgr = _rms_bwd_val(m_ref[...], g_ref[...], dhm)
        dg_ref[...] = jnp.sum(dgr, axis=0, keepdims=True)

    return pl.pallas_call(
        body, name=name,
        out_shape=[jax.ShapeDtypeStruct((D, 2 * MW), F32), jax.ShapeDtypeStruct((1, D), F32)],
        compiler_params=_params(),
    )(mem, gm, hm, wkv, dkv)


MEM_T = 512


def _mem_q_spec():
    return pl.BlockSpec((MEM_T, MW), lambda i: (i, Z_MEM // MW))


def _memattn_fwd(z, kv, name):
    S = z.shape[0]
    T = MEM_T

    def body(q_ref, kv_ref, o_ref):
        for h in range(MH):
            kh = kv_ref[:, h * MHD:(h + 1) * MHD]
            vh = kv_ref[:, MW + h * MHD:MW + (h + 1) * MHD]
            s = _dot_nt(q_ref[:, h * MHD:(h + 1) * MHD], kh) * MEM_SCALE
            e = jnp.exp(s - jnp.max(s, axis=-1, keepdims=True))
            p = e / jnp.sum(e, axis=-1, keepdims=True)
            o_ref[:, h * MHD:(h + 1) * MHD] = _dot(p.astype(BF16), vh).astype(BF16)

    return pl.pallas_call(
        body, name=name, grid=(S // T,),
        in_specs=[_mem_q_spec(), pl.BlockSpec((N_MEM, 2 * MW), _fixed)],
        out_specs=pl.BlockSpec((T, MW), _row),
        out_shape=jax.ShapeDtypeStruct((S, MW), BF16),
        compiler_params=_params(("parallel",)),
    )(z, kv)


def _memattn_bwd(z, kv, dom, dz, name):
    S = z.shape[0]
    T = MEM_T

    def body(q_ref, kv_ref, do_ref, dz_in, dq_ref, dkv_ref):
        @pl.when(pl.program_id(0) == 0)
        def _():
            dkv_ref[...] = jnp.zeros_like(dkv_ref)

        for h in range(MH):
            kh = kv_ref[:, h * MHD:(h + 1) * MHD]
            vh = kv_ref[:, MW + h * MHD:MW + (h + 1) * MHD]
            qh = q_ref[:, h * MHD:(h + 1) * MHD]
            doh = do_ref[:, h * MHD:(h + 1) * MHD]
            s = _dot_nt(qh, kh) * MEM_SCALE
            e = jnp.exp(s - jnp.max(s, axis=-1, keepdims=True))
            p = e / jnp.sum(e, axis=-1, keepdims=True)
            dkv_ref[:, MW + h * MHD:MW + (h + 1) * MHD] += _dot_tn(p.astype(BF16), doh)
            dp = _dot_nt(doh, vh)
            ds = (p * (dp - jnp.sum(dp * p, axis=-1, keepdims=True))).astype(BF16)
            dq_ref[:, h * MHD:(h + 1) * MHD] = (_dot(ds, kh) * MEM_SCALE).astype(BF16)
            dkv_ref[:, h * MHD:(h + 1) * MHD] += _dot_tn(ds, qh) * MEM_SCALE

    return pl.pallas_call(
        body, name=name, grid=(S // T,),
        in_specs=[_mem_q_spec(), pl.BlockSpec((N_MEM, 2 * MW), _fixed), pl.BlockSpec((T, MW), _row),
                  pl.BlockSpec(memory_space=pl.ANY)],
        out_specs=[_mem_q_spec(), pl.BlockSpec((N_MEM, 2 * MW), _fixed)],
        out_shape=[jax.ShapeDtypeStruct(dz.shape, BF16), jax.ShapeDtypeStruct((N_MEM, 2 * MW), F32)],
        input_output_aliases={3: 0},
        compiler_params=_params(("arbitrary",)),
    )(z, kv, dom, dz)


MERGE_T = 512


def _gate_spec(T):
    return pl.BlockSpec((T, 3 * D), lambda i: (i, Z_GATE // (3 * D)))


def _branches(ca_ref, oa_ref, om_ref, wco_ref, wao_ref, wmo_ref, zg_ref, bg_ref):
    ys = [_dot(ca_ref[...], wco_ref[...]), _dot(oa_ref[...], wao_ref[...]), _dot(om_ref[...], wmo_ref[...])]
    gs = [_sigmoid(zg_ref[:, b * D:(b + 1) * D].astype(F32) + bg_ref[:, b * D:(b + 1) * D]) for b in range(3)]
    return ys, gs


def _merge_fwd(x, cact, oatt, om, z, wco, wao, wmo, wout, bgate, gpost, gnext, name):
    S = x.shape[0]
    T = MERGE_T

    def body(x_ref, ca_ref, oa_ref, om_ref, zg_ref, wco_ref, wao_ref, wmo_ref, wout_ref, bg_ref, gp_ref, gn_ref,
             x1_ref, mg_ref, t_ref, h_ref):
        ys, gs = _branches(ca_ref, oa_ref, om_ref, wco_ref, wao_ref, wmo_ref, zg_ref, bg_ref)
        mb = (gs[0] * ys[0] + gs[1] * ys[1] + gs[2] * ys[2]).astype(BF16)
        t = _dot(mb, wout_ref[...])
        mg_ref[...] = mb
        t_ref[...] = t
        x1 = x_ref[...] + _rms_fwd_val(t, gp_ref[...])
        x1_ref[...] = x1
        h_ref[...] = _rms_fwd_val(x1, gn_ref[...]).astype(BF16)

    full = lambda a: pl.BlockSpec(a.shape, _fixed)
    return pl.pallas_call(
        body, name=name, grid=(S // T,),
        in_specs=[pl.BlockSpec((T, D), _row), pl.BlockSpec((T, CW), _row), pl.BlockSpec((T, GW), _row),
                  pl.BlockSpec((T, MW), _row), _gate_spec(T)]
        + [full(wco), full(wao), full(wmo), full(wout), full(bgate), full(gpost), full(gnext)],
        out_specs=[pl.BlockSpec((T, D), _row)] * 4,
        out_shape=[jax.ShapeDtypeStruct((S, D), F32), jax.ShapeDtypeStruct((S, D), BF16),
                   jax.ShapeDtypeStruct((S, D), F32), jax.ShapeDtypeStruct((S, D), BF16)],
        compiler_params=_params(("parallel",)),
    )(x, cact, oatt, om, z, wco, wao, wmo, wout, bgate, gpost, gnext)


def _merge_bwd(dx1, t, mg, cact, oatt, om, z, wco, wao, wmo, wout, bgate, gpost, name):
    S = dx1.shape[0]
    T = MERGE_T

    def body(dx_ref, t_ref, mg_ref, ca_ref, oa_ref, om_ref, zg_ref, wco_ref, wao_ref, wmo_ref, wout_ref,
             bg_ref, gp_ref, dzg_ref, dca_ref, doa_ref, dom_ref, dwco_ref, dwao_ref, dwmo_ref, dwout_ref,
             dbg_ref, dgp_ref):
        accs = (dwco_ref, dwao_ref, dwmo_ref, dwout_ref, dbg_ref, dgp_ref)

        @pl.when(pl.program_id(0) == 0)
        def _():
            for a in accs:
                a[...] = jnp.zeros_like(a)

        dt, dgr = _rms_bwd_val(t_ref[...], gp_ref[...], dx_ref[...])
        dgp_ref[...] += jnp.sum(dgr, axis=0, keepdims=True)
        dtb = dt.astype(BF16)
        dwout_ref[...] += _dot_tn(mg_ref[...], dtb)
        dm = _dot_nt(dtb, wout_ref[...])
        ys, gs = _branches(ca_ref, oa_ref, om_ref, wco_ref, wao_ref, wmo_ref, zg_ref, bg_ref)
        for b, (act_ref, w_ref, dw_ref, da_ref) in enumerate(
                ((ca_ref, wco_ref, dwco_ref, dca_ref), (oa_ref, wao_ref, dwao_ref, doa_ref),
                 (om_ref, wmo_ref, dwmo_ref, dom_ref))):
            dzg = dm * ys[b] * gs[b] * (1.0 - gs[b])
            dzg_ref[:, b * D:(b + 1) * D] = dzg.astype(BF16)
            dbg_ref[:, b * D:(b + 1) * D] += jnp.sum(dzg, axis=0, keepdims=True)
            dy = (dm * gs[b]).astype(BF16)
            dw_ref[...] += _dot_tn(act_ref[...], dy)
            da_ref[...] = _dot_nt(dy, w_ref[...]).astype(BF16)

    full = lambda a: pl.BlockSpec(a.shape, _fixed)
    fullf = lambda a: jax.ShapeDtypeStruct(a.shape, F32)
    return pl.pallas_call(
        body, name=name, grid=(S // T,),
        in_specs=[pl.BlockSpec((T, D), _row), pl.BlockSpec((T, D), _row), pl.BlockSpec((T, D), _row),
                  pl.BlockSpec((T, CW), _row), pl.BlockSpec((T, GW), _row), pl.BlockSpec((T, MW), _row)]
        + [_gate_spec(T), full(wco), full(wao), full(wmo), full(wout), full(bgate), full(gpost)],
        out_specs=[_gate_spec(T), pl.BlockSpec((T, CW), _row), pl.BlockSpec((T, GW), _row),
                   pl.BlockSpec((T, MW), _row), full(wco), full(wao), full(wmo), full(wout), full(bgate), full(gpost)],
        out_shape=[jax.ShapeDtypeStruct((S, NIN), BF16), jax.ShapeDtypeStruct((S, CW), BF16),
                   jax.ShapeDtypeStruct((S, GW), BF16), jax.ShapeDtypeStruct((S, MW), BF16),
                   fullf(wco), fullf(wao), fullf(wmo), fullf(wout), fullf(bgate), fullf(gpost)],
        compiler_params=_params(("arbitrary",)),
    )(dx1, t, mg, cact, oatt, om, z, wco, wao, wmo, wout, bgate, gpost)


FFN_T = 256


def _ffn_fwd(x1, gu, wfo, gpost, gnext, name):
    S = x1.shape[0]
    T = FFN_T

    nxt = gnext is not None

    def body(x_ref, gu_ref, w_ref, gp_ref, *rest):
        x2_ref, f_ref = rest[nxt:nxt + 2]
        gv = gu_ref[:, :FH].astype(F32)
        uv = gu_ref[:, FH:].astype(F32)
        act = (gv * _sigmoid(gv) * uv).astype(BF16)
        f = _dot(act, w_ref[...])
        f_ref[...] = f
        x2 = x_ref[...] + _rms_fwd_val(f, gp_ref[...])
        x2_ref[...] = x2
        if nxt:
            rest[3][...] = _rms_fwd_val(x2, rest[0][...]).astype(BF16)

    return pl.pallas_call(
        body, name=name, grid=(S // T,),
        in_specs=[pl.BlockSpec((T, D), _row), pl.BlockSpec((T, 2 * FH), _row), pl.BlockSpec((FH, D), _fixed),
                  pl.BlockSpec((1, D), _fixed)] + [pl.BlockSpec((1, D), _fixed)] * nxt,
        out_specs=[pl.BlockSpec((T, D), _row)] * (2 + nxt),
        out_shape=[jax.ShapeDtypeStruct((S, D), F32)] * 2 + [jax.ShapeDtypeStruct((S, D), BF16)] * nxt,
        compiler_params=_params(("parallel",)),
    )(x1, gu, wfo, gpost, *([gnext] if nxt else []))


def _ffn_bwd(dx2, f, gu, wfo, gpost, name):
    S = dx2.shape[0]
    T = FFN_T

    def body(dx_ref, f_ref, gu_ref, w_ref, gp_ref, dgu_ref, df_ref, act_ref, dgp_ref):
        @pl.when(pl.program_id(0) == 0)
        def _():
            dgp_ref[...] = jnp.zeros_like(dgp_ref)

        df, dgr = _rms_bwd_val(f_ref[...], gp_ref[...], dx_ref[...])
        dgp_ref[...] += jnp.sum(dgr, axis=0, keepdims=True)
        dfb = df.astype(BF16)
        df_ref[...] = dfb
        dact = _dot_nt(dfb, w_ref[...])
        gv = gu_ref[:, :FH].astype(F32)
        uv = gu_ref[:, FH:].astype(F32)
        sg = _sigmoid(gv)
        silu = gv * sg
        act_ref[...] = (silu * uv).astype(BF16)
        dgu_ref[:, :FH] = (dact * uv * (sg * (1.0 + gv * (1.0 - sg)))).astype(BF16)
        dgu_ref[:, FH:] = (dact * silu).astype(BF16)

    return pl.pallas_call(
        body, name=name, grid=(S // T,),
        in_specs=[pl.BlockSpec((T, D), _row), pl.BlockSpec((T, D), _row), pl.BlockSpec((T, 2 * FH), _row),
                  pl.BlockSpec((FH, D), _fixed), pl.BlockSpec((1, D), _fixed)],
        out_specs=[pl.BlockSpec((T, 2 * FH), _row), pl.BlockSpec((T, D), _row), pl.BlockSpec((T, FH), _row),
                   pl.BlockSpec((1, D), _fixed)],
        out_shape=[jax.ShapeDtypeStruct((S, 2 * FH), BF16), jax.ShapeDtypeStruct((S, D), BF16),
                   jax.ShapeDtypeStruct((S, FH), BF16), jax.ShapeDtypeStruct((1, D), F32)],
        compiler_params=_params(("arbitrary",)),
    )(dx2, f, gu, wfo, gpost)


def _loss_head(y, target, name):
    S = y.shape[0]
    T = 512

    def body(y_ref, t_ref, dy_ref, l_ref):
        @pl.when(pl.program_id(0) == 0)
        def _():
            l_ref[...] = jnp.zeros_like(l_ref)

        e = y_ref[...] - t_ref[...]
        dy_ref[...] = e * (1.0 / D)
        l_ref[...] += (0.5 / D) * jnp.sum(jnp.sum(e * e, axis=1, keepdims=True), axis=0, keepdims=True)

    return pl.pallas_call(
        body, name=name, grid=(S // T,),
        in_specs=[pl.BlockSpec((T, D), _row)] * 2,
        out_specs=[pl.BlockSpec((T, D), _row), pl.BlockSpec((8, 128), _fixed)],
        out_shape=[jax.ShapeDtypeStruct((S, D), F32), jax.ShapeDtypeStruct((8, 128), F32)],
        compiler_params=_params(("arbitrary",)),
    )(y, target)


BIG = ("w_in", "w_conv_out", "w_att_out", "w_mem_kv", "w_mem_out", "w_out", "w_ffn_in", "w_ffn_out")
SMALL = ("rel_bias", "norm_mix_pre", "b_gate", "conv_dw_bias", "conv_ln_g", "conv_ln_b", "norm_mem",
         "norm_mix_post", "norm_ffn_pre", "norm_ffn_post")


def _layer_fwd(l, x, h, mem, w, rel_bias, gnext):
    tag = f"_l{l}"
    z = _mm_nn(h, w["w_in"], 1024, 2304, BF16, "mm_in" + tag)
    yc, cact = _conv_fwd(z, w["conv_dw"], w["conv_dw_bias"], w["conv_ln_g"], w["conv_ln_b"], "conv_fwd" + tag)
    og, lg = zip(*[_att_fwd(z, rel_bias, g, f"att_fwd_g{g}" + tag) for g in range(3)])
    oatt, lse = _att_combine(og, lg, "att_combine" + tag)
    hm, kv = _memkv_fwd(mem, w["norm_mem"], w["w_mem_kv"], "memkv_fwd" + tag)
    om = _memattn_fwd(z, kv, "memattn_fwd" + tag)
    x1, mg, t, h2 = _merge_fwd(x, cact, oatt, om, z, w["w_conv_out"], w["w_att_out"], w["w_mem_out"], w["w_out"],
                               w["b_gate"], w["norm_mix_post"], w["norm_ffn_pre"], "merge_fwd" + tag)
    gu = _mm_nn(h2, w["w_ffn_in"], 1024, 1408, BF16, "mm_ffn_in" + tag)
    x2, f, *hn = _ffn_fwd(x1, gu, w["w_ffn_out"], w["norm_ffn_post"], gnext, "ffn_fwd" + tag)
    saved = dict(x=x, h=h, z=z, yc=yc, cact=cact, oatt=oatt, lse=lse, hm=hm, kv=kv, om=om, x1=x1, mg=mg, t=t,
                 h2=h2, gu=gu, f=f)
    return x2, (hn[0] if hn else None), saved


def _layer_bwd(l, dx2, mem, w, rel_bias, s):
    tag = f"_l{l}"
    gr = {}
    dgu, df, act, gr["norm_ffn_post"] = _ffn_bwd(dx2, s["f"], s["gu"], w["w_ffn_out"], w["norm_ffn_post"], "ffn_bwd" + tag)
    gr["w_ffn_out"] = _mm_tn(act, df, 1024, 512, "dw_ffn_out" + tag)
    gr["w_ffn_in"] = _mm_tn(s["h2"], dgu, 2048, 1408, "dw_ffn_in" + tag)
    dx1, gr["norm_ffn_pre"] = _mm_nt_rms_bwd(dgu, w["w_ffn_in"], s["x1"], w["norm_ffn_pre"], dx2, 1024, 1408,
                                             "dh_ffn" + tag)
    (dz, dcact, doatt, dom, gr["w_conv_out"], gr["w_att_out"], gr["w_mem_out"], gr["w_out"], gr["b_gate"],
     gr["norm_mix_post"]) = _merge_bwd(dx1, s["t"], s["mg"], s["cact"], s["oatt"], s["om"], s["z"], w["w_conv_out"],
                                       w["w_att_out"], w["w_mem_out"], w["w_out"], w["b_gate"], w["norm_mix_post"],
                                       "merge_bwd" + tag)
    dyc, gr["conv_ln_g"], gr["conv_ln_b"], gr["conv_dw_bias"] = _conv_bwd_ln(
        s["yc"], dcact, w["conv_ln_g"], w["conv_ln_b"], "conv_bwd_ln" + tag)
    dz, dwdw = _conv_bwd_dw(s["z"], dyc, w["conv_dw"], dz, "conv_bwd_dw" + tag)
    gr["conv_dw"] = dwdw[:KSIZE]
    ld = _att_prep(doatt, s["oatt"], s["lse"], "att_prep" + tag)
    drb = []
    for g in range(3):
        dz, db = _att_bwd(s["z"], rel_bias, doatt, ld, dz, g, f"att_bwd_g{g}" + tag)
        drb.append(db)
    dz, dkv = _memattn_bwd(s["z"], s["kv"], dom, dz, "memattn_bwd" + tag)
    gr["w_mem_kv"], gr["norm_mem"] = _memkv_bwd(mem, w["norm_mem"], s["hm"], w["w_mem_kv"], dkv, "memkv_bwd" + tag)
    gr["w_in"] = _mm_tn(s["h"], dz, 2048, 1152, "dw_in" + tag)
    dx, gr["norm_mix_pre"] = _mm_nt_rms_bwd(dz, w["w_in"], s["x"], w["norm_mix_pre"], dx1, 1024, 1152, "dh_in" + tag)
    return dx, gr, drb


def _rel_bias_total(parts, name):
    def body(*refs):
        out_ref = refs[-1]
        acc = jnp.zeros((NUM_BUCKETS, 128), F32)
        for l in range(DEPTH):
            for g in range(3):
                v = refs[l * 3 + g][...]
                acc = acc + (v if g == 0 else pltpu.roll(v, HPG * g, axis=1))
        out_ref[...] = acc

    return pl.pallas_call(body, name=name, out_shape=jax.ShapeDtypeStruct((NUM_BUCKETS, 128), F32),
                          compiler_params=_params())(*[p for layer in parts for p in layer])


def _local_step(x, mem, target, rel_bias, layers):
    saved = []
    h = _rms_h(x, layers[0]["norm_mix_pre"], "rms_mix_l0")
    for l in range(DEPTH):
        gnext = layers[l + 1]["norm_mix_pre"] if l + 1 < DEPTH else None
        x, h, s = _layer_fwd(l, x, h, mem, layers[l], rel_bias, gnext)
        saved.append(s)
    dy, lpart = _loss_head(x, target, "loss_head")
    grads = [None] * DEPTH
    drb = [None] * DEPTH
    for l in reversed(range(DEPTH)):
        dy, grads[l], drb[l] = _layer_bwd(l, dy, mem, layers[l], rel_bias, saved[l])
    return lpart[0, 0], dy, grads, _rel_bias_total(drb, "rel_bias_total")


def _z_cols_from_ref(w):
    att = [w[..., R_ATT + (3 * j + g) * GW:R_ATT + (3 * j + g + 1) * GW] for g in range(3) for j in range(3)]
    return jnp.concatenate([w[..., R_GATE:], w[..., :C1], w[..., R_MEM:R_GATE]] + att, axis=-1)


def _ref_cols_from_z(w):
    att = [w[..., Z_ATT + (3 * g + j) * GW:Z_ATT + (3 * g + j + 1) * GW] for j in range(3) for g in range(3)]
    return jnp.concatenate([w[..., Z_CONV:Z_MEM]] + att + [w[..., Z_MEM:Z_ATT], w[..., Z_GATE:Z_CONV]], axis=-1)


N_CHIPS = 4
SHARD = {"w_in": ((D, NIN // 4), 1), "w_conv_out": ((CW, D // 4), 1), "w_att_out": ((GW, D // 4), 1),
         "w_mem_kv": ((D // 4, 2 * MW), 0), "w_mem_out": ((MW, D // 4), 1), "w_out": ((D // 4, D), 0),
         "w_ffn_in": ((D, 2 * FH // 4), 1), "w_ffn_out": ((FH // 4, D), 0)}
CDW_ROWS = 64
VEC_ROWS = (("norm_mix_pre", 1), ("b_gate", 3), ("conv_dw_bias", 1), ("conv_ln_g", 1), ("conv_ln_b", 1),
            ("norm_mem", 1), ("norm_mix_post", 1), ("norm_ffn_pre", 1), ("norm_ffn_post", 1))
VEC_LROWS = sum(r for _, r in VEC_ROWS)
REL_ROW = DEPTH * VEC_LROWS
CDW_ROW = REL_ROW + 1
CDW_GROWS = DEPTH * KSIZE * CW // D
SMALL_ROWS = -(-(CDW_ROW + CDW_GROWS) // 8) * 8


def _mesh_pos():
    return lax.axis_index("x"), lax.axis_index("y"), lax.axis_index("c")


def _other_chips(x, y):
    chips = [(1 - x, y), (x, 1 - y), (1 - x, 1 - y)]
    return chips, [2 * cx + cy for cx, cy in chips]


NBIG = len(BIG)
ANY_SPEC = pl.BlockSpec(memory_space=pl.ANY)


def _remote(src, dst, send_sems, recv_sems, k, to):
    return pltpu.make_async_remote_copy(src_ref=src, dst_ref=dst, send_sem=send_sems.at[k], recv_sem=recv_sems.at[k],
                                        device_id=to, device_id_type=MESH)


def _all_gather(ws, cdw):
    def body(*refs):
        w_refs, cdw_ref = refs[:NBIG], refs[NBIG]
        g_refs, gc_ref = refs[NBIG + 1:2 * NBIG + 1], refs[2 * NBIG + 1]
        send_sems, recv_sems = refs[2 * NBIG + 2:]
        x, y, c = _mesh_pos()
        j = 2 * x + y
        sibling = (x, y, 1 - c)
        chips, blocks = _other_chips(x, y)
        copy = functools.partial(_remote, send_sems=send_sems, recv_sems=recv_sems)
        pairs = list(zip(w_refs, g_refs))
        first = [copy(w.at[c], g.at[j, c], k=k * NBIG + n, to=(*chip, c))
                 for k, chip in enumerate(chips) for n, (w, g) in enumerate(pairs)]
        first += [copy(cdw_ref, gc_ref.at[j], k=6 * NBIG + k, to=(*chip, c)) for k, chip in enumerate(chips)]
        for cp in first:
            cp.start()
        passed = []
        for k, b in enumerate(blocks):
            for n, (w, g) in enumerate(pairs):
                copy(w.at[c], g.at[b, c], k=k * NBIG + n, to=sibling).wait_recv()
            onward = [copy(g.at[b, c], g.at[b, c], k=(3 + k) * NBIG + n, to=sibling) for n, (w, g) in enumerate(pairs)]
            for cp in onward:
                cp.start()
            passed += onward
        for k, b in enumerate(blocks):
            for n, (w, g) in enumerate(pairs):
                copy(w.at[c], g.at[b, 1 - c], k=(3 + k) * NBIG + n, to=sibling).wait_recv()
            copy(cdw_ref, gc_ref.at[b], k=6 * NBIG + k, to=sibling).wait_recv()
        for cp in first + passed:
            cp.wait_send()

    nsem = 6 * NBIG + 3
    return pl.pallas_call(
        body, name="all_gather_weights",
        out_shape=[jax.ShapeDtypeStruct((N_CHIPS,) + w.shape, BF16) for w in ws]
        + [jax.ShapeDtypeStruct((N_CHIPS, CDW_ROWS, 128), F32)],
        in_specs=[ANY_SPEC] * (NBIG + 1), out_specs=[ANY_SPEC] * (NBIG + 1),
        scratch_shapes=[pltpu.SemaphoreType.DMA((nsem,)), pltpu.SemaphoreType.DMA((nsem,))],
    )(*ws, cdw)


def _sibling_exchange(ps):
    def body(*refs):
        p_refs, r_refs, (send_sems, recv_sems) = refs[:NBIG], refs[NBIG:2 * NBIG], refs[2 * NBIG:]
        x, y, c = _mesh_pos()
        cps = [_remote(p.at[1 - c], r, send_sems, recv_sems, n, (x, y, 1 - c))
               for n, (p, r) in enumerate(zip(p_refs, r_refs))]
        for cp in cps:
            cp.start()
        for cp in cps:
            cp.wait()

    return pl.pallas_call(
        body, name="grad_sibling_exchange", out_shape=[jax.ShapeDtypeStruct(p.shape[1:], p.dtype) for p in ps],
        in_specs=[ANY_SPEC] * NBIG, out_specs=[ANY_SPEC] * NBIG,
        scratch_shapes=[pltpu.SemaphoreType.DMA((NBIG,)), pltpu.SemaphoreType.DMA((NBIG,))],
    )(*ps)


SUM_BLOCK_BYTES = 2 * 1024 * 1024


def _sum_rows(s0, s1):
    return s0 if s0 * s1 * 2 <= SUM_BLOCK_BYTES else s0 // 2


def _add_own_layer(where, p, r, name):
    _, _, s0, s1 = p.shape
    T = _sum_rows(s0, s1)

    def body(where_ref, p_ref, r_ref, o_ref):
        o_ref[...] = (p_ref[0].astype(F32) + r_ref[...].astype(F32)).astype(BF16)

    return pl.pallas_call(
        body, name=name,
        grid_spec=pltpu.PrefetchScalarGridSpec(
            num_scalar_prefetch=1, grid=(N_CHIPS, s0 // T),
            in_specs=[pl.BlockSpec((1, 1, T, s1), lambda j, i, wh: (wh[0], j, i, 0)),
                      pl.BlockSpec((1, T, s1), lambda j, i, wh: (j, i, 0))],
            out_specs=pl.BlockSpec((1, T, s1), lambda j, i, wh: (j, i, 0))),
        out_shape=jax.ShapeDtypeStruct(r.shape, BF16), compiler_params=_params(("parallel", "parallel")),
    )(where, p, r)


def _chip_exchange(as_):
    def body(*refs):
        a_refs, r_refs, (send_sems, recv_sems) = refs[:NBIG], refs[NBIG:2 * NBIG], refs[2 * NBIG:]
        x, y, c = _mesh_pos()
        chips, blocks = _other_chips(x, y)
        cps = [_remote(a.at[b], r.at[k], send_sems, recv_sems, k * NBIG + n, (*chip, c))
               for k, (chip, b) in enumerate(zip(chips, blocks)) for n, (a, r) in enumerate(zip(a_refs, r_refs))]
        for cp in cps:
            cp.start()
        for cp in cps:
            cp.wait_recv()
        for cp in cps:
            cp.wait_send()

    return pl.pallas_call(
        body, name="grad_chip_exchange", out_shape=[jax.ShapeDtypeStruct((3,) + a.shape[1:], a.dtype) for a in as_],
        in_specs=[ANY_SPEC] * NBIG, out_specs=[ANY_SPEC] * NBIG,
        scratch_shapes=[pltpu.SemaphoreType.DMA((3 * NBIG,)), pltpu.SemaphoreType.DMA((3 * NBIG,))],
    )(*as_)


def _sum_chips(where, a, r, name):
    _, s0, s1 = a.shape
    T = _sum_rows(s0, s1)

    def body(where_ref, a_ref, r_ref, o_ref):
        acc = a_ref[0].astype(F32)
        for k in range(3):
            acc = acc + r_ref[k].astype(F32)
        o_ref[0] = acc

    return pl.pallas_call(
        body, name=name,
        grid_spec=pltpu.PrefetchScalarGridSpec(
            num_scalar_prefetch=1, grid=(s0 // T,),
            in_specs=[pl.BlockSpec((1, T, s1), lambda i, wh: (wh[1], i, 0)),
                      pl.BlockSpec((3, T, s1), lambda i, wh: (0, i, 0))],
            out_specs=pl.BlockSpec((1, T, s1), lambda i, wh: (wh[0], i, 0))),
        out_shape=jax.ShapeDtypeStruct((DEPTH, s0, s1), F32), compiler_params=_params(("parallel",)),
    )(where, a, r)


def _sibling_share(os_):
    def body(*refs):
        o_refs, (send_sems, recv_sems) = refs[NBIG:2 * NBIG], refs[2 * NBIG:]
        x, y, c = _mesh_pos()
        cps = [_remote(o.at[c], o.at[c], send_sems, recv_sems, n, (x, y, 1 - c)) for n, o in enumerate(o_refs)]
        for cp in cps:
            cp.start()
        for n, o in enumerate(o_refs):
            _remote(o.at[c], o.at[1 - c], send_sems, recv_sems, n, (x, y, 1 - c)).wait_recv()
        for cp in cps:
            cp.wait_send()

    return pl.pallas_call(
        body, name="grad_sibling_share", out_shape=[jax.ShapeDtypeStruct(o.shape, o.dtype) for o in os_],
        in_specs=[ANY_SPEC] * NBIG, out_specs=[ANY_SPEC] * NBIG,
        input_output_aliases={n: n for n in range(NBIG)},
        scratch_shapes=[pltpu.SemaphoreType.DMA((NBIG,)), pltpu.SemaphoreType.DMA((NBIG,))],
    )(*os_)


def _all_reduce_small(sp):
    def body(sp_ref, out_ref, buf, send_sems, recv_sems):
        x, y, c = _mesh_pos()
        me = 4 * x + 2 * y + c
        buf[0] = sp_ref[...]
        cps = []
        for k in range(1, 8):
            peer = (x ^ (k >> 2 & 1), y ^ (k >> 1 & 1), c ^ (k & 1))
            cps.append(pltpu.make_async_remote_copy(src_ref=sp_ref, dst_ref=buf.at[k], send_sem=send_sems.at[k - 1],
                                                    recv_sem=recv_sems.at[k - 1], device_id=peer, device_id_type=MESH))
        for cp in cps:
            cp.start()
        for cp in cps:
            cp.wait_recv()
        for cp in cps:
            cp.wait_send()
        acc = buf[me]
        for p in range(1, 8):
            acc = acc + buf[p ^ me]
        out_ref[...] = acc

    vm = pl.BlockSpec(memory_space=pltpu.VMEM)
    return pl.pallas_call(
        body, name="all_reduce_small", out_shape=jax.ShapeDtypeStruct(sp.shape, F32),
        in_specs=[vm], out_specs=vm,
        scratch_shapes=[pltpu.VMEM((8,) + sp.shape, F32), pltpu.SemaphoreType.DMA((7,)), pltpu.SemaphoreType.DMA((7,))],
        compiler_params=_params(),
    )(sp)


def _adamw(w, g, m, v, name):
    R, C = w.shape
    T = next((t for t in (256, 128) if R % t == 0), R)

    def body(w_ref, g_ref, m_ref, v_ref, d_ref, m2_ref, v2_ref):
        gv = g_ref[...]
        m2 = ADAM_B1 * m_ref[...] + (1.0 - ADAM_B1) * gv
        v2 = ADAM_B2 * v_ref[...] + (1.0 - ADAM_B2) * (gv * gv)
        m_hat = m2 / (1.0 - ADAM_B1 ** ADAM_STEP)
        v_hat = v2 / (1.0 - ADAM_B2 ** ADAM_STEP)
        d_ref[...] = -ADAM_LR * (m_hat / (jnp.sqrt(v_hat) + ADAM_EPS) + ADAM_WD * w_ref[...])
        m2_ref[...] = m2
        v2_ref[...] = v2

    blk = pl.BlockSpec((T, C), _row)
    return pl.pallas_call(
        body, name=name, grid=(R // T,), in_specs=[blk] * 4, out_specs=[blk] * 3,
        out_shape=[jax.ShapeDtypeStruct((R, C), F32)] * 3, compiler_params=_params(("parallel",)),
    )(w, g, m, v)


def _pack_vectors(get, rel, cdw, name):
    rows = []
    for l in range(DEPTH):
        for n, r in VEC_ROWS:
            v = get(n)[l]
            rows.append(jnp.pad(v, (0, r * D - v.shape[0])).reshape(r, D))
    rows.append(jnp.pad(rel.reshape(-1), (0, D - NUM_BUCKETS * 3 * HPG)).reshape(1, D))
    rows.append(cdw.reshape(CDW_GROWS, D))

    def body(*refs):
        out_ref = refs[-1]
        out_ref[...] = jnp.zeros_like(out_ref)
        at = 0
        for ref in refs[:-1]:
            out_ref[at:at + ref.shape[0], :] = ref[...]
            at += ref.shape[0]

    return pl.pallas_call(body, name=name, out_shape=jax.ShapeDtypeStruct((SMALL_ROWS, D), F32),
                          compiler_params=_params())(*rows)


def _unpack_vectors(packed, lens):
    out = {n: [] for n, _ in VEC_ROWS}
    for l in range(DEPTH):
        at = l * VEC_LROWS
        for n, r in VEC_ROWS:
            out[n].append(packed[at:at + r].reshape(-1)[:lens[n]])
            at += r
    rel = packed[REL_ROW, :NUM_BUCKETS * 3 * HPG].reshape(NUM_BUCKETS, 3 * HPG)
    return {n: jnp.stack(v) for n, v in out.items()}, rel


INPUT_NAMES = ("x", "mem") + ("rel_bias", "norm_mix_pre", "w_in", "b_gate", "conv_dw", "conv_dw_bias", "conv_ln_g",
                              "conv_ln_b", "w_conv_out", "w_att_out", "norm_mem", "w_mem_kv", "w_mem_out", "w_out",
                              "norm_mix_post", "norm_ffn_pre", "w_ffn_in", "w_ffn_out", "norm_ffn_post")
WEIGHT_NAMES = INPUT_NAMES[2:]


def kernel(*args):
    nw = len(WEIGHT_NAMES)
    a = dict(zip(INPUT_NAMES, args[:2 + nw]))
    target = args[2 + nw]
    mom = dict(zip(WEIGHT_NAMES, args[3 + nw:3 + 2 * nw]))
    var = dict(zip(WEIGHT_NAMES, args[3 + 2 * nw:3 + 3 * nw]))
    xi, yi, ci = _mesh_pos()
    chip = 2 * xi + yi
    where = jnp.stack([ci, chip]).astype(I32)

    shards = [a[n].astype(BF16) for n in BIG]
    cdw = jnp.pad(a["conv_dw"].reshape(DEPTH * KSIZE, CW // 4), ((0, CDW_ROWS - DEPTH * KSIZE), (0, 0)))
    *gathered, gcdw = _all_gather(shards, cdw)
    gathered = [lax.dynamic_update_slice(g, s[None], (chip, 0, 0, 0)) for g, s in zip(gathered, shards)]
    gcdw = lax.dynamic_update_slice(gcdw, cdw[None], (chip, 0, 0))
    conv_dw = gcdw[:, :DEPTH * KSIZE].reshape(N_CHIPS, DEPTH, KSIZE, CW // 4).transpose(1, 2, 0, 3)
    conv_dw = jnp.pad(conv_dw.reshape(DEPTH, KSIZE, CW), ((0, 0), (0, 1), (0, 0)))
    layers = []
    for l in range(DEPTH):
        w = {"conv_dw": conv_dw[l]}
        for n, g in zip(BIG, gathered):
            (s0, s1), axis = SHARD[n]
            blk = g[:, l]
            w[n] = blk.reshape(N_CHIPS * s0, s1) if axis == 0 else blk.transpose(1, 0, 2).reshape(s0, N_CHIPS * s1)
        w["w_in"] = _z_cols_from_ref(w["w_in"])
        for n, _ in VEC_ROWS:
            w[n] = a[n][l][None, :]
        layers.append(w)

    loss_part, gx, grads, drel = _local_step(a["x"][0], a["mem"][0], target[0], a["rel_bias"], layers)
    loss = lax.psum(loss_part, ("x", "y", "c"))

    packed = []
    for n in BIG:
        (s0, s1), axis = SHARD[n]
        per_layer = []
        for l in range(DEPTH):
            g = _ref_cols_from_z(grads[l][n]) if n == "w_in" else grads[l][n]
            per_layer.append(g.reshape(N_CHIPS, s0, s1) if axis == 0 else g.reshape(s0, N_CHIPS, s1).transpose(1, 0, 2))
        packed.append(jnp.stack(per_layer).astype(BF16))
    from_sibling = _sibling_exchange(packed)
    chip_sums = [_add_own_layer(where, p, r, "grad_add_sibling_" + n) for n, p, r in zip(BIG, packed, from_sibling)]
    from_chips = _chip_exchange(chip_sums)
    reduced = _sibling_share([_sum_chips(where, s, r, "grad_sum_chips_" + n)
                              for n, s, r in zip(BIG, chip_sums, from_chips)])

    gvec = _all_reduce_small(_pack_vectors(
        lambda n: jnp.stack([grads[l][n][0] for l in range(DEPTH)]), drel[:, :3 * HPG],
        jnp.stack([grads[l]["conv_dw"] for l in range(DEPTH)]), "pack_vector_grads"))
    lens = {n: a[n].shape[1] for n, _ in VEC_ROWS}
    g_vec, g_rel = _unpack_vectors(gvec, lens)
    g_cdw = lax.dynamic_slice_in_dim(gvec[CDW_ROW:CDW_ROW + CDW_GROWS].reshape(DEPTH, KSIZE, CW), chip * (CW // 4),
                                     CW // 4, axis=2)

    grad, delta, new_m, new_v = {}, {}, {}, {}
    for n, g in zip(BIG, reduced):
        shape = a[n].shape
        flat2 = lambda t: t.reshape(shape[0] * shape[1], shape[2])
        d, m2, v2 = _adamw(flat2(a[n]), flat2(g), flat2(mom[n]), flat2(var[n]), "adamw_" + n)
        grad[n], delta[n], new_m[n], new_v[n] = g, d.reshape(shape), m2.reshape(shape), v2.reshape(shape)
    shape = a["conv_dw"].shape
    flat2 = lambda t: t.reshape(shape[0] * shape[1], shape[2])
    d, m2, v2 = _adamw(flat2(a["conv_dw"]), flat2(g_cdw), flat2(mom["conv_dw"]), flat2(var["conv_dw"]), "adamw_conv_dw")
    grad["conv_dw"], delta["conv_dw"], new_m["conv_dw"], new_v["conv_dw"] = (
        g_cdw, d.reshape(shape), m2.reshape(shape), v2.reshape(shape))
    zero_cdw = jnp.zeros((DEPTH, KSIZE, CW), F32)
    pk = lambda src, name: _pack_vectors(lambda n: src[n], src["rel_bias"], zero_cdw, name)
    d, m2, v2 = _adamw(pk(a, "pack_vector_w"), gvec, pk(mom, "pack_vector_m"), pk(var, "pack_vector_v"),
                       "adamw_vectors")
    for src, dst in ((d, delta), (m2, new_m), (v2, new_v)):
        vec, rel = _unpack_vectors(src, lens)
        dst.update(vec)
        dst["rel_bias"] = rel
    grad.update(g_vec)
    grad["rel_bias"] = g_rel

    outs = [loss, gx[None]]
    for group in (grad, delta, new_m, new_v):
        outs += [group[n] for n in WEIGHT_NAMES]
    return tuple(outs)
```

```python
import functools
import math

import jax
import jax.numpy as jnp
from jax import lax
from jax.experimental import pallas as pl
from jax.experimental.pallas import tpu as pltpu

F32 = jnp.float32
BF16 = jnp.bfloat16
I32 = jnp.int32

D = 1024
DEPTH = 2
N_MEM = 256
CW = 512
KSIZE = 31
PAD = KSIZE // 2
DILS = (1, 4, 16)
RADIUS = 64
HPG = 4
HD = 64
GW = HPG * HD
MH = 4
MHD = 128
MW = MH * MHD
FH = 2816
NIN = 6912
C1 = 2 * CW
R_ATT = C1
R_MEM = R_ATT + 9 * GW
R_GATE = R_MEM + MW
Z_GATE = 0
Z_CONV = 3 * D
Z_MEM = Z_CONV + C1
Z_ATT = Z_MEM + MW
NUM_BUCKETS = 32
MAX_DISTANCE = 1024
RMS_EPS = 1e-6
LN_EPS = 1e-5
NEG_INF = -1e30
ATT_SCALE = HD ** -0.5
MEM_SCALE = MHD ** -0.5

ADAM_LR = 0.001
ADAM_B1 = 0.9
ADAM_B2 = 0.999
ADAM_EPS = 1e-08
ADAM_WD = 0.01
ADAM_STEP = 10

VMEM_LIMIT_BYTES = 56 * 1024 * 1024
ATT_QB = 128
ATT_TB = 16 * ATT_QB

MESH = pl.DeviceIdType.MESH


def _params(sem=None):
    return pltpu.CompilerParams(dimension_semantics=sem, vmem_limit_bytes=VMEM_LIMIT_BYTES)


def _sigmoid(v):
    return 1.0 / (1.0 + jnp.exp(-v))


def _dot(a, b):
    return jnp.dot(a, b, preferred_element_type=F32)


def _dot_nt(a, b):
    return lax.dot_general(a, b, (((1,), (1,)), ((), ())), preferred_element_type=F32)


def _dot_tn(a, b):
    return lax.dot_general(a, b, (((0,), (0,)), ((), ())), preferred_element_type=F32)


def _rms_fwd_val(v, g):
    r = lax.rsqrt(jnp.mean(v * v, axis=-1, keepdims=True) + RMS_EPS)
    return v * r * g


def _rms_bwd_val(v, g, dy):
    r = lax.rsqrt(jnp.mean(v * v, axis=-1, keepdims=True) + RMS_EPS)
    vh = v * r
    dvh = dy * g
    dv = r * (dvh - vh * jnp.mean(dvh * vh, axis=-1, keepdims=True))
    return dv, dy * vh


def _row(i):
    return (i, 0)


def _fixed(*_):
    return (0, 0)


def _mm_nn(a, b, tm, out_dtype, name):
    M, K = a.shape
    N = b.shape[1]

    def body(a_ref, b_ref, o_ref):
        o_ref[...] = _dot(a_ref[...], b_ref[...]).astype(out_dtype)

    return pl.pallas_call(
        body, name=name, grid=(M // tm,),
        in_specs=[pl.BlockSpec((tm, K), _row), pl.BlockSpec((K, N), _fixed, pipeline_mode=pl.Buffered(1))],
        out_specs=pl.BlockSpec((tm, N), _row),
        out_shape=jax.ShapeDtypeStruct((M, N), out_dtype),
        compiler_params=_params(("parallel",)),
    )(a, b)


def _mm_nt_rms_bwd(a, b, x, g, dres, tm, name):
    M, N = a.shape

    def body(a_ref, b_ref, x_ref, g_ref, dres_ref, dx_ref, dg_ref):
        @pl.when(pl.program_id(0) == 0)
        def _():
            dg_ref[...] = jnp.zeros_like(dg_ref)

        dv, dgr = _rms_bwd_val(x_ref[...], g_ref[...], _dot_nt(a_ref[...], b_ref[...]))
        dx_ref[...] = dres_ref[...] + dv
        dg_ref[...] += jnp.sum(dgr, axis=0, keepdims=True)

    rows = pl.BlockSpec((tm, D), _row)
    return pl.pallas_call(
        body, name=name, grid=(M // tm,),
        in_specs=[pl.BlockSpec((tm, N), _row), pl.BlockSpec((D, N), _fixed, pipeline_mode=pl.Buffered(1)), rows,
                  pl.BlockSpec((1, D), _fixed), rows],
        out_specs=[rows, pl.BlockSpec((1, D), _fixed)],
        out_shape=[jax.ShapeDtypeStruct((M, D), F32), jax.ShapeDtypeStruct((1, D), F32)],
        compiler_params=_params(("arbitrary",)),
    )(a, b, x, g, dres)


def _mm_tn(a, b, ts, tn, name):
    S, K = a.shape
    N = b.shape[1]

    def body(a_ref, b_ref, o_ref):
        @pl.when(pl.program_id(1) == 0)
        def _():
            o_ref[...] = jnp.zeros_like(o_ref)

        o_ref[...] += _dot_tn(a_ref[...], b_ref[...])

    return pl.pallas_call(
        body, name=name, grid=(N // tn, S // ts),
        in_specs=[pl.BlockSpec((ts, K), lambda j, s: (s, 0)), pl.BlockSpec((ts, tn), lambda j, s: (s, j))],
        out_specs=pl.BlockSpec((K, tn), lambda j, s: (0, j)),
        out_shape=jax.ShapeDtypeStruct((K, N), F32),
        compiler_params=_params(("parallel", "arbitrary")),
    )(a, b)


def _rms_h(x, g, name):
    S = x.shape[0]
    T = 512

    def body(x_ref, g_ref, h_ref):
        h_ref[...] = _rms_fwd_val(x_ref[...], g_ref[...]).astype(BF16)

    return pl.pallas_call(
        body, name=name, grid=(S // T,),
        in_specs=[pl.BlockSpec((T, D), _row), pl.BlockSpec((1, D), _fixed)],
        out_specs=pl.BlockSpec((T, D), _row),
        out_shape=jax.ShapeDtypeStruct((S, D), BF16),
        compiler_params=_params(("parallel",)),
    )(x, g)


CONV_T = 256
CONV_HALO = 16
CONV_RC = 32


def _halo_specs(T, halo, S, width, col):
    per = T // halo
    last = S // halo - 1
    return [
        pl.BlockSpec((T, width), lambda i: (i, col)),
        pl.BlockSpec((halo, width), lambda i: (jnp.maximum(i * per - 1, 0), col)),
        pl.BlockSpec((halo, width), lambda i: (jnp.minimum((i + 1) * per, last), col)),
    ]


def _glu(zb):
    zb = zb.astype(F32)
    return zb[:, :CW] * _sigmoid(zb[:, CW:])


CONV_EXT = CONV_T + 2 * CONV_HALO
SUBLANES = 8


def _fill_shifted(sh_ref, ext_ref, cur, prev, nxt):
    T, halo = CONV_T, CONV_HALO
    i = pl.program_id(0)
    n = pl.num_programs(0)
    ext_ref[0:halo, :] = jnp.where(i > 0, prev, 0.0)
    ext_ref[halo:halo + T, :] = cur
    ext_ref[halo + T:CONV_EXT, :] = jnp.where(i < n - 1, nxt, 0.0)
    ext_ref[CONV_EXT:CONV_EXT + SUBLANES, :] = jnp.zeros((SUBLANES, CW), F32)
    for b in range(SUBLANES):
        sh_ref[b] = ext_ref[b:b + CONV_EXT, :]


def _window(sh_ref, start, rows):
    b = start % SUBLANES
    return sh_ref[b, start - b:start - b + rows, :]


def _shifted_scratch():
    return [pltpu.VMEM((CONV_EXT + SUBLANES, CW), F32), pltpu.VMEM((SUBLANES, CONV_EXT, CW), F32)]


def _conv_fwd(z, wdw, bdw, lng, lnb, name):
    S = z.shape[0]
    T, HL, RC = CONV_T, CONV_HALO, CONV_RC

    def body(cur_ref, prev_ref, next_ref, w_ref, b_ref, g_ref, bb_ref, yc_ref, act_ref, ext_ref, sh_ref):
        _fill_shifted(sh_ref, ext_ref, _glu(cur_ref[...]), _glu(prev_ref[...]), _glu(next_ref[...]))
        for c in range(T // RC):
            acc = jnp.zeros((RC, CW), F32)
            for k in range(KSIZE):
                acc = acc + w_ref[k:k + 1, :] * _window(sh_ref, c * RC + k + HL - PAD, RC)
            yc = acc + b_ref[...]
            yc_ref[c * RC:(c + 1) * RC, :] = yc
            mu = jnp.mean(yc, axis=-1, keepdims=True)
            xc = yc - mu
            ln = xc * lax.rsqrt(jnp.mean(xc * xc, axis=-1, keepdims=True) + LN_EPS) * g_ref[...] + bb_ref[...]
            act_ref[c * RC:(c + 1) * RC, :] = (ln * _sigmoid(ln)).astype(BF16)

    return pl.pallas_call(
        body, name=name, grid=(S // T,),
        in_specs=_halo_specs(T, HL, S, C1, Z_CONV // C1) + [pl.BlockSpec((32, CW), _fixed)]
        + [pl.BlockSpec((1, CW), _fixed)] * 3,
        out_specs=[pl.BlockSpec((T, CW), _row), pl.BlockSpec((T, CW), _row)],
        out_shape=[jax.ShapeDtypeStruct((S, CW), F32), jax.ShapeDtypeStruct((S, CW), BF16)],
        scratch_shapes=_shifted_scratch(),
        compiler_params=_params(("parallel",)),
    )(z, z, z, wdw, bdw, lng, lnb)


def _conv_bwd_ln(yc, dact, lng, lnb, name):
    S = yc.shape[0]
    T = 512

    def body(yc_ref, da_ref, g_ref, b_ref, dyc_ref, dg_ref, db_ref, dbias_ref):
        yc_v = yc_ref[...]
        mu = jnp.mean(yc_v, axis=-1, keepdims=True)
        xc = yc_v - mu
        r = lax.rsqrt(jnp.mean(xc * xc, axis=-1, keepdims=True) + LN_EPS)
        yn = xc * r
        ln = yn * g_ref[...] + b_ref[...]
        sg = _sigmoid(ln)
        dln = da_ref[...].astype(F32) * (sg * (1.0 + ln * (1.0 - sg)))
        dyn = dln * g_ref[...]
        dyc = r * (dyn - jnp.mean(dyn, axis=-1, keepdims=True) - yn * jnp.mean(dyn * yn, axis=-1, keepdims=True))
        dyc_ref[...] = dyc

        @pl.when(pl.program_id(0) == 0)
        def _():
            dg_ref[...] = jnp.zeros_like(dg_ref)
            db_ref[...] = jnp.zeros_like(db_ref)
            dbias_ref[...] = jnp.zeros_like(dbias_ref)

        dg_ref[...] += jnp.sum(dln * yn, axis=0, keepdims=True)
        db_ref[...] += jnp.sum(dln, axis=0, keepdims=True)
        dbias_ref[...] += jnp.sum(dyc, axis=0, keepdims=True)

    vec = pl.BlockSpec((1, CW), _fixed)
    return pl.pallas_call(
        body, name=name, grid=(S // T,),
        in_specs=[pl.BlockSpec((T, CW), _row), pl.BlockSpec((T, CW), _row), vec, vec],
        out_specs=[pl.BlockSpec((T, CW), _row), vec, vec, vec],
        out_shape=[jax.ShapeDtypeStruct((S, CW), F32)] + [jax.ShapeDtypeStruct((1, CW), F32)] * 3,
        compiler_params=_params(("arbitrary",)),
    )(yc, dact, lng, lnb)


def _conv_bwd_dw(z, dyc, wdw, dz, name):
    S = z.shape[0]
    T, HL, RC = CONV_T, CONV_HALO, CONV_RC

    def body(zc_ref, zp_ref, zn_ref, dc_ref, dp_ref, dn_ref, w_ref, dz_in, dz_ref, dw_ref, uext_ref, ush_ref,
             dext_ref, dsh_ref, dwacc_ref):
        _fill_shifted(ush_ref, uext_ref, _glu(zc_ref[...]), _glu(zp_ref[...]), _glu(zn_ref[...]))
        _fill_shifted(dsh_ref, dext_ref, dc_ref[...], dp_ref[...], dn_ref[...])

        @pl.when(pl.program_id(0) == 0)
        def _():
            dwacc_ref[...] = jnp.zeros_like(dwacc_ref)

        for c in range(T // RC):
            dcur = dc_ref[c * RC:(c + 1) * RC, :]
            du = jnp.zeros((RC, CW), F32)
            for k in range(KSIZE):
                du = du + w_ref[k:k + 1, :] * _window(dsh_ref, c * RC + HL + PAD - k, RC)
                prod = dcur * _window(ush_ref, c * RC + k + HL - PAD, RC)
                dwacc_ref[k] += jnp.sum(prod.reshape(RC // SUBLANES, SUBLANES, CW), axis=0)
            zc = zc_ref[c * RC:(c + 1) * RC, :].astype(F32)
            a, gt = zc[:, :CW], zc[:, CW:]
            sg = _sigmoid(gt)
            dz_ref[c * RC:(c + 1) * RC, 0:CW] = (du * sg).astype(BF16)
            dz_ref[c * RC:(c + 1) * RC, CW:C1] = (du * a * sg * (1.0 - sg)).astype(BF16)

        @pl.when(pl.program_id(0) == pl.num_programs(0) - 1)
        def _():
            dw_ref[...] = jnp.sum(dwacc_ref[...], axis=1)

    return pl.pallas_call(
        body, name=name, grid=(S // T,),
        in_specs=_halo_specs(T, HL, S, C1, Z_CONV // C1) + _halo_specs(T, HL, S, CW, 0)
        + [pl.BlockSpec((32, CW), _fixed), pl.BlockSpec(memory_space=pl.ANY)],
        out_specs=[pl.BlockSpec((T, C1), lambda i: (i, Z_CONV // C1)), pl.BlockSpec((32, CW), _fixed)],
        out_shape=[jax.ShapeDtypeStruct(dz.shape, BF16), jax.ShapeDtypeStruct((32, CW), F32)],
        input_output_aliases={7: 0},
        scratch_shapes=_shifted_scratch() + _shifted_scratch() + [pltpu.VMEM((32, SUBLANES, CW), F32)],
        compiler_params=_params(("arbitrary",)),
    )(z, z, z, dyc, dyc, dyc, wdw, dz)


def _t5_bucket(rel):
    nb = NUM_BUCKETS // 2
    max_exact = nb // 2
    ret = jnp.where(rel > 0, nb, 0)
    n = jnp.abs(rel)
    nf = jnp.maximum(n, 1).astype(F32)
    large = max_exact + (jnp.log(nf / max_exact) / math.log(MAX_DISTANCE / max_exact)
                         * (nb - max_exact)).astype(I32)
    large = jnp.minimum(large, nb - 1)
    return ret + jnp.where(n < max_exact, n, large)


def _offsets_qk(nq, nk, shift):
    return lax.broadcasted_iota(I32, (nq, nk), 1) + shift - lax.broadcasted_iota(I32, (nq, nk), 0)


def _bias_table(bk, rb_ref, col, off):
    acc = jnp.zeros(bk.shape, F32)
    for b in range(NUM_BUCKETS):
        acc = jnp.where(bk == b, rb_ref[b, col], acc)
    return jnp.where(jnp.abs(off) <= RADIUS, acc, NEG_INF)


def _to_halves(scr, row0, val):
    rows = val.shape[0]
    v = val.astype(F32)
    scr[0, row0:row0 + rows, :] = v[:, :128]
    scr[1, row0:row0 + rows, :] = v[:, 128:]


ATT_FWD_GROUP = 2
ATT_BWD_GROUP = 1


def _att_units(d, fn, group):
    nj = ATT_TB // (ATT_QB * d)
    if nj == 1:
        def trip(t, c):
            r0 = pl.multiple_of(t * 8, 8)
            for u in range(0, 8, group):
                fn([(r0 + u + v, 0) for v in range(group)])
            return c

        lax.fori_loop(0, d // 8, trip, 0)
        return
    for r in range(d):
        def step(t, c, r=r):
            fn([(r, t * group + u) for u in range(group)])
            return c

        lax.fori_loop(0, nj // group, step, 0)


def _unit_row(r, j, d):
    if isinstance(j, int):
        return j * ATT_QB * d + r
    return pl.multiple_of(j * (ATT_QB * d), ATT_QB) + r


def _att_fwd(z, rel_bias, g, name):
    S = z.shape[0]
    d = DILS[g]
    TB, QB = ATT_TB, ATT_QB
    H = RADIUS * d
    L = S // d
    cq = (Z_ATT + 3 * GW * g) // GW
    ck, cv = cq + 1, cq + 2
    bk = _t5_bucket(_offsets_qk(QB, 2 * QB, -RADIUS) * d)

    def body(rb_ref, bk_ref, q_ref, kc_ref, kp_ref, kn_ref, vc_ref, vp_ref, vn_ref, o_ref, l_ref,
             qs, ks, vs, os_, ls, bias):
        i = pl.program_id(0)

        @pl.when(i == 0)
        def _():
            off = _offsets_qk(QB, 2 * QB, -RADIUS)
            for h in range(HPG):
                bias[h] = _bias_table(bk_ref[...], rb_ref, g * HPG + h, off)

        _to_halves(qs, 0, q_ref[...].astype(F32) * ATT_SCALE)
        for scr, p_ref, c_ref, n_ref in ((ks, kp_ref, kc_ref, kn_ref), (vs, vp_ref, vc_ref, vn_ref)):
            _to_halves(scr, 0, p_ref[...])
            _to_halves(scr, H, c_ref[...])
            _to_halves(scr, H + TB, n_ref[...])

        lo = lax.broadcasted_iota(I32, (QB, 128), 1) < HD

        def units(rjs):
            work = []
            for r, j in rjs:
                row = _unit_row(r, j, d)
                km = lax.broadcasted_iota(I32, (1, 2 * QB), 1) + (i * (TB // d) + j * QB - RADIUS)
                edge = jnp.where(jnp.where(km >= 0, km, L) < L, 0.0, NEG_INF)
                for hf in (0, 1):
                    q2 = qs[hf, pl.ds(row, QB, stride=d), :]
                    k2 = ks[hf, pl.ds(row, 2 * QB, stride=d), :].astype(BF16)
                    v2 = vs[hf, pl.ds(row, 2 * QB, stride=d), :].astype(BF16)
                    qq = jnp.concatenate([jnp.where(lo, q2, 0.0), jnp.where(lo, 0.0, q2)], axis=0).astype(BF16)
                    work.append((row, hf, edge, k2, v2, qq))
            scores = [_dot_nt(qq, k2) for (_, _, _, k2, _, qq) in work]
            probs = []
            for (row, hf, edge, *_), ss in zip(work, scores):
                es, stats = [], []
                for hh in (0, 1):
                    s = ss[hh * QB:(hh + 1) * QB] + bias[2 * hf + hh] + edge
                    m = jnp.max(s, axis=-1, keepdims=True)
                    e = jnp.exp(s - m)
                    den = jnp.sum(e, axis=-1, keepdims=True)
                    es.append(e.astype(BF16))
                    stats.append((1.0 / den, m + jnp.log(den)))
                probs.append((jnp.concatenate(es, axis=0), stats))
            for (row, hf, _, _, v2, _), (ee, stats) in zip(work, probs):
                oo = _dot(ee, v2)
                os_[hf, pl.ds(row, QB, stride=d), :] = jnp.where(lo, oo[:QB] * stats[0][0], oo[QB:] * stats[1][0])
                ls[hf, pl.ds(row, QB, stride=d), :] = jnp.where(lo, stats[0][1], stats[1][1])

        _att_units(d, units, ATT_FWD_GROUP)
        for hf in (0, 1):
            o_ref[:, hf * 128:(hf + 1) * 128] = os_[hf].astype(BF16)
            l_ref[:, hf * 128:(hf + 1) * 128] = ls[hf]

    def halo3(col):
        c, p, n = _halo_specs(TB, H, S, GW, col)
        return [c, p, n]

    return pl.pallas_call(
        body, name=name, grid=(S // TB,),
        in_specs=[pl.BlockSpec(memory_space=pltpu.SMEM), pl.BlockSpec((QB, 2 * QB), _fixed),
                  pl.BlockSpec((TB, GW), lambda i: (i, cq))] + halo3(ck) + halo3(cv),
        out_specs=[pl.BlockSpec((TB, GW), _row), pl.BlockSpec((TB, GW), _row)],
        out_shape=[jax.ShapeDtypeStruct((S, GW), BF16), jax.ShapeDtypeStruct((S, GW), F32)],
        scratch_shapes=[pltpu.VMEM((2, TB, 128), F32), pltpu.VMEM((2, TB + 2 * H, 128), F32),
                        pltpu.VMEM((2, TB + 2 * H, 128), F32), pltpu.VMEM((2, TB, 128), F32),
                        pltpu.VMEM((2, TB, 128), F32), pltpu.VMEM((HPG, QB, 2 * QB), F32)],
        compiler_params=_params(("arbitrary",)),
    )(rel_bias, bk, z, z, z, z, z, z, z)


def _att_combine(os3, ls3, name):
    S = os3[0].shape[0]
    T = 1024

    def body(o1, o2, o3, l1, l2, l3, o_ref, l_ref):
        lv = [l1[...], l2[...], l3[...]]
        m = jnp.maximum(jnp.maximum(lv[0], lv[1]), lv[2])
        e = [jnp.exp(v - m) for v in lv]
        den = e[0] + e[1] + e[2]
        acc = jnp.zeros_like(m)
        for ev, o in zip(e, (o1, o2, o3)):
            acc = acc + (ev / den) * o[...].astype(F32)
        o_ref[...] = acc.astype(BF16)
        l_ref[...] = m + jnp.log(den)

    blk = pl.BlockSpec((T, GW), _row)
    return pl.pallas_call(
        body, name=name, grid=(S // T,), in_specs=[blk] * 6, out_specs=[blk, blk],
        out_shape=[jax.ShapeDtypeStruct((S, GW), BF16), jax.ShapeDtypeStruct((S, GW), F32)],
        compiler_params=_params(("parallel",)),
    )(*os3, *ls3)


def _att_prep(do, o, lse, name):
    S = do.shape[0]
    T = 1024

    def body(do_ref, o_ref, l_ref, out_ref):
        prod = do_ref[...].astype(F32) * o_ref[...].astype(F32)
        dd = [jnp.broadcast_to(jnp.sum(prod[:, h * HD:(h + 1) * HD], axis=-1, keepdims=True), (T, HD))
              for h in range(HPG)]
        lane = lax.broadcasted_iota(I32, (T, GW), 1)
        out_ref[...] = jnp.where(lane % HD < HD // 2, l_ref[...], jnp.concatenate(dd, axis=-1))

    blk = pl.BlockSpec((T, GW), _row)
    return pl.pallas_call(
        body, name=name, grid=(S // T,), in_specs=[blk] * 3, out_specs=blk,
        out_shape=jax.ShapeDtypeStruct((S, GW), F32), compiler_params=_params(("parallel",)),
    )(do, o, lse)


def _att_bwd(z, rel_bias, do, ld, dz, g, name):
    S = z.shape[0]
    d = DILS[g]
    TB, QB = ATT_TB, ATT_QB
    H = RADIUS * d
    L = S // d
    E = TB + 2 * H
    cq = (Z_ATT + 3 * GW * g) // GW
    ck, cv = cq + 1, cq + 2
    bk_a = _t5_bucket(_offsets_qk(QB, 2 * QB, -RADIUS) * d)
    bk_b = _t5_bucket(-_offsets_qk(QB, 2 * QB, -RADIUS) * d)

    def body(rb_ref, bka_ref, bkb_ref, *refs):
        ins, (dz_ref, db_ref) = refs[:15], refs[16:18]
        qs, ks, vs, dos, ls, dqs, dks, dvs, bias_a, bias_b, dbias = refs[18:]
        i = pl.program_id(0)
        n = pl.num_programs(0)

        @pl.when(i == 0)
        def _():
            off = _offsets_qk(QB, 2 * QB, -RADIUS)
            for h in range(HPG):
                bias_a[h] = _bias_table(bka_ref[...], rb_ref, g * HPG + h, off)
                bias_b[h] = _bias_table(bkb_ref[...], rb_ref, g * HPG + h, off)
            dbias[...] = jnp.zeros_like(dbias)

        for a, scr in enumerate((qs, ks, vs, dos, ls)):
            c_ref, p_ref, n_ref = ins[3 * a:3 * a + 3]
            pre = (lambda v: v.astype(F32) * ATT_SCALE) if a == 0 else (lambda v: v)
            _to_halves(scr, 0, pre(p_ref[...]))
            _to_halves(scr, H, pre(c_ref[...]))
            _to_halves(scr, H + TB, pre(n_ref[...]))

        lo = lax.broadcasted_iota(I32, (QB, 128), 1) < HD

        def split(v):
            return jnp.concatenate([jnp.where(lo, v, 0.0), jnp.where(lo, 0.0, v)], axis=0).astype(BF16)

        def halves(v):
            return v[:QB], v[QB:]

        def units(rjs):
            work = []
            for r, j in rjs:
                row = _unit_row(r, j, d)
                cur = row + H
                m0 = i * (TB // d) + j * QB - RADIUS
                km = lax.broadcasted_iota(I32, (1, 2 * QB), 1) + m0
                edge_a = jnp.where(jnp.where(km >= 0, km, L) < L, 0.0, NEG_INF)
                for hf in (0, 1):
                    ld = lambda scr, at, nrow: scr[hf, pl.ds(at, nrow, stride=d), :]
                    w = dict(row=row, hf=hf, edge=edge_a, l_c=ld(ls, cur, QB), l_t=ld(ls, row, 2 * QB).T)
                    for nm, scr in (("q", qs), ("k", ks), ("v", vs), ("do", dos)):
                        w[nm + "_c"] = split(ld(scr, cur, QB))
                        w[nm + "_e"] = ld(scr, row, 2 * QB).astype(BF16)
                    work.append(w)
            for w in work:
                w["s"] = halves(_dot_nt(w["q_c"], w["k_e"]))
                w["dp"] = halves(_dot_nt(w["do_c"], w["v_e"]))
                w["s2"] = halves(_dot_nt(w["k_c"], w["q_e"]))
                w["dp2"] = halves(_dot_nt(w["v_c"], w["do_e"]))
            for w in work:
                w["ds"], w["p2"], w["ds2"] = [], [], []
                for hh in (0, 1):
                    h, c0 = 2 * w["hf"] + hh, HD * hh
                    l_c, l_t = w["l_c"], w["l_t"]
                    p = jnp.exp(w["s"][hh] + bias_a[h] + w["edge"] - l_c[:, c0:c0 + 1])
                    ds = p * (w["dp"][hh] - l_c[:, c0 + HD // 2:c0 + HD // 2 + 1])
                    dbias[h] += ds
                    p2 = jnp.exp(w["s2"][hh] + bias_b[h] + w["edge"] - l_t[c0:c0 + 1, :])
                    ds2 = p2 * (w["dp2"][hh] - l_t[c0 + HD // 2:c0 + HD // 2 + 1, :])
                    w["ds"].append(ds.astype(BF16))
                    w["p2"].append(p2.astype(BF16))
                    w["ds2"].append(ds2.astype(BF16))
            for w in work:
                at = pl.ds(w["row"], QB, stride=d)
                both = lambda pair, rhs: halves(_dot(jnp.concatenate(pair, axis=0), rhs))
                dq = both(w["ds"], w["k_e"])
                dqs[w["hf"], at, :] = jnp.where(lo, dq[0], dq[1]) * ATT_SCALE
                dv = both(w["p2"], w["do_e"])
                dvs[w["hf"], at, :] = jnp.where(lo, dv[0], dv[1])
                dk = both(w["ds2"], w["q_e"])
                dks[w["hf"], at, :] = jnp.where(lo, dk[0], dk[1])

        _att_units(d, units, ATT_BWD_GROUP)
        for a, scr in enumerate((dqs, dks, dvs)):
            for hf in (0, 1):
                dz_ref[:, a * GW + hf * 128:a * GW + (hf + 1) * 128] = scr[hf].astype(BF16)

        @pl.when(i == n - 1)
        def _():
            rows = lax.broadcasted_iota(I32, (NUM_BUCKETS, 128), 0)
            lanes = lax.broadcasted_iota(I32, (NUM_BUCKETS, 128), 1)
            out = jnp.zeros((NUM_BUCKETS, 128), F32)
            bk = bka_ref[...]
            for h in range(HPG):
                acc = dbias[h]
                for b in range(NUM_BUCKETS):
                    tot = jnp.sum(jnp.sum(jnp.where(bk == b, acc, 0.0), axis=1, keepdims=True), axis=0, keepdims=True)
                    out = out + jnp.where((rows == b) & (lanes == h), tot, 0.0)
            db_ref[...] = out

    def halo3(col, width=GW):
        return _halo_specs(TB, H, S, width, col)

    one = pl.Buffered(1)

    def single(specs):
        return [pl.BlockSpec(s.block_shape, s.index_map, pipeline_mode=one) for s in specs]

    in_specs = ([pl.BlockSpec(memory_space=pltpu.SMEM), pl.BlockSpec((QB, 2 * QB), _fixed),
                 pl.BlockSpec((QB, 2 * QB), _fixed)]
                + single(halo3(cq) + halo3(ck) + halo3(cv) + halo3(0) + halo3(0))
                + [pl.BlockSpec(memory_space=pl.ANY)])
    return pl.pallas_call(
        body, name=name, grid=(S // TB,), in_specs=in_specs,
        out_specs=[pl.BlockSpec((TB, 3 * GW), lambda i: (i, cq // 3)), pl.BlockSpec((NUM_BUCKETS, 128), _fixed)],
        out_shape=[jax.ShapeDtypeStruct(dz.shape, BF16), jax.ShapeDtypeStruct((NUM_BUCKETS, 128), F32)],
        input_output_aliases={18: 0},
        scratch_shapes=[pltpu.VMEM((2, E, 128), F32)] * 5 + [pltpu.VMEM((2, TB, 128), F32)] * 3
        + [pltpu.VMEM((HPG, QB, 2 * QB), F32)] * 3,
        compiler_params=_params(("arbitrary",)),
    )(rel_bias, bk_a, bk_b, z, z, z, z, z, z, z, z, z, do, do, do, ld, ld, ld, dz)


def _memkv_fwd(mem, gm, wkv, name):
    def body(m_ref, g_ref, w_ref, hm_ref, kv_ref):
        hm = _rms_fwd_val(m_ref[...], g_ref[...]).astype(BF16)
        hm_ref[...] = hm
        kv_ref[...] = _dot(hm, w_ref[...]).astype(BF16)

    return pl.pallas_call(
        body, name=name,
        out_shape=[jax.ShapeDtypeStruct((N_MEM, D), BF16), jax.ShapeDtypeStruct((N_MEM, 2 * MW), BF16)],
        compiler_params=_params(),
    )(mem, gm, wkv)


def _memkv_bwd(mem, gm, hm, wkv, dkv, name):
    def body(m_ref, g_ref, hm_ref, w_ref, dkv_ref, dw_ref, dg_ref):
        dkv_b = dkv_ref[...].astype(BF16)
        dw_ref[...] = _dot_tn(hm_ref[...], dkv_b)
        dhm = _dot_nt(dkv_b, w_ref[...])
        _, dgr = _rms_bwd_val(m_ref[...], g_ref[...], dhm)
        dg_ref[...] = jnp.sum(dgr, axis=0, keepdims=True)

    return pl.pallas_call(
        body, name=name,
        out_shape=[jax.ShapeDtypeStruct((D, 2 * MW), F32), jax.ShapeDtypeStruct((1, D), F32)],
        compiler_params=_params(),
    )(mem, gm, hm, wkv, dkv)


MEM_T = 512


def _mem_q_spec():
    return pl.BlockSpec((MEM_T, MW), lambda i: (i, Z_MEM // MW))


def _memattn_fwd(z, kv, name):
    S = z.shape[0]
    T = MEM_T

    def body(q_ref, kv_ref, o_ref):
        for h in range(MH):
            kh = kv_ref[:, h * MHD:(h + 1) * MHD]
            vh = kv_ref[:, MW + h * MHD:MW + (h + 1) * MHD]
            s = _dot_nt(q_ref[:, h * MHD:(h + 1) * MHD], kh) * MEM_SCALE
            e = jnp.exp(s - jnp.max(s, axis=-1, keepdims=True))
            p = e / jnp.sum(e, axis=-1, keepdims=True)
            o_ref[:, h * MHD:(h + 1) * MHD] = _dot(p.astype(BF16), vh).astype(BF16)

    return pl.pallas_call(
        body, name=name, grid=(S // T,),
        in_specs=[_mem_q_spec(), pl.BlockSpec((N_MEM, 2 * MW), _fixed)],
        out_specs=pl.BlockSpec((T, MW), _row),
        out_shape=jax.ShapeDtypeStruct((S, MW), BF16),
        compiler_params=_params(("parallel",)),
    )(z, kv)


def _memattn_bwd(z, kv, dom, dz, name):
    S = z.shape[0]
    T = MEM_T

    def body(q_ref, kv_ref, do_ref, dz_in, dq_ref, dkv_ref):
        @pl.when(pl.program_id(0) == 0)
        def _():
            dkv_ref[...] = jnp.zeros_like(dkv_ref)

        for h in range(MH):
            kh = kv_ref[:, h * MHD:(h + 1) * MHD]
            vh = kv_ref[:, MW + h * MHD:MW + (h + 1) * MHD]
            qh = q_ref[:, h * MHD:(h + 1) * MHD]
            doh = do_ref[:, h * MHD:(h + 1) * MHD]
            s = _dot_nt(qh, kh) * MEM_SCALE
            e = jnp.exp(s - jnp.max(s, axis=-1, keepdims=True))
            p = e / jnp.sum(e, axis=-1, keepdims=True)
            dkv_ref[:, MW + h * MHD:MW + (h + 1) * MHD] += _dot_tn(p.astype(BF16), doh)
            dp = _dot_nt(doh, vh)
            ds = (p * (dp - jnp.sum(dp * p, axis=-1, keepdims=True))).astype(BF16)
            dq_ref[:, h * MHD:(h + 1) * MHD] = (_dot(ds, kh) * MEM_SCALE).astype(BF16)
            dkv_ref[:, h * MHD:(h + 1) * MHD] += _dot_tn(ds, qh) * MEM_SCALE

    return pl.pallas_call(
        body, name=name, grid=(S // T,),
        in_specs=[_mem_q_spec(), pl.BlockSpec((N_MEM, 2 * MW), _fixed), pl.BlockSpec((T, MW), _row),
                  pl.BlockSpec(memory_space=pl.ANY)],
        out_specs=[_mem_q_spec(), pl.BlockSpec((N_MEM, 2 * MW), _fixed)],
        out_shape=[jax.ShapeDtypeStruct(dz.shape, BF16), jax.ShapeDtypeStruct((N_MEM, 2 * MW), F32)],
        input_output_aliases={3: 0},
        compiler_params=_params(("arbitrary",)),
    )(z, kv, dom, dz)


MERGE_T = 512


def _gate_spec(T):
    return pl.BlockSpec((T, 3 * D), lambda i: (i, Z_GATE // (3 * D)))


def _branches(ca_ref, oa_ref, om_ref, wco_ref, wao_ref, wmo_ref, zg_ref, bg_ref):
    ys = [_dot(ca_ref[...], wco_ref[...]), _dot(oa_ref[...], wao_ref[...]), _dot(om_ref[...], wmo_ref[...])]
    gs = [_sigmoid(zg_ref[:, b * D:(b + 1) * D].astype(F32) + bg_ref[:, b * D:(b + 1) * D]) for b in range(3)]
    return ys, gs


def _merge_fwd(x, cact, oatt, om, z, wco, wao, wmo, wout, bgate, gpost, gnext, name):
    S = x.shape[0]
    T = MERGE_T

    def body(x_ref, ca_ref, oa_ref, om_ref, zg_ref, wco_ref, wao_ref, wmo_ref, wout_ref, bg_ref, gp_ref, gn_ref,
             x1_ref, mg_ref, t_ref, h_ref):
        ys, gs = _branches(ca_ref, oa_ref, om_ref, wco_ref, wao_ref, wmo_ref, zg_ref, bg_ref)
        mb = (gs[0] * ys[0] + gs[1] * ys[1] + gs[2] * ys[2]).astype(BF16)
        t = _dot(mb, wout_ref[...])
        mg_ref[...] = mb
        t_ref[...] = t
        x1 = x_ref[...] + _rms_fwd_val(t, gp_ref[...])
        x1_ref[...] = x1
        h_ref[...] = _rms_fwd_val(x1, gn_ref[...]).astype(BF16)

    full = lambda a: pl.BlockSpec(a.shape, _fixed)
    return pl.pallas_call(
        body, name=name, grid=(S // T,),
        in_specs=[pl.BlockSpec((T, D), _row), pl.BlockSpec((T, CW), _row), pl.BlockSpec((T, GW), _row),
                  pl.BlockSpec((T, MW), _row), _gate_spec(T)]
        + [full(wco), full(wao), full(wmo), full(wout), full(bgate), full(gpost), full(gnext)],
        out_specs=[pl.BlockSpec((T, D), _row)] * 4,
        out_shape=[jax.ShapeDtypeStruct((S, D), F32), jax.ShapeDtypeStruct((S, D), BF16),
                   jax.ShapeDtypeStruct((S, D), F32), jax.ShapeDtypeStruct((S, D), BF16)],
        compiler_params=_params(("parallel",)),
    )(x, cact, oatt, om, z, wco, wao, wmo, wout, bgate, gpost, gnext)


def _merge_bwd(dx1, t, mg, cact, oatt, om, z, wco, wao, wmo, wout, bgate, gpost, name):
    S = dx1.shape[0]
    T = MERGE_T

    def body(dx_ref, t_ref, mg_ref, ca_ref, oa_ref, om_ref, zg_ref, wco_ref, wao_ref, wmo_ref, wout_ref,
             bg_ref, gp_ref, dzg_ref, dca_ref, doa_ref, dom_ref, dwco_ref, dwao_ref, dwmo_ref, dwout_ref,
             dbg_ref, dgp_ref):
        accs = (dwco_ref, dwao_ref, dwmo_ref, dwout_ref, dbg_ref, dgp_ref)

        @pl.when(pl.program_id(0) == 0)
        def _():
            for a in accs:
                a[...] = jnp.zeros_like(a)

        dt, dgr = _rms_bwd_val(t_ref[...], gp_ref[...], dx_ref[...])
        dgp_ref[...] += jnp.sum(dgr, axis=0, keepdims=True)
        dtb = dt.astype(BF16)
        dwout_ref[...] += _dot_tn(mg_ref[...], dtb)
        dm = _dot_nt(dtb, wout_ref[...])
        ys, gs = _branches(ca_ref, oa_ref, om_ref, wco_ref, wao_ref, wmo_ref, zg_ref, bg_ref)
        for b, (act_ref, w_ref, dw_ref, da_ref) in enumerate(
                ((ca_ref, wco_ref, dwco_ref, dca_ref), (oa_ref, wao_ref, dwao_ref, doa_ref),
                 (om_ref, wmo_ref, dwmo_ref, dom_ref))):
            dzg = dm * ys[b] * gs[b] * (1.0 - gs[b])
            dzg_ref[:, b * D:(b + 1) * D] = dzg.astype(BF16)
            dbg_ref[:, b * D:(b + 1) * D] += jnp.sum(dzg, axis=0, keepdims=True)
            dy = (dm * gs[b]).astype(BF16)
            dw_ref[...] += _dot_tn(act_ref[...], dy)
            da_ref[...] = _dot_nt(dy, w_ref[...]).astype(BF16)

    full = lambda a: pl.BlockSpec(a.shape, _fixed)
    fullf = lambda a: jax.ShapeDtypeStruct(a.shape, F32)
    return pl.pallas_call(
        body, name=name, grid=(S // T,),
        in_specs=[pl.BlockSpec((T, D), _row), pl.BlockSpec((T, D), _row), pl.BlockSpec((T, D), _row),
                  pl.BlockSpec((T, CW), _row), pl.BlockSpec((T, GW), _row), pl.BlockSpec((T, MW), _row)]
        + [_gate_spec(T), full(wco), full(wao), full(wmo), full(wout), full(bgate), full(gpost)],
        out_specs=[_gate_spec(T), pl.BlockSpec((T, CW), _row), pl.BlockSpec((T, GW), _row),
                   pl.BlockSpec((T, MW), _row), full(wco), full(wao), full(wmo), full(wout), full(bgate), full(gpost)],
        out_shape=[jax.ShapeDtypeStruct((S, NIN), BF16), jax.ShapeDtypeStruct((S, CW), BF16),
                   jax.ShapeDtypeStruct((S, GW), BF16), jax.ShapeDtypeStruct((S, MW), BF16),
                   fullf(wco), fullf(wao), fullf(wmo), fullf(wout), fullf(bgate), fullf(gpost)],
        compiler_params=_params(("arbitrary",)),
    )(dx1, t, mg, cact, oatt, om, z, wco, wao, wmo, wout, bgate, gpost)


FFN_T = 256


def _ffn_fwd(x1, gu, wfo, gpost, gnext, name):
    S = x1.shape[0]
    T = FFN_T

    nxt = gnext is not None

    def body(x_ref, gu_ref, w_ref, gp_ref, *rest):
        x2_ref, f_ref = rest[nxt:nxt + 2]
        gv = gu_ref[:, :FH].astype(F32)
        uv = gu_ref[:, FH:].astype(F32)
        act = (gv * _sigmoid(gv) * uv).astype(BF16)
        f = _dot(act, w_ref[...])
        f_ref[...] = f
        x2 = x_ref[...] + _rms_fwd_val(f, gp_ref[...])
        x2_ref[...] = x2
        if nxt:
            rest[3][...] = _rms_fwd_val(x2, rest[0][...]).astype(BF16)

    return pl.pallas_call(
        body, name=name, grid=(S // T,),
        in_specs=[pl.BlockSpec((T, D), _row), pl.BlockSpec((T, 2 * FH), _row), pl.BlockSpec((FH, D), _fixed),
                  pl.BlockSpec((1, D), _fixed)] + [pl.BlockSpec((1, D), _fixed)] * nxt,
        out_specs=[pl.BlockSpec((T, D), _row)] * (2 + nxt),
        out_shape=[jax.ShapeDtypeStruct((S, D), F32)] * 2 + [jax.ShapeDtypeStruct((S, D), BF16)] * nxt,
        compiler_params=_params(("parallel",)),
    )(x1, gu, wfo, gpost, *([gnext] if nxt else []))


def _ffn_bwd(dx2, f, gu, wfo, gpost, name):
    S = dx2.shape[0]
    T = FFN_T

    def body(dx_ref, f_ref, gu_ref, w_ref, gp_ref, dgu_ref, df_ref, act_ref, dgp_ref):
        @pl.when(pl.program_id(0) == 0)
        def _():
            dgp_ref[...] = jnp.zeros_like(dgp_ref)

        df, dgr = _rms_bwd_val(f_ref[...], gp_ref[...], dx_ref[...])
        dgp_ref[...] += jnp.sum(dgr, axis=0, keepdims=True)
        dfb = df.astype(BF16)
        df_ref[...] = dfb
        dact = _dot_nt(dfb, w_ref[...])
        gv = gu_ref[:, :FH].astype(F32)
        uv = gu_ref[:, FH:].astype(F32)
        sg = _sigmoid(gv)
        silu = gv * sg
        act_ref[...] = (silu * uv).astype(BF16)
        dgu_ref[:, :FH] = (dact * uv * (sg * (1.0 + gv * (1.0 - sg)))).astype(BF16)
        dgu_ref[:, FH:] = (dact * silu).astype(BF16)

    return pl.pallas_call(
        body, name=name, grid=(S // T,),
        in_specs=[pl.BlockSpec((T, D), _row), pl.BlockSpec((T, D), _row), pl.BlockSpec((T, 2 * FH), _row),
                  pl.BlockSpec((FH, D), _fixed), pl.BlockSpec((1, D), _fixed)],
        out_specs=[pl.BlockSpec((T, 2 * FH), _row), pl.BlockSpec((T, D), _row), pl.BlockSpec((T, FH), _row),
                   pl.BlockSpec((1, D), _fixed)],
        out_shape=[jax.ShapeDtypeStruct((S, 2 * FH), BF16), jax.ShapeDtypeStruct((S, D), BF16),
                   jax.ShapeDtypeStruct((S, FH), BF16), jax.ShapeDtypeStruct((1, D), F32)],
        compiler_params=_params(("arbitrary",)),
    )(dx2, f, gu, wfo, gpost)


def _loss_head(y, target, name):
    S = y.shape[0]
    T = 512

    def body(y_ref, t_ref, dy_ref, l_ref):
        @pl.when(pl.program_id(0) == 0)
        def _():
            l_ref[...] = jnp.zeros_like(l_ref)

        e = y_ref[...] - t_ref[...]
        dy_ref[...] = e * (1.0 / D)
        l_ref[...] += (0.5 / D) * jnp.sum(jnp.sum(e * e, axis=1, keepdims=True), axis=0, keepdims=True)

    return pl.pallas_call(
        body, name=name, grid=(S // T,),
        in_specs=[pl.BlockSpec((T, D), _row)] * 2,
        out_specs=[pl.BlockSpec((T, D), _row), pl.BlockSpec((8, 128), _fixed)],
        out_shape=[jax.ShapeDtypeStruct((S, D), F32), jax.ShapeDtypeStruct((8, 128), F32)],
        compiler_params=_params(("arbitrary",)),
    )(y, target)


BIG = ("w_in", "w_conv_out", "w_att_out", "w_mem_kv", "w_mem_out", "w_out", "w_ffn_in", "w_ffn_out")
SMALL = ("rel_bias", "norm_mix_pre", "b_gate", "conv_dw_bias", "conv_ln_g", "conv_ln_b", "norm_mem",
         "norm_mix_post", "norm_ffn_pre", "norm_ffn_post")


def _layer_fwd(l, x, h, mem, w, rel_bias, gnext):
    tag = f"_l{l}"
    z = _mm_nn(h, w["w_in"], 512, BF16, "mm_in" + tag)
    yc, cact = _conv_fwd(z, w["conv_dw"], w["conv_dw_bias"], w["conv_ln_g"], w["conv_ln_b"], "conv_fwd" + tag)
    og, lg = zip(*[_att_fwd(z, rel_bias, g, f"att_fwd_g{g}" + tag) for g in range(3)])
    oatt, lse = _att_combine(og, lg, "att_combine" + tag)
    hm, kv = _memkv_fwd(mem, w["norm_mem"], w["w_mem_kv"], "memkv_fwd" + tag)
    om = _memattn_fwd(z, kv, "memattn_fwd" + tag)
    x1, mg, t, h2 = _merge_fwd(x, cact, oatt, om, z, w["w_conv_out"], w["w_att_out"], w["w_mem_out"], w["w_out"],
                               w["b_gate"], w["norm_mix_post"], w["norm_ffn_pre"], "merge_fwd" + tag)
    gu = _mm_nn(h2, w["w_ffn_in"], 512, BF16, "mm_ffn_in" + tag)
    x2, f, *hn = _ffn_fwd(x1, gu, w["w_ffn_out"], w["norm_ffn_post"], gnext, "ffn_fwd" + tag)
    saved = dict(x=x, h=h, z=z, yc=yc, cact=cact, oatt=oatt, lse=lse, hm=hm, kv=kv, om=om, x1=x1, mg=mg, t=t,
                 h2=h2, gu=gu, f=f)
    return x2, (hn[0] if hn else None), saved


def _layer_bwd(l, dx2, mem, w, rel_bias, s):
    tag = f"_l{l}"
    gr = {}
    dgu, df, act, gr["norm_ffn_post"] = _ffn_bwd(dx2, s["f"], s["gu"], w["w_ffn_out"], w["norm_ffn_post"], "ffn_bwd" + tag)
    gr["w_ffn_out"] = _mm_tn(act, df, 1024, 512, "dw_ffn_out" + tag)
    gr["w_ffn_in"] = _mm_tn(s["h2"], dgu, 2048, 1408, "dw_ffn_in" + tag)
    dx1, gr["norm_ffn_pre"] = _mm_nt_rms_bwd(dgu, w["w_ffn_in"], s["x1"], w["norm_ffn_pre"], dx2, 512, "dh_ffn" + tag)
    (dz, dcact, doatt, dom, gr["w_conv_out"], gr["w_att_out"], gr["w_mem_out"], gr["w_out"], gr["b_gate"],
     gr["norm_mix_post"]) = _merge_bwd(dx1, s["t"], s["mg"], s["cact"], s["oatt"], s["om"], s["z"], w["w_conv_out"],
                                       w["w_att_out"], w["w_mem_out"], w["w_out"], w["b_gate"], w["norm_mix_post"],
                                       "merge_bwd" + tag)
    dyc, gr["conv_ln_g"], gr["conv_ln_b"], gr["conv_dw_bias"] = _conv_bwd_ln(
        s["yc"], dcact, w["conv_ln_g"], w["conv_ln_b"], "conv_bwd_ln" + tag)
    dz, dwdw = _conv_bwd_dw(s["z"], dyc, w["conv_dw"], dz, "conv_bwd_dw" + tag)
    gr["conv_dw"] = dwdw[:KSIZE]
    ld = _att_prep(doatt, s["oatt"], s["lse"], "att_prep" + tag)
    drb = []
    for g in range(3):
        dz, db = _att_bwd(s["z"], rel_bias, doatt, ld, dz, g, f"att_bwd_g{g}" + tag)
        drb.append(db)
    dz, dkv = _memattn_bwd(s["z"], s["kv"], dom, dz, "memattn_bwd" + tag)
    gr["w_mem_kv"], gr["norm_mem"] = _memkv_bwd(mem, w["norm_mem"], s["hm"], w["w_mem_kv"], dkv, "memkv_bwd" + tag)
    gr["w_in"] = _mm_tn(s["h"], dz, 2048, 1152, "dw_in" + tag)
    dx, gr["norm_mix_pre"] = _mm_nt_rms_bwd(dz, w["w_in"], s["x"], w["norm_mix_pre"], dx1, 512, "dh_in" + tag)
    return dx, gr, drb


def _rel_bias_total(parts, name):
    def body(*refs):
        out_ref = refs[-1]
        acc = jnp.zeros((NUM_BUCKETS, 128), F32)
        for l in range(DEPTH):
            for g in range(3):
                v = refs[l * 3 + g][...]
                acc = acc + (v if g == 0 else pltpu.roll(v, HPG * g, axis=1))
        out_ref[...] = acc

    return pl.pallas_call(body, name=name, out_shape=jax.ShapeDtypeStruct((NUM_BUCKETS, 128), F32),
                          compiler_params=_params())(*[p for layer in parts for p in layer])


def _local_step(x, mem, target, rel_bias, layer_fns, gmix):
    saved, layers = [], []
    h = _rms_h(x, gmix[0], "rms_mix_l0")
    for l in range(DEPTH):
        layers.append(layer_fns[l](x))
        x, h, s = _layer_fwd(l, x, h, mem, layers[l], rel_bias, gmix[l + 1] if l + 1 < DEPTH else None)
        saved.append(s)
    dy, lpart = _loss_head(x, target, "loss_head")
    grads = [None] * DEPTH
    drb = [None] * DEPTH
    for l in reversed(range(DEPTH)):
        dy, grads[l], drb[l] = _layer_bwd(l, dy, mem, layers[l], rel_bias, saved[l])
    return lpart[0, 0], dy, grads, _rel_bias_total(drb, "rel_bias_total")


def _z_cols_from_ref(w):
    att = [w[..., R_ATT + (3 * j + g) * GW:R_ATT + (3 * j + g + 1) * GW] for g in range(3) for j in range(3)]
    return jnp.concatenate([w[..., R_GATE:], w[..., :C1], w[..., R_MEM:R_GATE]] + att, axis=-1)


def _ref_cols_from_z(w):
    att = [w[..., Z_ATT + (3 * g + j) * GW:Z_ATT + (3 * g + j + 1) * GW] for j in range(3) for g in range(3)]
    return jnp.concatenate([w[..., Z_CONV:Z_MEM]] + att + [w[..., Z_MEM:Z_ATT], w[..., Z_GATE:Z_CONV]], axis=-1)


N_CHIPS = 4
SHARD = {"w_in": ((D, NIN // 4), 1), "w_conv_out": ((CW, D // 4), 1), "w_att_out": ((GW, D // 4), 1),
         "w_mem_kv": ((D // 4, 2 * MW), 0), "w_mem_out": ((MW, D // 4), 1), "w_out": ((D // 4, D), 0),
         "w_ffn_in": ((D, 2 * FH // 4), 1), "w_ffn_out": ((FH // 4, D), 0)}
CDW_ROWS = 64
VEC_ROWS = (("norm_mix_pre", 1), ("b_gate", 3), ("conv_dw_bias", 1), ("conv_ln_g", 1), ("conv_ln_b", 1),
            ("norm_mem", 1), ("norm_mix_post", 1), ("norm_ffn_pre", 1), ("norm_ffn_post", 1))
VEC_LROWS = sum(r for _, r in VEC_ROWS)
REL_ROW = DEPTH * VEC_LROWS
CDW_ROW = REL_ROW + 1
CDW_GROWS = DEPTH * KSIZE * CW // D
SMALL_ROWS = -(-(CDW_ROW + CDW_GROWS) // 8) * 8


def _mesh_pos():
    return lax.axis_index("x"), lax.axis_index("y"), lax.axis_index("c")


def _other_chips(x, y):
    chips = [(1 - x, y), (x, 1 - y), (1 - x, 1 - y)]
    return chips, [2 * cx + cy for cx, cy in chips]


NBIG = len(BIG)
ANY_SPEC = pl.BlockSpec(memory_space=pl.ANY)


def _remote(src, dst, send_sems, recv_sems, k, to):
    return pltpu.make_async_remote_copy(src_ref=src, dst_ref=dst, send_sem=send_sems.at[k], recv_sem=recv_sems.at[k],
                                        device_id=to, device_id_type=MESH)


def _half(ref, c):
    h = ref.shape[0] // 2
    return ref.at[pl.ds(c * h if isinstance(c, int) else pl.multiple_of(c * h, 16), h)]


def _all_gather(ws, cdw):
    def body(*refs):
        w_refs, cdw_ref = refs[:NBIG], refs[NBIG]
        g_refs, gc_ref = refs[NBIG + 1:2 * NBIG + 1], refs[2 * NBIG + 1]
        send_sems, recv_sems = refs[2 * NBIG + 2:]
        x, y, c = _mesh_pos()
        j = 2 * x + y
        sibling = (x, y, 1 - c)
        chips, blocks = _other_chips(x, y)
        copy = functools.partial(_remote, send_sems=send_sems, recv_sems=recv_sems)
        pairs = list(zip(w_refs, g_refs))
        first = [copy(_half(w, c), _half(g.at[j], c), k=k * NBIG + n, to=(*chip, c))
                 for k, chip in enumerate(chips) for n, (w, g) in enumerate(pairs)]
        first += [copy(cdw_ref, gc_ref.at[j], k=6 * NBIG + k, to=(*chip, c)) for k, chip in enumerate(chips)]
        for cp in first:
            cp.start()
        passed = []
        for k, b in enumerate(blocks):
            for n, (w, g) in enumerate(pairs):
                copy(_half(w, c), _half(g.at[b], c), k=k * NBIG + n, to=sibling).wait_recv()
            onward = [copy(_half(g.at[b], c), _half(g.at[b], c), k=(3 + k) * NBIG + n, to=sibling)
                      for n, (w, g) in enumerate(pairs)]
            for cp in onward:
                cp.start()
            passed += onward
        for k, b in enumerate(blocks):
            for n, (w, g) in enumerate(pairs):
                copy(_half(w, c), _half(g.at[b], 1 - c), k=(3 + k) * NBIG + n, to=sibling).wait_recv()
            copy(cdw_ref, gc_ref.at[b], k=6 * NBIG + k, to=sibling).wait_recv()
        for cp in first + passed:
            cp.wait_send()

    nsem = 6 * NBIG + 3
    return pl.pallas_call(
        body, name="all_gather_weights",
        out_shape=[jax.ShapeDtypeStruct((N_CHIPS,) + w.shape, BF16) for w in ws]
        + [jax.ShapeDtypeStruct((N_CHIPS, CDW_ROWS, 128), F32)],
        in_specs=[ANY_SPEC] * (NBIG + 1), out_specs=[ANY_SPEC] * (NBIG + 1),
        scratch_shapes=[pltpu.SemaphoreType.DMA((nsem,)), pltpu.SemaphoreType.DMA((nsem,))],
    )(*ws, cdw)


SEM_SPEC = pl.BlockSpec(memory_space=pltpu.SEMAPHORE)
DATAFLOW = pltpu.SideEffectType.DATAFLOW_SIDE_EFFECTING


def _gather_copies(w_refs, g_refs, send_sem, recv_sem):
    x, y, c = _mesh_pos()
    j = 2 * x + y
    chips, _ = _other_chips(x, y)
    return [pltpu.make_async_remote_copy(src_ref=_half(w, c), dst_ref=_half(g.at[j], c), send_sem=send_sem,
                                         recv_sem=recv_sem, device_id=(*chip, cc), device_id_type=MESH)
            for chip in chips for cc in (0, 1) for w, g in zip(w_refs, g_refs)]


def _all_gather_start(ws, after):
    def body(*refs):
        w_refs, g_refs = refs[:NBIG], refs[NBIG:2 * NBIG]
        send_sem, recv_sem = refs[2 * NBIG + 1:2 * NBIG + 3]
        token = refs[-1]
        for cp in _gather_copies(w_refs, g_refs, send_sem, recv_sem):
            cp.start()
        token[...] = jnp.zeros_like(token)

    lands = [pltpu.with_memory_space_constraint(lax.empty((N_CHIPS,) + w.shape, BF16), pltpu.HBM) for w in ws]
    ws = [pltpu.with_memory_space_constraint(w, pltpu.HBM) for w in ws]
    hbm = pl.BlockSpec(memory_space=pltpu.HBM)
    out = pl.pallas_call(
        body, name="all_gather_start",
        out_shape=[pltpu.SemaphoreType.DMA(()), pltpu.SemaphoreType.DMA(())]
        + [pltpu.HBM(w.shape, BF16) for w in ws] + [pltpu.HBM(g.shape, BF16) for g in lands]
        + [jax.ShapeDtypeStruct((8, 128), F32)],
        in_specs=[hbm] * (2 * NBIG) + [ANY_SPEC],
        out_specs=[SEM_SPEC, SEM_SPEC] + [hbm] * (2 * NBIG) + [pl.BlockSpec(memory_space=pltpu.VMEM)],
        input_output_aliases={n: 2 + n for n in range(2 * NBIG)},
        compiler_params=pltpu.CompilerParams(has_side_effects=DATAFLOW),
    )(*ws, *lands, after)
    return out[0], out[1], out[2:2 + NBIG], out[2 + NBIG:2 + 2 * NBIG], out[-1]


def _all_gather_wait(send_sem, recv_sem, ws, lands, after):
    def body(*refs):
        w_refs, g_refs = refs[:NBIG], refs[NBIG:2 * NBIG]
        send_sem, recv_sem = refs[2 * NBIG:2 * NBIG + 2]
        x, y, c = _mesh_pos()
        _, blocks = _other_chips(x, y)
        for cp in _gather_copies(w_refs, g_refs, send_sem, recv_sem):
            cp.wait_send()
        for b in blocks:
            for cc in (0, 1):
                for w, g in zip(w_refs, g_refs):
                    pltpu.make_async_remote_copy(src_ref=_half(w, cc), dst_ref=_half(g.at[b], cc), send_sem=send_sem,
                                                 recv_sem=recv_sem, device_id=(x, y, c),
                                                 device_id_type=MESH).wait_recv()

    hbm = pl.BlockSpec(memory_space=pltpu.HBM)
    out = pl.pallas_call(
        body, name="all_gather_wait",
        out_shape=[pltpu.HBM(w.shape, BF16) for w in ws] + [pltpu.HBM(g.shape, BF16) for g in lands],
        in_specs=[hbm] * (2 * NBIG) + [SEM_SPEC, SEM_SPEC, ANY_SPEC],
        out_specs=[hbm] * (2 * NBIG),
        input_output_aliases={n: n for n in range(2 * NBIG)},
        compiler_params=pltpu.CompilerParams(has_side_effects=DATAFLOW),
    )(*ws, *lands, send_sem, recv_sem, after)
    return out[:NBIG], out[NBIG:]


def _sibling_exchange(ps):
    def body(*refs):
        p_refs, r_refs, (send_sems, recv_sems) = refs[:NBIG], refs[NBIG:2 * NBIG], refs[2 * NBIG:]
        x, y, c = _mesh_pos()
        cps = [_remote(p.at[1 - c], r, send_sems, recv_sems, n, (x, y, 1 - c))
               for n, (p, r) in enumerate(zip(p_refs, r_refs))]
        for cp in cps:
            cp.start()
        for cp in cps:
            cp.wait()

    return pl.pallas_call(
        body, name="grad_sibling_exchange", out_shape=[jax.ShapeDtypeStruct(p.shape[1:], p.dtype) for p in ps],
        in_specs=[ANY_SPEC] * NBIG, out_specs=[ANY_SPEC] * NBIG,
        scratch_shapes=[pltpu.SemaphoreType.DMA((NBIG,)), pltpu.SemaphoreType.DMA((NBIG,))],
    )(*ps)


SUM_BLOCK_BYTES = 2 * 1024 * 1024


def _sum_rows(s0, s1):
    return s0 if s0 * s1 * 2 <= SUM_BLOCK_BYTES else s0 // 2


def _add_own_layer(where, p, r, name):
    _, _, s0, s1 = p.shape
    T = _sum_rows(s0, s1)

    def body(where_ref, p_ref, r_ref, o_ref):
        o_ref[...] = (p_ref[0].astype(F32) + r_ref[...].astype(F32)).astype(BF16)

    return pl.pallas_call(
        body, name=name,
        grid_spec=pltpu.PrefetchScalarGridSpec(
            num_scalar_prefetch=1, grid=(N_CHIPS, s0 // T),
            in_specs=[pl.BlockSpec((1, 1, T, s1), lambda j, i, wh: (wh[0], j, i, 0)),
                      pl.BlockSpec((1, T, s1), lambda j, i, wh: (j, i, 0))],
            out_specs=pl.BlockSpec((1, T, s1), lambda j, i, wh: (j, i, 0))),
        out_shape=jax.ShapeDtypeStruct(r.shape, BF16), compiler_params=_params(("parallel", "parallel")),
    )(where, p, r)


def _chip_exchange(as_):
    def body(*refs):
        a_refs, r_refs, (send_sems, recv_sems) = refs[:NBIG], refs[NBIG:2 * NBIG], refs[2 * NBIG:]
        x, y, c = _mesh_pos()
        chips, blocks = _other_chips(x, y)
        cps = [_remote(a.at[b], r.at[k], send_sems, recv_sems, k * NBIG + n, (*chip, c))
               for k, (chip, b) in enumerate(zip(chips, blocks)) for n, (a, r) in enumerate(zip(a_refs, r_refs))]
        for cp in cps:
            cp.start()
        for cp in cps:
            cp.wait_recv()
        for cp in cps:
            cp.wait_send()

    return pl.pallas_call(
        body, name="grad_chip_exchange", out_shape=[jax.ShapeDtypeStruct((3,) + a.shape[1:], a.dtype) for a in as_],
        in_specs=[ANY_SPEC] * NBIG, out_specs=[ANY_SPEC] * NBIG,
        scratch_shapes=[pltpu.SemaphoreType.DMA((3 * NBIG,)), pltpu.SemaphoreType.DMA((3 * NBIG,))],
    )(*as_)


def _sum_chips(where, a, r, name):
    _, s0, s1 = a.shape
    T = _sum_rows(s0, s1)

    def body(where_ref, a_ref, r_ref, o_ref):
        acc = a_ref[0].astype(F32)
        for k in range(3):
            acc = acc + r_ref[k].astype(F32)
        o_ref[0] = acc

    return pl.pallas_call(
        body, name=name,
        grid_spec=pltpu.PrefetchScalarGridSpec(
            num_scalar_prefetch=1, grid=(s0 // T,),
            in_specs=[pl.BlockSpec((1, T, s1), lambda i, wh: (wh[1], i, 0)),
                      pl.BlockSpec((3, T, s1), lambda i, wh: (0, i, 0))],
            out_specs=pl.BlockSpec((1, T, s1), lambda i, wh: (wh[0], i, 0))),
        out_shape=jax.ShapeDtypeStruct((DEPTH, s0, s1), F32), compiler_params=_params(("parallel",)),
    )(where, a, r)


def _sibling_share(os_):
    def body(*refs):
        o_refs, (send_sems, recv_sems) = refs[NBIG:2 * NBIG], refs[2 * NBIG:]
        x, y, c = _mesh_pos()
        cps = [_remote(o.at[c], o.at[c], send_sems, recv_sems, n, (x, y, 1 - c)) for n, o in enumerate(o_refs)]
        for cp in cps:
            cp.start()
        for n, o in enumerate(o_refs):
            _remote(o.at[c], o.at[1 - c], send_sems, recv_sems, n, (x, y, 1 - c)).wait_recv()
        for cp in cps:
            cp.wait_send()

    return pl.pallas_call(
        body, name="grad_sibling_share", out_shape=[jax.ShapeDtypeStruct(o.shape, o.dtype) for o in os_],
        in_specs=[ANY_SPEC] * NBIG, out_specs=[ANY_SPEC] * NBIG,
        input_output_aliases={n: n for n in range(NBIG)},
        scratch_shapes=[pltpu.SemaphoreType.DMA((NBIG,)), pltpu.SemaphoreType.DMA((NBIG,))],
    )(*os_)


def _all_reduce_small(sp):
    def body(sp_ref, out_ref, buf, send_sems, recv_sems):
        x, y, c = _mesh_pos()
        me = 4 * x + 2 * y + c
        buf[0] = sp_ref[...]
        cps = []
        for k in range(1, 8):
            peer = (x ^ (k >> 2 & 1), y ^ (k >> 1 & 1), c ^ (k & 1))
            cps.append(pltpu.make_async_remote_copy(src_ref=sp_ref, dst_ref=buf.at[k], send_sem=send_sems.at[k - 1],
                                                    recv_sem=recv_sems.at[k - 1], device_id=peer, device_id_type=MESH))
        for cp in cps:
            cp.start()
        for cp in cps:
            cp.wait_recv()
        for cp in cps:
            cp.wait_send()
        acc = buf[me]
        for p in range(1, 8):
            acc = acc + buf[p ^ me]
        out_ref[...] = acc

    vm = pl.BlockSpec(memory_space=pltpu.VMEM)
    return pl.pallas_call(
        body, name="all_reduce_small", out_shape=jax.ShapeDtypeStruct(sp.shape, F32),
        in_specs=[vm], out_specs=vm,
        scratch_shapes=[pltpu.VMEM((8,) + sp.shape, F32), pltpu.SemaphoreType.DMA((7,)), pltpu.SemaphoreType.DMA((7,))],
        compiler_params=_params(),
    )(sp)


def _adamw(w, g, m, v, name):
    R, C = w.shape
    T = next((t for t in (256, 128) if R % t == 0), R)

    def body(w_ref, g_ref, m_ref, v_ref, d_ref, m2_ref, v2_ref):
        gv = g_ref[...]
        m2 = ADAM_B1 * m_ref[...] + (1.0 - ADAM_B1) * gv
        v2 = ADAM_B2 * v_ref[...] + (1.0 - ADAM_B2) * (gv * gv)
        m_hat = m2 / (1.0 - ADAM_B1 ** ADAM_STEP)
        v_hat = v2 / (1.0 - ADAM_B2 ** ADAM_STEP)
        d_ref[...] = -ADAM_LR * (m_hat / (jnp.sqrt(v_hat) + ADAM_EPS) + ADAM_WD * w_ref[...])
        m2_ref[...] = m2
        v2_ref[...] = v2

    blk = pl.BlockSpec((T, C), _row)
    return pl.pallas_call(
        body, name=name, grid=(R // T,), in_specs=[blk] * 4, out_specs=[blk] * 3,
        out_shape=[jax.ShapeDtypeStruct((R, C), F32)] * 3, compiler_params=_params(("parallel",)),
    )(w, g, m, v)


def _pack_vectors(get, rel, cdw, name):
    rows = []
    for l in range(DEPTH):
        for n, r in VEC_ROWS:
            v = get(n)[l]
            rows.append(jnp.pad(v, (0, r * D - v.shape[0])).reshape(r, D))
    rows.append(jnp.pad(rel.reshape(-1), (0, D - NUM_BUCKETS * 3 * HPG)).reshape(1, D))
    rows.append(cdw.reshape(CDW_GROWS, D))

    def body(*refs):
        out_ref = refs[-1]
        out_ref[...] = jnp.zeros_like(out_ref)
        at = 0
        for ref in refs[:-1]:
            out_ref[at:at + ref.shape[0], :] = ref[...]
            at += ref.shape[0]

    return pl.pallas_call(body, name=name, out_shape=jax.ShapeDtypeStruct((SMALL_ROWS, D), F32),
                          compiler_params=_params())(*rows)


def _unpack_vectors(packed, lens):
    out = {n: [] for n, _ in VEC_ROWS}
    for l in range(DEPTH):
        at = l * VEC_LROWS
        for n, r in VEC_ROWS:
            out[n].append(packed[at:at + r].reshape(-1)[:lens[n]])
            at += r
    rel = packed[REL_ROW, :NUM_BUCKETS * 3 * HPG].reshape(NUM_BUCKETS, 3 * HPG)
    return {n: jnp.stack(v) for n, v in out.items()}, rel


INPUT_NAMES = ("x", "mem") + ("rel_bias", "norm_mix_pre", "w_in", "b_gate", "conv_dw", "conv_dw_bias", "conv_ln_g",
                              "conv_ln_b", "w_conv_out", "w_att_out", "norm_mem", "w_mem_kv", "w_mem_out", "w_out",
                              "norm_mix_post", "norm_ffn_pre", "w_ffn_in", "w_ffn_out", "norm_ffn_post")
WEIGHT_NAMES = INPUT_NAMES[2:]


def kernel(*args):
    nw = len(WEIGHT_NAMES)
    a = dict(zip(INPUT_NAMES, args[:2 + nw]))
    target = args[2 + nw]
    mom = dict(zip(WEIGHT_NAMES, args[3 + nw:3 + 2 * nw]))
    var = dict(zip(WEIGHT_NAMES, args[3 + 2 * nw:3 + 3 * nw]))
    xi, yi, ci = _mesh_pos()
    chip = 2 * xi + yi
    where = jnp.stack([ci, chip]).astype(I32)

    shards = [[a[n][l].astype(BF16) for n in BIG] for l in range(DEPTH)]
    cdw = jnp.pad(a["conv_dw"].reshape(DEPTH * KSIZE, CW // 4), ((0, CDW_ROWS - DEPTH * KSIZE), (0, 0)))
    *gathered0, gcdw = _all_gather(shards[0], cdw)
    in_flight = _all_gather_start(shards[1], gathered0[0])
    gcdw = lax.dynamic_update_slice(gcdw, cdw[None], (chip, 0, 0))
    conv_dw = gcdw[:, :DEPTH * KSIZE].reshape(N_CHIPS, DEPTH, KSIZE, CW // 4).transpose(1, 2, 0, 3)
    conv_dw = jnp.pad(conv_dw.reshape(DEPTH, KSIZE, CW), ((0, 0), (0, 1), (0, 0)))
    gmix = [a["norm_mix_pre"][l][None, :] for l in range(DEPTH)]
    gmix[0] = gmix[0] + in_flight[4][0, 0]

    def layer_weights(l, gathered, own):
        w = {"conv_dw": conv_dw[l]}
        for n, g, s in zip(BIG, gathered, own):
            (s0, s1), axis = SHARD[n]
            blk = lax.dynamic_update_slice(g, s[None], (chip, 0, 0))
            w[n] = blk.reshape(N_CHIPS * s0, s1) if axis == 0 else blk.transpose(1, 0, 2).reshape(s0, N_CHIPS * s1)
        w["w_in"] = _z_cols_from_ref(w["w_in"])
        for n, _ in VEC_ROWS:
            w[n] = a[n][l][None, :]
        return w

    def layer1(x):
        send_sem, recv_sem, thru, lands, _ = in_flight
        own, gathered1 = _all_gather_wait(send_sem, recv_sem, thru, lands, x)
        return layer_weights(1, gathered1, own)

    loss_part, gx, grads, drel = _local_step(a["x"][0], a["mem"][0], target[0], a["rel_bias"],
                                             [lambda x: layer_weights(0, gathered0, shards[0]), layer1], gmix)
    loss = lax.psum(loss_part, ("x", "y", "c"))

    packed = []
    for n in BIG:
        (s0, s1), axis = SHARD[n]
        per_layer = []
        for l in range(DEPTH):
            g = _ref_cols_from_z(grads[l][n]) if n == "w_in" else grads[l][n]
            per_layer.append(g.reshape(N_CHIPS, s0, s1) if axis == 0 else g.reshape(s0, N_CHIPS, s1).transpose(1, 0, 2))
        packed.append(jnp.stack(per_layer).astype(BF16))
    from_sibling = _sibling_exchange(packed)
    chip_sums = [_add_own_layer(where, p, r, "grad_add_sibling_" + n) for n, p, r in zip(BIG, packed, from_sibling)]
    from_chips = _chip_exchange(chip_sums)
    reduced = _sibling_share([_sum_chips(where, s, r, "grad_sum_chips_" + n)
                              for n, s, r in zip(BIG, chip_sums, from_chips)])

    gvec = _all_reduce_small(_pack_vectors(
        lambda n: jnp.stack([grads[l][n][0] for l in range(DEPTH)]), drel[:, :3 * HPG],
        jnp.stack([grads[l]["conv_dw"] for l in range(DEPTH)]), "pack_vector_grads"))
    lens = {n: a[n].shape[1] for n, _ in VEC_ROWS}
    g_vec, g_rel = _unpack_vectors(gvec, lens)
    g_cdw = lax.dynamic_slice_in_dim(gvec[CDW_ROW:CDW_ROW + CDW_GROWS].reshape(DEPTH, KSIZE, CW), chip * (CW // 4),
                                     CW // 4, axis=2)

    grad, delta, new_m, new_v = {}, {}, {}, {}
    for n, g in zip(BIG, reduced):
        shape = a[n].shape
        flat2 = lambda t: t.reshape(shape[0] * shape[1], shape[2])
        d, m2, v2 = _adamw(flat2(a[n]), flat2(g), flat2(mom[n]), flat2(var[n]), "adamw_" + n)
        grad[n], delta[n], new_m[n], new_v[n] = g, d.reshape(shape), m2.reshape(shape), v2.reshape(shape)
    shape = a["conv_dw"].shape
    flat2 = lambda t: t.reshape(shape[0] * shape[1], shape[2])
    d, m2, v2 = _adamw(flat2(a["conv_dw"]), flat2(g_cdw), flat2(mom["conv_dw"]), flat2(var["conv_dw"]), "adamw_conv_dw")
    grad["conv_dw"], delta["conv_dw"], new_m["conv_dw"], new_v["conv_dw"] = (
        g_cdw, d.reshape(shape), m2.reshape(shape), v2.reshape(shape))
    zero_cdw = jnp.zeros((DEPTH, KSIZE, CW), F32)
    pk = lambda src, name: _pack_vectors(lambda n: src[n], src["rel_bias"], zero_cdw, name)
    d, m2, v2 = _adamw(pk(a, "pack_vector_w"), gvec, pk(mom, "pack_vector_m"), pk(var, "pack_vector_v"),
                       "adamw_vectors")
    for src, dst in ((d, delta), (m2, new_m), (v2, new_v)):
        vec, rel = _unpack_vectors(src, lens)
        dst.update(vec)
        dst["rel_bias"] = rel
    grad.update(g_vec)
    grad["rel_bias"] = g_rel

    outs = [loss, gx[None]]
    for group in (grad, delta, new_m, new_v):
        outs += [group[n] for n in WEIGHT_NAMES]
    return tuple(outs)
```

```python
import functools
import math

import jax
import jax.numpy as jnp
from jax import lax
from jax.experimental import pallas as pl
from jax.experimental.pallas import tpu as pltpu

F32 = jnp.float32
BF16 = jnp.bfloat16
I32 = jnp.int32

D = 1024
DEPTH = 2
N_MEM = 256
CW = 512
KSIZE = 31
PAD = KSIZE // 2
DILS = (1, 4, 16)
RADIUS = 64
HPG = 4
HD = 64
GW = HPG * HD
MH = 4
MHD = 128
MW = MH * MHD
FH = 2816
NIN = 6912
C1 = 2 * CW
R_ATT = C1
R_MEM = R_ATT + 9 * GW
R_GATE = R_MEM + MW
Z_GATE = 0
Z_CONV = 3 * D
Z_MEM = Z_CONV + C1
Z_ATT = Z_MEM + MW
NUM_BUCKETS = 32
MAX_DISTANCE = 1024
RMS_EPS = 1e-6
LN_EPS = 1e-5
NEG_INF = -1e30
ATT_SCALE = HD ** -0.5
MEM_SCALE = MHD ** -0.5

ADAM_LR = 0.001
ADAM_B1 = 0.9
ADAM_B2 = 0.999
ADAM_EPS = 1e-08
ADAM_WD = 0.01
ADAM_STEP = 10

VMEM_LIMIT_BYTES = 56 * 1024 * 1024
ATT_QB = 128
ATT_TB = 16 * ATT_QB

MESH = pl.DeviceIdType.MESH


def _params(sem=None):
    return pltpu.CompilerParams(dimension_semantics=sem, vmem_limit_bytes=VMEM_LIMIT_BYTES)


def _sigmoid(v):
    return 1.0 / (1.0 + jnp.exp(-v))


def _dot(a, b):
    return jnp.dot(a, b, preferred_element_type=F32)


def _dot_nt(a, b):
    return lax.dot_general(a, b, (((1,), (1,)), ((), ())), preferred_element_type=F32)


def _dot_tn(a, b):
    return lax.dot_general(a, b, (((0,), (0,)), ((), ())), preferred_element_type=F32)


def _rms_fwd_val(v, g):
    r = lax.rsqrt(jnp.mean(v * v, axis=-1, keepdims=True) + RMS_EPS)
    return v * r * g


def _rms_bwd_val(v, g, dy):
    r = lax.rsqrt(jnp.mean(v * v, axis=-1, keepdims=True) + RMS_EPS)
    vh = v * r
    dvh = dy * g
    dv = r * (dvh - vh * jnp.mean(dvh * vh, axis=-1, keepdims=True))
    return dv, dy * vh


def _row(i):
    return (i, 0)


def _fixed(*_):
    return (0, 0)


def _mm_nn(a, b, tm, out_dtype, name):
    M, K = a.shape
    N = b.shape[1]

    def body(a_ref, b_ref, o_ref):
        o_ref[...] = _dot(a_ref[...], b_ref[...]).astype(out_dtype)

    return pl.pallas_call(
        body, name=name, grid=(M // tm,),
        in_specs=[pl.BlockSpec((tm, K), _row), pl.BlockSpec((K, N), _fixed, pipeline_mode=pl.Buffered(1))],
        out_specs=pl.BlockSpec((tm, N), _row),
        out_shape=jax.ShapeDtypeStruct((M, N), out_dtype),
        compiler_params=_params(("parallel",)),
    )(a, b)


def _mm_nt_rms_bwd(a, b, x, g, dres, tm, name):
    M, N = a.shape

    def body(a_ref, b_ref, x_ref, g_ref, dres_ref, dx_ref, dg_ref):
        @pl.when(pl.program_id(0) == 0)
        def _():
            dg_ref[...] = jnp.zeros_like(dg_ref)

        dv, dgr = _rms_bwd_val(x_ref[...], g_ref[...], _dot_nt(a_ref[...], b_ref[...]))
        dx_ref[...] = dres_ref[...] + dv
        dg_ref[...] += jnp.sum(dgr, axis=0, keepdims=True)

    rows = pl.BlockSpec((tm, D), _row)
    return pl.pallas_call(
        body, name=name, grid=(M // tm,),
        in_specs=[pl.BlockSpec((tm, N), _row), pl.BlockSpec((D, N), _fixed, pipeline_mode=pl.Buffered(1)), rows,
                  pl.BlockSpec((1, D), _fixed), rows],
        out_specs=[rows, pl.BlockSpec((1, D), _fixed)],
        out_shape=[jax.ShapeDtypeStruct((M, D), F32), jax.ShapeDtypeStruct((1, D), F32)],
        compiler_params=_params(("arbitrary",)),
    )(a, b, x, g, dres)


def _mm_tn(a, b, ts, tn, name):
    S, K = a.shape
    N = b.shape[1]

    def body(a_ref, b_ref, o_ref):
        @pl.when(pl.program_id(1) == 0)
        def _():
            o_ref[...] = jnp.zeros_like(o_ref)

        o_ref[...] += _dot_tn(a_ref[...], b_ref[...])

    return pl.pallas_call(
        body, name=name, grid=(N // tn, S // ts),
        in_specs=[pl.BlockSpec((ts, K), lambda j, s: (s, 0)), pl.BlockSpec((ts, tn), lambda j, s: (s, j))],
        out_specs=pl.BlockSpec((K, tn), lambda j, s: (0, j)),
        out_shape=jax.ShapeDtypeStruct((K, N), F32),
        compiler_params=_params(("parallel", "arbitrary")),
    )(a, b)


def _rms_h(x, g, name):
    S = x.shape[0]
    T = 512

    def body(x_ref, g_ref, h_ref):
        h_ref[...] = _rms_fwd_val(x_ref[...], g_ref[...]).astype(BF16)

    return pl.pallas_call(
        body, name=name, grid=(S // T,),
        in_specs=[pl.BlockSpec((T, D), _row), pl.BlockSpec((1, D), _fixed)],
        out_specs=pl.BlockSpec((T, D), _row),
        out_shape=jax.ShapeDtypeStruct((S, D), BF16),
        compiler_params=_params(("parallel",)),
    )(x, g)


CONV_T = 256
CONV_HALO = 16
CONV_RC = 32


def _halo_specs(T, halo, S, width, col):
    per = T // halo
    last = S // halo - 1
    return [
        pl.BlockSpec((T, width), lambda i: (i, col)),
        pl.BlockSpec((halo, width), lambda i: (jnp.maximum(i * per - 1, 0), col)),
        pl.BlockSpec((halo, width), lambda i: (jnp.minimum((i + 1) * per, last), col)),
    ]


def _glu(zb):
    zb = zb.astype(F32)
    return zb[:, :CW] * _sigmoid(zb[:, CW:])


CONV_EXT = CONV_T + 2 * CONV_HALO
SUBLANES = 8


def _fill_shifted(sh_ref, ext_ref, cur, prev, nxt):
    T, halo = CONV_T, CONV_HALO
    i = pl.program_id(0)
    n = pl.num_programs(0)
    ext_ref[0:halo, :] = jnp.where(i > 0, prev, 0.0)
    ext_ref[halo:halo + T, :] = cur
    ext_ref[halo + T:CONV_EXT, :] = jnp.where(i < n - 1, nxt, 0.0)
    ext_ref[CONV_EXT:CONV_EXT + SUBLANES, :] = jnp.zeros((SUBLANES, CW), F32)
    for b in range(SUBLANES):
        sh_ref[b] = ext_ref[b:b + CONV_EXT, :]


def _window(sh_ref, start, rows):
    b = start % SUBLANES
    return sh_ref[b, start - b:start - b + rows, :]


def _shifted_scratch():
    return [pltpu.VMEM((CONV_EXT + SUBLANES, CW), F32), pltpu.VMEM((SUBLANES, CONV_EXT, CW), F32)]


def _conv_fwd(z, wdw, bdw, lng, lnb, name):
    S = z.shape[0]
    T, HL, RC = CONV_T, CONV_HALO, CONV_RC

    def body(cur_ref, prev_ref, next_ref, w_ref, b_ref, g_ref, bb_ref, yc_ref, act_ref, ext_ref, sh_ref):
        _fill_shifted(sh_ref, ext_ref, _glu(cur_ref[...]), _glu(prev_ref[...]), _glu(next_ref[...]))
        for c in range(T // RC):
            acc = jnp.zeros((RC, CW), F32)
            for k in range(KSIZE):
                acc = acc + w_ref[k:k + 1, :] * _window(sh_ref, c * RC + k + HL - PAD, RC)
            yc = acc + b_ref[...]
            yc_ref[c * RC:(c + 1) * RC, :] = yc
            mu = jnp.mean(yc, axis=-1, keepdims=True)
            xc = yc - mu
            ln = xc * lax.rsqrt(jnp.mean(xc * xc, axis=-1, keepdims=True) + LN_EPS) * g_ref[...] + bb_ref[...]
            act_ref[c * RC:(c + 1) * RC, :] = (ln * _sigmoid(ln)).astype(BF16)

    return pl.pallas_call(
        body, name=name, grid=(S // T,),
        in_specs=_halo_specs(T, HL, S, C1, Z_CONV // C1) + [pl.BlockSpec((32, CW), _fixed)]
        + [pl.BlockSpec((1, CW), _fixed)] * 3,
        out_specs=[pl.BlockSpec((T, CW), _row), pl.BlockSpec((T, CW), _row)],
        out_shape=[jax.ShapeDtypeStruct((S, CW), F32), jax.ShapeDtypeStruct((S, CW), BF16)],
        scratch_shapes=_shifted_scratch(),
        compiler_params=_params(("parallel",)),
    )(z, z, z, wdw, bdw, lng, lnb)


def _conv_bwd_ln(yc, dact, lng, lnb, name):
    S = yc.shape[0]
    T = 512

    def body(yc_ref, da_ref, g_ref, b_ref, dyc_ref, dg_ref, db_ref, dbias_ref):
        yc_v = yc_ref[...]
        mu = jnp.mean(yc_v, axis=-1, keepdims=True)
        xc = yc_v - mu
        r = lax.rsqrt(jnp.mean(xc * xc, axis=-1, keepdims=True) + LN_EPS)
        yn = xc * r
        ln = yn * g_ref[...] + b_ref[...]
        sg = _sigmoid(ln)
        dln = da_ref[...].astype(F32) * (sg * (1.0 + ln * (1.0 - sg)))
        dyn = dln * g_ref[...]
        dyc = r * (dyn - jnp.mean(dyn, axis=-1, keepdims=True) - yn * jnp.mean(dyn * yn, axis=-1, keepdims=True))
        dyc_ref[...] = dyc

        @pl.when(pl.program_id(0) == 0)
        def _():
            dg_ref[...] = jnp.zeros_like(dg_ref)
            db_ref[...] = jnp.zeros_like(db_ref)
            dbias_ref[...] = jnp.zeros_like(dbias_ref)

        dg_ref[...] += jnp.sum(dln * yn, axis=0, keepdims=True)
        db_ref[...] += jnp.sum(dln, axis=0, keepdims=True)
        dbias_ref[...] += jnp.sum(dyc, axis=0, keepdims=True)

    vec = pl.BlockSpec((1, CW), _fixed)
    return pl.pallas_call(
        body, name=name, grid=(S // T,),
        in_specs=[pl.BlockSpec((T, CW), _row), pl.BlockSpec((T, CW), _row), vec, vec],
        out_specs=[pl.BlockSpec((T, CW), _row), vec, vec, vec],
        out_shape=[jax.ShapeDtypeStruct((S, CW), F32)] + [jax.ShapeDtypeStruct((1, CW), F32)] * 3,
        compiler_params=_params(("arbitrary",)),
    )(yc, dact, lng, lnb)


def _conv_bwd_dw(z, dyc, wdw, dz, name):
    S = z.shape[0]
    T, HL, RC = CONV_T, CONV_HALO, CONV_RC

    def body(zc_ref, zp_ref, zn_ref, dc_ref, dp_ref, dn_ref, w_ref, dz_in, dz_ref, dw_ref, uext_ref, ush_ref,
             dext_ref, dsh_ref, dwacc_ref):
        _fill_shifted(ush_ref, uext_ref, _glu(zc_ref[...]), _glu(zp_ref[...]), _glu(zn_ref[...]))
        _fill_shifted(dsh_ref, dext_ref, dc_ref[...], dp_ref[...], dn_ref[...])

        @pl.when(pl.program_id(0) == 0)
        def _():
            dwacc_ref[...] = jnp.zeros_like(dwacc_ref)

        for c in range(T // RC):
            dcur = dc_ref[c * RC:(c + 1) * RC, :]
            du = jnp.zeros((RC, CW), F32)
            for k in range(KSIZE):
                du = du + w_ref[k:k + 1, :] * _window(dsh_ref, c * RC + HL + PAD - k, RC)
                prod = dcur * _window(ush_ref, c * RC + k + HL - PAD, RC)
                dwacc_ref[k] += jnp.sum(prod.reshape(RC // SUBLANES, SUBLANES, CW), axis=0)
            zc = zc_ref[c * RC:(c + 1) * RC, :].astype(F32)
            a, gt = zc[:, :CW], zc[:, CW:]
            sg = _sigmoid(gt)
            dz_ref[c * RC:(c + 1) * RC, 0:CW] = (du * sg).astype(BF16)
            dz_ref[c * RC:(c + 1) * RC, CW:C1] = (du * a * sg * (1.0 - sg)).astype(BF16)

        @pl.when(pl.program_id(0) == pl.num_programs(0) - 1)
        def _():
            dw_ref[...] = jnp.sum(dwacc_ref[...], axis=1)

    return pl.pallas_call(
        body, name=name, grid=(S // T,),
        in_specs=_halo_specs(T, HL, S, C1, Z_CONV // C1) + _halo_specs(T, HL, S, CW, 0)
        + [pl.BlockSpec((32, CW), _fixed), pl.BlockSpec(memory_space=pl.ANY)],
        out_specs=[pl.BlockSpec((T, C1), lambda i: (i, Z_CONV // C1)), pl.BlockSpec((32, CW), _fixed)],
        out_shape=[jax.ShapeDtypeStruct(dz.shape, BF16), jax.ShapeDtypeStruct((32, CW), F32)],
        input_output_aliases={7: 0},
        scratch_shapes=_shifted_scratch() + _shifted_scratch() + [pltpu.VMEM((32, SUBLANES, CW), F32)],
        compiler_params=_params(("arbitrary",)),
    )(z, z, z, dyc, dyc, dyc, wdw, dz)


def _t5_bucket(rel):
    nb = NUM_BUCKETS // 2
    max_exact = nb // 2
    ret = jnp.where(rel > 0, nb, 0)
    n = jnp.abs(rel)
    nf = jnp.maximum(n, 1).astype(F32)
    large = max_exact + (jnp.log(nf / max_exact) / math.log(MAX_DISTANCE / max_exact)
                         * (nb - max_exact)).astype(I32)
    large = jnp.minimum(large, nb - 1)
    return ret + jnp.where(n < max_exact, n, large)


def _offsets_qk(nq, nk, shift):
    return lax.broadcasted_iota(I32, (nq, nk), 1) + shift - lax.broadcasted_iota(I32, (nq, nk), 0)


def _bias_table(bk, rb_ref, col, off):
    acc = jnp.zeros(bk.shape, F32)
    for b in range(NUM_BUCKETS):
        acc = jnp.where(bk == b, rb_ref[b, col], acc)
    return jnp.where(jnp.abs(off) <= RADIUS, acc, NEG_INF)


def _to_halves(scr, row0, val):
    rows = val.shape[0]
    v = val.astype(F32)
    scr[0, row0:row0 + rows, :] = v[:, :128]
    scr[1, row0:row0 + rows, :] = v[:, 128:]


ATT_FWD_GROUP = 2
ATT_BWD_GROUP = 1


def _att_units(d, fn, group):
    nj = ATT_TB // (ATT_QB * d)
    if nj == 1:
        def trip(t, c):
            r0 = pl.multiple_of(t * 8, 8)
            for u in range(0, 8, group):
                fn([(r0 + u + v, 0) for v in range(group)])
            return c

        lax.fori_loop(0, d // 8, trip, 0)
        return
    for r in range(d):
        def step(t, c, r=r):
            fn([(r, t * group + u) for u in range(group)])
            return c

        lax.fori_loop(0, nj // group, step, 0)


def _unit_row(r, j, d):
    if isinstance(j, int):
        return j * ATT_QB * d + r
    return pl.multiple_of(j * (ATT_QB * d), ATT_QB) + r


def _att_fwd(z, rel_bias, g, name):
    S = z.shape[0]
    d = DILS[g]
    TB, QB = ATT_TB, ATT_QB
    H = RADIUS * d
    L = S // d
    cq = (Z_ATT + 3 * GW * g) // GW
    ck, cv = cq + 1, cq + 2
    bk = _t5_bucket(_offsets_qk(QB, 2 * QB, -RADIUS) * d)

    def body(rb_ref, bk_ref, q_ref, kc_ref, kp_ref, kn_ref, vc_ref, vp_ref, vn_ref, o_ref, l_ref,
             qs, ks, vs, os_, ls, bias):
        i = pl.program_id(0)

        @pl.when(i == 0)
        def _():
            off = _offsets_qk(QB, 2 * QB, -RADIUS)
            for h in range(HPG):
                bias[h] = _bias_table(bk_ref[...], rb_ref, g * HPG + h, off)

        _to_halves(qs, 0, q_ref[...].astype(F32) * ATT_SCALE)
        for scr, p_ref, c_ref, n_ref in ((ks, kp_ref, kc_ref, kn_ref), (vs, vp_ref, vc_ref, vn_ref)):
            _to_halves(scr, 0, p_ref[...])
            _to_halves(scr, H, c_ref[...])
            _to_halves(scr, H + TB, n_ref[...])

        lo = lax.broadcasted_iota(I32, (QB, 128), 1) < HD

        def units(rjs):
            work = []
            for r, j in rjs:
                row = _unit_row(r, j, d)
                km = lax.broadcasted_iota(I32, (1, 2 * QB), 1) + (i * (TB // d) + j * QB - RADIUS)
                edge = jnp.where(jnp.where(km >= 0, km, L) < L, 0.0, NEG_INF)
                for hf in (0, 1):
                    q2 = qs[hf, pl.ds(row, QB, stride=d), :]
                    k2 = ks[hf, pl.ds(row, 2 * QB, stride=d), :].astype(BF16)
                    v2 = vs[hf, pl.ds(row, 2 * QB, stride=d), :].astype(BF16)
                    qq = jnp.concatenate([jnp.where(lo, q2, 0.0), jnp.where(lo, 0.0, q2)], axis=0).astype(BF16)
                    work.append((row, hf, edge, k2, v2, qq))
            scores = [_dot_nt(qq, k2) for (_, _, _, k2, _, qq) in work]
            probs = []
            for (row, hf, edge, *_), ss in zip(work, scores):
                es, stats = [], []
                for hh in (0, 1):
                    s = ss[hh * QB:(hh + 1) * QB] + bias[2 * hf + hh] + edge
                    m = jnp.max(s, axis=-1, keepdims=True)
                    e = jnp.exp(s - m)
                    den = jnp.sum(e, axis=-1, keepdims=True)
                    es.append(e.astype(BF16))
                    stats.append((1.0 / den, m + jnp.log(den)))
                probs.append((jnp.concatenate(es, axis=0), stats))
            for (row, hf, _, _, v2, _), (ee, stats) in zip(work, probs):
                oo = _dot(ee, v2)
                os_[hf, pl.ds(row, QB, stride=d), :] = jnp.where(lo, oo[:QB] * stats[0][0], oo[QB:] * stats[1][0])
                ls[hf, pl.ds(row, QB, stride=d), :] = jnp.where(lo, stats[0][1], stats[1][1])

        _att_units(d, units, ATT_FWD_GROUP)
        for hf in (0, 1):
            o_ref[:, hf * 128:(hf + 1) * 128] = os_[hf].astype(BF16)
            l_ref[:, hf * 128:(hf + 1) * 128] = ls[hf]

    def halo3(col):
        c, p, n = _halo_specs(TB, H, S, GW, col)
        return [c, p, n]

    return pl.pallas_call(
        body, name=name, grid=(S // TB,),
        in_specs=[pl.BlockSpec(memory_space=pltpu.SMEM), pl.BlockSpec((QB, 2 * QB), _fixed),
                  pl.BlockSpec((TB, GW), lambda i: (i, cq))] + halo3(ck) + halo3(cv),
        out_specs=[pl.BlockSpec((TB, GW), _row), pl.BlockSpec((TB, GW), _row)],
        out_shape=[jax.ShapeDtypeStruct((S, GW), BF16), jax.ShapeDtypeStruct((S, GW), F32)],
        scratch_shapes=[pltpu.VMEM((2, TB, 128), F32), pltpu.VMEM((2, TB + 2 * H, 128), F32),
                        pltpu.VMEM((2, TB + 2 * H, 128), F32), pltpu.VMEM((2, TB, 128), F32),
                        pltpu.VMEM((2, TB, 128), F32), pltpu.VMEM((HPG, QB, 2 * QB), F32)],
        compiler_params=_params(("arbitrary",)),
    )(rel_bias, bk, z, z, z, z, z, z, z)


def _att_combine(os3, ls3, name):
    S = os3[0].shape[0]
    T = 1024

    def body(o1, o2, o3, l1, l2, l3, o_ref, l_ref):
        lv = [l1[...], l2[...], l3[...]]
        m = jnp.maximum(jnp.maximum(lv[0], lv[1]), lv[2])
        e = [jnp.exp(v - m) for v in lv]
        den = e[0] + e[1] + e[2]
        acc = jnp.zeros_like(m)
        for ev, o in zip(e, (o1, o2, o3)):
            acc = acc + (ev / den) * o[...].astype(F32)
        o_ref[...] = acc.astype(BF16)
        l_ref[...] = m + jnp.log(den)

    blk = pl.BlockSpec((T, GW), _row)
    return pl.pallas_call(
        body, name=name, grid=(S // T,), in_specs=[blk] * 6, out_specs=[blk, blk],
        out_shape=[jax.ShapeDtypeStruct((S, GW), BF16), jax.ShapeDtypeStruct((S, GW), F32)],
        compiler_params=_params(("parallel",)),
    )(*os3, *ls3)


def _att_prep(do, o, lse, name):
    S = do.shape[0]
    T = 1024

    def body(do_ref, o_ref, l_ref, out_ref):
        prod = do_ref[...].astype(F32) * o_ref[...].astype(F32)
        dd = [jnp.broadcast_to(jnp.sum(prod[:, h * HD:(h + 1) * HD], axis=-1, keepdims=True), (T, HD))
              for h in range(HPG)]
        lane = lax.broadcasted_iota(I32, (T, GW), 1)
        out_ref[...] = jnp.where(lane % HD < HD // 2, l_ref[...], jnp.concatenate(dd, axis=-1))

    blk = pl.BlockSpec((T, GW), _row)
    return pl.pallas_call(
        body, name=name, grid=(S // T,), in_specs=[blk] * 3, out_specs=blk,
        out_shape=jax.ShapeDtypeStruct((S, GW), F32), compiler_params=_params(("parallel",)),
    )(do, o, lse)


def _att_bwd(z, rel_bias, do, ld, dz, g, name):
    S = z.shape[0]
    d = DILS[g]
    TB, QB = ATT_TB, ATT_QB
    H = RADIUS * d
    L = S // d
    E = TB + 2 * H
    cq = (Z_ATT + 3 * GW * g) // GW
    ck, cv = cq + 1, cq + 2
    bk_a = _t5_bucket(_offsets_qk(QB, 2 * QB, -RADIUS) * d)
    bk_b = _t5_bucket(-_offsets_qk(QB, 2 * QB, -RADIUS) * d)

    def body(rb_ref, bka_ref, bkb_ref, *refs):
        ins, (dz_ref, db_ref) = refs[:15], refs[16:18]
        qs, ks, vs, dos, ls, dqs, dks, dvs, bias_a, bias_b, dbias = refs[18:]
        i = pl.program_id(0)
        n = pl.num_programs(0)

        @pl.when(i == 0)
        def _():
            off = _offsets_qk(QB, 2 * QB, -RADIUS)
            for h in range(HPG):
                bias_a[h] = _bias_table(bka_ref[...], rb_ref, g * HPG + h, off)
                bias_b[h] = _bias_table(bkb_ref[...], rb_ref, g * HPG + h, off)
            dbias[...] = jnp.zeros_like(dbias)

        for a, scr in enumerate((qs, ks, vs, dos, ls)):
            c_ref, p_ref, n_ref = ins[3 * a:3 * a + 3]
            pre = (lambda v: v.astype(F32) * ATT_SCALE) if a == 0 else (lambda v: v)
            _to_halves(scr, 0, pre(p_ref[...]))
            _to_halves(scr, H, pre(c_ref[...]))
            _to_halves(scr, H + TB, pre(n_ref[...]))

        lo = lax.broadcasted_iota(I32, (QB, 128), 1) < HD

        def split(v):
            return jnp.concatenate([jnp.where(lo, v, 0.0), jnp.where(lo, 0.0, v)], axis=0).astype(BF16)

        def halves(v):
            return v[:QB], v[QB:]

        def units(rjs):
            work = []
            for r, j in rjs:
                row = _unit_row(r, j, d)
                cur = row + H
                m0 = i * (TB // d) + j * QB - RADIUS
                km = lax.broadcasted_iota(I32, (1, 2 * QB), 1) + m0
                edge_a = jnp.where(jnp.where(km >= 0, km, L) < L, 0.0, NEG_INF)
                for hf in (0, 1):
                    ld = lambda scr, at, nrow: scr[hf, pl.ds(at, nrow, stride=d), :]
                    w = dict(row=row, hf=hf, edge=edge_a, l_c=ld(ls, cur, QB), l_t=ld(ls, row, 2 * QB).T)
                    for nm, scr in (("q", qs), ("k", ks), ("v", vs), ("do", dos)):
                        w[nm + "_c"] = split(ld(scr, cur, QB))
                        w[nm + "_e"] = ld(scr, row, 2 * QB).astype(BF16)
                    work.append(w)
            for w in work:
                w["s"] = halves(_dot_nt(w["q_c"], w["k_e"]))
                w["dp"] = halves(_dot_nt(w["do_c"], w["v_e"]))
                w["s2"] = halves(_dot_nt(w["k_c"], w["q_e"]))
                w["dp2"] = halves(_dot_nt(w["v_c"], w["do_e"]))
            for w in work:
                w["ds"], w["p2"], w["ds2"] = [], [], []
                for hh in (0, 1):
                    h, c0 = 2 * w["hf"] + hh, HD * hh
                    l_c, l_t = w["l_c"], w["l_t"]
                    p = jnp.exp(w["s"][hh] + bias_a[h] + w["edge"] - l_c[:, c0:c0 + 1])
                    ds = p * (w["dp"][hh] - l_c[:, c0 + HD // 2:c0 + HD // 2 + 1])
                    dbias[h] += ds
                    p2 = jnp.exp(w["s2"][hh] + bias_b[h] + w["edge"] - l_t[c0:c0 + 1, :])
                    ds2 = p2 * (w["dp2"][hh] - l_t[c0 + HD // 2:c0 + HD // 2 + 1, :])
                    w["ds"].append(ds.astype(BF16))
                    w["p2"].append(p2.astype(BF16))
                    w["ds2"].append(ds2.astype(BF16))
            for w in work:
                at = pl.ds(w["row"], QB, stride=d)
                both = lambda pair, rhs: halves(_dot(jnp.concatenate(pair, axis=0), rhs))
                dq = both(w["ds"], w["k_e"])
                dqs[w["hf"], at, :] = jnp.where(lo, dq[0], dq[1]) * ATT_SCALE
                dv = both(w["p2"], w["do_e"])
                dvs[w["hf"], at, :] = jnp.where(lo, dv[0], dv[1])
                dk = both(w["ds2"], w["q_e"])
                dks[w["hf"], at, :] = jnp.where(lo, dk[0], dk[1])

        _att_units(d, units, ATT_BWD_GROUP)
        for a, scr in enumerate((dqs, dks, dvs)):
            for hf in (0, 1):
                dz_ref[:, a * GW + hf * 128:a * GW + (hf + 1) * 128] = scr[hf].astype(BF16)

        @pl.when(i == n - 1)
        def _():
            rows = lax.broadcasted_iota(I32, (NUM_BUCKETS, 128), 0)
            lanes = lax.broadcasted_iota(I32, (NUM_BUCKETS, 128), 1)
            out = jnp.zeros((NUM_BUCKETS, 128), F32)
            bk = bka_ref[...]
            for h in range(HPG):
                acc = dbias[h]
                for b in range(NUM_BUCKETS):
                    tot = jnp.sum(jnp.sum(jnp.where(bk == b, acc, 0.0), axis=1, keepdims=True), axis=0, keepdims=True)
                    out = out + jnp.where((rows == b) & (lanes == h), tot, 0.0)
            db_ref[...] = out

    def halo3(col, width=GW):
        return _halo_specs(TB, H, S, width, col)

    one = pl.Buffered(1)

    def single(specs):
        return [pl.BlockSpec(s.block_shape, s.index_map, pipeline_mode=one) for s in specs]

    in_specs = ([pl.BlockSpec(memory_space=pltpu.SMEM), pl.BlockSpec((QB, 2 * QB), _fixed),
                 pl.BlockSpec((QB, 2 * QB), _fixed)]
                + single(halo3(cq) + halo3(ck) + halo3(cv) + halo3(0) + halo3(0))
                + [pl.BlockSpec(memory_space=pl.ANY)])
    return pl.pallas_call(
        body, name=name, grid=(S // TB,), in_specs=in_specs,
        out_specs=[pl.BlockSpec((TB, 3 * GW), lambda i: (i, cq // 3)), pl.BlockSpec((NUM_BUCKETS, 128), _fixed)],
        out_shape=[jax.ShapeDtypeStruct(dz.shape, BF16), jax.ShapeDtypeStruct((NUM_BUCKETS, 128), F32)],
        input_output_aliases={18: 0},
        scratch_shapes=[pltpu.VMEM((2, E, 128), F32)] * 5 + [pltpu.VMEM((2, TB, 128), F32)] * 3
        + [pltpu.VMEM((HPG, QB, 2 * QB), F32)] * 3,
        compiler_params=_params(("arbitrary",)),
    )(rel_bias, bk_a, bk_b, z, z, z, z, z, z, z, z, z, do, do, do, ld, ld, ld, dz)


def _memkv_fwd(mem, gm, wkv, name):
    def body(m_ref, g_ref, w_ref, hm_ref, kv_ref):
        hm = _rms_fwd_val(m_ref[...], g_ref[...]).astype(BF16)
        hm_ref[...] = hm
        kv_ref[...] = _dot(hm, w_ref[...]).astype(BF16)

    return pl.pallas_call(
        body, name=name,
        out_shape=[jax.ShapeDtypeStruct((N_MEM, D), BF16), jax.ShapeDtypeStruct((N_MEM, 2 * MW), BF16)],
        compiler_params=_params(),
    )(mem, gm, wkv)


def _memkv_bwd(mem, gm, hm, wkv, dkv, name):
    def body(m_ref, g_ref, hm_ref, w_ref, dkv_ref, dw_ref, dg_ref):
        dkv_b = dkv_ref[...].astype(BF16)
        dw_ref[...] = _dot_tn(hm_ref[...], dkv_b)
        dhm = _dot_nt(dkv_b, w_ref[...])
        _, dgr = _rms_bwd_val(m_ref[...], g_ref[...], dhm)
        dg_ref[...] = jnp.sum(dgr, axis=0, keepdims=True)

    return pl.pallas_call(
        body, name=name,
        out_shape=[jax.ShapeDtypeStruct((D, 2 * MW), F32), jax.ShapeDtypeStruct((1, D), F32)],
        compiler_params=_params(),
    )(mem, gm, hm, wkv, dkv)


MEM_T = 512


def _mem_q_spec():
    return pl.BlockSpec((MEM_T, MW), lambda i: (i, Z_MEM // MW))


def _memattn_fwd(z, kv, name):
    S = z.shape[0]
    T = MEM_T

    def body(q_ref, kv_ref, o_ref):
        for h in range(MH):
            kh = kv_ref[:, h * MHD:(h + 1) * MHD]
            vh = kv_ref[:, MW + h * MHD:MW + (h + 1) * MHD]
            s = _dot_nt(q_ref[:, h * MHD:(h + 1) * MHD], kh) * MEM_SCALE
            e = jnp.exp(s - jnp.max(s, axis=-1, keepdims=True))
            p = e / jnp.sum(e, axis=-1, keepdims=True)
            o_ref[:, h * MHD:(h + 1) * MHD] = _dot(p.astype(BF16), vh).astype(BF16)

    return pl.pallas_call(
        body, name=name, grid=(S // T,),
        in_specs=[_mem_q_spec(), pl.BlockSpec((N_MEM, 2 * MW), _fixed)],
        out_specs=pl.BlockSpec((T, MW), _row),
        out_shape=jax.ShapeDtypeStruct((S, MW), BF16),
        compiler_params=_params(("parallel",)),
    )(z, kv)


def _memattn_bwd(z, kv, dom, dz, name):
    S = z.shape[0]
    T = MEM_T

    def body(q_ref, kv_ref, do_ref, dz_in, dq_ref, dkv_ref):
        @pl.when(pl.program_id(0) == 0)
        def _():
            dkv_ref[...] = jnp.zeros_like(dkv_ref)

        for h in range(MH):
            kh = kv_ref[:, h * MHD:(h + 1) * MHD]
            vh = kv_ref[:, MW + h * MHD:MW + (h + 1) * MHD]
            qh = q_ref[:, h * MHD:(h + 1) * MHD]
            doh = do_ref[:, h * MHD:(h + 1) * MHD]
            s = _dot_nt(qh, kh) * MEM_SCALE
            e = jnp.exp(s - jnp.max(s, axis=-1, keepdims=True))
            p = e / jnp.sum(e, axis=-1, keepdims=True)
            dkv_ref[:, MW + h * MHD:MW + (h + 1) * MHD] += _dot_tn(p.astype(BF16), doh)
            dp = _dot_nt(doh, vh)
            ds = (p * (dp - jnp.sum(dp * p, axis=-1, keepdims=True))).astype(BF16)
            dq_ref[:, h * MHD:(h + 1) * MHD] = (_dot(ds, kh) * MEM_SCALE).astype(BF16)
            dkv_ref[:, h * MHD:(h + 1) * MHD] += _dot_tn(ds, qh) * MEM_SCALE

    return pl.pallas_call(
        body, name=name, grid=(S // T,),
        in_specs=[_mem_q_spec(), pl.BlockSpec((N_MEM, 2 * MW), _fixed), pl.BlockSpec((T, MW), _row),
                  pl.BlockSpec(memory_space=pl.ANY)],
        out_specs=[_mem_q_spec(), pl.BlockSpec((N_MEM, 2 * MW), _fixed)],
        out_shape=[jax.ShapeDtypeStruct(dz.shape, BF16), jax.ShapeDtypeStruct((N_MEM, 2 * MW), F32)],
        input_output_aliases={3: 0},
        compiler_params=_params(("arbitrary",)),
    )(z, kv, dom, dz)


MERGE_T = 512


def _gate_spec(T):
    return pl.BlockSpec((T, 3 * D), lambda i: (i, Z_GATE // (3 * D)))


def _branches(ca_ref, oa_ref, om_ref, wco_ref, wao_ref, wmo_ref, zg_ref, bg_ref):
    ys = [_dot(ca_ref[...], wco_ref[...]), _dot(oa_ref[...], wao_ref[...]), _dot(om_ref[...], wmo_ref[...])]
    gs = [_sigmoid(zg_ref[:, b * D:(b + 1) * D].astype(F32) + bg_ref[:, b * D:(b + 1) * D]) for b in range(3)]
    return ys, gs


def _merge_fwd(x, cact, oatt, om, z, wco, wao, wmo, wout, bgate, gpost, gnext, name):
    S = x.shape[0]
    T = MERGE_T

    def body(x_ref, ca_ref, oa_ref, om_ref, zg_ref, wco_ref, wao_ref, wmo_ref, wout_ref, bg_ref, gp_ref, gn_ref,
             x1_ref, mg_ref, t_ref, h_ref):
        ys, gs = _branches(ca_ref, oa_ref, om_ref, wco_ref, wao_ref, wmo_ref, zg_ref, bg_ref)
        mb = (gs[0] * ys[0] + gs[1] * ys[1] + gs[2] * ys[2]).astype(BF16)
        t = _dot(mb, wout_ref[...])
        mg_ref[...] = mb
        t_ref[...] = t
        x1 = x_ref[...] + _rms_fwd_val(t, gp_ref[...])
        x1_ref[...] = x1
        h_ref[...] = _rms_fwd_val(x1, gn_ref[...]).astype(BF16)

    full = lambda a: pl.BlockSpec(a.shape, _fixed)
    return pl.pallas_call(
        body, name=name, grid=(S // T,),
        in_specs=[pl.BlockSpec((T, D), _row), pl.BlockSpec((T, CW), _row), pl.BlockSpec((T, GW), _row),
                  pl.BlockSpec((T, MW), _row), _gate_spec(T)]
        + [full(wco), full(wao), full(wmo), full(wout), full(bgate), full(gpost), full(gnext)],
        out_specs=[pl.BlockSpec((T, D), _row)] * 4,
        out_shape=[jax.ShapeDtypeStruct((S, D), F32), jax.ShapeDtypeStruct((S, D), BF16),
                   jax.ShapeDtypeStruct((S, D), F32), jax.ShapeDtypeStruct((S, D), BF16)],
        compiler_params=_params(("parallel",)),
    )(x, cact, oatt, om, z, wco, wao, wmo, wout, bgate, gpost, gnext)


def _merge_bwd(dx1, t, mg, cact, oatt, om, z, wco, wao, wmo, wout, bgate, gpost, name):
    S = dx1.shape[0]
    T = MERGE_T

    def body(dx_ref, t_ref, mg_ref, ca_ref, oa_ref, om_ref, zg_ref, wco_ref, wao_ref, wmo_ref, wout_ref,
             bg_ref, gp_ref, dzg_ref, dca_ref, doa_ref, dom_ref, dwco_ref, dwao_ref, dwmo_ref, dwout_ref,
             dbg_ref, dgp_ref):
        accs = (dwco_ref, dwao_ref, dwmo_ref, dwout_ref, dbg_ref, dgp_ref)

        @pl.when(pl.program_id(0) == 0)
        def _():
            for a in accs:
                a[...] = jnp.zeros_like(a)

        dt, dgr = _rms_bwd_val(t_ref[...], gp_ref[...], dx_ref[...])
        dgp_ref[...] += jnp.sum(dgr, axis=0, keepdims=True)
        dtb = dt.astype(BF16)
        dwout_ref[...] += _dot_tn(mg_ref[...], dtb)
        dm = _dot_nt(dtb, wout_ref[...])
        ys, gs = _branches(ca_ref, oa_ref, om_ref, wco_ref, wao_ref, wmo_ref, zg_ref, bg_ref)
        for b, (act_ref, w_ref, dw_ref, da_ref) in enumerate(
                ((ca_ref, wco_ref, dwco_ref, dca_ref), (oa_ref, wao_ref, dwao_ref, doa_ref),
                 (om_ref, wmo_ref, dwmo_ref, dom_ref))):
            dzg = dm * ys[b] * gs[b] * (1.0 - gs[b])
            dzg_ref[:, b * D:(b + 1) * D] = dzg.astype(BF16)
            dbg_ref[:, b * D:(b + 1) * D] += jnp.sum(dzg, axis=0, keepdims=True)
            dy = (dm * gs[b]).astype(BF16)
            dw_ref[...] += _dot_tn(act_ref[...], dy)
            da_ref[...] = _dot_nt(dy, w_ref[...]).astype(BF16)

    full = lambda a: pl.BlockSpec(a.shape, _fixed)
    fullf = lambda a: jax.ShapeDtypeStruct(a.shape, F32)
    return pl.pallas_call(
        body, name=name, grid=(S // T,),
        in_specs=[pl.BlockSpec((T, D), _row), pl.BlockSpec((T, D), _row), pl.BlockSpec((T, D), _row),
                  pl.BlockSpec((T, CW), _row), pl.BlockSpec((T, GW), _row), pl.BlockSpec((T, MW), _row)]
        + [_gate_spec(T), full(wco), full(wao), full(wmo), full(wout), full(bgate), full(gpost)],
        out_specs=[_gate_spec(T), pl.BlockSpec((T, CW), _row), pl.BlockSpec((T, GW), _row),
                   pl.BlockSpec((T, MW), _row), full(wco), full(wao), full(wmo), full(wout), full(bgate), full(gpost)],
        out_shape=[jax.ShapeDtypeStruct((S, NIN), BF16), jax.ShapeDtypeStruct((S, CW), BF16),
                   jax.ShapeDtypeStruct((S, GW), BF16), jax.ShapeDtypeStruct((S, MW), BF16),
                   fullf(wco), fullf(wao), fullf(wmo), fullf(wout), fullf(bgate), fullf(gpost)],
        compiler_params=_params(("arbitrary",)),
    )(dx1, t, mg, cact, oatt, om, z, wco, wao, wmo, wout, bgate, gpost)


FFN_T = 256


def _ffn_fwd(x1, gu, wfo, gpost, gnext, name):
    S = x1.shape[0]
    T = FFN_T

    nxt = gnext is not None

    def body(x_ref, gu_ref, w_ref, gp_ref, *rest):
        x2_ref, f_ref = rest[nxt:nxt + 2]
        gv = gu_ref[:, :FH].astype(F32)
        uv = gu_ref[:, FH:].astype(F32)
        act = (gv * _sigmoid(gv) * uv).astype(BF16)
        f = _dot(act, w_ref[...])
        f_ref[...] = f
        x2 = x_ref[...] + _rms_fwd_val(f, gp_ref[...])
        x2_ref[...] = x2
        if nxt:
            rest[3][...] = _rms_fwd_val(x2, rest[0][...]).astype(BF16)

    return pl.pallas_call(
        body, name=name, grid=(S // T,),
        in_specs=[pl.BlockSpec((T, D), _row), pl.BlockSpec((T, 2 * FH), _row), pl.BlockSpec((FH, D), _fixed),
                  pl.BlockSpec((1, D), _fixed)] + [pl.BlockSpec((1, D), _fixed)] * nxt,
        out_specs=[pl.BlockSpec((T, D), _row)] * (2 + nxt),
        out_shape=[jax.ShapeDtypeStruct((S, D), F32)] * 2 + [jax.ShapeDtypeStruct((S, D), BF16)] * nxt,
        compiler_params=_params(("parallel",)),
    )(x1, gu, wfo, gpost, *([gnext] if nxt else []))


def _ffn_bwd(dx2, f, gu, wfo, gpost, name):
    S = dx2.shape[0]
    T = FFN_T

    def body(dx_ref, f_ref, gu_ref, w_ref, gp_ref, dgu_ref, df_ref, act_ref, dgp_ref):
        @pl.when(pl.program_id(0) == 0)
        def _():
            dgp_ref[...] = jnp.zeros_like(dgp_ref)

        df, dgr = _rms_bwd_val(f_ref[...], gp_ref[...], dx_ref[...])
        dgp_ref[...] += jnp.sum(dgr, axis=0, keepdims=True)
        dfb = df.astype(BF16)
        df_ref[...] = dfb
        dact = _dot_nt(dfb, w_ref[...])
        gv = gu_ref[:, :FH].astype(F32)
        uv = gu_ref[:, FH:].astype(F32)
        sg = _sigmoid(gv)
        silu = gv * sg
        act_ref[...] = (silu * uv).astype(BF16)
        dgu_ref[:, :FH] = (dact * uv * (sg * (1.0 + gv * (1.0 - sg)))).astype(BF16)
        dgu_ref[:, FH:] = (dact * silu).astype(BF16)

    return pl.pallas_call(
        body, name=name, grid=(S // T,),
        in_specs=[pl.BlockSpec((T, D), _row), pl.BlockSpec((T, D), _row), pl.BlockSpec((T, 2 * FH), _row),
                  pl.BlockSpec((FH, D), _fixed), pl.BlockSpec((1, D), _fixed)],
        out_specs=[pl.BlockSpec((T, 2 * FH), _row), pl.BlockSpec((T, D), _row), pl.BlockSpec((T, FH), _row),
                   pl.BlockSpec((1, D), _fixed)],
        out_shape=[jax.ShapeDtypeStruct((S, 2 * FH), BF16), jax.ShapeDtypeStruct((S, D), BF16),
                   jax.ShapeDtypeStruct((S, FH), BF16), jax.ShapeDtypeStruct((1, D), F32)],
        compiler_params=_params(("arbitrary",)),
    )(dx2, f, gu, wfo, gpost)


def _loss_head(y, target, name):
    S = y.shape[0]
    T = 512

    def body(y_ref, t_ref, dy_ref, l_ref):
        @pl.when(pl.program_id(0) == 0)
        def _():
            l_ref[...] = jnp.zeros_like(l_ref)

        e = y_ref[...] - t_ref[...]
        dy_ref[...] = e * (1.0 / D)
        l_ref[...] += (0.5 / D) * jnp.sum(jnp.sum(e * e, axis=1, keepdims=True), axis=0, keepdims=True)

    return pl.pallas_call(
        body, name=name, grid=(S // T,),
        in_specs=[pl.BlockSpec((T, D), _row)] * 2,
        out_specs=[pl.BlockSpec((T, D), _row), pl.BlockSpec((8, 128), _fixed)],
        out_shape=[jax.ShapeDtypeStruct((S, D), F32), jax.ShapeDtypeStruct((8, 128), F32)],
        compiler_params=_params(("arbitrary",)),
    )(y, target)


BIG = ("w_in", "w_conv_out", "w_att_out", "w_mem_kv", "w_mem_out", "w_out", "w_ffn_in", "w_ffn_out")
SMALL = ("rel_bias", "norm_mix_pre", "b_gate", "conv_dw_bias", "conv_ln_g", "conv_ln_b", "norm_mem",
         "norm_mix_post", "norm_ffn_pre", "norm_ffn_post")


def _layer_fwd(l, x, h, mem, w, rel_bias, gnext):
    tag = f"_l{l}"
    z = _mm_nn(h, w["w_in"], 512, BF16, "mm_in" + tag)
    yc, cact = _conv_fwd(z, w["conv_dw"], w["conv_dw_bias"], w["conv_ln_g"], w["conv_ln_b"], "conv_fwd" + tag)
    og, lg = zip(*[_att_fwd(z, rel_bias, g, f"att_fwd_g{g}" + tag) for g in range(3)])
    oatt, lse = _att_combine(og, lg, "att_combine" + tag)
    hm, kv = _memkv_fwd(mem, w["norm_mem"], w["w_mem_kv"], "memkv_fwd" + tag)
    om = _memattn_fwd(z, kv, "memattn_fwd" + tag)
    x1, mg, t, h2 = _merge_fwd(x, cact, oatt, om, z, w["w_conv_out"], w["w_att_out"], w["w_mem_out"], w["w_out"],
                               w["b_gate"], w["norm_mix_post"], w["norm_ffn_pre"], "merge_fwd" + tag)
    gu = _mm_nn(h2, w["w_ffn_in"], 512, BF16, "mm_ffn_in" + tag)
    x2, f, *hn = _ffn_fwd(x1, gu, w["w_ffn_out"], w["norm_ffn_post"], gnext, "ffn_fwd" + tag)
    saved = dict(x=x, h=h, z=z, yc=yc, cact=cact, oatt=oatt, lse=lse, hm=hm, kv=kv, om=om, x1=x1, mg=mg, t=t,
                 h2=h2, gu=gu, f=f)
    return x2, (hn[0] if hn else None), saved


def _layer_bwd(l, dx2, mem, w, rel_bias, s):
    tag = f"_l{l}"
    gr = {}
    dgu, df, act, gr["norm_ffn_post"] = _ffn_bwd(dx2, s["f"], s["gu"], w["w_ffn_out"], w["norm_ffn_post"], "ffn_bwd" + tag)
    gr["w_ffn_out"] = _mm_tn(act, df, 1024, 512, "dw_ffn_out" + tag)
    gr["w_ffn_in"] = _mm_tn(s["h2"], dgu, 2048, 1408, "dw_ffn_in" + tag)
    dx1, gr["norm_ffn_pre"] = _mm_nt_rms_bwd(dgu, w["w_ffn_in"], s["x1"], w["norm_ffn_pre"], dx2, 512, "dh_ffn" + tag)
    (dz, dcact, doatt, dom, gr["w_conv_out"], gr["w_att_out"], gr["w_mem_out"], gr["w_out"], gr["b_gate"],
     gr["norm_mix_post"]) = _merge_bwd(dx1, s["t"], s["mg"], s["cact"], s["oatt"], s["om"], s["z"], w["w_conv_out"],
                                       w["w_att_out"], w["w_mem_out"], w["w_out"], w["b_gate"], w["norm_mix_post"],
                                       "merge_bwd" + tag)
    dyc, gr["conv_ln_g"], gr["conv_ln_b"], gr["conv_dw_bias"] = _conv_bwd_ln(
        s["yc"], dcact, w["conv_ln_g"], w["conv_ln_b"], "conv_bwd_ln" + tag)
    dz, dwdw = _conv_bwd_dw(s["z"], dyc, w["conv_dw"], dz, "conv_bwd_dw" + tag)
    gr["conv_dw"] = dwdw[:KSIZE]
    ld = _att_prep(doatt, s["oatt"], s["lse"], "att_prep" + tag)
    drb = []
    for g in range(3):
        dz, db = _att_bwd(s["z"], rel_bias, doatt, ld, dz, g, f"att_bwd_g{g}" + tag)
        drb.append(db)
    dz, dkv = _memattn_bwd(s["z"], s["kv"], dom, dz, "memattn_bwd" + tag)
    gr["w_mem_kv"], gr["norm_mem"] = _memkv_bwd(mem, w["norm_mem"], s["hm"], w["w_mem_kv"], dkv, "memkv_bwd" + tag)
    gr["w_in"] = _mm_tn(s["h"], dz, 2048, 1152, "dw_in" + tag)
    dx, gr["norm_mix_pre"] = _mm_nt_rms_bwd(dz, w["w_in"], s["x"], w["norm_mix_pre"], dx1, 512, "dh_in" + tag)
    return dx, gr, drb


def _rel_bias_total(parts, name):
    def body(*refs):
        out_ref = refs[-1]
        acc = jnp.zeros((NUM_BUCKETS, 128), F32)
        for l in range(DEPTH):
            for g in range(3):
                v = refs[l * 3 + g][...]
                acc = acc + (v if g == 0 else pltpu.roll(v, HPG * g, axis=1))
        out_ref[...] = acc

    return pl.pallas_call(body, name=name, out_shape=jax.ShapeDtypeStruct((NUM_BUCKETS, 128), F32),
                          compiler_params=_params())(*[p for layer in parts for p in layer])


def _local_step(x, mem, target, rel_bias, layer_fns, gmix, on_grads=None):
    saved, layers = [], []
    h = _rms_h(x, gmix[0], "rms_mix_l0")
    for l in range(DEPTH):
        layers.append(layer_fns[l](x))
        x, h, s = _layer_fwd(l, x, h, mem, layers[l], rel_bias, gmix[l + 1] if l + 1 < DEPTH else None)
        saved.append(s)
    dy, lpart = _loss_head(x, target, "loss_head")
    grads = [None] * DEPTH
    drb = [None] * DEPTH
    for l in reversed(range(DEPTH)):
        dy, grads[l], drb[l] = _layer_bwd(l, dy, mem, layers[l], rel_bias, saved[l])
        if on_grads is not None and l > 0:
            below = dict(layers[l - 1])
            below["norm_ffn_post"] = below["norm_ffn_post"] + on_grads(l, grads[l])[0, 0]
            layers[l - 1] = below
    return lpart[0, 0], dy, grads, _rel_bias_total(drb, "rel_bias_total")


def _z_cols_from_ref(w):
    att = [w[..., R_ATT + (3 * j + g) * GW:R_ATT + (3 * j + g + 1) * GW] for g in range(3) for j in range(3)]
    return jnp.concatenate([w[..., R_GATE:], w[..., :C1], w[..., R_MEM:R_GATE]] + att, axis=-1)


def _ref_cols_from_z(w):
    att = [w[..., Z_ATT + (3 * g + j) * GW:Z_ATT + (3 * g + j + 1) * GW] for j in range(3) for g in range(3)]
    return jnp.concatenate([w[..., Z_CONV:Z_MEM]] + att + [w[..., Z_MEM:Z_ATT], w[..., Z_GATE:Z_CONV]], axis=-1)


N_CHIPS = 4
SHARD = {"w_in": ((D, NIN // 4), 1), "w_conv_out": ((CW, D // 4), 1), "w_att_out": ((GW, D // 4), 1),
         "w_mem_kv": ((D // 4, 2 * MW), 0), "w_mem_out": ((MW, D // 4), 1), "w_out": ((D // 4, D), 0),
         "w_ffn_in": ((D, 2 * FH // 4), 1), "w_ffn_out": ((FH // 4, D), 0)}
CDW_ROWS = 64
VEC_ROWS = (("norm_mix_pre", 1), ("b_gate", 3), ("conv_dw_bias", 1), ("conv_ln_g", 1), ("conv_ln_b", 1),
            ("norm_mem", 1), ("norm_mix_post", 1), ("norm_ffn_pre", 1), ("norm_ffn_post", 1))
VEC_LROWS = sum(r for _, r in VEC_ROWS)
REL_ROW = DEPTH * VEC_LROWS
CDW_ROW = REL_ROW + 1
CDW_GROWS = DEPTH * KSIZE * CW // D
SMALL_ROWS = -(-(CDW_ROW + CDW_GROWS) // 8) * 8


def _mesh_pos():
    return lax.axis_index("x"), lax.axis_index("y"), lax.axis_index("c")


def _other_chips(x, y):
    chips = [(1 - x, y), (x, 1 - y), (1 - x, 1 - y)]
    return chips, [2 * cx + cy for cx, cy in chips]


NBIG = len(BIG)
ANY_SPEC = pl.BlockSpec(memory_space=pl.ANY)


def _remote(src, dst, send_sems, recv_sems, k, to):
    return pltpu.make_async_remote_copy(src_ref=src, dst_ref=dst, send_sem=send_sems.at[k], recv_sem=recv_sems.at[k],
                                        device_id=to, device_id_type=MESH)


def _half(ref, c):
    h = ref.shape[0] // 2
    return ref.at[pl.ds(c * h if isinstance(c, int) else pl.multiple_of(c * h, 16), h)]


def _all_gather(ws, cdw):
    def body(*refs):
        w_refs, cdw_ref = refs[:NBIG], refs[NBIG]
        g_refs, gc_ref = refs[NBIG + 1:2 * NBIG + 1], refs[2 * NBIG + 1]
        send_sems, recv_sems = refs[2 * NBIG + 2:]
        x, y, c = _mesh_pos()
        j = 2 * x + y
        sibling = (x, y, 1 - c)
        chips, blocks = _other_chips(x, y)
        copy = functools.partial(_remote, send_sems=send_sems, recv_sems=recv_sems)
        pairs = list(zip(w_refs, g_refs))
        first = [copy(_half(w, c), _half(g.at[j], c), k=k * NBIG + n, to=(*chip, c))
                 for k, chip in enumerate(chips) for n, (w, g) in enumerate(pairs)]
        first += [copy(cdw_ref, gc_ref.at[j], k=6 * NBIG + k, to=(*chip, c)) for k, chip in enumerate(chips)]
        for cp in first:
            cp.start()
        passed = []
        for k, b in enumerate(blocks):
            for n, (w, g) in enumerate(pairs):
                copy(_half(w, c), _half(g.at[b], c), k=k * NBIG + n, to=sibling).wait_recv()
            onward = [copy(_half(g.at[b], c), _half(g.at[b], c), k=(3 + k) * NBIG + n, to=sibling)
                      for n, (w, g) in enumerate(pairs)]
            for cp in onward:
                cp.start()
            passed += onward
        for k, b in enumerate(blocks):
            for n, (w, g) in enumerate(pairs):
                copy(_half(w, c), _half(g.at[b], 1 - c), k=(3 + k) * NBIG + n, to=sibling).wait_recv()
            copy(cdw_ref, gc_ref.at[b], k=6 * NBIG + k, to=sibling).wait_recv()
        for cp in first + passed:
            cp.wait_send()

    nsem = 6 * NBIG + 3
    return pl.pallas_call(
        body, name="all_gather_weights",
        out_shape=[jax.ShapeDtypeStruct((N_CHIPS,) + w.shape, BF16) for w in ws]
        + [jax.ShapeDtypeStruct((N_CHIPS, CDW_ROWS, 128), F32)],
        in_specs=[ANY_SPEC] * (NBIG + 1), out_specs=[ANY_SPEC] * (NBIG + 1),
        scratch_shapes=[pltpu.SemaphoreType.DMA((nsem,)), pltpu.SemaphoreType.DMA((nsem,))],
    )(*ws, cdw)


SEM_SPEC = pl.BlockSpec(memory_space=pltpu.SEMAPHORE)
DATAFLOW = pltpu.SideEffectType.DATAFLOW_SIDE_EFFECTING


def _gather_copies(w_refs, g_refs, send_sem, recv_sem):
    x, y, c = _mesh_pos()
    j = 2 * x + y
    chips, _ = _other_chips(x, y)
    return [pltpu.make_async_remote_copy(src_ref=_half(w, c), dst_ref=_half(g.at[j], c), send_sem=send_sem,
                                         recv_sem=recv_sem, device_id=(*chip, cc), device_id_type=MESH)
            for chip in chips for cc in (0, 1) for w, g in zip(w_refs, g_refs)]


def _all_gather_start(ws, after):
    def body(*refs):
        w_refs, g_refs = refs[:NBIG], refs[NBIG:2 * NBIG]
        send_sem, recv_sem = refs[2 * NBIG + 1:2 * NBIG + 3]
        token = refs[-1]
        for cp in _gather_copies(w_refs, g_refs, send_sem, recv_sem):
            cp.start()
        token[...] = jnp.zeros_like(token)

    lands = [pltpu.with_memory_space_constraint(lax.empty((N_CHIPS,) + w.shape, BF16), pltpu.HBM) for w in ws]
    ws = [pltpu.with_memory_space_constraint(w, pltpu.HBM) for w in ws]
    hbm = pl.BlockSpec(memory_space=pltpu.HBM)
    out = pl.pallas_call(
        body, name="all_gather_start",
        out_shape=[pltpu.SemaphoreType.DMA(()), pltpu.SemaphoreType.DMA(())]
        + [pltpu.HBM(w.shape, BF16) for w in ws] + [pltpu.HBM(g.shape, BF16) for g in lands]
        + [jax.ShapeDtypeStruct((8, 128), F32)],
        in_specs=[hbm] * (2 * NBIG) + [ANY_SPEC],
        out_specs=[SEM_SPEC, SEM_SPEC] + [hbm] * (2 * NBIG) + [pl.BlockSpec(memory_space=pltpu.VMEM)],
        input_output_aliases={n: 2 + n for n in range(2 * NBIG)},
        compiler_params=pltpu.CompilerParams(has_side_effects=DATAFLOW),
    )(*ws, *lands, after)
    return out[0], out[1], out[2:2 + NBIG], out[2 + NBIG:2 + 2 * NBIG], out[-1]


def _all_gather_wait(send_sem, recv_sem, ws, lands, after):
    def body(*refs):
        w_refs, g_refs = refs[:NBIG], refs[NBIG:2 * NBIG]
        send_sem, recv_sem = refs[2 * NBIG:2 * NBIG + 2]
        x, y, c = _mesh_pos()
        _, blocks = _other_chips(x, y)
        for cp in _gather_copies(w_refs, g_refs, send_sem, recv_sem):
            cp.wait_send()
        for b in blocks:
            for cc in (0, 1):
                for w, g in zip(w_refs, g_refs):
                    pltpu.make_async_remote_copy(src_ref=_half(w, cc), dst_ref=_half(g.at[b], cc), send_sem=send_sem,
                                                 recv_sem=recv_sem, device_id=(x, y, c),
                                                 device_id_type=MESH).wait_recv()

    hbm = pl.BlockSpec(memory_space=pltpu.HBM)
    out = pl.pallas_call(
        body, name="all_gather_wait",
        out_shape=[pltpu.HBM(w.shape, BF16) for w in ws] + [pltpu.HBM(g.shape, BF16) for g in lands],
        in_specs=[hbm] * (2 * NBIG) + [SEM_SPEC, SEM_SPEC, ANY_SPEC],
        out_specs=[hbm] * (2 * NBIG),
        input_output_aliases={n: n for n in range(2 * NBIG)},
        compiler_params=pltpu.CompilerParams(has_side_effects=DATAFLOW),
    )(*ws, *lands, send_sem, recv_sem, after)
    return out[:NBIG], out[NBIG:]


def _half_rows(ref, c):
    h = ref.shape[1] // 2
    return ref.at[:, pl.ds(pl.multiple_of(c * h, 16), h)]


def _sibling_exchange(ps):
    def body(*refs):
        p_refs, r_refs, (send_sems, recv_sems) = refs[:NBIG], refs[NBIG:2 * NBIG], refs[2 * NBIG:]
        x, y, c = _mesh_pos()
        cps = [_remote(_half_rows(p, 1 - c), r, send_sems, recv_sems, n, (x, y, 1 - c))
               for n, (p, r) in enumerate(zip(p_refs, r_refs))]
        for cp in cps:
            cp.start()
        for cp in cps:
            cp.wait()

    return pl.pallas_call(
        body, name="grad_sibling_exchange",
        out_shape=[jax.ShapeDtypeStruct((N_CHIPS, p.shape[1] // 2, p.shape[2]), p.dtype) for p in ps],
        in_specs=[ANY_SPEC] * NBIG, out_specs=[ANY_SPEC] * NBIG,
        scratch_shapes=[pltpu.SemaphoreType.DMA((NBIG,)), pltpu.SemaphoreType.DMA((NBIG,))],
    )(*ps)


SUM_BLOCK_BYTES = 2 * 1024 * 1024


def _sum_rows(s0, s1):
    return s0 if s0 * s1 * 2 <= SUM_BLOCK_BYTES else s0 // 2


def _add_own_half(where, p, r, name):
    _, h, s1 = r.shape
    T = _sum_rows(h, s1)
    nt = h // T

    def body(where_ref, p_ref, r_ref, o_ref):
        o_ref[...] = (p_ref[...].astype(F32) + r_ref[...].astype(F32)).astype(BF16)

    return pl.pallas_call(
        body, name=name,
        grid_spec=pltpu.PrefetchScalarGridSpec(
            num_scalar_prefetch=1, grid=(N_CHIPS, nt),
            in_specs=[pl.BlockSpec((1, T, s1), lambda j, i, wh: (j, wh[0] * nt + i, 0)),
                      pl.BlockSpec((1, T, s1), lambda j, i, wh: (j, i, 0))],
            out_specs=pl.BlockSpec((1, T, s1), lambda j, i, wh: (j, i, 0))),
        out_shape=jax.ShapeDtypeStruct(r.shape, BF16), compiler_params=_params(("parallel", "parallel")),
    )(where, p, r)


def _chip_exchange(as_):
    def body(*refs):
        a_refs, r_refs, (send_sems, recv_sems) = refs[:NBIG], refs[NBIG:2 * NBIG], refs[2 * NBIG:]
        x, y, c = _mesh_pos()
        chips, blocks = _other_chips(x, y)
        cps = [_remote(a.at[b], r.at[k], send_sems, recv_sems, k * NBIG + n, (*chip, c))
               for k, (chip, b) in enumerate(zip(chips, blocks)) for n, (a, r) in enumerate(zip(a_refs, r_refs))]
        for cp in cps:
            cp.start()
        for cp in cps:
            cp.wait_recv()
        for cp in cps:
            cp.wait_send()

    return pl.pallas_call(
        body, name="grad_chip_exchange", out_shape=[jax.ShapeDtypeStruct((3,) + a.shape[1:], a.dtype) for a in as_],
        in_specs=[ANY_SPEC] * NBIG, out_specs=[ANY_SPEC] * NBIG,
        scratch_shapes=[pltpu.SemaphoreType.DMA((3 * NBIG,)), pltpu.SemaphoreType.DMA((3 * NBIG,))],
    )(*as_)


def _sum_chips(where, a, r, o, name):
    _, h, s1 = a.shape
    T = _sum_rows(h, s1)
    nt = h // T

    def body(where_ref, a_ref, r_ref, o_in, o_ref):
        acc = a_ref[0].astype(F32)
        for k in range(3):
            acc = acc + r_ref[k].astype(F32)
        o_ref[0] = acc

    return pl.pallas_call(
        body, name=name,
        grid_spec=pltpu.PrefetchScalarGridSpec(
            num_scalar_prefetch=1, grid=(nt,),
            in_specs=[pl.BlockSpec((1, T, s1), lambda i, wh: (wh[1], i, 0)),
                      pl.BlockSpec((3, T, s1), lambda i, wh: (0, i, 0)), ANY_SPEC],
            out_specs=pl.BlockSpec((1, T, s1), lambda i, wh: (0, wh[0] * nt + i, 0))),
        out_shape=jax.ShapeDtypeStruct(o.shape, F32), input_output_aliases={3: 0},
        compiler_params=_params(("parallel",)),
    )(where, a, r, o)


def _sibling_share(os_):
    def body(*refs):
        o_refs, (send_sems, recv_sems) = refs[NBIG:2 * NBIG], refs[2 * NBIG:]
        x, y, c = _mesh_pos()
        mine = lambda o, cc: _half(o.at[0], cc)
        cps = [_remote(mine(o, c), mine(o, c), send_sems, recv_sems, n, (x, y, 1 - c)) for n, o in enumerate(o_refs)]
        for cp in cps:
            cp.start()
        for n, o in enumerate(o_refs):
            _remote(mine(o, c), mine(o, 1 - c), send_sems, recv_sems, n, (x, y, 1 - c)).wait_recv()
        for cp in cps:
            cp.wait_send()

    return pl.pallas_call(
        body, name="grad_sibling_share", out_shape=[jax.ShapeDtypeStruct(o.shape, o.dtype) for o in os_],
        in_specs=[ANY_SPEC] * NBIG, out_specs=[ANY_SPEC] * NBIG,
        input_output_aliases={n: n for n in range(NBIG)},
        scratch_shapes=[pltpu.SemaphoreType.DMA((NBIG,)), pltpu.SemaphoreType.DMA((NBIG,))],
    )(*os_)


N_DEV = 8


def _scatter_copies(p_refs, r_refs, send_sem, recv_sem):
    x, y, c = _mesh_pos()
    cps = []
    for m in range(1, N_DEV):
        px, py, pc = x ^ (m >> 2 & 1), y ^ (m >> 1 & 1), c ^ (m & 1)
        for p, r in zip(p_refs, r_refs):
            cps.append(pltpu.make_async_remote_copy(src_ref=p.at[2 * px + py], dst_ref=r.at[m - 1], send_sem=send_sem,
                                                    recv_sem=recv_sem, device_id=(px, py, pc), device_id_type=MESH))
    return cps


def _reduce_start(ps, after):
    def body(*refs):
        p_refs, r_refs = refs[:NBIG], refs[NBIG:2 * NBIG]
        send_sem, recv_sem = refs[2 * NBIG + 1:2 * NBIG + 3]
        for cp in _scatter_copies(p_refs, r_refs, send_sem, recv_sem):
            cp.start()
        refs[-1][...] = jnp.zeros_like(refs[-1])

    lands = [pltpu.with_memory_space_constraint(lax.empty((N_DEV - 1,) + p.shape[1:], BF16), pltpu.HBM) for p in ps]
    ps = [pltpu.with_memory_space_constraint(p, pltpu.HBM) for p in ps]
    hbm = pl.BlockSpec(memory_space=pltpu.HBM)
    out = pl.pallas_call(
        body, name="grad_reduce_start",
        out_shape=[pltpu.SemaphoreType.DMA(()), pltpu.SemaphoreType.DMA(())]
        + [pltpu.HBM(p.shape, BF16) for p in ps] + [pltpu.HBM(r.shape, BF16) for r in lands]
        + [jax.ShapeDtypeStruct((8, 128), F32)],
        in_specs=[hbm] * (2 * NBIG) + [ANY_SPEC],
        out_specs=[SEM_SPEC, SEM_SPEC] + [hbm] * (2 * NBIG) + [pl.BlockSpec(memory_space=pltpu.VMEM)],
        input_output_aliases={n: 2 + n for n in range(2 * NBIG)},
        compiler_params=pltpu.CompilerParams(has_side_effects=DATAFLOW),
    )(*ps, *lands, after)
    return out[0], out[1], out[2:2 + NBIG], out[2 + NBIG:2 + 2 * NBIG], out[-1]


def _reduce_wait(send_sem, recv_sem, ps, lands, after):
    def body(*refs):
        p_refs, r_refs = refs[:NBIG], refs[NBIG:2 * NBIG]
        send_sem, recv_sem = refs[2 * NBIG:2 * NBIG + 2]
        x, y, c = _mesh_pos()
        for cp in _scatter_copies(p_refs, r_refs, send_sem, recv_sem):
            cp.wait_send()
        for m in range(1, N_DEV):
            for p, r in zip(p_refs, r_refs):
                pltpu.make_async_remote_copy(src_ref=p.at[0], dst_ref=r.at[m - 1], send_sem=send_sem,
                                             recv_sem=recv_sem, device_id=(x, y, c), device_id_type=MESH).wait_recv()

    hbm = pl.BlockSpec(memory_space=pltpu.HBM)
    out = pl.pallas_call(
        body, name="grad_reduce_wait",
        out_shape=[pltpu.HBM(p.shape, BF16) for p in ps] + [pltpu.HBM(r.shape, BF16) for r in lands],
        in_specs=[hbm] * (2 * NBIG) + [SEM_SPEC, SEM_SPEC, ANY_SPEC],
        out_specs=[hbm] * (2 * NBIG),
        input_output_aliases={n: n for n in range(2 * NBIG)},
        compiler_params=pltpu.CompilerParams(has_side_effects=DATAFLOW),
    )(*ps, *lands, send_sem, recv_sem, after)
    return out[:NBIG], out[NBIG:]


SUM8_BLOCK_BYTES = 6 * 1024 * 1024


def _sum_devices(where, p, r, name):
    _, s0, s1 = p.shape
    T = s0
    while (N_DEV - 1) * T * s1 * 2 > SUM8_BLOCK_BYTES:
        T //= 2

    def body(where_ref, p_ref, r_ref, o_ref):
        me = 2 * where_ref[1] + where_ref[0]
        acc = None
        for dev in range(N_DEV):
            m = dev ^ me
            val = jnp.where(m == 0, p_ref[0], r_ref[jnp.maximum(m, 1) - 1]).astype(F32)
            acc = val if acc is None else acc + val
        o_ref[0] = acc

    return pl.pallas_call(
        body, name=name,
        grid_spec=pltpu.PrefetchScalarGridSpec(
            num_scalar_prefetch=1, grid=(s0 // T,),
            in_specs=[pl.BlockSpec((1, T, s1), lambda i, wh: (wh[1], i, 0)),
                      pl.BlockSpec((N_DEV - 1, T, s1), lambda i, wh: (0, i, 0))],
            out_specs=pl.BlockSpec((1, T, s1), lambda i, wh: (1, i, 0))),
        out_shape=jax.ShapeDtypeStruct((DEPTH, s0, s1), F32), compiler_params=_params(("parallel",)),
    )(where, p, r)


def _all_reduce_small(sp):
    def body(sp_ref, out_ref, buf, send_sems, recv_sems):
        x, y, c = _mesh_pos()
        me = 4 * x + 2 * y + c
        buf[0] = sp_ref[...]
        cps = []
        for k in range(1, 8):
            peer = (x ^ (k >> 2 & 1), y ^ (k >> 1 & 1), c ^ (k & 1))
            cps.append(pltpu.make_async_remote_copy(src_ref=sp_ref, dst_ref=buf.at[k], send_sem=send_sems.at[k - 1],
                                                    recv_sem=recv_sems.at[k - 1], device_id=peer, device_id_type=MESH))
        for cp in cps:
            cp.start()
        for cp in cps:
            cp.wait_recv()
        for cp in cps:
            cp.wait_send()
        acc = buf[me]
        for p in range(1, 8):
            acc = acc + buf[p ^ me]
        out_ref[...] = acc

    vm = pl.BlockSpec(memory_space=pltpu.VMEM)
    return pl.pallas_call(
        body, name="all_reduce_small", out_shape=jax.ShapeDtypeStruct(sp.shape, F32),
        in_specs=[vm], out_specs=vm,
        scratch_shapes=[pltpu.VMEM((8,) + sp.shape, F32), pltpu.SemaphoreType.DMA((7,)), pltpu.SemaphoreType.DMA((7,))],
        compiler_params=_params(),
    )(sp)


def _adamw(w, g, m, v, name):
    R, C = w.shape
    T = next((t for t in (256, 128) if R % t == 0), R)

    def body(w_ref, g_ref, m_ref, v_ref, d_ref, m2_ref, v2_ref):
        gv = g_ref[...]
        m2 = ADAM_B1 * m_ref[...] + (1.0 - ADAM_B1) * gv
        v2 = ADAM_B2 * v_ref[...] + (1.0 - ADAM_B2) * (gv * gv)
        m_hat = m2 / (1.0 - ADAM_B1 ** ADAM_STEP)
        v_hat = v2 / (1.0 - ADAM_B2 ** ADAM_STEP)
        d_ref[...] = -ADAM_LR * (m_hat / (jnp.sqrt(v_hat) + ADAM_EPS) + ADAM_WD * w_ref[...])
        m2_ref[...] = m2
        v2_ref[...] = v2

    blk = pl.BlockSpec((T, C), _row)
    return pl.pallas_call(
        body, name=name, grid=(R // T,), in_specs=[blk] * 4, out_specs=[blk] * 3,
        out_shape=[jax.ShapeDtypeStruct((R, C), F32)] * 3, compiler_params=_params(("parallel",)),
    )(w, g, m, v)


def _pack_vectors(get, rel, cdw, name):
    rows = []
    for l in range(DEPTH):
        for n, r in VEC_ROWS:
            v = get(n)[l]
            rows.append(jnp.pad(v, (0, r * D - v.shape[0])).reshape(r, D))
    rows.append(jnp.pad(rel.reshape(-1), (0, D - NUM_BUCKETS * 3 * HPG)).reshape(1, D))
    rows.append(cdw.reshape(CDW_GROWS, D))

    def body(*refs):
        out_ref = refs[-1]
        out_ref[...] = jnp.zeros_like(out_ref)
        at = 0
        for ref in refs[:-1]:
            out_ref[at:at + ref.shape[0], :] = ref[...]
            at += ref.shape[0]

    return pl.pallas_call(body, name=name, out_shape=jax.ShapeDtypeStruct((SMALL_ROWS, D), F32),
                          compiler_params=_params())(*rows)


def _unpack_vectors(packed, lens):
    out = {n: [] for n, _ in VEC_ROWS}
    for l in range(DEPTH):
        at = l * VEC_LROWS
        for n, r in VEC_ROWS:
            out[n].append(packed[at:at + r].reshape(-1)[:lens[n]])
            at += r
    rel = packed[REL_ROW, :NUM_BUCKETS * 3 * HPG].reshape(NUM_BUCKETS, 3 * HPG)
    return {n: jnp.stack(v) for n, v in out.items()}, rel


INPUT_NAMES = ("x", "mem") + ("rel_bias", "norm_mix_pre", "w_in", "b_gate", "conv_dw", "conv_dw_bias", "conv_ln_g",
                              "conv_ln_b", "w_conv_out", "w_att_out", "norm_mem", "w_mem_kv", "w_mem_out", "w_out",
                              "norm_mix_post", "norm_ffn_pre", "w_ffn_in", "w_ffn_out", "norm_ffn_post")
WEIGHT_NAMES = INPUT_NAMES[2:]


def kernel(*args):
    nw = len(WEIGHT_NAMES)
    a = dict(zip(INPUT_NAMES, args[:2 + nw]))
    target = args[2 + nw]
    mom = dict(zip(WEIGHT_NAMES, args[3 + nw:3 + 2 * nw]))
    var = dict(zip(WEIGHT_NAMES, args[3 + 2 * nw:3 + 3 * nw]))
    xi, yi, ci = _mesh_pos()
    chip = 2 * xi + yi
    where = jnp.stack([ci, chip]).astype(I32)

    shards = [[a[n][l].astype(BF16) for n in BIG] for l in range(DEPTH)]
    cdw = jnp.pad(a["conv_dw"].reshape(DEPTH * KSIZE, CW // 4), ((0, CDW_ROWS - DEPTH * KSIZE), (0, 0)))
    *gathered0, gcdw = _all_gather(shards[0], cdw)
    in_flight = _all_gather_start(shards[1], gathered0[0])
    gcdw = lax.dynamic_update_slice(gcdw, cdw[None], (chip, 0, 0))
    conv_dw = gcdw[:, :DEPTH * KSIZE].reshape(N_CHIPS, DEPTH, KSIZE, CW // 4).transpose(1, 2, 0, 3)
    conv_dw = jnp.pad(conv_dw.reshape(DEPTH, KSIZE, CW), ((0, 0), (0, 1), (0, 0)))
    gmix = [a["norm_mix_pre"][l][None, :] for l in range(DEPTH)]
    gmix[0] = gmix[0] + in_flight[4][0, 0]

    def layer_weights(l, gathered, own):
        w = {"conv_dw": conv_dw[l]}
        for n, g, s in zip(BIG, gathered, own):
            (s0, s1), axis = SHARD[n]
            blk = lax.dynamic_update_slice(g, s[None], (chip, 0, 0))
            w[n] = blk.reshape(N_CHIPS * s0, s1) if axis == 0 else blk.transpose(1, 0, 2).reshape(s0, N_CHIPS * s1)
        w["w_in"] = _z_cols_from_ref(w["w_in"])
        for n, _ in VEC_ROWS:
            w[n] = a[n][l][None, :]
        return w

    def layer1(x):
        send_sem, recv_sem, thru, lands, _ = in_flight
        own, gathered1 = _all_gather_wait(send_sem, recv_sem, thru, lands, x)
        return layer_weights(1, gathered1, own)

    def by_chip(layer_grads):
        out = []
        for n in BIG:
            (s0, s1), axis = SHARD[n]
            g = _ref_cols_from_z(layer_grads[n]) if n == "w_in" else layer_grads[n]
            g = g.reshape(N_CHIPS, s0, s1) if axis == 0 else g.reshape(s0, N_CHIPS, s1).transpose(1, 0, 2)
            out.append(g.astype(BF16))
        return out

    scattering = []

    def on_grads(l, layer_grads):
        scattering.append(_reduce_start(by_chip(layer_grads), layer_grads["w_in"]))
        return scattering[0][4]

    loss_part, gx, grads, drel = _local_step(a["x"][0], a["mem"][0], target[0], a["rel_bias"],
                                             [lambda x: layer_weights(0, gathered0, shards[0]), layer1], gmix, on_grads)
    loss = lax.psum(loss_part, ("x", "y", "c"))

    send_sem, recv_sem, thru, lands, _ = scattering[0]
    sent, landed = _reduce_wait(send_sem, recv_sem, thru, lands, gx)
    reduced = [_sum_devices(where, p, r, "grad_sum_devices_" + n) for n, p, r in zip(BIG, sent, landed)]
    packed = by_chip(grads[0])
    from_sibling = _sibling_exchange(packed)
    chip_sums = [_add_own_half(where, p, r, "grad_add_sibling_" + n) for n, p, r in zip(BIG, packed, from_sibling)]
    from_chips = _chip_exchange(chip_sums)
    reduced = _sibling_share([_sum_chips(where, s, r, o, "grad_sum_chips_" + n)
                              for n, s, r, o in zip(BIG, chip_sums, from_chips, reduced)])

    gvec = _all_reduce_small(_pack_vectors(
        lambda n: jnp.stack([grads[l][n][0] for l in range(DEPTH)]), drel[:, :3 * HPG],
        jnp.stack([grads[l]["conv_dw"] for l in range(DEPTH)]), "pack_vector_grads"))
    lens = {n: a[n].shape[1] for n, _ in VEC_ROWS}
    g_vec, g_rel = _unpack_vectors(gvec, lens)
    g_cdw = lax.dynamic_slice_in_dim(gvec[CDW_ROW:CDW_ROW + CDW_GROWS].reshape(DEPTH, KSIZE, CW), chip * (CW // 4),
                                     CW // 4, axis=2)

    grad, delta, new_m, new_v = {}, {}, {}, {}
    for n, g in zip(BIG, reduced):
        shape = a[n].shape
        flat2 = lambda t: t.reshape(shape[0] * shape[1], shape[2])
        d, m2, v2 = _adamw(flat2(a[n]), flat2(g), flat2(mom[n]), flat2(var[n]), "adamw_" + n)
        grad[n], delta[n], new_m[n], new_v[n] = g, d.reshape(shape), m2.reshape(shape), v2.reshape(shape)
    shape = a["conv_dw"].shape
    flat2 = lambda t: t.reshape(shape[0] * shape[1], shape[2])
    d, m2, v2 = _adamw(flat2(a["conv_dw"]), flat2(g_cdw), flat2(mom["conv_dw"]), flat2(var["conv_dw"]), "adamw_conv_dw")
    grad["conv_dw"], delta["conv_dw"], new_m["conv_dw"], new_v["conv_dw"] = (
        g_cdw, d.reshape(shape), m2.reshape(shape), v2.reshape(shape))
    zero_cdw = jnp.zeros((DEPTH, KSIZE, CW), F32)
    pk = lambda src, name: _pack_vectors(lambda n: src[n], src["rel_bias"], zero_cdw, name)
    d, m2, v2 = _adamw(pk(a, "pack_vector_w"), gvec, pk(mom, "pack_vector_m"), pk(var, "pack_vector_v"),
                       "adamw_vectors")
    for src, dst in ((d, delta), (m2, new_m), (v2, new_v)):
        vec, rel = _unpack_vectors(src, lens)
        dst.update(vec)
        dst["rel_bias"] = rel
    grad.update(g_vec)
    grad["rel_bias"] = g_rel

    outs = [loss, gx[None]]
    for group in (grad, delta, new_m, new_v):
        outs += [group[n] for n in WEIGHT_NAMES]
    return tuple(outs)
```

```python
import functools
import math

import jax
import jax.numpy as jnp
from jax import lax
from jax.experimental import pallas as pl
from jax.experimental.pallas import tpu as pltpu

F32 = jnp.float32
BF16 = jnp.bfloat16
I32 = jnp.int32

D = 1024
DEPTH = 2
N_MEM = 256
CW = 512
KSIZE = 31
PAD = KSIZE // 2
DILS = (1, 4, 16)
RADIUS = 64
HPG = 4
HD = 64
GW = HPG * HD
MH = 4
MHD = 128
MW = MH * MHD
FH = 2816
NIN = 6912
C1 = 2 * CW
R_ATT = C1
R_MEM = R_ATT + 9 * GW
R_GATE = R_MEM + MW
Z_GATE = 0
Z_CONV = 3 * D
Z_MEM = Z_CONV + C1
Z_ATT = Z_MEM + MW
NUM_BUCKETS = 32
MAX_DISTANCE = 1024
RMS_EPS = 1e-6
LN_EPS = 1e-5
NEG_INF = -1e30
ATT_SCALE = HD ** -0.5
MEM_SCALE = MHD ** -0.5

ADAM_LR = 0.001
ADAM_B1 = 0.9
ADAM_B2 = 0.999
ADAM_EPS = 1e-08
ADAM_WD = 0.01
ADAM_STEP = 10

VMEM_LIMIT_BYTES = 56 * 1024 * 1024
ATT_QB = 128
ATT_TB = 16 * ATT_QB

MESH = pl.DeviceIdType.MESH


def _params(sem=None):
    return pltpu.CompilerParams(dimension_semantics=sem, vmem_limit_bytes=VMEM_LIMIT_BYTES)


def _sigmoid(v):
    return 1.0 / (1.0 + jnp.exp(-v))


def _dot(a, b):
    return jnp.dot(a, b, preferred_element_type=F32)


def _dot_nt(a, b):
    return lax.dot_general(a, b, (((1,), (1,)), ((), ())), preferred_element_type=F32)


def _dot_tn(a, b):
    return lax.dot_general(a, b, (((0,), (0,)), ((), ())), preferred_element_type=F32)


def _rms_fwd_val(v, g):
    r = lax.rsqrt(jnp.mean(v * v, axis=-1, keepdims=True) + RMS_EPS)
    return v * r * g


def _rms_bwd_val(v, g, dy):
    r = lax.rsqrt(jnp.mean(v * v, axis=-1, keepdims=True) + RMS_EPS)
    vh = v * r
    dvh = dy * g
    dv = r * (dvh - vh * jnp.mean(dvh * vh, axis=-1, keepdims=True))
    return dv, dy * vh


def _row(i):
    return (i, 0)


def _fixed(*_):
    return (0, 0)


def _mm_nn(a, b, tm, out_dtype, name):
    M, K = a.shape
    N = b.shape[1]

    def body(a_ref, b_ref, o_ref):
        o_ref[...] = _dot(a_ref[...], b_ref[...]).astype(out_dtype)

    return pl.pallas_call(
        body, name=name, grid=(M // tm,),
        in_specs=[pl.BlockSpec((tm, K), _row), pl.BlockSpec((K, N), _fixed, pipeline_mode=pl.Buffered(1))],
        out_specs=pl.BlockSpec((tm, N), _row),
        out_shape=jax.ShapeDtypeStruct((M, N), out_dtype),
        compiler_params=_params(("parallel",)),
    )(a, b)


def _mm_nt_rms_bwd(a, b, x, g, dres, tm, name):
    M, N = a.shape

    def body(a_ref, b_ref, x_ref, g_ref, dres_ref, dx_ref, dg_ref):
        @pl.when(pl.program_id(0) == 0)
        def _():
            dg_ref[...] = jnp.zeros_like(dg_ref)

        dv, dgr = _rms_bwd_val(x_ref[...], g_ref[...], _dot_nt(a_ref[...], b_ref[...]))
        dx_ref[...] = dres_ref[...] + dv
        dg_ref[...] += jnp.sum(dgr, axis=0, keepdims=True)

    rows = pl.BlockSpec((tm, D), _row)
    return pl.pallas_call(
        body, name=name, grid=(M // tm,),
        in_specs=[pl.BlockSpec((tm, N), _row), pl.BlockSpec((D, N), _fixed, pipeline_mode=pl.Buffered(1)), rows,
                  pl.BlockSpec((1, D), _fixed), rows],
        out_specs=[rows, pl.BlockSpec((1, D), _fixed)],
        out_shape=[jax.ShapeDtypeStruct((M, D), F32), jax.ShapeDtypeStruct((1, D), F32)],
        compiler_params=_params(("arbitrary",)),
    )(a, b, x, g, dres)


def _mm_tn(a, b, ts, tn, name):
    S, K = a.shape
    N = b.shape[1]

    def body(a_ref, b_ref, o_ref):
        @pl.when(pl.program_id(1) == 0)
        def _():
            o_ref[...] = jnp.zeros_like(o_ref)

        o_ref[...] += _dot_tn(a_ref[...], b_ref[...])

    return pl.pallas_call(
        body, name=name, grid=(N // tn, S // ts),
        in_specs=[pl.BlockSpec((ts, K), lambda j, s: (s, 0)), pl.BlockSpec((ts, tn), lambda j, s: (s, j))],
        out_specs=pl.BlockSpec((K, tn), lambda j, s: (0, j)),
        out_shape=jax.ShapeDtypeStruct((K, N), F32),
        compiler_params=_params(("parallel", "arbitrary")),
    )(a, b)


def _rms_h(x, g, name):
    S = x.shape[0]
    T = 512

    def body(x_ref, g_ref, h_ref):
        h_ref[...] = _rms_fwd_val(x_ref[...], g_ref[...]).astype(BF16)

    return pl.pallas_call(
        body, name=name, grid=(S // T,),
        in_specs=[pl.BlockSpec((T, D), _row), pl.BlockSpec((1, D), _fixed)],
        out_specs=pl.BlockSpec((T, D), _row),
        out_shape=jax.ShapeDtypeStruct((S, D), BF16),
        compiler_params=_params(("parallel",)),
    )(x, g)


CONV_T = 256
CONV_HALO = 16
CONV_RC = 32


def _halo_specs(T, halo, S, width, col):
    per = T // halo
    last = S // halo - 1
    return [
        pl.BlockSpec((T, width), lambda i: (i, col)),
        pl.BlockSpec((halo, width), lambda i: (jnp.maximum(i * per - 1, 0), col)),
        pl.BlockSpec((halo, width), lambda i: (jnp.minimum((i + 1) * per, last), col)),
    ]


def _glu(zb):
    zb = zb.astype(F32)
    return zb[:, :CW] * _sigmoid(zb[:, CW:])


CONV_EXT = CONV_T + 2 * CONV_HALO
SUBLANES = 8


def _fill_shifted(sh_ref, ext_ref, cur, prev, nxt):
    T, halo = CONV_T, CONV_HALO
    i = pl.program_id(0)
    n = pl.num_programs(0)
    ext_ref[0:halo, :] = jnp.where(i > 0, prev, 0.0)
    ext_ref[halo:halo + T, :] = cur
    ext_ref[halo + T:CONV_EXT, :] = jnp.where(i < n - 1, nxt, 0.0)
    ext_ref[CONV_EXT:CONV_EXT + SUBLANES, :] = jnp.zeros((SUBLANES, CW), F32)
    for b in range(SUBLANES):
        sh_ref[b] = ext_ref[b:b + CONV_EXT, :]


def _window(sh_ref, start, rows):
    b = start % SUBLANES
    return sh_ref[b, start - b:start - b + rows, :]


def _shifted_scratch():
    return [pltpu.VMEM((CONV_EXT + SUBLANES, CW), F32), pltpu.VMEM((SUBLANES, CONV_EXT, CW), F32)]


def _conv_fwd(z, wdw, bdw, lng, lnb, name):
    S = z.shape[0]
    T, HL, RC = CONV_T, CONV_HALO, CONV_RC

    def body(cur_ref, prev_ref, next_ref, w_ref, b_ref, g_ref, bb_ref, yc_ref, act_ref, ext_ref, sh_ref):
        _fill_shifted(sh_ref, ext_ref, _glu(cur_ref[...]), _glu(prev_ref[...]), _glu(next_ref[...]))
        for c in range(T // RC):
            acc = jnp.zeros((RC, CW), F32)
            for k in range(KSIZE):
                acc = acc + w_ref[k:k + 1, :] * _window(sh_ref, c * RC + k + HL - PAD, RC)
            yc = acc + b_ref[...]
            yc_ref[c * RC:(c + 1) * RC, :] = yc
            mu = jnp.mean(yc, axis=-1, keepdims=True)
            xc = yc - mu
            ln = xc * lax.rsqrt(jnp.mean(xc * xc, axis=-1, keepdims=True) + LN_EPS) * g_ref[...] + bb_ref[...]
            act_ref[c * RC:(c + 1) * RC, :] = (ln * _sigmoid(ln)).astype(BF16)

    return pl.pallas_call(
        body, name=name, grid=(S // T,),
        in_specs=_halo_specs(T, HL, S, C1, Z_CONV // C1) + [pl.BlockSpec((32, CW), _fixed)]
        + [pl.BlockSpec((1, CW), _fixed)] * 3,
        out_specs=[pl.BlockSpec((T, CW), _row), pl.BlockSpec((T, CW), _row)],
        out_shape=[jax.ShapeDtypeStruct((S, CW), F32), jax.ShapeDtypeStruct((S, CW), BF16)],
        scratch_shapes=_shifted_scratch(),
        compiler_params=_params(("parallel",)),
    )(z, z, z, wdw, bdw, lng, lnb)


def _conv_bwd_ln(yc, dact, lng, lnb, name):
    S = yc.shape[0]
    T = 512

    def body(yc_ref, da_ref, g_ref, b_ref, dyc_ref, dg_ref, db_ref, dbias_ref):
        yc_v = yc_ref[...]
        mu = jnp.mean(yc_v, axis=-1, keepdims=True)
        xc = yc_v - mu
        r = lax.rsqrt(jnp.mean(xc * xc, axis=-1, keepdims=True) + LN_EPS)
        yn = xc * r
        ln = yn * g_ref[...] + b_ref[...]
        sg = _sigmoid(ln)
        dln = da_ref[...].astype(F32) * (sg * (1.0 + ln * (1.0 - sg)))
        dyn = dln * g_ref[...]
        dyc = r * (dyn - jnp.mean(dyn, axis=-1, keepdims=True) - yn * jnp.mean(dyn * yn, axis=-1, keepdims=True))
        dyc_ref[...] = dyc

        @pl.when(pl.program_id(0) == 0)
        def _():
            dg_ref[...] = jnp.zeros_like(dg_ref)
            db_ref[...] = jnp.zeros_like(db_ref)
            dbias_ref[...] = jnp.zeros_like(dbias_ref)

        dg_ref[...] += jnp.sum(dln * yn, axis=0, keepdims=True)
        db_ref[...] += jnp.sum(dln, axis=0, keepdims=True)
        dbias_ref[...] += jnp.sum(dyc, axis=0, keepdims=True)

    vec = pl.BlockSpec((1, CW), _fixed)
    return pl.pallas_call(
        body, name=name, grid=(S // T,),
        in_specs=[pl.BlockSpec((T, CW), _row), pl.BlockSpec((T, CW), _row), vec, vec],
        out_specs=[pl.BlockSpec((T, CW), _row), vec, vec, vec],
        out_shape=[jax.ShapeDtypeStruct((S, CW), F32)] + [jax.ShapeDtypeStruct((1, CW), F32)] * 3,
        compiler_params=_params(("arbitrary",)),
    )(yc, dact, lng, lnb)


def _conv_bwd_dw(z, dyc, wdw, dz, name):
    S = z.shape[0]
    T, HL, RC = CONV_T, CONV_HALO, CONV_RC

    def body(zc_ref, zp_ref, zn_ref, dc_ref, dp_ref, dn_ref, w_ref, dz_in, dz_ref, dw_ref, uext_ref, ush_ref,
             dext_ref, dsh_ref, dwacc_ref):
        _fill_shifted(ush_ref, uext_ref, _glu(zc_ref[...]), _glu(zp_ref[...]), _glu(zn_ref[...]))
        _fill_shifted(dsh_ref, dext_ref, dc_ref[...], dp_ref[...], dn_ref[...])

        @pl.when(pl.program_id(0) == 0)
        def _():
            dwacc_ref[...] = jnp.zeros_like(dwacc_ref)

        for c in range(T // RC):
            dcur = dc_ref[c * RC:(c + 1) * RC, :]
            du = jnp.zeros((RC, CW), F32)
            for k in range(KSIZE):
                du = du + w_ref[k:k + 1, :] * _window(dsh_ref, c * RC + HL + PAD - k, RC)
                prod = dcur * _window(ush_ref, c * RC + k + HL - PAD, RC)
                dwacc_ref[k] += jnp.sum(prod.reshape(RC // SUBLANES, SUBLANES, CW), axis=0)
            zc = zc_ref[c * RC:(c + 1) * RC, :].astype(F32)
            a, gt = zc[:, :CW], zc[:, CW:]
            sg = _sigmoid(gt)
            dz_ref[c * RC:(c + 1) * RC, 0:CW] = (du * sg).astype(BF16)
            dz_ref[c * RC:(c + 1) * RC, CW:C1] = (du * a * sg * (1.0 - sg)).astype(BF16)

        @pl.when(pl.program_id(0) == pl.num_programs(0) - 1)
        def _():
            dw_ref[...] = jnp.sum(dwacc_ref[...], axis=1)

    return pl.pallas_call(
        body, name=name, grid=(S // T,),
        in_specs=_halo_specs(T, HL, S, C1, Z_CONV // C1) + _halo_specs(T, HL, S, CW, 0)
        + [pl.BlockSpec((32, CW), _fixed), pl.BlockSpec(memory_space=pl.ANY)],
        out_specs=[pl.BlockSpec((T, C1), lambda i: (i, Z_CONV // C1)), pl.BlockSpec((32, CW), _fixed)],
        out_shape=[jax.ShapeDtypeStruct(dz.shape, BF16), jax.ShapeDtypeStruct((32, CW), F32)],
        input_output_aliases={7: 0},
        scratch_shapes=_shifted_scratch() + _shifted_scratch() + [pltpu.VMEM((32, SUBLANES, CW), F32)],
        compiler_params=_params(("arbitrary",)),
    )(z, z, z, dyc, dyc, dyc, wdw, dz)


def _t5_bucket(rel):
    nb = NUM_BUCKETS // 2
    max_exact = nb // 2
    ret = jnp.where(rel > 0, nb, 0)
    n = jnp.abs(rel)
    nf = jnp.maximum(n, 1).astype(F32)
    large = max_exact + (jnp.log(nf / max_exact) / math.log(MAX_DISTANCE / max_exact)
                         * (nb - max_exact)).astype(I32)
    large = jnp.minimum(large, nb - 1)
    return ret + jnp.where(n < max_exact, n, large)


def _offsets_qk(nq, nk, shift):
    return lax.broadcasted_iota(I32, (nq, nk), 1) + shift - lax.broadcasted_iota(I32, (nq, nk), 0)


def _bias_table(bk, rb_ref, col, off):
    acc = jnp.zeros(bk.shape, F32)
    for b in range(NUM_BUCKETS):
        acc = jnp.where(bk == b, rb_ref[b, col], acc)
    return jnp.where(jnp.abs(off) <= RADIUS, acc, NEG_INF)


def _to_halves(scr, row0, val):
    rows = val.shape[0]
    v = val.astype(F32)
    scr[0, row0:row0 + rows, :] = v[:, :128]
    scr[1, row0:row0 + rows, :] = v[:, 128:]


ATT_FWD_GROUP = 2
ATT_BWD_GROUP = 1


def _att_units(d, fn, group):
    nj = ATT_TB // (ATT_QB * d)
    if nj == 1:
        def trip(t, c):
            r0 = pl.multiple_of(t * 8, 8)
            for u in range(0, 8, group):
                fn([(r0 + u + v, 0) for v in range(group)])
            return c

        lax.fori_loop(0, d // 8, trip, 0)
        return
    for r in range(d):
        def step(t, c, r=r):
            fn([(r, t * group + u) for u in range(group)])
            return c

        lax.fori_loop(0, nj // group, step, 0)


def _unit_row(r, j, d):
    if isinstance(j, int):
        return j * ATT_QB * d + r
    return pl.multiple_of(j * (ATT_QB * d), ATT_QB) + r


def _att_fwd(z, rel_bias, g, name):
    S = z.shape[0]
    d = DILS[g]
    TB, QB = ATT_TB, ATT_QB
    H = RADIUS * d
    L = S // d
    cq = (Z_ATT + 3 * GW * g) // GW
    ck, cv = cq + 1, cq + 2
    bk = _t5_bucket(_offsets_qk(QB, 2 * QB, -RADIUS) * d)

    def body(rb_ref, bk_ref, q_ref, kc_ref, kp_ref, kn_ref, vc_ref, vp_ref, vn_ref, o_ref, l_ref,
             qs, ks, vs, os_, ls, bias):
        i = pl.program_id(0)

        @pl.when(i == 0)
        def _():
            off = _offsets_qk(QB, 2 * QB, -RADIUS)
            for h in range(HPG):
                bias[h] = _bias_table(bk_ref[...], rb_ref, g * HPG + h, off)

        _to_halves(qs, 0, q_ref[...].astype(F32) * ATT_SCALE)
        for scr, p_ref, c_ref, n_ref in ((ks, kp_ref, kc_ref, kn_ref), (vs, vp_ref, vc_ref, vn_ref)):
            _to_halves(scr, 0, p_ref[...])
            _to_halves(scr, H, c_ref[...])
            _to_halves(scr, H + TB, n_ref[...])

        lo = lax.broadcasted_iota(I32, (QB, 128), 1) < HD

        def units(rjs):
            work = []
            for r, j in rjs:
                row = _unit_row(r, j, d)
                km = lax.broadcasted_iota(I32, (1, 2 * QB), 1) + (i * (TB // d) + j * QB - RADIUS)
                edge = jnp.where(jnp.where(km >= 0, km, L) < L, 0.0, NEG_INF)
                for hf in (0, 1):
                    q2 = qs[hf, pl.ds(row, QB, stride=d), :]
                    k2 = ks[hf, pl.ds(row, 2 * QB, stride=d), :].astype(BF16)
                    v2 = vs[hf, pl.ds(row, 2 * QB, stride=d), :].astype(BF16)
                    qq = jnp.concatenate([jnp.where(lo, q2, 0.0), jnp.where(lo, 0.0, q2)], axis=0).astype(BF16)
                    work.append((row, hf, edge, k2, v2, qq))
            scores = [_dot_nt(qq, k2) for (_, _, _, k2, _, qq) in work]
            probs = []
            for (row, hf, edge, *_), ss in zip(work, scores):
                es, stats = [], []
                for hh in (0, 1):
                    s = ss[hh * QB:(hh + 1) * QB] + bias[2 * hf + hh] + edge
                    m = jnp.max(s, axis=-1, keepdims=True)
                    e = jnp.exp(s - m)
                    den = jnp.sum(e, axis=-1, keepdims=True)
                    es.append(e.astype(BF16))
                    stats.append((1.0 / den, m + jnp.log(den)))
                probs.append((jnp.concatenate(es, axis=0), stats))
            for (row, hf, _, _, v2, _), (ee, stats) in zip(work, probs):
                oo = _dot(ee, v2)
                os_[hf, pl.ds(row, QB, stride=d), :] = jnp.where(lo, oo[:QB] * stats[0][0], oo[QB:] * stats[1][0])
                ls[hf, pl.ds(row, QB, stride=d), :] = jnp.where(lo, stats[0][1], stats[1][1])

        _att_units(d, units, ATT_FWD_GROUP)
        for hf in (0, 1):
            o_ref[:, hf * 128:(hf + 1) * 128] = os_[hf].astype(BF16)
            l_ref[:, hf * 128:(hf + 1) * 128] = ls[hf]

    def halo3(col):
        c, p, n = _halo_specs(TB, H, S, GW, col)
        return [c, p, n]

    return pl.pallas_call(
        body, name=name, grid=(S // TB,),
        in_specs=[pl.BlockSpec(memory_space=pltpu.SMEM), pl.BlockSpec((QB, 2 * QB), _fixed),
                  pl.BlockSpec((TB, GW), lambda i: (i, cq))] + halo3(ck) + halo3(cv),
        out_specs=[pl.BlockSpec((TB, GW), _row), pl.BlockSpec((TB, GW), _row)],
        out_shape=[jax.ShapeDtypeStruct((S, GW), BF16), jax.ShapeDtypeStruct((S, GW), F32)],
        scratch_shapes=[pltpu.VMEM((2, TB, 128), F32), pltpu.VMEM((2, TB + 2 * H, 128), F32),
                        pltpu.VMEM((2, TB + 2 * H, 128), F32), pltpu.VMEM((2, TB, 128), F32),
                        pltpu.VMEM((2, TB, 128), F32), pltpu.VMEM((HPG, QB, 2 * QB), F32)],
        compiler_params=_params(("arbitrary",)),
    )(rel_bias, bk, z, z, z, z, z, z, z)


def _att_combine(os3, ls3, name):
    S = os3[0].shape[0]
    T = 1024

    def body(o1, o2, o3, l1, l2, l3, o_ref, l_ref):
        lv = [l1[...], l2[...], l3[...]]
        m = jnp.maximum(jnp.maximum(lv[0], lv[1]), lv[2])
        e = [jnp.exp(v - m) for v in lv]
        den = e[0] + e[1] + e[2]
        acc = jnp.zeros_like(m)
        for ev, o in zip(e, (o1, o2, o3)):
            acc = acc + (ev / den) * o[...].astype(F32)
        o_ref[...] = acc.astype(BF16)
        l_ref[...] = m + jnp.log(den)

    blk = pl.BlockSpec((T, GW), _row)
    return pl.pallas_call(
        body, name=name, grid=(S // T,), in_specs=[blk] * 6, out_specs=[blk, blk],
        out_shape=[jax.ShapeDtypeStruct((S, GW), BF16), jax.ShapeDtypeStruct((S, GW), F32)],
        compiler_params=_params(("parallel",)),
    )(*os3, *ls3)


def _att_prep(do, o, lse, name):
    S = do.shape[0]
    T = 1024

    def body(do_ref, o_ref, l_ref, out_ref):
        prod = do_ref[...].astype(F32) * o_ref[...].astype(F32)
        dd = [jnp.broadcast_to(jnp.sum(prod[:, h * HD:(h + 1) * HD], axis=-1, keepdims=True), (T, HD))
              for h in range(HPG)]
        lane = lax.broadcasted_iota(I32, (T, GW), 1)
        out_ref[...] = jnp.where(lane % HD < HD // 2, l_ref[...], jnp.concatenate(dd, axis=-1))

    blk = pl.BlockSpec((T, GW), _row)
    return pl.pallas_call(
        body, name=name, grid=(S // T,), in_specs=[blk] * 3, out_specs=blk,
        out_shape=jax.ShapeDtypeStruct((S, GW), F32), compiler_params=_params(("parallel",)),
    )(do, o, lse)


def _att_bwd(z, rel_bias, do, ld, dz, g, name):
    S = z.shape[0]
    d = DILS[g]
    TB, QB = ATT_TB, ATT_QB
    H = RADIUS * d
    L = S // d
    E = TB + 2 * H
    cq = (Z_ATT + 3 * GW * g) // GW
    ck, cv = cq + 1, cq + 2
    bk_a = _t5_bucket(_offsets_qk(QB, 2 * QB, -RADIUS) * d)
    bk_b = _t5_bucket(-_offsets_qk(QB, 2 * QB, -RADIUS) * d)

    def body(rb_ref, bka_ref, bkb_ref, *refs):
        ins, (dz_ref, db_ref) = refs[:15], refs[16:18]
        qs, ks, vs, dos, ls, dqs, dks, dvs, bias_a, bias_b, dbias = refs[18:]
        i = pl.program_id(0)
        n = pl.num_programs(0)

        @pl.when(i == 0)
        def _():
            off = _offsets_qk(QB, 2 * QB, -RADIUS)
            for h in range(HPG):
                bias_a[h] = _bias_table(bka_ref[...], rb_ref, g * HPG + h, off)
                bias_b[h] = _bias_table(bkb_ref[...], rb_ref, g * HPG + h, off)
            dbias[...] = jnp.zeros_like(dbias)

        for a, scr in enumerate((qs, ks, vs, dos, ls)):
            c_ref, p_ref, n_ref = ins[3 * a:3 * a + 3]
            pre = (lambda v: v.astype(F32) * ATT_SCALE) if a == 0 else (lambda v: v)
            _to_halves(scr, 0, pre(p_ref[...]))
            _to_halves(scr, H, pre(c_ref[...]))
            _to_halves(scr, H + TB, pre(n_ref[...]))

        lo = lax.broadcasted_iota(I32, (QB, 128), 1) < HD

        def split(v):
            return jnp.concatenate([jnp.where(lo, v, 0.0), jnp.where(lo, 0.0, v)], axis=0).astype(BF16)

        def halves(v):
            return v[:QB], v[QB:]

        def units(rjs):
            work = []
            for r, j in rjs:
                row = _unit_row(r, j, d)
                cur = row + H
                m0 = i * (TB // d) + j * QB - RADIUS
                km = lax.broadcasted_iota(I32, (1, 2 * QB), 1) + m0
                edge_a = jnp.where(jnp.where(km >= 0, km, L) < L, 0.0, NEG_INF)
                for hf in (0, 1):
                    ld = lambda scr, at, nrow: scr[hf, pl.ds(at, nrow, stride=d), :]
                    w = dict(row=row, hf=hf, edge=edge_a, l_c=ld(ls, cur, QB), l_t=ld(ls, row, 2 * QB).T)
                    for nm, scr in (("q", qs), ("k", ks), ("v", vs), ("do", dos)):
                        w[nm + "_c"] = split(ld(scr, cur, QB))
                        w[nm + "_e"] = ld(scr, row, 2 * QB).astype(BF16)
                    work.append(w)
            for w in work:
                w["s"] = halves(_dot_nt(w["q_c"], w["k_e"]))
                w["dp"] = halves(_dot_nt(w["do_c"], w["v_e"]))
                w["s2"] = halves(_dot_nt(w["k_c"], w["q_e"]))
                w["dp2"] = halves(_dot_nt(w["v_c"], w["do_e"]))
            for w in work:
                w["ds"], w["p2"], w["ds2"] = [], [], []
                for hh in (0, 1):
                    h, c0 = 2 * w["hf"] + hh, HD * hh
                    l_c, l_t = w["l_c"], w["l_t"]
                    p = jnp.exp(w["s"][hh] + bias_a[h] + w["edge"] - l_c[:, c0:c0 + 1])
                    ds = p * (w["dp"][hh] - l_c[:, c0 + HD // 2:c0 + HD // 2 + 1])
                    dbias[h] += ds
                    p2 = jnp.exp(w["s2"][hh] + bias_b[h] + w["edge"] - l_t[c0:c0 + 1, :])
                    ds2 = p2 * (w["dp2"][hh] - l_t[c0 + HD // 2:c0 + HD // 2 + 1, :])
                    w["ds"].append(ds.astype(BF16))
                    w["p2"].append(p2.astype(BF16))
                    w["ds2"].append(ds2.astype(BF16))
            for w in work:
                at = pl.ds(w["row"], QB, stride=d)
                both = lambda pair, rhs: halves(_dot(jnp.concatenate(pair, axis=0), rhs))
                dq = both(w["ds"], w["k_e"])
                dqs[w["hf"], at, :] = jnp.where(lo, dq[0], dq[1]) * ATT_SCALE
                dv = both(w["p2"], w["do_e"])
                dvs[w["hf"], at, :] = jnp.where(lo, dv[0], dv[1])
                dk = both(w["ds2"], w["q_e"])
                dks[w["hf"], at, :] = jnp.where(lo, dk[0], dk[1])

        _att_units(d, units, ATT_BWD_GROUP)
        for a, scr in enumerate((dqs, dks, dvs)):
            for hf in (0, 1):
                dz_ref[:, a * GW + hf * 128:a * GW + (hf + 1) * 128] = scr[hf].astype(BF16)

        @pl.when(i == n - 1)
        def _():
            rows = lax.broadcasted_iota(I32, (NUM_BUCKETS, 128), 0)
            lanes = lax.broadcasted_iota(I32, (NUM_BUCKETS, 128), 1)
            out = jnp.zeros((NUM_BUCKETS, 128), F32)
            bk = bka_ref[...]
            for h in range(HPG):
                acc = dbias[h]
                for b in range(NUM_BUCKETS):
                    tot = jnp.sum(jnp.sum(jnp.where(bk == b, acc, 0.0), axis=1, keepdims=True), axis=0, keepdims=True)
                    out = out + jnp.where((rows == b) & (lanes == h), tot, 0.0)
            db_ref[...] = out

    def halo3(col, width=GW):
        return _halo_specs(TB, H, S, width, col)

    one = pl.Buffered(1)

    def single(specs):
        return [pl.BlockSpec(s.block_shape, s.index_map, pipeline_mode=one) for s in specs]

    in_specs = ([pl.BlockSpec(memory_space=pltpu.SMEM), pl.BlockSpec((QB, 2 * QB), _fixed),
                 pl.BlockSpec((QB, 2 * QB), _fixed)]
                + single(halo3(cq) + halo3(ck) + halo3(cv) + halo3(0) + halo3(0))
                + [pl.BlockSpec(memory_space=pl.ANY)])
    return pl.pallas_call(
        body, name=name, grid=(S // TB,), in_specs=in_specs,
        out_specs=[pl.BlockSpec((TB, 3 * GW), lambda i: (i, cq // 3)), pl.BlockSpec((NUM_BUCKETS, 128), _fixed)],
        out_shape=[jax.ShapeDtypeStruct(dz.shape, BF16), jax.ShapeDtypeStruct((NUM_BUCKETS, 128), F32)],
        input_output_aliases={18: 0},
        scratch_shapes=[pltpu.VMEM((2, E, 128), F32)] * 5 + [pltpu.VMEM((2, TB, 128), F32)] * 3
        + [pltpu.VMEM((HPG, QB, 2 * QB), F32)] * 3,
        compiler_params=_params(("arbitrary",)),
    )(rel_bias, bk_a, bk_b, z, z, z, z, z, z, z, z, z, do, do, do, ld, ld, ld, dz)


def _memkv_fwd(mem, gm, wkv, name):
    def body(m_ref, g_ref, w_ref, hm_ref, kv_ref):
        hm = _rms_fwd_val(m_ref[...], g_ref[...]).astype(BF16)
        hm_ref[...] = hm
        kv_ref[...] = _dot(hm, w_ref[...]).astype(BF16)

    return pl.pallas_call(
        body, name=name,
        out_shape=[jax.ShapeDtypeStruct((N_MEM, D), BF16), jax.ShapeDtypeStruct((N_MEM, 2 * MW), BF16)],
        compiler_params=_params(),
    )(mem, gm, wkv)


def _memkv_bwd(mem, gm, hm, wkv, dkv, name):
    def body(m_ref, g_ref, hm_ref, w_ref, dkv_ref, dw_ref, dg_ref):
        dkv_b = dkv_ref[...].astype(BF16)
        dw_ref[...] = _dot_tn(hm_ref[...], dkv_b)
        dhm = _dot_nt(dkv_b, w_ref[...])
        _, dgr = _rms_bwd_val(m_ref[...], g_ref[...], dhm)
        dg_ref[...] = jnp.sum(dgr, axis=0, keepdims=True)

    return pl.pallas_call(
        body, name=name,
        out_shape=[jax.ShapeDtypeStruct((D, 2 * MW), F32), jax.ShapeDtypeStruct((1, D), F32)],
        compiler_params=_params(),
    )(mem, gm, hm, wkv, dkv)


MEM_T = 512


def _mem_q_spec():
    return pl.BlockSpec((MEM_T, MW), lambda i: (i, Z_MEM // MW))


def _memattn_fwd(z, kv, name):
    S = z.shape[0]
    T = MEM_T

    def body(q_ref, kv_ref, o_ref):
        for h in range(MH):
            kh = kv_ref[:, h * MHD:(h + 1) * MHD]
            vh = kv_ref[:, MW + h * MHD:MW + (h + 1) * MHD]
            s = _dot_nt(q_ref[:, h * MHD:(h + 1) * MHD], kh) * MEM_SCALE
            e = jnp.exp(s - jnp.max(s, axis=-1, keepdims=True))
            p = e / jnp.sum(e, axis=-1, keepdims=True)
            o_ref[:, h * MHD:(h + 1) * MHD] = _dot(p.astype(BF16), vh).astype(BF16)

    return pl.pallas_call(
        body, name=name, grid=(S // T,),
        in_specs=[_mem_q_spec(), pl.BlockSpec((N_MEM, 2 * MW), _fixed)],
        out_specs=pl.BlockSpec((T, MW), _row),
        out_shape=jax.ShapeDtypeStruct((S, MW), BF16),
        compiler_params=_params(("parallel",)),
    )(z, kv)


def _memattn_bwd(z, kv, dom, dz, name):
    S = z.shape[0]
    T = MEM_T

    def body(q_ref, kv_ref, do_ref, dz_in, dq_ref, dkv_ref):
        @pl.when(pl.program_id(0) == 0)
        def _():
            dkv_ref[...] = jnp.zeros_like(dkv_ref)

        for h in range(MH):
            kh = kv_ref[:, h * MHD:(h + 1) * MHD]
            vh = kv_ref[:, MW + h * MHD:MW + (h + 1) * MHD]
            qh = q_ref[:, h * MHD:(h + 1) * MHD]
            doh = do_ref[:, h * MHD:(h + 1) * MHD]
            s = _dot_nt(qh, kh) * MEM_SCALE
            e = jnp.exp(s - jnp.max(s, axis=-1, keepdims=True))
            p = e / jnp.sum(e, axis=-1, keepdims=True)
            dkv_ref[:, MW + h * MHD:MW + (h + 1) * MHD] += _dot_tn(p.astype(BF16), doh)
            dp = _dot_nt(doh, vh)
            ds = (p * (dp - jnp.sum(dp * p, axis=-1, keepdims=True))).astype(BF16)
            dq_ref[:, h * MHD:(h + 1) * MHD] = (_dot(ds, kh) * MEM_SCALE).astype(BF16)
            dkv_ref[:, h * MHD:(h + 1) * MHD] += _dot_tn(ds, qh) * MEM_SCALE

    return pl.pallas_call(
        body, name=name, grid=(S // T,),
        in_specs=[_mem_q_spec(), pl.BlockSpec((N_MEM, 2 * MW), _fixed), pl.BlockSpec((T, MW), _row),
                  pl.BlockSpec(memory_space=pl.ANY)],
        out_specs=[_mem_q_spec(), pl.BlockSpec((N_MEM, 2 * MW), _fixed)],
        out_shape=[jax.ShapeDtypeStruct(dz.shape, BF16), jax.ShapeDtypeStruct((N_MEM, 2 * MW), F32)],
        input_output_aliases={3: 0},
        compiler_params=_params(("arbitrary",)),
    )(z, kv, dom, dz)


MERGE_T = 512


def _gate_spec(T):
    return pl.BlockSpec((T, 3 * D), lambda i: (i, Z_GATE // (3 * D)))


def _branches(ca_ref, oa_ref, om_ref, wco_ref, wao_ref, wmo_ref, zg_ref, bg_ref):
    ys = [_dot(ca_ref[...], wco_ref[...]), _dot(oa_ref[...], wao_ref[...]), _dot(om_ref[...], wmo_ref[...])]
    gs = [_sigmoid(zg_ref[:, b * D:(b + 1) * D].astype(F32) + bg_ref[:, b * D:(b + 1) * D]) for b in range(3)]
    return ys, gs


def _merge_fwd(x, cact, oatt, om, z, wco, wao, wmo, wout, bgate, gpost, gnext, name):
    S = x.shape[0]
    T = MERGE_T

    def body(x_ref, ca_ref, oa_ref, om_ref, zg_ref, wco_ref, wao_ref, wmo_ref, wout_ref, bg_ref, gp_ref, gn_ref,
             x1_ref, mg_ref, t_ref, h_ref):
        ys, gs = _branches(ca_ref, oa_ref, om_ref, wco_ref, wao_ref, wmo_ref, zg_ref, bg_ref)
        mb = (gs[0] * ys[0] + gs[1] * ys[1] + gs[2] * ys[2]).astype(BF16)
        t = _dot(mb, wout_ref[...])
        mg_ref[...] = mb
        t_ref[...] = t
        x1 = x_ref[...] + _rms_fwd_val(t, gp_ref[...])
        x1_ref[...] = x1
        h_ref[...] = _rms_fwd_val(x1, gn_ref[...]).astype(BF16)

    full = lambda a: pl.BlockSpec(a.shape, _fixed)
    return pl.pallas_call(
        body, name=name, grid=(S // T,),
        in_specs=[pl.BlockSpec((T, D), _row), pl.BlockSpec((T, CW), _row), pl.BlockSpec((T, GW), _row),
                  pl.BlockSpec((T, MW), _row), _gate_spec(T)]
        + [full(wco), full(wao), full(wmo), full(wout), full(bgate), full(gpost), full(gnext)],
        out_specs=[pl.BlockSpec((T, D), _row)] * 4,
        out_shape=[jax.ShapeDtypeStruct((S, D), F32), jax.ShapeDtypeStruct((S, D), BF16),
                   jax.ShapeDtypeStruct((S, D), F32), jax.ShapeDtypeStruct((S, D), BF16)],
        compiler_params=_params(("parallel",)),
    )(x, cact, oatt, om, z, wco, wao, wmo, wout, bgate, gpost, gnext)


def _merge_bwd(dx1, t, mg, cact, oatt, om, z, wco, wao, wmo, wout, bgate, gpost, name):
    S = dx1.shape[0]
    T = MERGE_T

    def body(dx_ref, t_ref, mg_ref, ca_ref, oa_ref, om_ref, zg_ref, wco_ref, wao_ref, wmo_ref, wout_ref,
             bg_ref, gp_ref, dzg_ref, dca_ref, doa_ref, dom_ref, dwco_ref, dwao_ref, dwmo_ref, dwout_ref,
             dbg_ref, dgp_ref):
        accs = (dwco_ref, dwao_ref, dwmo_ref, dwout_ref, dbg_ref, dgp_ref)

        @pl.when(pl.program_id(0) == 0)
        def _():
            for a in accs:
                a[...] = jnp.zeros_like(a)

        dt, dgr = _rms_bwd_val(t_ref[...], gp_ref[...], dx_ref[...])
        dgp_ref[...] += jnp.sum(dgr, axis=0, keepdims=True)
        dtb = dt.astype(BF16)
        dwout_ref[...] += _dot_tn(mg_ref[...], dtb)
        dm = _dot_nt(dtb, wout_ref[...])
        ys, gs = _branches(ca_ref, oa_ref, om_ref, wco_ref, wao_ref, wmo_ref, zg_ref, bg_ref)
        for b, (act_ref, w_ref, dw_ref, da_ref) in enumerate(
                ((ca_ref, wco_ref, dwco_ref, dca_ref), (oa_ref, wao_ref, dwao_ref, doa_ref),
                 (om_ref, wmo_ref, dwmo_ref, dom_ref))):
            dzg = dm * ys[b] * gs[b] * (1.0 - gs[b])
            dzg_ref[:, b * D:(b + 1) * D] = dzg.astype(BF16)
            dbg_ref[:, b * D:(b + 1) * D] += jnp.sum(dzg, axis=0, keepdims=True)
            dy = (dm * gs[b]).astype(BF16)
            dw_ref[...] += _dot_tn(act_ref[...], dy)
            da_ref[...] = _dot_nt(dy, w_ref[...]).astype(BF16)

    full = lambda a: pl.BlockSpec(a.shape, _fixed)
    fullf = lambda a: jax.ShapeDtypeStruct(a.shape, F32)
    return pl.pallas_call(
        body, name=name, grid=(S // T,),
        in_specs=[pl.BlockSpec((T, D), _row), pl.BlockSpec((T, D), _row), pl.BlockSpec((T, D), _row),
                  pl.BlockSpec((T, CW), _row), pl.BlockSpec((T, GW), _row), pl.BlockSpec((T, MW), _row)]
        + [_gate_spec(T), full(wco), full(wao), full(wmo), full(wout), full(bgate), full(gpost)],
        out_specs=[_gate_spec(T), pl.BlockSpec((T, CW), _row), pl.BlockSpec((T, GW), _row),
                   pl.BlockSpec((T, MW), _row), full(wco), full(wao), full(wmo), full(wout), full(bgate), full(gpost)],
        out_shape=[jax.ShapeDtypeStruct((S, NIN), BF16), jax.ShapeDtypeStruct((S, CW), BF16),
                   jax.ShapeDtypeStruct((S, GW), BF16), jax.ShapeDtypeStruct((S, MW), BF16),
                   fullf(wco), fullf(wao), fullf(wmo), fullf(wout), fullf(bgate), fullf(gpost)],
        compiler_params=_params(("arbitrary",)),
    )(dx1, t, mg, cact, oatt, om, z, wco, wao, wmo, wout, bgate, gpost)


FFN_T = 256


def _ffn_fwd(x1, gu, wfo, gpost, gnext, name):
    S = x1.shape[0]
    T = FFN_T

    nxt = gnext is not None

    def body(x_ref, gu_ref, w_ref, gp_ref, *rest):
        x2_ref, f_ref = rest[nxt:nxt + 2]
        gv = gu_ref[:, :FH].astype(F32)
        uv = gu_ref[:, FH:].astype(F32)
        act = (gv * _sigmoid(gv) * uv).astype(BF16)
        f = _dot(act, w_ref[...])
        f_ref[...] = f
        x2 = x_ref[...] + _rms_fwd_val(f, gp_ref[...])
        x2_ref[...] = x2
        if nxt:
            rest[3][...] = _rms_fwd_val(x2, rest[0][...]).astype(BF16)

    return pl.pallas_call(
        body, name=name, grid=(S // T,),
        in_specs=[pl.BlockSpec((T, D), _row), pl.BlockSpec((T, 2 * FH), _row), pl.BlockSpec((FH, D), _fixed),
                  pl.BlockSpec((1, D), _fixed)] + [pl.BlockSpec((1, D), _fixed)] * nxt,
        out_specs=[pl.BlockSpec((T, D), _row)] * (2 + nxt),
        out_shape=[jax.ShapeDtypeStruct((S, D), F32)] * 2 + [jax.ShapeDtypeStruct((S, D), BF16)] * nxt,
        compiler_params=_params(("parallel",)),
    )(x1, gu, wfo, gpost, *([gnext] if nxt else []))


def _ffn_bwd(dx2, f, gu, wfo, gpost, name):
    S = dx2.shape[0]
    T = FFN_T

    def body(dx_ref, f_ref, gu_ref, w_ref, gp_ref, dgu_ref, df_ref, act_ref, dgp_ref):
        @pl.when(pl.program_id(0) == 0)
        def _():
            dgp_ref[...] = jnp.zeros_like(dgp_ref)

        df, dgr = _rms_bwd_val(f_ref[...], gp_ref[...], dx_ref[...])
        dgp_ref[...] += jnp.sum(dgr, axis=0, keepdims=True)
        dfb = df.astype(BF16)
        df_ref[...] = dfb
        dact = _dot_nt(dfb, w_ref[...])
        gv = gu_ref[:, :FH].astype(F32)
        uv = gu_ref[:, FH:].astype(F32)
        sg = _sigmoid(gv)
        silu = gv * sg
        act_ref[...] = (silu * uv).astype(BF16)
        dgu_ref[:, :FH] = (dact * uv * (sg * (1.0 + gv * (1.0 - sg)))).astype(BF16)
        dgu_ref[:, FH:] = (dact * silu).astype(BF16)

    return pl.pallas_call(
        body, name=name, grid=(S // T,),
        in_specs=[pl.BlockSpec((T, D), _row), pl.BlockSpec((T, D), _row), pl.BlockSpec((T, 2 * FH), _row),
                  pl.BlockSpec((FH, D), _fixed), pl.BlockSpec((1, D), _fixed)],
        out_specs=[pl.BlockSpec((T, 2 * FH), _row), pl.BlockSpec((T, D), _row), pl.BlockSpec((T, FH), _row),
                   pl.BlockSpec((1, D), _fixed)],
        out_shape=[jax.ShapeDtypeStruct((S, 2 * FH), BF16), jax.ShapeDtypeStruct((S, D), BF16),
                   jax.ShapeDtypeStruct((S, FH), BF16), jax.ShapeDtypeStruct((1, D), F32)],
        compiler_params=_params(("arbitrary",)),
    )(dx2, f, gu, wfo, gpost)


def _loss_head(y, target, name):
    S = y.shape[0]
    T = 512

    def body(y_ref, t_ref, dy_ref, l_ref):
        @pl.when(pl.program_id(0) == 0)
        def _():
            l_ref[...] = jnp.zeros_like(l_ref)

        e = y_ref[...] - t_ref[...]
        dy_ref[...] = e * (1.0 / D)
        l_ref[...] += (0.5 / D) * jnp.sum(jnp.sum(e * e, axis=1, keepdims=True), axis=0, keepdims=True)

    return pl.pallas_call(
        body, name=name, grid=(S // T,),
        in_specs=[pl.BlockSpec((T, D), _row)] * 2,
        out_specs=[pl.BlockSpec((T, D), _row), pl.BlockSpec((8, 128), _fixed)],
        out_shape=[jax.ShapeDtypeStruct((S, D), F32), jax.ShapeDtypeStruct((8, 128), F32)],
        compiler_params=_params(("arbitrary",)),
    )(y, target)


BIG = ("w_in", "w_conv_out", "w_att_out", "w_mem_kv", "w_mem_out", "w_out", "w_ffn_in", "w_ffn_out")
SMALL = ("rel_bias", "norm_mix_pre", "b_gate", "conv_dw_bias", "conv_ln_g", "conv_ln_b", "norm_mem",
         "norm_mix_post", "norm_ffn_pre", "norm_ffn_post")


def _layer_fwd(l, x, h, mem, w, rel_bias, gnext):
    tag = f"_l{l}"
    z = _mm_nn(h, w["w_in"], 512, BF16, "mm_in" + tag)
    yc, cact = _conv_fwd(z, w["conv_dw"], w["conv_dw_bias"], w["conv_ln_g"], w["conv_ln_b"], "conv_fwd" + tag)
    og, lg = zip(*[_att_fwd(z, rel_bias, g, f"att_fwd_g{g}" + tag) for g in range(3)])
    oatt, lse = _att_combine(og, lg, "att_combine" + tag)
    hm, kv = _memkv_fwd(mem, w["norm_mem"], w["w_mem_kv"], "memkv_fwd" + tag)
    om = _memattn_fwd(z, kv, "memattn_fwd" + tag)
    x1, mg, t, h2 = _merge_fwd(x, cact, oatt, om, z, w["w_conv_out"], w["w_att_out"], w["w_mem_out"], w["w_out"],
                               w["b_gate"], w["norm_mix_post"], w["norm_ffn_pre"], "merge_fwd" + tag)
    gu = _mm_nn(h2, w["w_ffn_in"], 512, BF16, "mm_ffn_in" + tag)
    x2, f, *hn = _ffn_fwd(x1, gu, w["w_ffn_out"], w["norm_ffn_post"], gnext, "ffn_fwd" + tag)
    saved = dict(x=x, h=h, z=z, yc=yc, cact=cact, oatt=oatt, lse=lse, hm=hm, kv=kv, om=om, x1=x1, mg=mg, t=t,
                 h2=h2, gu=gu, f=f)
    return x2, (hn[0] if hn else None), saved


def _layer_bwd(l, dx2, mem, w, rel_bias, s, on_ffn_grads=None):
    tag = f"_l{l}"
    gr = {}
    dgu, df, act, gr["norm_ffn_post"] = _ffn_bwd(dx2, s["f"], s["gu"], w["w_ffn_out"], w["norm_ffn_post"], "ffn_bwd" + tag)
    gr["w_ffn_out"] = _mm_tn(act, df, 1024, 512, "dw_ffn_out" + tag)
    gr["w_ffn_in"] = _mm_tn(s["h2"], dgu, 2048, 1408, "dw_ffn_in" + tag)
    gpre = w["norm_ffn_pre"]
    if on_ffn_grads is not None:
        gpre = gpre + on_ffn_grads(gr)[0, 0]
    dx1, gr["norm_ffn_pre"] = _mm_nt_rms_bwd(dgu, w["w_ffn_in"], s["x1"], gpre, dx2, 512, "dh_ffn" + tag)
    (dz, dcact, doatt, dom, gr["w_conv_out"], gr["w_att_out"], gr["w_mem_out"], gr["w_out"], gr["b_gate"],
     gr["norm_mix_post"]) = _merge_bwd(dx1, s["t"], s["mg"], s["cact"], s["oatt"], s["om"], s["z"], w["w_conv_out"],
                                       w["w_att_out"], w["w_mem_out"], w["w_out"], w["b_gate"], w["norm_mix_post"],
                                       "merge_bwd" + tag)
    dyc, gr["conv_ln_g"], gr["conv_ln_b"], gr["conv_dw_bias"] = _conv_bwd_ln(
        s["yc"], dcact, w["conv_ln_g"], w["conv_ln_b"], "conv_bwd_ln" + tag)
    dz, dwdw = _conv_bwd_dw(s["z"], dyc, w["conv_dw"], dz, "conv_bwd_dw" + tag)
    gr["conv_dw"] = dwdw[:KSIZE]
    ld = _att_prep(doatt, s["oatt"], s["lse"], "att_prep" + tag)
    drb = []
    for g in range(3):
        dz, db = _att_bwd(s["z"], rel_bias, doatt, ld, dz, g, f"att_bwd_g{g}" + tag)
        drb.append(db)
    dz, dkv = _memattn_bwd(s["z"], s["kv"], dom, dz, "memattn_bwd" + tag)
    gr["w_mem_kv"], gr["norm_mem"] = _memkv_bwd(mem, w["norm_mem"], s["hm"], w["w_mem_kv"], dkv, "memkv_bwd" + tag)
    gr["w_in"] = _mm_tn(s["h"], dz, 2048, 1152, "dw_in" + tag)
    dx, gr["norm_mix_pre"] = _mm_nt_rms_bwd(dz, w["w_in"], s["x"], w["norm_mix_pre"], dx1, 512, "dh_in" + tag)
    return dx, gr, drb


def _rel_bias_total(parts, name):
    def body(*refs):
        out_ref = refs[-1]
        acc = jnp.zeros((NUM_BUCKETS, 128), F32)
        for l in range(DEPTH):
            for g in range(3):
                v = refs[l * 3 + g][...]
                acc = acc + (v if g == 0 else pltpu.roll(v, HPG * g, axis=1))
        out_ref[...] = acc

    return pl.pallas_call(body, name=name, out_shape=jax.ShapeDtypeStruct((NUM_BUCKETS, 128), F32),
                          compiler_params=_params())(*[p for layer in parts for p in layer])


def _local_step(x, mem, target, rel_bias, layer_fns, gmix, on_grads=None, on_last_ffn_grads=None):
    saved, layers = [], []
    h = _rms_h(x, gmix[0], "rms_mix_l0")
    for l in range(DEPTH):
        layers.append(layer_fns[l](x))
        x, h, s = _layer_fwd(l, x, h, mem, layers[l], rel_bias, gmix[l + 1] if l + 1 < DEPTH else None)
        saved.append(s)
    dy, lpart = _loss_head(x, target, "loss_head")
    grads = [None] * DEPTH
    drb = [None] * DEPTH
    for l in reversed(range(DEPTH)):
        dy, grads[l], drb[l] = _layer_bwd(l, dy, mem, layers[l], rel_bias, saved[l],
                                          on_last_ffn_grads if l == 0 else None)
        if on_grads is not None and l > 0:
            below = dict(layers[l - 1])
            below["norm_ffn_post"] = below["norm_ffn_post"] + on_grads(l, grads[l])[0, 0]
            layers[l - 1] = below
    return lpart[0, 0], dy, grads, _rel_bias_total(drb, "rel_bias_total")


def _z_cols_from_ref(w):
    att = [w[..., R_ATT + (3 * j + g) * GW:R_ATT + (3 * j + g + 1) * GW] for g in range(3) for j in range(3)]
    return jnp.concatenate([w[..., R_GATE:], w[..., :C1], w[..., R_MEM:R_GATE]] + att, axis=-1)


def _ref_cols_from_z(w):
    att = [w[..., Z_ATT + (3 * g + j) * GW:Z_ATT + (3 * g + j + 1) * GW] for j in range(3) for g in range(3)]
    return jnp.concatenate([w[..., Z_CONV:Z_MEM]] + att + [w[..., Z_MEM:Z_ATT], w[..., Z_GATE:Z_CONV]], axis=-1)


N_CHIPS = 4
SHARD = {"w_in": ((D, NIN // 4), 1), "w_conv_out": ((CW, D // 4), 1), "w_att_out": ((GW, D // 4), 1),
         "w_mem_kv": ((D // 4, 2 * MW), 0), "w_mem_out": ((MW, D // 4), 1), "w_out": ((D // 4, D), 0),
         "w_ffn_in": ((D, 2 * FH // 4), 1), "w_ffn_out": ((FH // 4, D), 0)}
CDW_ROWS = 64
VEC_ROWS = (("norm_mix_pre", 1), ("b_gate", 3), ("conv_dw_bias", 1), ("conv_ln_g", 1), ("conv_ln_b", 1),
            ("norm_mem", 1), ("norm_mix_post", 1), ("norm_ffn_pre", 1), ("norm_ffn_post", 1))
VEC_LROWS = sum(r for _, r in VEC_ROWS)
REL_ROW = DEPTH * VEC_LROWS
CDW_ROW = REL_ROW + 1
CDW_GROWS = DEPTH * KSIZE * CW // D
SMALL_ROWS = -(-(CDW_ROW + CDW_GROWS) // 8) * 8


def _mesh_pos():
    return lax.axis_index("x"), lax.axis_index("y"), lax.axis_index("c")


def _other_chips(x, y):
    chips = [(1 - x, y), (x, 1 - y), (1 - x, 1 - y)]
    return chips, [2 * cx + cy for cx, cy in chips]


NBIG = len(BIG)
ANY_SPEC = pl.BlockSpec(memory_space=pl.ANY)


def _remote(src, dst, send_sems, recv_sems, k, to):
    return pltpu.make_async_remote_copy(src_ref=src, dst_ref=dst, send_sem=send_sems.at[k], recv_sem=recv_sems.at[k],
                                        device_id=to, device_id_type=MESH)


def _half(ref, c):
    h = ref.shape[0] // 2
    return ref.at[pl.ds(c * h if isinstance(c, int) else pl.multiple_of(c * h, 16), h)]


def _all_gather(ws, cdw):
    def body(*refs):
        w_refs, cdw_ref = refs[:NBIG], refs[NBIG]
        g_refs, gc_ref = refs[NBIG + 1:2 * NBIG + 1], refs[2 * NBIG + 1]
        send_sems, recv_sems = refs[2 * NBIG + 2:]
        x, y, c = _mesh_pos()
        j = 2 * x + y
        sibling = (x, y, 1 - c)
        chips, blocks = _other_chips(x, y)
        copy = functools.partial(_remote, send_sems=send_sems, recv_sems=recv_sems)
        pairs = list(zip(w_refs, g_refs))
        first = [copy(_half(w, c), _half(g.at[j], c), k=k * NBIG + n, to=(*chip, c))
                 for k, chip in enumerate(chips) for n, (w, g) in enumerate(pairs)]
        first += [copy(cdw_ref, gc_ref.at[j], k=6 * NBIG + k, to=(*chip, c)) for k, chip in enumerate(chips)]
        for cp in first:
            cp.start()
        passed = []
        for k, b in enumerate(blocks):
            for n, (w, g) in enumerate(pairs):
                copy(_half(w, c), _half(g.at[b], c), k=k * NBIG + n, to=sibling).wait_recv()
            onward = [copy(_half(g.at[b], c), _half(g.at[b], c), k=(3 + k) * NBIG + n, to=sibling)
                      for n, (w, g) in enumerate(pairs)]
            for cp in onward:
                cp.start()
            passed += onward
        for k, b in enumerate(blocks):
            for n, (w, g) in enumerate(pairs):
                copy(_half(w, c), _half(g.at[b], 1 - c), k=(3 + k) * NBIG + n, to=sibling).wait_recv()
            copy(cdw_ref, gc_ref.at[b], k=6 * NBIG + k, to=sibling).wait_recv()
        for cp in first + passed:
            cp.wait_send()

    nsem = 6 * NBIG + 3
    return pl.pallas_call(
        body, name="all_gather_weights",
        out_shape=[jax.ShapeDtypeStruct((N_CHIPS,) + w.shape, BF16) for w in ws]
        + [jax.ShapeDtypeStruct((N_CHIPS, CDW_ROWS, 128), F32)],
        in_specs=[ANY_SPEC] * (NBIG + 1), out_specs=[ANY_SPEC] * (NBIG + 1),
        scratch_shapes=[pltpu.SemaphoreType.DMA((nsem,)), pltpu.SemaphoreType.DMA((nsem,))],
    )(*ws, cdw)


SEM_SPEC = pl.BlockSpec(memory_space=pltpu.SEMAPHORE)
DATAFLOW = pltpu.SideEffectType.DATAFLOW_SIDE_EFFECTING


def _gather_copies(w_refs, g_refs, send_sem, recv_sem):
    x, y, c = _mesh_pos()
    j = 2 * x + y
    chips, _ = _other_chips(x, y)
    return [pltpu.make_async_remote_copy(src_ref=_half(w, c), dst_ref=_half(g.at[j], c), send_sem=send_sem,
                                         recv_sem=recv_sem, device_id=(*chip, cc), device_id_type=MESH)
            for chip in chips for cc in (0, 1) for w, g in zip(w_refs, g_refs)]


def _all_gather_start(ws, after):
    def body(*refs):
        w_refs, g_refs = refs[:NBIG], refs[NBIG:2 * NBIG]
        send_sem, recv_sem = refs[2 * NBIG + 1:2 * NBIG + 3]
        token = refs[-1]
        for cp in _gather_copies(w_refs, g_refs, send_sem, recv_sem):
            cp.start()
        token[...] = jnp.zeros_like(token)

    lands = [pltpu.with_memory_space_constraint(lax.empty((N_CHIPS,) + w.shape, BF16), pltpu.HBM) for w in ws]
    ws = [pltpu.with_memory_space_constraint(w, pltpu.HBM) for w in ws]
    hbm = pl.BlockSpec(memory_space=pltpu.HBM)
    out = pl.pallas_call(
        body, name="all_gather_start",
        out_shape=[pltpu.SemaphoreType.DMA(()), pltpu.SemaphoreType.DMA(())]
        + [pltpu.HBM(w.shape, BF16) for w in ws] + [pltpu.HBM(g.shape, BF16) for g in lands]
        + [jax.ShapeDtypeStruct((8, 128), F32)],
        in_specs=[hbm] * (2 * NBIG) + [ANY_SPEC],
        out_specs=[SEM_SPEC, SEM_SPEC] + [hbm] * (2 * NBIG) + [pl.BlockSpec(memory_space=pltpu.VMEM)],
        input_output_aliases={n: 2 + n for n in range(2 * NBIG)},
        compiler_params=pltpu.CompilerParams(has_side_effects=DATAFLOW),
    )(*ws, *lands, after)
    return out[0], out[1], out[2:2 + NBIG], out[2 + NBIG:2 + 2 * NBIG], out[-1]


def _all_gather_wait(send_sem, recv_sem, ws, lands, after):
    def body(*refs):
        w_refs, g_refs = refs[:NBIG], refs[NBIG:2 * NBIG]
        send_sem, recv_sem = refs[2 * NBIG:2 * NBIG + 2]
        x, y, c = _mesh_pos()
        _, blocks = _other_chips(x, y)
        for cp in _gather_copies(w_refs, g_refs, send_sem, recv_sem):
            cp.wait_send()
        for b in blocks:
            for cc in (0, 1):
                for w, g in zip(w_refs, g_refs):
                    pltpu.make_async_remote_copy(src_ref=_half(w, cc), dst_ref=_half(g.at[b], cc), send_sem=send_sem,
                                                 recv_sem=recv_sem, device_id=(x, y, c),
                                                 device_id_type=MESH).wait_recv()

    hbm = pl.BlockSpec(memory_space=pltpu.HBM)
    out = pl.pallas_call(
        body, name="all_gather_wait",
        out_shape=[pltpu.HBM(w.shape, BF16) for w in ws] + [pltpu.HBM(g.shape, BF16) for g in lands],
        in_specs=[hbm] * (2 * NBIG) + [SEM_SPEC, SEM_SPEC, ANY_SPEC],
        out_specs=[hbm] * (2 * NBIG),
        input_output_aliases={n: n for n in range(2 * NBIG)},
        compiler_params=pltpu.CompilerParams(has_side_effects=DATAFLOW),
    )(*ws, *lands, send_sem, recv_sem, after)
    return out[:NBIG], out[NBIG:]


def _half_rows(ref, c):
    h = ref.shape[1] // 2
    return ref.at[:, pl.ds(pl.multiple_of(c * h, 16), h)]


def _sibling_exchange(ps):
    nw = len(ps)

    def body(*refs):
        p_refs, r_refs, (send_sems, recv_sems) = refs[:nw], refs[nw:2 * nw], refs[2 * nw:]
        x, y, c = _mesh_pos()
        cps = [_remote(_half_rows(p, 1 - c), r, send_sems, recv_sems, n, (x, y, 1 - c))
               for n, (p, r) in enumerate(zip(p_refs, r_refs))]
        for cp in cps:
            cp.start()
        for cp in cps:
            cp.wait()

    return pl.pallas_call(
        body, name="grad_sibling_exchange",
        out_shape=[jax.ShapeDtypeStruct((N_CHIPS, p.shape[1] // 2, p.shape[2]), p.dtype) for p in ps],
        in_specs=[ANY_SPEC] * nw, out_specs=[ANY_SPEC] * nw,
        scratch_shapes=[pltpu.SemaphoreType.DMA((nw,)), pltpu.SemaphoreType.DMA((nw,))],
    )(*ps)


SUM_BLOCK_BYTES = 2 * 1024 * 1024


def _sum_rows(s0, s1):
    return s0 if s0 * s1 * 2 <= SUM_BLOCK_BYTES else s0 // 2


def _add_own_half(where, p, r, name):
    _, h, s1 = r.shape
    T = _sum_rows(h, s1)
    nt = h // T

    def body(where_ref, p_ref, r_ref, o_ref):
        o_ref[...] = (p_ref[...].astype(F32) + r_ref[...].astype(F32)).astype(BF16)

    return pl.pallas_call(
        body, name=name,
        grid_spec=pltpu.PrefetchScalarGridSpec(
            num_scalar_prefetch=1, grid=(N_CHIPS, nt),
            in_specs=[pl.BlockSpec((1, T, s1), lambda j, i, wh: (j, wh[0] * nt + i, 0)),
                      pl.BlockSpec((1, T, s1), lambda j, i, wh: (j, i, 0))],
            out_specs=pl.BlockSpec((1, T, s1), lambda j, i, wh: (j, i, 0))),
        out_shape=jax.ShapeDtypeStruct(r.shape, BF16), compiler_params=_params(("parallel", "parallel")),
    )(where, p, r)


def _chip_exchange(as_):
    nw = len(as_)

    def body(*refs):
        a_refs, r_refs, (send_sems, recv_sems) = refs[:nw], refs[nw:2 * nw], refs[2 * nw:]
        x, y, c = _mesh_pos()
        chips, blocks = _other_chips(x, y)
        cps = [_remote(a.at[b], r.at[k], send_sems, recv_sems, k * nw + n, (*chip, c))
               for k, (chip, b) in enumerate(zip(chips, blocks)) for n, (a, r) in enumerate(zip(a_refs, r_refs))]
        for cp in cps:
            cp.start()
        for cp in cps:
            cp.wait_recv()
        for cp in cps:
            cp.wait_send()

    return pl.pallas_call(
        body, name="grad_chip_exchange", out_shape=[jax.ShapeDtypeStruct((3,) + a.shape[1:], a.dtype) for a in as_],
        in_specs=[ANY_SPEC] * nw, out_specs=[ANY_SPEC] * nw,
        scratch_shapes=[pltpu.SemaphoreType.DMA((3 * nw,)), pltpu.SemaphoreType.DMA((3 * nw,))],
    )(*as_)


def _sum_chips(where, a, r, o, name):
    _, h, s1 = a.shape
    T = _sum_rows(h, s1)
    nt = h // T

    def body(where_ref, a_ref, r_ref, o_in, o_ref):
        acc = a_ref[0].astype(F32)
        for k in range(3):
            acc = acc + r_ref[k].astype(F32)
        o_ref[0] = acc

    return pl.pallas_call(
        body, name=name,
        grid_spec=pltpu.PrefetchScalarGridSpec(
            num_scalar_prefetch=1, grid=(nt,),
            in_specs=[pl.BlockSpec((1, T, s1), lambda i, wh: (wh[1], i, 0)),
                      pl.BlockSpec((3, T, s1), lambda i, wh: (0, i, 0)), ANY_SPEC],
            out_specs=pl.BlockSpec((1, T, s1), lambda i, wh: (0, wh[0] * nt + i, 0))),
        out_shape=jax.ShapeDtypeStruct(o.shape, F32), input_output_aliases={3: 0},
        compiler_params=_params(("parallel",)),
    )(where, a, r, o)


def _sibling_share(os_):
    nw = len(os_)

    def body(*refs):
        o_refs, (send_sems, recv_sems) = refs[nw:2 * nw], refs[2 * nw:]
        x, y, c = _mesh_pos()
        mine = lambda o, cc: _half(o.at[0], cc)
        cps = [_remote(mine(o, c), mine(o, c), send_sems, recv_sems, n, (x, y, 1 - c)) for n, o in enumerate(o_refs)]
        for cp in cps:
            cp.start()
        for n, o in enumerate(o_refs):
            _remote(mine(o, c), mine(o, 1 - c), send_sems, recv_sems, n, (x, y, 1 - c)).wait_recv()
        for cp in cps:
            cp.wait_send()

    return pl.pallas_call(
        body, name="grad_sibling_share", out_shape=[jax.ShapeDtypeStruct(o.shape, o.dtype) for o in os_],
        in_specs=[ANY_SPEC] * nw, out_specs=[ANY_SPEC] * nw,
        input_output_aliases={n: n for n in range(nw)},
        scratch_shapes=[pltpu.SemaphoreType.DMA((nw,)), pltpu.SemaphoreType.DMA((nw,))],
    )(*os_)


N_DEV = 8


def _scatter_copies(p_refs, r_refs, send_sem, recv_sem):
    x, y, c = _mesh_pos()
    cps = []
    for m in range(1, N_DEV):
        px, py, pc = x ^ (m >> 2 & 1), y ^ (m >> 1 & 1), c ^ (m & 1)
        for p, r in zip(p_refs, r_refs):
            cps.append(pltpu.make_async_remote_copy(src_ref=p.at[2 * px + py], dst_ref=r.at[m - 1], send_sem=send_sem,
                                                    recv_sem=recv_sem, device_id=(px, py, pc), device_id_type=MESH))
    return cps


def _reduce_start(ps, after, name):
    nw = len(ps)

    def body(*refs):
        p_refs, r_refs = refs[:nw], refs[nw:2 * nw]
        send_sem, recv_sem = refs[2 * nw + 1:2 * nw + 3]
        for cp in _scatter_copies(p_refs, r_refs, send_sem, recv_sem):
            cp.start()
        refs[-1][...] = jnp.zeros_like(refs[-1])

    lands = [pltpu.with_memory_space_constraint(lax.empty((N_DEV - 1,) + p.shape[1:], BF16), pltpu.HBM) for p in ps]
    ps = [pltpu.with_memory_space_constraint(p, pltpu.HBM) for p in ps]
    hbm = pl.BlockSpec(memory_space=pltpu.HBM)
    out = pl.pallas_call(
        body, name=name,
        out_shape=[pltpu.SemaphoreType.DMA(()), pltpu.SemaphoreType.DMA(())]
        + [pltpu.HBM(p.shape, BF16) for p in ps] + [pltpu.HBM(r.shape, BF16) for r in lands]
        + [jax.ShapeDtypeStruct((8, 128), F32)],
        in_specs=[hbm] * (2 * nw) + [ANY_SPEC],
        out_specs=[SEM_SPEC, SEM_SPEC] + [hbm] * (2 * nw) + [pl.BlockSpec(memory_space=pltpu.VMEM)],
        input_output_aliases={n: 2 + n for n in range(2 * nw)},
        compiler_params=pltpu.CompilerParams(has_side_effects=DATAFLOW),
    )(*ps, *lands, after)
    return out[0], out[1], out[2:2 + nw], out[2 + nw:2 + 2 * nw], out[-1]


def _reduce_wait(send_sem, recv_sem, ps, lands, after, name):
    nw = len(ps)

    def body(*refs):
        p_refs, r_refs = refs[:nw], refs[nw:2 * nw]
        send_sem, recv_sem = refs[2 * nw:2 * nw + 2]
        x, y, c = _mesh_pos()
        for cp in _scatter_copies(p_refs, r_refs, send_sem, recv_sem):
            cp.wait_send()
        for m in range(1, N_DEV):
            for p, r in zip(p_refs, r_refs):
                pltpu.make_async_remote_copy(src_ref=p.at[0], dst_ref=r.at[m - 1], send_sem=send_sem,
                                             recv_sem=recv_sem, device_id=(x, y, c), device_id_type=MESH).wait_recv()

    hbm = pl.BlockSpec(memory_space=pltpu.HBM)
    out = pl.pallas_call(
        body, name=name,
        out_shape=[pltpu.HBM(p.shape, BF16) for p in ps] + [pltpu.HBM(r.shape, BF16) for r in lands],
        in_specs=[hbm] * (2 * nw) + [SEM_SPEC, SEM_SPEC, ANY_SPEC],
        out_specs=[hbm] * (2 * nw),
        input_output_aliases={n: n for n in range(2 * nw)},
        compiler_params=pltpu.CompilerParams(has_side_effects=DATAFLOW),
    )(*ps, *lands, send_sem, recv_sem, after)
    return out[:nw], out[nw:]


SUM8_BLOCK_BYTES = 6 * 1024 * 1024


def _sum_devices(where, p, r, layer, o, name):
    _, s0, s1 = p.shape
    T = s0
    while (N_DEV - 1) * T * s1 * 2 > SUM8_BLOCK_BYTES:
        T //= 2

    def body(where_ref, p_ref, r_ref, *rest):
        me = 2 * where_ref[1] + where_ref[0]
        acc = None
        for dev in range(N_DEV):
            m = dev ^ me
            val = jnp.where(m == 0, p_ref[0], r_ref[jnp.maximum(m, 1) - 1]).astype(F32)
            acc = val if acc is None else acc + val
        rest[-1][0] = acc

    given = o is not None
    return pl.pallas_call(
        body, name=name,
        grid_spec=pltpu.PrefetchScalarGridSpec(
            num_scalar_prefetch=1, grid=(s0 // T,),
            in_specs=[pl.BlockSpec((1, T, s1), lambda i, wh: (wh[1], i, 0)),
                      pl.BlockSpec((N_DEV - 1, T, s1), lambda i, wh: (0, i, 0))] + [ANY_SPEC] * given,
            out_specs=pl.BlockSpec((1, T, s1), lambda i, wh: (layer, i, 0))),
        out_shape=jax.ShapeDtypeStruct((DEPTH, s0, s1), F32), input_output_aliases={3: 0} if given else {},
        compiler_params=_params(("parallel",)),
    )(where, p, r, *([o] if given else []))


def _all_reduce_small(sp):
    def body(sp_ref, out_ref, buf, send_sems, recv_sems):
        x, y, c = _mesh_pos()
        me = 4 * x + 2 * y + c
        buf[0] = sp_ref[...]
        cps = []
        for k in range(1, 8):
            peer = (x ^ (k >> 2 & 1), y ^ (k >> 1 & 1), c ^ (k & 1))
            cps.append(pltpu.make_async_remote_copy(src_ref=sp_ref, dst_ref=buf.at[k], send_sem=send_sems.at[k - 1],
                                                    recv_sem=recv_sems.at[k - 1], device_id=peer, device_id_type=MESH))
        for cp in cps:
            cp.start()
        for cp in cps:
            cp.wait_recv()
        for cp in cps:
            cp.wait_send()
        acc = buf[me]
        for p in range(1, 8):
            acc = acc + buf[p ^ me]
        out_ref[...] = acc

    vm = pl.BlockSpec(memory_space=pltpu.VMEM)
    return pl.pallas_call(
        body, name="all_reduce_small", out_shape=jax.ShapeDtypeStruct(sp.shape, F32),
        in_specs=[vm], out_specs=vm,
        scratch_shapes=[pltpu.VMEM((8,) + sp.shape, F32), pltpu.SemaphoreType.DMA((7,)), pltpu.SemaphoreType.DMA((7,))],
        compiler_params=_params(),
    )(sp)


def _adamw(w, g, m, v, name):
    R, C = w.shape
    T = next((t for t in (256, 128) if R % t == 0), R)

    def body(w_ref, g_ref, m_ref, v_ref, d_ref, m2_ref, v2_ref):
        gv = g_ref[...]
        m2 = ADAM_B1 * m_ref[...] + (1.0 - ADAM_B1) * gv
        v2 = ADAM_B2 * v_ref[...] + (1.0 - ADAM_B2) * (gv * gv)
        m_hat = m2 / (1.0 - ADAM_B1 ** ADAM_STEP)
        v_hat = v2 / (1.0 - ADAM_B2 ** ADAM_STEP)
        d_ref[...] = -ADAM_LR * (m_hat / (jnp.sqrt(v_hat) + ADAM_EPS) + ADAM_WD * w_ref[...])
        m2_ref[...] = m2
        v2_ref[...] = v2

    blk = pl.BlockSpec((T, C), _row)
    return pl.pallas_call(
        body, name=name, grid=(R // T,), in_specs=[blk] * 4, out_specs=[blk] * 3,
        out_shape=[jax.ShapeDtypeStruct((R, C), F32)] * 3, compiler_params=_params(("parallel",)),
    )(w, g, m, v)


def _pack_vectors(get, rel, cdw, name):
    rows = []
    for l in range(DEPTH):
        for n, r in VEC_ROWS:
            v = get(n)[l]
            rows.append(jnp.pad(v, (0, r * D - v.shape[0])).reshape(r, D))
    rows.append(jnp.pad(rel.reshape(-1), (0, D - NUM_BUCKETS * 3 * HPG)).reshape(1, D))
    rows.append(cdw.reshape(CDW_GROWS, D))

    def body(*refs):
        out_ref = refs[-1]
        out_ref[...] = jnp.zeros_like(out_ref)
        at = 0
        for ref in refs[:-1]:
            out_ref[at:at + ref.shape[0], :] = ref[...]
            at += ref.shape[0]

    return pl.pallas_call(body, name=name, out_shape=jax.ShapeDtypeStruct((SMALL_ROWS, D), F32),
                          compiler_params=_params())(*rows)


def _unpack_vectors(packed, lens):
    out = {n: [] for n, _ in VEC_ROWS}
    for l in range(DEPTH):
        at = l * VEC_LROWS
        for n, r in VEC_ROWS:
            out[n].append(packed[at:at + r].reshape(-1)[:lens[n]])
            at += r
    rel = packed[REL_ROW, :NUM_BUCKETS * 3 * HPG].reshape(NUM_BUCKETS, 3 * HPG)
    return {n: jnp.stack(v) for n, v in out.items()}, rel


INPUT_NAMES = ("x", "mem") + ("rel_bias", "norm_mix_pre", "w_in", "b_gate", "conv_dw", "conv_dw_bias", "conv_ln_g",
                              "conv_ln_b", "w_conv_out", "w_att_out", "norm_mem", "w_mem_kv", "w_mem_out", "w_out",
                              "norm_mix_post", "norm_ffn_pre", "w_ffn_in", "w_ffn_out", "norm_ffn_post")
WEIGHT_NAMES = INPUT_NAMES[2:]


def kernel(*args):
    nw = len(WEIGHT_NAMES)
    a = dict(zip(INPUT_NAMES, args[:2 + nw]))
    target = args[2 + nw]
    mom = dict(zip(WEIGHT_NAMES, args[3 + nw:3 + 2 * nw]))
    var = dict(zip(WEIGHT_NAMES, args[3 + 2 * nw:3 + 3 * nw]))
    xi, yi, ci = _mesh_pos()
    chip = 2 * xi + yi
    where = jnp.stack([ci, chip]).astype(I32)

    shards = [[a[n][l].astype(BF16) for n in BIG] for l in range(DEPTH)]
    cdw = jnp.pad(a["conv_dw"].reshape(DEPTH * KSIZE, CW // 4), ((0, CDW_ROWS - DEPTH * KSIZE), (0, 0)))
    *gathered0, gcdw = _all_gather(shards[0], cdw)
    in_flight = _all_gather_start(shards[1], gathered0[0])
    gcdw = lax.dynamic_update_slice(gcdw, cdw[None], (chip, 0, 0))
    conv_dw = gcdw[:, :DEPTH * KSIZE].reshape(N_CHIPS, DEPTH, KSIZE, CW // 4).transpose(1, 2, 0, 3)
    conv_dw = jnp.pad(conv_dw.reshape(DEPTH, KSIZE, CW), ((0, 0), (0, 1), (0, 0)))
    gmix = [a["norm_mix_pre"][l][None, :] for l in range(DEPTH)]
    gmix[0] = gmix[0] + in_flight[4][0, 0]

    def layer_weights(l, gathered, own):
        w = {"conv_dw": conv_dw[l]}
        for n, g, s in zip(BIG, gathered, own):
            (s0, s1), axis = SHARD[n]
            blk = lax.dynamic_update_slice(g, s[None], (chip, 0, 0))
            w[n] = blk.reshape(N_CHIPS * s0, s1) if axis == 0 else blk.transpose(1, 0, 2).reshape(s0, N_CHIPS * s1)
        w["w_in"] = _z_cols_from_ref(w["w_in"])
        for n, _ in VEC_ROWS:
            w[n] = a[n][l][None, :]
        return w

    def layer1(x):
        send_sem, recv_sem, thru, lands, _ = in_flight
        own, gathered1 = _all_gather_wait(send_sem, recv_sem, thru, lands, x)
        return layer_weights(1, gathered1, own)

    def by_chip(layer_grads, names):
        out = []
        for n in names:
            (s0, s1), axis = SHARD[n]
            g = _ref_cols_from_z(layer_grads[n]) if n == "w_in" else layer_grads[n]
            g = g.reshape(N_CHIPS, s0, s1) if axis == 0 else g.reshape(s0, N_CHIPS, s1).transpose(1, 0, 2)
            out.append(g.astype(BF16))
        return out

    early = ("w_ffn_in", "w_ffn_out")
    late = tuple(n for n in BIG if n not in early)
    scattering = {}

    def on_grads(l, layer_grads):
        scattering["l1"] = _reduce_start(by_chip(layer_grads, BIG), layer_grads["w_in"], "grad_reduce_start_l1")
        return scattering["l1"][4]

    def on_last_ffn_grads(layer_grads):
        scattering["ffn"] = _reduce_start(by_chip(layer_grads, early), layer_grads["w_ffn_in"], "grad_reduce_start_ffn")
        return scattering["ffn"][4]

    loss_part, gx, grads, drel = _local_step(a["x"][0], a["mem"][0], target[0], a["rel_bias"],
                                             [lambda x: layer_weights(0, gathered0, shards[0]), layer1], gmix,
                                             on_grads, on_last_ffn_grads)
    loss = lax.psum(loss_part, ("x", "y", "c"))

    reduced = {}
    send_sem, recv_sem, thru, lands, _ = scattering["l1"]
    sent, landed = _reduce_wait(send_sem, recv_sem, thru, lands, gx, "grad_reduce_wait_l1")
    for n, p, r in zip(BIG, sent, landed):
        reduced[n] = _sum_devices(where, p, r, 1, None, "grad_sum_devices_l1_" + n)
    send_sem, recv_sem, thru, lands, _ = scattering["ffn"]
    sent, landed = _reduce_wait(send_sem, recv_sem, thru, lands, gx, "grad_reduce_wait_ffn")
    for n, p, r in zip(early, sent, landed):
        reduced[n] = _sum_devices(where, p, r, 0, reduced[n], "grad_sum_devices_l0_" + n)
    packed = by_chip(grads[0], late)
    from_sibling = _sibling_exchange(packed)
    chip_sums = [_add_own_half(where, p, r, "grad_add_sibling_" + n) for n, p, r in zip(late, packed, from_sibling)]
    from_chips = _chip_exchange(chip_sums)
    shared = _sibling_share([_sum_chips(where, s, r, reduced[n], "grad_sum_chips_" + n)
                             for n, s, r in zip(late, chip_sums, from_chips)])
    reduced.update(zip(late, shared))
    reduced = [reduced[n] for n in BIG]

    gvec = _all_reduce_small(_pack_vectors(
        lambda n: jnp.stack([grads[l][n][0] for l in range(DEPTH)]), drel[:, :3 * HPG],
        jnp.stack([grads[l]["conv_dw"] for l in range(DEPTH)]), "pack_vector_grads"))
    lens = {n: a[n].shape[1] for n, _ in VEC_ROWS}
    g_vec, g_rel = _unpack_vectors(gvec, lens)
    g_cdw = lax.dynamic_slice_in_dim(gvec[CDW_ROW:CDW_ROW + CDW_GROWS].reshape(DEPTH, KSIZE, CW), chip * (CW // 4),
                                     CW // 4, axis=2)

    grad, delta, new_m, new_v = {}, {}, {}, {}
    for n, g in zip(BIG, reduced):
        shape = a[n].shape
        flat2 = lambda t: t.reshape(shape[0] * shape[1], shape[2])
        d, m2, v2 = _adamw(flat2(a[n]), flat2(g), flat2(mom[n]), flat2(var[n]), "adamw_" + n)
        grad[n], delta[n], new_m[n], new_v[n] = g, d.reshape(shape), m2.reshape(shape), v2.reshape(shape)
    shape = a["conv_dw"].shape
    flat2 = lambda t: t.reshape(shape[0] * shape[1], shape[2])
    d, m2, v2 = _adamw(flat2(a["conv_dw"]), flat2(g_cdw), flat2(mom["conv_dw"]), flat2(var["conv_dw"]), "adamw_conv_dw")
    grad["conv_dw"], delta["conv_dw"], new_m["conv_dw"], new_v["conv_dw"] = (
        g_cdw, d.reshape(shape), m2.reshape(shape), v2.reshape(shape))
    zero_cdw = jnp.zeros((DEPTH, KSIZE, CW), F32)
    pk = lambda src, name: _pack_vectors(lambda n: src[n], src["rel_bias"], zero_cdw, name)
    d, m2, v2 = _adamw(pk(a, "pack_vector_w"), gvec, pk(mom, "pack_vector_m"), pk(var, "pack_vector_v"),
                       "adamw_vectors")
    for src, dst in ((d, delta), (m2, new_m), (v2, new_v)):
        vec, rel = _unpack_vectors(src, lens)
        dst.update(vec)
        dst["rel_bias"] = rel
    grad.update(g_vec)
    grad["rel_bias"] = g_rel

    outs = [loss, gx[None]]
    for group in (grad, delta, new_m, new_v):
        outs += [group[n] for n in WEIGHT_NAMES]
    return tuple(outs)
```

```python
import functools
import math

import jax
import jax.numpy as jnp
from jax import lax
from jax.experimental import pallas as pl
from jax.experimental.pallas import tpu as pltpu

F32 = jnp.float32
BF16 = jnp.bfloat16
I32 = jnp.int32

D = 1024
DEPTH = 2
N_MEM = 256
CW = 512
KSIZE = 31
PAD = KSIZE // 2
DILS = (1, 4, 16)
RADIUS = 64
HPG = 4
HD = 64
GW = HPG * HD
MH = 4
MHD = 128
MW = MH * MHD
FH = 2816
NIN = 6912
C1 = 2 * CW
R_ATT = C1
R_MEM = R_ATT + 9 * GW
R_GATE = R_MEM + MW
Z_GATE = 0
Z_CONV = 3 * D
Z_MEM = Z_CONV + C1
Z_ATT = Z_MEM + MW
NUM_BUCKETS = 32
MAX_DISTANCE = 1024
RMS_EPS = 1e-6
LN_EPS = 1e-5
NEG_INF = -1e30
ATT_SCALE = HD ** -0.5
MEM_SCALE = MHD ** -0.5

ADAM_LR = 0.001
ADAM_B1 = 0.9
ADAM_B2 = 0.999
ADAM_EPS = 1e-08
ADAM_WD = 0.01
ADAM_STEP = 10

VMEM_LIMIT_BYTES = 56 * 1024 * 1024
ATT_QB = 128
ATT_TB = 16 * ATT_QB

MESH = pl.DeviceIdType.MESH


def _params(sem=None):
    return pltpu.CompilerParams(dimension_semantics=sem, vmem_limit_bytes=VMEM_LIMIT_BYTES)


def _sigmoid(v):
    return 1.0 / (1.0 + jnp.exp(-v))


def _dot(a, b):
    return jnp.dot(a, b, preferred_element_type=F32)


def _dot_nt(a, b):
    return lax.dot_general(a, b, (((1,), (1,)), ((), ())), preferred_element_type=F32)


def _dot_tn(a, b):
    return lax.dot_general(a, b, (((0,), (0,)), ((), ())), preferred_element_type=F32)


def _rms_fwd_val(v, g):
    r = lax.rsqrt(jnp.mean(v * v, axis=-1, keepdims=True) + RMS_EPS)
    return v * r * g


def _rms_bwd_val(v, g, dy):
    r = lax.rsqrt(jnp.mean(v * v, axis=-1, keepdims=True) + RMS_EPS)
    vh = v * r
    dvh = dy * g
    dv = r * (dvh - vh * jnp.mean(dvh * vh, axis=-1, keepdims=True))
    return dv, dy * vh


def _row(i):
    return (i, 0)


def _fixed(*_):
    return (0, 0)


def _mm_nn(a, b, tm, out_dtype, name):
    M, K = a.shape
    N = b.shape[1]

    def body(a_ref, b_ref, o_ref):
        o_ref[...] = _dot(a_ref[...], b_ref[...]).astype(out_dtype)

    return pl.pallas_call(
        body, name=name, grid=(M // tm,),
        in_specs=[pl.BlockSpec((tm, K), _row), pl.BlockSpec((K, N), _fixed, pipeline_mode=pl.Buffered(1))],
        out_specs=pl.BlockSpec((tm, N), _row),
        out_shape=jax.ShapeDtypeStruct((M, N), out_dtype),
        compiler_params=_params(("parallel",)),
    )(a, b)


def _mm_nt_rms_bwd(a, b, x, g, dres, tm, name):
    M, N = a.shape

    def body(a_ref, b_ref, x_ref, g_ref, dres_ref, dx_ref, dg_ref):
        @pl.when(pl.program_id(0) == 0)
        def _():
            dg_ref[...] = jnp.zeros_like(dg_ref)

        dv, dgr = _rms_bwd_val(x_ref[...], g_ref[...], _dot_nt(a_ref[...], b_ref[...]))
        dx_ref[...] = dres_ref[...] + dv
        dg_ref[...] += jnp.sum(dgr, axis=0, keepdims=True)

    rows = pl.BlockSpec((tm, D), _row)
    return pl.pallas_call(
        body, name=name, grid=(M // tm,),
        in_specs=[pl.BlockSpec((tm, N), _row), pl.BlockSpec((D, N), _fixed, pipeline_mode=pl.Buffered(1)), rows,
                  pl.BlockSpec((1, D), _fixed), rows],
        out_specs=[rows, pl.BlockSpec((1, D), _fixed)],
        out_shape=[jax.ShapeDtypeStruct((M, D), F32), jax.ShapeDtypeStruct((1, D), F32)],
        compiler_params=_params(("arbitrary",)),
    )(a, b, x, g, dres)


def _mm_tn(a, b, ts, tn, name):
    S, K = a.shape
    N = b.shape[1]

    def body(a_ref, b_ref, o_ref):
        @pl.when(pl.program_id(1) == 0)
        def _():
            o_ref[...] = jnp.zeros_like(o_ref)

        o_ref[...] += _dot_tn(a_ref[...], b_ref[...])

    return pl.pallas_call(
        body, name=name, grid=(N // tn, S // ts),
        in_specs=[pl.BlockSpec((ts, K), lambda j, s: (s, 0)), pl.BlockSpec((ts, tn), lambda j, s: (s, j))],
        out_specs=pl.BlockSpec((K, tn), lambda j, s: (0, j)),
        out_shape=jax.ShapeDtypeStruct((K, N), F32),
        compiler_params=_params(("parallel", "arbitrary")),
    )(a, b)


def _rms_h(x, g, name):
    S = x.shape[0]
    T = 512

    def body(x_ref, g_ref, h_ref):
        h_ref[...] = _rms_fwd_val(x_ref[...], g_ref[...]).astype(BF16)

    return pl.pallas_call(
        body, name=name, grid=(S // T,),
        in_specs=[pl.BlockSpec((T, D), _row), pl.BlockSpec((1, D), _fixed)],
        out_specs=pl.BlockSpec((T, D), _row),
        out_shape=jax.ShapeDtypeStruct((S, D), BF16),
        compiler_params=_params(("parallel",)),
    )(x, g)


CONV_T = 256
CONV_HALO = 16
CONV_RC = 32


def _halo_specs(T, halo, S, width, col):
    per = T // halo
    last = S // halo - 1
    return [
        pl.BlockSpec((T, width), lambda i: (i, col)),
        pl.BlockSpec((halo, width), lambda i: (jnp.maximum(i * per - 1, 0), col)),
        pl.BlockSpec((halo, width), lambda i: (jnp.minimum((i + 1) * per, last), col)),
    ]


def _glu(zb):
    zb = zb.astype(F32)
    return zb[:, :CW] * _sigmoid(zb[:, CW:])


CONV_EXT = CONV_T + 2 * CONV_HALO
SUBLANES = 8


def _fill_shifted(sh_ref, ext_ref, cur, prev, nxt):
    T, halo = CONV_T, CONV_HALO
    i = pl.program_id(0)
    n = pl.num_programs(0)
    ext_ref[0:halo, :] = jnp.where(i > 0, prev, 0.0)
    ext_ref[halo:halo + T, :] = cur
    ext_ref[halo + T:CONV_EXT, :] = jnp.where(i < n - 1, nxt, 0.0)
    ext_ref[CONV_EXT:CONV_EXT + SUBLANES, :] = jnp.zeros((SUBLANES, CW), F32)
    for b in range(SUBLANES):
        sh_ref[b] = ext_ref[b:b + CONV_EXT, :]


def _window(sh_ref, start, rows):
    b = start % SUBLANES
    return sh_ref[b, start - b:start - b + rows, :]


def _shifted_scratch():
    return [pltpu.VMEM((CONV_EXT + SUBLANES, CW), F32), pltpu.VMEM((SUBLANES, CONV_EXT, CW), F32)]


def _conv_fwd(z, wdw, bdw, lng, lnb, name):
    S = z.shape[0]
    T, HL, RC = CONV_T, CONV_HALO, CONV_RC

    def body(cur_ref, prev_ref, next_ref, w_ref, b_ref, g_ref, bb_ref, yc_ref, act_ref, ext_ref, sh_ref):
        _fill_shifted(sh_ref, ext_ref, _glu(cur_ref[...]), _glu(prev_ref[...]), _glu(next_ref[...]))
        for c in range(T // RC):
            acc = jnp.zeros((RC, CW), F32)
            for k in range(KSIZE):
                acc = acc + w_ref[k:k + 1, :] * _window(sh_ref, c * RC + k + HL - PAD, RC)
            yc = acc + b_ref[...]
            yc_ref[c * RC:(c + 1) * RC, :] = yc
            mu = jnp.mean(yc, axis=-1, keepdims=True)
            xc = yc - mu
            ln = xc * lax.rsqrt(jnp.mean(xc * xc, axis=-1, keepdims=True) + LN_EPS) * g_ref[...] + bb_ref[...]
            act_ref[c * RC:(c + 1) * RC, :] = (ln * _sigmoid(ln)).astype(BF16)

    return pl.pallas_call(
        body, name=name, grid=(S // T,),
        in_specs=_halo_specs(T, HL, S, C1, Z_CONV // C1) + [pl.BlockSpec((32, CW), _fixed)]
        + [pl.BlockSpec((1, CW), _fixed)] * 3,
        out_specs=[pl.BlockSpec((T, CW), _row), pl.BlockSpec((T, CW), _row)],
        out_shape=[jax.ShapeDtypeStruct((S, CW), F32), jax.ShapeDtypeStruct((S, CW), BF16)],
        scratch_shapes=_shifted_scratch(),
        compiler_params=_params(("parallel",)),
    )(z, z, z, wdw, bdw, lng, lnb)


def _conv_bwd_ln(yc, dact, lng, lnb, name):
    S = yc.shape[0]
    T = 512

    def body(yc_ref, da_ref, g_ref, b_ref, dyc_ref, dg_ref, db_ref, dbias_ref):
        yc_v = yc_ref[...]
        mu = jnp.mean(yc_v, axis=-1, keepdims=True)
        xc = yc_v - mu
        r = lax.rsqrt(jnp.mean(xc * xc, axis=-1, keepdims=True) + LN_EPS)
        yn = xc * r
        ln = yn * g_ref[...] + b_ref[...]
        sg = _sigmoid(ln)
        dln = da_ref[...].astype(F32) * (sg * (1.0 + ln * (1.0 - sg)))
        dyn = dln * g_ref[...]
        dyc = r * (dyn - jnp.mean(dyn, axis=-1, keepdims=True) - yn * jnp.mean(dyn * yn, axis=-1, keepdims=True))
        dyc_ref[...] = dyc

        @pl.when(pl.program_id(0) == 0)
        def _():
            dg_ref[...] = jnp.zeros_like(dg_ref)
            db_ref[...] = jnp.zeros_like(db_ref)
            dbias_ref[...] = jnp.zeros_like(dbias_ref)

        dg_ref[...] += jnp.sum(dln * yn, axis=0, keepdims=True)
        db_ref[...] += jnp.sum(dln, axis=0, keepdims=True)
        dbias_ref[...] += jnp.sum(dyc, axis=0, keepdims=True)

    vec = pl.BlockSpec((1, CW), _fixed)
    return pl.pallas_call(
        body, name=name, grid=(S // T,),
        in_specs=[pl.BlockSpec((T, CW), _row), pl.BlockSpec((T, CW), _row), vec, vec],
        out_specs=[pl.BlockSpec((T, CW), _row), vec, vec, vec],
        out_shape=[jax.ShapeDtypeStruct((S, CW), F32)] + [jax.ShapeDtypeStruct((1, CW), F32)] * 3,
        compiler_params=_params(("arbitrary",)),
    )(yc, dact, lng, lnb)


def _conv_bwd_dw(z, dyc, wdw, dz, name):
    S = z.shape[0]
    T, HL, RC = CONV_T, CONV_HALO, CONV_RC

    def body(zc_ref, zp_ref, zn_ref, dc_ref, dp_ref, dn_ref, w_ref, dz_in, dz_ref, dw_ref, uext_ref, ush_ref,
             dext_ref, dsh_ref, dwacc_ref):
        _fill_shifted(ush_ref, uext_ref, _glu(zc_ref[...]), _glu(zp_ref[...]), _glu(zn_ref[...]))
        _fill_shifted(dsh_ref, dext_ref, dc_ref[...], dp_ref[...], dn_ref[...])

        @pl.when(pl.program_id(0) == 0)
        def _():
            dwacc_ref[...] = jnp.zeros_like(dwacc_ref)

        for c in range(T // RC):
            dcur = dc_ref[c * RC:(c + 1) * RC, :]
            du = jnp.zeros((RC, CW), F32)
            for k in range(KSIZE):
                du = du + w_ref[k:k + 1, :] * _window(dsh_ref, c * RC + HL + PAD - k, RC)
                prod = dcur * _window(ush_ref, c * RC + k + HL - PAD, RC)
                dwacc_ref[k] += jnp.sum(prod.reshape(RC // SUBLANES, SUBLANES, CW), axis=0)
            zc = zc_ref[c * RC:(c + 1) * RC, :].astype(F32)
            a, gt = zc[:, :CW], zc[:, CW:]
            sg = _sigmoid(gt)
            dz_ref[c * RC:(c + 1) * RC, 0:CW] = (du * sg).astype(BF16)
            dz_ref[c * RC:(c + 1) * RC, CW:C1] = (du * a * sg * (1.0 - sg)).astype(BF16)

        @pl.when(pl.program_id(0) == pl.num_programs(0) - 1)
        def _():
            dw_ref[...] = jnp.sum(dwacc_ref[...], axis=1)

    return pl.pallas_call(
        body, name=name, grid=(S // T,),
        in_specs=_halo_specs(T, HL, S, C1, Z_CONV // C1) + _halo_specs(T, HL, S, CW, 0)
        + [pl.BlockSpec((32, CW), _fixed), pl.BlockSpec(memory_space=pl.ANY)],
        out_specs=[pl.BlockSpec((T, C1), lambda i: (i, Z_CONV // C1)), pl.BlockSpec((32, CW), _fixed)],
        out_shape=[jax.ShapeDtypeStruct(dz.shape, BF16), jax.ShapeDtypeStruct((32, CW), F32)],
        input_output_aliases={7: 0},
        scratch_shapes=_shifted_scratch() + _shifted_scratch() + [pltpu.VMEM((32, SUBLANES, CW), F32)],
        compiler_params=_params(("arbitrary",)),
    )(z, z, z, dyc, dyc, dyc, wdw, dz)


def _t5_bucket(rel):
    nb = NUM_BUCKETS // 2
    max_exact = nb // 2
    ret = jnp.where(rel > 0, nb, 0)
    n = jnp.abs(rel)
    nf = jnp.maximum(n, 1).astype(F32)
    large = max_exact + (jnp.log(nf / max_exact) / math.log(MAX_DISTANCE / max_exact)
                         * (nb - max_exact)).astype(I32)
    large = jnp.minimum(large, nb - 1)
    return ret + jnp.where(n < max_exact, n, large)


def _offsets_qk(nq, nk, shift):
    return lax.broadcasted_iota(I32, (nq, nk), 1) + shift - lax.broadcasted_iota(I32, (nq, nk), 0)


def _bias_table(bk, rb_ref, col, off):
    acc = jnp.zeros(bk.shape, F32)
    for b in range(NUM_BUCKETS):
        acc = jnp.where(bk == b, rb_ref[b, col], acc)
    return jnp.where(jnp.abs(off) <= RADIUS, acc, NEG_INF)


def _to_halves(scr, row0, val):
    rows = val.shape[0]
    v = val.astype(F32)
    scr[0, row0:row0 + rows, :] = v[:, :128]
    scr[1, row0:row0 + rows, :] = v[:, 128:]


ATT_FWD_GROUP = 2
ATT_BWD_GROUP = 1


def _att_units(d, fn, group):
    nj = ATT_TB // (ATT_QB * d)
    if nj == 1:
        def trip(t, c):
            r0 = pl.multiple_of(t * 8, 8)
            for u in range(0, 8, group):
                fn([(r0 + u + v, 0) for v in range(group)])
            return c

        lax.fori_loop(0, d // 8, trip, 0)
        return
    for r in range(d):
        def step(t, c, r=r):
            fn([(r, t * group + u) for u in range(group)])
            return c

        lax.fori_loop(0, nj // group, step, 0)


def _unit_row(r, j, d):
    if isinstance(j, int):
        return j * ATT_QB * d + r
    return pl.multiple_of(j * (ATT_QB * d), ATT_QB) + r


def _att_fwd(z, rel_bias, g, name):
    S = z.shape[0]
    d = DILS[g]
    TB, QB = ATT_TB, ATT_QB
    H = RADIUS * d
    L = S // d
    cq = (Z_ATT + 3 * GW * g) // GW
    ck, cv = cq + 1, cq + 2
    bk = _t5_bucket(_offsets_qk(QB, 2 * QB, -RADIUS) * d)

    def body(rb_ref, bk_ref, q_ref, kc_ref, kp_ref, kn_ref, vc_ref, vp_ref, vn_ref, o_ref, l_ref,
             qs, ks, vs, os_, ls, bias):
        i = pl.program_id(0)

        @pl.when(i == 0)
        def _():
            off = _offsets_qk(QB, 2 * QB, -RADIUS)
            for h in range(HPG):
                bias[h] = _bias_table(bk_ref[...], rb_ref, g * HPG + h, off)

        _to_halves(qs, 0, q_ref[...].astype(F32) * ATT_SCALE)
        for scr, p_ref, c_ref, n_ref in ((ks, kp_ref, kc_ref, kn_ref), (vs, vp_ref, vc_ref, vn_ref)):
            _to_halves(scr, 0, p_ref[...])
            _to_halves(scr, H, c_ref[...])
            _to_halves(scr, H + TB, n_ref[...])

        lo = lax.broadcasted_iota(I32, (QB, 128), 1) < HD

        def units(rjs):
            work = []
            for r, j in rjs:
                row = _unit_row(r, j, d)
                km = lax.broadcasted_iota(I32, (1, 2 * QB), 1) + (i * (TB // d) + j * QB - RADIUS)
                edge = jnp.where(jnp.where(km >= 0, km, L) < L, 0.0, NEG_INF)
                for hf in (0, 1):
                    q2 = qs[hf, pl.ds(row, QB, stride=d), :]
                    k2 = ks[hf, pl.ds(row, 2 * QB, stride=d), :].astype(BF16)
                    v2 = vs[hf, pl.ds(row, 2 * QB, stride=d), :].astype(BF16)
                    qq = jnp.concatenate([jnp.where(lo, q2, 0.0), jnp.where(lo, 0.0, q2)], axis=0).astype(BF16)
                    work.append((row, hf, edge, k2, v2, qq))
            scores = [_dot_nt(qq, k2) for (_, _, _, k2, _, qq) in work]
            probs = []
            for (row, hf, edge, *_), ss in zip(work, scores):
                es, stats = [], []
                for hh in (0, 1):
                    s = ss[hh * QB:(hh + 1) * QB] + bias[2 * hf + hh] + edge
                    m = jnp.max(s, axis=-1, keepdims=True)
                    e = jnp.exp(s - m)
                    den = jnp.sum(e, axis=-1, keepdims=True)
                    es.append(e.astype(BF16))
                    stats.append((1.0 / den, m + jnp.log(den)))
                probs.append((jnp.concatenate(es, axis=0), stats))
            for (row, hf, _, _, v2, _), (ee, stats) in zip(work, probs):
                oo = _dot(ee, v2)
                os_[hf, pl.ds(row, QB, stride=d), :] = jnp.where(lo, oo[:QB] * stats[0][0], oo[QB:] * stats[1][0])
                ls[hf, pl.ds(row, QB, stride=d), :] = jnp.where(lo, stats[0][1], stats[1][1])

        _att_units(d, units, ATT_FWD_GROUP)
        for hf in (0, 1):
            o_ref[:, hf * 128:(hf + 1) * 128] = os_[hf].astype(BF16)
            l_ref[:, hf * 128:(hf + 1) * 128] = ls[hf]

    def halo3(col):
        c, p, n = _halo_specs(TB, H, S, GW, col)
        return [c, p, n]

    return pl.pallas_call(
        body, name=name, grid=(S // TB,),
        in_specs=[pl.BlockSpec(memory_space=pltpu.SMEM), pl.BlockSpec((QB, 2 * QB), _fixed),
                  pl.BlockSpec((TB, GW), lambda i: (i, cq))] + halo3(ck) + halo3(cv),
        out_specs=[pl.BlockSpec((TB, GW), _row), pl.BlockSpec((TB, GW), _row)],
        out_shape=[jax.ShapeDtypeStruct((S, GW), BF16), jax.ShapeDtypeStruct((S, GW), F32)],
        scratch_shapes=[pltpu.VMEM((2, TB, 128), F32), pltpu.VMEM((2, TB + 2 * H, 128), F32),
                        pltpu.VMEM((2, TB + 2 * H, 128), F32), pltpu.VMEM((2, TB, 128), F32),
                        pltpu.VMEM((2, TB, 128), F32), pltpu.VMEM((HPG, QB, 2 * QB), F32)],
        compiler_params=_params(("arbitrary",)),
    )(rel_bias, bk, z, z, z, z, z, z, z)


def _att_combine(os3, ls3, name):
    S = os3[0].shape[0]
    T = 1024

    def body(o1, o2, o3, l1, l2, l3, o_ref, l_ref):
        lv = [l1[...], l2[...], l3[...]]
        m = jnp.maximum(jnp.maximum(lv[0], lv[1]), lv[2])
        e = [jnp.exp(v - m) for v in lv]
        den = e[0] + e[1] + e[2]
        acc = jnp.zeros_like(m)
        for ev, o in zip(e, (o1, o2, o3)):
            acc = acc + (ev / den) * o[...].astype(F32)
        o_ref[...] = acc.astype(BF16)
        l_ref[...] = m + jnp.log(den)

    blk = pl.BlockSpec((T, GW), _row)
    return pl.pallas_call(
        body, name=name, grid=(S // T,), in_specs=[blk] * 6, out_specs=[blk, blk],
        out_shape=[jax.ShapeDtypeStruct((S, GW), BF16), jax.ShapeDtypeStruct((S, GW), F32)],
        compiler_params=_params(("parallel",)),
    )(*os3, *ls3)


def _att_prep(do, o, lse, name):
    S = do.shape[0]
    T = 1024

    def body(do_ref, o_ref, l_ref, out_ref):
        prod = do_ref[...].astype(F32) * o_ref[...].astype(F32)
        dd = [jnp.broadcast_to(jnp.sum(prod[:, h * HD:(h + 1) * HD], axis=-1, keepdims=True), (T, HD))
              for h in range(HPG)]
        lane = lax.broadcasted_iota(I32, (T, GW), 1)
        out_ref[...] = jnp.where(lane % HD < HD // 2, l_ref[...], jnp.concatenate(dd, axis=-1))

    blk = pl.BlockSpec((T, GW), _row)
    return pl.pallas_call(
        body, name=name, grid=(S // T,), in_specs=[blk] * 3, out_specs=blk,
        out_shape=jax.ShapeDtypeStruct((S, GW), F32), compiler_params=_params(("parallel",)),
    )(do, o, lse)


def _att_bwd(z, rel_bias, do, ld, dz, g, name):
    S = z.shape[0]
    d = DILS[g]
    TB, QB = ATT_TB, ATT_QB
    H = RADIUS * d
    L = S // d
    E = TB + 2 * H
    cq = (Z_ATT + 3 * GW * g) // GW
    ck, cv = cq + 1, cq + 2
    bk_a = _t5_bucket(_offsets_qk(QB, 2 * QB, -RADIUS) * d)
    bk_b = _t5_bucket(-_offsets_qk(QB, 2 * QB, -RADIUS) * d)

    def body(rb_ref, bka_ref, bkb_ref, *refs):
        ins, (dz_ref, db_ref) = refs[:15], refs[16:18]
        qs, ks, vs, dos, ls, dqs, dks, dvs, bias_a, bias_b, dbias = refs[18:]
        i = pl.program_id(0)
        n = pl.num_programs(0)

        @pl.when(i == 0)
        def _():
            off = _offsets_qk(QB, 2 * QB, -RADIUS)
            for h in range(HPG):
                bias_a[h] = _bias_table(bka_ref[...], rb_ref, g * HPG + h, off)
                bias_b[h] = _bias_table(bkb_ref[...], rb_ref, g * HPG + h, off)
            dbias[...] = jnp.zeros_like(dbias)

        for a, scr in enumerate((qs, ks, vs, dos, ls)):
            c_ref, p_ref, n_ref = ins[3 * a:3 * a + 3]
            pre = (lambda v: v.astype(F32) * ATT_SCALE) if a == 0 else (lambda v: v)
            _to_halves(scr, 0, pre(p_ref[...]))
            _to_halves(scr, H, pre(c_ref[...]))
            _to_halves(scr, H + TB, pre(n_ref[...]))

        lo = lax.broadcasted_iota(I32, (QB, 128), 1) < HD

        def split(v):
            return jnp.concatenate([jnp.where(lo, v, 0.0), jnp.where(lo, 0.0, v)], axis=0).astype(BF16)

        def halves(v):
            return v[:QB], v[QB:]

        def units(rjs):
            work = []
            for r, j in rjs:
                row = _unit_row(r, j, d)
                cur = row + H
                m0 = i * (TB // d) + j * QB - RADIUS
                km = lax.broadcasted_iota(I32, (1, 2 * QB), 1) + m0
                edge_a = jnp.where(jnp.where(km >= 0, km, L) < L, 0.0, NEG_INF)
                for hf in (0, 1):
                    ld = lambda scr, at, nrow: scr[hf, pl.ds(at, nrow, stride=d), :]
                    w = dict(row=row, hf=hf, edge=edge_a, l_c=ld(ls, cur, QB), l_t=ld(ls, row, 2 * QB).T)
                    for nm, scr in (("q", qs), ("k", ks), ("v", vs), ("do", dos)):
                        w[nm + "_c"] = split(ld(scr, cur, QB))
                        w[nm + "_e"] = ld(scr, row, 2 * QB).astype(BF16)
                    work.append(w)
            for w in work:
                w["s"] = halves(_dot_nt(w["q_c"], w["k_e"]))
                w["dp"] = halves(_dot_nt(w["do_c"], w["v_e"]))
                w["s2"] = halves(_dot_nt(w["k_c"], w["q_e"]))
                w["dp2"] = halves(_dot_nt(w["v_c"], w["do_e"]))
            for w in work:
                w["ds"], w["p2"], w["ds2"] = [], [], []
                for hh in (0, 1):
                    h, c0 = 2 * w["hf"] + hh, HD * hh
                    l_c, l_t = w["l_c"], w["l_t"]
                    p = jnp.exp(w["s"][hh] + bias_a[h] + w["edge"] - l_c[:, c0:c0 + 1])
                    ds = p * (w["dp"][hh] - l_c[:, c0 + HD // 2:c0 + HD // 2 + 1])
                    dbias[h] += ds
                    p2 = jnp.exp(w["s2"][hh] + bias_b[h] + w["edge"] - l_t[c0:c0 + 1, :])
                    ds2 = p2 * (w["dp2"][hh] - l_t[c0 + HD // 2:c0 + HD // 2 + 1, :])
                    w["ds"].append(ds.astype(BF16))
                    w["p2"].append(p2.astype(BF16))
                    w["ds2"].append(ds2.astype(BF16))
            for w in work:
                at = pl.ds(w["row"], QB, stride=d)
                both = lambda pair, rhs: halves(_dot(jnp.concatenate(pair, axis=0), rhs))
                dq = both(w["ds"], w["k_e"])
                dqs[w["hf"], at, :] = jnp.where(lo, dq[0], dq[1]) * ATT_SCALE
                dv = both(w["p2"], w["do_e"])
                dvs[w["hf"], at, :] = jnp.where(lo, dv[0], dv[1])
                dk = both(w["ds2"], w["q_e"])
                dks[w["hf"], at, :] = jnp.where(lo, dk[0], dk[1])

        _att_units(d, units, ATT_BWD_GROUP)
        for a, scr in enumerate((dqs, dks, dvs)):
            for hf in (0, 1):
                dz_ref[:, a * GW + hf * 128:a * GW + (hf + 1) * 128] = scr[hf].astype(BF16)

        @pl.when(i == n - 1)
        def _():
            rows = lax.broadcasted_iota(I32, (NUM_BUCKETS, 128), 0)
            lanes = lax.broadcasted_iota(I32, (NUM_BUCKETS, 128), 1)
            out = jnp.zeros((NUM_BUCKETS, 128), F32)
            bk = bka_ref[...]
            for h in range(HPG):
                acc = dbias[h]
                for b in range(NUM_BUCKETS):
                    tot = jnp.sum(jnp.sum(jnp.where(bk == b, acc, 0.0), axis=1, keepdims=True), axis=0, keepdims=True)
                    out = out + jnp.where((rows == b) & (lanes == h), tot, 0.0)
            db_ref[...] = out

    def halo3(col, width=GW):
        return _halo_specs(TB, H, S, width, col)

    one = pl.Buffered(1)

    def single(specs):
        return [pl.BlockSpec(s.block_shape, s.index_map, pipeline_mode=one) for s in specs]

    in_specs = ([pl.BlockSpec(memory_space=pltpu.SMEM), pl.BlockSpec((QB, 2 * QB), _fixed),
                 pl.BlockSpec((QB, 2 * QB), _fixed)]
                + single(halo3(cq) + halo3(ck) + halo3(cv) + halo3(0) + halo3(0))
                + [pl.BlockSpec(memory_space=pl.ANY)])
    return pl.pallas_call(
        body, name=name, grid=(S // TB,), in_specs=in_specs,
        out_specs=[pl.BlockSpec((TB, 3 * GW), lambda i: (i, cq // 3)), pl.BlockSpec((NUM_BUCKETS, 128), _fixed)],
        out_shape=[jax.ShapeDtypeStruct(dz.shape, BF16), jax.ShapeDtypeStruct((NUM_BUCKETS, 128), F32)],
        input_output_aliases={18: 0},
        scratch_shapes=[pltpu.VMEM((2, E, 128), F32)] * 5 + [pltpu.VMEM((2, TB, 128), F32)] * 3
        + [pltpu.VMEM((HPG, QB, 2 * QB), F32)] * 3,
        compiler_params=_params(("arbitrary",)),
    )(rel_bias, bk_a, bk_b, z, z, z, z, z, z, z, z, z, do, do, do, ld, ld, ld, dz)


def _memkv_fwd(mem, gm, wkv, name):
    def body(m_ref, g_ref, w_ref, hm_ref, kv_ref):
        hm = _rms_fwd_val(m_ref[...], g_ref[...]).astype(BF16)
        hm_ref[...] = hm
        kv_ref[...] = _dot(hm, w_ref[...]).astype(BF16)

    return pl.pallas_call(
        body, name=name,
        out_shape=[jax.ShapeDtypeStruct((N_MEM, D), BF16), jax.ShapeDtypeStruct((N_MEM, 2 * MW), BF16)],
        compiler_params=_params(),
    )(mem, gm, wkv)


def _memkv_bwd(mem, gm, hm, wkv, dkv, name):
    def body(m_ref, g_ref, hm_ref, w_ref, dkv_ref, dw_ref, dg_ref):
        dkv_b = dkv_ref[...].astype(BF16)
        dw_ref[...] = _dot_tn(hm_ref[...], dkv_b)
        dhm = _dot_nt(dkv_b, w_ref[...])
        _, dgr = _rms_bwd_val(m_ref[...], g_ref[...], dhm)
        dg_ref[...] = jnp.sum(dgr, axis=0, keepdims=True)

    return pl.pallas_call(
        body, name=name,
        out_shape=[jax.ShapeDtypeStruct((D, 2 * MW), F32), jax.ShapeDtypeStruct((1, D), F32)],
        compiler_params=_params(),
    )(mem, gm, hm, wkv, dkv)


MEM_T = 512


def _mem_q_spec():
    return pl.BlockSpec((MEM_T, MW), lambda i: (i, Z_MEM // MW))


def _memattn_fwd(z, kv, name):
    S = z.shape[0]
    T = MEM_T

    def body(q_ref, kv_ref, o_ref):
        for h in range(MH):
            kh = kv_ref[:, h * MHD:(h + 1) * MHD]
            vh = kv_ref[:, MW + h * MHD:MW + (h + 1) * MHD]
            s = _dot_nt(q_ref[:, h * MHD:(h + 1) * MHD], kh) * MEM_SCALE
            e = jnp.exp(s - jnp.max(s, axis=-1, keepdims=True))
            p = e / jnp.sum(e, axis=-1, keepdims=True)
            o_ref[:, h * MHD:(h + 1) * MHD] = _dot(p.astype(BF16), vh).astype(BF16)

    return pl.pallas_call(
        body, name=name, grid=(S // T,),
        in_specs=[_mem_q_spec(), pl.BlockSpec((N_MEM, 2 * MW), _fixed)],
        out_specs=pl.BlockSpec((T, MW), _row),
        out_shape=jax.ShapeDtypeStruct((S, MW), BF16),
        compiler_params=_params(("parallel",)),
    )(z, kv)


def _memattn_bwd(z, kv, dom, dz, name):
    S = z.shape[0]
    T = MEM_T

    def body(q_ref, kv_ref, do_ref, dz_in, dq_ref, dkv_ref):
        @pl.when(pl.program_id(0) == 0)
        def _():
            dkv_ref[...] = jnp.zeros_like(dkv_ref)

        for h in range(MH):
            kh = kv_ref[:, h * MHD:(h + 1) * MHD]
            vh = kv_ref[:, MW + h * MHD:MW + (h + 1) * MHD]
            qh = q_ref[:, h * MHD:(h + 1) * MHD]
            doh = do_ref[:, h * MHD:(h + 1) * MHD]
            s = _dot_nt(qh, kh) * MEM_SCALE
            e = jnp.exp(s - jnp.max(s, axis=-1, keepdims=True))
            p = e / jnp.sum(e, axis=-1, keepdims=True)
            dkv_ref[:, MW + h * MHD:MW + (h + 1) * MHD] += _dot_tn(p.astype(BF16), doh)
            dp = _dot_nt(doh, vh)
            ds = (p * (dp - jnp.sum(dp * p, axis=-1, keepdims=True))).astype(BF16)
            dq_ref[:, h * MHD:(h + 1) * MHD] = (_dot(ds, kh) * MEM_SCALE).astype(BF16)
            dkv_ref[:, h * MHD:(h + 1) * MHD] += _dot_tn(ds, qh) * MEM_SCALE

    return pl.pallas_call(
        body, name=name, grid=(S // T,),
        in_specs=[_mem_q_spec(), pl.BlockSpec((N_MEM, 2 * MW), _fixed), pl.BlockSpec((T, MW), _row),
                  pl.BlockSpec(memory_space=pl.ANY)],
        out_specs=[_mem_q_spec(), pl.BlockSpec((N_MEM, 2 * MW), _fixed)],
        out_shape=[jax.ShapeDtypeStruct(dz.shape, BF16), jax.ShapeDtypeStruct((N_MEM, 2 * MW), F32)],
        input_output_aliases={3: 0},
        compiler_params=_params(("arbitrary",)),
    )(z, kv, dom, dz)


MERGE_T = 512


def _gate_spec(T):
    return pl.BlockSpec((T, 3 * D), lambda i: (i, Z_GATE // (3 * D)))


def _branches(ca_ref, oa_ref, om_ref, wco_ref, wao_ref, wmo_ref, zg_ref, bg_ref):
    ys = [_dot(ca_ref[...], wco_ref[...]), _dot(oa_ref[...], wao_ref[...]), _dot(om_ref[...], wmo_ref[...])]
    gs = [_sigmoid(zg_ref[:, b * D:(b + 1) * D].astype(F32) + bg_ref[:, b * D:(b + 1) * D]) for b in range(3)]
    return ys, gs


def _merge_fwd(x, cact, oatt, om, z, wco, wao, wmo, wout, bgate, gpost, gnext, name):
    S = x.shape[0]
    T = MERGE_T

    def body(x_ref, ca_ref, oa_ref, om_ref, zg_ref, wco_ref, wao_ref, wmo_ref, wout_ref, bg_ref, gp_ref, gn_ref,
             x1_ref, mg_ref, t_ref, h_ref):
        ys, gs = _branches(ca_ref, oa_ref, om_ref, wco_ref, wao_ref, wmo_ref, zg_ref, bg_ref)
        mb = (gs[0] * ys[0] + gs[1] * ys[1] + gs[2] * ys[2]).astype(BF16)
        t = _dot(mb, wout_ref[...])
        mg_ref[...] = mb
        t_ref[...] = t
        x1 = x_ref[...] + _rms_fwd_val(t, gp_ref[...])
        x1_ref[...] = x1
        h_ref[...] = _rms_fwd_val(x1, gn_ref[...]).astype(BF16)

    full = lambda a: pl.BlockSpec(a.shape, _fixed)
    return pl.pallas_call(
        body, name=name, grid=(S // T,),
        in_specs=[pl.BlockSpec((T, D), _row), pl.BlockSpec((T, CW), _row), pl.BlockSpec((T, GW), _row),
                  pl.BlockSpec((T, MW), _row), _gate_spec(T)]
        + [full(wco), full(wao), full(wmo), full(wout), full(bgate), full(gpost), full(gnext)],
        out_specs=[pl.BlockSpec((T, D), _row)] * 4,
        out_shape=[jax.ShapeDtypeStruct((S, D), F32), jax.ShapeDtypeStruct((S, D), BF16),
                   jax.ShapeDtypeStruct((S, D), F32), jax.ShapeDtypeStruct((S, D), BF16)],
        compiler_params=_params(("parallel",)),
    )(x, cact, oatt, om, z, wco, wao, wmo, wout, bgate, gpost, gnext)


def _merge_bwd(dx1, t, mg, cact, oatt, om, z, wco, wao, wmo, wout, bgate, gpost, name):
    S = dx1.shape[0]
    T = MERGE_T

    def body(dx_ref, t_ref, mg_ref, ca_ref, oa_ref, om_ref, zg_ref, wco_ref, wao_ref, wmo_ref, wout_ref,
             bg_ref, gp_ref, dzg_ref, dca_ref, doa_ref, dom_ref, dwco_ref, dwao_ref, dwmo_ref, dwout_ref,
             dbg_ref, dgp_ref):
        accs = (dwco_ref, dwao_ref, dwmo_ref, dwout_ref, dbg_ref, dgp_ref)

        @pl.when(pl.program_id(0) == 0)
        def _():
            for a in accs:
                a[...] = jnp.zeros_like(a)

        dt, dgr = _rms_bwd_val(t_ref[...], gp_ref[...], dx_ref[...])
        dgp_ref[...] += jnp.sum(dgr, axis=0, keepdims=True)
        dtb = dt.astype(BF16)
        dwout_ref[...] += _dot_tn(mg_ref[...], dtb)
        dm = _dot_nt(dtb, wout_ref[...])
        ys, gs = _branches(ca_ref, oa_ref, om_ref, wco_ref, wao_ref, wmo_ref, zg_ref, bg_ref)
        for b, (act_ref, w_ref, dw_ref, da_ref) in enumerate(
                ((ca_ref, wco_ref, dwco_ref, dca_ref), (oa_ref, wao_ref, dwao_ref, doa_ref),
                 (om_ref, wmo_ref, dwmo_ref, dom_ref))):
            dzg = dm * ys[b] * gs[b] * (1.0 - gs[b])
            dzg_ref[:, b * D:(b + 1) * D] = dzg.astype(BF16)
            dbg_ref[:, b * D:(b + 1) * D] += jnp.sum(dzg, axis=0, keepdims=True)
            dy = (dm * gs[b]).astype(BF16)
            dw_ref[...] += _dot_tn(act_ref[...], dy)
            da_ref[...] = _dot_nt(dy, w_ref[...]).astype(BF16)

    full = lambda a: pl.BlockSpec(a.shape, _fixed)
    fullf = lambda a: jax.ShapeDtypeStruct(a.shape, F32)
    return pl.pallas_call(
        body, name=name, grid=(S // T,),
        in_specs=[pl.BlockSpec((T, D), _row), pl.BlockSpec((T, D), _row), pl.BlockSpec((T, D), _row),
                  pl.BlockSpec((T, CW), _row), pl.BlockSpec((T, GW), _row), pl.BlockSpec((T, MW), _row)]
        + [_gate_spec(T), full(wco), full(wao), full(wmo), full(wout), full(bgate), full(gpost)],
        out_specs=[_gate_spec(T), pl.BlockSpec((T, CW), _row), pl.BlockSpec((T, GW), _row),
                   pl.BlockSpec((T, MW), _row), full(wco), full(wao), full(wmo), full(wout), full(bgate), full(gpost)],
        out_shape=[jax.ShapeDtypeStruct((S, NIN), BF16), jax.ShapeDtypeStruct((S, CW), BF16),
                   jax.ShapeDtypeStruct((S, GW), BF16), jax.ShapeDtypeStruct((S, MW), BF16),
                   fullf(wco), fullf(wao), fullf(wmo), fullf(wout), fullf(bgate), fullf(gpost)],
        compiler_params=_params(("arbitrary",)),
    )(dx1, t, mg, cact, oatt, om, z, wco, wao, wmo, wout, bgate, gpost)


FFN_T = 256


def _ffn_fwd(x1, gu, wfo, gpost, gnext, name):
    S = x1.shape[0]
    T = FFN_T

    nxt = gnext is not None

    def body(x_ref, gu_ref, w_ref, gp_ref, *rest):
        x2_ref, f_ref = rest[nxt:nxt + 2]
        gv = gu_ref[:, :FH].astype(F32)
        uv = gu_ref[:, FH:].astype(F32)
        act = (gv * _sigmoid(gv) * uv).astype(BF16)
        f = _dot(act, w_ref[...])
        f_ref[...] = f
        x2 = x_ref[...] + _rms_fwd_val(f, gp_ref[...])
        x2_ref[...] = x2
        if nxt:
            rest[3][...] = _rms_fwd_val(x2, rest[0][...]).astype(BF16)

    return pl.pallas_call(
        body, name=name, grid=(S // T,),
        in_specs=[pl.BlockSpec((T, D), _row), pl.BlockSpec((T, 2 * FH), _row), pl.BlockSpec((FH, D), _fixed),
                  pl.BlockSpec((1, D), _fixed)] + [pl.BlockSpec((1, D), _fixed)] * nxt,
        out_specs=[pl.BlockSpec((T, D), _row)] * (2 + nxt),
        out_shape=[jax.ShapeDtypeStruct((S, D), F32)] * 2 + [jax.ShapeDtypeStruct((S, D), BF16)] * nxt,
        compiler_params=_params(("parallel",)),
    )(x1, gu, wfo, gpost, *([gnext] if nxt else []))


def _ffn_bwd(dx2, f, gu, wfo, gpost, name):
    S = dx2.shape[0]
    T = FFN_T

    def body(dx_ref, f_ref, gu_ref, w_ref, gp_ref, dgu_ref, df_ref, act_ref, dgp_ref):
        @pl.when(pl.program_id(0) == 0)
        def _():
            dgp_ref[...] = jnp.zeros_like(dgp_ref)

        df, dgr = _rms_bwd_val(f_ref[...], gp_ref[...], dx_ref[...])
        dgp_ref[...] += jnp.sum(dgr, axis=0, keepdims=True)
        dfb = df.astype(BF16)
        df_ref[...] = dfb
        dact = _dot_nt(dfb, w_ref[...])
        gv = gu_ref[:, :FH].astype(F32)
        uv = gu_ref[:, FH:].astype(F32)
        sg = _sigmoid(gv)
        silu = gv * sg
        act_ref[...] = (silu * uv).astype(BF16)
        dgu_ref[:, :FH] = (dact * uv * (sg * (1.0 + gv * (1.0 - sg)))).astype(BF16)
        dgu_ref[:, FH:] = (dact * silu).astype(BF16)

    return pl.pallas_call(
        body, name=name, grid=(S // T,),
        in_specs=[pl.BlockSpec((T, D), _row), pl.BlockSpec((T, D), _row), pl.BlockSpec((T, 2 * FH), _row),
                  pl.BlockSpec((FH, D), _fixed), pl.BlockSpec((1, D), _fixed)],
        out_specs=[pl.BlockSpec((T, 2 * FH), _row), pl.BlockSpec((T, D), _row), pl.BlockSpec((T, FH), _row),
                   pl.BlockSpec((1, D), _fixed)],
        out_shape=[jax.ShapeDtypeStruct((S, 2 * FH), BF16), jax.ShapeDtypeStruct((S, D), BF16),
                   jax.ShapeDtypeStruct((S, FH), BF16), jax.ShapeDtypeStruct((1, D), F32)],
        compiler_params=_params(("arbitrary",)),
    )(dx2, f, gu, wfo, gpost)


def _loss_head(y, target, name):
    S = y.shape[0]
    T = 512

    def body(y_ref, t_ref, dy_ref, l_ref):
        @pl.when(pl.program_id(0) == 0)
        def _():
            l_ref[...] = jnp.zeros_like(l_ref)

        e = y_ref[...] - t_ref[...]
        dy_ref[...] = e * (1.0 / D)
        l_ref[...] += (0.5 / D) * jnp.sum(jnp.sum(e * e, axis=1, keepdims=True), axis=0, keepdims=True)

    return pl.pallas_call(
        body, name=name, grid=(S // T,),
        in_specs=[pl.BlockSpec((T, D), _row)] * 2,
        out_specs=[pl.BlockSpec((T, D), _row), pl.BlockSpec((8, 128), _fixed)],
        out_shape=[jax.ShapeDtypeStruct((S, D), F32), jax.ShapeDtypeStruct((8, 128), F32)],
        compiler_params=_params(("arbitrary",)),
    )(y, target)


BIG = ("w_in", "w_conv_out", "w_att_out", "w_mem_kv", "w_mem_out", "w_out", "w_ffn_in", "w_ffn_out")
SMALL = ("rel_bias", "norm_mix_pre", "b_gate", "conv_dw_bias", "conv_ln_g", "conv_ln_b", "norm_mem",
         "norm_mix_post", "norm_ffn_pre", "norm_ffn_post")


def _layer_fwd(l, x, h, mem, w, rel_bias, gnext):
    tag = f"_l{l}"
    z = _mm_nn(h, w["w_in"], 512, BF16, "mm_in" + tag)
    yc, cact = _conv_fwd(z, w["conv_dw"], w["conv_dw_bias"], w["conv_ln_g"], w["conv_ln_b"], "conv_fwd" + tag)
    og, lg = zip(*[_att_fwd(z, rel_bias, g, f"att_fwd_g{g}" + tag) for g in range(3)])
    oatt, lse = _att_combine(og, lg, "att_combine" + tag)
    hm, kv = _memkv_fwd(mem, w["norm_mem"], w["w_mem_kv"], "memkv_fwd" + tag)
    om = _memattn_fwd(z, kv, "memattn_fwd" + tag)
    x1, mg, t, h2 = _merge_fwd(x, cact, oatt, om, z, w["w_conv_out"], w["w_att_out"], w["w_mem_out"], w["w_out"],
                               w["b_gate"], w["norm_mix_post"], w["norm_ffn_pre"], "merge_fwd" + tag)
    gu = _mm_nn(h2, w["w_ffn_in"], 512, BF16, "mm_ffn_in" + tag)
    x2, f, *hn = _ffn_fwd(x1, gu, w["w_ffn_out"], w["norm_ffn_post"], gnext, "ffn_fwd" + tag)
    saved = dict(x=x, h=h, z=z, yc=yc, cact=cact, oatt=oatt, lse=lse, hm=hm, kv=kv, om=om, x1=x1, mg=mg, t=t,
                 h2=h2, gu=gu, f=f)
    return x2, (hn[0] if hn else None), saved


def _layer_bwd(l, dx2, mem, w, rel_bias, s, on_ffn_grads=None):
    tag = f"_l{l}"
    gr = {}
    dgu, df, act, gr["norm_ffn_post"] = _ffn_bwd(dx2, s["f"], s["gu"], w["w_ffn_out"], w["norm_ffn_post"], "ffn_bwd" + tag)
    gr["w_ffn_out"] = _mm_tn(act, df, 1024, 512, "dw_ffn_out" + tag)
    gr["w_ffn_in"] = _mm_tn(s["h2"], dgu, 2048, 1408, "dw_ffn_in" + tag)
    gpre = w["norm_ffn_pre"]
    if on_ffn_grads is not None:
        gpre = gpre + on_ffn_grads(gr)[0, 0]
    dx1, gr["norm_ffn_pre"] = _mm_nt_rms_bwd(dgu, w["w_ffn_in"], s["x1"], gpre, dx2, 512, "dh_ffn" + tag)
    (dz, dcact, doatt, dom, gr["w_conv_out"], gr["w_att_out"], gr["w_mem_out"], gr["w_out"], gr["b_gate"],
     gr["norm_mix_post"]) = _merge_bwd(dx1, s["t"], s["mg"], s["cact"], s["oatt"], s["om"], s["z"], w["w_conv_out"],
                                       w["w_att_out"], w["w_mem_out"], w["w_out"], w["b_gate"], w["norm_mix_post"],
                                       "merge_bwd" + tag)
    dyc, gr["conv_ln_g"], gr["conv_ln_b"], gr["conv_dw_bias"] = _conv_bwd_ln(
        s["yc"], dcact, w["conv_ln_g"], w["conv_ln_b"], "conv_bwd_ln" + tag)
    dz, dwdw = _conv_bwd_dw(s["z"], dyc, w["conv_dw"], dz, "conv_bwd_dw" + tag)
    gr["conv_dw"] = dwdw[:KSIZE]
    ld = _att_prep(doatt, s["oatt"], s["lse"], "att_prep" + tag)
    drb = []
    for g in range(3):
        dz, db = _att_bwd(s["z"], rel_bias, doatt, ld, dz, g, f"att_bwd_g{g}" + tag)
        drb.append(db)
    dz, dkv = _memattn_bwd(s["z"], s["kv"], dom, dz, "memattn_bwd" + tag)
    gr["w_mem_kv"], gr["norm_mem"] = _memkv_bwd(mem, w["norm_mem"], s["hm"], w["w_mem_kv"], dkv, "memkv_bwd" + tag)
    gr["w_in"] = _mm_tn(s["h"], dz, 2048, 1152, "dw_in" + tag)
    dx, gr["norm_mix_pre"] = _mm_nt_rms_bwd(dz, w["w_in"], s["x"], w["norm_mix_pre"], dx1, 512, "dh_in" + tag)
    return dx, gr, drb


def _rel_bias_total(parts, name):
    def body(*refs):
        out_ref = refs[-1]
        acc = jnp.zeros((NUM_BUCKETS, 128), F32)
        for l in range(DEPTH):
            for g in range(3):
                v = refs[l * 3 + g][...]
                acc = acc + (v if g == 0 else pltpu.roll(v, HPG * g, axis=1))
        out_ref[...] = acc

    return pl.pallas_call(body, name=name, out_shape=jax.ShapeDtypeStruct((NUM_BUCKETS, 128), F32),
                          compiler_params=_params())(*[p for layer in parts for p in layer])


def _local_step(x, mem, target, rel_bias, layer_fns, gmix, on_grads=None, on_last_ffn_grads=None):
    saved, layers = [], []
    h = _rms_h(x, gmix[0], "rms_mix_l0")
    for l in range(DEPTH):
        layers.append(layer_fns[l](x))
        x, h, s = _layer_fwd(l, x, h, mem, layers[l], rel_bias, gmix[l + 1] if l + 1 < DEPTH else None)
        saved.append(s)
    dy, lpart = _loss_head(x, target, "loss_head")
    grads = [None] * DEPTH
    drb = [None] * DEPTH
    for l in reversed(range(DEPTH)):
        dy, grads[l], drb[l] = _layer_bwd(l, dy, mem, layers[l], rel_bias, saved[l],
                                          on_last_ffn_grads if l == 0 else None)
        if on_grads is not None and l > 0:
            below = dict(layers[l - 1])
            below["norm_ffn_post"] = below["norm_ffn_post"] + on_grads(l, grads[l])[0, 0]
            layers[l - 1] = below
    return lpart[0, 0], dy, grads, _rel_bias_total(drb, "rel_bias_total")


def _z_cols_from_ref(w):
    att = [w[..., R_ATT + (3 * j + g) * GW:R_ATT + (3 * j + g + 1) * GW] for g in range(3) for j in range(3)]
    return jnp.concatenate([w[..., R_GATE:], w[..., :C1], w[..., R_MEM:R_GATE]] + att, axis=-1)


def _ref_cols_from_z(w):
    att = [w[..., Z_ATT + (3 * g + j) * GW:Z_ATT + (3 * g + j + 1) * GW] for j in range(3) for g in range(3)]
    return jnp.concatenate([w[..., Z_CONV:Z_MEM]] + att + [w[..., Z_MEM:Z_ATT], w[..., Z_GATE:Z_CONV]], axis=-1)


N_CHIPS = 4
SHARD = {"w_in": ((D, NIN // 4), 1), "w_conv_out": ((CW, D // 4), 1), "w_att_out": ((GW, D // 4), 1),
         "w_mem_kv": ((D // 4, 2 * MW), 0), "w_mem_out": ((MW, D // 4), 1), "w_out": ((D // 4, D), 0),
         "w_ffn_in": ((D, 2 * FH // 4), 1), "w_ffn_out": ((FH // 4, D), 0)}
CDW_ROWS = 64
VEC_ROWS = (("norm_mix_pre", 1), ("b_gate", 3), ("conv_dw_bias", 1), ("conv_ln_g", 1), ("conv_ln_b", 1),
            ("norm_mem", 1), ("norm_mix_post", 1), ("norm_ffn_pre", 1), ("norm_ffn_post", 1))
VEC_LROWS = sum(r for _, r in VEC_ROWS)
REL_ROW = DEPTH * VEC_LROWS
CDW_ROW = REL_ROW + 1
CDW_GROWS = DEPTH * KSIZE * CW // D
LOSS_ROW = CDW_ROW + CDW_GROWS
SMALL_ROWS = -(-(LOSS_ROW + 1) // 8) * 8


def _mesh_pos():
    return lax.axis_index("x"), lax.axis_index("y"), lax.axis_index("c")


def _other_chips(x, y):
    chips = [(1 - x, y), (x, 1 - y), (1 - x, 1 - y)]
    return chips, [2 * cx + cy for cx, cy in chips]


NBIG = len(BIG)
ANY_SPEC = pl.BlockSpec(memory_space=pl.ANY)


def _remote(src, dst, send_sems, recv_sems, k, to):
    return pltpu.make_async_remote_copy(src_ref=src, dst_ref=dst, send_sem=send_sems.at[k], recv_sem=recv_sems.at[k],
                                        device_id=to, device_id_type=MESH)


def _half(ref, c):
    h = ref.shape[0] // 2
    return ref.at[pl.ds(c * h if isinstance(c, int) else pl.multiple_of(c * h, 16), h)]


def _all_gather(ws, cdw):
    def body(*refs):
        w_refs, cdw_ref = refs[:NBIG], refs[NBIG]
        g_refs, gc_ref = refs[NBIG + 1:2 * NBIG + 1], refs[2 * NBIG + 1]
        send_sems, recv_sems = refs[2 * NBIG + 2:]
        x, y, c = _mesh_pos()
        j = 2 * x + y
        sibling = (x, y, 1 - c)
        chips, blocks = _other_chips(x, y)
        copy = functools.partial(_remote, send_sems=send_sems, recv_sems=recv_sems)
        pairs = list(zip(w_refs, g_refs))
        first = [copy(_half(w, c), _half(g.at[j], c), k=k * NBIG + n, to=(*chip, c))
                 for k, chip in enumerate(chips) for n, (w, g) in enumerate(pairs)]
        first += [copy(cdw_ref, gc_ref.at[j], k=6 * NBIG + k, to=(*chip, c)) for k, chip in enumerate(chips)]
        for cp in first:
            cp.start()
        passed = []
        for k, b in enumerate(blocks):
            for n, (w, g) in enumerate(pairs):
                copy(_half(w, c), _half(g.at[b], c), k=k * NBIG + n, to=sibling).wait_recv()
            onward = [copy(_half(g.at[b], c), _half(g.at[b], c), k=(3 + k) * NBIG + n, to=sibling)
                      for n, (w, g) in enumerate(pairs)]
            for cp in onward:
                cp.start()
            passed += onward
        for k, b in enumerate(blocks):
            for n, (w, g) in enumerate(pairs):
                copy(_half(w, c), _half(g.at[b], 1 - c), k=(3 + k) * NBIG + n, to=sibling).wait_recv()
            copy(cdw_ref, gc_ref.at[b], k=6 * NBIG + k, to=sibling).wait_recv()
        for cp in first + passed:
            cp.wait_send()

    nsem = 6 * NBIG + 3
    return pl.pallas_call(
        body, name="all_gather_weights",
        out_shape=[jax.ShapeDtypeStruct((N_CHIPS,) + w.shape, BF16) for w in ws]
        + [jax.ShapeDtypeStruct((N_CHIPS, CDW_ROWS, 128), F32)],
        in_specs=[ANY_SPEC] * (NBIG + 1), out_specs=[ANY_SPEC] * (NBIG + 1),
        scratch_shapes=[pltpu.SemaphoreType.DMA((nsem,)), pltpu.SemaphoreType.DMA((nsem,))],
    )(*ws, cdw)


SEM_SPEC = pl.BlockSpec(memory_space=pltpu.SEMAPHORE)
DATAFLOW = pltpu.SideEffectType.DATAFLOW_SIDE_EFFECTING


def _gather_copies(w_refs, g_refs, send_sem, recv_sem):
    x, y, c = _mesh_pos()
    j = 2 * x + y
    chips, _ = _other_chips(x, y)
    return [pltpu.make_async_remote_copy(src_ref=_half(w, c), dst_ref=_half(g.at[j], c), send_sem=send_sem,
                                         recv_sem=recv_sem, device_id=(*chip, cc), device_id_type=MESH)
            for chip in chips for cc in (0, 1) for w, g in zip(w_refs, g_refs)]


def _all_gather_start(ws, after):
    def body(*refs):
        w_refs, g_refs = refs[:NBIG], refs[NBIG:2 * NBIG]
        send_sem, recv_sem = refs[2 * NBIG + 1:2 * NBIG + 3]
        token = refs[-1]
        for cp in _gather_copies(w_refs, g_refs, send_sem, recv_sem):
            cp.start()
        token[...] = jnp.zeros_like(token)

    lands = [pltpu.with_memory_space_constraint(lax.empty((N_CHIPS,) + w.shape, BF16), pltpu.HBM) for w in ws]
    ws = [pltpu.with_memory_space_constraint(w, pltpu.HBM) for w in ws]
    hbm = pl.BlockSpec(memory_space=pltpu.HBM)
    out = pl.pallas_call(
        body, name="all_gather_start",
        out_shape=[pltpu.SemaphoreType.DMA(()), pltpu.SemaphoreType.DMA(())]
        + [pltpu.HBM(w.shape, BF16) for w in ws] + [pltpu.HBM(g.shape, BF16) for g in lands]
        + [jax.ShapeDtypeStruct((8, 128), F32)],
        in_specs=[hbm] * (2 * NBIG) + [ANY_SPEC],
        out_specs=[SEM_SPEC, SEM_SPEC] + [hbm] * (2 * NBIG) + [pl.BlockSpec(memory_space=pltpu.VMEM)],
        input_output_aliases={n: 2 + n for n in range(2 * NBIG)},
        compiler_params=pltpu.CompilerParams(has_side_effects=DATAFLOW),
    )(*ws, *lands, after)
    return out[0], out[1], out[2:2 + NBIG], out[2 + NBIG:2 + 2 * NBIG], out[-1]


def _all_gather_wait(send_sem, recv_sem, ws, lands, after):
    def body(*refs):
        w_refs, g_refs = refs[:NBIG], refs[NBIG:2 * NBIG]
        send_sem, recv_sem = refs[2 * NBIG:2 * NBIG + 2]
        x, y, c = _mesh_pos()
        _, blocks = _other_chips(x, y)
        for cp in _gather_copies(w_refs, g_refs, send_sem, recv_sem):
            cp.wait_send()
        for b in blocks:
            for cc in (0, 1):
                for w, g in zip(w_refs, g_refs):
                    pltpu.make_async_remote_copy(src_ref=_half(w, cc), dst_ref=_half(g.at[b], cc), send_sem=send_sem,
                                                 recv_sem=recv_sem, device_id=(x, y, c),
                                                 device_id_type=MESH).wait_recv()

    hbm = pl.BlockSpec(memory_space=pltpu.HBM)
    out = pl.pallas_call(
        body, name="all_gather_wait",
        out_shape=[pltpu.HBM(w.shape, BF16) for w in ws] + [pltpu.HBM(g.shape, BF16) for g in lands],
        in_specs=[hbm] * (2 * NBIG) + [SEM_SPEC, SEM_SPEC, ANY_SPEC],
        out_specs=[hbm] * (2 * NBIG),
        input_output_aliases={n: n for n in range(2 * NBIG)},
        compiler_params=pltpu.CompilerParams(has_side_effects=DATAFLOW),
    )(*ws, *lands, send_sem, recv_sem, after)
    return out[:NBIG], out[NBIG:]


def _half_rows(ref, c):
    h = ref.shape[1] // 2
    return ref.at[:, pl.ds(pl.multiple_of(c * h, 16), h)]


def _sibling_exchange(ps):
    nw = len(ps)

    def body(*refs):
        p_refs, r_refs, (send_sems, recv_sems) = refs[:nw], refs[nw:2 * nw], refs[2 * nw:]
        x, y, c = _mesh_pos()
        cps = [_remote(_half_rows(p, 1 - c), r, send_sems, recv_sems, n, (x, y, 1 - c))
               for n, (p, r) in enumerate(zip(p_refs, r_refs))]
        for cp in cps:
            cp.start()
        for cp in cps:
            cp.wait()

    return pl.pallas_call(
        body, name="grad_sibling_exchange",
        out_shape=[jax.ShapeDtypeStruct((N_CHIPS, p.shape[1] // 2, p.shape[2]), p.dtype) for p in ps],
        in_specs=[ANY_SPEC] * nw, out_specs=[ANY_SPEC] * nw,
        scratch_shapes=[pltpu.SemaphoreType.DMA((nw,)), pltpu.SemaphoreType.DMA((nw,))],
    )(*ps)


SUM_BLOCK_BYTES = 2 * 1024 * 1024


def _sum_rows(s0, s1):
    return s0 if s0 * s1 * 2 <= SUM_BLOCK_BYTES else s0 // 2


def _add_own_half(where, p, r, name):
    _, h, s1 = r.shape
    T = _sum_rows(h, s1)
    nt = h // T

    def body(where_ref, p_ref, r_ref, o_ref):
        o_ref[...] = (p_ref[...].astype(F32) + r_ref[...].astype(F32)).astype(BF16)

    return pl.pallas_call(
        body, name=name,
        grid_spec=pltpu.PrefetchScalarGridSpec(
            num_scalar_prefetch=1, grid=(N_CHIPS, nt),
            in_specs=[pl.BlockSpec((1, T, s1), lambda j, i, wh: (j, wh[0] * nt + i, 0)),
                      pl.BlockSpec((1, T, s1), lambda j, i, wh: (j, i, 0))],
            out_specs=pl.BlockSpec((1, T, s1), lambda j, i, wh: (j, i, 0))),
        out_shape=jax.ShapeDtypeStruct(r.shape, BF16), compiler_params=_params(("parallel", "parallel")),
    )(where, p, r)


def _chip_exchange(as_):
    nw = len(as_)

    def body(*refs):
        a_refs, r_refs, (send_sems, recv_sems) = refs[:nw], refs[nw:2 * nw], refs[2 * nw:]
        x, y, c = _mesh_pos()
        chips, blocks = _other_chips(x, y)
        cps = [_remote(a.at[b], r.at[k], send_sems, recv_sems, k * nw + n, (*chip, c))
               for k, (chip, b) in enumerate(zip(chips, blocks)) for n, (a, r) in enumerate(zip(a_refs, r_refs))]
        for cp in cps:
            cp.start()
        for cp in cps:
            cp.wait_recv()
        for cp in cps:
            cp.wait_send()

    return pl.pallas_call(
        body, name="grad_chip_exchange", out_shape=[jax.ShapeDtypeStruct((3,) + a.shape[1:], a.dtype) for a in as_],
        in_specs=[ANY_SPEC] * nw, out_specs=[ANY_SPEC] * nw,
        scratch_shapes=[pltpu.SemaphoreType.DMA((3 * nw,)), pltpu.SemaphoreType.DMA((3 * nw,))],
    )(*as_)


def _sum_chips(where, a, r, o, name):
    _, h, s1 = a.shape
    T = _sum_rows(h, s1)
    nt = h // T

    def body(where_ref, a_ref, r_ref, o_in, o_ref):
        acc = a_ref[0].astype(F32)
        for k in range(3):
            acc = acc + r_ref[k].astype(F32)
        o_ref[0] = acc

    return pl.pallas_call(
        body, name=name,
        grid_spec=pltpu.PrefetchScalarGridSpec(
            num_scalar_prefetch=1, grid=(nt,),
            in_specs=[pl.BlockSpec((1, T, s1), lambda i, wh: (wh[1], i, 0)),
                      pl.BlockSpec((3, T, s1), lambda i, wh: (0, i, 0)), ANY_SPEC],
            out_specs=pl.BlockSpec((1, T, s1), lambda i, wh: (0, wh[0] * nt + i, 0))),
        out_shape=jax.ShapeDtypeStruct(o.shape, F32), input_output_aliases={3: 0},
        compiler_params=_params(("parallel",)),
    )(where, a, r, o)


def _sibling_share(os_):
    nw = len(os_)

    def body(*refs):
        o_refs, (send_sems, recv_sems) = refs[nw:2 * nw], refs[2 * nw:]
        x, y, c = _mesh_pos()
        mine = lambda o, cc: _half(o.at[0], cc)
        cps = [_remote(mine(o, c), mine(o, c), send_sems, recv_sems, n, (x, y, 1 - c)) for n, o in enumerate(o_refs)]
        for cp in cps:
            cp.start()
        for n, o in enumerate(o_refs):
            _remote(mine(o, c), mine(o, 1 - c), send_sems, recv_sems, n, (x, y, 1 - c)).wait_recv()
        for cp in cps:
            cp.wait_send()

    return pl.pallas_call(
        body, name="grad_sibling_share", out_shape=[jax.ShapeDtypeStruct(o.shape, o.dtype) for o in os_],
        in_specs=[ANY_SPEC] * nw, out_specs=[ANY_SPEC] * nw,
        input_output_aliases={n: n for n in range(nw)},
        scratch_shapes=[pltpu.SemaphoreType.DMA((nw,)), pltpu.SemaphoreType.DMA((nw,))],
    )(*os_)


N_DEV = 8


def _scatter_copies(p_refs, r_refs, send_sem, recv_sem):
    x, y, c = _mesh_pos()
    cps = []
    for m in range(1, N_DEV):
        px, py, pc = x ^ (m >> 2 & 1), y ^ (m >> 1 & 1), c ^ (m & 1)
        for p, r in zip(p_refs, r_refs):
            cps.append(pltpu.make_async_remote_copy(src_ref=p.at[2 * px + py], dst_ref=r.at[m - 1], send_sem=send_sem,
                                                    recv_sem=recv_sem, device_id=(px, py, pc), device_id_type=MESH))
    return cps


def _reduce_start(ps, after, name):
    nw = len(ps)

    def body(*refs):
        p_refs, r_refs = refs[:nw], refs[nw:2 * nw]
        send_sem, recv_sem = refs[2 * nw + 1:2 * nw + 3]
        for cp in _scatter_copies(p_refs, r_refs, send_sem, recv_sem):
            cp.start()
        refs[-1][...] = jnp.zeros_like(refs[-1])

    lands = [pltpu.with_memory_space_constraint(lax.empty((N_DEV - 1,) + p.shape[1:], BF16), pltpu.HBM) for p in ps]
    ps = [pltpu.with_memory_space_constraint(p, pltpu.HBM) for p in ps]
    hbm = pl.BlockSpec(memory_space=pltpu.HBM)
    out = pl.pallas_call(
        body, name=name,
        out_shape=[pltpu.SemaphoreType.DMA(()), pltpu.SemaphoreType.DMA(())]
        + [pltpu.HBM(p.shape, BF16) for p in ps] + [pltpu.HBM(r.shape, BF16) for r in lands]
        + [jax.ShapeDtypeStruct((8, 128), F32)],
        in_specs=[hbm] * (2 * nw) + [ANY_SPEC],
        out_specs=[SEM_SPEC, SEM_SPEC] + [hbm] * (2 * nw) + [pl.BlockSpec(memory_space=pltpu.VMEM)],
        input_output_aliases={n: 2 + n for n in range(2 * nw)},
        compiler_params=pltpu.CompilerParams(has_side_effects=DATAFLOW),
    )(*ps, *lands, after)
    return out[0], out[1], out[2:2 + nw], out[2 + nw:2 + 2 * nw], out[-1]


def _reduce_wait(send_sem, recv_sem, ps, lands, after, name):
    nw = len(ps)

    def body(*refs):
        p_refs, r_refs = refs[:nw], refs[nw:2 * nw]
        send_sem, recv_sem = refs[2 * nw:2 * nw + 2]
        x, y, c = _mesh_pos()
        for cp in _scatter_copies(p_refs, r_refs, send_sem, recv_sem):
            cp.wait_send()
        for m in range(1, N_DEV):
            for p, r in zip(p_refs, r_refs):
                pltpu.make_async_remote_copy(src_ref=p.at[0], dst_ref=r.at[m - 1], send_sem=send_sem,
                                             recv_sem=recv_sem, device_id=(x, y, c), device_id_type=MESH).wait_recv()

    hbm = pl.BlockSpec(memory_space=pltpu.HBM)
    out = pl.pallas_call(
        body, name=name,
        out_shape=[pltpu.HBM(p.shape, BF16) for p in ps] + [pltpu.HBM(r.shape, BF16) for r in lands],
        in_specs=[hbm] * (2 * nw) + [SEM_SPEC, SEM_SPEC, ANY_SPEC],
        out_specs=[hbm] * (2 * nw),
        input_output_aliases={n: n for n in range(2 * nw)},
        compiler_params=pltpu.CompilerParams(has_side_effects=DATAFLOW),
    )(*ps, *lands, send_sem, recv_sem, after)
    return out[:nw], out[nw:]


SUM8_BLOCK_BYTES = 6 * 1024 * 1024


def _sum_devices(where, p, r, layer, o, name):
    _, s0, s1 = p.shape
    T = s0
    while (N_DEV - 1) * T * s1 * 2 > SUM8_BLOCK_BYTES:
        T //= 2

    def body(where_ref, p_ref, r_ref, *rest):
        me = 2 * where_ref[1] + where_ref[0]
        acc = None
        for dev in range(N_DEV):
            m = dev ^ me
            val = jnp.where(m == 0, p_ref[0], r_ref[jnp.maximum(m, 1) - 1]).astype(F32)
            acc = val if acc is None else acc + val
        rest[-1][0] = acc

    given = o is not None
    return pl.pallas_call(
        body, name=name,
        grid_spec=pltpu.PrefetchScalarGridSpec(
            num_scalar_prefetch=1, grid=(s0 // T,),
            in_specs=[pl.BlockSpec((1, T, s1), lambda i, wh: (wh[1], i, 0)),
                      pl.BlockSpec((N_DEV - 1, T, s1), lambda i, wh: (0, i, 0))] + [ANY_SPEC] * given,
            out_specs=pl.BlockSpec((1, T, s1), lambda i, wh: (layer, i, 0))),
        out_shape=jax.ShapeDtypeStruct((DEPTH, s0, s1), F32), input_output_aliases={3: 0} if given else {},
        compiler_params=_params(("parallel",)),
    )(where, p, r, *([o] if given else []))


def _all_reduce_small(sp):
    def body(sp_ref, out_ref, buf, send_sems, recv_sems):
        x, y, c = _mesh_pos()
        me = 4 * x + 2 * y + c
        buf[0] = sp_ref[...]
        cps = []
        for k in range(1, 8):
            peer = (x ^ (k >> 2 & 1), y ^ (k >> 1 & 1), c ^ (k & 1))
            cps.append(pltpu.make_async_remote_copy(src_ref=sp_ref, dst_ref=buf.at[k], send_sem=send_sems.at[k - 1],
                                                    recv_sem=recv_sems.at[k - 1], device_id=peer, device_id_type=MESH))
        for cp in cps:
            cp.start()
        for cp in cps:
            cp.wait_recv()
        for cp in cps:
            cp.wait_send()
        acc = buf[me]
        for p in range(1, 8):
            acc = acc + buf[p ^ me]
        out_ref[...] = acc

    vm = pl.BlockSpec(memory_space=pltpu.VMEM)
    return pl.pallas_call(
        body, name="all_reduce_small", out_shape=jax.ShapeDtypeStruct(sp.shape, F32),
        in_specs=[vm], out_specs=vm,
        scratch_shapes=[pltpu.VMEM((8,) + sp.shape, F32), pltpu.SemaphoreType.DMA((7,)), pltpu.SemaphoreType.DMA((7,))],
        compiler_params=_params(),
    )(sp)


def _adamw(w, g, m, v, name):
    R, C = w.shape
    T = next((t for t in (256, 128) if R % t == 0), R)

    def body(w_ref, g_ref, m_ref, v_ref, d_ref, m2_ref, v2_ref):
        gv = g_ref[...]
        m2 = ADAM_B1 * m_ref[...] + (1.0 - ADAM_B1) * gv
        v2 = ADAM_B2 * v_ref[...] + (1.0 - ADAM_B2) * (gv * gv)
        m_hat = m2 / (1.0 - ADAM_B1 ** ADAM_STEP)
        v_hat = v2 / (1.0 - ADAM_B2 ** ADAM_STEP)
        d_ref[...] = -ADAM_LR * (m_hat / (jnp.sqrt(v_hat) + ADAM_EPS) + ADAM_WD * w_ref[...])
        m2_ref[...] = m2
        v2_ref[...] = v2

    blk = pl.BlockSpec((T, C), _row)
    return pl.pallas_call(
        body, name=name, grid=(R // T,), in_specs=[blk] * 4, out_specs=[blk] * 3,
        out_shape=[jax.ShapeDtypeStruct((R, C), F32)] * 3, compiler_params=_params(("parallel",)),
    )(w, g, m, v)


def _pack_vectors(get, rel, cdw, name, loss=None):
    rows = []
    for l in range(DEPTH):
        for n, r in VEC_ROWS:
            v = get(n)[l]
            rows.append(jnp.pad(v, (0, r * D - v.shape[0])).reshape(r, D))
    rows.append(jnp.pad(rel.reshape(-1), (0, D - NUM_BUCKETS * 3 * HPG)).reshape(1, D))
    rows.append(cdw.reshape(CDW_GROWS, D))
    if loss is not None:
        rows.append(jnp.full((1, D), loss, F32))

    def body(*refs):
        out_ref = refs[-1]
        out_ref[...] = jnp.zeros_like(out_ref)
        at = 0
        for ref in refs[:-1]:
            out_ref[at:at + ref.shape[0], :] = ref[...]
            at += ref.shape[0]

    return pl.pallas_call(body, name=name, out_shape=jax.ShapeDtypeStruct((SMALL_ROWS, D), F32),
                          compiler_params=_params())(*rows)


def _unpack_vectors(packed, lens):
    out = {n: [] for n, _ in VEC_ROWS}
    for l in range(DEPTH):
        at = l * VEC_LROWS
        for n, r in VEC_ROWS:
            out[n].append(packed[at:at + r].reshape(-1)[:lens[n]])
            at += r
    rel = packed[REL_ROW, :NUM_BUCKETS * 3 * HPG].reshape(NUM_BUCKETS, 3 * HPG)
    return {n: jnp.stack(v) for n, v in out.items()}, rel


INPUT_NAMES = ("x", "mem") + ("rel_bias", "norm_mix_pre", "w_in", "b_gate", "conv_dw", "conv_dw_bias", "conv_ln_g",
                              "conv_ln_b", "w_conv_out", "w_att_out", "norm_mem", "w_mem_kv", "w_mem_out", "w_out",
                              "norm_mix_post", "norm_ffn_pre", "w_ffn_in", "w_ffn_out", "norm_ffn_post")
WEIGHT_NAMES = INPUT_NAMES[2:]


def kernel(*args):
    nw = len(WEIGHT_NAMES)
    a = dict(zip(INPUT_NAMES, args[:2 + nw]))
    target = args[2 + nw]
    mom = dict(zip(WEIGHT_NAMES, args[3 + nw:3 + 2 * nw]))
    var = dict(zip(WEIGHT_NAMES, args[3 + 2 * nw:3 + 3 * nw]))
    xi, yi, ci = _mesh_pos()
    chip = 2 * xi + yi
    where = jnp.stack([ci, chip]).astype(I32)

    shards = [[a[n][l].astype(BF16) for n in BIG] for l in range(DEPTH)]
    cdw = jnp.pad(a["conv_dw"].reshape(DEPTH * KSIZE, CW // 4), ((0, CDW_ROWS - DEPTH * KSIZE), (0, 0)))
    *gathered0, gcdw = _all_gather(shards[0], cdw)
    in_flight = _all_gather_start(shards[1], gathered0[0])
    gcdw = lax.dynamic_update_slice(gcdw, cdw[None], (chip, 0, 0))
    conv_dw = gcdw[:, :DEPTH * KSIZE].reshape(N_CHIPS, DEPTH, KSIZE, CW // 4).transpose(1, 2, 0, 3)
    conv_dw = jnp.pad(conv_dw.reshape(DEPTH, KSIZE, CW), ((0, 0), (0, 1), (0, 0)))
    gmix = [a["norm_mix_pre"][l][None, :] for l in range(DEPTH)]
    gmix[0] = gmix[0] + in_flight[4][0, 0]

    def layer_weights(l, gathered, own):
        w = {"conv_dw": conv_dw[l]}
        for n, g, s in zip(BIG, gathered, own):
            (s0, s1), axis = SHARD[n]
            blk = lax.dynamic_update_slice(g, s[None], (chip, 0, 0))
            w[n] = blk.reshape(N_CHIPS * s0, s1) if axis == 0 else blk.transpose(1, 0, 2).reshape(s0, N_CHIPS * s1)
        w["w_in"] = _z_cols_from_ref(w["w_in"])
        for n, _ in VEC_ROWS:
            w[n] = a[n][l][None, :]
        return w

    def layer1(x):
        send_sem, recv_sem, thru, lands, _ = in_flight
        own, gathered1 = _all_gather_wait(send_sem, recv_sem, thru, lands, x)
        return layer_weights(1, gathered1, own)

    def by_chip(layer_grads, names):
        out = []
        for n in names:
            (s0, s1), axis = SHARD[n]
            g = _ref_cols_from_z(layer_grads[n]) if n == "w_in" else layer_grads[n]
            g = g.reshape(N_CHIPS, s0, s1) if axis == 0 else g.reshape(s0, N_CHIPS, s1).transpose(1, 0, 2)
            out.append(g.astype(BF16))
        return out

    early = ("w_ffn_in", "w_ffn_out")
    late = tuple(n for n in BIG if n not in early)
    scattering = {}

    def on_grads(l, layer_grads):
        scattering["l1"] = _reduce_start(by_chip(layer_grads, BIG), layer_grads["w_in"], "grad_reduce_start_l1")
        return scattering["l1"][4]

    def on_last_ffn_grads(layer_grads):
        scattering["ffn"] = _reduce_start(by_chip(layer_grads, early), layer_grads["w_ffn_in"], "grad_reduce_start_ffn")
        return scattering["ffn"][4]

    loss_part, gx, grads, drel = _local_step(a["x"][0], a["mem"][0], target[0], a["rel_bias"],
                                             [lambda x: layer_weights(0, gathered0, shards[0]), layer1], gmix,
                                             on_grads, on_last_ffn_grads)

    reduced = {}
    send_sem, recv_sem, thru, lands, _ = scattering["l1"]
    sent, landed = _reduce_wait(send_sem, recv_sem, thru, lands, gx, "grad_reduce_wait_l1")
    for n, p, r in zip(BIG, sent, landed):
        reduced[n] = _sum_devices(where, p, r, 1, None, "grad_sum_devices_l1_" + n)
    send_sem, recv_sem, thru, lands, _ = scattering["ffn"]
    sent, landed = _reduce_wait(send_sem, recv_sem, thru, lands, gx, "grad_reduce_wait_ffn")
    for n, p, r in zip(early, sent, landed):
        reduced[n] = _sum_devices(where, p, r, 0, reduced[n], "grad_sum_devices_l0_" + n)
    packed = by_chip(grads[0], late)
    from_sibling = _sibling_exchange(packed)
    chip_sums = [_add_own_half(where, p, r, "grad_add_sibling_" + n) for n, p, r in zip(late, packed, from_sibling)]
    from_chips = _chip_exchange(chip_sums)
    shared = _sibling_share([_sum_chips(where, s, r, reduced[n], "grad_sum_chips_" + n)
                             for n, s, r in zip(late, chip_sums, from_chips)])
    reduced.update(zip(late, shared))
    reduced = [reduced[n] for n in BIG]

    gvec = _all_reduce_small(_pack_vectors(
        lambda n: jnp.stack([grads[l][n][0] for l in range(DEPTH)]), drel[:, :3 * HPG],
        jnp.stack([grads[l]["conv_dw"] for l in range(DEPTH)]), "pack_vector_grads", loss_part))
    loss = gvec[LOSS_ROW, 0]
    lens = {n: a[n].shape[1] for n, _ in VEC_ROWS}
    g_vec, g_rel = _unpack_vectors(gvec, lens)
    g_cdw = lax.dynamic_slice_in_dim(gvec[CDW_ROW:CDW_ROW + CDW_GROWS].reshape(DEPTH, KSIZE, CW), chip * (CW // 4),
                                     CW // 4, axis=2)

    grad, delta, new_m, new_v = {}, {}, {}, {}
    for n, g in zip(BIG, reduced):
        shape = a[n].shape
        flat2 = lambda t: t.reshape(shape[0] * shape[1], shape[2])
        d, m2, v2 = _adamw(flat2(a[n]), flat2(g), flat2(mom[n]), flat2(var[n]), "adamw_" + n)
        grad[n], delta[n], new_m[n], new_v[n] = g, d.reshape(shape), m2.reshape(shape), v2.reshape(shape)
    shape = a["conv_dw"].shape
    flat2 = lambda t: t.reshape(shape[0] * shape[1], shape[2])
    d, m2, v2 = _adamw(flat2(a["conv_dw"]), flat2(g_cdw), flat2(mom["conv_dw"]), flat2(var["conv_dw"]), "adamw_conv_dw")
    grad["conv_dw"], delta["conv_dw"], new_m["conv_dw"], new_v["conv_dw"] = (
        g_cdw, d.reshape(shape), m2.reshape(shape), v2.reshape(shape))
    zero_cdw = jnp.zeros((DEPTH, KSIZE, CW), F32)
    pk = lambda src, name: _pack_vectors(lambda n: src[n], src["rel_bias"], zero_cdw, name)
    d, m2, v2 = _adamw(pk(a, "pack_vector_w"), gvec, pk(mom, "pack_vector_m"), pk(var, "pack_vector_v"),
                       "adamw_vectors")
    for src, dst in ((d, delta), (m2, new_m), (v2, new_v)):
        vec, rel = _unpack_vectors(src, lens)
        dst.update(vec)
        dst["rel_bias"] = rel
    grad.update(g_vec)
    grad["rel_bias"] = g_rel

    outs = [loss, gx[None]]
    for group in (grad, delta, new_m, new_v):
        outs += [group[n] for n in WEIGHT_NAMES]
    return tuple(outs)
```

```python
import functools
import math

import jax
import jax.numpy as jnp
from jax import lax
from jax.experimental import pallas as pl
from jax.experimental.pallas import tpu as pltpu

F32 = jnp.float32
BF16 = jnp.bfloat16
I32 = jnp.int32

D = 1024
DEPTH = 2
N_MEM = 256
CW = 512
KSIZE = 31
PAD = KSIZE // 2
DILS = (1, 4, 16)
RADIUS = 64
HPG = 4
HD = 64
GW = HPG * HD
MH = 4
MHD = 128
MW = MH * MHD
FH = 2816
NIN = 6912
C1 = 2 * CW
R_ATT = C1
R_MEM = R_ATT + 9 * GW
R_GATE = R_MEM + MW
Z_GATE = 0
Z_CONV = 3 * D
Z_MEM = Z_CONV + C1
Z_ATT = Z_MEM + MW
NUM_BUCKETS = 32
MAX_DISTANCE = 1024
RMS_EPS = 1e-6
LN_EPS = 1e-5
NEG_INF = -1e30
ATT_SCALE = HD ** -0.5
MEM_SCALE = MHD ** -0.5

ADAM_LR = 0.001
ADAM_B1 = 0.9
ADAM_B2 = 0.999
ADAM_EPS = 1e-08
ADAM_WD = 0.01
ADAM_STEP = 10

VMEM_LIMIT_BYTES = 56 * 1024 * 1024
ATT_QB = 128
ATT_TB = 16 * ATT_QB

MESH = pl.DeviceIdType.MESH


def _params(sem=None):
    return pltpu.CompilerParams(dimension_semantics=sem, vmem_limit_bytes=VMEM_LIMIT_BYTES)


def _sigmoid(v):
    return 1.0 / (1.0 + jnp.exp(-v))


def _dot(a, b):
    return jnp.dot(a, b, preferred_element_type=F32)


def _dot_nt(a, b):
    return lax.dot_general(a, b, (((1,), (1,)), ((), ())), preferred_element_type=F32)


def _dot_tn(a, b):
    return lax.dot_general(a, b, (((0,), (0,)), ((), ())), preferred_element_type=F32)


def _rms_fwd_val(v, g):
    r = lax.rsqrt(jnp.mean(v * v, axis=-1, keepdims=True) + RMS_EPS)
    return v * r * g


def _rms_bwd_val(v, g, dy):
    r = lax.rsqrt(jnp.mean(v * v, axis=-1, keepdims=True) + RMS_EPS)
    vh = v * r
    dvh = dy * g
    dv = r * (dvh - vh * jnp.mean(dvh * vh, axis=-1, keepdims=True))
    return dv, dy * vh


def _row(i):
    return (i, 0)


def _fixed(*_):
    return (0, 0)


def _mm_nn(a, b, tm, out_dtype, name):
    M, K = a.shape
    N = b.shape[1]

    def body(a_ref, b_ref, o_ref):
        o_ref[...] = _dot(a_ref[...], b_ref[...]).astype(out_dtype)

    return pl.pallas_call(
        body, name=name, grid=(M // tm,),
        in_specs=[pl.BlockSpec((tm, K), _row), pl.BlockSpec((K, N), _fixed, pipeline_mode=pl.Buffered(1))],
        out_specs=pl.BlockSpec((tm, N), _row),
        out_shape=jax.ShapeDtypeStruct((M, N), out_dtype),
        compiler_params=_params(("parallel",)),
    )(a, b)


def _mm_nt_rms_bwd(a, b, x, g, dres, tm, name):
    M, N = a.shape

    def body(a_ref, b_ref, x_ref, g_ref, dres_ref, dx_ref, dg_ref):
        @pl.when(pl.program_id(0) == 0)
        def _():
            dg_ref[...] = jnp.zeros_like(dg_ref)

        dv, dgr = _rms_bwd_val(x_ref[...], g_ref[...], _dot_nt(a_ref[...], b_ref[...]))
        dx_ref[...] = dres_ref[...] + dv
        dg_ref[...] += jnp.sum(dgr, axis=0, keepdims=True)

    rows = pl.BlockSpec((tm, D), _row)
    return pl.pallas_call(
        body, name=name, grid=(M // tm,),
        in_specs=[pl.BlockSpec((tm, N), _row), pl.BlockSpec((D, N), _fixed, pipeline_mode=pl.Buffered(1)), rows,
                  pl.BlockSpec((1, D), _fixed), rows],
        out_specs=[rows, pl.BlockSpec((1, D), _fixed)],
        out_shape=[jax.ShapeDtypeStruct((M, D), F32), jax.ShapeDtypeStruct((1, D), F32)],
        compiler_params=_params(("arbitrary",)),
    )(a, b, x, g, dres)


def _mm_tn(a, b, ts, tn, name):
    S, K = a.shape
    N = b.shape[1]

    def body(a_ref, b_ref, o_ref):
        @pl.when(pl.program_id(1) == 0)
        def _():
            o_ref[...] = jnp.zeros_like(o_ref)

        o_ref[...] += _dot_tn(a_ref[...], b_ref[...])

    return pl.pallas_call(
        body, name=name, grid=(N // tn, S // ts),
        in_specs=[pl.BlockSpec((ts, K), lambda j, s: (s, 0)), pl.BlockSpec((ts, tn), lambda j, s: (s, j))],
        out_specs=pl.BlockSpec((K, tn), lambda j, s: (0, j)),
        out_shape=jax.ShapeDtypeStruct((K, N), F32),
        compiler_params=_params(("parallel", "arbitrary")),
    )(a, b)


def _rms_h(x, g, name):
    S = x.shape[0]
    T = 512

    def body(x_ref, g_ref, h_ref):
        h_ref[...] = _rms_fwd_val(x_ref[...], g_ref[...]).astype(BF16)

    return pl.pallas_call(
        body, name=name, grid=(S // T,),
        in_specs=[pl.BlockSpec((T, D), _row), pl.BlockSpec((1, D), _fixed)],
        out_specs=pl.BlockSpec((T, D), _row),
        out_shape=jax.ShapeDtypeStruct((S, D), BF16),
        compiler_params=_params(("parallel",)),
    )(x, g)


CONV_T = 256
CONV_HALO = 16
CONV_RC = 32


def _halo_specs(T, halo, S, width, col):
    per = T // halo
    last = S // halo - 1
    return [
        pl.BlockSpec((T, width), lambda i: (i, col)),
        pl.BlockSpec((halo, width), lambda i: (jnp.maximum(i * per - 1, 0), col)),
        pl.BlockSpec((halo, width), lambda i: (jnp.minimum((i + 1) * per, last), col)),
    ]


def _glu(zb):
    zb = zb.astype(F32)
    return zb[:, :CW] * _sigmoid(zb[:, CW:])


CONV_EXT = CONV_T + 2 * CONV_HALO
SUBLANES = 8


def _fill_shifted(sh_ref, ext_ref, cur, prev, nxt):
    T, halo = CONV_T, CONV_HALO
    i = pl.program_id(0)
    n = pl.num_programs(0)
    ext_ref[0:halo, :] = jnp.where(i > 0, prev, 0.0)
    ext_ref[halo:halo + T, :] = cur
    ext_ref[halo + T:CONV_EXT, :] = jnp.where(i < n - 1, nxt, 0.0)
    ext_ref[CONV_EXT:CONV_EXT + SUBLANES, :] = jnp.zeros((SUBLANES, CW), F32)
    for b in range(SUBLANES):
        sh_ref[b] = ext_ref[b:b + CONV_EXT, :]


def _window(sh_ref, start, rows):
    b = start % SUBLANES
    return sh_ref[b, start - b:start - b + rows, :]


def _shifted_scratch():
    return [pltpu.VMEM((CONV_EXT + SUBLANES, CW), F32), pltpu.VMEM((SUBLANES, CONV_EXT, CW), F32)]


def _conv_fwd(z, wdw, bdw, lng, lnb, name):
    S = z.shape[0]
    T, HL, RC = CONV_T, CONV_HALO, CONV_RC

    def body(cur_ref, prev_ref, next_ref, w_ref, b_ref, g_ref, bb_ref, yc_ref, act_ref, ext_ref, sh_ref):
        _fill_shifted(sh_ref, ext_ref, _glu(cur_ref[...]), _glu(prev_ref[...]), _glu(next_ref[...]))
        for c in range(T // RC):
            acc = jnp.zeros((RC, CW), F32)
            for k in range(KSIZE):
                acc = acc + w_ref[k:k + 1, :] * _window(sh_ref, c * RC + k + HL - PAD, RC)
            yc = acc + b_ref[...]
            yc_ref[c * RC:(c + 1) * RC, :] = yc
            mu = jnp.mean(yc, axis=-1, keepdims=True)
            xc = yc - mu
            ln = xc * lax.rsqrt(jnp.mean(xc * xc, axis=-1, keepdims=True) + LN_EPS) * g_ref[...] + bb_ref[...]
            act_ref[c * RC:(c + 1) * RC, :] = (ln * _sigmoid(ln)).astype(BF16)

    return pl.pallas_call(
        body, name=name, grid=(S // T,),
        in_specs=_halo_specs(T, HL, S, C1, Z_CONV // C1) + [pl.BlockSpec((32, CW), _fixed)]
        + [pl.BlockSpec((1, CW), _fixed)] * 3,
        out_specs=[pl.BlockSpec((T, CW), _row), pl.BlockSpec((T, CW), _row)],
        out_shape=[jax.ShapeDtypeStruct((S, CW), F32), jax.ShapeDtypeStruct((S, CW), BF16)],
        scratch_shapes=_shifted_scratch(),
        compiler_params=_params(("parallel",)),
    )(z, z, z, wdw, bdw, lng, lnb)


def _conv_bwd_ln(yc, dact, lng, lnb, name):
    S = yc.shape[0]
    T = 512

    def body(yc_ref, da_ref, g_ref, b_ref, dyc_ref, dg_ref, db_ref, dbias_ref):
        yc_v = yc_ref[...]
        mu = jnp.mean(yc_v, axis=-1, keepdims=True)
        xc = yc_v - mu
        r = lax.rsqrt(jnp.mean(xc * xc, axis=-1, keepdims=True) + LN_EPS)
        yn = xc * r
        ln = yn * g_ref[...] + b_ref[...]
        sg = _sigmoid(ln)
        dln = da_ref[...].astype(F32) * (sg * (1.0 + ln * (1.0 - sg)))
        dyn = dln * g_ref[...]
        dyc = r * (dyn - jnp.mean(dyn, axis=-1, keepdims=True) - yn * jnp.mean(dyn * yn, axis=-1, keepdims=True))
        dyc_ref[...] = dyc

        @pl.when(pl.program_id(0) == 0)
        def _():
            dg_ref[...] = jnp.zeros_like(dg_ref)
            db_ref[...] = jnp.zeros_like(db_ref)
            dbias_ref[...] = jnp.zeros_like(dbias_ref)

        dg_ref[...] += jnp.sum(dln * yn, axis=0, keepdims=True)
        db_ref[...] += jnp.sum(dln, axis=0, keepdims=True)
        dbias_ref[...] += jnp.sum(dyc, axis=0, keepdims=True)

    vec = pl.BlockSpec((1, CW), _fixed)
    return pl.pallas_call(
        body, name=name, grid=(S // T,),
        in_specs=[pl.BlockSpec((T, CW), _row), pl.BlockSpec((T, CW), _row), vec, vec],
        out_specs=[pl.BlockSpec((T, CW), _row), vec, vec, vec],
        out_shape=[jax.ShapeDtypeStruct((S, CW), F32)] + [jax.ShapeDtypeStruct((1, CW), F32)] * 3,
        compiler_params=_params(("arbitrary",)),
    )(yc, dact, lng, lnb)


def _conv_bwd_dw(z, dyc, wdw, dz, name):
    S = z.shape[0]
    T, HL, RC = CONV_T, CONV_HALO, CONV_RC

    def body(zc_ref, zp_ref, zn_ref, dc_ref, dp_ref, dn_ref, w_ref, dz_in, dz_ref, dw_ref, uext_ref, ush_ref,
             dext_ref, dsh_ref, dwacc_ref):
        _fill_shifted(ush_ref, uext_ref, _glu(zc_ref[...]), _glu(zp_ref[...]), _glu(zn_ref[...]))
        _fill_shifted(dsh_ref, dext_ref, dc_ref[...], dp_ref[...], dn_ref[...])

        @pl.when(pl.program_id(0) == 0)
        def _():
            dwacc_ref[...] = jnp.zeros_like(dwacc_ref)

        for c in range(T // RC):
            dcur = dc_ref[c * RC:(c + 1) * RC, :]
            du = jnp.zeros((RC, CW), F32)
            for k in range(KSIZE):
                du = du + w_ref[k:k + 1, :] * _window(dsh_ref, c * RC + HL + PAD - k, RC)
                prod = dcur * _window(ush_ref, c * RC + k + HL - PAD, RC)
                dwacc_ref[k] += jnp.sum(prod.reshape(RC // SUBLANES, SUBLANES, CW), axis=0)
            zc = zc_ref[c * RC:(c + 1) * RC, :].astype(F32)
            a, gt = zc[:, :CW], zc[:, CW:]
            sg = _sigmoid(gt)
            dz_ref[c * RC:(c + 1) * RC, 0:CW] = (du * sg).astype(BF16)
            dz_ref[c * RC:(c + 1) * RC, CW:C1] = (du * a * sg * (1.0 - sg)).astype(BF16)

        @pl.when(pl.program_id(0) == pl.num_programs(0) - 1)
        def _():
            dw_ref[...] = jnp.sum(dwacc_ref[...], axis=1)

    return pl.pallas_call(
        body, name=name, grid=(S // T,),
        in_specs=_halo_specs(T, HL, S, C1, Z_CONV // C1) + _halo_specs(T, HL, S, CW, 0)
        + [pl.BlockSpec((32, CW), _fixed), pl.BlockSpec(memory_space=pl.ANY)],
        out_specs=[pl.BlockSpec((T, C1), lambda i: (i, Z_CONV // C1)), pl.BlockSpec((32, CW), _fixed)],
        out_shape=[jax.ShapeDtypeStruct(dz.shape, BF16), jax.ShapeDtypeStruct((32, CW), F32)],
        input_output_aliases={7: 0},
        scratch_shapes=_shifted_scratch() + _shifted_scratch() + [pltpu.VMEM((32, SUBLANES, CW), F32)],
        compiler_params=_params(("arbitrary",)),
    )(z, z, z, dyc, dyc, dyc, wdw, dz)


def _t5_bucket(rel):
    nb = NUM_BUCKETS // 2
    max_exact = nb // 2
    ret = jnp.where(rel > 0, nb, 0)
    n = jnp.abs(rel)
    nf = jnp.maximum(n, 1).astype(F32)
    large = max_exact + (jnp.log(nf / max_exact) / math.log(MAX_DISTANCE / max_exact)
                         * (nb - max_exact)).astype(I32)
    large = jnp.minimum(large, nb - 1)
    return ret + jnp.where(n < max_exact, n, large)


def _offsets_qk(nq, nk, shift):
    return lax.broadcasted_iota(I32, (nq, nk), 1) + shift - lax.broadcasted_iota(I32, (nq, nk), 0)


def _bias_table(bk, rb_ref, col, off):
    acc = jnp.zeros(bk.shape, F32)
    for b in range(NUM_BUCKETS):
        acc = jnp.where(bk == b, rb_ref[b, col], acc)
    return jnp.where(jnp.abs(off) <= RADIUS, acc, NEG_INF)


def _to_halves(scr, row0, val):
    rows = val.shape[0]
    v = val.astype(F32)
    scr[0, row0:row0 + rows, :] = v[:, :128]
    scr[1, row0:row0 + rows, :] = v[:, 128:]


ATT_FWD_GROUP = 2
ATT_BWD_GROUP = 1


def _att_units(d, fn, group):
    nj = ATT_TB // (ATT_QB * d)
    if nj == 1:
        def trip(t, c):
            r0 = pl.multiple_of(t * 8, 8)
            for u in range(0, 8, group):
                fn([(r0 + u + v, 0) for v in range(group)])
            return c

        lax.fori_loop(0, d // 8, trip, 0)
        return
    for r in range(d):
        def step(t, c, r=r):
            fn([(r, t * group + u) for u in range(group)])
            return c

        lax.fori_loop(0, nj // group, step, 0)


def _unit_row(r, j, d):
    if isinstance(j, int):
        return j * ATT_QB * d + r
    return pl.multiple_of(j * (ATT_QB * d), ATT_QB) + r


def _att_fwd(z, rel_bias, g, name):
    S = z.shape[0]
    d = DILS[g]
    TB, QB = ATT_TB, ATT_QB
    H = RADIUS * d
    L = S // d
    cq = (Z_ATT + 3 * GW * g) // GW
    ck, cv = cq + 1, cq + 2
    bk = _t5_bucket(_offsets_qk(QB, 2 * QB, -RADIUS) * d)

    def body(rb_ref, bk_ref, q_ref, kc_ref, kp_ref, kn_ref, vc_ref, vp_ref, vn_ref, o_ref, l_ref,
             qs, ks, vs, os_, ls, bias):
        i = pl.program_id(0)

        @pl.when(i == 0)
        def _():
            off = _offsets_qk(QB, 2 * QB, -RADIUS)
            for h in range(HPG):
                bias[h] = _bias_table(bk_ref[...], rb_ref, g * HPG + h, off)

        _to_halves(qs, 0, q_ref[...].astype(F32) * ATT_SCALE)
        for scr, p_ref, c_ref, n_ref in ((ks, kp_ref, kc_ref, kn_ref), (vs, vp_ref, vc_ref, vn_ref)):
            _to_halves(scr, 0, p_ref[...])
            _to_halves(scr, H, c_ref[...])
            _to_halves(scr, H + TB, n_ref[...])

        lo = lax.broadcasted_iota(I32, (QB, 128), 1) < HD

        def units(rjs):
            work = []
            for r, j in rjs:
                row = _unit_row(r, j, d)
                km = lax.broadcasted_iota(I32, (1, 2 * QB), 1) + (i * (TB // d) + j * QB - RADIUS)
                edge = jnp.where(jnp.where(km >= 0, km, L) < L, 0.0, NEG_INF)
                for hf in (0, 1):
                    q2 = qs[hf, pl.ds(row, QB, stride=d), :]
                    k2 = ks[hf, pl.ds(row, 2 * QB, stride=d), :].astype(BF16)
                    v2 = vs[hf, pl.ds(row, 2 * QB, stride=d), :].astype(BF16)
                    qq = jnp.concatenate([jnp.where(lo, q2, 0.0), jnp.where(lo, 0.0, q2)], axis=0).astype(BF16)
                    work.append((row, hf, edge, k2, v2, qq))
            scores = [_dot_nt(qq, k2) for (_, _, _, k2, _, qq) in work]
            probs = []
            for (row, hf, edge, *_), ss in zip(work, scores):
                es, stats = [], []
                for hh in (0, 1):
                    s = ss[hh * QB:(hh + 1) * QB] + bias[2 * hf + hh] + edge
                    m = jnp.max(s, axis=-1, keepdims=True)
                    e = jnp.exp(s - m)
                    den = jnp.sum(e, axis=-1, keepdims=True)
                    es.append(e.astype(BF16))
                    stats.append((1.0 / den, m + jnp.log(den)))
                probs.append((jnp.concatenate(es, axis=0), stats))
            for (row, hf, _, _, v2, _), (ee, stats) in zip(work, probs):
                oo = _dot(ee, v2)
                os_[hf, pl.ds(row, QB, stride=d), :] = jnp.where(lo, oo[:QB] * stats[0][0], oo[QB:] * stats[1][0])
                ls[hf, pl.ds(row, QB, stride=d), :] = jnp.where(lo, stats[0][1], stats[1][1])

        _att_units(d, units, ATT_FWD_GROUP)
        for hf in (0, 1):
            o_ref[:, hf * 128:(hf + 1) * 128] = os_[hf].astype(BF16)
            l_ref[:, hf * 128:(hf + 1) * 128] = ls[hf]

    def halo3(col):
        c, p, n = _halo_specs(TB, H, S, GW, col)
        return [c, p, n]

    return pl.pallas_call(
        body, name=name, grid=(S // TB,),
        in_specs=[pl.BlockSpec(memory_space=pltpu.SMEM), pl.BlockSpec((QB, 2 * QB), _fixed),
                  pl.BlockSpec((TB, GW), lambda i: (i, cq))] + halo3(ck) + halo3(cv),
        out_specs=[pl.BlockSpec((TB, GW), _row), pl.BlockSpec((TB, GW), _row)],
        out_shape=[jax.ShapeDtypeStruct((S, GW), BF16), jax.ShapeDtypeStruct((S, GW), F32)],
        scratch_shapes=[pltpu.VMEM((2, TB, 128), F32), pltpu.VMEM((2, TB + 2 * H, 128), F32),
                        pltpu.VMEM((2, TB + 2 * H, 128), F32), pltpu.VMEM((2, TB, 128), F32),
                        pltpu.VMEM((2, TB, 128), F32), pltpu.VMEM((HPG, QB, 2 * QB), F32)],
        compiler_params=_params(("arbitrary",)),
    )(rel_bias, bk, z, z, z, z, z, z, z)


def _att_combine(os3, ls3, name):
    S = os3[0].shape[0]
    T = 1024

    def body(o1, o2, o3, l1, l2, l3, o_ref, l_ref):
        lv = [l1[...], l2[...], l3[...]]
        m = jnp.maximum(jnp.maximum(lv[0], lv[1]), lv[2])
        e = [jnp.exp(v - m) for v in lv]
        den = e[0] + e[1] + e[2]
        acc = jnp.zeros_like(m)
        for ev, o in zip(e, (o1, o2, o3)):
            acc = acc + (ev / den) * o[...].astype(F32)
        o_ref[...] = acc.astype(BF16)
        l_ref[...] = m + jnp.log(den)

    blk = pl.BlockSpec((T, GW), _row)
    return pl.pallas_call(
        body, name=name, grid=(S // T,), in_specs=[blk] * 6, out_specs=[blk, blk],
        out_shape=[jax.ShapeDtypeStruct((S, GW), BF16), jax.ShapeDtypeStruct((S, GW), F32)],
        compiler_params=_params(("parallel",)),
    )(*os3, *ls3)


def _att_prep(do, o, lse, name):
    S = do.shape[0]
    T = 1024

    def body(do_ref, o_ref, l_ref, out_ref):
        prod = do_ref[...].astype(F32) * o_ref[...].astype(F32)
        dd = [jnp.broadcast_to(jnp.sum(prod[:, h * HD:(h + 1) * HD], axis=-1, keepdims=True), (T, HD))
              for h in range(HPG)]
        lane = lax.broadcasted_iota(I32, (T, GW), 1)
        out_ref[...] = jnp.where(lane % HD < HD // 2, l_ref[...], jnp.concatenate(dd, axis=-1))

    blk = pl.BlockSpec((T, GW), _row)
    return pl.pallas_call(
        body, name=name, grid=(S // T,), in_specs=[blk] * 3, out_specs=blk,
        out_shape=jax.ShapeDtypeStruct((S, GW), F32), compiler_params=_params(("parallel",)),
    )(do, o, lse)


def _att_bwd(z, rel_bias, do, ld, dz, g, name):
    S = z.shape[0]
    d = DILS[g]
    TB, QB = ATT_TB, ATT_QB
    H = RADIUS * d
    L = S // d
    E = TB + 2 * H
    cq = (Z_ATT + 3 * GW * g) // GW
    ck, cv = cq + 1, cq + 2
    bk_a = _t5_bucket(_offsets_qk(QB, 2 * QB, -RADIUS) * d)
    bk_b = _t5_bucket(-_offsets_qk(QB, 2 * QB, -RADIUS) * d)

    def body(rb_ref, bka_ref, bkb_ref, *refs):
        ins, (dz_ref, db_ref) = refs[:15], refs[16:18]
        qs, ks, vs, dos, ls, dqs, dks, dvs, bias_a, bias_b, dbias = refs[18:]
        i = pl.program_id(0)
        n = pl.num_programs(0)

        @pl.when(i == 0)
        def _():
            off = _offsets_qk(QB, 2 * QB, -RADIUS)
            for h in range(HPG):
                bias_a[h] = _bias_table(bka_ref[...], rb_ref, g * HPG + h, off)
                bias_b[h] = _bias_table(bkb_ref[...], rb_ref, g * HPG + h, off)
            dbias[...] = jnp.zeros_like(dbias)

        for a, scr in enumerate((qs, ks, vs, dos, ls)):
            c_ref, p_ref, n_ref = ins[3 * a:3 * a + 3]
            pre = (lambda v: v.astype(F32) * ATT_SCALE) if a == 0 else (lambda v: v)
            _to_halves(scr, 0, pre(p_ref[...]))
            _to_halves(scr, H, pre(c_ref[...]))
            _to_halves(scr, H + TB, pre(n_ref[...]))

        lo = lax.broadcasted_iota(I32, (QB, 128), 1) < HD

        def split(v):
            return jnp.concatenate([jnp.where(lo, v, 0.0), jnp.where(lo, 0.0, v)], axis=0).astype(BF16)

        def halves(v):
            return v[:QB], v[QB:]

        def units(rjs):
            work = []
            for r, j in rjs:
                row = _unit_row(r, j, d)
                cur = row + H
                m0 = i * (TB // d) + j * QB - RADIUS
                km = lax.broadcasted_iota(I32, (1, 2 * QB), 1) + m0
                edge_a = jnp.where(jnp.where(km >= 0, km, L) < L, 0.0, NEG_INF)
                for hf in (0, 1):
                    ld = lambda scr, at, nrow: scr[hf, pl.ds(at, nrow, stride=d), :]
                    w = dict(row=row, hf=hf, edge=edge_a, l_c=ld(ls, cur, QB), l_t=ld(ls, row, 2 * QB).T)
                    for nm, scr in (("q", qs), ("k", ks), ("v", vs), ("do", dos)):
                        w[nm + "_c"] = split(ld(scr, cur, QB))
                        w[nm + "_e"] = ld(scr, row, 2 * QB).astype(BF16)
                    work.append(w)
            for w in work:
                w["s"] = halves(_dot_nt(w["q_c"], w["k_e"]))
                w["dp"] = halves(_dot_nt(w["do_c"], w["v_e"]))
                w["s2"] = halves(_dot_nt(w["k_c"], w["q_e"]))
                w["dp2"] = halves(_dot_nt(w["v_c"], w["do_e"]))
            for w in work:
                w["ds"], w["p2"], w["ds2"] = [], [], []
                for hh in (0, 1):
                    h, c0 = 2 * w["hf"] + hh, HD * hh
                    l_c, l_t = w["l_c"], w["l_t"]
                    p = jnp.exp(w["s"][hh] + bias_a[h] + w["edge"] - l_c[:, c0:c0 + 1])
                    ds = p * (w["dp"][hh] - l_c[:, c0 + HD // 2:c0 + HD // 2 + 1])
                    dbias[h] += ds
                    p2 = jnp.exp(w["s2"][hh] + bias_b[h] + w["edge"] - l_t[c0:c0 + 1, :])
                    ds2 = p2 * (w["dp2"][hh] - l_t[c0 + HD // 2:c0 + HD // 2 + 1, :])
                    w["ds"].append(ds.astype(BF16))
                    w["p2"].append(p2.astype(BF16))
                    w["ds2"].append(ds2.astype(BF16))
            for w in work:
                at = pl.ds(w["row"], QB, stride=d)
                both = lambda pair, rhs: halves(_dot(jnp.concatenate(pair, axis=0), rhs))
                dq = both(w["ds"], w["k_e"])
                dqs[w["hf"], at, :] = jnp.where(lo, dq[0], dq[1]) * ATT_SCALE
                dv = both(w["p2"], w["do_e"])
                dvs[w["hf"], at, :] = jnp.where(lo, dv[0], dv[1])
                dk = both(w["ds2"], w["q_e"])
                dks[w["hf"], at, :] = jnp.where(lo, dk[0], dk[1])

        _att_units(d, units, ATT_BWD_GROUP)
        for a, scr in enumerate((dqs, dks, dvs)):
            for hf in (0, 1):
                dz_ref[:, a * GW + hf * 128:a * GW + (hf + 1) * 128] = scr[hf].astype(BF16)

        @pl.when(i == n - 1)
        def _():
            rows = lax.broadcasted_iota(I32, (NUM_BUCKETS, 128), 0)
            lanes = lax.broadcasted_iota(I32, (NUM_BUCKETS, 128), 1)
            out = jnp.zeros((NUM_BUCKETS, 128), F32)
            bk = bka_ref[...]
            for h in range(HPG):
                acc = dbias[h]
                for b in range(NUM_BUCKETS):
                    tot = jnp.sum(jnp.sum(jnp.where(bk == b, acc, 0.0), axis=1, keepdims=True), axis=0, keepdims=True)
                    out = out + jnp.where((rows == b) & (lanes == h), tot, 0.0)
            db_ref[...] = out

    def halo3(col, width=GW):
        return _halo_specs(TB, H, S, width, col)

    one = pl.Buffered(1)

    def single(specs):
        return [pl.BlockSpec(s.block_shape, s.index_map, pipeline_mode=one) for s in specs]

    in_specs = ([pl.BlockSpec(memory_space=pltpu.SMEM), pl.BlockSpec((QB, 2 * QB), _fixed),
                 pl.BlockSpec((QB, 2 * QB), _fixed)]
                + single(halo3(cq) + halo3(ck) + halo3(cv) + halo3(0) + halo3(0))
                + [pl.BlockSpec(memory_space=pl.ANY)])
    return pl.pallas_call(
        body, name=name, grid=(S // TB,), in_specs=in_specs,
        out_specs=[pl.BlockSpec((TB, 3 * GW), lambda i: (i, cq // 3)), pl.BlockSpec((NUM_BUCKETS, 128), _fixed)],
        out_shape=[jax.ShapeDtypeStruct(dz.shape, BF16), jax.ShapeDtypeStruct((NUM_BUCKETS, 128), F32)],
        input_output_aliases={18: 0},
        scratch_shapes=[pltpu.VMEM((2, E, 128), F32)] * 5 + [pltpu.VMEM((2, TB, 128), F32)] * 3
        + [pltpu.VMEM((HPG, QB, 2 * QB), F32)] * 3,
        compiler_params=_params(("arbitrary",)),
    )(rel_bias, bk_a, bk_b, z, z, z, z, z, z, z, z, z, do, do, do, ld, ld, ld, dz)


def _memkv_fwd(mem, gm, wkv, name):
    def body(m_ref, g_ref, w_ref, hm_ref, kv_ref):
        hm = _rms_fwd_val(m_ref[...], g_ref[...]).astype(BF16)
        hm_ref[...] = hm
        kv_ref[...] = _dot(hm, w_ref[...]).astype(BF16)

    return pl.pallas_call(
        body, name=name,
        out_shape=[jax.ShapeDtypeStruct((N_MEM, D), BF16), jax.ShapeDtypeStruct((N_MEM, 2 * MW), BF16)],
        compiler_params=_params(),
    )(mem, gm, wkv)


def _memkv_bwd(mem, gm, hm, wkv, dkv, name):
    def body(m_ref, g_ref, hm_ref, w_ref, dkv_ref, dw_ref, dg_ref):
        dkv_b = dkv_ref[...].astype(BF16)
        dw_ref[...] = _dot_tn(hm_ref[...], dkv_b)
        dhm = _dot_nt(dkv_b, w_ref[...])
        _, dgr = _rms_bwd_val(m_ref[...], g_ref[...], dhm)
        dg_ref[...] = jnp.sum(dgr, axis=0, keepdims=True)

    return pl.pallas_call(
        body, name=name,
        out_shape=[jax.ShapeDtypeStruct((D, 2 * MW), F32), jax.ShapeDtypeStruct((1, D), F32)],
        compiler_params=_params(),
    )(mem, gm, hm, wkv, dkv)


MEM_T = 512


def _mem_q_spec():
    return pl.BlockSpec((MEM_T, MW), lambda i: (i, Z_MEM // MW))


def _memattn_fwd(z, kv, name):
    S = z.shape[0]
    T = MEM_T

    def body(q_ref, kv_ref, o_ref):
        for h in range(MH):
            kh = kv_ref[:, h * MHD:(h + 1) * MHD]
            vh = kv_ref[:, MW + h * MHD:MW + (h + 1) * MHD]
            s = _dot_nt(q_ref[:, h * MHD:(h + 1) * MHD], kh) * MEM_SCALE
            e = jnp.exp(s - jnp.max(s, axis=-1, keepdims=True))
            p = e / jnp.sum(e, axis=-1, keepdims=True)
            o_ref[:, h * MHD:(h + 1) * MHD] = _dot(p.astype(BF16), vh).astype(BF16)

    return pl.pallas_call(
        body, name=name, grid=(S // T,),
        in_specs=[_mem_q_spec(), pl.BlockSpec((N_MEM, 2 * MW), _fixed)],
        out_specs=pl.BlockSpec((T, MW), _row),
        out_shape=jax.ShapeDtypeStruct((S, MW), BF16),
        compiler_params=_params(("parallel",)),
    )(z, kv)


def _memattn_bwd(z, kv, dom, dz, name):
    S = z.shape[0]
    T = MEM_T

    def body(q_ref, kv_ref, do_ref, dz_in, dq_ref, dkv_ref):
        @pl.when(pl.program_id(0) == 0)
        def _():
            dkv_ref[...] = jnp.zeros_like(dkv_ref)

        for h in range(MH):
            kh = kv_ref[:, h * MHD:(h + 1) * MHD]
            vh = kv_ref[:, MW + h * MHD:MW + (h + 1) * MHD]
            qh = q_ref[:, h * MHD:(h + 1) * MHD]
            doh = do_ref[:, h * MHD:(h + 1) * MHD]
            s = _dot_nt(qh, kh) * MEM_SCALE
            e = jnp.exp(s - jnp.max(s, axis=-1, keepdims=True))
            p = e / jnp.sum(e, axis=-1, keepdims=True)
            dkv_ref[:, MW + h * MHD:MW + (h + 1) * MHD] += _dot_tn(p.astype(BF16), doh)
            dp = _dot_nt(doh, vh)
            ds = (p * (dp - jnp.sum(dp * p, axis=-1, keepdims=True))).astype(BF16)
            dq_ref[:, h * MHD:(h + 1) * MHD] = (_dot(ds, kh) * MEM_SCALE).astype(BF16)
            dkv_ref[:, h * MHD:(h + 1) * MHD] += _dot_tn(ds, qh) * MEM_SCALE

    return pl.pallas_call(
        body, name=name, grid=(S // T,),
        in_specs=[_mem_q_spec(), pl.BlockSpec((N_MEM, 2 * MW), _fixed), pl.BlockSpec((T, MW), _row),
                  pl.BlockSpec(memory_space=pl.ANY)],
        out_specs=[_mem_q_spec(), pl.BlockSpec((N_MEM, 2 * MW), _fixed)],
        out_shape=[jax.ShapeDtypeStruct(dz.shape, BF16), jax.ShapeDtypeStruct((N_MEM, 2 * MW), F32)],
        input_output_aliases={3: 0},
        compiler_params=_params(("arbitrary",)),
    )(z, kv, dom, dz)


MERGE_T = 512


def _gate_spec(T):
    return pl.BlockSpec((T, 3 * D), lambda i: (i, Z_GATE // (3 * D)))


def _branches(ca_ref, oa_ref, om_ref, wco_ref, wao_ref, wmo_ref, zg_ref, bg_ref):
    ys = [_dot(ca_ref[...], wco_ref[...]), _dot(oa_ref[...], wao_ref[...]), _dot(om_ref[...], wmo_ref[...])]
    gs = [_sigmoid(zg_ref[:, b * D:(b + 1) * D].astype(F32) + bg_ref[:, b * D:(b + 1) * D]) for b in range(3)]
    return ys, gs


def _merge_fwd(x, cact, oatt, om, z, wco, wao, wmo, wout, bgate, gpost, gnext, name):
    S = x.shape[0]
    T = MERGE_T

    def body(x_ref, ca_ref, oa_ref, om_ref, zg_ref, wco_ref, wao_ref, wmo_ref, wout_ref, bg_ref, gp_ref, gn_ref,
             x1_ref, mg_ref, t_ref, h_ref):
        ys, gs = _branches(ca_ref, oa_ref, om_ref, wco_ref, wao_ref, wmo_ref, zg_ref, bg_ref)
        mb = (gs[0] * ys[0] + gs[1] * ys[1] + gs[2] * ys[2]).astype(BF16)
        t = _dot(mb, wout_ref[...])
        mg_ref[...] = mb
        t_ref[...] = t
        x1 = x_ref[...] + _rms_fwd_val(t, gp_ref[...])
        x1_ref[...] = x1
        h_ref[...] = _rms_fwd_val(x1, gn_ref[...]).astype(BF16)

    full = lambda a: pl.BlockSpec(a.shape, _fixed)
    return pl.pallas_call(
        body, name=name, grid=(S // T,),
        in_specs=[pl.BlockSpec((T, D), _row), pl.BlockSpec((T, CW), _row), pl.BlockSpec((T, GW), _row),
                  pl.BlockSpec((T, MW), _row), _gate_spec(T)]
        + [full(wco), full(wao), full(wmo), full(wout), full(bgate), full(gpost), full(gnext)],
        out_specs=[pl.BlockSpec((T, D), _row)] * 4,
        out_shape=[jax.ShapeDtypeStruct((S, D), F32), jax.ShapeDtypeStruct((S, D), BF16),
                   jax.ShapeDtypeStruct((S, D), F32), jax.ShapeDtypeStruct((S, D), BF16)],
        compiler_params=_params(("parallel",)),
    )(x, cact, oatt, om, z, wco, wao, wmo, wout, bgate, gpost, gnext)


def _merge_bwd(dx1, t, mg, cact, oatt, om, z, wco, wao, wmo, wout, bgate, gpost, name):
    S = dx1.shape[0]
    T = MERGE_T

    def body(dx_ref, t_ref, mg_ref, ca_ref, oa_ref, om_ref, zg_ref, wco_ref, wao_ref, wmo_ref, wout_ref,
             bg_ref, gp_ref, dzg_ref, dca_ref, doa_ref, dom_ref, dwco_ref, dwao_ref, dwmo_ref, dwout_ref,
             dbg_ref, dgp_ref):
        accs = (dwco_ref, dwao_ref, dwmo_ref, dwout_ref, dbg_ref, dgp_ref)

        @pl.when(pl.program_id(0) == 0)
        def _():
            for a in accs:
                a[...] = jnp.zeros_like(a)

        dt, dgr = _rms_bwd_val(t_ref[...], gp_ref[...], dx_ref[...])
        dgp_ref[...] += jnp.sum(dgr, axis=0, keepdims=True)
        dtb = dt.astype(BF16)
        dwout_ref[...] += _dot_tn(mg_ref[...], dtb)
        dm = _dot_nt(dtb, wout_ref[...])
        ys, gs = _branches(ca_ref, oa_ref, om_ref, wco_ref, wao_ref, wmo_ref, zg_ref, bg_ref)
        for b, (act_ref, w_ref, dw_ref, da_ref) in enumerate(
                ((ca_ref, wco_ref, dwco_ref, dca_ref), (oa_ref, wao_ref, dwao_ref, doa_ref),
                 (om_ref, wmo_ref, dwmo_ref, dom_ref))):
            dzg = dm * ys[b] * gs[b] * (1.0 - gs[b])
            dzg_ref[:, b * D:(b + 1) * D] = dzg.astype(BF16)
            dbg_ref[:, b * D:(b + 1) * D] += jnp.sum(dzg, axis=0, keepdims=True)
            dy = (dm * gs[b]).astype(BF16)
            dw_ref[...] += _dot_tn(act_ref[...], dy)
            da_ref[...] = _dot_nt(dy, w_ref[...]).astype(BF16)

    full = lambda a: pl.BlockSpec(a.shape, _fixed)
    fullf = lambda a: jax.ShapeDtypeStruct(a.shape, F32)
    return pl.pallas_call(
        body, name=name, grid=(S // T,),
        in_specs=[pl.BlockSpec((T, D), _row), pl.BlockSpec((T, D), _row), pl.BlockSpec((T, D), _row),
                  pl.BlockSpec((T, CW), _row), pl.BlockSpec((T, GW), _row), pl.BlockSpec((T, MW), _row)]
        + [_gate_spec(T), full(wco), full(wao), full(wmo), full(wout), full(bgate), full(gpost)],
        out_specs=[_gate_spec(T), pl.BlockSpec((T, CW), _row), pl.BlockSpec((T, GW), _row),
                   pl.BlockSpec((T, MW), _row), full(wco), full(wao), full(wmo), full(wout), full(bgate), full(gpost)],
        out_shape=[jax.ShapeDtypeStruct((S, NIN), BF16), jax.ShapeDtypeStruct((S, CW), BF16),
                   jax.ShapeDtypeStruct((S, GW), BF16), jax.ShapeDtypeStruct((S, MW), BF16),
                   fullf(wco), fullf(wao), fullf(wmo), fullf(wout), fullf(bgate), fullf(gpost)],
        compiler_params=_params(("arbitrary",)),
    )(dx1, t, mg, cact, oatt, om, z, wco, wao, wmo, wout, bgate, gpost)


FFN_T = 256


def _ffn_fwd(x1, gu, wfo, gpost, gnext, name):
    S = x1.shape[0]
    T = FFN_T

    nxt = gnext is not None

    def body(x_ref, gu_ref, w_ref, gp_ref, *rest):
        x2_ref, f_ref = rest[nxt:nxt + 2]
        gv = gu_ref[:, :FH].astype(F32)
        uv = gu_ref[:, FH:].astype(F32)
        act = (gv * _sigmoid(gv) * uv).astype(BF16)
        f = _dot(act, w_ref[...])
        f_ref[...] = f
        x2 = x_ref[...] + _rms_fwd_val(f, gp_ref[...])
        x2_ref[...] = x2
        if nxt:
            rest[3][...] = _rms_fwd_val(x2, rest[0][...]).astype(BF16)

    return pl.pallas_call(
        body, name=name, grid=(S // T,),
        in_specs=[pl.BlockSpec((T, D), _row), pl.BlockSpec((T, 2 * FH), _row), pl.BlockSpec((FH, D), _fixed),
                  pl.BlockSpec((1, D), _fixed)] + [pl.BlockSpec((1, D), _fixed)] * nxt,
        out_specs=[pl.BlockSpec((T, D), _row)] * (2 + nxt),
        out_shape=[jax.ShapeDtypeStruct((S, D), F32)] * 2 + [jax.ShapeDtypeStruct((S, D), BF16)] * nxt,
        compiler_params=_params(("parallel",)),
    )(x1, gu, wfo, gpost, *([gnext] if nxt else []))


def _ffn_bwd(dx2, f, gu, wfo, gpost, name):
    S = dx2.shape[0]
    T = FFN_T

    def body(dx_ref, f_ref, gu_ref, w_ref, gp_ref, dgu_ref, df_ref, act_ref, dgp_ref):
        @pl.when(pl.program_id(0) == 0)
        def _():
            dgp_ref[...] = jnp.zeros_like(dgp_ref)

        df, dgr = _rms_bwd_val(f_ref[...], gp_ref[...], dx_ref[...])
        dgp_ref[...] += jnp.sum(dgr, axis=0, keepdims=True)
        dfb = df.astype(BF16)
        df_ref[...] = dfb
        dact = _dot_nt(dfb, w_ref[...])
        gv = gu_ref[:, :FH].astype(F32)
        uv = gu_ref[:, FH:].astype(F32)
        sg = _sigmoid(gv)
        silu = gv * sg
        act_ref[...] = (silu * uv).astype(BF16)
        dgu_ref[:, :FH] = (dact * uv * (sg * (1.0 + gv * (1.0 - sg)))).astype(BF16)
        dgu_ref[:, FH:] = (dact * silu).astype(BF16)

    return pl.pallas_call(
        body, name=name, grid=(S // T,),
        in_specs=[pl.BlockSpec((T, D), _row), pl.BlockSpec((T, D), _row), pl.BlockSpec((T, 2 * FH), _row),
                  pl.BlockSpec((FH, D), _fixed), pl.BlockSpec((1, D), _fixed)],
        out_specs=[pl.BlockSpec((T, 2 * FH), _row), pl.BlockSpec((T, D), _row), pl.BlockSpec((T, FH), _row),
                   pl.BlockSpec((1, D), _fixed)],
        out_shape=[jax.ShapeDtypeStruct((S, 2 * FH), BF16), jax.ShapeDtypeStruct((S, D), BF16),
                   jax.ShapeDtypeStruct((S, FH), BF16), jax.ShapeDtypeStruct((1, D), F32)],
        compiler_params=_params(("arbitrary",)),
    )(dx2, f, gu, wfo, gpost)


def _loss_head(y, target, name):
    S = y.shape[0]
    T = 512

    def body(y_ref, t_ref, dy_ref, l_ref):
        @pl.when(pl.program_id(0) == 0)
        def _():
            l_ref[...] = jnp.zeros_like(l_ref)

        e = y_ref[...] - t_ref[...]
        dy_ref[...] = e * (1.0 / D)
        l_ref[...] += (0.5 / D) * jnp.sum(jnp.sum(e * e, axis=1, keepdims=True), axis=0, keepdims=True)

    return pl.pallas_call(
        body, name=name, grid=(S // T,),
        in_specs=[pl.BlockSpec((T, D), _row)] * 2,
        out_specs=[pl.BlockSpec((T, D), _row), pl.BlockSpec((8, 128), _fixed)],
        out_shape=[jax.ShapeDtypeStruct((S, D), F32), jax.ShapeDtypeStruct((8, 128), F32)],
        compiler_params=_params(("arbitrary",)),
    )(y, target)


BIG = ("w_in", "w_conv_out", "w_att_out", "w_mem_kv", "w_mem_out", "w_out", "w_ffn_in", "w_ffn_out")
SMALL = ("rel_bias", "norm_mix_pre", "b_gate", "conv_dw_bias", "conv_ln_g", "conv_ln_b", "norm_mem",
         "norm_mix_post", "norm_ffn_pre", "norm_ffn_post")


def _layer_fwd(l, x, h, mem, w, rest, rel_bias, gnext):
    tag = f"_l{l}"
    z = _mm_nn(h, w["w_in"], 512, BF16, "mm_in" + tag)
    yc, cact = _conv_fwd(z, w["conv_dw"], w["conv_dw_bias"], w["conv_ln_g"], w["conv_ln_b"], "conv_fwd" + tag)
    og, lg = zip(*[_att_fwd(z, rel_bias, g, f"att_fwd_g{g}" + tag) for g in range(3)])
    oatt, lse = _att_combine(og, lg, "att_combine" + tag)
    w = {**w, **rest(oatt)}
    hm, kv = _memkv_fwd(mem, w["norm_mem"], w["w_mem_kv"], "memkv_fwd" + tag)
    om = _memattn_fwd(z, kv, "memattn_fwd" + tag)
    x1, mg, t, h2 = _merge_fwd(x, cact, oatt, om, z, w["w_conv_out"], w["w_att_out"], w["w_mem_out"], w["w_out"],
                               w["b_gate"], w["norm_mix_post"], w["norm_ffn_pre"], "merge_fwd" + tag)
    gu = _mm_nn(h2, w["w_ffn_in"], 512, BF16, "mm_ffn_in" + tag)
    x2, f, *hn = _ffn_fwd(x1, gu, w["w_ffn_out"], w["norm_ffn_post"], gnext, "ffn_fwd" + tag)
    saved = dict(x=x, h=h, z=z, yc=yc, cact=cact, oatt=oatt, lse=lse, hm=hm, kv=kv, om=om, x1=x1, mg=mg, t=t,
                 h2=h2, gu=gu, f=f)
    return x2, (hn[0] if hn else None), saved, w


def _layer_bwd(l, dx2, mem, w, rel_bias, s, on_ffn_grads=None):
    tag = f"_l{l}"
    gr = {}
    dgu, df, act, gr["norm_ffn_post"] = _ffn_bwd(dx2, s["f"], s["gu"], w["w_ffn_out"], w["norm_ffn_post"], "ffn_bwd" + tag)
    gr["w_ffn_out"] = _mm_tn(act, df, 1024, 512, "dw_ffn_out" + tag)
    gr["w_ffn_in"] = _mm_tn(s["h2"], dgu, 2048, 1408, "dw_ffn_in" + tag)
    gpre = w["norm_ffn_pre"]
    if on_ffn_grads is not None:
        gpre = gpre + on_ffn_grads(gr)[0, 0]
    dx1, gr["norm_ffn_pre"] = _mm_nt_rms_bwd(dgu, w["w_ffn_in"], s["x1"], gpre, dx2, 512, "dh_ffn" + tag)
    (dz, dcact, doatt, dom, gr["w_conv_out"], gr["w_att_out"], gr["w_mem_out"], gr["w_out"], gr["b_gate"],
     gr["norm_mix_post"]) = _merge_bwd(dx1, s["t"], s["mg"], s["cact"], s["oatt"], s["om"], s["z"], w["w_conv_out"],
                                       w["w_att_out"], w["w_mem_out"], w["w_out"], w["b_gate"], w["norm_mix_post"],
                                       "merge_bwd" + tag)
    dyc, gr["conv_ln_g"], gr["conv_ln_b"], gr["conv_dw_bias"] = _conv_bwd_ln(
        s["yc"], dcact, w["conv_ln_g"], w["conv_ln_b"], "conv_bwd_ln" + tag)
    dz, dwdw = _conv_bwd_dw(s["z"], dyc, w["conv_dw"], dz, "conv_bwd_dw" + tag)
    gr["conv_dw"] = dwdw[:KSIZE]
    ld = _att_prep(doatt, s["oatt"], s["lse"], "att_prep" + tag)
    drb = []
    for g in range(3):
        dz, db = _att_bwd(s["z"], rel_bias, doatt, ld, dz, g, f"att_bwd_g{g}" + tag)
        drb.append(db)
    dz, dkv = _memattn_bwd(s["z"], s["kv"], dom, dz, "memattn_bwd" + tag)
    gr["w_mem_kv"], gr["norm_mem"] = _memkv_bwd(mem, w["norm_mem"], s["hm"], w["w_mem_kv"], dkv, "memkv_bwd" + tag)
    gr["w_in"] = _mm_tn(s["h"], dz, 2048, 1152, "dw_in" + tag)
    dx, gr["norm_mix_pre"] = _mm_nt_rms_bwd(dz, w["w_in"], s["x"], w["norm_mix_pre"], dx1, 512, "dh_in" + tag)
    return dx, gr, drb


def _rel_bias_total(parts, name):
    def body(*refs):
        out_ref = refs[-1]
        acc = jnp.zeros((NUM_BUCKETS, 128), F32)
        for l in range(DEPTH):
            for g in range(3):
                v = refs[l * 3 + g][...]
                acc = acc + (v if g == 0 else pltpu.roll(v, HPG * g, axis=1))
        out_ref[...] = acc

    return pl.pallas_call(body, name=name, out_shape=jax.ShapeDtypeStruct((NUM_BUCKETS, 128), F32),
                          compiler_params=_params())(*[p for layer in parts for p in layer])


def _local_step(x, mem, target, rel_bias, layer_fns, gmix, on_grads=None, on_last_ffn_grads=None):
    saved, layers = [], []
    h = _rms_h(x, gmix[0], "rms_mix_l0")
    for l in range(DEPTH):
        first, rest = layer_fns[l](x)
        x, h, s, w = _layer_fwd(l, x, h, mem, first, rest, rel_bias, gmix[l + 1] if l + 1 < DEPTH else None)
        saved.append(s)
        layers.append(w)
    dy, lpart = _loss_head(x, target, "loss_head")
    grads = [None] * DEPTH
    drb = [None] * DEPTH
    for l in reversed(range(DEPTH)):
        dy, grads[l], drb[l] = _layer_bwd(l, dy, mem, layers[l], rel_bias, saved[l],
                                          on_last_ffn_grads if l == 0 else None)
        if on_grads is not None and l > 0:
            below = dict(layers[l - 1])
            below["norm_ffn_post"] = below["norm_ffn_post"] + on_grads(l, grads[l])[0, 0]
            layers[l - 1] = below
    return lpart[0, 0], dy, grads, _rel_bias_total(drb, "rel_bias_total")


def _z_cols_from_ref(w):
    att = [w[..., R_ATT + (3 * j + g) * GW:R_ATT + (3 * j + g + 1) * GW] for g in range(3) for j in range(3)]
    return jnp.concatenate([w[..., R_GATE:], w[..., :C1], w[..., R_MEM:R_GATE]] + att, axis=-1)


def _ref_cols_from_z(w):
    att = [w[..., Z_ATT + (3 * g + j) * GW:Z_ATT + (3 * g + j + 1) * GW] for j in range(3) for g in range(3)]
    return jnp.concatenate([w[..., Z_CONV:Z_MEM]] + att + [w[..., Z_MEM:Z_ATT], w[..., Z_GATE:Z_CONV]], axis=-1)


N_CHIPS = 4
SHARD = {"w_in": ((D, NIN // 4), 1), "w_conv_out": ((CW, D // 4), 1), "w_att_out": ((GW, D // 4), 1),
         "w_mem_kv": ((D // 4, 2 * MW), 0), "w_mem_out": ((MW, D // 4), 1), "w_out": ((D // 4, D), 0),
         "w_ffn_in": ((D, 2 * FH // 4), 1), "w_ffn_out": ((FH // 4, D), 0)}
CDW_ROWS = 64
VEC_ROWS = (("norm_mix_pre", 1), ("b_gate", 3), ("conv_dw_bias", 1), ("conv_ln_g", 1), ("conv_ln_b", 1),
            ("norm_mem", 1), ("norm_mix_post", 1), ("norm_ffn_pre", 1), ("norm_ffn_post", 1))
VEC_LROWS = sum(r for _, r in VEC_ROWS)
REL_ROW = DEPTH * VEC_LROWS
CDW_ROW = REL_ROW + 1
CDW_GROWS = DEPTH * KSIZE * CW // D
LOSS_ROW = CDW_ROW + CDW_GROWS
SMALL_ROWS = -(-(LOSS_ROW + 1) // 8) * 8


def _mesh_pos():
    return lax.axis_index("x"), lax.axis_index("y"), lax.axis_index("c")


def _other_chips(x, y):
    chips = [(1 - x, y), (x, 1 - y), (1 - x, 1 - y)]
    return chips, [2 * cx + cy for cx, cy in chips]


NBIG = len(BIG)
ANY_SPEC = pl.BlockSpec(memory_space=pl.ANY)


def _remote(src, dst, send_sems, recv_sems, k, to):
    return pltpu.make_async_remote_copy(src_ref=src, dst_ref=dst, send_sem=send_sems.at[k], recv_sem=recv_sems.at[k],
                                        device_id=to, device_id_type=MESH)


def _half(ref, c):
    h = ref.shape[0] // 2
    return ref.at[pl.ds(c * h if isinstance(c, int) else pl.multiple_of(c * h, 16), h)]


def _all_gather(ws, cdw):
    NBIG = len(ws)

    def body(*refs):
        w_refs, cdw_ref = refs[:NBIG], refs[NBIG]
        g_refs, gc_ref = refs[NBIG + 1:2 * NBIG + 1], refs[2 * NBIG + 1]
        send_sems, recv_sems = refs[2 * NBIG + 2:]
        x, y, c = _mesh_pos()
        j = 2 * x + y
        sibling = (x, y, 1 - c)
        chips, blocks = _other_chips(x, y)
        copy = functools.partial(_remote, send_sems=send_sems, recv_sems=recv_sems)
        pairs = list(zip(w_refs, g_refs))
        first = [copy(_half(w, c), _half(g.at[j], c), k=k * NBIG + n, to=(*chip, c))
                 for k, chip in enumerate(chips) for n, (w, g) in enumerate(pairs)]
        first += [copy(cdw_ref, gc_ref.at[j], k=6 * NBIG + k, to=(*chip, c)) for k, chip in enumerate(chips)]
        for cp in first:
            cp.start()
        passed = []
        for k, b in enumerate(blocks):
            for n, (w, g) in enumerate(pairs):
                copy(_half(w, c), _half(g.at[b], c), k=k * NBIG + n, to=sibling).wait_recv()
            onward = [copy(_half(g.at[b], c), _half(g.at[b], c), k=(3 + k) * NBIG + n, to=sibling)
                      for n, (w, g) in enumerate(pairs)]
            for cp in onward:
                cp.start()
            passed += onward
        for k, b in enumerate(blocks):
            for n, (w, g) in enumerate(pairs):
                copy(_half(w, c), _half(g.at[b], 1 - c), k=(3 + k) * NBIG + n, to=sibling).wait_recv()
            copy(cdw_ref, gc_ref.at[b], k=6 * NBIG + k, to=sibling).wait_recv()
        for cp in first + passed:
            cp.wait_send()

    nsem = 6 * NBIG + 3
    return pl.pallas_call(
        body, name="all_gather_weights",
        out_shape=[jax.ShapeDtypeStruct((N_CHIPS,) + w.shape, BF16) for w in ws]
        + [jax.ShapeDtypeStruct((N_CHIPS, CDW_ROWS, 128), F32)],
        in_specs=[ANY_SPEC] * (NBIG + 1), out_specs=[ANY_SPEC] * (NBIG + 1),
        scratch_shapes=[pltpu.SemaphoreType.DMA((nsem,)), pltpu.SemaphoreType.DMA((nsem,))],
    )(*ws, cdw)


SEM_SPEC = pl.BlockSpec(memory_space=pltpu.SEMAPHORE)
DATAFLOW = pltpu.SideEffectType.DATAFLOW_SIDE_EFFECTING


def _gather_copies(w_refs, g_refs, send_sem, recv_sem):
    x, y, c = _mesh_pos()
    j = 2 * x + y
    chips, _ = _other_chips(x, y)
    return [pltpu.make_async_remote_copy(src_ref=_half(w, c), dst_ref=_half(g.at[j], c), send_sem=send_sem,
                                         recv_sem=recv_sem, device_id=(*chip, cc), device_id_type=MESH)
            for chip in chips for cc in (0, 1) for w, g in zip(w_refs, g_refs)]


def _all_gather_start(ws, after, name):
    NBIG = len(ws)

    def body(*refs):
        w_refs, g_refs = refs[:NBIG], refs[NBIG:2 * NBIG]
        send_sem, recv_sem = refs[2 * NBIG + 1:2 * NBIG + 3]
        token = refs[-1]
        for cp in _gather_copies(w_refs, g_refs, send_sem, recv_sem):
            cp.start()
        token[...] = jnp.zeros_like(token)

    lands = [pltpu.with_memory_space_constraint(lax.empty((N_CHIPS,) + w.shape, BF16), pltpu.HBM) for w in ws]
    ws = [pltpu.with_memory_space_constraint(w, pltpu.HBM) for w in ws]
    hbm = pl.BlockSpec(memory_space=pltpu.HBM)
    out = pl.pallas_call(
        body, name=name,
        out_shape=[pltpu.SemaphoreType.DMA(()), pltpu.SemaphoreType.DMA(())]
        + [pltpu.HBM(w.shape, BF16) for w in ws] + [pltpu.HBM(g.shape, BF16) for g in lands]
        + [jax.ShapeDtypeStruct((8, 128), F32)],
        in_specs=[hbm] * (2 * NBIG) + [ANY_SPEC],
        out_specs=[SEM_SPEC, SEM_SPEC] + [hbm] * (2 * NBIG) + [pl.BlockSpec(memory_space=pltpu.VMEM)],
        input_output_aliases={n: 2 + n for n in range(2 * NBIG)},
        compiler_params=pltpu.CompilerParams(has_side_effects=DATAFLOW),
    )(*ws, *lands, after)
    return out[0], out[1], out[2:2 + NBIG], out[2 + NBIG:2 + 2 * NBIG], out[-1]


def _all_gather_wait(send_sem, recv_sem, ws, lands, after, name):
    NBIG = len(ws)

    def body(*refs):
        w_refs, g_refs = refs[:NBIG], refs[NBIG:2 * NBIG]
        send_sem, recv_sem = refs[2 * NBIG:2 * NBIG + 2]
        x, y, c = _mesh_pos()
        _, blocks = _other_chips(x, y)
        for cp in _gather_copies(w_refs, g_refs, send_sem, recv_sem):
            cp.wait_send()
        for b in blocks:
            for cc in (0, 1):
                for w, g in zip(w_refs, g_refs):
                    pltpu.make_async_remote_copy(src_ref=_half(w, cc), dst_ref=_half(g.at[b], cc), send_sem=send_sem,
                                                 recv_sem=recv_sem, device_id=(x, y, c),
                                                 device_id_type=MESH).wait_recv()

    hbm = pl.BlockSpec(memory_space=pltpu.HBM)
    out = pl.pallas_call(
        body, name=name,
        out_shape=[pltpu.HBM(w.shape, BF16) for w in ws] + [pltpu.HBM(g.shape, BF16) for g in lands],
        in_specs=[hbm] * (2 * NBIG) + [SEM_SPEC, SEM_SPEC, ANY_SPEC],
        out_specs=[hbm] * (2 * NBIG),
        input_output_aliases={n: n for n in range(2 * NBIG)},
        compiler_params=pltpu.CompilerParams(has_side_effects=DATAFLOW),
    )(*ws, *lands, send_sem, recv_sem, after)
    return out[:NBIG], out[NBIG:]


def _half_rows(ref, c):
    h = ref.shape[1] // 2
    return ref.at[:, pl.ds(pl.multiple_of(c * h, 16), h)]


def _sibling_exchange(ps):
    nw = len(ps)

    def body(*refs):
        p_refs, r_refs, (send_sems, recv_sems) = refs[:nw], refs[nw:2 * nw], refs[2 * nw:]
        x, y, c = _mesh_pos()
        cps = [_remote(_half_rows(p, 1 - c), r, send_sems, recv_sems, n, (x, y, 1 - c))
               for n, (p, r) in enumerate(zip(p_refs, r_refs))]
        for cp in cps:
            cp.start()
        for cp in cps:
            cp.wait()

    return pl.pallas_call(
        body, name="grad_sibling_exchange",
        out_shape=[jax.ShapeDtypeStruct((N_CHIPS, p.shape[1] // 2, p.shape[2]), p.dtype) for p in ps],
        in_specs=[ANY_SPEC] * nw, out_specs=[ANY_SPEC] * nw,
        scratch_shapes=[pltpu.SemaphoreType.DMA((nw,)), pltpu.SemaphoreType.DMA((nw,))],
    )(*ps)


SUM_BLOCK_BYTES = 2 * 1024 * 1024


def _sum_rows(s0, s1):
    return s0 if s0 * s1 * 2 <= SUM_BLOCK_BYTES else s0 // 2


def _add_own_half(where, p, r, name):
    _, h, s1 = r.shape
    T = _sum_rows(h, s1)
    nt = h // T

    def body(where_ref, p_ref, r_ref, o_ref):
        o_ref[...] = (p_ref[...].astype(F32) + r_ref[...].astype(F32)).astype(BF16)

    return pl.pallas_call(
        body, name=name,
        grid_spec=pltpu.PrefetchScalarGridSpec(
            num_scalar_prefetch=1, grid=(N_CHIPS, nt),
            in_specs=[pl.BlockSpec((1, T, s1), lambda j, i, wh: (j, wh[0] * nt + i, 0)),
                      pl.BlockSpec((1, T, s1), lambda j, i, wh: (j, i, 0))],
            out_specs=pl.BlockSpec((1, T, s1), lambda j, i, wh: (j, i, 0))),
        out_shape=jax.ShapeDtypeStruct(r.shape, BF16), compiler_params=_params(("parallel", "parallel")),
    )(where, p, r)


def _chip_exchange(as_):
    nw = len(as_)

    def body(*refs):
        a_refs, r_refs, (send_sems, recv_sems) = refs[:nw], refs[nw:2 * nw], refs[2 * nw:]
        x, y, c = _mesh_pos()
        chips, blocks = _other_chips(x, y)
        cps = [_remote(a.at[b], r.at[k], send_sems, recv_sems, k * nw + n, (*chip, c))
               for k, (chip, b) in enumerate(zip(chips, blocks)) for n, (a, r) in enumerate(zip(a_refs, r_refs))]
        for cp in cps:
            cp.start()
        for cp in cps:
            cp.wait_recv()
        for cp in cps:
            cp.wait_send()

    return pl.pallas_call(
        body, name="grad_chip_exchange", out_shape=[jax.ShapeDtypeStruct((3,) + a.shape[1:], a.dtype) for a in as_],
        in_specs=[ANY_SPEC] * nw, out_specs=[ANY_SPEC] * nw,
        scratch_shapes=[pltpu.SemaphoreType.DMA((3 * nw,)), pltpu.SemaphoreType.DMA((3 * nw,))],
    )(*as_)


def _sum_chips(where, a, r, o, name):
    _, h, s1 = a.shape
    T = _sum_rows(h, s1)
    nt = h // T

    def body(where_ref, a_ref, r_ref, o_in, o_ref):
        acc = a_ref[0].astype(F32)
        for k in range(3):
            acc = acc + r_ref[k].astype(F32)
        o_ref[0] = acc

    return pl.pallas_call(
        body, name=name,
        grid_spec=pltpu.PrefetchScalarGridSpec(
            num_scalar_prefetch=1, grid=(nt,),
            in_specs=[pl.BlockSpec((1, T, s1), lambda i, wh: (wh[1], i, 0)),
                      pl.BlockSpec((3, T, s1), lambda i, wh: (0, i, 0)), ANY_SPEC],
            out_specs=pl.BlockSpec((1, T, s1), lambda i, wh: (0, wh[0] * nt + i, 0))),
        out_shape=jax.ShapeDtypeStruct(o.shape, F32), input_output_aliases={3: 0},
        compiler_params=_params(("parallel",)),
    )(where, a, r, o)


def _sibling_share(os_):
    nw = len(os_)

    def body(*refs):
        o_refs, (send_sems, recv_sems) = refs[nw:2 * nw], refs[2 * nw:]
        x, y, c = _mesh_pos()
        mine = lambda o, cc: _half(o.at[0], cc)
        cps = [_remote(mine(o, c), mine(o, c), send_sems, recv_sems, n, (x, y, 1 - c)) for n, o in enumerate(o_refs)]
        for cp in cps:
            cp.start()
        for n, o in enumerate(o_refs):
            _remote(mine(o, c), mine(o, 1 - c), send_sems, recv_sems, n, (x, y, 1 - c)).wait_recv()
        for cp in cps:
            cp.wait_send()

    return pl.pallas_call(
        body, name="grad_sibling_share", out_shape=[jax.ShapeDtypeStruct(o.shape, o.dtype) for o in os_],
        in_specs=[ANY_SPEC] * nw, out_specs=[ANY_SPEC] * nw,
        input_output_aliases={n: n for n in range(nw)},
        scratch_shapes=[pltpu.SemaphoreType.DMA((nw,)), pltpu.SemaphoreType.DMA((nw,))],
    )(*os_)


N_DEV = 8


def _scatter_copies(p_refs, r_refs, send_sem, recv_sem):
    x, y, c = _mesh_pos()
    cps = []
    for m in range(1, N_DEV):
        px, py, pc = x ^ (m >> 2 & 1), y ^ (m >> 1 & 1), c ^ (m & 1)
        for p, r in zip(p_refs, r_refs):
            cps.append(pltpu.make_async_remote_copy(src_ref=p.at[2 * px + py], dst_ref=r.at[m - 1], send_sem=send_sem,
                                                    recv_sem=recv_sem, device_id=(px, py, pc), device_id_type=MESH))
    return cps


def _reduce_start(ps, after, name):
    nw = len(ps)

    def body(*refs):
        p_refs, r_refs = refs[:nw], refs[nw:2 * nw]
        send_sem, recv_sem = refs[2 * nw + 1:2 * nw + 3]
        for cp in _scatter_copies(p_refs, r_refs, send_sem, recv_sem):
            cp.start()
        refs[-1][...] = jnp.zeros_like(refs[-1])

    lands = [pltpu.with_memory_space_constraint(lax.empty((N_DEV - 1,) + p.shape[1:], BF16), pltpu.HBM) for p in ps]
    ps = [pltpu.with_memory_space_constraint(p, pltpu.HBM) for p in ps]
    hbm = pl.BlockSpec(memory_space=pltpu.HBM)
    out = pl.pallas_call(
        body, name=name,
        out_shape=[pltpu.SemaphoreType.DMA(()), pltpu.SemaphoreType.DMA(())]
        + [pltpu.HBM(p.shape, BF16) for p in ps] + [pltpu.HBM(r.shape, BF16) for r in lands]
        + [jax.ShapeDtypeStruct((8, 128), F32)],
        in_specs=[hbm] * (2 * nw) + [ANY_SPEC],
        out_specs=[SEM_SPEC, SEM_SPEC] + [hbm] * (2 * nw) + [pl.BlockSpec(memory_space=pltpu.VMEM)],
        input_output_aliases={n: 2 + n for n in range(2 * nw)},
        compiler_params=pltpu.CompilerParams(has_side_effects=DATAFLOW),
    )(*ps, *lands, after)
    return out[0], out[1], out[2:2 + nw], out[2 + nw:2 + 2 * nw], out[-1]


def _reduce_wait(send_sem, recv_sem, ps, lands, after, name):
    nw = len(ps)

    def body(*refs):
        p_refs, r_refs = refs[:nw], refs[nw:2 * nw]
        send_sem, recv_sem = refs[2 * nw:2 * nw + 2]
        x, y, c = _mesh_pos()
        for cp in _scatter_copies(p_refs, r_refs, send_sem, recv_sem):
            cp.wait_send()
        for m in range(1, N_DEV):
            for p, r in zip(p_refs, r_refs):
                pltpu.make_async_remote_copy(src_ref=p.at[0], dst_ref=r.at[m - 1], send_sem=send_sem,
                                             recv_sem=recv_sem, device_id=(x, y, c), device_id_type=MESH).wait_recv()

    hbm = pl.BlockSpec(memory_space=pltpu.HBM)
    out = pl.pallas_call(
        body, name=name,
        out_shape=[pltpu.HBM(p.shape, BF16) for p in ps] + [pltpu.HBM(r.shape, BF16) for r in lands],
        in_specs=[hbm] * (2 * nw) + [SEM_SPEC, SEM_SPEC, ANY_SPEC],
        out_specs=[hbm] * (2 * nw),
        input_output_aliases={n: n for n in range(2 * nw)},
        compiler_params=pltpu.CompilerParams(has_side_effects=DATAFLOW),
    )(*ps, *lands, send_sem, recv_sem, after)
    return out[:nw], out[nw:]


SUM8_BLOCK_BYTES = 6 * 1024 * 1024


def _sum_devices(where, p, r, layer, o, name):
    _, s0, s1 = p.shape
    T = s0
    while (N_DEV - 1) * T * s1 * 2 > SUM8_BLOCK_BYTES:
        T //= 2

    def body(where_ref, p_ref, r_ref, *rest):
        me = 2 * where_ref[1] + where_ref[0]
        acc = None
        for dev in range(N_DEV):
            m = dev ^ me
            val = jnp.where(m == 0, p_ref[0], r_ref[jnp.maximum(m, 1) - 1]).astype(F32)
            acc = val if acc is None else acc + val
        rest[-1][0] = acc

    given = o is not None
    return pl.pallas_call(
        body, name=name,
        grid_spec=pltpu.PrefetchScalarGridSpec(
            num_scalar_prefetch=1, grid=(s0 // T,),
            in_specs=[pl.BlockSpec((1, T, s1), lambda i, wh: (wh[1], i, 0)),
                      pl.BlockSpec((N_DEV - 1, T, s1), lambda i, wh: (0, i, 0))] + [ANY_SPEC] * given,
            out_specs=pl.BlockSpec((1, T, s1), lambda i, wh: (layer, i, 0))),
        out_shape=jax.ShapeDtypeStruct((DEPTH, s0, s1), F32), input_output_aliases={3: 0} if given else {},
        compiler_params=_params(("parallel",)),
    )(where, p, r, *([o] if given else []))


def _all_reduce_small(sp):
    def body(sp_ref, out_ref, buf, send_sems, recv_sems):
        x, y, c = _mesh_pos()
        me = 4 * x + 2 * y + c
        buf[0] = sp_ref[...]
        cps = []
        for k in range(1, 8):
            peer = (x ^ (k >> 2 & 1), y ^ (k >> 1 & 1), c ^ (k & 1))
            cps.append(pltpu.make_async_remote_copy(src_ref=sp_ref, dst_ref=buf.at[k], send_sem=send_sems.at[k - 1],
                                                    recv_sem=recv_sems.at[k - 1], device_id=peer, device_id_type=MESH))
        for cp in cps:
            cp.start()
        for cp in cps:
            cp.wait_recv()
        for cp in cps:
            cp.wait_send()
        acc = buf[me]
        for p in range(1, 8):
            acc = acc + buf[p ^ me]
        out_ref[...] = acc

    vm = pl.BlockSpec(memory_space=pltpu.VMEM)
    return pl.pallas_call(
        body, name="all_reduce_small", out_shape=jax.ShapeDtypeStruct(sp.shape, F32),
        in_specs=[vm], out_specs=vm,
        scratch_shapes=[pltpu.VMEM((8,) + sp.shape, F32), pltpu.SemaphoreType.DMA((7,)), pltpu.SemaphoreType.DMA((7,))],
        compiler_params=_params(),
    )(sp)


def _adamw(w, g, m, v, name):
    R, C = w.shape
    T = next((t for t in (256, 128) if R % t == 0), R)

    def body(w_ref, g_ref, m_ref, v_ref, d_ref, m2_ref, v2_ref):
        gv = g_ref[...]
        m2 = ADAM_B1 * m_ref[...] + (1.0 - ADAM_B1) * gv
        v2 = ADAM_B2 * v_ref[...] + (1.0 - ADAM_B2) * (gv * gv)
        m_hat = m2 / (1.0 - ADAM_B1 ** ADAM_STEP)
        v_hat = v2 / (1.0 - ADAM_B2 ** ADAM_STEP)
        d_ref[...] = -ADAM_LR * (m_hat / (jnp.sqrt(v_hat) + ADAM_EPS) + ADAM_WD * w_ref[...])
        m2_ref[...] = m2
        v2_ref[...] = v2

    blk = pl.BlockSpec((T, C), _row)
    return pl.pallas_call(
        body, name=name, grid=(R // T,), in_specs=[blk] * 4, out_specs=[blk] * 3,
        out_shape=[jax.ShapeDtypeStruct((R, C), F32)] * 3, compiler_params=_params(("parallel",)),
    )(w, g, m, v)


def _pack_vectors(get, rel, cdw, name, loss=None):
    rows = []
    for l in range(DEPTH):
        for n, r in VEC_ROWS:
            v = get(n)[l]
            rows.append(jnp.pad(v, (0, r * D - v.shape[0])).reshape(r, D))
    rows.append(jnp.pad(rel.reshape(-1), (0, D - NUM_BUCKETS * 3 * HPG)).reshape(1, D))
    rows.append(cdw.reshape(CDW_GROWS, D))
    if loss is not None:
        rows.append(jnp.full((1, D), loss, F32))

    def body(*refs):
        out_ref = refs[-1]
        out_ref[...] = jnp.zeros_like(out_ref)
        at = 0
        for ref in refs[:-1]:
            out_ref[at:at + ref.shape[0], :] = ref[...]
            at += ref.shape[0]

    return pl.pallas_call(body, name=name, out_shape=jax.ShapeDtypeStruct((SMALL_ROWS, D), F32),
                          compiler_params=_params())(*rows)


def _unpack_vectors(packed, lens):
    out = {n: [] for n, _ in VEC_ROWS}
    for l in range(DEPTH):
        at = l * VEC_LROWS
        for n, r in VEC_ROWS:
            out[n].append(packed[at:at + r].reshape(-1)[:lens[n]])
            at += r
    rel = packed[REL_ROW, :NUM_BUCKETS * 3 * HPG].reshape(NUM_BUCKETS, 3 * HPG)
    return {n: jnp.stack(v) for n, v in out.items()}, rel


INPUT_NAMES = ("x", "mem") + ("rel_bias", "norm_mix_pre", "w_in", "b_gate", "conv_dw", "conv_dw_bias", "conv_ln_g",
                              "conv_ln_b", "w_conv_out", "w_att_out", "norm_mem", "w_mem_kv", "w_mem_out", "w_out",
                              "norm_mix_post", "norm_ffn_pre", "w_ffn_in", "w_ffn_out", "norm_ffn_post")
WEIGHT_NAMES = INPUT_NAMES[2:]


def kernel(*args):
    nw = len(WEIGHT_NAMES)
    a = dict(zip(INPUT_NAMES, args[:2 + nw]))
    target = args[2 + nw]
    mom = dict(zip(WEIGHT_NAMES, args[3 + nw:3 + 2 * nw]))
    var = dict(zip(WEIGHT_NAMES, args[3 + 2 * nw:3 + 3 * nw]))
    xi, yi, ci = _mesh_pos()
    chip = 2 * xi + yi
    where = jnp.stack([ci, chip]).astype(I32)

    head, tail = ("w_in",), tuple(n for n in BIG if n != "w_in")
    shard = lambda l, names: [a[n][l].astype(BF16) for n in names]
    cdw = jnp.pad(a["conv_dw"].reshape(DEPTH * KSIZE, CW // 4), ((0, CDW_ROWS - DEPTH * KSIZE), (0, 0)))
    own_head0 = shard(0, head)
    *got_head0, gcdw = _all_gather(own_head0, cdw)
    gcdw = lax.dynamic_update_slice(gcdw, cdw[None], (chip, 0, 0))
    conv_dw = gcdw[:, :DEPTH * KSIZE].reshape(N_CHIPS, DEPTH, KSIZE, CW // 4).transpose(1, 2, 0, 3)
    conv_dw = jnp.pad(conv_dw.reshape(DEPTH, KSIZE, CW), ((0, 0), (0, 1), (0, 0)))
    gmix = [a["norm_mix_pre"][l][None, :] for l in range(DEPTH)]

    def whole(names, gathered, own):
        w = {}
        for n, g, s in zip(names, gathered, own):
            (s0, s1), axis = SHARD[n]
            blk = lax.dynamic_update_slice(g, s[None], (chip, 0, 0))
            w[n] = blk.reshape(N_CHIPS * s0, s1) if axis == 0 else blk.transpose(1, 0, 2).reshape(s0, N_CHIPS * s1)
        if "w_in" in w:
            w["w_in"] = _z_cols_from_ref(w["w_in"])
        return w

    def vectors(l):
        w = {n: a[n][l][None, :] for n, _ in VEC_ROWS}
        w["conv_dw"] = conv_dw[l]
        return w

    def landed(flight, after, name):
        send_sem, recv_sem, thru, lands, _ = flight
        return _all_gather_wait(send_sem, recv_sem, thru, lands, after, name)

    flights = {"tail0": _all_gather_start(shard(0, tail), got_head0[0], "all_gather_start_tail0")}
    gmix[0] = gmix[0] + flights["tail0"][4][0, 0]

    def layer0(x):
        def rest(after):
            own, got = landed(flights["tail0"], after, "all_gather_wait_tail0")
            flights["head1"] = _all_gather_start(shard(1, head), got[0], "all_gather_start_head1")
            w = whole(tail, got, own)
            w["b_gate"] = a["b_gate"][0][None, :] + flights["head1"][4][0, 0]
            return w

        return {**vectors(0), **whole(head, got_head0, own_head0)}, rest

    def layer1(x):
        own, got = landed(flights["head1"], x, "all_gather_wait_head1")
        flights["tail1"] = _all_gather_start(shard(1, tail), got[0], "all_gather_start_tail1")
        first = {**vectors(1), **whole(head, got, own)}
        first["conv_dw_bias"] = first["conv_dw_bias"] + flights["tail1"][4][0, 0]

        def rest(after):
            own_t, got_t = landed(flights["tail1"], after, "all_gather_wait_tail1")
            return whole(tail, got_t, own_t)

        return first, rest

    def by_chip(layer_grads, names):
        out = []
        for n in names:
            (s0, s1), axis = SHARD[n]
            g = _ref_cols_from_z(layer_grads[n]) if n == "w_in" else layer_grads[n]
            g = g.reshape(N_CHIPS, s0, s1) if axis == 0 else g.reshape(s0, N_CHIPS, s1).transpose(1, 0, 2)
            out.append(g.astype(BF16))
        return out

    early = ("w_ffn_in", "w_ffn_out")
    late = tuple(n for n in BIG if n not in early)
    scattering = {}

    def on_grads(l, layer_grads):
        scattering["l1"] = _reduce_start(by_chip(layer_grads, BIG), layer_grads["w_in"], "grad_reduce_start_l1")
        return scattering["l1"][4]

    def on_last_ffn_grads(layer_grads):
        scattering["ffn"] = _reduce_start(by_chip(layer_grads, early), layer_grads["w_ffn_in"], "grad_reduce_start_ffn")
        return scattering["ffn"][4]

    loss_part, gx, grads, drel = _local_step(a["x"][0], a["mem"][0], target[0], a["rel_bias"],
                                             [layer0, layer1], gmix, on_grads, on_last_ffn_grads)

    reduced = {}
    send_sem, recv_sem, thru, lands, _ = scattering["l1"]
    sent, landed = _reduce_wait(send_sem, recv_sem, thru, lands, gx, "grad_reduce_wait_l1")
    for n, p, r in zip(BIG, sent, landed):
        reduced[n] = _sum_devices(where, p, r, 1, None, "grad_sum_devices_l1_" + n)
    send_sem, recv_sem, thru, lands, _ = scattering["ffn"]
    sent, landed = _reduce_wait(send_sem, recv_sem, thru, lands, gx, "grad_reduce_wait_ffn")
    for n, p, r in zip(early, sent, landed):
        reduced[n] = _sum_devices(where, p, r, 0, reduced[n], "grad_sum_devices_l0_" + n)
    packed = by_chip(grads[0], late)
    from_sibling = _sibling_exchange(packed)
    chip_sums = [_add_own_half(where, p, r, "grad_add_sibling_" + n) for n, p, r in zip(late, packed, from_sibling)]
    from_chips = _chip_exchange(chip_sums)
    shared = _sibling_share([_sum_chips(where, s, r, reduced[n], "grad_sum_chips_" + n)
                             for n, s, r in zip(late, chip_sums, from_chips)])
    reduced.update(zip(late, shared))
    reduced = [reduced[n] for n in BIG]

    gvec = _all_reduce_small(_pack_vectors(
        lambda n: jnp.stack([grads[l][n][0] for l in range(DEPTH)]), drel[:, :3 * HPG],
        jnp.stack([grads[l]["conv_dw"] for l in range(DEPTH)]), "pack_vector_grads", loss_part))
    loss = gvec[LOSS_ROW, 0]
    lens = {n: a[n].shape[1] for n, _ in VEC_ROWS}
    g_vec, g_rel = _unpack_vectors(gvec, lens)
    g_cdw = lax.dynamic_slice_in_dim(gvec[CDW_ROW:CDW_ROW + CDW_GROWS].reshape(DEPTH, KSIZE, CW), chip * (CW // 4),
                                     CW // 4, axis=2)

    grad, delta, new_m, new_v = {}, {}, {}, {}
    for n, g in zip(BIG, reduced):
        shape = a[n].shape
        flat2 = lambda t: t.reshape(shape[0] * shape[1], shape[2])
        d, m2, v2 = _adamw(flat2(a[n]), flat2(g), flat2(mom[n]), flat2(var[n]), "adamw_" + n)
        grad[n], delta[n], new_m[n], new_v[n] = g, d.reshape(shape), m2.reshape(shape), v2.reshape(shape)
    shape = a["conv_dw"].shape
    flat2 = lambda t: t.reshape(shape[0] * shape[1], shape[2])
    d, m2, v2 = _adamw(flat2(a["conv_dw"]), flat2(g_cdw), flat2(mom["conv_dw"]), flat2(var["conv_dw"]), "adamw_conv_dw")
    grad["conv_dw"], delta["conv_dw"], new_m["conv_dw"], new_v["conv_dw"] = (
        g_cdw, d.reshape(shape), m2.reshape(shape), v2.reshape(shape))
    zero_cdw = jnp.zeros((DEPTH, KSIZE, CW), F32)
    pk = lambda src, name: _pack_vectors(lambda n: src[n], src["rel_bias"], zero_cdw, name)
    d, m2, v2 = _adamw(pk(a, "pack_vector_w"), gvec, pk(mom, "pack_vector_m"), pk(var, "pack_vector_v"),
                       "adamw_vectors")
    for src, dst in ((d, delta), (m2, new_m), (v2, new_v)):
        vec, rel = _unpack_vectors(src, lens)
        dst.update(vec)
        dst["rel_bias"] = rel
    grad.update(g_vec)
    grad["rel_bias"] = g_rel

    outs = [loss, gx[None]]
    for group in (grad, delta, new_m, new_v):
        outs += [group[n] for n in WEIGHT_NAMES]
    return tuple(outs)
```

```python
import functools
import math

import jax
import jax.numpy as jnp
from jax import lax
from jax.experimental import pallas as pl
from jax.experimental.pallas import tpu as pltpu

F32 = jnp.float32
BF16 = jnp.bfloat16
I32 = jnp.int32

D = 1024
DEPTH = 2
N_MEM = 256
CW = 512
KSIZE = 31
PAD = KSIZE // 2
DILS = (1, 4, 16)
RADIUS = 64
HPG = 4
HD = 64
GW = HPG * HD
MH = 4
MHD = 128
MW = MH * MHD
FH = 2816
NIN = 6912
C1 = 2 * CW
R_ATT = C1
R_MEM = R_ATT + 9 * GW
R_GATE = R_MEM + MW
Z_GATE = 0
Z_CONV = 3 * D
Z_MEM = Z_CONV + C1
Z_ATT = Z_MEM + MW
NUM_BUCKETS = 32
MAX_DISTANCE = 1024
RMS_EPS = 1e-6
LN_EPS = 1e-5
NEG_INF = -1e30
ATT_SCALE = HD ** -0.5
MEM_SCALE = MHD ** -0.5

ADAM_LR = 0.001
ADAM_B1 = 0.9
ADAM_B2 = 0.999
ADAM_EPS = 1e-08
ADAM_WD = 0.01
ADAM_STEP = 10

VMEM_LIMIT_BYTES = 56 * 1024 * 1024
ATT_QB = 128
ATT_TB = 16 * ATT_QB

MESH = pl.DeviceIdType.MESH


def _params(sem=None):
    return pltpu.CompilerParams(dimension_semantics=sem, vmem_limit_bytes=VMEM_LIMIT_BYTES)


def _sigmoid(v):
    return 1.0 / (1.0 + jnp.exp(-v))


def _dot(a, b):
    return jnp.dot(a, b, preferred_element_type=F32)


def _dot_nt(a, b):
    return lax.dot_general(a, b, (((1,), (1,)), ((), ())), preferred_element_type=F32)


def _dot_tn(a, b):
    return lax.dot_general(a, b, (((0,), (0,)), ((), ())), preferred_element_type=F32)


def _rms_fwd_val(v, g):
    r = lax.rsqrt(jnp.mean(v * v, axis=-1, keepdims=True) + RMS_EPS)
    return v * r * g


def _rms_bwd_val(v, g, dy):
    r = lax.rsqrt(jnp.mean(v * v, axis=-1, keepdims=True) + RMS_EPS)
    vh = v * r
    dvh = dy * g
    dv = r * (dvh - vh * jnp.mean(dvh * vh, axis=-1, keepdims=True))
    return dv, dy * vh


def _row(i):
    return (i, 0)


def _fixed(*_):
    return (0, 0)


def _mm_nn(a, b, tm, out_dtype, name):
    M, K = a.shape
    N = b.shape[1]

    def body(a_ref, b_ref, o_ref):
        o_ref[...] = _dot(a_ref[...], b_ref[...]).astype(out_dtype)

    return pl.pallas_call(
        body, name=name, grid=(M // tm,),
        in_specs=[pl.BlockSpec((tm, K), _row), pl.BlockSpec((K, N), _fixed, pipeline_mode=pl.Buffered(1))],
        out_specs=pl.BlockSpec((tm, N), _row),
        out_shape=jax.ShapeDtypeStruct((M, N), out_dtype),
        compiler_params=_params(("parallel",)),
    )(a, b)


def _mm_nt_rms_bwd(a, b, x, g, dres, tm, name):
    M, N = a.shape

    def body(a_ref, b_ref, x_ref, g_ref, dres_ref, dx_ref, dg_ref):
        @pl.when(pl.program_id(0) == 0)
        def _():
            dg_ref[...] = jnp.zeros_like(dg_ref)

        dv, dgr = _rms_bwd_val(x_ref[...], g_ref[...], _dot_nt(a_ref[...], b_ref[...]))
        dx_ref[...] = dres_ref[...] + dv
        dg_ref[...] += jnp.sum(dgr, axis=0, keepdims=True)

    rows = pl.BlockSpec((tm, D), _row)
    return pl.pallas_call(
        body, name=name, grid=(M // tm,),
        in_specs=[pl.BlockSpec((tm, N), _row), pl.BlockSpec((D, N), _fixed, pipeline_mode=pl.Buffered(1)), rows,
                  pl.BlockSpec((1, D), _fixed), rows],
        out_specs=[rows, pl.BlockSpec((1, D), _fixed)],
        out_shape=[jax.ShapeDtypeStruct((M, D), F32), jax.ShapeDtypeStruct((1, D), F32)],
        compiler_params=_params(("arbitrary",)),
    )(a, b, x, g, dres)


def _mm_tn(a, b, ts, tn, name):
    S, K = a.shape
    N = b.shape[1]

    def body(a_ref, b_ref, o_ref):
        @pl.when(pl.program_id(1) == 0)
        def _():
            o_ref[...] = jnp.zeros_like(o_ref)

        o_ref[...] += _dot_tn(a_ref[...], b_ref[...])

    return pl.pallas_call(
        body, name=name, grid=(N // tn, S // ts),
        in_specs=[pl.BlockSpec((ts, K), lambda j, s: (s, 0)), pl.BlockSpec((ts, tn), lambda j, s: (s, j))],
        out_specs=pl.BlockSpec((K, tn), lambda j, s: (0, j)),
        out_shape=jax.ShapeDtypeStruct((K, N), F32),
        compiler_params=_params(("parallel", "arbitrary")),
    )(a, b)


def _rms_h(x, g, name):
    S = x.shape[0]
    T = 512

    def body(x_ref, g_ref, h_ref):
        h_ref[...] = _rms_fwd_val(x_ref[...], g_ref[...]).astype(BF16)

    return pl.pallas_call(
        body, name=name, grid=(S // T,),
        in_specs=[pl.BlockSpec((T, D), _row), pl.BlockSpec((1, D), _fixed)],
        out_specs=pl.BlockSpec((T, D), _row),
        out_shape=jax.ShapeDtypeStruct((S, D), BF16),
        compiler_params=_params(("parallel",)),
    )(x, g)


CONV_T = 256
CONV_HALO = 16
CONV_RC = 32


def _halo_specs(T, halo, S, width, col):
    per = T // halo
    last = S // halo - 1
    return [
        pl.BlockSpec((T, width), lambda i: (i, col)),
        pl.BlockSpec((halo, width), lambda i: (jnp.maximum(i * per - 1, 0), col)),
        pl.BlockSpec((halo, width), lambda i: (jnp.minimum((i + 1) * per, last), col)),
    ]


def _glu(zb):
    zb = zb.astype(F32)
    return zb[:, :CW] * _sigmoid(zb[:, CW:])


CONV_EXT = CONV_T + 2 * CONV_HALO
SUBLANES = 8


def _fill_shifted(sh_ref, ext_ref, cur, prev, nxt):
    T, halo = CONV_T, CONV_HALO
    i = pl.program_id(0)
    n = pl.num_programs(0)
    ext_ref[0:halo, :] = jnp.where(i > 0, prev, 0.0)
    ext_ref[halo:halo + T, :] = cur
    ext_ref[halo + T:CONV_EXT, :] = jnp.where(i < n - 1, nxt, 0.0)
    ext_ref[CONV_EXT:CONV_EXT + SUBLANES, :] = jnp.zeros((SUBLANES, CW), F32)
    for b in range(SUBLANES):
        sh_ref[b] = ext_ref[b:b + CONV_EXT, :]


def _window(sh_ref, start, rows):
    b = start % SUBLANES
    return sh_ref[b, start - b:start - b + rows, :]


def _shifted_scratch():
    return [pltpu.VMEM((CONV_EXT + SUBLANES, CW), F32), pltpu.VMEM((SUBLANES, CONV_EXT, CW), F32)]


def _conv_fwd(z, wdw, bdw, lng, lnb, name):
    S = z.shape[0]
    T, HL, RC = CONV_T, CONV_HALO, CONV_RC

    def body(cur_ref, prev_ref, next_ref, w_ref, b_ref, g_ref, bb_ref, yc_ref, act_ref, ext_ref, sh_ref):
        _fill_shifted(sh_ref, ext_ref, _glu(cur_ref[...]), _glu(prev_ref[...]), _glu(next_ref[...]))
        for c in range(T // RC):
            acc = jnp.zeros((RC, CW), F32)
            for k in range(KSIZE):
                acc = acc + w_ref[k:k + 1, :] * _window(sh_ref, c * RC + k + HL - PAD, RC)
            yc = acc + b_ref[...]
            yc_ref[c * RC:(c + 1) * RC, :] = yc
            mu = jnp.mean(yc, axis=-1, keepdims=True)
            xc = yc - mu
            ln = xc * lax.rsqrt(jnp.mean(xc * xc, axis=-1, keepdims=True) + LN_EPS) * g_ref[...] + bb_ref[...]
            act_ref[c * RC:(c + 1) * RC, :] = (ln * _sigmoid(ln)).astype(BF16)

    return pl.pallas_call(
        body, name=name, grid=(S // T,),
        in_specs=_halo_specs(T, HL, S, C1, Z_CONV // C1) + [pl.BlockSpec((32, CW), _fixed)]
        + [pl.BlockSpec((1, CW), _fixed)] * 3,
        out_specs=[pl.BlockSpec((T, CW), _row), pl.BlockSpec((T, CW), _row)],
        out_shape=[jax.ShapeDtypeStruct((S, CW), F32), jax.ShapeDtypeStruct((S, CW), BF16)],
        scratch_shapes=_shifted_scratch(),
        compiler_params=_params(("parallel",)),
    )(z, z, z, wdw, bdw, lng, lnb)


def _conv_bwd_ln(yc, dact, lng, lnb, name):
    S = yc.shape[0]
    T = 512

    def body(yc_ref, da_ref, g_ref, b_ref, dyc_ref, dg_ref, db_ref, dbias_ref):
        yc_v = yc_ref[...]
        mu = jnp.mean(yc_v, axis=-1, keepdims=True)
        xc = yc_v - mu
        r = lax.rsqrt(jnp.mean(xc * xc, axis=-1, keepdims=True) + LN_EPS)
        yn = xc * r
        ln = yn * g_ref[...] + b_ref[...]
        sg = _sigmoid(ln)
        dln = da_ref[...].astype(F32) * (sg * (1.0 + ln * (1.0 - sg)))
        dyn = dln * g_ref[...]
        dyc = r * (dyn - jnp.mean(dyn, axis=-1, keepdims=True) - yn * jnp.mean(dyn * yn, axis=-1, keepdims=True))
        dyc_ref[...] = dyc

        @pl.when(pl.program_id(0) == 0)
        def _():
            dg_ref[...] = jnp.zeros_like(dg_ref)
            db_ref[...] = jnp.zeros_like(db_ref)
            dbias_ref[...] = jnp.zeros_like(dbias_ref)

        dg_ref[...] += jnp.sum(dln * yn, axis=0, keepdims=True)
        db_ref[...] += jnp.sum(dln, axis=0, keepdims=True)
        dbias_ref[...] += jnp.sum(dyc, axis=0, keepdims=True)

    vec = pl.BlockSpec((1, CW), _fixed)
    return pl.pallas_call(
        body, name=name, grid=(S // T,),
        in_specs=[pl.BlockSpec((T, CW), _row), pl.BlockSpec((T, CW), _row), vec, vec],
        out_specs=[pl.BlockSpec((T, CW), _row), vec, vec, vec],
        out_shape=[jax.ShapeDtypeStruct((S, CW), F32)] + [jax.ShapeDtypeStruct((1, CW), F32)] * 3,
        compiler_params=_params(("arbitrary",)),
    )(yc, dact, lng, lnb)


def _conv_bwd_dw(z, dyc, wdw, dz, name):
    S = z.shape[0]
    T, HL, RC = CONV_T, CONV_HALO, CONV_RC

    def body(zc_ref, zp_ref, zn_ref, dc_ref, dp_ref, dn_ref, w_ref, dz_in, dz_ref, dw_ref, uext_ref, ush_ref,
             dext_ref, dsh_ref, dwacc_ref):
        _fill_shifted(ush_ref, uext_ref, _glu(zc_ref[...]), _glu(zp_ref[...]), _glu(zn_ref[...]))
        _fill_shifted(dsh_ref, dext_ref, dc_ref[...], dp_ref[...], dn_ref[...])

        @pl.when(pl.program_id(0) == 0)
        def _():
            dwacc_ref[...] = jnp.zeros_like(dwacc_ref)

        for c in range(T // RC):
            dcur = dc_ref[c * RC:(c + 1) * RC, :]
            du = jnp.zeros((RC, CW), F32)
            for k in range(KSIZE):
                du = du + w_ref[k:k + 1, :] * _window(dsh_ref, c * RC + HL + PAD - k, RC)
                prod = dcur * _window(ush_ref, c * RC + k + HL - PAD, RC)
                dwacc_ref[k] += jnp.sum(prod.reshape(RC // SUBLANES, SUBLANES, CW), axis=0)
            zc = zc_ref[c * RC:(c + 1) * RC, :].astype(F32)
            a, gt = zc[:, :CW], zc[:, CW:]
            sg = _sigmoid(gt)
            dz_ref[c * RC:(c + 1) * RC, 0:CW] = (du * sg).astype(BF16)
            dz_ref[c * RC:(c + 1) * RC, CW:C1] = (du * a * sg * (1.0 - sg)).astype(BF16)

        @pl.when(pl.program_id(0) == pl.num_programs(0) - 1)
        def _():
            dw_ref[...] = jnp.sum(dwacc_ref[...], axis=1)

    return pl.pallas_call(
        body, name=name, grid=(S // T,),
        in_specs=_halo_specs(T, HL, S, C1, Z_CONV // C1) + _halo_specs(T, HL, S, CW, 0)
        + [pl.BlockSpec((32, CW), _fixed), pl.BlockSpec(memory_space=pl.ANY)],
        out_specs=[pl.BlockSpec((T, C1), lambda i: (i, Z_CONV // C1)), pl.BlockSpec((32, CW), _fixed)],
        out_shape=[jax.ShapeDtypeStruct(dz.shape, BF16), jax.ShapeDtypeStruct((32, CW), F32)],
        input_output_aliases={7: 0},
        scratch_shapes=_shifted_scratch() + _shifted_scratch() + [pltpu.VMEM((32, SUBLANES, CW), F32)],
        compiler_params=_params(("arbitrary",)),
    )(z, z, z, dyc, dyc, dyc, wdw, dz)


def _t5_bucket(rel):
    nb = NUM_BUCKETS // 2
    max_exact = nb // 2
    ret = jnp.where(rel > 0, nb, 0)
    n = jnp.abs(rel)
    nf = jnp.maximum(n, 1).astype(F32)
    large = max_exact + (jnp.log(nf / max_exact) / math.log(MAX_DISTANCE / max_exact)
                         * (nb - max_exact)).astype(I32)
    large = jnp.minimum(large, nb - 1)
    return ret + jnp.where(n < max_exact, n, large)


def _offsets_qk(nq, nk, shift):
    return lax.broadcasted_iota(I32, (nq, nk), 1) + shift - lax.broadcasted_iota(I32, (nq, nk), 0)


def _bias_table(bk, rb_ref, col, off):
    acc = jnp.zeros(bk.shape, F32)
    for b in range(NUM_BUCKETS):
        acc = jnp.where(bk == b, rb_ref[b, col], acc)
    return jnp.where(jnp.abs(off) <= RADIUS, acc, NEG_INF)


def _to_halves(scr, row0, val):
    rows = val.shape[0]
    v = val.astype(F32)
    scr[0, row0:row0 + rows, :] = v[:, :128]
    scr[1, row0:row0 + rows, :] = v[:, 128:]


ATT_FWD_GROUP = 2
ATT_BWD_GROUP = 1


def _att_units(d, fn, group):
    nj = ATT_TB // (ATT_QB * d)
    if nj == 1:
        def trip(t, c):
            r0 = pl.multiple_of(t * 8, 8)
            for u in range(0, 8, group):
                fn([(r0 + u + v, 0) for v in range(group)])
            return c

        lax.fori_loop(0, d // 8, trip, 0)
        return
    for r in range(d):
        def step(t, c, r=r):
            fn([(r, t * group + u) for u in range(group)])
            return c

        lax.fori_loop(0, nj // group, step, 0)


def _unit_row(r, j, d):
    if isinstance(j, int):
        return j * ATT_QB * d + r
    return pl.multiple_of(j * (ATT_QB * d), ATT_QB) + r


def _att_fwd(z, rel_bias, g, name):
    S = z.shape[0]
    d = DILS[g]
    TB, QB = ATT_TB, ATT_QB
    H = RADIUS * d
    L = S // d
    cq = (Z_ATT + 3 * GW * g) // GW
    ck, cv = cq + 1, cq + 2
    bk = _t5_bucket(_offsets_qk(QB, 2 * QB, -RADIUS) * d)

    def body(rb_ref, bk_ref, q_ref, kc_ref, kp_ref, kn_ref, vc_ref, vp_ref, vn_ref, o_ref, l_ref,
             qs, ks, vs, os_, ls, bias):
        i = pl.program_id(0)

        @pl.when(i == 0)
        def _():
            off = _offsets_qk(QB, 2 * QB, -RADIUS)
            for h in range(HPG):
                bias[h] = _bias_table(bk_ref[...], rb_ref, g * HPG + h, off)

        _to_halves(qs, 0, q_ref[...].astype(F32) * ATT_SCALE)
        for scr, p_ref, c_ref, n_ref in ((ks, kp_ref, kc_ref, kn_ref), (vs, vp_ref, vc_ref, vn_ref)):
            _to_halves(scr, 0, p_ref[...])
            _to_halves(scr, H, c_ref[...])
            _to_halves(scr, H + TB, n_ref[...])

        lo = lax.broadcasted_iota(I32, (QB, 128), 1) < HD

        def units(rjs):
            work = []
            for r, j in rjs:
                row = _unit_row(r, j, d)
                km = lax.broadcasted_iota(I32, (1, 2 * QB), 1) + (i * (TB // d) + j * QB - RADIUS)
                edge = jnp.where(jnp.where(km >= 0, km, L) < L, 0.0, NEG_INF)
                for hf in (0, 1):
                    q2 = qs[hf, pl.ds(row, QB, stride=d), :]
                    k2 = ks[hf, pl.ds(row, 2 * QB, stride=d), :].astype(BF16)
                    v2 = vs[hf, pl.ds(row, 2 * QB, stride=d), :].astype(BF16)
                    qq = jnp.concatenate([jnp.where(lo, q2, 0.0), jnp.where(lo, 0.0, q2)], axis=0).astype(BF16)
                    work.append((row, hf, edge, k2, v2, qq))
            scores = [_dot_nt(qq, k2) for (_, _, _, k2, _, qq) in work]
            probs = []
            for (row, hf, edge, *_), ss in zip(work, scores):
                es, stats = [], []
                for hh in (0, 1):
                    s = ss[hh * QB:(hh + 1) * QB] + bias[2 * hf + hh] + edge
                    m = jnp.max(s, axis=-1, keepdims=True)
                    e = jnp.exp(s - m)
                    den = jnp.sum(e, axis=-1, keepdims=True)
                    es.append(e.astype(BF16))
                    stats.append((1.0 / den, m + jnp.log(den)))
                probs.append((jnp.concatenate(es, axis=0), stats))
            for (row, hf, _, _, v2, _), (ee, stats) in zip(work, probs):
                oo = _dot(ee, v2)
                os_[hf, pl.ds(row, QB, stride=d), :] = jnp.where(lo, oo[:QB] * stats[0][0], oo[QB:] * stats[1][0])
                ls[hf, pl.ds(row, QB, stride=d), :] = jnp.where(lo, stats[0][1], stats[1][1])

        _att_units(d, units, ATT_FWD_GROUP)
        for hf in (0, 1):
            o_ref[:, hf * 128:(hf + 1) * 128] = os_[hf].astype(BF16)
            l_ref[:, hf * 128:(hf + 1) * 128] = ls[hf]

    def halo3(col):
        c, p, n = _halo_specs(TB, H, S, GW, col)
        return [c, p, n]

    return pl.pallas_call(
        body, name=name, grid=(S // TB,),
        in_specs=[pl.BlockSpec(memory_space=pltpu.SMEM), pl.BlockSpec((QB, 2 * QB), _fixed),
                  pl.BlockSpec((TB, GW), lambda i: (i, cq))] + halo3(ck) + halo3(cv),
        out_specs=[pl.BlockSpec((TB, GW), _row), pl.BlockSpec((TB, GW), _row)],
        out_shape=[jax.ShapeDtypeStruct((S, GW), BF16), jax.ShapeDtypeStruct((S, GW), F32)],
        scratch_shapes=[pltpu.VMEM((2, TB, 128), F32), pltpu.VMEM((2, TB + 2 * H, 128), F32),
                        pltpu.VMEM((2, TB + 2 * H, 128), F32), pltpu.VMEM((2, TB, 128), F32),
                        pltpu.VMEM((2, TB, 128), F32), pltpu.VMEM((HPG, QB, 2 * QB), F32)],
        compiler_params=_params(("arbitrary",)),
    )(rel_bias, bk, z, z, z, z, z, z, z)


def _att_combine(os3, ls3, name):
    S = os3[0].shape[0]
    T = 1024

    def body(o1, o2, o3, l1, l2, l3, o_ref, l_ref):
        lv = [l1[...], l2[...], l3[...]]
        m = jnp.maximum(jnp.maximum(lv[0], lv[1]), lv[2])
        e = [jnp.exp(v - m) for v in lv]
        den = e[0] + e[1] + e[2]
        acc = jnp.zeros_like(m)
        for ev, o in zip(e, (o1, o2, o3)):
            acc = acc + (ev / den) * o[...].astype(F32)
        o_ref[...] = acc.astype(BF16)
        l_ref[...] = m + jnp.log(den)

    blk = pl.BlockSpec((T, GW), _row)
    return pl.pallas_call(
        body, name=name, grid=(S // T,), in_specs=[blk] * 6, out_specs=[blk, blk],
        out_shape=[jax.ShapeDtypeStruct((S, GW), BF16), jax.ShapeDtypeStruct((S, GW), F32)],
        compiler_params=_params(("parallel",)),
    )(*os3, *ls3)


def _att_prep(do, o, lse, name):
    S = do.shape[0]
    T = 1024

    def body(do_ref, o_ref, l_ref, out_ref):
        prod = do_ref[...].astype(F32) * o_ref[...].astype(F32)
        dd = [jnp.broadcast_to(jnp.sum(prod[:, h * HD:(h + 1) * HD], axis=-1, keepdims=True), (T, HD))
              for h in range(HPG)]
        lane = lax.broadcasted_iota(I32, (T, GW), 1)
        out_ref[...] = jnp.where(lane % HD < HD // 2, l_ref[...], jnp.concatenate(dd, axis=-1))

    blk = pl.BlockSpec((T, GW), _row)
    return pl.pallas_call(
        body, name=name, grid=(S // T,), in_specs=[blk] * 3, out_specs=blk,
        out_shape=jax.ShapeDtypeStruct((S, GW), F32), compiler_params=_params(("parallel",)),
    )(do, o, lse)


def _att_bwd(z, rel_bias, do, ld, dz, g, name):
    S = z.shape[0]
    d = DILS[g]
    TB, QB = ATT_TB, ATT_QB
    H = RADIUS * d
    L = S // d
    E = TB + 2 * H
    cq = (Z_ATT + 3 * GW * g) // GW
    ck, cv = cq + 1, cq + 2
    bk_a = _t5_bucket(_offsets_qk(QB, 2 * QB, -RADIUS) * d)
    bk_b = _t5_bucket(-_offsets_qk(QB, 2 * QB, -RADIUS) * d)

    def body(rb_ref, bka_ref, bkb_ref, *refs):
        ins, (dz_ref, db_ref) = refs[:15], refs[16:18]
        qs, ks, vs, dos, ls, dqs, dks, dvs, bias_a, bias_b, dbias = refs[18:]
        i = pl.program_id(0)
        n = pl.num_programs(0)

        @pl.when(i == 0)
        def _():
            off = _offsets_qk(QB, 2 * QB, -RADIUS)
            for h in range(HPG):
                bias_a[h] = _bias_table(bka_ref[...], rb_ref, g * HPG + h, off)
                bias_b[h] = _bias_table(bkb_ref[...], rb_ref, g * HPG + h, off)
            dbias[...] = jnp.zeros_like(dbias)

        for a, scr in enumerate((qs, ks, vs, dos, ls)):
            c_ref, p_ref, n_ref = ins[3 * a:3 * a + 3]
            pre = (lambda v: v.astype(F32) * ATT_SCALE) if a == 0 else (lambda v: v)
            _to_halves(scr, 0, pre(p_ref[...]))
            _to_halves(scr, H, pre(c_ref[...]))
            _to_halves(scr, H + TB, pre(n_ref[...]))

        lo = lax.broadcasted_iota(I32, (QB, 128), 1) < HD

        def split(v):
            return jnp.concatenate([jnp.where(lo, v, 0.0), jnp.where(lo, 0.0, v)], axis=0).astype(BF16)

        def halves(v):
            return v[:QB], v[QB:]

        def units(rjs):
            work = []
            for r, j in rjs:
                row = _unit_row(r, j, d)
                cur = row + H
                m0 = i * (TB // d) + j * QB - RADIUS
                km = lax.broadcasted_iota(I32, (1, 2 * QB), 1) + m0
                edge_a = jnp.where(jnp.where(km >= 0, km, L) < L, 0.0, NEG_INF)
                for hf in (0, 1):
                    ld = lambda scr, at, nrow: scr[hf, pl.ds(at, nrow, stride=d), :]
                    w = dict(row=row, hf=hf, edge=edge_a, l_c=ld(ls, cur, QB), l_t=ld(ls, row, 2 * QB).T)
                    for nm, scr in (("q", qs), ("k", ks), ("v", vs), ("do", dos)):
                        w[nm + "_c"] = split(ld(scr, cur, QB))
                        w[nm + "_e"] = ld(scr, row, 2 * QB).astype(BF16)
                    work.append(w)
            for w in work:
                w["s"] = halves(_dot_nt(w["q_c"], w["k_e"]))
                w["dp"] = halves(_dot_nt(w["do_c"], w["v_e"]))
                w["s2"] = halves(_dot_nt(w["k_c"], w["q_e"]))
                w["dp2"] = halves(_dot_nt(w["v_c"], w["do_e"]))
            for w in work:
                w["ds"], w["p2"], w["ds2"] = [], [], []
                for hh in (0, 1):
                    h, c0 = 2 * w["hf"] + hh, HD * hh
                    l_c, l_t = w["l_c"], w["l_t"]
                    p = jnp.exp(w["s"][hh] + bias_a[h] + w["edge"] - l_c[:, c0:c0 + 1])
                    ds = p * (w["dp"][hh] - l_c[:, c0 + HD // 2:c0 + HD // 2 + 1])
                    dbias[h] += ds
                    p2 = jnp.exp(w["s2"][hh] + bias_b[h] + w["edge"] - l_t[c0:c0 + 1, :])
                    ds2 = p2 * (w["dp2"][hh] - l_t[c0 + HD // 2:c0 + HD // 2 + 1, :])
                    w["ds"].append(ds.astype(BF16))
                    w["p2"].append(p2.astype(BF16))
                    w["ds2"].append(ds2.astype(BF16))
            for w in work:
                at = pl.ds(w["row"], QB, stride=d)
                both = lambda pair, rhs: halves(_dot(jnp.concatenate(pair, axis=0), rhs))
                dq = both(w["ds"], w["k_e"])
                dqs[w["hf"], at, :] = jnp.where(lo, dq[0], dq[1]) * ATT_SCALE
                dv = both(w["p2"], w["do_e"])
                dvs[w["hf"], at, :] = jnp.where(lo, dv[0], dv[1])
                dk = both(w["ds2"], w["q_e"])
                dks[w["hf"], at, :] = jnp.where(lo, dk[0], dk[1])

        _att_units(d, units, ATT_BWD_GROUP)
        for a, scr in enumerate((dqs, dks, dvs)):
            for hf in (0, 1):
                dz_ref[:, a * GW + hf * 128:a * GW + (hf + 1) * 128] = scr[hf].astype(BF16)

        @pl.when(i == n - 1)
        def _():
            rows = lax.broadcasted_iota(I32, (NUM_BUCKETS, 128), 0)
            lanes = lax.broadcasted_iota(I32, (NUM_BUCKETS, 128), 1)
            out = jnp.zeros((NUM_BUCKETS, 128), F32)
            bk = bka_ref[...]
            for h in range(HPG):
                acc = dbias[h]
                for b in range(NUM_BUCKETS):
                    tot = jnp.sum(jnp.sum(jnp.where(bk == b, acc, 0.0), axis=1, keepdims=True), axis=0, keepdims=True)
                    out = out + jnp.where((rows == b) & (lanes == h), tot, 0.0)
            db_ref[...] = out

    def halo3(col, width=GW):
        return _halo_specs(TB, H, S, width, col)

    one = pl.Buffered(1)

    def single(specs):
        return [pl.BlockSpec(s.block_shape, s.index_map, pipeline_mode=one) for s in specs]

    in_specs = ([pl.BlockSpec(memory_space=pltpu.SMEM), pl.BlockSpec((QB, 2 * QB), _fixed),
                 pl.BlockSpec((QB, 2 * QB), _fixed)]
                + single(halo3(cq) + halo3(ck) + halo3(cv) + halo3(0) + halo3(0))
                + [pl.BlockSpec(memory_space=pl.ANY)])
    return pl.pallas_call(
        body, name=name, grid=(S // TB,), in_specs=in_specs,
        out_specs=[pl.BlockSpec((TB, 3 * GW), lambda i: (i, cq // 3)), pl.BlockSpec((NUM_BUCKETS, 128), _fixed)],
        out_shape=[jax.ShapeDtypeStruct(dz.shape, BF16), jax.ShapeDtypeStruct((NUM_BUCKETS, 128), F32)],
        input_output_aliases={18: 0},
        scratch_shapes=[pltpu.VMEM((2, E, 128), F32)] * 5 + [pltpu.VMEM((2, TB, 128), F32)] * 3
        + [pltpu.VMEM((HPG, QB, 2 * QB), F32)] * 3,
        compiler_params=_params(("arbitrary",)),
    )(rel_bias, bk_a, bk_b, z, z, z, z, z, z, z, z, z, do, do, do, ld, ld, ld, dz)


def _memkv_fwd(mem, gm, wkv, name):
    def body(m_ref, g_ref, w_ref, hm_ref, kv_ref):
        hm = _rms_fwd_val(m_ref[...], g_ref[...]).astype(BF16)
        hm_ref[...] = hm
        kv_ref[...] = _dot(hm, w_ref[...]).astype(BF16)

    return pl.pallas_call(
        body, name=name,
        out_shape=[jax.ShapeDtypeStruct((N_MEM, D), BF16), jax.ShapeDtypeStruct((N_MEM, 2 * MW), BF16)],
        compiler_params=_params(),
    )(mem, gm, wkv)


def _memkv_bwd(mem, gm, hm, wkv, dkv, name):
    def body(m_ref, g_ref, hm_ref, w_ref, dkv_ref, dw_ref, dg_ref):
        dkv_b = dkv_ref[...].astype(BF16)
        dw_ref[...] = _dot_tn(hm_ref[...], dkv_b)
        dhm = _dot_nt(dkv_b, w_ref[...])
        _, dgr = _rms_bwd_val(m_ref[...], g_ref[...], dhm)
        dg_ref[...] = jnp.sum(dgr, axis=0, keepdims=True)

    return pl.pallas_call(
        body, name=name,
        out_shape=[jax.ShapeDtypeStruct((D, 2 * MW), F32), jax.ShapeDtypeStruct((1, D), F32)],
        compiler_params=_params(),
    )(mem, gm, hm, wkv, dkv)


MEM_T = 512


def _mem_q_spec():
    return pl.BlockSpec((MEM_T, MW), lambda i: (i, Z_MEM // MW))


def _memattn_fwd(z, kv, name):
    S = z.shape[0]
    T = MEM_T

    def body(q_ref, kv_ref, o_ref):
        for h in range(MH):
            kh = kv_ref[:, h * MHD:(h + 1) * MHD]
            vh = kv_ref[:, MW + h * MHD:MW + (h + 1) * MHD]
            s = _dot_nt(q_ref[:, h * MHD:(h + 1) * MHD], kh) * MEM_SCALE
            e = jnp.exp(s - jnp.max(s, axis=-1, keepdims=True))
            p = e / jnp.sum(e, axis=-1, keepdims=True)
            o_ref[:, h * MHD:(h + 1) * MHD] = _dot(p.astype(BF16), vh).astype(BF16)

    return pl.pallas_call(
        body, name=name, grid=(S // T,),
        in_specs=[_mem_q_spec(), pl.BlockSpec((N_MEM, 2 * MW), _fixed)],
        out_specs=pl.BlockSpec((T, MW), _row),
        out_shape=jax.ShapeDtypeStruct((S, MW), BF16),
        compiler_params=_params(("parallel",)),
    )(z, kv)


def _memattn_bwd(z, kv, dom, dz, name):
    S = z.shape[0]
    T = MEM_T

    def body(q_ref, kv_ref, do_ref, dz_in, dq_ref, dkv_ref):
        @pl.when(pl.program_id(0) == 0)
        def _():
            dkv_ref[...] = jnp.zeros_like(dkv_ref)

        for h in range(MH):
            kh = kv_ref[:, h * MHD:(h + 1) * MHD]
            vh = kv_ref[:, MW + h * MHD:MW + (h + 1) * MHD]
            qh = q_ref[:, h * MHD:(h + 1) * MHD]
            doh = do_ref[:, h * MHD:(h + 1) * MHD]
            s = _dot_nt(qh, kh) * MEM_SCALE
            e = jnp.exp(s - jnp.max(s, axis=-1, keepdims=True))
            p = e / jnp.sum(e, axis=-1, keepdims=True)
            dkv_ref[:, MW + h * MHD:MW + (h + 1) * MHD] += _dot_tn(p.astype(BF16), doh)
            dp = _dot_nt(doh, vh)
            ds = (p * (dp - jnp.sum(dp * p, axis=-1, keepdims=True))).astype(BF16)
            dq_ref[:, h * MHD:(h + 1) * MHD] = (_dot(ds, kh) * MEM_SCALE).astype(BF16)
            dkv_ref[:, h * MHD:(h + 1) * MHD] += _dot_tn(ds, qh) * MEM_SCALE

    return pl.pallas_call(
        body, name=name, grid=(S // T,),
        in_specs=[_mem_q_spec(), pl.BlockSpec((N_MEM, 2 * MW), _fixed), pl.BlockSpec((T, MW), _row),
                  pl.BlockSpec(memory_space=pl.ANY)],
        out_specs=[_mem_q_spec(), pl.BlockSpec((N_MEM, 2 * MW), _fixed)],
        out_shape=[jax.ShapeDtypeStruct(dz.shape, BF16), jax.ShapeDtypeStruct((N_MEM, 2 * MW), F32)],
        input_output_aliases={3: 0},
        compiler_params=_params(("arbitrary",)),
    )(z, kv, dom, dz)


MERGE_T = 512


def _gate_spec(T):
    return pl.BlockSpec((T, 3 * D), lambda i: (i, Z_GATE // (3 * D)))


def _branches(ca_ref, oa_ref, om_ref, wco_ref, wao_ref, wmo_ref, zg_ref, bg_ref):
    ys = [_dot(ca_ref[...], wco_ref[...]), _dot(oa_ref[...], wao_ref[...]), _dot(om_ref[...], wmo_ref[...])]
    gs = [_sigmoid(zg_ref[:, b * D:(b + 1) * D].astype(F32) + bg_ref[:, b * D:(b + 1) * D]) for b in range(3)]
    return ys, gs


def _merge_fwd(x, cact, oatt, om, z, wco, wao, wmo, wout, bgate, gpost, gnext, name):
    S = x.shape[0]
    T = MERGE_T

    def body(x_ref, ca_ref, oa_ref, om_ref, zg_ref, wco_ref, wao_ref, wmo_ref, wout_ref, bg_ref, gp_ref, gn_ref,
             x1_ref, mg_ref, t_ref, h_ref):
        ys, gs = _branches(ca_ref, oa_ref, om_ref, wco_ref, wao_ref, wmo_ref, zg_ref, bg_ref)
        mb = (gs[0] * ys[0] + gs[1] * ys[1] + gs[2] * ys[2]).astype(BF16)
        t = _dot(mb, wout_ref[...])
        mg_ref[...] = mb
        t_ref[...] = t
        x1 = x_ref[...] + _rms_fwd_val(t, gp_ref[...])
        x1_ref[...] = x1
        h_ref[...] = _rms_fwd_val(x1, gn_ref[...]).astype(BF16)

    full = lambda a: pl.BlockSpec(a.shape, _fixed)
    return pl.pallas_call(
        body, name=name, grid=(S // T,),
        in_specs=[pl.BlockSpec((T, D), _row), pl.BlockSpec((T, CW), _row), pl.BlockSpec((T, GW), _row),
                  pl.BlockSpec((T, MW), _row), _gate_spec(T)]
        + [full(wco), full(wao), full(wmo), full(wout), full(bgate), full(gpost), full(gnext)],
        out_specs=[pl.BlockSpec((T, D), _row)] * 4,
        out_shape=[jax.ShapeDtypeStruct((S, D), F32), jax.ShapeDtypeStruct((S, D), BF16),
                   jax.ShapeDtypeStruct((S, D), F32), jax.ShapeDtypeStruct((S, D), BF16)],
        compiler_params=_params(("parallel",)),
    )(x, cact, oatt, om, z, wco, wao, wmo, wout, bgate, gpost, gnext)


def _merge_bwd(dx1, t, mg, cact, oatt, om, z, wco, wao, wmo, wout, bgate, gpost, name):
    S = dx1.shape[0]
    T = MERGE_T

    def body(dx_ref, t_ref, mg_ref, ca_ref, oa_ref, om_ref, zg_ref, wco_ref, wao_ref, wmo_ref, wout_ref,
             bg_ref, gp_ref, dzg_ref, dca_ref, doa_ref, dom_ref, dwco_ref, dwao_ref, dwmo_ref, dwout_ref,
             dbg_ref, dgp_ref):
        accs = (dwco_ref, dwao_ref, dwmo_ref, dwout_ref, dbg_ref, dgp_ref)

        @pl.when(pl.program_id(0) == 0)
        def _():
            for a in accs:
                a[...] = jnp.zeros_like(a)

        dt, dgr = _rms_bwd_val(t_ref[...], gp_ref[...], dx_ref[...])
        dgp_ref[...] += jnp.sum(dgr, axis=0, keepdims=True)
        dtb = dt.astype(BF16)
        dwout_ref[...] += _dot_tn(mg_ref[...], dtb)
        dm = _dot_nt(dtb, wout_ref[...])
        ys, gs = _branches(ca_ref, oa_ref, om_ref, wco_ref, wao_ref, wmo_ref, zg_ref, bg_ref)
        for b, (act_ref, w_ref, dw_ref, da_ref) in enumerate(
                ((ca_ref, wco_ref, dwco_ref, dca_ref), (oa_ref, wao_ref, dwao_ref, doa_ref),
                 (om_ref, wmo_ref, dwmo_ref, dom_ref))):
            dzg = dm * ys[b] * gs[b] * (1.0 - gs[b])
            dzg_ref[:, b * D:(b + 1) * D] = dzg.astype(BF16)
            dbg_ref[:, b * D:(b + 1) * D] += jnp.sum(dzg, axis=0, keepdims=True)
            dy = (dm * gs[b]).astype(BF16)
            dw_ref[...] += _dot_tn(act_ref[...], dy)
            da_ref[...] = _dot_nt(dy, w_ref[...]).astype(BF16)

    full = lambda a: pl.BlockSpec(a.shape, _fixed)
    fullf = lambda a: jax.ShapeDtypeStruct(a.shape, F32)
    return pl.pallas_call(
        body, name=name, grid=(S // T,),
        in_specs=[pl.BlockSpec((T, D), _row), pl.BlockSpec((T, D), _row), pl.BlockSpec((T, D), _row),
                  pl.BlockSpec((T, CW), _row), pl.BlockSpec((T, GW), _row), pl.BlockSpec((T, MW), _row)]
        + [_gate_spec(T), full(wco), full(wao), full(wmo), full(wout), full(bgate), full(gpost)],
        out_specs=[_gate_spec(T), pl.BlockSpec((T, CW), _row), pl.BlockSpec((T, GW), _row),
                   pl.BlockSpec((T, MW), _row), full(wco), full(wao), full(wmo), full(wout), full(bgate), full(gpost)],
        out_shape=[jax.ShapeDtypeStruct((S, NIN), BF16), jax.ShapeDtypeStruct((S, CW), BF16),
                   jax.ShapeDtypeStruct((S, GW), BF16), jax.ShapeDtypeStruct((S, MW), BF16),
                   fullf(wco), fullf(wao), fullf(wmo), fullf(wout), fullf(bgate), fullf(gpost)],
        compiler_params=_params(("arbitrary",)),
    )(dx1, t, mg, cact, oatt, om, z, wco, wao, wmo, wout, bgate, gpost)


FFN_T = 256


def _ffn_fwd(x1, gu, wfo, gpost, gnext, name):
    S = x1.shape[0]
    T = FFN_T

    nxt = gnext is not None

    def body(x_ref, gu_ref, w_ref, gp_ref, *rest):
        x2_ref, f_ref = rest[nxt:nxt + 2]
        gv = gu_ref[:, :FH].astype(F32)
        uv = gu_ref[:, FH:].astype(F32)
        act = (gv * _sigmoid(gv) * uv).astype(BF16)
        f = _dot(act, w_ref[...])
        f_ref[...] = f
        x2 = x_ref[...] + _rms_fwd_val(f, gp_ref[...])
        x2_ref[...] = x2
        if nxt:
            rest[3][...] = _rms_fwd_val(x2, rest[0][...]).astype(BF16)

    return pl.pallas_call(
        body, name=name, grid=(S // T,),
        in_specs=[pl.BlockSpec((T, D), _row), pl.BlockSpec((T, 2 * FH), _row), pl.BlockSpec((FH, D), _fixed),
                  pl.BlockSpec((1, D), _fixed)] + [pl.BlockSpec((1, D), _fixed)] * nxt,
        out_specs=[pl.BlockSpec((T, D), _row)] * (2 + nxt),
        out_shape=[jax.ShapeDtypeStruct((S, D), F32)] * 2 + [jax.ShapeDtypeStruct((S, D), BF16)] * nxt,
        compiler_params=_params(("parallel",)),
    )(x1, gu, wfo, gpost, *([gnext] if nxt else []))


def _ffn_bwd(dx2, f, gu, wfo, gpost, name):
    S = dx2.shape[0]
    T = FFN_T

    def body(dx_ref, f_ref, gu_ref, w_ref, gp_ref, dgu_ref, df_ref, act_ref, dgp_ref):
        @pl.when(pl.program_id(0) == 0)
        def _():
            dgp_ref[...] = jnp.zeros_like(dgp_ref)

        df, dgr = _rms_bwd_val(f_ref[...], gp_ref[...], dx_ref[...])
        dgp_ref[...] += jnp.sum(dgr, axis=0, keepdims=True)
        dfb = df.astype(BF16)
        df_ref[...] = dfb
        dact = _dot_nt(dfb, w_ref[...])
        gv = gu_ref[:, :FH].astype(F32)
        uv = gu_ref[:, FH:].astype(F32)
        sg = _sigmoid(gv)
        silu = gv * sg
        act_ref[...] = (silu * uv).astype(BF16)
        dgu_ref[:, :FH] = (dact * uv * (sg * (1.0 + gv * (1.0 - sg)))).astype(BF16)
        dgu_ref[:, FH:] = (dact * silu).astype(BF16)

    return pl.pallas_call(
        body, name=name, grid=(S // T,),
        in_specs=[pl.BlockSpec((T, D), _row), pl.BlockSpec((T, D), _row), pl.BlockSpec((T, 2 * FH), _row),
                  pl.BlockSpec((FH, D), _fixed), pl.BlockSpec((1, D), _fixed)],
        out_specs=[pl.BlockSpec((T, 2 * FH), _row), pl.BlockSpec((T, D), _row), pl.BlockSpec((T, FH), _row),
                   pl.BlockSpec((1, D), _fixed)],
        out_shape=[jax.ShapeDtypeStruct((S, 2 * FH), BF16), jax.ShapeDtypeStruct((S, D), BF16),
                   jax.ShapeDtypeStruct((S, FH), BF16), jax.ShapeDtypeStruct((1, D), F32)],
        compiler_params=_params(("arbitrary",)),
    )(dx2, f, gu, wfo, gpost)


def _loss_head(y, target, name):
    S = y.shape[0]
    T = 512

    def body(y_ref, t_ref, dy_ref, l_ref):
        @pl.when(pl.program_id(0) == 0)
        def _():
            l_ref[...] = jnp.zeros_like(l_ref)

        e = y_ref[...] - t_ref[...]
        dy_ref[...] = e * (1.0 / D)
        l_ref[...] += (0.5 / D) * jnp.sum(jnp.sum(e * e, axis=1, keepdims=True), axis=0, keepdims=True)

    return pl.pallas_call(
        body, name=name, grid=(S // T,),
        in_specs=[pl.BlockSpec((T, D), _row)] * 2,
        out_specs=[pl.BlockSpec((T, D), _row), pl.BlockSpec((8, 128), _fixed)],
        out_shape=[jax.ShapeDtypeStruct((S, D), F32), jax.ShapeDtypeStruct((8, 128), F32)],
        compiler_params=_params(("arbitrary",)),
    )(y, target)


BIG = ("w_in", "w_conv_out", "w_att_out", "w_mem_kv", "w_mem_out", "w_out", "w_ffn_in", "w_ffn_out")
SMALL = ("rel_bias", "norm_mix_pre", "b_gate", "conv_dw_bias", "conv_ln_g", "conv_ln_b", "norm_mem",
         "norm_mix_post", "norm_ffn_pre", "norm_ffn_post")


def _layer_fwd(l, x, h, mem, w, rest, rel_bias, gnext):
    tag = f"_l{l}"
    z = _mm_nn(h, w["w_in"], 512, BF16, "mm_in" + tag)
    yc, cact = _conv_fwd(z, w["conv_dw"], w["conv_dw_bias"], w["conv_ln_g"], w["conv_ln_b"], "conv_fwd" + tag)
    og, lg = zip(*[_att_fwd(z, rel_bias, g, f"att_fwd_g{g}" + tag) for g in range(3)])
    oatt, lse = _att_combine(og, lg, "att_combine" + tag)
    w = {**w, **rest(oatt)}
    hm, kv = _memkv_fwd(mem, w["norm_mem"], w["w_mem_kv"], "memkv_fwd" + tag)
    om = _memattn_fwd(z, kv, "memattn_fwd" + tag)
    x1, mg, t, h2 = _merge_fwd(x, cact, oatt, om, z, w["w_conv_out"], w["w_att_out"], w["w_mem_out"], w["w_out"],
                               w["b_gate"], w["norm_mix_post"], w["norm_ffn_pre"], "merge_fwd" + tag)
    gu = _mm_nn(h2, w["w_ffn_in"], 512, BF16, "mm_ffn_in" + tag)
    x2, f, *hn = _ffn_fwd(x1, gu, w["w_ffn_out"], w["norm_ffn_post"], gnext, "ffn_fwd" + tag)
    saved = dict(x=x, h=h, z=z, yc=yc, cact=cact, oatt=oatt, lse=lse, hm=hm, kv=kv, om=om, x1=x1, mg=mg, t=t,
                 h2=h2, gu=gu, f=f)
    return x2, (hn[0] if hn else None), saved, w


def _layer_bwd(l, dx2, mem, w, rel_bias, s, on_ffn_grads=None, on_w_in_grad=None):
    tag = f"_l{l}"
    gr = {}
    dgu, df, act, gr["norm_ffn_post"] = _ffn_bwd(dx2, s["f"], s["gu"], w["w_ffn_out"], w["norm_ffn_post"], "ffn_bwd" + tag)
    gr["w_ffn_out"] = _mm_tn(act, df, 1024, 512, "dw_ffn_out" + tag)
    gr["w_ffn_in"] = _mm_tn(s["h2"], dgu, 2048, 1408, "dw_ffn_in" + tag)
    gpre = w["norm_ffn_pre"]
    if on_ffn_grads is not None:
        gpre = gpre + on_ffn_grads(gr)[0, 0]
    dx1, gr["norm_ffn_pre"] = _mm_nt_rms_bwd(dgu, w["w_ffn_in"], s["x1"], gpre, dx2, 512, "dh_ffn" + tag)
    (dz, dcact, doatt, dom, gr["w_conv_out"], gr["w_att_out"], gr["w_mem_out"], gr["w_out"], gr["b_gate"],
     gr["norm_mix_post"]) = _merge_bwd(dx1, s["t"], s["mg"], s["cact"], s["oatt"], s["om"], s["z"], w["w_conv_out"],
                                       w["w_att_out"], w["w_mem_out"], w["w_out"], w["b_gate"], w["norm_mix_post"],
                                       "merge_bwd" + tag)
    dyc, gr["conv_ln_g"], gr["conv_ln_b"], gr["conv_dw_bias"] = _conv_bwd_ln(
        s["yc"], dcact, w["conv_ln_g"], w["conv_ln_b"], "conv_bwd_ln" + tag)
    dz, dwdw = _conv_bwd_dw(s["z"], dyc, w["conv_dw"], dz, "conv_bwd_dw" + tag)
    gr["conv_dw"] = dwdw[:KSIZE]
    ld = _att_prep(doatt, s["oatt"], s["lse"], "att_prep" + tag)
    drb = []
    for g in range(3):
        dz, db = _att_bwd(s["z"], rel_bias, doatt, ld, dz, g, f"att_bwd_g{g}" + tag)
        drb.append(db)
    dz, dkv = _memattn_bwd(s["z"], s["kv"], dom, dz, "memattn_bwd" + tag)
    gr["w_mem_kv"], gr["norm_mem"] = _memkv_bwd(mem, w["norm_mem"], s["hm"], w["w_mem_kv"], dkv, "memkv_bwd" + tag)
    gr["w_in"] = _mm_tn(s["h"], dz, 2048, 1152, "dw_in" + tag)
    gmix = w["norm_mix_pre"]
    if on_w_in_grad is not None:
        gmix = gmix + on_w_in_grad(gr)[0, 0]
    dx, gr["norm_mix_pre"] = _mm_nt_rms_bwd(dz, w["w_in"], s["x"], gmix, dx1, 512, "dh_in" + tag)
    return dx, gr, drb


def _rel_bias_total(parts, name):
    def body(*refs):
        out_ref = refs[-1]
        acc = jnp.zeros((NUM_BUCKETS, 128), F32)
        for l in range(DEPTH):
            for g in range(3):
                v = refs[l * 3 + g][...]
                acc = acc + (v if g == 0 else pltpu.roll(v, HPG * g, axis=1))
        out_ref[...] = acc

    return pl.pallas_call(body, name=name, out_shape=jax.ShapeDtypeStruct((NUM_BUCKETS, 128), F32),
                          compiler_params=_params())(*[p for layer in parts for p in layer])


def _local_step(x, mem, target, rel_bias, layer_fns, gmix, on_grads=None, on_last_ffn_grads=None,
                on_last_w_in_grad=None):
    saved, layers = [], []
    h = _rms_h(x, gmix[0], "rms_mix_l0")
    for l in range(DEPTH):
        first, rest = layer_fns[l](x)
        x, h, s, w = _layer_fwd(l, x, h, mem, first, rest, rel_bias, gmix[l + 1] if l + 1 < DEPTH else None)
        saved.append(s)
        layers.append(w)
    dy, lpart = _loss_head(x, target, "loss_head")
    grads = [None] * DEPTH
    drb = [None] * DEPTH
    for l in reversed(range(DEPTH)):
        dy, grads[l], drb[l] = _layer_bwd(l, dy, mem, layers[l], rel_bias, saved[l],
                                          on_last_ffn_grads if l == 0 else None,
                                          on_last_w_in_grad if l == 0 else None)
        if on_grads is not None and l > 0:
            below = dict(layers[l - 1])
            below["norm_ffn_post"] = below["norm_ffn_post"] + on_grads(l, grads[l])[0, 0]
            layers[l - 1] = below
    return lpart[0, 0], dy, grads, _rel_bias_total(drb, "rel_bias_total")


def _z_cols_from_ref(w):
    att = [w[..., R_ATT + (3 * j + g) * GW:R_ATT + (3 * j + g + 1) * GW] for g in range(3) for j in range(3)]
    return jnp.concatenate([w[..., R_GATE:], w[..., :C1], w[..., R_MEM:R_GATE]] + att, axis=-1)


def _ref_cols_from_z(w):
    att = [w[..., Z_ATT + (3 * g + j) * GW:Z_ATT + (3 * g + j + 1) * GW] for j in range(3) for g in range(3)]
    return jnp.concatenate([w[..., Z_CONV:Z_MEM]] + att + [w[..., Z_MEM:Z_ATT], w[..., Z_GATE:Z_CONV]], axis=-1)


N_CHIPS = 4
SHARD = {"w_in": ((D, NIN // 4), 1), "w_conv_out": ((CW, D // 4), 1), "w_att_out": ((GW, D // 4), 1),
         "w_mem_kv": ((D // 4, 2 * MW), 0), "w_mem_out": ((MW, D // 4), 1), "w_out": ((D // 4, D), 0),
         "w_ffn_in": ((D, 2 * FH // 4), 1), "w_ffn_out": ((FH // 4, D), 0)}
CDW_ROWS = 64
VEC_ROWS = (("norm_mix_pre", 1), ("b_gate", 3), ("conv_dw_bias", 1), ("conv_ln_g", 1), ("conv_ln_b", 1),
            ("norm_mem", 1), ("norm_mix_post", 1), ("norm_ffn_pre", 1), ("norm_ffn_post", 1))
VEC_LROWS = sum(r for _, r in VEC_ROWS)
REL_ROW = DEPTH * VEC_LROWS
CDW_ROW = REL_ROW + 1
CDW_GROWS = DEPTH * KSIZE * CW // D
LOSS_ROW = CDW_ROW + CDW_GROWS
SMALL_ROWS = -(-(LOSS_ROW + 1) // 8) * 8


def _mesh_pos():
    return lax.axis_index("x"), lax.axis_index("y"), lax.axis_index("c")


def _other_chips(x, y):
    chips = [(1 - x, y), (x, 1 - y), (1 - x, 1 - y)]
    return chips, [2 * cx + cy for cx, cy in chips]


ANY_SPEC = pl.BlockSpec(memory_space=pl.ANY)


def _remote(src, dst, send_sems, recv_sems, k, to):
    return pltpu.make_async_remote_copy(src_ref=src, dst_ref=dst, send_sem=send_sems.at[k], recv_sem=recv_sems.at[k],
                                        device_id=to, device_id_type=MESH)


def _half(ref, c):
    h = ref.shape[0] // 2
    return ref.at[pl.ds(c * h if isinstance(c, int) else pl.multiple_of(c * h, 16), h)]


def _all_gather(ws, cdw):
    nw = len(ws)

    def body(*refs):
        w_refs, cdw_ref = refs[:nw], refs[nw]
        g_refs, gc_ref = refs[nw + 1:2 * nw + 1], refs[2 * nw + 1]
        send_sems, recv_sems = refs[2 * nw + 2:]
        x, y, c = _mesh_pos()
        j = 2 * x + y
        sibling = (x, y, 1 - c)
        chips, blocks = _other_chips(x, y)
        copy = functools.partial(_remote, send_sems=send_sems, recv_sems=recv_sems)
        pairs = list(zip(w_refs, g_refs))
        first = [copy(_half(w, c), _half(g.at[j], c), k=k * nw + n, to=(*chip, c))
                 for k, chip in enumerate(chips) for n, (w, g) in enumerate(pairs)]
        first += [copy(cdw_ref, gc_ref.at[j], k=6 * nw + k, to=(*chip, c)) for k, chip in enumerate(chips)]
        for cp in first:
            cp.start()
        passed = []
        for k, b in enumerate(blocks):
            for n, (w, g) in enumerate(pairs):
                copy(_half(w, c), _half(g.at[b], c), k=k * nw + n, to=sibling).wait_recv()
            onward = [copy(_half(g.at[b], c), _half(g.at[b], c), k=(3 + k) * nw + n, to=sibling)
                      for n, (w, g) in enumerate(pairs)]
            for cp in onward:
                cp.start()
            passed += onward
        for k, b in enumerate(blocks):
            for n, (w, g) in enumerate(pairs):
                copy(_half(w, c), _half(g.at[b], 1 - c), k=(3 + k) * nw + n, to=sibling).wait_recv()
            copy(cdw_ref, gc_ref.at[b], k=6 * nw + k, to=sibling).wait_recv()
        for cp in first + passed:
            cp.wait_send()

    nsem = 6 * nw + 3
    return pl.pallas_call(
        body, name="all_gather_weights",
        out_shape=[jax.ShapeDtypeStruct((N_CHIPS,) + w.shape, BF16) for w in ws]
        + [jax.ShapeDtypeStruct((N_CHIPS, CDW_ROWS, 128), F32)],
        in_specs=[ANY_SPEC] * (nw + 1), out_specs=[ANY_SPEC] * (nw + 1),
        scratch_shapes=[pltpu.SemaphoreType.DMA((nsem,)), pltpu.SemaphoreType.DMA((nsem,))],
    )(*ws, cdw)


SEM_SPEC = pl.BlockSpec(memory_space=pltpu.SEMAPHORE)
DATAFLOW = pltpu.SideEffectType.DATAFLOW_SIDE_EFFECTING


def _gather_copies(w_refs, g_refs, send_sem, recv_sem):
    x, y, c = _mesh_pos()
    j = 2 * x + y
    chips, _ = _other_chips(x, y)
    return [pltpu.make_async_remote_copy(src_ref=_half(w, c), dst_ref=_half(g.at[j], c), send_sem=send_sem,
                                         recv_sem=recv_sem, device_id=(*chip, cc), device_id_type=MESH)
            for chip in chips for cc in (0, 1) for w, g in zip(w_refs, g_refs)]


def _all_gather_start(ws, after, name):
    nw = len(ws)

    def body(*refs):
        w_refs, g_refs = refs[:nw], refs[nw:2 * nw]
        send_sem, recv_sem = refs[2 * nw + 1:2 * nw + 3]
        token = refs[-1]
        for cp in _gather_copies(w_refs, g_refs, send_sem, recv_sem):
            cp.start()
        token[...] = jnp.zeros_like(token)

    lands = [pltpu.with_memory_space_constraint(lax.empty((N_CHIPS,) + w.shape, BF16), pltpu.HBM) for w in ws]
    ws = [pltpu.with_memory_space_constraint(w, pltpu.HBM) for w in ws]
    hbm = pl.BlockSpec(memory_space=pltpu.HBM)
    out = pl.pallas_call(
        body, name=name,
        out_shape=[pltpu.SemaphoreType.DMA(()), pltpu.SemaphoreType.DMA(())]
        + [pltpu.HBM(w.shape, BF16) for w in ws] + [pltpu.HBM(g.shape, BF16) for g in lands]
        + [jax.ShapeDtypeStruct((8, 128), F32)],
        in_specs=[hbm] * (2 * nw) + [ANY_SPEC],
        out_specs=[SEM_SPEC, SEM_SPEC] + [hbm] * (2 * nw) + [pl.BlockSpec(memory_space=pltpu.VMEM)],
        input_output_aliases={n: 2 + n for n in range(2 * nw)},
        compiler_params=pltpu.CompilerParams(has_side_effects=DATAFLOW),
    )(*ws, *lands, after)
    return out[0], out[1], out[2:2 + nw], out[2 + nw:2 + 2 * nw], out[-1]


def _all_gather_wait(send_sem, recv_sem, ws, lands, after, name):
    nw = len(ws)

    def body(*refs):
        w_refs, g_refs = refs[:nw], refs[nw:2 * nw]
        send_sem, recv_sem = refs[2 * nw:2 * nw + 2]
        x, y, c = _mesh_pos()
        _, blocks = _other_chips(x, y)
        for cp in _gather_copies(w_refs, g_refs, send_sem, recv_sem):
            cp.wait_send()
        for b in blocks:
            for cc in (0, 1):
                for w, g in zip(w_refs, g_refs):
                    pltpu.make_async_remote_copy(src_ref=_half(w, cc), dst_ref=_half(g.at[b], cc), send_sem=send_sem,
                                                 recv_sem=recv_sem, device_id=(x, y, c),
                                                 device_id_type=MESH).wait_recv()

    hbm = pl.BlockSpec(memory_space=pltpu.HBM)
    out = pl.pallas_call(
        body, name=name,
        out_shape=[pltpu.HBM(w.shape, BF16) for w in ws] + [pltpu.HBM(g.shape, BF16) for g in lands],
        in_specs=[hbm] * (2 * nw) + [SEM_SPEC, SEM_SPEC, ANY_SPEC],
        out_specs=[hbm] * (2 * nw),
        input_output_aliases={n: n for n in range(2 * nw)},
        compiler_params=pltpu.CompilerParams(has_side_effects=DATAFLOW),
    )(*ws, *lands, send_sem, recv_sem, after)
    return out[:nw], out[nw:]


def _half_rows(ref, c):
    h = ref.shape[1] // 2
    return ref.at[:, pl.ds(pl.multiple_of(c * h, 16), h)]


def _sibling_exchange(ps):
    nw = len(ps)

    def body(*refs):
        p_refs, r_refs, (send_sems, recv_sems) = refs[:nw], refs[nw:2 * nw], refs[2 * nw:]
        x, y, c = _mesh_pos()
        cps = [_remote(_half_rows(p, 1 - c), r, send_sems, recv_sems, n, (x, y, 1 - c))
               for n, (p, r) in enumerate(zip(p_refs, r_refs))]
        for cp in cps:
            cp.start()
        for cp in cps:
            cp.wait()

    return pl.pallas_call(
        body, name="grad_sibling_exchange",
        out_shape=[jax.ShapeDtypeStruct((N_CHIPS, p.shape[1] // 2, p.shape[2]), p.dtype) for p in ps],
        in_specs=[ANY_SPEC] * nw, out_specs=[ANY_SPEC] * nw,
        scratch_shapes=[pltpu.SemaphoreType.DMA((nw,)), pltpu.SemaphoreType.DMA((nw,))],
    )(*ps)


SUM_BLOCK_BYTES = 2 * 1024 * 1024


def _sum_rows(s0, s1):
    return s0 if s0 * s1 * 2 <= SUM_BLOCK_BYTES else s0 // 2


def _add_own_half(where, p, r, name):
    _, h, s1 = r.shape
    T = _sum_rows(h, s1)
    nt = h // T

    def body(where_ref, p_ref, r_ref, o_ref):
        o_ref[...] = (p_ref[...].astype(F32) + r_ref[...].astype(F32)).astype(BF16)

    return pl.pallas_call(
        body, name=name,
        grid_spec=pltpu.PrefetchScalarGridSpec(
            num_scalar_prefetch=1, grid=(N_CHIPS, nt),
            in_specs=[pl.BlockSpec((1, T, s1), lambda j, i, wh: (j, wh[0] * nt + i, 0)),
                      pl.BlockSpec((1, T, s1), lambda j, i, wh: (j, i, 0))],
            out_specs=pl.BlockSpec((1, T, s1), lambda j, i, wh: (j, i, 0))),
        out_shape=jax.ShapeDtypeStruct(r.shape, BF16), compiler_params=_params(("parallel", "parallel")),
    )(where, p, r)


def _chip_exchange(as_):
    nw = len(as_)

    def body(*refs):
        a_refs, r_refs, (send_sems, recv_sems) = refs[:nw], refs[nw:2 * nw], refs[2 * nw:]
        x, y, c = _mesh_pos()
        chips, blocks = _other_chips(x, y)
        cps = [_remote(a.at[b], r.at[k], send_sems, recv_sems, k * nw + n, (*chip, c))
               for k, (chip, b) in enumerate(zip(chips, blocks)) for n, (a, r) in enumerate(zip(a_refs, r_refs))]
        for cp in cps:
            cp.start()
        for cp in cps:
            cp.wait_recv()
        for cp in cps:
            cp.wait_send()

    return pl.pallas_call(
        body, name="grad_chip_exchange", out_shape=[jax.ShapeDtypeStruct((3,) + a.shape[1:], a.dtype) for a in as_],
        in_specs=[ANY_SPEC] * nw, out_specs=[ANY_SPEC] * nw,
        scratch_shapes=[pltpu.SemaphoreType.DMA((3 * nw,)), pltpu.SemaphoreType.DMA((3 * nw,))],
    )(*as_)


def _sum_chips(where, a, r, o, name):
    _, h, s1 = a.shape
    T = _sum_rows(h, s1)
    nt = h // T

    def body(where_ref, a_ref, r_ref, o_in, o_ref):
        acc = a_ref[0].astype(F32)
        for k in range(3):
            acc = acc + r_ref[k].astype(F32)
        o_ref[0] = acc

    return pl.pallas_call(
        body, name=name,
        grid_spec=pltpu.PrefetchScalarGridSpec(
            num_scalar_prefetch=1, grid=(nt,),
            in_specs=[pl.BlockSpec((1, T, s1), lambda i, wh: (wh[1], i, 0)),
                      pl.BlockSpec((3, T, s1), lambda i, wh: (0, i, 0)), ANY_SPEC],
            out_specs=pl.BlockSpec((1, T, s1), lambda i, wh: (0, wh[0] * nt + i, 0))),
        out_shape=jax.ShapeDtypeStruct(o.shape, F32), input_output_aliases={3: 0},
        compiler_params=_params(("parallel",)),
    )(where, a, r, o)


def _sibling_share(os_):
    nw = len(os_)

    def body(*refs):
        o_refs, (send_sems, recv_sems) = refs[nw:2 * nw], refs[2 * nw:]
        x, y, c = _mesh_pos()
        mine = lambda o, cc: _half(o.at[0], cc)
        cps = [_remote(mine(o, c), mine(o, c), send_sems, recv_sems, n, (x, y, 1 - c)) for n, o in enumerate(o_refs)]
        for cp in cps:
            cp.start()
        for n, o in enumerate(o_refs):
            _remote(mine(o, c), mine(o, 1 - c), send_sems, recv_sems, n, (x, y, 1 - c)).wait_recv()
        for cp in cps:
            cp.wait_send()

    return pl.pallas_call(
        body, name="grad_sibling_share", out_shape=[jax.ShapeDtypeStruct(o.shape, o.dtype) for o in os_],
        in_specs=[ANY_SPEC] * nw, out_specs=[ANY_SPEC] * nw,
        input_output_aliases={n: n for n in range(nw)},
        scratch_shapes=[pltpu.SemaphoreType.DMA((nw,)), pltpu.SemaphoreType.DMA((nw,))],
    )(*os_)


N_DEV = 8


def _scatter_copies(p_refs, r_refs, send_sem, recv_sem):
    x, y, c = _mesh_pos()
    cps = []
    for m in range(1, N_DEV):
        px, py, pc = x ^ (m >> 2 & 1), y ^ (m >> 1 & 1), c ^ (m & 1)
        for p, r in zip(p_refs, r_refs):
            cps.append(pltpu.make_async_remote_copy(src_ref=p.at[2 * px + py], dst_ref=r.at[m - 1], send_sem=send_sem,
                                                    recv_sem=recv_sem, device_id=(px, py, pc), device_id_type=MESH))
    return cps


def _reduce_start(ps, after, name):
    nw = len(ps)

    def body(*refs):
        p_refs, r_refs = refs[:nw], refs[nw:2 * nw]
        send_sem, recv_sem = refs[2 * nw + 1:2 * nw + 3]
        for cp in _scatter_copies(p_refs, r_refs, send_sem, recv_sem):
            cp.start()
        refs[-1][...] = jnp.zeros_like(refs[-1])

    lands = [pltpu.with_memory_space_constraint(lax.empty((N_DEV - 1,) + p.shape[1:], BF16), pltpu.HBM) for p in ps]
    ps = [pltpu.with_memory_space_constraint(p, pltpu.HBM) for p in ps]
    hbm = pl.BlockSpec(memory_space=pltpu.HBM)
    out = pl.pallas_call(
        body, name=name,
        out_shape=[pltpu.SemaphoreType.DMA(()), pltpu.SemaphoreType.DMA(())]
        + [pltpu.HBM(p.shape, BF16) for p in ps] + [pltpu.HBM(r.shape, BF16) for r in lands]
        + [jax.ShapeDtypeStruct((8, 128), F32)],
        in_specs=[hbm] * (2 * nw) + [ANY_SPEC],
        out_specs=[SEM_SPEC, SEM_SPEC] + [hbm] * (2 * nw) + [pl.BlockSpec(memory_space=pltpu.VMEM)],
        input_output_aliases={n: 2 + n for n in range(2 * nw)},
        compiler_params=pltpu.CompilerParams(has_side_effects=DATAFLOW),
    )(*ps, *lands, after)
    return out[0], out[1], out[2:2 + nw], out[2 + nw:2 + 2 * nw], out[-1]


def _reduce_wait(send_sem, recv_sem, ps, lands, after, name):
    nw = len(ps)

    def body(*refs):
        p_refs, r_refs = refs[:nw], refs[nw:2 * nw]
        send_sem, recv_sem = refs[2 * nw:2 * nw + 2]
        x, y, c = _mesh_pos()
        for cp in _scatter_copies(p_refs, r_refs, send_sem, recv_sem):
            cp.wait_send()
        for m in range(1, N_DEV):
            for p, r in zip(p_refs, r_refs):
                pltpu.make_async_remote_copy(src_ref=p.at[0], dst_ref=r.at[m - 1], send_sem=send_sem,
                                             recv_sem=recv_sem, device_id=(x, y, c), device_id_type=MESH).wait_recv()

    hbm = pl.BlockSpec(memory_space=pltpu.HBM)
    out = pl.pallas_call(
        body, name=name,
        out_shape=[pltpu.HBM(p.shape, BF16) for p in ps] + [pltpu.HBM(r.shape, BF16) for r in lands],
        in_specs=[hbm] * (2 * nw) + [SEM_SPEC, SEM_SPEC, ANY_SPEC],
        out_specs=[hbm] * (2 * nw),
        input_output_aliases={n: n for n in range(2 * nw)},
        compiler_params=pltpu.CompilerParams(has_side_effects=DATAFLOW),
    )(*ps, *lands, send_sem, recv_sem, after)
    return out[:nw], out[nw:]


SUM8_BLOCK_BYTES = 6 * 1024 * 1024


def _sum_devices(where, p, r, layer, o, name):
    _, s0, s1 = p.shape
    T = s0
    while (N_DEV - 1) * T * s1 * 2 > SUM8_BLOCK_BYTES:
        T //= 2

    def body(where_ref, p_ref, r_ref, *rest):
        me = 2 * where_ref[1] + where_ref[0]
        acc = None
        for dev in range(N_DEV):
            m = dev ^ me
            val = jnp.where(m == 0, p_ref[0], r_ref[jnp.maximum(m, 1) - 1]).astype(F32)
            acc = val if acc is None else acc + val
        rest[-1][0] = acc

    given = o is not None
    return pl.pallas_call(
        body, name=name,
        grid_spec=pltpu.PrefetchScalarGridSpec(
            num_scalar_prefetch=1, grid=(s0 // T,),
            in_specs=[pl.BlockSpec((1, T, s1), lambda i, wh: (wh[1], i, 0)),
                      pl.BlockSpec((N_DEV - 1, T, s1), lambda i, wh: (0, i, 0))] + [ANY_SPEC] * given,
            out_specs=pl.BlockSpec((1, T, s1), lambda i, wh: (layer, i, 0))),
        out_shape=jax.ShapeDtypeStruct((DEPTH, s0, s1), F32), input_output_aliases={3: 0} if given else {},
        compiler_params=_params(("parallel",)),
    )(where, p, r, *([o] if given else []))


def _all_reduce_small(sp):
    def body(sp_ref, out_ref, buf, send_sems, recv_sems):
        x, y, c = _mesh_pos()
        me = 4 * x + 2 * y + c
        buf[0] = sp_ref[...]
        cps = []
        for k in range(1, 8):
            peer = (x ^ (k >> 2 & 1), y ^ (k >> 1 & 1), c ^ (k & 1))
            cps.append(pltpu.make_async_remote_copy(src_ref=sp_ref, dst_ref=buf.at[k], send_sem=send_sems.at[k - 1],
                                                    recv_sem=recv_sems.at[k - 1], device_id=peer, device_id_type=MESH))
        for cp in cps:
            cp.start()
        for cp in cps:
            cp.wait_recv()
        for cp in cps:
            cp.wait_send()
        acc = buf[me]
        for p in range(1, 8):
            acc = acc + buf[p ^ me]
        out_ref[...] = acc

    vm = pl.BlockSpec(memory_space=pltpu.VMEM)
    return pl.pallas_call(
        body, name="all_reduce_small", out_shape=jax.ShapeDtypeStruct(sp.shape, F32),
        in_specs=[vm], out_specs=vm,
        scratch_shapes=[pltpu.VMEM((8,) + sp.shape, F32), pltpu.SemaphoreType.DMA((7,)), pltpu.SemaphoreType.DMA((7,))],
        compiler_params=_params(),
    )(sp)


def _adamw(w, g, m, v, name):
    R, C = w.shape
    T = next((t for t in (256, 128) if R % t == 0), R)

    def body(w_ref, g_ref, m_ref, v_ref, d_ref, m2_ref, v2_ref):
        gv = g_ref[...]
        m2 = ADAM_B1 * m_ref[...] + (1.0 - ADAM_B1) * gv
        v2 = ADAM_B2 * v_ref[...] + (1.0 - ADAM_B2) * (gv * gv)
        m_hat = m2 / (1.0 - ADAM_B1 ** ADAM_STEP)
        v_hat = v2 / (1.0 - ADAM_B2 ** ADAM_STEP)
        d_ref[...] = -ADAM_LR * (m_hat / (jnp.sqrt(v_hat) + ADAM_EPS) + ADAM_WD * w_ref[...])
        m2_ref[...] = m2
        v2_ref[...] = v2

    blk = pl.BlockSpec((T, C), _row)
    return pl.pallas_call(
        body, name=name, grid=(R // T,), in_specs=[blk] * 4, out_specs=[blk] * 3,
        out_shape=[jax.ShapeDtypeStruct((R, C), F32)] * 3, compiler_params=_params(("parallel",)),
    )(w, g, m, v)


def _pack_vectors(get, rel, cdw, name, loss=None):
    rows = []
    for l in range(DEPTH):
        for n, r in VEC_ROWS:
            v = get(n)[l]
            rows.append(jnp.pad(v, (0, r * D - v.shape[0])).reshape(r, D))
    rows.append(jnp.pad(rel.reshape(-1), (0, D - NUM_BUCKETS * 3 * HPG)).reshape(1, D))
    rows.append(cdw.reshape(CDW_GROWS, D))
    if loss is not None:
        rows.append(jnp.full((1, D), loss, F32))

    def body(*refs):
        out_ref = refs[-1]
        out_ref[...] = jnp.zeros_like(out_ref)
        at = 0
        for ref in refs[:-1]:
            out_ref[at:at + ref.shape[0], :] = ref[...]
            at += ref.shape[0]

    return pl.pallas_call(body, name=name, out_shape=jax.ShapeDtypeStruct((SMALL_ROWS, D), F32),
                          compiler_params=_params())(*rows)


def _unpack_vectors(packed, lens):
    out = {n: [] for n, _ in VEC_ROWS}
    for l in range(DEPTH):
        at = l * VEC_LROWS
        for n, r in VEC_ROWS:
            out[n].append(packed[at:at + r].reshape(-1)[:lens[n]])
            at += r
    rel = packed[REL_ROW, :NUM_BUCKETS * 3 * HPG].reshape(NUM_BUCKETS, 3 * HPG)
    return {n: jnp.stack(v) for n, v in out.items()}, rel


INPUT_NAMES = ("x", "mem") + ("rel_bias", "norm_mix_pre", "w_in", "b_gate", "conv_dw", "conv_dw_bias", "conv_ln_g",
                              "conv_ln_b", "w_conv_out", "w_att_out", "norm_mem", "w_mem_kv", "w_mem_out", "w_out",
                              "norm_mix_post", "norm_ffn_pre", "w_ffn_in", "w_ffn_out", "norm_ffn_post")
WEIGHT_NAMES = INPUT_NAMES[2:]


def kernel(*args):
    nw = len(WEIGHT_NAMES)
    a = dict(zip(INPUT_NAMES, args[:2 + nw]))
    target = args[2 + nw]
    mom = dict(zip(WEIGHT_NAMES, args[3 + nw:3 + 2 * nw]))
    var = dict(zip(WEIGHT_NAMES, args[3 + 2 * nw:3 + 3 * nw]))
    xi, yi, ci = _mesh_pos()
    chip = 2 * xi + yi
    where = jnp.stack([ci, chip]).astype(I32)

    head, tail = ("w_in",), tuple(n for n in BIG if n != "w_in")
    shard = lambda l, names: [a[n][l].astype(BF16) for n in names]
    cdw = jnp.pad(a["conv_dw"].reshape(DEPTH * KSIZE, CW // 4), ((0, CDW_ROWS - DEPTH * KSIZE), (0, 0)))
    own_head0 = shard(0, head)
    *got_head0, gcdw = _all_gather(own_head0, cdw)
    gcdw = lax.dynamic_update_slice(gcdw, cdw[None], (chip, 0, 0))
    conv_dw = gcdw[:, :DEPTH * KSIZE].reshape(N_CHIPS, DEPTH, KSIZE, CW // 4).transpose(1, 2, 0, 3)
    conv_dw = jnp.pad(conv_dw.reshape(DEPTH, KSIZE, CW), ((0, 0), (0, 1), (0, 0)))
    gmix = [a["norm_mix_pre"][l][None, :] for l in range(DEPTH)]

    def whole(names, gathered, own):
        w = {}
        for n, g, s in zip(names, gathered, own):
            (s0, s1), axis = SHARD[n]
            blk = lax.dynamic_update_slice(g, s[None], (chip, 0, 0))
            w[n] = blk.reshape(N_CHIPS * s0, s1) if axis == 0 else blk.transpose(1, 0, 2).reshape(s0, N_CHIPS * s1)
        if "w_in" in w:
            w["w_in"] = _z_cols_from_ref(w["w_in"])
        return w

    def vectors(l):
        w = {n: a[n][l][None, :] for n, _ in VEC_ROWS}
        w["conv_dw"] = conv_dw[l]
        return w

    def landed(flight, after, name):
        send_sem, recv_sem, thru, lands, _ = flight
        return _all_gather_wait(send_sem, recv_sem, thru, lands, after, name)

    flights = {"tail0": _all_gather_start(shard(0, tail), got_head0[0], "all_gather_start_tail0")}
    gmix[0] = gmix[0] + flights["tail0"][4][0, 0]

    def layer0(x):
        def rest(after):
            own, got = landed(flights["tail0"], after, "all_gather_wait_tail0")
            flights["head1"] = _all_gather_start(shard(1, head), got[0], "all_gather_start_head1")
            w = whole(tail, got, own)
            w["b_gate"] = a["b_gate"][0][None, :] + flights["head1"][4][0, 0]
            return w

        return {**vectors(0), **whole(head, got_head0, own_head0)}, rest

    def layer1(x):
        own, got = landed(flights["head1"], x, "all_gather_wait_head1")
        flights["tail1"] = _all_gather_start(shard(1, tail), got[0], "all_gather_start_tail1")
        first = {**vectors(1), **whole(head, got, own)}
        first["conv_dw_bias"] = first["conv_dw_bias"] + flights["tail1"][4][0, 0]

        def rest(after):
            own_t, got_t = landed(flights["tail1"], after, "all_gather_wait_tail1")
            return whole(tail, got_t, own_t)

        return first, rest

    def by_chip(layer_grads, names):
        out = []
        for n in names:
            (s0, s1), axis = SHARD[n]
            g = _ref_cols_from_z(layer_grads[n]) if n == "w_in" else layer_grads[n]
            g = g.reshape(N_CHIPS, s0, s1) if axis == 0 else g.reshape(s0, N_CHIPS, s1).transpose(1, 0, 2)
            out.append(g.astype(BF16))
        return out

    early = ("w_ffn_in", "w_ffn_out")
    last = ("w_in",)
    late = tuple(n for n in BIG if n not in early + last)
    scattering = {}

    def on_grads(l, layer_grads):
        scattering["l1"] = _reduce_start(by_chip(layer_grads, BIG), layer_grads["w_in"], "grad_reduce_start_l1")
        return scattering["l1"][4]

    def on_last_ffn_grads(layer_grads):
        scattering["ffn"] = _reduce_start(by_chip(layer_grads, early), layer_grads["w_ffn_in"], "grad_reduce_start_ffn")
        return scattering["ffn"][4]

    def on_last_w_in_grad(layer_grads):
        scattering["w_in"] = _reduce_start(by_chip(layer_grads, last), layer_grads["w_in"], "grad_reduce_start_w_in")
        return scattering["w_in"][4]

    loss_part, gx, grads, drel = _local_step(a["x"][0], a["mem"][0], target[0], a["rel_bias"], [layer0, layer1], gmix,
                                             on_grads, on_last_ffn_grads, on_last_w_in_grad)

    reduced = {}
    send_sem, recv_sem, thru, lands, _ = scattering["l1"]
    sent, arrived = _reduce_wait(send_sem, recv_sem, thru, lands, gx, "grad_reduce_wait_l1")
    for n, p, r in zip(BIG, sent, arrived):
        reduced[n] = _sum_devices(where, p, r, 1, None, "grad_sum_devices_l1_" + n)
    for key, names in (("ffn", early), ("w_in", last)):
        send_sem, recv_sem, thru, lands, _ = scattering[key]
        sent, arrived = _reduce_wait(send_sem, recv_sem, thru, lands, gx, "grad_reduce_wait_" + key)
        for n, p, r in zip(names, sent, arrived):
            reduced[n] = _sum_devices(where, p, r, 0, reduced[n], "grad_sum_devices_l0_" + n)
    packed = by_chip(grads[0], late)
    from_sibling = _sibling_exchange(packed)
    chip_sums = [_add_own_half(where, p, r, "grad_add_sibling_" + n) for n, p, r in zip(late, packed, from_sibling)]
    from_chips = _chip_exchange(chip_sums)
    shared = _sibling_share([_sum_chips(where, s, r, reduced[n], "grad_sum_chips_" + n)
                             for n, s, r in zip(late, chip_sums, from_chips)])
    reduced.update(zip(late, shared))
    reduced = [reduced[n] for n in BIG]

    gvec = _all_reduce_small(_pack_vectors(
        lambda n: jnp.stack([grads[l][n][0] for l in range(DEPTH)]), drel[:, :3 * HPG],
        jnp.stack([grads[l]["conv_dw"] for l in range(DEPTH)]), "pack_vector_grads", loss_part))
    loss = gvec[LOSS_ROW, 0]
    lens = {n: a[n].shape[1] for n, _ in VEC_ROWS}
    g_vec, g_rel = _unpack_vectors(gvec, lens)
    g_cdw = lax.dynamic_slice_in_dim(gvec[CDW_ROW:CDW_ROW + CDW_GROWS].reshape(DEPTH, KSIZE, CW), chip * (CW // 4),
                                     CW // 4, axis=2)

    grad, delta, new_m, new_v = {}, {}, {}, {}
    for n, g in zip(BIG, reduced):
        shape = a[n].shape
        flat2 = lambda t: t.reshape(shape[0] * shape[1], shape[2])
        d, m2, v2 = _adamw(flat2(a[n]), flat2(g), flat2(mom[n]), flat2(var[n]), "adamw_" + n)
        grad[n], delta[n], new_m[n], new_v[n] = g, d.reshape(shape), m2.reshape(shape), v2.reshape(shape)
    shape = a["conv_dw"].shape
    flat2 = lambda t: t.reshape(shape[0] * shape[1], shape[2])
    d, m2, v2 = _adamw(flat2(a["conv_dw"]), flat2(g_cdw), flat2(mom["conv_dw"]), flat2(var["conv_dw"]), "adamw_conv_dw")
    grad["conv_dw"], delta["conv_dw"], new_m["conv_dw"], new_v["conv_dw"] = (
        g_cdw, d.reshape(shape), m2.reshape(shape), v2.reshape(shape))
    zero_cdw = jnp.zeros((DEPTH, KSIZE, CW), F32)
    pk = lambda src, name: _pack_vectors(lambda n: src[n], src["rel_bias"], zero_cdw, name)
    d, m2, v2 = _adamw(pk(a, "pack_vector_w"), gvec, pk(mom, "pack_vector_m"), pk(var, "pack_vector_v"),
                       "adamw_vectors")
    for src, dst in ((d, delta), (m2, new_m), (v2, new_v)):
        vec, rel = _unpack_vectors(src, lens)
        dst.update(vec)
        dst["rel_bias"] = rel
    grad.update(g_vec)
    grad["rel_bias"] = g_rel

    outs = [loss, gx[None]]
    for group in (grad, delta, new_m, new_v):
        outs += [group[n] for n in WEIGHT_NAMES]
    return tuple(outs)
```

```python
import functools
import math

import jax
import jax.numpy as jnp
from jax import lax
from jax.experimental import pallas as pl
from jax.experimental.pallas import tpu as pltpu

F32 = jnp.float32
BF16 = jnp.bfloat16
I32 = jnp.int32

D = 1024
DEPTH = 2
N_MEM = 256
CW = 512
KSIZE = 31
PAD = KSIZE // 2
DILS = (1, 4, 16)
RADIUS = 64
HPG = 4
HD = 64
GW = HPG * HD
MH = 4
MHD = 128
MW = MH * MHD
FH = 2816
NIN = 6912
C1 = 2 * CW
R_ATT = C1
R_MEM = R_ATT + 9 * GW
R_GATE = R_MEM + MW
Z_GATE = 0
Z_CONV = 3 * D
Z_MEM = Z_CONV + C1
Z_ATT = Z_MEM + MW
NUM_BUCKETS = 32
MAX_DISTANCE = 1024
RMS_EPS = 1e-6
LN_EPS = 1e-5
NEG_INF = -1e30
ATT_SCALE = HD ** -0.5
MEM_SCALE = MHD ** -0.5

ADAM_LR = 0.001
ADAM_B1 = 0.9
ADAM_B2 = 0.999
ADAM_EPS = 1e-08
ADAM_WD = 0.01
ADAM_STEP = 10

VMEM_LIMIT_BYTES = 56 * 1024 * 1024
ATT_QB = 128
ATT_TB = 16 * ATT_QB

MESH = pl.DeviceIdType.MESH


def _params(sem=None):
    return pltpu.CompilerParams(dimension_semantics=sem, vmem_limit_bytes=VMEM_LIMIT_BYTES)


def _sigmoid(v):
    return 1.0 / (1.0 + jnp.exp(-v))


def _dot(a, b):
    return jnp.dot(a, b, preferred_element_type=F32)


def _dot_nt(a, b):
    return lax.dot_general(a, b, (((1,), (1,)), ((), ())), preferred_element_type=F32)


def _dot_tn(a, b):
    return lax.dot_general(a, b, (((0,), (0,)), ((), ())), preferred_element_type=F32)


def _rms_fwd_val(v, g):
    r = lax.rsqrt(jnp.mean(v * v, axis=-1, keepdims=True) + RMS_EPS)
    return v * r * g


def _rms_bwd_val(v, g, dy):
    r = lax.rsqrt(jnp.mean(v * v, axis=-1, keepdims=True) + RMS_EPS)
    vh = v * r
    dvh = dy * g
    dv = r * (dvh - vh * jnp.mean(dvh * vh, axis=-1, keepdims=True))
    return dv, dy * vh


def _row(i):
    return (i, 0)


def _fixed(*_):
    return (0, 0)


def _mm_nn(a, b, tm, out_dtype, name):
    M, K = a.shape
    N = b.shape[1]

    def body(a_ref, b_ref, o_ref):
        o_ref[...] = _dot(a_ref[...], b_ref[...]).astype(out_dtype)

    return pl.pallas_call(
        body, name=name, grid=(M // tm,),
        in_specs=[pl.BlockSpec((tm, K), _row), pl.BlockSpec((K, N), _fixed, pipeline_mode=pl.Buffered(1))],
        out_specs=pl.BlockSpec((tm, N), _row),
        out_shape=jax.ShapeDtypeStruct((M, N), out_dtype),
        compiler_params=_params(("parallel",)),
    )(a, b)


def _mm_nt_rms_bwd(a, b, x, g, dres, tm, name):
    M, N = a.shape

    def body(a_ref, b_ref, x_ref, g_ref, dres_ref, dx_ref, dg_ref):
        @pl.when(pl.program_id(0) == 0)
        def _():
            dg_ref[...] = jnp.zeros_like(dg_ref)

        dv, dgr = _rms_bwd_val(x_ref[...], g_ref[...], _dot_nt(a_ref[...], b_ref[...]))
        dx_ref[...] = dres_ref[...] + dv
        dg_ref[...] += jnp.sum(dgr, axis=0, keepdims=True)

    rows = pl.BlockSpec((tm, D), _row)
    return pl.pallas_call(
        body, name=name, grid=(M // tm,),
        in_specs=[pl.BlockSpec((tm, N), _row), pl.BlockSpec((D, N), _fixed, pipeline_mode=pl.Buffered(1)), rows,
                  pl.BlockSpec((1, D), _fixed), rows],
        out_specs=[rows, pl.BlockSpec((1, D), _fixed)],
        out_shape=[jax.ShapeDtypeStruct((M, D), F32), jax.ShapeDtypeStruct((1, D), F32)],
        compiler_params=_params(("arbitrary",)),
    )(a, b, x, g, dres)


def _mm_tn(a, b, ts, tn, name):
    S, K = a.shape
    N = b.shape[1]

    def body(a_ref, b_ref, o_ref):
        @pl.when(pl.program_id(1) == 0)
        def _():
            o_ref[...] = jnp.zeros_like(o_ref)

        o_ref[...] += _dot_tn(a_ref[...], b_ref[...])

    return pl.pallas_call(
        body, name=name, grid=(N // tn, S // ts),
        in_specs=[pl.BlockSpec((ts, K), lambda j, s: (s, 0)), pl.BlockSpec((ts, tn), lambda j, s: (s, j))],
        out_specs=pl.BlockSpec((K, tn), lambda j, s: (0, j)),
        out_shape=jax.ShapeDtypeStruct((K, N), F32),
        compiler_params=_params(("parallel", "arbitrary")),
    )(a, b)


def _rms_h(x, g, name):
    S = x.shape[0]
    T = 512

    def body(x_ref, g_ref, h_ref):
        h_ref[...] = _rms_fwd_val(x_ref[...], g_ref[...]).astype(BF16)

    return pl.pallas_call(
        body, name=name, grid=(S // T,),
        in_specs=[pl.BlockSpec((T, D), _row), pl.BlockSpec((1, D), _fixed)],
        out_specs=pl.BlockSpec((T, D), _row),
        out_shape=jax.ShapeDtypeStruct((S, D), BF16),
        compiler_params=_params(("parallel",)),
    )(x, g)


CONV_T = 256
CONV_HALO = 16
CONV_RC = 32


def _halo_specs(T, halo, S, width, col):
    per = T // halo
    last = S // halo - 1
    return [
        pl.BlockSpec((T, width), lambda i: (i, col)),
        pl.BlockSpec((halo, width), lambda i: (jnp.maximum(i * per - 1, 0), col)),
        pl.BlockSpec((halo, width), lambda i: (jnp.minimum((i + 1) * per, last), col)),
    ]


def _glu(zb):
    zb = zb.astype(F32)
    return zb[:, :CW] * _sigmoid(zb[:, CW:])


CONV_EXT = CONV_T + 2 * CONV_HALO
SUBLANES = 8


def _fill_shifted(sh_ref, ext_ref, cur, prev, nxt):
    T, halo = CONV_T, CONV_HALO
    i = pl.program_id(0)
    n = pl.num_programs(0)
    ext_ref[0:halo, :] = jnp.where(i > 0, prev, 0.0)
    ext_ref[halo:halo + T, :] = cur
    ext_ref[halo + T:CONV_EXT, :] = jnp.where(i < n - 1, nxt, 0.0)
    ext_ref[CONV_EXT:CONV_EXT + SUBLANES, :] = jnp.zeros((SUBLANES, CW), F32)
    for b in range(SUBLANES):
        sh_ref[b] = ext_ref[b:b + CONV_EXT, :]


def _window(sh_ref, start, rows):
    b = start % SUBLANES
    return sh_ref[b, start - b:start - b + rows, :]


def _shifted_scratch():
    return [pltpu.VMEM((CONV_EXT + SUBLANES, CW), F32), pltpu.VMEM((SUBLANES, CONV_EXT, CW), F32)]


def _conv_fwd(z, wdw, bdw, lng, lnb, name):
    S = z.shape[0]
    T, HL, RC = CONV_T, CONV_HALO, CONV_RC

    def body(cur_ref, prev_ref, next_ref, w_ref, b_ref, g_ref, bb_ref, yc_ref, act_ref, ext_ref, sh_ref):
        _fill_shifted(sh_ref, ext_ref, _glu(cur_ref[...]), _glu(prev_ref[...]), _glu(next_ref[...]))
        for c in range(T // RC):
            acc = jnp.zeros((RC, CW), F32)
            for k in range(KSIZE):
                acc = acc + w_ref[k:k + 1, :] * _window(sh_ref, c * RC + k + HL - PAD, RC)
            yc = acc + b_ref[...]
            yc_ref[c * RC:(c + 1) * RC, :] = yc
            mu = jnp.mean(yc, axis=-1, keepdims=True)
            xc = yc - mu
            ln = xc * lax.rsqrt(jnp.mean(xc * xc, axis=-1, keepdims=True) + LN_EPS) * g_ref[...] + bb_ref[...]
            act_ref[c * RC:(c + 1) * RC, :] = (ln * _sigmoid(ln)).astype(BF16)

    return pl.pallas_call(
        body, name=name, grid=(S // T,),
        in_specs=_halo_specs(T, HL, S, C1, Z_CONV // C1) + [pl.BlockSpec((32, CW), _fixed)]
        + [pl.BlockSpec((1, CW), _fixed)] * 3,
        out_specs=[pl.BlockSpec((T, CW), _row), pl.BlockSpec((T, CW), _row)],
        out_shape=[jax.ShapeDtypeStruct((S, CW), F32), jax.ShapeDtypeStruct((S, CW), BF16)],
        scratch_shapes=_shifted_scratch(),
        compiler_params=_params(("parallel",)),
    )(z, z, z, wdw, bdw, lng, lnb)


def _conv_bwd_ln(yc, dact, lng, lnb, name):
    S = yc.shape[0]
    T = 512

    def body(yc_ref, da_ref, g_ref, b_ref, dyc_ref, dg_ref, db_ref, dbias_ref):
        yc_v = yc_ref[...]
        mu = jnp.mean(yc_v, axis=-1, keepdims=True)
        xc = yc_v - mu
        r = lax.rsqrt(jnp.mean(xc * xc, axis=-1, keepdims=True) + LN_EPS)
        yn = xc * r
        ln = yn * g_ref[...] + b_ref[...]
        sg = _sigmoid(ln)
        dln = da_ref[...].astype(F32) * (sg * (1.0 + ln * (1.0 - sg)))
        dyn = dln * g_ref[...]
        dyc = r * (dyn - jnp.mean(dyn, axis=-1, keepdims=True) - yn * jnp.mean(dyn * yn, axis=-1, keepdims=True))
        dyc_ref[...] = dyc

        @pl.when(pl.program_id(0) == 0)
        def _():
            dg_ref[...] = jnp.zeros_like(dg_ref)
            db_ref[...] = jnp.zeros_like(db_ref)
            dbias_ref[...] = jnp.zeros_like(dbias_ref)

        dg_ref[...] += jnp.sum(dln * yn, axis=0, keepdims=True)
        db_ref[...] += jnp.sum(dln, axis=0, keepdims=True)
        dbias_ref[...] += jnp.sum(dyc, axis=0, keepdims=True)

    vec = pl.BlockSpec((1, CW), _fixed)
    return pl.pallas_call(
        body, name=name, grid=(S // T,),
        in_specs=[pl.BlockSpec((T, CW), _row), pl.BlockSpec((T, CW), _row), vec, vec],
        out_specs=[pl.BlockSpec((T, CW), _row), vec, vec, vec],
        out_shape=[jax.ShapeDtypeStruct((S, CW), F32)] + [jax.ShapeDtypeStruct((1, CW), F32)] * 3,
        compiler_params=_params(("arbitrary",)),
    )(yc, dact, lng, lnb)


def _conv_bwd_dw(z, dyc, wdw, dz, name):
    S = z.shape[0]
    T, HL, RC = CONV_T, CONV_HALO, CONV_RC

    def body(zc_ref, zp_ref, zn_ref, dc_ref, dp_ref, dn_ref, w_ref, dz_in, dz_ref, dw_ref, uext_ref, ush_ref,
             dext_ref, dsh_ref, dwacc_ref):
        _fill_shifted(ush_ref, uext_ref, _glu(zc_ref[...]), _glu(zp_ref[...]), _glu(zn_ref[...]))
        _fill_shifted(dsh_ref, dext_ref, dc_ref[...], dp_ref[...], dn_ref[...])

        @pl.when(pl.program_id(0) == 0)
        def _():
            dwacc_ref[...] = jnp.zeros_like(dwacc_ref)

        for c in range(T // RC):
            dcur = dc_ref[c * RC:(c + 1) * RC, :]
            du = jnp.zeros((RC, CW), F32)
            for k in range(KSIZE):
                du = du + w_ref[k:k + 1, :] * _window(dsh_ref, c * RC + HL + PAD - k, RC)
                prod = dcur * _window(ush_ref, c * RC + k + HL - PAD, RC)
                dwacc_ref[k] += jnp.sum(prod.reshape(RC // SUBLANES, SUBLANES, CW), axis=0)
            zc = zc_ref[c * RC:(c + 1) * RC, :].astype(F32)
            a, gt = zc[:, :CW], zc[:, CW:]
            sg = _sigmoid(gt)
            dz_ref[c * RC:(c + 1) * RC, 0:CW] = (du * sg).astype(BF16)
            dz_ref[c * RC:(c + 1) * RC, CW:C1] = (du * a * sg * (1.0 - sg)).astype(BF16)

        @pl.when(pl.program_id(0) == pl.num_programs(0) - 1)
        def _():
            dw_ref[...] = jnp.sum(dwacc_ref[...], axis=1)

    return pl.pallas_call(
        body, name=name, grid=(S // T,),
        in_specs=_halo_specs(T, HL, S, C1, Z_CONV // C1) + _halo_specs(T, HL, S, CW, 0)
        + [pl.BlockSpec((32, CW), _fixed), pl.BlockSpec(memory_space=pl.ANY)],
        out_specs=[pl.BlockSpec((T, C1), lambda i: (i, Z_CONV // C1)), pl.BlockSpec((32, CW), _fixed)],
        out_shape=[jax.ShapeDtypeStruct(dz.shape, BF16), jax.ShapeDtypeStruct((32, CW), F32)],
        input_output_aliases={7: 0},
        scratch_shapes=_shifted_scratch() + _shifted_scratch() + [pltpu.VMEM((32, SUBLANES, CW), F32)],
        compiler_params=_params(("arbitrary",)),
    )(z, z, z, dyc, dyc, dyc, wdw, dz)


def _t5_bucket(rel):
    nb = NUM_BUCKETS // 2
    max_exact = nb // 2
    ret = jnp.where(rel > 0, nb, 0)
    n = jnp.abs(rel)
    nf = jnp.maximum(n, 1).astype(F32)
    large = max_exact + (jnp.log(nf / max_exact) / math.log(MAX_DISTANCE / max_exact)
                         * (nb - max_exact)).astype(I32)
    large = jnp.minimum(large, nb - 1)
    return ret + jnp.where(n < max_exact, n, large)


def _offsets_qk(nq, nk, shift):
    return lax.broadcasted_iota(I32, (nq, nk), 1) + shift - lax.broadcasted_iota(I32, (nq, nk), 0)


def _bias_table(bk, rb_ref, col, off):
    acc = jnp.zeros(bk.shape, F32)
    for b in range(NUM_BUCKETS):
        acc = jnp.where(bk == b, rb_ref[b, col], acc)
    return jnp.where(jnp.abs(off) <= RADIUS, acc, NEG_INF)


def _to_halves(scr, row0, val):
    rows = val.shape[0]
    v = val.astype(F32)
    scr[0, row0:row0 + rows, :] = v[:, :128]
    scr[1, row0:row0 + rows, :] = v[:, 128:]


ATT_FWD_GROUP = 2
ATT_BWD_GROUP = 1


def _att_units(d, fn, group):
    nj = ATT_TB // (ATT_QB * d)
    if nj == 1:
        def trip(t, c):
            r0 = pl.multiple_of(t * 8, 8)
            for u in range(0, 8, group):
                fn([(r0 + u + v, 0) for v in range(group)])
            return c

        lax.fori_loop(0, d // 8, trip, 0)
        return
    for r in range(d):
        def step(t, c, r=r):
            fn([(r, t * group + u) for u in range(group)])
            return c

        lax.fori_loop(0, nj // group, step, 0)


def _unit_row(r, j, d):
    if isinstance(j, int):
        return j * ATT_QB * d + r
    return pl.multiple_of(j * (ATT_QB * d), ATT_QB) + r


def _att_fwd(z, rel_bias, g, name):
    S = z.shape[0]
    d = DILS[g]
    TB, QB = ATT_TB, ATT_QB
    H = RADIUS * d
    L = S // d
    cq = (Z_ATT + 3 * GW * g) // GW
    ck, cv = cq + 1, cq + 2
    bk = _t5_bucket(_offsets_qk(QB, 2 * QB, -RADIUS) * d)

    def body(rb_ref, bk_ref, q_ref, kc_ref, kp_ref, kn_ref, vc_ref, vp_ref, vn_ref, o_ref, l_ref,
             qs, ks, vs, os_, ls, bias):
        i = pl.program_id(0)

        @pl.when(i == 0)
        def _():
            off = _offsets_qk(QB, 2 * QB, -RADIUS)
            for h in range(HPG):
                bias[h] = _bias_table(bk_ref[...], rb_ref, g * HPG + h, off)

        _to_halves(qs, 0, q_ref[...].astype(F32) * ATT_SCALE)
        for scr, p_ref, c_ref, n_ref in ((ks, kp_ref, kc_ref, kn_ref), (vs, vp_ref, vc_ref, vn_ref)):
            _to_halves(scr, 0, p_ref[...])
            _to_halves(scr, H, c_ref[...])
            _to_halves(scr, H + TB, n_ref[...])

        lo = lax.broadcasted_iota(I32, (QB, 128), 1) < HD

        def units(rjs):
            work = []
            for r, j in rjs:
                row = _unit_row(r, j, d)
                km = lax.broadcasted_iota(I32, (1, 2 * QB), 1) + (i * (TB // d) + j * QB - RADIUS)
                edge = jnp.where(jnp.where(km >= 0, km, L) < L, 0.0, NEG_INF)
                for hf in (0, 1):
                    q2 = qs[hf, pl.ds(row, QB, stride=d), :]
                    k2 = ks[hf, pl.ds(row, 2 * QB, stride=d), :].astype(BF16)
                    v2 = vs[hf, pl.ds(row, 2 * QB, stride=d), :].astype(BF16)
                    qq = jnp.concatenate([jnp.where(lo, q2, 0.0), jnp.where(lo, 0.0, q2)], axis=0).astype(BF16)
                    work.append((row, hf, edge, k2, v2, qq))
            scores = [_dot_nt(qq, k2) for (_, _, _, k2, _, qq) in work]
            probs = []
            for (row, hf, edge, *_), ss in zip(work, scores):
                es, stats = [], []
                for hh in (0, 1):
                    s = ss[hh * QB:(hh + 1) * QB] + bias[2 * hf + hh] + edge
                    m = jnp.max(s, axis=-1, keepdims=True)
                    e = jnp.exp(s - m)
                    den = jnp.sum(e, axis=-1, keepdims=True)
                    es.append(e.astype(BF16))
                    stats.append((1.0 / den, m + jnp.log(den)))
                probs.append((jnp.concatenate(es, axis=0), stats))
            for (row, hf, _, _, v2, _), (ee, stats) in zip(work, probs):
                oo = _dot(ee, v2)
                os_[hf, pl.ds(row, QB, stride=d), :] = jnp.where(lo, oo[:QB] * stats[0][0], oo[QB:] * stats[1][0])
                ls[hf, pl.ds(row, QB, stride=d), :] = jnp.where(lo, stats[0][1], stats[1][1])

        _att_units(d, units, ATT_FWD_GROUP)
        for hf in (0, 1):
            o_ref[:, hf * 128:(hf + 1) * 128] = os_[hf].astype(BF16)
            l_ref[:, hf * 128:(hf + 1) * 128] = ls[hf]

    def halo3(col):
        c, p, n = _halo_specs(TB, H, S, GW, col)
        return [c, p, n]

    return pl.pallas_call(
        body, name=name, grid=(S // TB,),
        in_specs=[pl.BlockSpec(memory_space=pltpu.SMEM), pl.BlockSpec((QB, 2 * QB), _fixed),
                  pl.BlockSpec((TB, GW), lambda i: (i, cq))] + halo3(ck) + halo3(cv),
        out_specs=[pl.BlockSpec((TB, GW), _row), pl.BlockSpec((TB, GW), _row)],
        out_shape=[jax.ShapeDtypeStruct((S, GW), BF16), jax.ShapeDtypeStruct((S, GW), F32)],
        scratch_shapes=[pltpu.VMEM((2, TB, 128), F32), pltpu.VMEM((2, TB + 2 * H, 128), F32),
                        pltpu.VMEM((2, TB + 2 * H, 128), F32), pltpu.VMEM((2, TB, 128), F32),
                        pltpu.VMEM((2, TB, 128), F32), pltpu.VMEM((HPG, QB, 2 * QB), F32)],
        compiler_params=_params(("arbitrary",)),
    )(rel_bias, bk, z, z, z, z, z, z, z)


def _att_combine(os3, ls3, name):
    S = os3[0].shape[0]
    T = 1024

    def body(o1, o2, o3, l1, l2, l3, o_ref, l_ref):
        lv = [l1[...], l2[...], l3[...]]
        m = jnp.maximum(jnp.maximum(lv[0], lv[1]), lv[2])
        e = [jnp.exp(v - m) for v in lv]
        den = e[0] + e[1] + e[2]
        acc = jnp.zeros_like(m)
        for ev, o in zip(e, (o1, o2, o3)):
            acc = acc + (ev / den) * o[...].astype(F32)
        o_ref[...] = acc.astype(BF16)
        l_ref[...] = m + jnp.log(den)

    blk = pl.BlockSpec((T, GW), _row)
    return pl.pallas_call(
        body, name=name, grid=(S // T,), in_specs=[blk] * 6, out_specs=[blk, blk],
        out_shape=[jax.ShapeDtypeStruct((S, GW), BF16), jax.ShapeDtypeStruct((S, GW), F32)],
        compiler_params=_params(("parallel",)),
    )(*os3, *ls3)


def _att_prep(do, o, lse, name):
    S = do.shape[0]
    T = 1024

    def body(do_ref, o_ref, l_ref, out_ref):
        prod = do_ref[...].astype(F32) * o_ref[...].astype(F32)
        dd = [jnp.broadcast_to(jnp.sum(prod[:, h * HD:(h + 1) * HD], axis=-1, keepdims=True), (T, HD))
              for h in range(HPG)]
        lane = lax.broadcasted_iota(I32, (T, GW), 1)
        out_ref[...] = jnp.where(lane % HD < HD // 2, l_ref[...], jnp.concatenate(dd, axis=-1))

    blk = pl.BlockSpec((T, GW), _row)
    return pl.pallas_call(
        body, name=name, grid=(S // T,), in_specs=[blk] * 3, out_specs=blk,
        out_shape=jax.ShapeDtypeStruct((S, GW), F32), compiler_params=_params(("parallel",)),
    )(do, o, lse)


def _att_bwd(z, rel_bias, do, ld, dz, g, name):
    S = z.shape[0]
    d = DILS[g]
    TB, QB = ATT_TB, ATT_QB
    H = RADIUS * d
    L = S // d
    E = TB + 2 * H
    cq = (Z_ATT + 3 * GW * g) // GW
    ck, cv = cq + 1, cq + 2
    bk_a = _t5_bucket(_offsets_qk(QB, 2 * QB, -RADIUS) * d)
    bk_b = _t5_bucket(-_offsets_qk(QB, 2 * QB, -RADIUS) * d)

    def body(rb_ref, bka_ref, bkb_ref, *refs):
        ins, (dz_ref, db_ref) = refs[:15], refs[16:18]
        qs, ks, vs, dos, ls, dqs, dks, dvs, bias_a, bias_b, dbias = refs[18:]
        i = pl.program_id(0)
        n = pl.num_programs(0)

        @pl.when(i == 0)
        def _():
            off = _offsets_qk(QB, 2 * QB, -RADIUS)
            for h in range(HPG):
                bias_a[h] = _bias_table(bka_ref[...], rb_ref, g * HPG + h, off)
                bias_b[h] = _bias_table(bkb_ref[...], rb_ref, g * HPG + h, off)
            dbias[...] = jnp.zeros_like(dbias)

        for a, scr in enumerate((qs, ks, vs, dos, ls)):
            c_ref, p_ref, n_ref = ins[3 * a:3 * a + 3]
            pre = (lambda v: v.astype(F32) * ATT_SCALE) if a == 0 else (lambda v: v)
            _to_halves(scr, 0, pre(p_ref[...]))
            _to_halves(scr, H, pre(c_ref[...]))
            _to_halves(scr, H + TB, pre(n_ref[...]))

        lo = lax.broadcasted_iota(I32, (QB, 128), 1) < HD

        def split(v):
            return jnp.concatenate([jnp.where(lo, v, 0.0), jnp.where(lo, 0.0, v)], axis=0).astype(BF16)

        def halves(v):
            return v[:QB], v[QB:]

        def units(rjs):
            work = []
            for r, j in rjs:
                row = _unit_row(r, j, d)
                cur = row + H
                m0 = i * (TB // d) + j * QB - RADIUS
                km = lax.broadcasted_iota(I32, (1, 2 * QB), 1) + m0
                edge_a = jnp.where(jnp.where(km >= 0, km, L) < L, 0.0, NEG_INF)
                for hf in (0, 1):
                    ld = lambda scr, at, nrow: scr[hf, pl.ds(at, nrow, stride=d), :]
                    w = dict(row=row, hf=hf, edge=edge_a, l_c=ld(ls, cur, QB), l_t=ld(ls, row, 2 * QB).T)
                    for nm, scr in (("q", qs), ("k", ks), ("v", vs), ("do", dos)):
                        w[nm + "_c"] = split(ld(scr, cur, QB))
                        w[nm + "_e"] = ld(scr, row, 2 * QB).astype(BF16)
                    work.append(w)
            for w in work:
                w["s"] = halves(_dot_nt(w["q_c"], w["k_e"]))
                w["dp"] = halves(_dot_nt(w["do_c"], w["v_e"]))
                w["s2"] = halves(_dot_nt(w["k_c"], w["q_e"]))
                w["dp2"] = halves(_dot_nt(w["v_c"], w["do_e"]))
            for w in work:
                w["ds"], w["p2"], w["ds2"] = [], [], []
                for hh in (0, 1):
                    h, c0 = 2 * w["hf"] + hh, HD * hh
                    l_c, l_t = w["l_c"], w["l_t"]
                    p = jnp.exp(w["s"][hh] + bias_a[h] + w["edge"] - l_c[:, c0:c0 + 1])
                    ds = p * (w["dp"][hh] - l_c[:, c0 + HD // 2:c0 + HD // 2 + 1])
                    dbias[h] += ds
                    p2 = jnp.exp(w["s2"][hh] + bias_b[h] + w["edge"] - l_t[c0:c0 + 1, :])
                    ds2 = p2 * (w["dp2"][hh] - l_t[c0 + HD // 2:c0 + HD // 2 + 1, :])
                    w["ds"].append(ds.astype(BF16))
                    w["p2"].append(p2.astype(BF16))
                    w["ds2"].append(ds2.astype(BF16))
            for w in work:
                at = pl.ds(w["row"], QB, stride=d)
                both = lambda pair, rhs: halves(_dot(jnp.concatenate(pair, axis=0), rhs))
                dq = both(w["ds"], w["k_e"])
                dqs[w["hf"], at, :] = jnp.where(lo, dq[0], dq[1]) * ATT_SCALE
                dv = both(w["p2"], w["do_e"])
                dvs[w["hf"], at, :] = jnp.where(lo, dv[0], dv[1])
                dk = both(w["ds2"], w["q_e"])
                dks[w["hf"], at, :] = jnp.where(lo, dk[0], dk[1])

        _att_units(d, units, ATT_BWD_GROUP)
        for a, scr in enumerate((dqs, dks, dvs)):
            for hf in (0, 1):
                dz_ref[:, a * GW + hf * 128:a * GW + (hf + 1) * 128] = scr[hf].astype(BF16)

        @pl.when(i == n - 1)
        def _():
            rows = lax.broadcasted_iota(I32, (NUM_BUCKETS, 128), 0)
            lanes = lax.broadcasted_iota(I32, (NUM_BUCKETS, 128), 1)
            out = jnp.zeros((NUM_BUCKETS, 128), F32)
            bk = bka_ref[...]
            for h in range(HPG):
                acc = dbias[h]
                for b in range(NUM_BUCKETS):
                    tot = jnp.sum(jnp.sum(jnp.where(bk == b, acc, 0.0), axis=1, keepdims=True), axis=0, keepdims=True)
                    out = out + jnp.where((rows == b) & (lanes == h), tot, 0.0)
            db_ref[...] = out

    def halo3(col, width=GW):
        return _halo_specs(TB, H, S, width, col)

    one = pl.Buffered(1)

    def single(specs):
        return [pl.BlockSpec(s.block_shape, s.index_map, pipeline_mode=one) for s in specs]

    in_specs = ([pl.BlockSpec(memory_space=pltpu.SMEM), pl.BlockSpec((QB, 2 * QB), _fixed),
                 pl.BlockSpec((QB, 2 * QB), _fixed)]
                + single(halo3(cq) + halo3(ck) + halo3(cv) + halo3(0) + halo3(0))
                + [pl.BlockSpec(memory_space=pl.ANY)])
    return pl.pallas_call(
        body, name=name, grid=(S // TB,), in_specs=in_specs,
        out_specs=[pl.BlockSpec((TB, 3 * GW), lambda i: (i, cq // 3)), pl.BlockSpec((NUM_BUCKETS, 128), _fixed)],
        out_shape=[jax.ShapeDtypeStruct(dz.shape, BF16), jax.ShapeDtypeStruct((NUM_BUCKETS, 128), F32)],
        input_output_aliases={18: 0},
        scratch_shapes=[pltpu.VMEM((2, E, 128), F32)] * 5 + [pltpu.VMEM((2, TB, 128), F32)] * 3
        + [pltpu.VMEM((HPG, QB, 2 * QB), F32)] * 3,
        compiler_params=_params(("arbitrary",)),
    )(rel_bias, bk_a, bk_b, z, z, z, z, z, z, z, z, z, do, do, do, ld, ld, ld, dz)


def _memkv_fwd(mem, gm, wkv, name):
    def body(m_ref, g_ref, w_ref, hm_ref, kv_ref):
        hm = _rms_fwd_val(m_ref[...], g_ref[...]).astype(BF16)
        hm_ref[...] = hm
        kv_ref[...] = _dot(hm, w_ref[...]).astype(BF16)

    return pl.pallas_call(
        body, name=name,
        out_shape=[jax.ShapeDtypeStruct((N_MEM, D), BF16), jax.ShapeDtypeStruct((N_MEM, 2 * MW), BF16)],
        compiler_params=_params(),
    )(mem, gm, wkv)


def _memkv_bwd(mem, gm, hm, wkv, dkv, name):
    def body(m_ref, g_ref, hm_ref, w_ref, dkv_ref, dw_ref, dg_ref):
        dkv_b = dkv_ref[...].astype(BF16)
        dw_ref[...] = _dot_tn(hm_ref[...], dkv_b)
        dhm = _dot_nt(dkv_b, w_ref[...])
        _, dgr = _rms_bwd_val(m_ref[...], g_ref[...], dhm)
        dg_ref[...] = jnp.sum(dgr, axis=0, keepdims=True)

    return pl.pallas_call(
        body, name=name,
        out_shape=[jax.ShapeDtypeStruct((D, 2 * MW), F32), jax.ShapeDtypeStruct((1, D), F32)],
        compiler_params=_params(),
    )(mem, gm, hm, wkv, dkv)


MEM_T = 512


def _mem_q_spec():
    return pl.BlockSpec((MEM_T, MW), lambda i: (i, Z_MEM // MW))


def _memattn_fwd(z, kv, name):
    S = z.shape[0]
    T = MEM_T

    def body(q_ref, kv_ref, o_ref):
        for h in range(MH):
            kh = kv_ref[:, h * MHD:(h + 1) * MHD]
            vh = kv_ref[:, MW + h * MHD:MW + (h + 1) * MHD]
            s = _dot_nt(q_ref[:, h * MHD:(h + 1) * MHD], kh) * MEM_SCALE
            e = jnp.exp(s - jnp.max(s, axis=-1, keepdims=True))
            p = e / jnp.sum(e, axis=-1, keepdims=True)
            o_ref[:, h * MHD:(h + 1) * MHD] = _dot(p.astype(BF16), vh).astype(BF16)

    return pl.pallas_call(
        body, name=name, grid=(S // T,),
        in_specs=[_mem_q_spec(), pl.BlockSpec((N_MEM, 2 * MW), _fixed)],
        out_specs=pl.BlockSpec((T, MW), _row),
        out_shape=jax.ShapeDtypeStruct((S, MW), BF16),
        compiler_params=_params(("parallel",)),
    )(z, kv)


def _memattn_bwd(z, kv, dom, dz, name):
    S = z.shape[0]
    T = MEM_T

    def body(q_ref, kv_ref, do_ref, dz_in, dq_ref, dkv_ref):
        @pl.when(pl.program_id(0) == 0)
        def _():
            dkv_ref[...] = jnp.zeros_like(dkv_ref)

        for h in range(MH):
            kh = kv_ref[:, h * MHD:(h + 1) * MHD]
            vh = kv_ref[:, MW + h * MHD:MW + (h + 1) * MHD]
            qh = q_ref[:, h * MHD:(h + 1) * MHD]
            doh = do_ref[:, h * MHD:(h + 1) * MHD]
            s = _dot_nt(qh, kh) * MEM_SCALE
            e = jnp.exp(s - jnp.max(s, axis=-1, keepdims=True))
            p = e / jnp.sum(e, axis=-1, keepdims=True)
            dkv_ref[:, MW + h * MHD:MW + (h + 1) * MHD] += _dot_tn(p.astype(BF16), doh)
            dp = _dot_nt(doh, vh)
            ds = (p * (dp - jnp.sum(dp * p, axis=-1, keepdims=True))).astype(BF16)
            dq_ref[:, h * MHD:(h + 1) * MHD] = (_dot(ds, kh) * MEM_SCALE).astype(BF16)
            dkv_ref[:, h * MHD:(h + 1) * MHD] += _dot_tn(ds, qh) * MEM_SCALE

    return pl.pallas_call(
        body, name=name, grid=(S // T,),
        in_specs=[_mem_q_spec(), pl.BlockSpec((N_MEM, 2 * MW), _fixed), pl.BlockSpec((T, MW), _row),
                  pl.BlockSpec(memory_space=pl.ANY)],
        out_specs=[_mem_q_spec(), pl.BlockSpec((N_MEM, 2 * MW), _fixed)],
        out_shape=[jax.ShapeDtypeStruct(dz.shape, BF16), jax.ShapeDtypeStruct((N_MEM, 2 * MW), F32)],
        input_output_aliases={3: 0},
        compiler_params=_params(("arbitrary",)),
    )(z, kv, dom, dz)


MERGE_T = 512


def _gate_spec(T):
    return pl.BlockSpec((T, 3 * D), lambda i: (i, Z_GATE // (3 * D)))


def _branches(ca_ref, oa_ref, om_ref, wco_ref, wao_ref, wmo_ref, zg_ref, bg_ref):
    ys = [_dot(ca_ref[...], wco_ref[...]), _dot(oa_ref[...], wao_ref[...]), _dot(om_ref[...], wmo_ref[...])]
    gs = [_sigmoid(zg_ref[:, b * D:(b + 1) * D].astype(F32) + bg_ref[:, b * D:(b + 1) * D]) for b in range(3)]
    return ys, gs


def _merge_fwd(x, cact, oatt, om, z, wco, wao, wmo, wout, bgate, gpost, gnext, name):
    S = x.shape[0]
    T = MERGE_T

    def body(x_ref, ca_ref, oa_ref, om_ref, zg_ref, wco_ref, wao_ref, wmo_ref, wout_ref, bg_ref, gp_ref, gn_ref,
             x1_ref, mg_ref, t_ref, h_ref):
        ys, gs = _branches(ca_ref, oa_ref, om_ref, wco_ref, wao_ref, wmo_ref, zg_ref, bg_ref)
        mb = (gs[0] * ys[0] + gs[1] * ys[1] + gs[2] * ys[2]).astype(BF16)
        t = _dot(mb, wout_ref[...])
        mg_ref[...] = mb
        t_ref[...] = t
        x1 = x_ref[...] + _rms_fwd_val(t, gp_ref[...])
        x1_ref[...] = x1
        h_ref[...] = _rms_fwd_val(x1, gn_ref[...]).astype(BF16)

    full = lambda a: pl.BlockSpec(a.shape, _fixed)
    return pl.pallas_call(
        body, name=name, grid=(S // T,),
        in_specs=[pl.BlockSpec((T, D), _row), pl.BlockSpec((T, CW), _row), pl.BlockSpec((T, GW), _row),
                  pl.BlockSpec((T, MW), _row), _gate_spec(T)]
        + [full(wco), full(wao), full(wmo), full(wout), full(bgate), full(gpost), full(gnext)],
        out_specs=[pl.BlockSpec((T, D), _row)] * 4,
        out_shape=[jax.ShapeDtypeStruct((S, D), F32), jax.ShapeDtypeStruct((S, D), BF16),
                   jax.ShapeDtypeStruct((S, D), F32), jax.ShapeDtypeStruct((S, D), BF16)],
        compiler_params=_params(("parallel",)),
    )(x, cact, oatt, om, z, wco, wao, wmo, wout, bgate, gpost, gnext)


def _merge_bwd(dx1, t, mg, cact, oatt, om, z, wco, wao, wmo, wout, bgate, gpost, name):
    S = dx1.shape[0]
    T = MERGE_T

    def body(dx_ref, t_ref, mg_ref, ca_ref, oa_ref, om_ref, zg_ref, wco_ref, wao_ref, wmo_ref, wout_ref,
             bg_ref, gp_ref, dzg_ref, dca_ref, doa_ref, dom_ref, dwco_ref, dwao_ref, dwmo_ref, dwout_ref,
             dbg_ref, dgp_ref):
        accs = (dwco_ref, dwao_ref, dwmo_ref, dwout_ref, dbg_ref, dgp_ref)

        @pl.when(pl.program_id(0) == 0)
        def _():
            for a in accs:
                a[...] = jnp.zeros_like(a)

        dt, dgr = _rms_bwd_val(t_ref[...], gp_ref[...], dx_ref[...])
        dgp_ref[...] += jnp.sum(dgr, axis=0, keepdims=True)
        dtb = dt.astype(BF16)
        dwout_ref[...] += _dot_tn(mg_ref[...], dtb)
        dm = _dot_nt(dtb, wout_ref[...])
        ys, gs = _branches(ca_ref, oa_ref, om_ref, wco_ref, wao_ref, wmo_ref, zg_ref, bg_ref)
        for b, (act_ref, w_ref, dw_ref, da_ref) in enumerate(
                ((ca_ref, wco_ref, dwco_ref, dca_ref), (oa_ref, wao_ref, dwao_ref, doa_ref),
                 (om_ref, wmo_ref, dwmo_ref, dom_ref))):
            dzg = dm * ys[b] * gs[b] * (1.0 - gs[b])
            dzg_ref[:, b * D:(b + 1) * D] = dzg.astype(BF16)
            dbg_ref[:, b * D:(b + 1) * D] += jnp.sum(dzg, axis=0, keepdims=True)
            dy = (dm * gs[b]).astype(BF16)
            dw_ref[...] += _dot_tn(act_ref[...], dy)
            da_ref[...] = _dot_nt(dy, w_ref[...]).astype(BF16)

    full = lambda a: pl.BlockSpec(a.shape, _fixed)
    fullf = lambda a: jax.ShapeDtypeStruct(a.shape, F32)
    return pl.pallas_call(
        body, name=name, grid=(S // T,),
        in_specs=[pl.BlockSpec((T, D), _row), pl.BlockSpec((T, D), _row), pl.BlockSpec((T, D), _row),
                  pl.BlockSpec((T, CW), _row), pl.BlockSpec((T, GW), _row), pl.BlockSpec((T, MW), _row)]
        + [_gate_spec(T), full(wco), full(wao), full(wmo), full(wout), full(bgate), full(gpost)],
        out_specs=[_gate_spec(T), pl.BlockSpec((T, CW), _row), pl.BlockSpec((T, GW), _row),
                   pl.BlockSpec((T, MW), _row), full(wco), full(wao), full(wmo), full(wout), full(bgate), full(gpost)],
        out_shape=[jax.ShapeDtypeStruct((S, NIN), BF16), jax.ShapeDtypeStruct((S, CW), BF16),
                   jax.ShapeDtypeStruct((S, GW), BF16), jax.ShapeDtypeStruct((S, MW), BF16),
                   fullf(wco), fullf(wao), fullf(wmo), fullf(wout), fullf(bgate), fullf(gpost)],
        compiler_params=_params(("arbitrary",)),
    )(dx1, t, mg, cact, oatt, om, z, wco, wao, wmo, wout, bgate, gpost)


FFN_T = 256


def _ffn_fwd(x1, gu, wfo, gpost, gnext, name):
    S = x1.shape[0]
    T = FFN_T

    nxt = gnext is not None

    def body(x_ref, gu_ref, w_ref, gp_ref, *rest):
        x2_ref, f_ref = rest[nxt:nxt + 2]
        gv = gu_ref[:, :FH].astype(F32)
        uv = gu_ref[:, FH:].astype(F32)
        act = (gv * _sigmoid(gv) * uv).astype(BF16)
        f = _dot(act, w_ref[...])
        f_ref[...] = f
        x2 = x_ref[...] + _rms_fwd_val(f, gp_ref[...])
        x2_ref[...] = x2
        if nxt:
            rest[3][...] = _rms_fwd_val(x2, rest[0][...]).astype(BF16)

    return pl.pallas_call(
        body, name=name, grid=(S // T,),
        in_specs=[pl.BlockSpec((T, D), _row), pl.BlockSpec((T, 2 * FH), _row), pl.BlockSpec((FH, D), _fixed),
                  pl.BlockSpec((1, D), _fixed)] + [pl.BlockSpec((1, D), _fixed)] * nxt,
        out_specs=[pl.BlockSpec((T, D), _row)] * (2 + nxt),
        out_shape=[jax.ShapeDtypeStruct((S, D), F32)] * 2 + [jax.ShapeDtypeStruct((S, D), BF16)] * nxt,
        compiler_params=_params(("parallel",)),
    )(x1, gu, wfo, gpost, *([gnext] if nxt else []))


def _ffn_bwd(dx2, f, gu, wfo, gpost, name):
    S = dx2.shape[0]
    T = FFN_T

    def body(dx_ref, f_ref, gu_ref, w_ref, gp_ref, dgu_ref, df_ref, act_ref, dgp_ref):
        @pl.when(pl.program_id(0) == 0)
        def _():
            dgp_ref[...] = jnp.zeros_like(dgp_ref)

        df, dgr = _rms_bwd_val(f_ref[...], gp_ref[...], dx_ref[...])
        dgp_ref[...] += jnp.sum(dgr, axis=0, keepdims=True)
        dfb = df.astype(BF16)
        df_ref[...] = dfb
        dact = _dot_nt(dfb, w_ref[...])
        gv = gu_ref[:, :FH].astype(F32)
        uv = gu_ref[:, FH:].astype(F32)
        sg = _sigmoid(gv)
        silu = gv * sg
        act_ref[...] = (silu * uv).astype(BF16)
        dgu_ref[:, :FH] = (dact * uv * (sg * (1.0 + gv * (1.0 - sg)))).astype(BF16)
        dgu_ref[:, FH:] = (dact * silu).astype(BF16)

    return pl.pallas_call(
        body, name=name, grid=(S // T,),
        in_specs=[pl.BlockSpec((T, D), _row), pl.BlockSpec((T, D), _row), pl.BlockSpec((T, 2 * FH), _row),
                  pl.BlockSpec((FH, D), _fixed), pl.BlockSpec((1, D), _fixed)],
        out_specs=[pl.BlockSpec((T, 2 * FH), _row), pl.BlockSpec((T, D), _row), pl.BlockSpec((T, FH), _row),
                   pl.BlockSpec((1, D), _fixed)],
        out_shape=[jax.ShapeDtypeStruct((S, 2 * FH), BF16), jax.ShapeDtypeStruct((S, D), BF16),
                   jax.ShapeDtypeStruct((S, FH), BF16), jax.ShapeDtypeStruct((1, D), F32)],
        compiler_params=_params(("arbitrary",)),
    )(dx2, f, gu, wfo, gpost)


def _loss_head(y, target, name):
    S = y.shape[0]
    T = 512

    def body(y_ref, t_ref, dy_ref, l_ref):
        @pl.when(pl.program_id(0) == 0)
        def _():
            l_ref[...] = jnp.zeros_like(l_ref)

        e = y_ref[...] - t_ref[...]
        dy_ref[...] = e * (1.0 / D)
        l_ref[...] += (0.5 / D) * jnp.sum(jnp.sum(e * e, axis=1, keepdims=True), axis=0, keepdims=True)

    return pl.pallas_call(
        body, name=name, grid=(S // T,),
        in_specs=[pl.BlockSpec((T, D), _row)] * 2,
        out_specs=[pl.BlockSpec((T, D), _row), pl.BlockSpec((8, 128), _fixed)],
        out_shape=[jax.ShapeDtypeStruct((S, D), F32), jax.ShapeDtypeStruct((8, 128), F32)],
        compiler_params=_params(("arbitrary",)),
    )(y, target)


BIG = ("w_in", "w_conv_out", "w_att_out", "w_mem_kv", "w_mem_out", "w_out", "w_ffn_in", "w_ffn_out")
SMALL = ("rel_bias", "norm_mix_pre", "b_gate", "conv_dw_bias", "conv_ln_g", "conv_ln_b", "norm_mem",
         "norm_mix_post", "norm_ffn_pre", "norm_ffn_post")


def _layer_fwd(l, x, h, mem, w, rest, rel_bias, gnext):
    tag = f"_l{l}"
    z = _mm_nn(h, w["w_in"], 512, BF16, "mm_in" + tag)
    yc, cact = _conv_fwd(z, w["conv_dw"], w["conv_dw_bias"], w["conv_ln_g"], w["conv_ln_b"], "conv_fwd" + tag)
    og, lg = zip(*[_att_fwd(z, rel_bias, g, f"att_fwd_g{g}" + tag) for g in range(3)])
    oatt, lse = _att_combine(og, lg, "att_combine" + tag)
    w = {**w, **rest(oatt)}
    hm, kv = _memkv_fwd(mem, w["norm_mem"], w["w_mem_kv"], "memkv_fwd" + tag)
    om = _memattn_fwd(z, kv, "memattn_fwd" + tag)
    x1, mg, t, h2 = _merge_fwd(x, cact, oatt, om, z, w["w_conv_out"], w["w_att_out"], w["w_mem_out"], w["w_out"],
                               w["b_gate"], w["norm_mix_post"], w["norm_ffn_pre"], "merge_fwd" + tag)
    gu = _mm_nn(h2, w["w_ffn_in"], 512, BF16, "mm_ffn_in" + tag)
    x2, f, *hn = _ffn_fwd(x1, gu, w["w_ffn_out"], w["norm_ffn_post"], gnext, "ffn_fwd" + tag)
    saved = dict(x=x, h=h, z=z, yc=yc, cact=cact, oatt=oatt, lse=lse, hm=hm, kv=kv, om=om, x1=x1, mg=mg, t=t,
                 h2=h2, gu=gu, f=f)
    return x2, (hn[0] if hn else None), saved, w


def _layer_bwd(l, dx2, mem, w, rel_bias, s, on_ffn_grads=None, on_w_in_grad=None):
    tag = f"_l{l}"
    gr = {}
    dgu, df, act, gr["norm_ffn_post"] = _ffn_bwd(dx2, s["f"], s["gu"], w["w_ffn_out"], w["norm_ffn_post"], "ffn_bwd" + tag)
    gr["w_ffn_out"] = _mm_tn(act, df, 1024, 512, "dw_ffn_out" + tag)
    gr["w_ffn_in"] = _mm_tn(s["h2"], dgu, 2048, 1408, "dw_ffn_in" + tag)
    gpre = w["norm_ffn_pre"]
    if on_ffn_grads is not None:
        gpre = gpre + on_ffn_grads(gr)[0, 0]
    dx1, gr["norm_ffn_pre"] = _mm_nt_rms_bwd(dgu, w["w_ffn_in"], s["x1"], gpre, dx2, 512, "dh_ffn" + tag)
    (dz, dcact, doatt, dom, gr["w_conv_out"], gr["w_att_out"], gr["w_mem_out"], gr["w_out"], gr["b_gate"],
     gr["norm_mix_post"]) = _merge_bwd(dx1, s["t"], s["mg"], s["cact"], s["oatt"], s["om"], s["z"], w["w_conv_out"],
                                       w["w_att_out"], w["w_mem_out"], w["w_out"], w["b_gate"], w["norm_mix_post"],
                                       "merge_bwd" + tag)
    dyc, gr["conv_ln_g"], gr["conv_ln_b"], gr["conv_dw_bias"] = _conv_bwd_ln(
        s["yc"], dcact, w["conv_ln_g"], w["conv_ln_b"], "conv_bwd_ln" + tag)
    dz, dwdw = _conv_bwd_dw(s["z"], dyc, w["conv_dw"], dz, "conv_bwd_dw" + tag)
    gr["conv_dw"] = dwdw[:KSIZE]
    ld = _att_prep(doatt, s["oatt"], s["lse"], "att_prep" + tag)
    drb = []
    for g in range(3):
        dz, db = _att_bwd(s["z"], rel_bias, doatt, ld, dz, g, f"att_bwd_g{g}" + tag)
        drb.append(db)
    dz, dkv = _memattn_bwd(s["z"], s["kv"], dom, dz, "memattn_bwd" + tag)
    gr["w_mem_kv"], gr["norm_mem"] = _memkv_bwd(mem, w["norm_mem"], s["hm"], w["w_mem_kv"], dkv, "memkv_bwd" + tag)
    gr["w_in"] = _mm_tn(s["h"], dz, 2048, 1152, "dw_in" + tag)
    gmix = w["norm_mix_pre"]
    if on_w_in_grad is not None:
        gmix = gmix + on_w_in_grad(gr)[0, 0]
    dx, gr["norm_mix_pre"] = _mm_nt_rms_bwd(dz, w["w_in"], s["x"], gmix, dx1, 512, "dh_in" + tag)
    return dx, gr, drb


def _rel_bias_total(parts, name):
    def body(*refs):
        out_ref = refs[-1]
        acc = jnp.zeros((NUM_BUCKETS, 128), F32)
        for l in range(DEPTH):
            for g in range(3):
                v = refs[l * 3 + g][...]
                acc = acc + (v if g == 0 else pltpu.roll(v, HPG * g, axis=1))
        out_ref[...] = acc

    return pl.pallas_call(body, name=name, out_shape=jax.ShapeDtypeStruct((NUM_BUCKETS, 128), F32),
                          compiler_params=_params())(*[p for layer in parts for p in layer])


def _local_step(x, mem, target, rel_bias, layer_fns, gmix, on_grads=None, on_last_ffn_grads=None,
                on_last_w_in_grad=None):
    saved, layers = [], []
    h = _rms_h(x, gmix[0], "rms_mix_l0")
    for l in range(DEPTH):
        first, rest = layer_fns[l](x)
        x, h, s, w = _layer_fwd(l, x, h, mem, first, rest, rel_bias, gmix[l + 1] if l + 1 < DEPTH else None)
        saved.append(s)
        layers.append(w)
    dy, lpart = _loss_head(x, target, "loss_head")
    grads = [None] * DEPTH
    drb = [None] * DEPTH
    for l in reversed(range(DEPTH)):
        dy, grads[l], drb[l] = _layer_bwd(l, dy, mem, layers[l], rel_bias, saved[l],
                                          on_last_ffn_grads if l == 0 else None,
                                          on_last_w_in_grad if l == 0 else None)
        if on_grads is not None and l > 0:
            below = dict(layers[l - 1])
            below["norm_ffn_post"] = below["norm_ffn_post"] + on_grads(l, grads[l])[0, 0]
            layers[l - 1] = below
    return lpart[0, 0], dy, grads, _rel_bias_total(drb, "rel_bias_total")


def _z_cols_from_ref(w):
    att = [w[..., R_ATT + (3 * j + g) * GW:R_ATT + (3 * j + g + 1) * GW] for g in range(3) for j in range(3)]
    return jnp.concatenate([w[..., R_GATE:], w[..., :C1], w[..., R_MEM:R_GATE]] + att, axis=-1)


def _ref_cols_from_z(w):
    att = [w[..., Z_ATT + (3 * g + j) * GW:Z_ATT + (3 * g + j + 1) * GW] for j in range(3) for g in range(3)]
    return jnp.concatenate([w[..., Z_CONV:Z_MEM]] + att + [w[..., Z_MEM:Z_ATT], w[..., Z_GATE:Z_CONV]], axis=-1)


N_CHIPS = 4
SHARD = {"w_in": ((D, NIN // 4), 1), "w_conv_out": ((CW, D // 4), 1), "w_att_out": ((GW, D // 4), 1),
         "w_mem_kv": ((D // 4, 2 * MW), 0), "w_mem_out": ((MW, D // 4), 1), "w_out": ((D // 4, D), 0),
         "w_ffn_in": ((D, 2 * FH // 4), 1), "w_ffn_out": ((FH // 4, D), 0)}
CDW_ROWS = 64
VEC_ROWS = (("norm_mix_pre", 1), ("b_gate", 3), ("conv_dw_bias", 1), ("conv_ln_g", 1), ("conv_ln_b", 1),
            ("norm_mem", 1), ("norm_mix_post", 1), ("norm_ffn_pre", 1), ("norm_ffn_post", 1))
VEC_LROWS = sum(r for _, r in VEC_ROWS)
REL_ROW = DEPTH * VEC_LROWS
CDW_ROW = REL_ROW + 1
CDW_GROWS = DEPTH * KSIZE * CW // D
LOSS_ROW = CDW_ROW + CDW_GROWS
SMALL_ROWS = -(-(LOSS_ROW + 1) // 8) * 8


def _mesh_pos():
    return lax.axis_index("x"), lax.axis_index("y"), lax.axis_index("c")


def _other_chips(x, y):
    chips = [(1 - x, y), (x, 1 - y), (1 - x, 1 - y)]
    return chips, [2 * cx + cy for cx, cy in chips]


ANY_SPEC = pl.BlockSpec(memory_space=pl.ANY)


def _remote(src, dst, send_sems, recv_sems, k, to):
    return pltpu.make_async_remote_copy(src_ref=src, dst_ref=dst, send_sem=send_sems.at[k], recv_sem=recv_sems.at[k],
                                        device_id=to, device_id_type=MESH)


def _half(ref, c):
    h = ref.shape[0] // 2
    return ref.at[pl.ds(c * h if isinstance(c, int) else pl.multiple_of(c * h, 16), h)]


def _all_gather(ws, cdw):
    nw = len(ws)

    def body(*refs):
        w_refs, cdw_ref = refs[:nw], refs[nw]
        g_refs, gc_ref = refs[nw + 1:2 * nw + 1], refs[2 * nw + 1]
        send_sems, recv_sems = refs[2 * nw + 2:]
        x, y, c = _mesh_pos()
        j = 2 * x + y
        sibling = (x, y, 1 - c)
        chips, blocks = _other_chips(x, y)
        copy = functools.partial(_remote, send_sems=send_sems, recv_sems=recv_sems)
        pairs = list(zip(w_refs, g_refs))
        first = [copy(_half(w, c), _half(g.at[j], c), k=k * nw + n, to=(*chip, c))
                 for k, chip in enumerate(chips) for n, (w, g) in enumerate(pairs)]
        first += [copy(cdw_ref, gc_ref.at[j], k=6 * nw + k, to=(*chip, c)) for k, chip in enumerate(chips)]
        for cp in first:
            cp.start()
        passed = []
        for k, b in enumerate(blocks):
            for n, (w, g) in enumerate(pairs):
                copy(_half(w, c), _half(g.at[b], c), k=k * nw + n, to=sibling).wait_recv()
            onward = [copy(_half(g.at[b], c), _half(g.at[b], c), k=(3 + k) * nw + n, to=sibling)
                      for n, (w, g) in enumerate(pairs)]
            for cp in onward:
                cp.start()
            passed += onward
        for k, b in enumerate(blocks):
            for n, (w, g) in enumerate(pairs):
                copy(_half(w, c), _half(g.at[b], 1 - c), k=(3 + k) * nw + n, to=sibling).wait_recv()
            copy(cdw_ref, gc_ref.at[b], k=6 * nw + k, to=sibling).wait_recv()
        for cp in first + passed:
            cp.wait_send()

    nsem = 6 * nw + 3
    return pl.pallas_call(
        body, name="all_gather_weights",
        out_shape=[jax.ShapeDtypeStruct((N_CHIPS,) + w.shape, BF16) for w in ws]
        + [jax.ShapeDtypeStruct((N_CHIPS, CDW_ROWS, 128), F32)],
        in_specs=[ANY_SPEC] * (nw + 1), out_specs=[ANY_SPEC] * (nw + 1),
        scratch_shapes=[pltpu.SemaphoreType.DMA((nsem,)), pltpu.SemaphoreType.DMA((nsem,))],
    )(*ws, cdw)


SEM_SPEC = pl.BlockSpec(memory_space=pltpu.SEMAPHORE)
DATAFLOW = pltpu.SideEffectType.DATAFLOW_SIDE_EFFECTING


def _gather_copies(w_refs, g_refs, send_sem, recv_sem):
    x, y, c = _mesh_pos()
    j = 2 * x + y
    chips, _ = _other_chips(x, y)
    return [pltpu.make_async_remote_copy(src_ref=_half(w, c), dst_ref=_half(g.at[j], c), send_sem=send_sem,
                                         recv_sem=recv_sem, device_id=(*chip, cc), device_id_type=MESH)
            for chip in chips for cc in (0, 1) for w, g in zip(w_refs, g_refs)]


def _all_gather_start(ws, after, name):
    nw = len(ws)

    def body(*refs):
        w_refs, g_refs = refs[:nw], refs[nw:2 * nw]
        send_sem, recv_sem = refs[2 * nw + 1:2 * nw + 3]
        token = refs[-1]
        for cp in _gather_copies(w_refs, g_refs, send_sem, recv_sem):
            cp.start()
        token[...] = jnp.zeros_like(token)

    lands = [pltpu.with_memory_space_constraint(lax.empty((N_CHIPS,) + w.shape, BF16), pltpu.HBM) for w in ws]
    ws = [pltpu.with_memory_space_constraint(w, pltpu.HBM) for w in ws]
    hbm = pl.BlockSpec(memory_space=pltpu.HBM)
    out = pl.pallas_call(
        body, name=name,
        out_shape=[pltpu.SemaphoreType.DMA(()), pltpu.SemaphoreType.DMA(())]
        + [pltpu.HBM(w.shape, BF16) for w in ws] + [pltpu.HBM(g.shape, BF16) for g in lands]
        + [jax.ShapeDtypeStruct((8, 128), F32)],
        in_specs=[hbm] * (2 * nw) + [ANY_SPEC],
        out_specs=[SEM_SPEC, SEM_SPEC] + [hbm] * (2 * nw) + [pl.BlockSpec(memory_space=pltpu.VMEM)],
        input_output_aliases={n: 2 + n for n in range(2 * nw)},
        compiler_params=pltpu.CompilerParams(has_side_effects=DATAFLOW),
    )(*ws, *lands, after)
    return out[0], out[1], out[2:2 + nw], out[2 + nw:2 + 2 * nw], out[-1]


def _all_gather_wait(send_sem, recv_sem, ws, lands, after, name):
    nw = len(ws)

    def body(*refs):
        w_refs, g_refs = refs[:nw], refs[nw:2 * nw]
        send_sem, recv_sem = refs[2 * nw:2 * nw + 2]
        x, y, c = _mesh_pos()
        _, blocks = _other_chips(x, y)
        for cp in _gather_copies(w_refs, g_refs, send_sem, recv_sem):
            cp.wait_send()
        for b in blocks:
            for cc in (0, 1):
                for w, g in zip(w_refs, g_refs):
                    pltpu.make_async_remote_copy(src_ref=_half(w, cc), dst_ref=_half(g.at[b], cc), send_sem=send_sem,
                                                 recv_sem=recv_sem, device_id=(x, y, c),
                                                 device_id_type=MESH).wait_recv()

    hbm = pl.BlockSpec(memory_space=pltpu.HBM)
    out = pl.pallas_call(
        body, name=name,
        out_shape=[pltpu.HBM(w.shape, BF16) for w in ws] + [pltpu.HBM(g.shape, BF16) for g in lands],
        in_specs=[hbm] * (2 * nw) + [SEM_SPEC, SEM_SPEC, ANY_SPEC],
        out_specs=[hbm] * (2 * nw),
        input_output_aliases={n: n for n in range(2 * nw)},
        compiler_params=pltpu.CompilerParams(has_side_effects=DATAFLOW),
    )(*ws, *lands, send_sem, recv_sem, after)
    return out[:nw], out[nw:]


def _half_rows(ref, c):
    h = ref.shape[1] // 2
    return ref.at[:, pl.ds(pl.multiple_of(c * h, 16), h)]


def _sibling_exchange(ps):
    nw = len(ps)

    def body(*refs):
        p_refs, r_refs, (send_sems, recv_sems) = refs[:nw], refs[nw:2 * nw], refs[2 * nw:]
        x, y, c = _mesh_pos()
        cps = [_remote(_half_rows(p, 1 - c), r, send_sems, recv_sems, n, (x, y, 1 - c))
               for n, (p, r) in enumerate(zip(p_refs, r_refs))]
        for cp in cps:
            cp.start()
        for cp in cps:
            cp.wait()

    return pl.pallas_call(
        body, name="grad_sibling_exchange",
        out_shape=[jax.ShapeDtypeStruct((N_CHIPS, p.shape[1] // 2, p.shape[2]), p.dtype) for p in ps],
        in_specs=[ANY_SPEC] * nw, out_specs=[ANY_SPEC] * nw,
        scratch_shapes=[pltpu.SemaphoreType.DMA((nw,)), pltpu.SemaphoreType.DMA((nw,))],
    )(*ps)


SUM_BLOCK_BYTES = 2 * 1024 * 1024


def _sum_rows(s0, s1):
    return s0 if s0 * s1 * 2 <= SUM_BLOCK_BYTES else s0 // 2


def _add_own_half(where, p, r, name):
    _, h, s1 = r.shape
    T = _sum_rows(h, s1)
    nt = h // T

    def body(where_ref, p_ref, r_ref, o_ref):
        o_ref[...] = (p_ref[...].astype(F32) + r_ref[...].astype(F32)).astype(BF16)

    return pl.pallas_call(
        body, name=name,
        grid_spec=pltpu.PrefetchScalarGridSpec(
            num_scalar_prefetch=1, grid=(N_CHIPS, nt),
            in_specs=[pl.BlockSpec((1, T, s1), lambda j, i, wh: (j, wh[0] * nt + i, 0)),
                      pl.BlockSpec((1, T, s1), lambda j, i, wh: (j, i, 0))],
            out_specs=pl.BlockSpec((1, T, s1), lambda j, i, wh: (j, i, 0))),
        out_shape=jax.ShapeDtypeStruct(r.shape, BF16), compiler_params=_params(("parallel", "parallel")),
    )(where, p, r)


def _chip_exchange(as_):
    nw = len(as_)

    def body(*refs):
        a_refs, r_refs, (send_sems, recv_sems) = refs[:nw], refs[nw:2 * nw], refs[2 * nw:]
        x, y, c = _mesh_pos()
        chips, blocks = _other_chips(x, y)
        cps = [_remote(a.at[b], r.at[k], send_sems, recv_sems, k * nw + n, (*chip, c))
               for k, (chip, b) in enumerate(zip(chips, blocks)) for n, (a, r) in enumerate(zip(a_refs, r_refs))]
        for cp in cps:
            cp.start()
        for cp in cps:
            cp.wait_recv()
        for cp in cps:
            cp.wait_send()

    return pl.pallas_call(
        body, name="grad_chip_exchange", out_shape=[jax.ShapeDtypeStruct((3,) + a.shape[1:], a.dtype) for a in as_],
        in_specs=[ANY_SPEC] * nw, out_specs=[ANY_SPEC] * nw,
        scratch_shapes=[pltpu.SemaphoreType.DMA((3 * nw,)), pltpu.SemaphoreType.DMA((3 * nw,))],
    )(*as_)


def _sum_chips(where, a, r, o, name):
    _, h, s1 = a.shape
    T = _sum_rows(h, s1)
    nt = h // T

    def body(where_ref, a_ref, r_ref, o_in, o_ref):
        acc = a_ref[0].astype(F32)
        for k in range(3):
            acc = acc + r_ref[k].astype(F32)
        o_ref[0] = acc

    return pl.pallas_call(
        body, name=name,
        grid_spec=pltpu.PrefetchScalarGridSpec(
            num_scalar_prefetch=1, grid=(nt,),
            in_specs=[pl.BlockSpec((1, T, s1), lambda i, wh: (wh[1], i, 0)),
                      pl.BlockSpec((3, T, s1), lambda i, wh: (0, i, 0)), ANY_SPEC],
            out_specs=pl.BlockSpec((1, T, s1), lambda i, wh: (0, wh[0] * nt + i, 0))),
        out_shape=jax.ShapeDtypeStruct(o.shape, F32), input_output_aliases={3: 0},
        compiler_params=_params(("parallel",)),
    )(where, a, r, o)


def _sibling_share(os_):
    nw = len(os_)

    def body(*refs):
        o_refs, (send_sems, recv_sems) = refs[nw:2 * nw], refs[2 * nw:]
        x, y, c = _mesh_pos()
        mine = lambda o, cc: _half(o.at[0], cc)
        cps = [_remote(mine(o, c), mine(o, c), send_sems, recv_sems, n, (x, y, 1 - c)) for n, o in enumerate(o_refs)]
        for cp in cps:
            cp.start()
        for n, o in enumerate(o_refs):
            _remote(mine(o, c), mine(o, 1 - c), send_sems, recv_sems, n, (x, y, 1 - c)).wait_recv()
        for cp in cps:
            cp.wait_send()

    return pl.pallas_call(
        body, name="grad_sibling_share", out_shape=[jax.ShapeDtypeStruct(o.shape, o.dtype) for o in os_],
        in_specs=[ANY_SPEC] * nw, out_specs=[ANY_SPEC] * nw,
        input_output_aliases={n: n for n in range(nw)},
        scratch_shapes=[pltpu.SemaphoreType.DMA((nw,)), pltpu.SemaphoreType.DMA((nw,))],
    )(*os_)


N_DEV = 8


def _scatter_copies(p_refs, r_refs, send_sem, recv_sem):
    x, y, c = _mesh_pos()
    cps = []
    for m in range(1, N_DEV):
        px, py, pc = x ^ (m >> 2 & 1), y ^ (m >> 1 & 1), c ^ (m & 1)
        for p, r in zip(p_refs, r_refs):
            cps.append(pltpu.make_async_remote_copy(src_ref=p.at[2 * px + py], dst_ref=r.at[m - 1], send_sem=send_sem,
                                                    recv_sem=recv_sem, device_id=(px, py, pc), device_id_type=MESH))
    return cps


def _reduce_start(ps, after, name):
    nw = len(ps)

    def body(*refs):
        p_refs, r_refs = refs[:nw], refs[nw:2 * nw]
        send_sem, recv_sem = refs[2 * nw + 1:2 * nw + 3]
        for cp in _scatter_copies(p_refs, r_refs, send_sem, recv_sem):
            cp.start()
        refs[-1][...] = jnp.zeros_like(refs[-1])

    lands = [pltpu.with_memory_space_constraint(lax.empty((N_DEV - 1,) + p.shape[1:], BF16), pltpu.HBM) for p in ps]
    ps = [pltpu.with_memory_space_constraint(p, pltpu.HBM) for p in ps]
    hbm = pl.BlockSpec(memory_space=pltpu.HBM)
    out = pl.pallas_call(
        body, name=name,
        out_shape=[pltpu.SemaphoreType.DMA(()), pltpu.SemaphoreType.DMA(())]
        + [pltpu.HBM(p.shape, BF16) for p in ps] + [pltpu.HBM(r.shape, BF16) for r in lands]
        + [jax.ShapeDtypeStruct((8, 128), F32)],
        in_specs=[hbm] * (2 * nw) + [ANY_SPEC],
        out_specs=[SEM_SPEC, SEM_SPEC] + [hbm] * (2 * nw) + [pl.BlockSpec(memory_space=pltpu.VMEM)],
        input_output_aliases={n: 2 + n for n in range(2 * nw)},
        compiler_params=pltpu.CompilerParams(has_side_effects=DATAFLOW),
    )(*ps, *lands, after)
    return out[0], out[1], out[2:2 + nw], out[2 + nw:2 + 2 * nw], out[-1]


def _reduce_wait(send_sem, recv_sem, ps, lands, after, name):
    nw = len(ps)

    def body(*refs):
        p_refs, r_refs = refs[:nw], refs[nw:2 * nw]
        send_sem, recv_sem = refs[2 * nw:2 * nw + 2]
        x, y, c = _mesh_pos()
        for cp in _scatter_copies(p_refs, r_refs, send_sem, recv_sem):
            cp.wait_send()
        for m in range(1, N_DEV):
            for p, r in zip(p_refs, r_refs):
                pltpu.make_async_remote_copy(src_ref=p.at[0], dst_ref=r.at[m - 1], send_sem=send_sem,
                                             recv_sem=recv_sem, device_id=(x, y, c), device_id_type=MESH).wait_recv()

    hbm = pl.BlockSpec(memory_space=pltpu.HBM)
    out = pl.pallas_call(
        body, name=name,
        out_shape=[pltpu.HBM(p.shape, BF16) for p in ps] + [pltpu.HBM(r.shape, BF16) for r in lands],
        in_specs=[hbm] * (2 * nw) + [SEM_SPEC, SEM_SPEC, ANY_SPEC],
        out_specs=[hbm] * (2 * nw),
        input_output_aliases={n: n for n in range(2 * nw)},
        compiler_params=pltpu.CompilerParams(has_side_effects=DATAFLOW),
    )(*ps, *lands, send_sem, recv_sem, after)
    return out[:nw], out[nw:]


SUM8_BLOCK_BYTES = 6 * 1024 * 1024


def _sum_devices(where, p, r, layer, o, name):
    _, s0, s1 = p.shape
    T = s0
    while (N_DEV - 1) * T * s1 * 2 > SUM8_BLOCK_BYTES:
        T //= 2

    def body(where_ref, p_ref, r_ref, *rest):
        me = 2 * where_ref[1] + where_ref[0]
        acc = None
        for dev in range(N_DEV):
            m = dev ^ me
            val = jnp.where(m == 0, p_ref[0], r_ref[jnp.maximum(m, 1) - 1]).astype(F32)
            acc = val if acc is None else acc + val
        rest[-1][0] = acc

    given = o is not None
    return pl.pallas_call(
        body, name=name,
        grid_spec=pltpu.PrefetchScalarGridSpec(
            num_scalar_prefetch=1, grid=(s0 // T,),
            in_specs=[pl.BlockSpec((1, T, s1), lambda i, wh: (wh[1], i, 0)),
                      pl.BlockSpec((N_DEV - 1, T, s1), lambda i, wh: (0, i, 0))] + [ANY_SPEC] * given,
            out_specs=pl.BlockSpec((1, T, s1), lambda i, wh: (layer, i, 0))),
        out_shape=jax.ShapeDtypeStruct((DEPTH, s0, s1), F32), input_output_aliases={3: 0} if given else {},
        compiler_params=_params(("parallel",)),
    )(where, p, r, *([o] if given else []))


def _all_reduce_small(sp, after):
    def body(sp_ref, after_ref, out_ref, buf, send_sems, recv_sems):
        x, y, c = _mesh_pos()
        me = 4 * x + 2 * y + c
        buf[0] = sp_ref[...]
        cps = []
        for k in range(1, 8):
            peer = (x ^ (k >> 2 & 1), y ^ (k >> 1 & 1), c ^ (k & 1))
            cps.append(pltpu.make_async_remote_copy(src_ref=sp_ref, dst_ref=buf.at[k], send_sem=send_sems.at[k - 1],
                                                    recv_sem=recv_sems.at[k - 1], device_id=peer, device_id_type=MESH))
        for cp in cps:
            cp.start()
        for cp in cps:
            cp.wait_recv()
        for cp in cps:
            cp.wait_send()
        acc = buf[me]
        for p in range(1, 8):
            acc = acc + buf[p ^ me]
        out_ref[...] = acc

    vm = pl.BlockSpec(memory_space=pltpu.VMEM)
    return pl.pallas_call(
        body, name="all_reduce_small", out_shape=jax.ShapeDtypeStruct(sp.shape, F32),
        in_specs=[vm, ANY_SPEC], out_specs=vm,
        scratch_shapes=[pltpu.VMEM((8,) + sp.shape, F32), pltpu.SemaphoreType.DMA((7,)), pltpu.SemaphoreType.DMA((7,))],
        compiler_params=_params(),
    )(sp, after)


def _adamw(w, g, m, v, name):
    R, C = w.shape
    T = next((t for t in (256, 128) if R % t == 0), R)

    def body(w_ref, g_ref, m_ref, v_ref, d_ref, m2_ref, v2_ref):
        gv = g_ref[...]
        m2 = ADAM_B1 * m_ref[...] + (1.0 - ADAM_B1) * gv
        v2 = ADAM_B2 * v_ref[...] + (1.0 - ADAM_B2) * (gv * gv)
        m_hat = m2 / (1.0 - ADAM_B1 ** ADAM_STEP)
        v_hat = v2 / (1.0 - ADAM_B2 ** ADAM_STEP)
        d_ref[...] = -ADAM_LR * (m_hat / (jnp.sqrt(v_hat) + ADAM_EPS) + ADAM_WD * w_ref[...])
        m2_ref[...] = m2
        v2_ref[...] = v2

    blk = pl.BlockSpec((T, C), _row)
    return pl.pallas_call(
        body, name=name, grid=(R // T,), in_specs=[blk] * 4, out_specs=[blk] * 3,
        out_shape=[jax.ShapeDtypeStruct((R, C), F32)] * 3, compiler_params=_params(("parallel",)),
    )(w, g, m, v)


def _pack_vectors(get, rel, cdw, name, loss=None):
    rows = []
    for l in range(DEPTH):
        for n, r in VEC_ROWS:
            v = get(n)[l]
            rows.append(jnp.pad(v, (0, r * D - v.shape[0])).reshape(r, D))
    rows.append(jnp.pad(rel.reshape(-1), (0, D - NUM_BUCKETS * 3 * HPG)).reshape(1, D))
    rows.append(cdw.reshape(CDW_GROWS, D))
    if loss is not None:
        rows.append(jnp.full((1, D), loss, F32))

    def body(*refs):
        out_ref = refs[-1]
        out_ref[...] = jnp.zeros_like(out_ref)
        at = 0
        for ref in refs[:-1]:
            out_ref[at:at + ref.shape[0], :] = ref[...]
            at += ref.shape[0]

    return pl.pallas_call(body, name=name, out_shape=jax.ShapeDtypeStruct((SMALL_ROWS, D), F32),
                          compiler_params=_params())(*rows)


def _unpack_vectors(packed, lens):
    out = {n: [] for n, _ in VEC_ROWS}
    for l in range(DEPTH):
        at = l * VEC_LROWS
        for n, r in VEC_ROWS:
            out[n].append(packed[at:at + r].reshape(-1)[:lens[n]])
            at += r
    rel = packed[REL_ROW, :NUM_BUCKETS * 3 * HPG].reshape(NUM_BUCKETS, 3 * HPG)
    return {n: jnp.stack(v) for n, v in out.items()}, rel


INPUT_NAMES = ("x", "mem") + ("rel_bias", "norm_mix_pre", "w_in", "b_gate", "conv_dw", "conv_dw_bias", "conv_ln_g",
                              "conv_ln_b", "w_conv_out", "w_att_out", "norm_mem", "w_mem_kv", "w_mem_out", "w_out",
                              "norm_mix_post", "norm_ffn_pre", "w_ffn_in", "w_ffn_out", "norm_ffn_post")
WEIGHT_NAMES = INPUT_NAMES[2:]


def kernel(*args):
    nw = len(WEIGHT_NAMES)
    a = dict(zip(INPUT_NAMES, args[:2 + nw]))
    target = args[2 + nw]
    mom = dict(zip(WEIGHT_NAMES, args[3 + nw:3 + 2 * nw]))
    var = dict(zip(WEIGHT_NAMES, args[3 + 2 * nw:3 + 3 * nw]))
    xi, yi, ci = _mesh_pos()
    chip = 2 * xi + yi
    where = jnp.stack([ci, chip]).astype(I32)

    head, tail = ("w_in",), tuple(n for n in BIG if n != "w_in")
    shard = lambda l, names: [a[n][l].astype(BF16) for n in names]
    cdw = jnp.pad(a["conv_dw"].reshape(DEPTH * KSIZE, CW // 4), ((0, CDW_ROWS - DEPTH * KSIZE), (0, 0)))
    own_head0 = shard(0, head)
    *got_head0, gcdw = _all_gather(own_head0, cdw)
    gcdw = lax.dynamic_update_slice(gcdw, cdw[None], (chip, 0, 0))
    conv_dw = gcdw[:, :DEPTH * KSIZE].reshape(N_CHIPS, DEPTH, KSIZE, CW // 4).transpose(1, 2, 0, 3)
    conv_dw = jnp.pad(conv_dw.reshape(DEPTH, KSIZE, CW), ((0, 0), (0, 1), (0, 0)))
    gmix = [a["norm_mix_pre"][l][None, :] for l in range(DEPTH)]

    def whole(names, gathered, own):
        w = {}
        for n, g, s in zip(names, gathered, own):
            (s0, s1), axis = SHARD[n]
            blk = lax.dynamic_update_slice(g, s[None], (chip, 0, 0))
            w[n] = blk.reshape(N_CHIPS * s0, s1) if axis == 0 else blk.transpose(1, 0, 2).reshape(s0, N_CHIPS * s1)
        if "w_in" in w:
            w["w_in"] = _z_cols_from_ref(w["w_in"])
        return w

    def vectors(l):
        w = {n: a[n][l][None, :] for n, _ in VEC_ROWS}
        w["conv_dw"] = conv_dw[l]
        return w

    def landed(flight, after, name):
        send_sem, recv_sem, thru, lands, _ = flight
        return _all_gather_wait(send_sem, recv_sem, thru, lands, after, name)

    flights = {"tail0": _all_gather_start(shard(0, tail), got_head0[0], "all_gather_start_tail0")}
    gmix[0] = gmix[0] + flights["tail0"][4][0, 0]

    def layer0(x):
        def rest(after):
            own, got = landed(flights["tail0"], after, "all_gather_wait_tail0")
            flights["head1"] = _all_gather_start(shard(1, head), got[0], "all_gather_start_head1")
            w = whole(tail, got, own)
            w["b_gate"] = a["b_gate"][0][None, :] + flights["head1"][4][0, 0]
            return w

        return {**vectors(0), **whole(head, got_head0, own_head0)}, rest

    def layer1(x):
        own, got = landed(flights["head1"], x, "all_gather_wait_head1")
        flights["tail1"] = _all_gather_start(shard(1, tail), got[0], "all_gather_start_tail1")
        first = {**vectors(1), **whole(head, got, own)}
        first["conv_dw_bias"] = first["conv_dw_bias"] + flights["tail1"][4][0, 0]

        def rest(after):
            own_t, got_t = landed(flights["tail1"], after, "all_gather_wait_tail1")
            return whole(tail, got_t, own_t)

        return first, rest

    def by_chip(layer_grads, names):
        out = []
        for n in names:
            (s0, s1), axis = SHARD[n]
            g = _ref_cols_from_z(layer_grads[n]) if n == "w_in" else layer_grads[n]
            g = g.reshape(N_CHIPS, s0, s1) if axis == 0 else g.reshape(s0, N_CHIPS, s1).transpose(1, 0, 2)
            out.append(g.astype(BF16))
        return out

    early = ("w_ffn_in", "w_ffn_out")
    last = ("w_in",)
    late = tuple(n for n in BIG if n not in early + last)
    scattering = {}

    def on_grads(l, layer_grads):
        scattering["l1"] = _reduce_start(by_chip(layer_grads, BIG), layer_grads["w_in"], "grad_reduce_start_l1")
        return scattering["l1"][4]

    def on_last_ffn_grads(layer_grads):
        scattering["ffn"] = _reduce_start(by_chip(layer_grads, early), layer_grads["w_ffn_in"], "grad_reduce_start_ffn")
        return scattering["ffn"][4]

    def on_last_w_in_grad(layer_grads):
        scattering["w_in"] = _reduce_start(by_chip(layer_grads, last), layer_grads["w_in"], "grad_reduce_start_w_in")
        return scattering["w_in"][4]

    loss_part, gx, grads, drel = _local_step(a["x"][0], a["mem"][0], target[0], a["rel_bias"], [layer0, layer1], gmix,
                                             on_grads, on_last_ffn_grads, on_last_w_in_grad)

    reduced = {}
    send_sem, recv_sem, thru, lands, _ = scattering["l1"]
    sent, arrived = _reduce_wait(send_sem, recv_sem, thru, lands, gx, "grad_reduce_wait_l1")
    for n, p, r in zip(BIG, sent, arrived):
        reduced[n] = _sum_devices(where, p, r, 1, None, "grad_sum_devices_l1_" + n)
    for key, names in (("ffn", early), ("w_in", last)):
        send_sem, recv_sem, thru, lands, _ = scattering[key]
        sent, arrived = _reduce_wait(send_sem, recv_sem, thru, lands, gx, "grad_reduce_wait_" + key)
        for n, p, r in zip(names, sent, arrived):
            reduced[n] = _sum_devices(where, p, r, 0, reduced[n], "grad_sum_devices_l0_" + n)
    packed = by_chip(grads[0], late)
    from_sibling = _sibling_exchange(packed)
    chip_sums = [_add_own_half(where, p, r, "grad_add_sibling_" + n) for n, p, r in zip(late, packed, from_sibling)]
    from_chips = _chip_exchange(chip_sums)
    shared = _sibling_share([_sum_chips(where, s, r, reduced[n], "grad_sum_chips_" + n)
                             for n, s, r in zip(late, chip_sums, from_chips)])
    reduced.update(zip(late, shared))
    reduced = [reduced[n] for n in BIG]

    gvec = _all_reduce_small(_pack_vectors(
        lambda n: jnp.stack([grads[l][n][0] for l in range(DEPTH)]), drel[:, :3 * HPG],
        jnp.stack([grads[l]["conv_dw"] for l in range(DEPTH)]), "pack_vector_grads", loss_part), reduced[0])
    loss = gvec[LOSS_ROW, 0]
    lens = {n: a[n].shape[1] for n, _ in VEC_ROWS}
    g_vec, g_rel = _unpack_vectors(gvec, lens)
    g_cdw = lax.dynamic_slice_in_dim(gvec[CDW_ROW:CDW_ROW + CDW_GROWS].reshape(DEPTH, KSIZE, CW), chip * (CW // 4),
                                     CW // 4, axis=2)

    grad, delta, new_m, new_v = {}, {}, {}, {}
    for n, g in zip(BIG, reduced):
        shape = a[n].shape
        flat2 = lambda t: t.reshape(shape[0] * shape[1], shape[2])
        d, m2, v2 = _adamw(flat2(a[n]), flat2(g), flat2(mom[n]), flat2(var[n]), "adamw_" + n)
        grad[n], delta[n], new_m[n], new_v[n] = g, d.reshape(shape), m2.reshape(shape), v2.reshape(shape)
    shape = a["conv_dw"].shape
    flat2 = lambda t: t.reshape(shape[0] * shape[1], shape[2])
    d, m2, v2 = _adamw(flat2(a["conv_dw"]), flat2(g_cdw), flat2(mom["conv_dw"]), flat2(var["conv_dw"]), "adamw_conv_dw")
    grad["conv_dw"], delta["conv_dw"], new_m["conv_dw"], new_v["conv_dw"] = (
        g_cdw, d.reshape(shape), m2.reshape(shape), v2.reshape(shape))
    zero_cdw = jnp.zeros((DEPTH, KSIZE, CW), F32)
    pk = lambda src, name: _pack_vectors(lambda n: src[n], src["rel_bias"], zero_cdw, name)
    d, m2, v2 = _adamw(pk(a, "pack_vector_w"), gvec, pk(mom, "pack_vector_m"), pk(var, "pack_vector_v"),
                       "adamw_vectors")
    for src, dst in ((d, delta), (m2, new_m), (v2, new_v)):
        vec, rel = _unpack_vectors(src, lens)
        dst.update(vec)
        dst["rel_bias"] = rel
    grad.update(g_vec)
    grad["rel_bias"] = g_rel

    outs = [loss, gx[None]]
    for group in (grad, delta, new_m, new_v):
        outs += [group[n] for n in WEIGHT_NAMES]
    return tuple(outs)
```

```python
import functools
import math

import jax
import jax.numpy as jnp
from jax import lax
from jax.experimental import pallas as pl
from jax.experimental.pallas import tpu as pltpu

F32 = jnp.float32
BF16 = jnp.bfloat16
I32 = jnp.int32

D = 1024
DEPTH = 2
N_MEM = 256
CW = 512
KSIZE = 31
PAD = KSIZE // 2
DILS = (1, 4, 16)
RADIUS = 64
HPG = 4
HD = 64
GW = HPG * HD
MH = 4
MHD = 128
MW = MH * MHD
FH = 2816
NIN = 6912
C1 = 2 * CW
R_ATT = C1
R_MEM = R_ATT + 9 * GW
R_GATE = R_MEM + MW
Z_GATE = 0
Z_CONV = 3 * D
Z_MEM = Z_CONV + C1
Z_ATT = Z_MEM + MW
NUM_BUCKETS = 32
MAX_DISTANCE = 1024
RMS_EPS = 1e-6
LN_EPS = 1e-5
NEG_INF = -1e30
ATT_SCALE = HD ** -0.5
MEM_SCALE = MHD ** -0.5

ADAM_LR = 0.001
ADAM_B1 = 0.9
ADAM_B2 = 0.999
ADAM_EPS = 1e-08
ADAM_WD = 0.01
ADAM_STEP = 10

VMEM_LIMIT_BYTES = 56 * 1024 * 1024
ATT_QB = 128
ATT_TB = 16 * ATT_QB

MESH = pl.DeviceIdType.MESH


def _params(sem=None):
    return pltpu.CompilerParams(dimension_semantics=sem, vmem_limit_bytes=VMEM_LIMIT_BYTES)


def _sigmoid(v):
    return 1.0 / (1.0 + jnp.exp(-v))


def _dot(a, b):
    return jnp.dot(a, b, preferred_element_type=F32)


def _dot_nt(a, b):
    return lax.dot_general(a, b, (((1,), (1,)), ((), ())), preferred_element_type=F32)


def _dot_tn(a, b):
    return lax.dot_general(a, b, (((0,), (0,)), ((), ())), preferred_element_type=F32)


def _rms_fwd_val(v, g):
    r = lax.rsqrt(jnp.mean(v * v, axis=-1, keepdims=True) + RMS_EPS)
    return v * r * g


def _rms_bwd_val(v, g, dy):
    r = lax.rsqrt(jnp.mean(v * v, axis=-1, keepdims=True) + RMS_EPS)
    vh = v * r
    dvh = dy * g
    dv = r * (dvh - vh * jnp.mean(dvh * vh, axis=-1, keepdims=True))
    return dv, dy * vh


def _row(i):
    return (i, 0)


def _fixed(*_):
    return (0, 0)


def _mm_nn(a, b, tm, out_dtype, name):
    M, K = a.shape
    N = b.shape[1]

    def body(a_ref, b_ref, o_ref):
        o_ref[...] = _dot(a_ref[...], b_ref[...]).astype(out_dtype)

    return pl.pallas_call(
        body, name=name, grid=(M // tm,),
        in_specs=[pl.BlockSpec((tm, K), _row), pl.BlockSpec((K, N), _fixed, pipeline_mode=pl.Buffered(1))],
        out_specs=pl.BlockSpec((tm, N), _row),
        out_shape=jax.ShapeDtypeStruct((M, N), out_dtype),
        compiler_params=_params(("parallel",)),
    )(a, b)


def _mm_nt_rms_bwd(a, b, x, g, dres, tm, name):
    M, N = a.shape

    def body(a_ref, b_ref, x_ref, g_ref, dres_ref, dx_ref, dg_ref):
        @pl.when(pl.program_id(0) == 0)
        def _():
            dg_ref[...] = jnp.zeros_like(dg_ref)

        dv, dgr = _rms_bwd_val(x_ref[...], g_ref[...], _dot_nt(a_ref[...], b_ref[...]))
        dx_ref[...] = dres_ref[...] + dv
        dg_ref[...] += jnp.sum(dgr, axis=0, keepdims=True)

    rows = pl.BlockSpec((tm, D), _row)
    return pl.pallas_call(
        body, name=name, grid=(M // tm,),
        in_specs=[pl.BlockSpec((tm, N), _row), pl.BlockSpec((D, N), _fixed, pipeline_mode=pl.Buffered(1)), rows,
                  pl.BlockSpec((1, D), _fixed), rows],
        out_specs=[rows, pl.BlockSpec((1, D), _fixed)],
        out_shape=[jax.ShapeDtypeStruct((M, D), F32), jax.ShapeDtypeStruct((1, D), F32)],
        compiler_params=_params(("arbitrary",)),
    )(a, b, x, g, dres)


def _mm_tn(a, b, ts, tn, name):
    S, K = a.shape
    N = b.shape[1]

    def body(a_ref, b_ref, o_ref):
        @pl.when(pl.program_id(1) == 0)
        def _():
            o_ref[...] = jnp.zeros_like(o_ref)

        o_ref[...] += _dot_tn(a_ref[...], b_ref[...])

    return pl.pallas_call(
        body, name=name, grid=(N // tn, S // ts),
        in_specs=[pl.BlockSpec((ts, K), lambda j, s: (s, 0)), pl.BlockSpec((ts, tn), lambda j, s: (s, j))],
        out_specs=pl.BlockSpec((K, tn), lambda j, s: (0, j)),
        out_shape=jax.ShapeDtypeStruct((K, N), F32),
        compiler_params=_params(("parallel", "arbitrary")),
    )(a, b)


def _rms_h(x, g, name):
    S = x.shape[0]
    T = 512

    def body(x_ref, g_ref, h_ref):
        h_ref[...] = _rms_fwd_val(x_ref[...], g_ref[...]).astype(BF16)

    return pl.pallas_call(
        body, name=name, grid=(S // T,),
        in_specs=[pl.BlockSpec((T, D), _row), pl.BlockSpec((1, D), _fixed)],
        out_specs=pl.BlockSpec((T, D), _row),
        out_shape=jax.ShapeDtypeStruct((S, D), BF16),
        compiler_params=_params(("parallel",)),
    )(x, g)


CONV_T = 256
CONV_HALO = 16
CONV_RC = 32


def _halo_specs(T, halo, S, width, col):
    per = T // halo
    last = S // halo - 1
    return [
        pl.BlockSpec((T, width), lambda i: (i, col)),
        pl.BlockSpec((halo, width), lambda i: (jnp.maximum(i * per - 1, 0), col)),
        pl.BlockSpec((halo, width), lambda i: (jnp.minimum((i + 1) * per, last), col)),
    ]


def _glu(zb):
    zb = zb.astype(F32)
    return zb[:, :CW] * _sigmoid(zb[:, CW:])


CONV_EXT = CONV_T + 2 * CONV_HALO
SUBLANES = 8


def _fill_shifted(sh_ref, ext_ref, cur, prev, nxt):
    T, halo = CONV_T, CONV_HALO
    i = pl.program_id(0)
    n = pl.num_programs(0)
    ext_ref[0:halo, :] = jnp.where(i > 0, prev, 0.0)
    ext_ref[halo:halo + T, :] = cur
    ext_ref[halo + T:CONV_EXT, :] = jnp.where(i < n - 1, nxt, 0.0)
    ext_ref[CONV_EXT:CONV_EXT + SUBLANES, :] = jnp.zeros((SUBLANES, CW), F32)
    for b in range(SUBLANES):
        sh_ref[b] = ext_ref[b:b + CONV_EXT, :]


def _window(sh_ref, start, rows):
    b = start % SUBLANES
    return sh_ref[b, start - b:start - b + rows, :]


def _shifted_scratch():
    return [pltpu.VMEM((CONV_EXT + SUBLANES, CW), F32), pltpu.VMEM((SUBLANES, CONV_EXT, CW), F32)]


def _conv_fwd(z, wdw, bdw, lng, lnb, name):
    S = z.shape[0]
    T, HL, RC = CONV_T, CONV_HALO, CONV_RC

    def body(cur_ref, prev_ref, next_ref, w_ref, b_ref, g_ref, bb_ref, yc_ref, act_ref, ext_ref, sh_ref):
        _fill_shifted(sh_ref, ext_ref, _glu(cur_ref[...]), _glu(prev_ref[...]), _glu(next_ref[...]))
        for c in range(T // RC):
            acc = jnp.zeros((RC, CW), F32)
            for k in range(KSIZE):
                acc = acc + w_ref[k:k + 1, :] * _window(sh_ref, c * RC + k + HL - PAD, RC)
            yc = acc + b_ref[...]
            yc_ref[c * RC:(c + 1) * RC, :] = yc
            mu = jnp.mean(yc, axis=-1, keepdims=True)
            xc = yc - mu
            ln = xc * lax.rsqrt(jnp.mean(xc * xc, axis=-1, keepdims=True) + LN_EPS) * g_ref[...] + bb_ref[...]
            act_ref[c * RC:(c + 1) * RC, :] = (ln * _sigmoid(ln)).astype(BF16)

    return pl.pallas_call(
        body, name=name, grid=(S // T,),
        in_specs=_halo_specs(T, HL, S, C1, Z_CONV // C1) + [pl.BlockSpec((32, CW), _fixed)]
        + [pl.BlockSpec((1, CW), _fixed)] * 3,
        out_specs=[pl.BlockSpec((T, CW), _row), pl.BlockSpec((T, CW), _row)],
        out_shape=[jax.ShapeDtypeStruct((S, CW), F32), jax.ShapeDtypeStruct((S, CW), BF16)],
        scratch_shapes=_shifted_scratch(),
        compiler_params=_params(("parallel",)),
    )(z, z, z, wdw, bdw, lng, lnb)


def _conv_bwd_ln(yc, dact, lng, lnb, name):
    S = yc.shape[0]
    T = 512

    def body(yc_ref, da_ref, g_ref, b_ref, dyc_ref, dg_ref, db_ref, dbias_ref):
        yc_v = yc_ref[...]
        mu = jnp.mean(yc_v, axis=-1, keepdims=True)
        xc = yc_v - mu
        r = lax.rsqrt(jnp.mean(xc * xc, axis=-1, keepdims=True) + LN_EPS)
        yn = xc * r
        ln = yn * g_ref[...] + b_ref[...]
        sg = _sigmoid(ln)
        dln = da_ref[...].astype(F32) * (sg * (1.0 + ln * (1.0 - sg)))
        dyn = dln * g_ref[...]
        dyc = r * (dyn - jnp.mean(dyn, axis=-1, keepdims=True) - yn * jnp.mean(dyn * yn, axis=-1, keepdims=True))
        dyc_ref[...] = dyc

        @pl.when(pl.program_id(0) == 0)
        def _():
            dg_ref[...] = jnp.zeros_like(dg_ref)
            db_ref[...] = jnp.zeros_like(db_ref)
            dbias_ref[...] = jnp.zeros_like(dbias_ref)

        dg_ref[...] += jnp.sum(dln * yn, axis=0, keepdims=True)
        db_ref[...] += jnp.sum(dln, axis=0, keepdims=True)
        dbias_ref[...] += jnp.sum(dyc, axis=0, keepdims=True)

    vec = pl.BlockSpec((1, CW), _fixed)
    return pl.pallas_call(
        body, name=name, grid=(S // T,),
        in_specs=[pl.BlockSpec((T, CW), _row), pl.BlockSpec((T, CW), _row), vec, vec],
        out_specs=[pl.BlockSpec((T, CW), _row), vec, vec, vec],
        out_shape=[jax.ShapeDtypeStruct((S, CW), F32)] + [jax.ShapeDtypeStruct((1, CW), F32)] * 3,
        compiler_params=_params(("arbitrary",)),
    )(yc, dact, lng, lnb)


def _conv_bwd_dw(z, dyc, wdw, dz, name):
    S = z.shape[0]
    T, HL, RC = CONV_T, CONV_HALO, CONV_RC

    def body(zc_ref, zp_ref, zn_ref, dc_ref, dp_ref, dn_ref, w_ref, dz_in, dz_ref, dw_ref, uext_ref, ush_ref,
             dext_ref, dsh_ref, dwacc_ref):
        _fill_shifted(ush_ref, uext_ref, _glu(zc_ref[...]), _glu(zp_ref[...]), _glu(zn_ref[...]))
        _fill_shifted(dsh_ref, dext_ref, dc_ref[...], dp_ref[...], dn_ref[...])

        @pl.when(pl.program_id(0) == 0)
        def _():
            dwacc_ref[...] = jnp.zeros_like(dwacc_ref)

        for c in range(T // RC):
            dcur = dc_ref[c * RC:(c + 1) * RC, :]
            du = jnp.zeros((RC, CW), F32)
            for k in range(KSIZE):
                du = du + w_ref[k:k + 1, :] * _window(dsh_ref, c * RC + HL + PAD - k, RC)
                prod = dcur * _window(ush_ref, c * RC + k + HL - PAD, RC)
                dwacc_ref[k] += jnp.sum(prod.reshape(RC // SUBLANES, SUBLANES, CW), axis=0)
            zc = zc_ref[c * RC:(c + 1) * RC, :].astype(F32)
            a, gt = zc[:, :CW], zc[:, CW:]
            sg = _sigmoid(gt)
            dz_ref[c * RC:(c + 1) * RC, 0:CW] = (du * sg).astype(BF16)
            dz_ref[c * RC:(c + 1) * RC, CW:C1] = (du * a * sg * (1.0 - sg)).astype(BF16)

        @pl.when(pl.program_id(0) == pl.num_programs(0) - 1)
        def _():
            dw_ref[...] = jnp.sum(dwacc_ref[...], axis=1)

    return pl.pallas_call(
        body, name=name, grid=(S // T,),
        in_specs=_halo_specs(T, HL, S, C1, Z_CONV // C1) + _halo_specs(T, HL, S, CW, 0)
        + [pl.BlockSpec((32, CW), _fixed), pl.BlockSpec(memory_space=pl.ANY)],
        out_specs=[pl.BlockSpec((T, C1), lambda i: (i, Z_CONV // C1)), pl.BlockSpec((32, CW), _fixed)],
        out_shape=[jax.ShapeDtypeStruct(dz.shape, BF16), jax.ShapeDtypeStruct((32, CW), F32)],
        input_output_aliases={7: 0},
        scratch_shapes=_shifted_scratch() + _shifted_scratch() + [pltpu.VMEM((32, SUBLANES, CW), F32)],
        compiler_params=_params(("arbitrary",)),
    )(z, z, z, dyc, dyc, dyc, wdw, dz)


def _t5_bucket(rel):
    nb = NUM_BUCKETS // 2
    max_exact = nb // 2
    ret = jnp.where(rel > 0, nb, 0)
    n = jnp.abs(rel)
    nf = jnp.maximum(n, 1).astype(F32)
    large = max_exact + (jnp.log(nf / max_exact) / math.log(MAX_DISTANCE / max_exact)
                         * (nb - max_exact)).astype(I32)
    large = jnp.minimum(large, nb - 1)
    return ret + jnp.where(n < max_exact, n, large)


def _offsets_qk(nq, nk, shift):
    return lax.broadcasted_iota(I32, (nq, nk), 1) + shift - lax.broadcasted_iota(I32, (nq, nk), 0)


def _bias_table(bk, rb_ref, col, off):
    acc = jnp.zeros(bk.shape, F32)
    for b in range(NUM_BUCKETS):
        acc = jnp.where(bk == b, rb_ref[b, col], acc)
    return jnp.where(jnp.abs(off) <= RADIUS, acc, NEG_INF)


def _to_halves(scr, row0, val):
    rows = val.shape[0]
    v = val.astype(F32)
    scr[0, row0:row0 + rows, :] = v[:, :128]
    scr[1, row0:row0 + rows, :] = v[:, 128:]


ATT_FWD_GROUP = 2
ATT_BWD_GROUP = 1


def _att_units(d, fn, group):
    nj = ATT_TB // (ATT_QB * d)
    if nj == 1:
        def trip(t, c):
            r0 = pl.multiple_of(t * 8, 8)
            for u in range(0, 8, group):
                fn([(r0 + u + v, 0) for v in range(group)])
            return c

        lax.fori_loop(0, d // 8, trip, 0)
        return
    for r in range(d):
        def step(t, c, r=r):
            fn([(r, t * group + u) for u in range(group)])
            return c

        lax.fori_loop(0, nj // group, step, 0)


def _unit_row(r, j, d):
    if isinstance(j, int):
        return j * ATT_QB * d + r
    return pl.multiple_of(j * (ATT_QB * d), ATT_QB) + r


def _att_fwd(z, rel_bias, g, name):
    S = z.shape[0]
    d = DILS[g]
    TB, QB = ATT_TB, ATT_QB
    H = RADIUS * d
    L = S // d
    cq = (Z_ATT + 3 * GW * g) // GW
    ck, cv = cq + 1, cq + 2
    bk = _t5_bucket(_offsets_qk(QB, 2 * QB, -RADIUS) * d)

    def body(rb_ref, bk_ref, q_ref, kc_ref, kp_ref, kn_ref, vc_ref, vp_ref, vn_ref, o_ref, l_ref,
             qs, ks, vs, os_, ls, bias):
        i = pl.program_id(0)

        @pl.when(i == 0)
        def _():
            off = _offsets_qk(QB, 2 * QB, -RADIUS)
            for h in range(HPG):
                bias[h] = _bias_table(bk_ref[...], rb_ref, g * HPG + h, off)

        _to_halves(qs, 0, q_ref[...].astype(F32) * ATT_SCALE)
        for scr, p_ref, c_ref, n_ref in ((ks, kp_ref, kc_ref, kn_ref), (vs, vp_ref, vc_ref, vn_ref)):
            _to_halves(scr, 0, p_ref[...])
            _to_halves(scr, H, c_ref[...])
            _to_halves(scr, H + TB, n_ref[...])

        lo = lax.broadcasted_iota(I32, (QB, 128), 1) < HD

        def units(rjs):
            work = []
            for r, j in rjs:
                row = _unit_row(r, j, d)
                km = lax.broadcasted_iota(I32, (1, 2 * QB), 1) + (i * (TB // d) + j * QB - RADIUS)
                edge = jnp.where(jnp.where(km >= 0, km, L) < L, 0.0, NEG_INF)
                for hf in (0, 1):
                    q2 = qs[hf, pl.ds(row, QB, stride=d), :]
                    k2 = ks[hf, pl.ds(row, 2 * QB, stride=d), :].astype(BF16)
                    v2 = vs[hf, pl.ds(row, 2 * QB, stride=d), :].astype(BF16)
                    qq = jnp.concatenate([jnp.where(lo, q2, 0.0), jnp.where(lo, 0.0, q2)], axis=0).astype(BF16)
                    work.append((row, hf, edge, k2, v2, qq))
            scores = [_dot_nt(qq, k2) for (_, _, _, k2, _, qq) in work]
            probs = []
            for (row, hf, edge, *_), ss in zip(work, scores):
                es, stats = [], []
                for hh in (0, 1):
                    s = ss[hh * QB:(hh + 1) * QB] + bias[2 * hf + hh] + edge
                    m = jnp.max(s, axis=-1, keepdims=True)
                    e = jnp.exp(s - m)
                    den = jnp.sum(e, axis=-1, keepdims=True)
                    es.append(e.astype(BF16))
                    stats.append((1.0 / den, m + jnp.log(den)))
                probs.append((jnp.concatenate(es, axis=0), stats))
            for (row, hf, _, _, v2, _), (ee, stats) in zip(work, probs):
                oo = _dot(ee, v2)
                os_[hf, pl.ds(row, QB, stride=d), :] = jnp.where(lo, oo[:QB] * stats[0][0], oo[QB:] * stats[1][0])
                ls[hf, pl.ds(row, QB, stride=d), :] = jnp.where(lo, stats[0][1], stats[1][1])

        _att_units(d, units, ATT_FWD_GROUP)
        for hf in (0, 1):
            o_ref[:, hf * 128:(hf + 1) * 128] = os_[hf].astype(BF16)
            l_ref[:, hf * 128:(hf + 1) * 128] = ls[hf]

    def halo3(col):
        c, p, n = _halo_specs(TB, H, S, GW, col)
        return [c, p, n]

    return pl.pallas_call(
        body, name=name, grid=(S // TB,),
        in_specs=[pl.BlockSpec(memory_space=pltpu.SMEM), pl.BlockSpec((QB, 2 * QB), _fixed),
                  pl.BlockSpec((TB, GW), lambda i: (i, cq))] + halo3(ck) + halo3(cv),
        out_specs=[pl.BlockSpec((TB, GW), _row), pl.BlockSpec((TB, GW), _row)],
        out_shape=[jax.ShapeDtypeStruct((S, GW), BF16), jax.ShapeDtypeStruct((S, GW), F32)],
        scratch_shapes=[pltpu.VMEM((2, TB, 128), F32), pltpu.VMEM((2, TB + 2 * H, 128), F32),
                        pltpu.VMEM((2, TB + 2 * H, 128), F32), pltpu.VMEM((2, TB, 128), F32),
                        pltpu.VMEM((2, TB, 128), F32), pltpu.VMEM((HPG, QB, 2 * QB), F32)],
        compiler_params=_params(("arbitrary",)),
    )(rel_bias, bk, z, z, z, z, z, z, z)


def _att_combine(os3, ls3, name):
    S = os3[0].shape[0]
    T = 1024

    def body(o1, o2, o3, l1, l2, l3, o_ref, l_ref):
        lv = [l1[...], l2[...], l3[...]]
        m = jnp.maximum(jnp.maximum(lv[0], lv[1]), lv[2])
        e = [jnp.exp(v - m) for v in lv]
        den = e[0] + e[1] + e[2]
        acc = jnp.zeros_like(m)
        for ev, o in zip(e, (o1, o2, o3)):
            acc = acc + (ev / den) * o[...].astype(F32)
        o_ref[...] = acc.astype(BF16)
        l_ref[...] = m + jnp.log(den)

    blk = pl.BlockSpec((T, GW), _row)
    return pl.pallas_call(
        body, name=name, grid=(S // T,), in_specs=[blk] * 6, out_specs=[blk, blk],
        out_shape=[jax.ShapeDtypeStruct((S, GW), BF16), jax.ShapeDtypeStruct((S, GW), F32)],
        compiler_params=_params(("parallel",)),
    )(*os3, *ls3)


def _att_prep(do, o, lse, name):
    S = do.shape[0]
    T = 1024

    def body(do_ref, o_ref, l_ref, out_ref):
        prod = do_ref[...].astype(F32) * o_ref[...].astype(F32)
        dd = [jnp.broadcast_to(jnp.sum(prod[:, h * HD:(h + 1) * HD], axis=-1, keepdims=True), (T, HD))
              for h in range(HPG)]
        lane = lax.broadcasted_iota(I32, (T, GW), 1)
        out_ref[...] = jnp.where(lane % HD < HD // 2, l_ref[...], jnp.concatenate(dd, axis=-1))

    blk = pl.BlockSpec((T, GW), _row)
    return pl.pallas_call(
        body, name=name, grid=(S // T,), in_specs=[blk] * 3, out_specs=blk,
        out_shape=jax.ShapeDtypeStruct((S, GW), F32), compiler_params=_params(("parallel",)),
    )(do, o, lse)


def _att_bwd(z, rel_bias, do, ld, dz, g, name):
    S = z.shape[0]
    d = DILS[g]
    TB, QB = ATT_TB, ATT_QB
    H = RADIUS * d
    L = S // d
    E = TB + 2 * H
    cq = (Z_ATT + 3 * GW * g) // GW
    ck, cv = cq + 1, cq + 2
    bk_a = _t5_bucket(_offsets_qk(QB, 2 * QB, -RADIUS) * d)
    bk_b = _t5_bucket(-_offsets_qk(QB, 2 * QB, -RADIUS) * d)

    def body(rb_ref, bka_ref, bkb_ref, *refs):
        ins, (dz_ref, db_ref) = refs[:15], refs[16:18]
        qs, ks, vs, dos, ls, dqs, dks, dvs, bias_a, bias_b, dbias = refs[18:]
        i = pl.program_id(0)
        n = pl.num_programs(0)

        @pl.when(i == 0)
        def _():
            off = _offsets_qk(QB, 2 * QB, -RADIUS)
            for h in range(HPG):
                bias_a[h] = _bias_table(bka_ref[...], rb_ref, g * HPG + h, off)
                bias_b[h] = _bias_table(bkb_ref[...], rb_ref, g * HPG + h, off)
            dbias[...] = jnp.zeros_like(dbias)

        for a, scr in enumerate((qs, ks, vs, dos, ls)):
            c_ref, p_ref, n_ref = ins[3 * a:3 * a + 3]
            pre = (lambda v: v.astype(F32) * ATT_SCALE) if a == 0 else (lambda v: v)
            _to_halves(scr, 0, pre(p_ref[...]))
            _to_halves(scr, H, pre(c_ref[...]))
            _to_halves(scr, H + TB, pre(n_ref[...]))

        lo = lax.broadcasted_iota(I32, (QB, 128), 1) < HD

        def split(v):
            return jnp.concatenate([jnp.where(lo, v, 0.0), jnp.where(lo, 0.0, v)], axis=0).astype(BF16)

        def halves(v):
            return v[:QB], v[QB:]

        def units(rjs):
            work = []
            for r, j in rjs:
                row = _unit_row(r, j, d)
                cur = row + H
                m0 = i * (TB // d) + j * QB - RADIUS
                km = lax.broadcasted_iota(I32, (1, 2 * QB), 1) + m0
                edge_a = jnp.where(jnp.where(km >= 0, km, L) < L, 0.0, NEG_INF)
                for hf in (0, 1):
                    ld = lambda scr, at, nrow: scr[hf, pl.ds(at, nrow, stride=d), :]
                    w = dict(row=row, hf=hf, edge=edge_a, l_c=ld(ls, cur, QB), l_t=ld(ls, row, 2 * QB).T)
                    for nm, scr in (("q", qs), ("k", ks), ("v", vs), ("do", dos)):
                        w[nm + "_c"] = split(ld(scr, cur, QB))
                        w[nm + "_e"] = ld(scr, row, 2 * QB).astype(BF16)
                    work.append(w)
            for w in work:
                w["s"] = halves(_dot_nt(w["q_c"], w["k_e"]))
                w["dp"] = halves(_dot_nt(w["do_c"], w["v_e"]))
                w["s2"] = halves(_dot_nt(w["k_c"], w["q_e"]))
                w["dp2"] = halves(_dot_nt(w["v_c"], w["do_e"]))
            for w in work:
                w["ds"], w["p2"], w["ds2"] = [], [], []
                for hh in (0, 1):
                    h, c0 = 2 * w["hf"] + hh, HD * hh
                    l_c, l_t = w["l_c"], w["l_t"]
                    p = jnp.exp(w["s"][hh] + bias_a[h] + w["edge"] - l_c[:, c0:c0 + 1])
                    ds = p * (w["dp"][hh] - l_c[:, c0 + HD // 2:c0 + HD // 2 + 1])
                    dbias[h] += ds
                    p2 = jnp.exp(w["s2"][hh] + bias_b[h] + w["edge"] - l_t[c0:c0 + 1, :])
                    ds2 = p2 * (w["dp2"][hh] - l_t[c0 + HD // 2:c0 + HD // 2 + 1, :])
                    w["ds"].append(ds.astype(BF16))
                    w["p2"].append(p2.astype(BF16))
                    w["ds2"].append(ds2.astype(BF16))
            for w in work:
                at = pl.ds(w["row"], QB, stride=d)
                both = lambda pair, rhs: halves(_dot(jnp.concatenate(pair, axis=0), rhs))
                dq = both(w["ds"], w["k_e"])
                dqs[w["hf"], at, :] = jnp.where(lo, dq[0], dq[1]) * ATT_SCALE
                dv = both(w["p2"], w["do_e"])
                dvs[w["hf"], at, :] = jnp.where(lo, dv[0], dv[1])
                dk = both(w["ds2"], w["q_e"])
                dks[w["hf"], at, :] = jnp.where(lo, dk[0], dk[1])

        _att_units(d, units, ATT_BWD_GROUP)
        for a, scr in enumerate((dqs, dks, dvs)):
            for hf in (0, 1):
                dz_ref[:, a * GW + hf * 128:a * GW + (hf + 1) * 128] = scr[hf].astype(BF16)

        @pl.when(i == n - 1)
        def _():
            rows = lax.broadcasted_iota(I32, (NUM_BUCKETS, 128), 0)
            lanes = lax.broadcasted_iota(I32, (NUM_BUCKETS, 128), 1)
            out = jnp.zeros((NUM_BUCKETS, 128), F32)
            bk = bka_ref[...]
            for h in range(HPG):
                acc = dbias[h]
                for b in range(NUM_BUCKETS):
                    tot = jnp.sum(jnp.sum(jnp.where(bk == b, acc, 0.0), axis=1, keepdims=True), axis=0, keepdims=True)
                    out = out + jnp.where((rows == b) & (lanes == h), tot, 0.0)
            db_ref[...] = out

    def halo3(col, width=GW):
        return _halo_specs(TB, H, S, width, col)

    one = pl.Buffered(1)

    def single(specs):
        return [pl.BlockSpec(s.block_shape, s.index_map, pipeline_mode=one) for s in specs]

    in_specs = ([pl.BlockSpec(memory_space=pltpu.SMEM), pl.BlockSpec((QB, 2 * QB), _fixed),
                 pl.BlockSpec((QB, 2 * QB), _fixed)]
                + single(halo3(cq) + halo3(ck) + halo3(cv) + halo3(0) + halo3(0))
                + [pl.BlockSpec(memory_space=pl.ANY)])
    return pl.pallas_call(
        body, name=name, grid=(S // TB,), in_specs=in_specs,
        out_specs=[pl.BlockSpec((TB, 3 * GW), lambda i: (i, cq // 3)), pl.BlockSpec((NUM_BUCKETS, 128), _fixed)],
        out_shape=[jax.ShapeDtypeStruct(dz.shape, BF16), jax.ShapeDtypeStruct((NUM_BUCKETS, 128), F32)],
        input_output_aliases={18: 0},
        scratch_shapes=[pltpu.VMEM((2, E, 128), F32)] * 5 + [pltpu.VMEM((2, TB, 128), F32)] * 3
        + [pltpu.VMEM((HPG, QB, 2 * QB), F32)] * 3,
        compiler_params=_params(("arbitrary",)),
    )(rel_bias, bk_a, bk_b, z, z, z, z, z, z, z, z, z, do, do, do, ld, ld, ld, dz)


def _memkv_fwd(mem, gm, wkv, name):
    def body(m_ref, g_ref, w_ref, hm_ref, kv_ref):
        hm = _rms_fwd_val(m_ref[...], g_ref[...]).astype(BF16)
        hm_ref[...] = hm
        kv_ref[...] = _dot(hm, w_ref[...]).astype(BF16)

    return pl.pallas_call(
        body, name=name,
        out_shape=[jax.ShapeDtypeStruct((N_MEM, D), BF16), jax.ShapeDtypeStruct((N_MEM, 2 * MW), BF16)],
        compiler_params=_params(),
    )(mem, gm, wkv)


def _memkv_bwd(mem, gm, hm, wkv, dkv, name):
    def body(m_ref, g_ref, hm_ref, w_ref, dkv_ref, dw_ref, dg_ref):
        dkv_b = dkv_ref[...].astype(BF16)
        dw_ref[...] = _dot_tn(hm_ref[...], dkv_b)
        dhm = _dot_nt(dkv_b, w_ref[...])
        _, dgr = _rms_bwd_val(m_ref[...], g_ref[...], dhm)
        dg_ref[...] = jnp.sum(dgr, axis=0, keepdims=True)

    return pl.pallas_call(
        body, name=name,
        out_shape=[jax.ShapeDtypeStruct((D, 2 * MW), F32), jax.ShapeDtypeStruct((1, D), F32)],
        compiler_params=_params(),
    )(mem, gm, hm, wkv, dkv)


MEM_T = 512


def _mem_q_spec():
    return pl.BlockSpec((MEM_T, MW), lambda i: (i, Z_MEM // MW))


def _memattn_fwd(z, kv, name):
    S = z.shape[0]
    T = MEM_T

    def body(q_ref, kv_ref, o_ref):
        for h in range(MH):
            kh = kv_ref[:, h * MHD:(h + 1) * MHD]
            vh = kv_ref[:, MW + h * MHD:MW + (h + 1) * MHD]
            s = _dot_nt(q_ref[:, h * MHD:(h + 1) * MHD], kh) * MEM_SCALE
            e = jnp.exp(s - jnp.max(s, axis=-1, keepdims=True))
            p = e / jnp.sum(e, axis=-1, keepdims=True)
            o_ref[:, h * MHD:(h + 1) * MHD] = _dot(p.astype(BF16), vh).astype(BF16)

    return pl.pallas_call(
        body, name=name, grid=(S // T,),
        in_specs=[_mem_q_spec(), pl.BlockSpec((N_MEM, 2 * MW), _fixed)],
        out_specs=pl.BlockSpec((T, MW), _row),
        out_shape=jax.ShapeDtypeStruct((S, MW), BF16),
        compiler_params=_params(("parallel",)),
    )(z, kv)


def _memattn_bwd(z, kv, dom, dz, name):
    S = z.shape[0]
    T = MEM_T

    def body(q_ref, kv_ref, do_ref, dz_in, dq_ref, dkv_ref):
        @pl.when(pl.program_id(0) == 0)
        def _():
            dkv_ref[...] = jnp.zeros_like(dkv_ref)

        for h in range(MH):
            kh = kv_ref[:, h * MHD:(h + 1) * MHD]
            vh = kv_ref[:, MW + h * MHD:MW + (h + 1) * MHD]
            qh = q_ref[:, h * MHD:(h + 1) * MHD]
            doh = do_ref[:, h * MHD:(h + 1) * MHD]
            s = _dot_nt(qh, kh) * MEM_SCALE
            e = jnp.exp(s - jnp.max(s, axis=-1, keepdims=True))
            p = e / jnp.sum(e, axis=-1, keepdims=True)
            dkv_ref[:, MW + h * MHD:MW + (h + 1) * MHD] += _dot_tn(p.astype(BF16), doh)
            dp = _dot_nt(doh, vh)
            ds = (p * (dp - jnp.sum(dp * p, axis=-1, keepdims=True))).astype(BF16)
            dq_ref[:, h * MHD:(h + 1) * MHD] = (_dot(ds, kh) * MEM_SCALE).astype(BF16)
            dkv_ref[:, h * MHD:(h + 1) * MHD] += _dot_tn(ds, qh) * MEM_SCALE

    return pl.pallas_call(
        body, name=name, grid=(S // T,),
        in_specs=[_mem_q_spec(), pl.BlockSpec((N_MEM, 2 * MW), _fixed), pl.BlockSpec((T, MW), _row),
                  pl.BlockSpec(memory_space=pl.ANY)],
        out_specs=[_mem_q_spec(), pl.BlockSpec((N_MEM, 2 * MW), _fixed)],
        out_shape=[jax.ShapeDtypeStruct(dz.shape, BF16), jax.ShapeDtypeStruct((N_MEM, 2 * MW), F32)],
        input_output_aliases={3: 0},
        compiler_params=_params(("arbitrary",)),
    )(z, kv, dom, dz)


MERGE_T = 512


def _gate_spec(T):
    return pl.BlockSpec((T, 3 * D), lambda i: (i, Z_GATE // (3 * D)))


def _branches(ca_ref, oa_ref, om_ref, wco_ref, wao_ref, wmo_ref, zg_ref, bg_ref):
    ys = [_dot(ca_ref[...], wco_ref[...]), _dot(oa_ref[...], wao_ref[...]), _dot(om_ref[...], wmo_ref[...])]
    gs = [_sigmoid(zg_ref[:, b * D:(b + 1) * D].astype(F32) + bg_ref[:, b * D:(b + 1) * D]) for b in range(3)]
    return ys, gs


def _merge_fwd(x, cact, oatt, om, z, wco, wao, wmo, wout, bgate, gpost, gnext, name):
    S = x.shape[0]
    T = MERGE_T

    def body(x_ref, ca_ref, oa_ref, om_ref, zg_ref, wco_ref, wao_ref, wmo_ref, wout_ref, bg_ref, gp_ref, gn_ref,
             x1_ref, mg_ref, t_ref, h_ref):
        ys, gs = _branches(ca_ref, oa_ref, om_ref, wco_ref, wao_ref, wmo_ref, zg_ref, bg_ref)
        mb = (gs[0] * ys[0] + gs[1] * ys[1] + gs[2] * ys[2]).astype(BF16)
        t = _dot(mb, wout_ref[...])
        mg_ref[...] = mb
        t_ref[...] = t
        x1 = x_ref[...] + _rms_fwd_val(t, gp_ref[...])
        x1_ref[...] = x1
        h_ref[...] = _rms_fwd_val(x1, gn_ref[...]).astype(BF16)

    full = lambda a: pl.BlockSpec(a.shape, _fixed)
    return pl.pallas_call(
        body, name=name, grid=(S // T,),
        in_specs=[pl.BlockSpec((T, D), _row), pl.BlockSpec((T, CW), _row), pl.BlockSpec((T, GW), _row),
                  pl.BlockSpec((T, MW), _row), _gate_spec(T)]
        + [full(wco), full(wao), full(wmo), full(wout), full(bgate), full(gpost), full(gnext)],
        out_specs=[pl.BlockSpec((T, D), _row)] * 4,
        out_shape=[jax.ShapeDtypeStruct((S, D), F32), jax.ShapeDtypeStruct((S, D), BF16),
                   jax.ShapeDtypeStruct((S, D), F32), jax.ShapeDtypeStruct((S, D), BF16)],
        compiler_params=_params(("parallel",)),
    )(x, cact, oatt, om, z, wco, wao, wmo, wout, bgate, gpost, gnext)


def _merge_bwd(dx1, t, mg, cact, oatt, om, z, wco, wao, wmo, wout, bgate, gpost, name):
    S = dx1.shape[0]
    T = MERGE_T

    def body(dx_ref, t_ref, mg_ref, ca_ref, oa_ref, om_ref, zg_ref, wco_ref, wao_ref, wmo_ref, wout_ref,
             bg_ref, gp_ref, dzg_ref, dca_ref, doa_ref, dom_ref, dwco_ref, dwao_ref, dwmo_ref, dwout_ref,
             dbg_ref, dgp_ref):
        accs = (dwco_ref, dwao_ref, dwmo_ref, dwout_ref, dbg_ref, dgp_ref)

        @pl.when(pl.program_id(0) == 0)
        def _():
            for a in accs:
                a[...] = jnp.zeros_like(a)

        dt, dgr = _rms_bwd_val(t_ref[...], gp_ref[...], dx_ref[...])
        dgp_ref[...] += jnp.sum(dgr, axis=0, keepdims=True)
        dtb = dt.astype(BF16)
        dwout_ref[...] += _dot_tn(mg_ref[...], dtb)
        dm = _dot_nt(dtb, wout_ref[...])
        ys, gs = _branches(ca_ref, oa_ref, om_ref, wco_ref, wao_ref, wmo_ref, zg_ref, bg_ref)
        for b, (act_ref, w_ref, dw_ref, da_ref) in enumerate(
                ((ca_ref, wco_ref, dwco_ref, dca_ref), (oa_ref, wao_ref, dwao_ref, doa_ref),
                 (om_ref, wmo_ref, dwmo_ref, dom_ref))):
            dzg = dm * ys[b] * gs[b] * (1.0 - gs[b])
            dzg_ref[:, b * D:(b + 1) * D] = dzg.astype(BF16)
            dbg_ref[:, b * D:(b + 1) * D] += jnp.sum(dzg, axis=0, keepdims=True)
            dy = (dm * gs[b]).astype(BF16)
            dw_ref[...] += _dot_tn(act_ref[...], dy)
            da_ref[...] = _dot_nt(dy, w_ref[...]).astype(BF16)

    full = lambda a: pl.BlockSpec(a.shape, _fixed)
    fullf = lambda a: jax.ShapeDtypeStruct(a.shape, F32)
    return pl.pallas_call(
        body, name=name, grid=(S // T,),
        in_specs=[pl.BlockSpec((T, D), _row), pl.BlockSpec((T, D), _row), pl.BlockSpec((T, D), _row),
                  pl.BlockSpec((T, CW), _row), pl.BlockSpec((T, GW), _row), pl.BlockSpec((T, MW), _row)]
        + [_gate_spec(T), full(wco), full(wao), full(wmo), full(wout), full(bgate), full(gpost)],
        out_specs=[_gate_spec(T), pl.BlockSpec((T, CW), _row), pl.BlockSpec((T, GW), _row),
                   pl.BlockSpec((T, MW), _row), full(wco), full(wao), full(wmo), full(wout), full(bgate), full(gpost)],
        out_shape=[jax.ShapeDtypeStruct((S, NIN), BF16), jax.ShapeDtypeStruct((S, CW), BF16),
                   jax.ShapeDtypeStruct((S, GW), BF16), jax.ShapeDtypeStruct((S, MW), BF16),
                   fullf(wco), fullf(wao), fullf(wmo), fullf(wout), fullf(bgate), fullf(gpost)],
        compiler_params=_params(("arbitrary",)),
    )(dx1, t, mg, cact, oatt, om, z, wco, wao, wmo, wout, bgate, gpost)


FFN_T = 256


def _ffn_fwd(x1, gu, wfo, gpost, gnext, name):
    S = x1.shape[0]
    T = FFN_T

    nxt = gnext is not None

    def body(x_ref, gu_ref, w_ref, gp_ref, *rest):
        x2_ref, f_ref = rest[nxt:nxt + 2]
        gv = gu_ref[:, :FH].astype(F32)
        uv = gu_ref[:, FH:].astype(F32)
        act = (gv * _sigmoid(gv) * uv).astype(BF16)
        f = _dot(act, w_ref[...])
        f_ref[...] = f
        x2 = x_ref[...] + _rms_fwd_val(f, gp_ref[...])
        x2_ref[...] = x2
        if nxt:
            rest[3][...] = _rms_fwd_val(x2, rest[0][...]).astype(BF16)

    return pl.pallas_call(
        body, name=name, grid=(S // T,),
        in_specs=[pl.BlockSpec((T, D), _row), pl.BlockSpec((T, 2 * FH), _row), pl.BlockSpec((FH, D), _fixed),
                  pl.BlockSpec((1, D), _fixed)] + [pl.BlockSpec((1, D), _fixed)] * nxt,
        out_specs=[pl.BlockSpec((T, D), _row)] * (2 + nxt),
        out_shape=[jax.ShapeDtypeStruct((S, D), F32)] * 2 + [jax.ShapeDtypeStruct((S, D), BF16)] * nxt,
        compiler_params=_params(("parallel",)),
    )(x1, gu, wfo, gpost, *([gnext] if nxt else []))


def _ffn_fwd_loss(x1, gu, wfo, gpost, target, name):
    S = x1.shape[0]
    T = FFN_T

    def body(x_ref, gu_ref, w_ref, gp_ref, t_ref, dy_ref, f_ref, l_ref):
        @pl.when(pl.program_id(0) == 0)
        def _():
            l_ref[...] = jnp.zeros_like(l_ref)

        gv = gu_ref[:, :FH].astype(F32)
        uv = gu_ref[:, FH:].astype(F32)
        act = (gv * _sigmoid(gv) * uv).astype(BF16)
        f = _dot(act, w_ref[...])
        f_ref[...] = f
        e = x_ref[...] + _rms_fwd_val(f, gp_ref[...]) - t_ref[...]
        dy_ref[...] = e * (1.0 / D)
        l_ref[...] += (0.5 / D) * jnp.sum(jnp.sum(e * e, axis=1, keepdims=True), axis=0, keepdims=True)

    return pl.pallas_call(
        body, name=name, grid=(S // T,),
        in_specs=[pl.BlockSpec((T, D), _row), pl.BlockSpec((T, 2 * FH), _row), pl.BlockSpec((FH, D), _fixed),
                  pl.BlockSpec((1, D), _fixed), pl.BlockSpec((T, D), _row)],
        out_specs=[pl.BlockSpec((T, D), _row), pl.BlockSpec((T, D), _row), pl.BlockSpec((8, 128), _fixed)],
        out_shape=[jax.ShapeDtypeStruct((S, D), F32)] * 2 + [jax.ShapeDtypeStruct((8, 128), F32)],
        compiler_params=_params(("arbitrary",)),
    )(x1, gu, wfo, gpost, target)


def _ffn_bwd(dx2, f, gu, wfo, gpost, name):
    S = dx2.shape[0]
    T = FFN_T

    def body(dx_ref, f_ref, gu_ref, w_ref, gp_ref, dgu_ref, df_ref, act_ref, dgp_ref):
        @pl.when(pl.program_id(0) == 0)
        def _():
            dgp_ref[...] = jnp.zeros_like(dgp_ref)

        df, dgr = _rms_bwd_val(f_ref[...], gp_ref[...], dx_ref[...])
        dgp_ref[...] += jnp.sum(dgr, axis=0, keepdims=True)
        dfb = df.astype(BF16)
        df_ref[...] = dfb
        dact = _dot_nt(dfb, w_ref[...])
        gv = gu_ref[:, :FH].astype(F32)
        uv = gu_ref[:, FH:].astype(F32)
        sg = _sigmoid(gv)
        silu = gv * sg
        act_ref[...] = (silu * uv).astype(BF16)
        dgu_ref[:, :FH] = (dact * uv * (sg * (1.0 + gv * (1.0 - sg)))).astype(BF16)
        dgu_ref[:, FH:] = (dact * silu).astype(BF16)

    return pl.pallas_call(
        body, name=name, grid=(S // T,),
        in_specs=[pl.BlockSpec((T, D), _row), pl.BlockSpec((T, D), _row), pl.BlockSpec((T, 2 * FH), _row),
                  pl.BlockSpec((FH, D), _fixed), pl.BlockSpec((1, D), _fixed)],
        out_specs=[pl.BlockSpec((T, 2 * FH), _row), pl.BlockSpec((T, D), _row), pl.BlockSpec((T, FH), _row),
                   pl.BlockSpec((1, D), _fixed)],
        out_shape=[jax.ShapeDtypeStruct((S, 2 * FH), BF16), jax.ShapeDtypeStruct((S, D), BF16),
                   jax.ShapeDtypeStruct((S, FH), BF16), jax.ShapeDtypeStruct((1, D), F32)],
        compiler_params=_params(("arbitrary",)),
    )(dx2, f, gu, wfo, gpost)


def _loss_head(y, target, name):
    S = y.shape[0]
    T = 512

    def body(y_ref, t_ref, dy_ref, l_ref):
        @pl.when(pl.program_id(0) == 0)
        def _():
            l_ref[...] = jnp.zeros_like(l_ref)

        e = y_ref[...] - t_ref[...]
        dy_ref[...] = e * (1.0 / D)
        l_ref[...] += (0.5 / D) * jnp.sum(jnp.sum(e * e, axis=1, keepdims=True), axis=0, keepdims=True)

    return pl.pallas_call(
        body, name=name, grid=(S // T,),
        in_specs=[pl.BlockSpec((T, D), _row)] * 2,
        out_specs=[pl.BlockSpec((T, D), _row), pl.BlockSpec((8, 128), _fixed)],
        out_shape=[jax.ShapeDtypeStruct((S, D), F32), jax.ShapeDtypeStruct((8, 128), F32)],
        compiler_params=_params(("arbitrary",)),
    )(y, target)


BIG = ("w_in", "w_conv_out", "w_att_out", "w_mem_kv", "w_mem_out", "w_out", "w_ffn_in", "w_ffn_out")
SMALL = ("rel_bias", "norm_mix_pre", "b_gate", "conv_dw_bias", "conv_ln_g", "conv_ln_b", "norm_mem",
         "norm_mix_post", "norm_ffn_pre", "norm_ffn_post")


def _layer_fwd(l, x, h, mem, w, rest, rel_bias, gnext, target=None):
    tag = f"_l{l}"
    z = _mm_nn(h, w["w_in"], 512, BF16, "mm_in" + tag)
    yc, cact = _conv_fwd(z, w["conv_dw"], w["conv_dw_bias"], w["conv_ln_g"], w["conv_ln_b"], "conv_fwd" + tag)
    og, lg = zip(*[_att_fwd(z, rel_bias, g, f"att_fwd_g{g}" + tag) for g in range(3)])
    oatt, lse = _att_combine(og, lg, "att_combine" + tag)
    w = {**w, **rest(oatt)}
    hm, kv = _memkv_fwd(mem, w["norm_mem"], w["w_mem_kv"], "memkv_fwd" + tag)
    om = _memattn_fwd(z, kv, "memattn_fwd" + tag)
    x1, mg, t, h2 = _merge_fwd(x, cact, oatt, om, z, w["w_conv_out"], w["w_att_out"], w["w_mem_out"], w["w_out"],
                               w["b_gate"], w["norm_mix_post"], w["norm_ffn_pre"], "merge_fwd" + tag)
    gu = _mm_nn(h2, w["w_ffn_in"], 512, BF16, "mm_ffn_in" + tag)
    if target is None:
        x2, f, *hn = _ffn_fwd(x1, gu, w["w_ffn_out"], w["norm_ffn_post"], gnext, "ffn_fwd" + tag)
    else:
        dy, f, lpart = _ffn_fwd_loss(x1, gu, w["w_ffn_out"], w["norm_ffn_post"], target, "ffn_fwd_loss" + tag)
        x2, hn = (dy, lpart), []
    saved = dict(x=x, h=h, z=z, yc=yc, cact=cact, oatt=oatt, lse=lse, hm=hm, kv=kv, om=om, x1=x1, mg=mg, t=t,
                 h2=h2, gu=gu, f=f)
    return x2, (hn[0] if hn else None), saved, w


def _layer_bwd(l, dx2, mem, w, rel_bias, s, on_ffn_grads=None):
    tag = f"_l{l}"
    gr = {}
    dgu, df, act, gr["norm_ffn_post"] = _ffn_bwd(dx2, s["f"], s["gu"], w["w_ffn_out"], w["norm_ffn_post"], "ffn_bwd" + tag)
    gr["w_ffn_out"] = _mm_tn(act, df, 1024, 512, "dw_ffn_out" + tag)
    gr["w_ffn_in"] = _mm_tn(s["h2"], dgu, 2048, 1408, "dw_ffn_in" + tag)
    gpre = w["norm_ffn_pre"]
    if on_ffn_grads is not None:
        gpre = gpre + on_ffn_grads(gr)[0, 0]
    dx1, gr["norm_ffn_pre"] = _mm_nt_rms_bwd(dgu, w["w_ffn_in"], s["x1"], gpre, dx2, 512, "dh_ffn" + tag)
    (dz, dcact, doatt, dom, gr["w_conv_out"], gr["w_att_out"], gr["w_mem_out"], gr["w_out"], gr["b_gate"],
     gr["norm_mix_post"]) = _merge_bwd(dx1, s["t"], s["mg"], s["cact"], s["oatt"], s["om"], s["z"], w["w_conv_out"],
                                       w["w_att_out"], w["w_mem_out"], w["w_out"], w["b_gate"], w["norm_mix_post"],
                                       "merge_bwd" + tag)
    dyc, gr["conv_ln_g"], gr["conv_ln_b"], gr["conv_dw_bias"] = _conv_bwd_ln(
        s["yc"], dcact, w["conv_ln_g"], w["conv_ln_b"], "conv_bwd_ln" + tag)
    dz, dwdw = _conv_bwd_dw(s["z"], dyc, w["conv_dw"], dz, "conv_bwd_dw" + tag)
    gr["conv_dw"] = dwdw[:KSIZE]
    ld = _att_prep(doatt, s["oatt"], s["lse"], "att_prep" + tag)
    drb = []
    for g in range(3):
        dz, db = _att_bwd(s["z"], rel_bias, doatt, ld, dz, g, f"att_bwd_g{g}" + tag)
        drb.append(db)
    dz, dkv = _memattn_bwd(s["z"], s["kv"], dom, dz, "memattn_bwd" + tag)
    gr["w_mem_kv"], gr["norm_mem"] = _memkv_bwd(mem, w["norm_mem"], s["hm"], w["w_mem_kv"], dkv, "memkv_bwd" + tag)
    gr["w_in"] = _mm_tn(s["h"], dz, 2048, 1152, "dw_in" + tag)
    dx, gr["norm_mix_pre"] = _mm_nt_rms_bwd(dz, w["w_in"], s["x"], w["norm_mix_pre"], dx1, 512, "dh_in" + tag)
    return dx, gr, drb


def _rel_bias_total(parts, name):
    def body(*refs):
        out_ref = refs[-1]
        acc = jnp.zeros((NUM_BUCKETS, 128), F32)
        for l in range(DEPTH):
            for g in range(3):
                v = refs[l * 3 + g][...]
                acc = acc + (v if g == 0 else pltpu.roll(v, HPG * g, axis=1))
        out_ref[...] = acc

    return pl.pallas_call(body, name=name, out_shape=jax.ShapeDtypeStruct((NUM_BUCKETS, 128), F32),
                          compiler_params=_params())(*[p for layer in parts for p in layer])


def _local_step(x, mem, target, rel_bias, layer_fns, gmix, on_grads=None, on_last_ffn_grads=None):
    saved, layers = [], []
    h = _rms_h(x, gmix[0], "rms_mix_l0")
    for l in range(DEPTH):
        first, rest = layer_fns[l](x)
        last = l + 1 == DEPTH
        x, h, s, w = _layer_fwd(l, x, h, mem, first, rest, rel_bias, None if last else gmix[l + 1],
                                target if last else None)
        saved.append(s)
        layers.append(w)
    dy, lpart = x
    grads = [None] * DEPTH
    drb = [None] * DEPTH
    for l in reversed(range(DEPTH)):
        dy, grads[l], drb[l] = _layer_bwd(l, dy, mem, layers[l], rel_bias, saved[l],
                                          on_last_ffn_grads if l == 0 else None)
        if on_grads is not None and l > 0:
            below = dict(layers[l - 1])
            below["norm_ffn_post"] = below["norm_ffn_post"] + on_grads(l, grads[l])[0, 0]
            layers[l - 1] = below
    return lpart[0, 0], dy, grads, _rel_bias_total(drb, "rel_bias_total")


def _z_cols_from_ref(w):
    att = [w[..., R_ATT + (3 * j + g) * GW:R_ATT + (3 * j + g + 1) * GW] for g in range(3) for j in range(3)]
    return jnp.concatenate([w[..., R_GATE:], w[..., :C1], w[..., R_MEM:R_GATE]] + att, axis=-1)


def _ref_cols_from_z(w):
    att = [w[..., Z_ATT + (3 * g + j) * GW:Z_ATT + (3 * g + j + 1) * GW] for j in range(3) for g in range(3)]
    return jnp.concatenate([w[..., Z_CONV:Z_MEM]] + att + [w[..., Z_MEM:Z_ATT], w[..., Z_GATE:Z_CONV]], axis=-1)


N_CHIPS = 4
SHARD = {"w_in": ((D, NIN // 4), 1), "w_conv_out": ((CW, D // 4), 1), "w_att_out": ((GW, D // 4), 1),
         "w_mem_kv": ((D // 4, 2 * MW), 0), "w_mem_out": ((MW, D // 4), 1), "w_out": ((D // 4, D), 0),
         "w_ffn_in": ((D, 2 * FH // 4), 1), "w_ffn_out": ((FH // 4, D), 0)}
CDW_ROWS = 64
VEC_ROWS = (("norm_mix_pre", 1), ("b_gate", 3), ("conv_dw_bias", 1), ("conv_ln_g", 1), ("conv_ln_b", 1),
            ("norm_mem", 1), ("norm_mix_post", 1), ("norm_ffn_pre", 1), ("norm_ffn_post", 1))
VEC_LROWS = sum(r for _, r in VEC_ROWS)
REL_ROW = DEPTH * VEC_LROWS
CDW_ROW = REL_ROW + 1
CDW_GROWS = DEPTH * KSIZE * CW // D
LOSS_ROW = CDW_ROW + CDW_GROWS
SMALL_ROWS = -(-(LOSS_ROW + 1) // 8) * 8


def _mesh_pos():
    return lax.axis_index("x"), lax.axis_index("y"), lax.axis_index("c")


def _other_chips(x, y):
    chips = [(1 - x, y), (x, 1 - y), (1 - x, 1 - y)]
    return chips, [2 * cx + cy for cx, cy in chips]


NBIG = len(BIG)
ANY_SPEC = pl.BlockSpec(memory_space=pl.ANY)


def _remote(src, dst, send_sems, recv_sems, k, to):
    return pltpu.make_async_remote_copy(src_ref=src, dst_ref=dst, send_sem=send_sems.at[k], recv_sem=recv_sems.at[k],
                                        device_id=to, device_id_type=MESH)


def _half(ref, c):
    h = ref.shape[0] // 2
    return ref.at[pl.ds(c * h if isinstance(c, int) else pl.multiple_of(c * h, 16), h)]


def _all_gather(ws, cdw):
    NBIG = len(ws)

    def body(*refs):
        w_refs, cdw_ref = refs[:NBIG], refs[NBIG]
        g_refs, gc_ref = refs[NBIG + 1:2 * NBIG + 1], refs[2 * NBIG + 1]
        send_sems, recv_sems = refs[2 * NBIG + 2:]
        x, y, c = _mesh_pos()
        j = 2 * x + y
        sibling = (x, y, 1 - c)
        chips, blocks = _other_chips(x, y)
        copy = functools.partial(_remote, send_sems=send_sems, recv_sems=recv_sems)
        pairs = list(zip(w_refs, g_refs))
        first = [copy(_half(w, c), _half(g.at[j], c), k=k * NBIG + n, to=(*chip, c))
                 for k, chip in enumerate(chips) for n, (w, g) in enumerate(pairs)]
        first += [copy(cdw_ref, gc_ref.at[j], k=6 * NBIG + k, to=(*chip, c)) for k, chip in enumerate(chips)]
        for cp in first:
            cp.start()
        passed = []
        for k, b in enumerate(blocks):
            for n, (w, g) in enumerate(pairs):
                copy(_half(w, c), _half(g.at[b], c), k=k * NBIG + n, to=sibling).wait_recv()
            onward = [copy(_half(g.at[b], c), _half(g.at[b], c), k=(3 + k) * NBIG + n, to=sibling)
                      for n, (w, g) in enumerate(pairs)]
            for cp in onward:
                cp.start()
            passed += onward
        for k, b in enumerate(blocks):
            for n, (w, g) in enumerate(pairs):
                copy(_half(w, c), _half(g.at[b], 1 - c), k=(3 + k) * NBIG + n, to=sibling).wait_recv()
            copy(cdw_ref, gc_ref.at[b], k=6 * NBIG + k, to=sibling).wait_recv()
        for cp in first + passed:
            cp.wait_send()

    nsem = 6 * NBIG + 3
    return pl.pallas_call(
        body, name="all_gather_weights",
        out_shape=[jax.ShapeDtypeStruct((N_CHIPS,) + w.shape, BF16) for w in ws]
        + [jax.ShapeDtypeStruct((N_CHIPS, CDW_ROWS, 128), F32)],
        in_specs=[ANY_SPEC] * (NBIG + 1), out_specs=[ANY_SPEC] * (NBIG + 1),
        scratch_shapes=[pltpu.SemaphoreType.DMA((nsem,)), pltpu.SemaphoreType.DMA((nsem,))],
    )(*ws, cdw)


SEM_SPEC = pl.BlockSpec(memory_space=pltpu.SEMAPHORE)
DATAFLOW = pltpu.SideEffectType.DATAFLOW_SIDE_EFFECTING


def _gather_copies(w_refs, g_refs, send_sem, recv_sem):
    x, y, c = _mesh_pos()
    j = 2 * x + y
    chips, _ = _other_chips(x, y)
    return [pltpu.make_async_remote_copy(src_ref=_half(w, c), dst_ref=_half(g.at[j], c), send_sem=send_sem,
                                         recv_sem=recv_sem, device_id=(*chip, cc), device_id_type=MESH)
            for chip in chips for cc in (0, 1) for w, g in zip(w_refs, g_refs)]


def _all_gather_start(ws, after, name):
    NBIG = len(ws)

    def body(*refs):
        w_refs, g_refs = refs[:NBIG], refs[NBIG:2 * NBIG]
        send_sem, recv_sem = refs[2 * NBIG + 1:2 * NBIG + 3]
        token = refs[-1]
        for cp in _gather_copies(w_refs, g_refs, send_sem, recv_sem):
            cp.start()
        token[...] = jnp.zeros_like(token)

    lands = [pltpu.with_memory_space_constraint(lax.empty((N_CHIPS,) + w.shape, BF16), pltpu.HBM) for w in ws]
    ws = [pltpu.with_memory_space_constraint(w, pltpu.HBM) for w in ws]
    hbm = pl.BlockSpec(memory_space=pltpu.HBM)
    out = pl.pallas_call(
        body, name=name,
        out_shape=[pltpu.SemaphoreType.DMA(()), pltpu.SemaphoreType.DMA(())]
        + [pltpu.HBM(w.shape, BF16) for w in ws] + [pltpu.HBM(g.shape, BF16) for g in lands]
        + [jax.ShapeDtypeStruct((8, 128), F32)],
        in_specs=[hbm] * (2 * NBIG) + [ANY_SPEC],
        out_specs=[SEM_SPEC, SEM_SPEC] + [hbm] * (2 * NBIG) + [pl.BlockSpec(memory_space=pltpu.VMEM)],
        input_output_aliases={n: 2 + n for n in range(2 * NBIG)},
        compiler_params=pltpu.CompilerParams(has_side_effects=DATAFLOW),
    )(*ws, *lands, after)
    return out[0], out[1], out[2:2 + NBIG], out[2 + NBIG:2 + 2 * NBIG], out[-1]


def _all_gather_wait(send_sem, recv_sem, ws, lands, after, name):
    NBIG = len(ws)

    def body(*refs):
        w_refs, g_refs = refs[:NBIG], refs[NBIG:2 * NBIG]
        send_sem, recv_sem = refs[2 * NBIG:2 * NBIG + 2]
        x, y, c = _mesh_pos()
        _, blocks = _other_chips(x, y)
        for cp in _gather_copies(w_refs, g_refs, send_sem, recv_sem):
            cp.wait_send()
        for b in blocks:
            for cc in (0, 1):
                for w, g in zip(w_refs, g_refs):
                    pltpu.make_async_remote_copy(src_ref=_half(w, cc), dst_ref=_half(g.at[b], cc), send_sem=send_sem,
                                                 recv_sem=recv_sem, device_id=(x, y, c),
                                                 device_id_type=MESH).wait_recv()

    hbm = pl.BlockSpec(memory_space=pltpu.HBM)
    out = pl.pallas_call(
        body, name=name,
        out_shape=[pltpu.HBM(w.shape, BF16) for w in ws] + [pltpu.HBM(g.shape, BF16) for g in lands],
        in_specs=[hbm] * (2 * NBIG) + [SEM_SPEC, SEM_SPEC, ANY_SPEC],
        out_specs=[hbm] * (2 * NBIG),
        input_output_aliases={n: n for n in range(2 * NBIG)},
        compiler_params=pltpu.CompilerParams(has_side_effects=DATAFLOW),
    )(*ws, *lands, send_sem, recv_sem, after)
    return out[:NBIG], out[NBIG:]


def _half_rows(ref, c):
    h = ref.shape[1] // 2
    return ref.at[:, pl.ds(pl.multiple_of(c * h, 16), h)]


def _sibling_exchange(ps):
    nw = len(ps)

    def body(*refs):
        p_refs, r_refs, (send_sems, recv_sems) = refs[:nw], refs[nw:2 * nw], refs[2 * nw:]
        x, y, c = _mesh_pos()
        cps = [_remote(_half_rows(p, 1 - c), r, send_sems, recv_sems, n, (x, y, 1 - c))
               for n, (p, r) in enumerate(zip(p_refs, r_refs))]
        for cp in cps:
            cp.start()
        for cp in cps:
            cp.wait()

    return pl.pallas_call(
        body, name="grad_sibling_exchange",
        out_shape=[jax.ShapeDtypeStruct((N_CHIPS, p.shape[1] // 2, p.shape[2]), p.dtype) for p in ps],
        in_specs=[ANY_SPEC] * nw, out_specs=[ANY_SPEC] * nw,
        scratch_shapes=[pltpu.SemaphoreType.DMA((nw,)), pltpu.SemaphoreType.DMA((nw,))],
    )(*ps)


SUM_BLOCK_BYTES = 2 * 1024 * 1024


def _sum_rows(s0, s1):
    return s0 if s0 * s1 * 2 <= SUM_BLOCK_BYTES else s0 // 2


def _add_own_half(where, p, r, name):
    _, h, s1 = r.shape
    T = _sum_rows(h, s1)
    nt = h // T

    def body(where_ref, p_ref, r_ref, o_ref):
        o_ref[...] = (p_ref[...].astype(F32) + r_ref[...].astype(F32)).astype(BF16)

    return pl.pallas_call(
        body, name=name,
        grid_spec=pltpu.PrefetchScalarGridSpec(
            num_scalar_prefetch=1, grid=(N_CHIPS, nt),
            in_specs=[pl.BlockSpec((1, T, s1), lambda j, i, wh: (j, wh[0] * nt + i, 0)),
                      pl.BlockSpec((1, T, s1), lambda j, i, wh: (j, i, 0))],
            out_specs=pl.BlockSpec((1, T, s1), lambda j, i, wh: (j, i, 0))),
        out_shape=jax.ShapeDtypeStruct(r.shape, BF16), compiler_params=_params(("parallel", "parallel")),
    )(where, p, r)


def _chip_exchange(as_):
    nw = len(as_)

    def body(*refs):
        a_refs, r_refs, (send_sems, recv_sems) = refs[:nw], refs[nw:2 * nw], refs[2 * nw:]
        x, y, c = _mesh_pos()
        chips, blocks = _other_chips(x, y)
        cps = [_remote(a.at[b], r.at[k], send_sems, recv_sems, k * nw + n, (*chip, c))
               for k, (chip, b) in enumerate(zip(chips, blocks)) for n, (a, r) in enumerate(zip(a_refs, r_refs))]
        for cp in cps:
            cp.start()
        for cp in cps:
            cp.wait_recv()
        for cp in cps:
            cp.wait_send()

    return pl.pallas_call(
        body, name="grad_chip_exchange", out_shape=[jax.ShapeDtypeStruct((3,) + a.shape[1:], a.dtype) for a in as_],
        in_specs=[ANY_SPEC] * nw, out_specs=[ANY_SPEC] * nw,
        scratch_shapes=[pltpu.SemaphoreType.DMA((3 * nw,)), pltpu.SemaphoreType.DMA((3 * nw,))],
    )(*as_)


def _sum_chips(where, a, r, o, name):
    _, h, s1 = a.shape
    T = _sum_rows(h, s1)
    nt = h // T

    def body(where_ref, a_ref, r_ref, o_in, o_ref):
        acc = a_ref[0].astype(F32)
        for k in range(3):
            acc = acc + r_ref[k].astype(F32)
        o_ref[0] = acc

    return pl.pallas_call(
        body, name=name,
        grid_spec=pltpu.PrefetchScalarGridSpec(
            num_scalar_prefetch=1, grid=(nt,),
            in_specs=[pl.BlockSpec((1, T, s1), lambda i, wh: (wh[1], i, 0)),
                      pl.BlockSpec((3, T, s1), lambda i, wh: (0, i, 0)), ANY_SPEC],
            out_specs=pl.BlockSpec((1, T, s1), lambda i, wh: (0, wh[0] * nt + i, 0))),
        out_shape=jax.ShapeDtypeStruct(o.shape, F32), input_output_aliases={3: 0},
        compiler_params=_params(("parallel",)),
    )(where, a, r, o)


def _sibling_share(os_):
    nw = len(os_)

    def body(*refs):
        o_refs, (send_sems, recv_sems) = refs[nw:2 * nw], refs[2 * nw:]
        x, y, c = _mesh_pos()
        mine = lambda o, cc: _half(o.at[0], cc)
        cps = [_remote(mine(o, c), mine(o, c), send_sems, recv_sems, n, (x, y, 1 - c)) for n, o in enumerate(o_refs)]
        for cp in cps:
            cp.start()
        for n, o in enumerate(o_refs):
            _remote(mine(o, c), mine(o, 1 - c), send_sems, recv_sems, n, (x, y, 1 - c)).wait_recv()
        for cp in cps:
            cp.wait_send()

    return pl.pallas_call(
        body, name="grad_sibling_share", out_shape=[jax.ShapeDtypeStruct(o.shape, o.dtype) for o in os_],
        in_specs=[ANY_SPEC] * nw, out_specs=[ANY_SPEC] * nw,
        input_output_aliases={n: n for n in range(nw)},
        scratch_shapes=[pltpu.SemaphoreType.DMA((nw,)), pltpu.SemaphoreType.DMA((nw,))],
    )(*os_)


N_DEV = 8


def _scatter_copies(p_refs, r_refs, send_sem, recv_sem):
    x, y, c = _mesh_pos()
    cps = []
    for m in range(1, N_DEV):
        px, py, pc = x ^ (m >> 2 & 1), y ^ (m >> 1 & 1), c ^ (m & 1)
        for p, r in zip(p_refs, r_refs):
            cps.append(pltpu.make_async_remote_copy(src_ref=p.at[2 * px + py], dst_ref=r.at[m - 1], send_sem=send_sem,
                                                    recv_sem=recv_sem, device_id=(px, py, pc), device_id_type=MESH))
    return cps


def _reduce_start(ps, after, name):
    nw = len(ps)

    def body(*refs):
        p_refs, r_refs = refs[:nw], refs[nw:2 * nw]
        send_sem, recv_sem = refs[2 * nw + 1:2 * nw + 3]
        for cp in _scatter_copies(p_refs, r_refs, send_sem, recv_sem):
            cp.start()
        refs[-1][...] = jnp.zeros_like(refs[-1])

    lands = [pltpu.with_memory_space_constraint(lax.empty((N_DEV - 1,) + p.shape[1:], BF16), pltpu.HBM) for p in ps]
    ps = [pltpu.with_memory_space_constraint(p, pltpu.HBM) for p in ps]
    hbm = pl.BlockSpec(memory_space=pltpu.HBM)
    out = pl.pallas_call(
        body, name=name,
        out_shape=[pltpu.SemaphoreType.DMA(()), pltpu.SemaphoreType.DMA(())]
        + [pltpu.HBM(p.shape, BF16) for p in ps] + [pltpu.HBM(r.shape, BF16) for r in lands]
        + [jax.ShapeDtypeStruct((8, 128), F32)],
        in_specs=[hbm] * (2 * nw) + [ANY_SPEC],
        out_specs=[SEM_SPEC, SEM_SPEC] + [hbm] * (2 * nw) + [pl.BlockSpec(memory_space=pltpu.VMEM)],
        input_output_aliases={n: 2 + n for n in range(2 * nw)},
        compiler_params=pltpu.CompilerParams(has_side_effects=DATAFLOW),
    )(*ps, *lands, after)
    return out[0], out[1], out[2:2 + nw], out[2 + nw:2 + 2 * nw], out[-1]


def _reduce_wait(send_sem, recv_sem, ps, lands, after, name):
    nw = len(ps)

    def body(*refs):
        p_refs, r_refs = refs[:nw], refs[nw:2 * nw]
        send_sem, recv_sem = refs[2 * nw:2 * nw + 2]
        x, y, c = _mesh_pos()
        for cp in _scatter_copies(p_refs, r_refs, send_sem, recv_sem):
            cp.wait_send()
        for m in range(1, N_DEV):
            for p, r in zip(p_refs, r_refs):
                pltpu.make_async_remote_copy(src_ref=p.at[0], dst_ref=r.at[m - 1], send_sem=send_sem,
                                             recv_sem=recv_sem, device_id=(x, y, c), device_id_type=MESH).wait_recv()

    hbm = pl.BlockSpec(memory_space=pltpu.HBM)
    out = pl.pallas_call(
        body, name=name,
        out_shape=[pltpu.HBM(p.shape, BF16) for p in ps] + [pltpu.HBM(r.shape, BF16) for r in lands],
        in_specs=[hbm] * (2 * nw) + [SEM_SPEC, SEM_SPEC, ANY_SPEC],
        out_specs=[hbm] * (2 * nw),
        input_output_aliases={n: n for n in range(2 * nw)},
        compiler_params=pltpu.CompilerParams(has_side_effects=DATAFLOW),
    )(*ps, *lands, send_sem, recv_sem, after)
    return out[:nw], out[nw:]


SUM8_BLOCK_BYTES = 6 * 1024 * 1024


def _sum_devices(where, p, r, layer, o, name):
    _, s0, s1 = p.shape
    T = s0
    while (N_DEV - 1) * T * s1 * 2 > SUM8_BLOCK_BYTES:
        T //= 2

    def body(where_ref, p_ref, r_ref, *rest):
        me = 2 * where_ref[1] + where_ref[0]
        acc = None
        for dev in range(N_DEV):
            m = dev ^ me
            val = jnp.where(m == 0, p_ref[0], r_ref[jnp.maximum(m, 1) - 1]).astype(F32)
            acc = val if acc is None else acc + val
        rest[-1][0] = acc

    given = o is not None
    return pl.pallas_call(
        body, name=name,
        grid_spec=pltpu.PrefetchScalarGridSpec(
            num_scalar_prefetch=1, grid=(s0 // T,),
            in_specs=[pl.BlockSpec((1, T, s1), lambda i, wh: (wh[1], i, 0)),
                      pl.BlockSpec((N_DEV - 1, T, s1), lambda i, wh: (0, i, 0))] + [ANY_SPEC] * given,
            out_specs=pl.BlockSpec((1, T, s1), lambda i, wh: (layer, i, 0))),
        out_shape=jax.ShapeDtypeStruct((DEPTH, s0, s1), F32), input_output_aliases={3: 0} if given else {},
        compiler_params=_params(("parallel",)),
    )(where, p, r, *([o] if given else []))


def _all_reduce_small(sp):
    def body(sp_ref, out_ref, buf, send_sems, recv_sems):
        x, y, c = _mesh_pos()
        me = 4 * x + 2 * y + c
        buf[0] = sp_ref[...]
        cps = []
        for k in range(1, 8):
            peer = (x ^ (k >> 2 & 1), y ^ (k >> 1 & 1), c ^ (k & 1))
            cps.append(pltpu.make_async_remote_copy(src_ref=sp_ref, dst_ref=buf.at[k], send_sem=send_sems.at[k - 1],
                                                    recv_sem=recv_sems.at[k - 1], device_id=peer, device_id_type=MESH))
        for cp in cps:
            cp.start()
        for cp in cps:
            cp.wait_recv()
        for cp in cps:
            cp.wait_send()
        acc = buf[me]
        for p in range(1, 8):
            acc = acc + buf[p ^ me]
        out_ref[...] = acc

    vm = pl.BlockSpec(memory_space=pltpu.VMEM)
    return pl.pallas_call(
        body, name="all_reduce_small", out_shape=jax.ShapeDtypeStruct(sp.shape, F32),
        in_specs=[vm], out_specs=vm,
        scratch_shapes=[pltpu.VMEM((8,) + sp.shape, F32), pltpu.SemaphoreType.DMA((7,)), pltpu.SemaphoreType.DMA((7,))],
        compiler_params=_params(),
    )(sp)


def _adamw(w, g, m, v, name):
    R, C = w.shape
    T = next((t for t in (256, 128) if R % t == 0), R)

    def body(w_ref, g_ref, m_ref, v_ref, d_ref, m2_ref, v2_ref):
        gv = g_ref[...]
        m2 = ADAM_B1 * m_ref[...] + (1.0 - ADAM_B1) * gv
        v2 = ADAM_B2 * v_ref[...] + (1.0 - ADAM_B2) * (gv * gv)
        m_hat = m2 / (1.0 - ADAM_B1 ** ADAM_STEP)
        v_hat = v2 / (1.0 - ADAM_B2 ** ADAM_STEP)
        d_ref[...] = -ADAM_LR * (m_hat / (jnp.sqrt(v_hat) + ADAM_EPS) + ADAM_WD * w_ref[...])
        m2_ref[...] = m2
        v2_ref[...] = v2

    blk = pl.BlockSpec((T, C), _row)
    return pl.pallas_call(
        body, name=name, grid=(R // T,), in_specs=[blk] * 4, out_specs=[blk] * 3,
        out_shape=[jax.ShapeDtypeStruct((R, C), F32)] * 3, compiler_params=_params(("parallel",)),
    )(w, g, m, v)


def _pack_vectors(get, rel, cdw, name, loss=None):
    rows = []
    for l in range(DEPTH):
        for n, r in VEC_ROWS:
            v = get(n)[l]
            rows.append(jnp.pad(v, (0, r * D - v.shape[0])).reshape(r, D))
    rows.append(jnp.pad(rel.reshape(-1), (0, D - NUM_BUCKETS * 3 * HPG)).reshape(1, D))
    rows.append(cdw.reshape(CDW_GROWS, D))
    if loss is not None:
        rows.append(jnp.full((1, D), loss, F32))

    def body(*refs):
        out_ref = refs[-1]
        out_ref[...] = jnp.zeros_like(out_ref)
        at = 0
        for ref in refs[:-1]:
            out_ref[at:at + ref.shape[0], :] = ref[...]
            at += ref.shape[0]

    return pl.pallas_call(body, name=name, out_shape=jax.ShapeDtypeStruct((SMALL_ROWS, D), F32),
                          compiler_params=_params())(*rows)


def _unpack_vectors(packed, lens):
    out = {n: [] for n, _ in VEC_ROWS}
    for l in range(DEPTH):
        at = l * VEC_LROWS
        for n, r in VEC_ROWS:
            out[n].append(packed[at:at + r].reshape(-1)[:lens[n]])
            at += r
    rel = packed[REL_ROW, :NUM_BUCKETS * 3 * HPG].reshape(NUM_BUCKETS, 3 * HPG)
    return {n: jnp.stack(v) for n, v in out.items()}, rel


INPUT_NAMES = ("x", "mem") + ("rel_bias", "norm_mix_pre", "w_in", "b_gate", "conv_dw", "conv_dw_bias", "conv_ln_g",
                              "conv_ln_b", "w_conv_out", "w_att_out", "norm_mem", "w_mem_kv", "w_mem_out", "w_out",
                              "norm_mix_post", "norm_ffn_pre", "w_ffn_in", "w_ffn_out", "norm_ffn_post")
WEIGHT_NAMES = INPUT_NAMES[2:]


def kernel(*args):
    nw = len(WEIGHT_NAMES)
    a = dict(zip(INPUT_NAMES, args[:2 + nw]))
    target = args[2 + nw]
    mom = dict(zip(WEIGHT_NAMES, args[3 + nw:3 + 2 * nw]))
    var = dict(zip(WEIGHT_NAMES, args[3 + 2 * nw:3 + 3 * nw]))
    xi, yi, ci = _mesh_pos()
    chip = 2 * xi + yi
    where = jnp.stack([ci, chip]).astype(I32)

    head, tail = ("w_in",), tuple(n for n in BIG if n != "w_in")
    shard = lambda l, names: [a[n][l].astype(BF16) for n in names]
    cdw = jnp.pad(a["conv_dw"].reshape(DEPTH * KSIZE, CW // 4), ((0, CDW_ROWS - DEPTH * KSIZE), (0, 0)))
    own_head0 = shard(0, head)
    *got_head0, gcdw = _all_gather(own_head0, cdw)
    gcdw = lax.dynamic_update_slice(gcdw, cdw[None], (chip, 0, 0))
    conv_dw = gcdw[:, :DEPTH * KSIZE].reshape(N_CHIPS, DEPTH, KSIZE, CW // 4).transpose(1, 2, 0, 3)
    conv_dw = jnp.pad(conv_dw.reshape(DEPTH, KSIZE, CW), ((0, 0), (0, 1), (0, 0)))
    gmix = [a["norm_mix_pre"][l][None, :] for l in range(DEPTH)]

    def whole(names, gathered, own):
        w = {}
        for n, g, s in zip(names, gathered, own):
            (s0, s1), axis = SHARD[n]
            blk = lax.dynamic_update_slice(g, s[None], (chip, 0, 0))
            w[n] = blk.reshape(N_CHIPS * s0, s1) if axis == 0 else blk.transpose(1, 0, 2).reshape(s0, N_CHIPS * s1)
        if "w_in" in w:
            w["w_in"] = _z_cols_from_ref(w["w_in"])
        return w

    def vectors(l):
        w = {n: a[n][l][None, :] for n, _ in VEC_ROWS}
        w["conv_dw"] = conv_dw[l]
        return w

    def landed(flight, after, name):
        send_sem, recv_sem, thru, lands, _ = flight
        return _all_gather_wait(send_sem, recv_sem, thru, lands, after, name)

    flights = {"tail0": _all_gather_start(shard(0, tail), got_head0[0], "all_gather_start_tail0")}
    gmix[0] = gmix[0] + flights["tail0"][4][0, 0]

    def layer0(x):
        def rest(after):
            own, got = landed(flights["tail0"], after, "all_gather_wait_tail0")
            flights["head1"] = _all_gather_start(shard(1, head), got[0], "all_gather_start_head1")
            w = whole(tail, got, own)
            w["b_gate"] = a["b_gate"][0][None, :] + flights["head1"][4][0, 0]
            return w

        return {**vectors(0), **whole(head, got_head0, own_head0)}, rest

    def layer1(x):
        own, got = landed(flights["head1"], x, "all_gather_wait_head1")
        flights["tail1"] = _all_gather_start(shard(1, tail), got[0], "all_gather_start_tail1")
        first = {**vectors(1), **whole(head, got, own)}
        first["conv_dw_bias"] = first["conv_dw_bias"] + flights["tail1"][4][0, 0]

        def rest(after):
            own_t, got_t = landed(flights["tail1"], after, "all_gather_wait_tail1")
            return whole(tail, got_t, own_t)

        return first, rest

    def by_chip(layer_grads, names):
        out = []
        for n in names:
            (s0, s1), axis = SHARD[n]
            g = _ref_cols_from_z(layer_grads[n]) if n == "w_in" else layer_grads[n]
            g = g.reshape(N_CHIPS, s0, s1) if axis == 0 else g.reshape(s0, N_CHIPS, s1).transpose(1, 0, 2)
            out.append(g.astype(BF16))
        return out

    early = ("w_ffn_in", "w_ffn_out")
    late = tuple(n for n in BIG if n not in early)
    scattering = {}

    def on_grads(l, layer_grads):
        scattering["l1"] = _reduce_start(by_chip(layer_grads, BIG), layer_grads["w_in"], "grad_reduce_start_l1")
        return scattering["l1"][4]

    def on_last_ffn_grads(layer_grads):
        scattering["ffn"] = _reduce_start(by_chip(layer_grads, early), layer_grads["w_ffn_in"], "grad_reduce_start_ffn")
        return scattering["ffn"][4]

    loss_part, gx, grads, drel = _local_step(a["x"][0], a["mem"][0], target[0], a["rel_bias"],
                                             [layer0, layer1], gmix, on_grads, on_last_ffn_grads)

    reduced = {}
    send_sem, recv_sem, thru, lands, _ = scattering["l1"]
    sent, landed = _reduce_wait(send_sem, recv_sem, thru, lands, gx, "grad_reduce_wait_l1")
    for n, p, r in zip(BIG, sent, landed):
        reduced[n] = _sum_devices(where, p, r, 1, None, "grad_sum_devices_l1_" + n)
    send_sem, recv_sem, thru, lands, _ = scattering["ffn"]
    sent, landed = _reduce_wait(send_sem, recv_sem, thru, lands, gx, "grad_reduce_wait_ffn")
    for n, p, r in zip(early, sent, landed):
        reduced[n] = _sum_devices(where, p, r, 0, reduced[n], "grad_sum_devices_l0_" + n)
    packed = by_chip(grads[0], late)
    from_sibling = _sibling_exchange(packed)
    chip_sums = [_add_own_half(where, p, r, "grad_add_sibling_" + n) for n, p, r in zip(late, packed, from_sibling)]
    from_chips = _chip_exchange(chip_sums)
    shared = _sibling_share([_sum_chips(where, s, r, reduced[n], "grad_sum_chips_" + n)
                             for n, s, r in zip(late, chip_sums, from_chips)])
    reduced.update(zip(late, shared))
    reduced = [reduced[n] for n in BIG]

    gvec = _all_reduce_small(_pack_vectors(
        lambda n: jnp.stack([grads[l][n][0] for l in range(DEPTH)]), drel[:, :3 * HPG],
        jnp.stack([grads[l]["conv_dw"] for l in range(DEPTH)]), "pack_vector_grads", loss_part))
    loss = gvec[LOSS_ROW, 0]
    lens = {n: a[n].shape[1] for n, _ in VEC_ROWS}
    g_vec, g_rel = _unpack_vectors(gvec, lens)
    g_cdw = lax.dynamic_slice_in_dim(gvec[CDW_ROW:CDW_ROW + CDW_GROWS].reshape(DEPTH, KSIZE, CW), chip * (CW // 4),
                                     CW // 4, axis=2)

    grad, delta, new_m, new_v = {}, {}, {}, {}
    for n, g in zip(BIG, reduced):
        shape = a[n].shape
        flat2 = lambda t: t.reshape(shape[0] * shape[1], shape[2])
        d, m2, v2 = _adamw(flat2(a[n]), flat2(g), flat2(mom[n]), flat2(var[n]), "adamw_" + n)
        grad[n], delta[n], new_m[n], new_v[n] = g, d.reshape(shape), m2.reshape(shape), v2.reshape(shape)
    shape = a["conv_dw"].shape
    flat2 = lambda t: t.reshape(shape[0] * shape[1], shape[2])
    d, m2, v2 = _adamw(flat2(a["conv_dw"]), flat2(g_cdw), flat2(mom["conv_dw"]), flat2(var["conv_dw"]), "adamw_conv_dw")
    grad["conv_dw"], delta["conv_dw"], new_m["conv_dw"], new_v["conv_dw"] = (
        g_cdw, d.reshape(shape), m2.reshape(shape), v2.reshape(shape))
    zero_cdw = jnp.zeros((DEPTH, KSIZE, CW), F32)
    pk = lambda src, name: _pack_vectors(lambda n: src[n], src["rel_bias"], zero_cdw, name)
    d, m2, v2 = _adamw(pk(a, "pack_vector_w"), gvec, pk(mom, "pack_vector_m"), pk(var, "pack_vector_v"),
                       "adamw_vectors")
    for src, dst in ((d, delta), (m2, new_m), (v2, new_v)):
        vec, rel = _unpack_vectors(src, lens)
        dst.update(vec)
        dst["rel_bias"] = rel
    grad.update(g_vec)
    grad["rel_bias"] = g_rel

    outs = [loss, gx[None]]
    for group in (grad, delta, new_m, new_v):
        outs += [group[n] for n in WEIGHT_NAMES]
    return tuple(outs)
```
